```python
import math
import jax, jax.numpy as jnp
from jax import lax
import numpy as np

D_MODEL = 1024
BATCH = 2
SEQ = 8192
DEPTH = 2

GRID_W = 64
CTX_LEN = 256
EPS = 1e-6

N_Q_HEADS = 8
N_KV_HEADS = 2
HEAD_DIM = 64
ATTN_WIDTH = N_Q_HEADS * HEAD_DIM
KV_WIDTH = N_KV_HEADS * HEAD_DIM
WINDOW = 128
BLOCK = 128
ROPE_BASE = 10000.0
ROPE_FREQS = HEAD_DIM // 4

N_SG_GROUPS = 8
SG_GROUP_DIM = 64
SG_WIDTH = N_SG_GROUPS * SG_GROUP_DIM
CHUNK = 128

IN_AB = ATTN_WIDTH + 2 * KV_WIDTH + 2 * SG_WIDTH
SPLITS_AB = [ATTN_WIDTH, ATTN_WIDTH + KV_WIDTH, ATTN_WIDTH + 2 * KV_WIDTH,
             ATTN_WIDTH + 2 * KV_WIDTH + SG_WIDTH]
MIX_AB = ATTN_WIDTH + SG_WIDTH

HYENA_WIDTH = 1024
HYENA_ORDER = 2
SHORT_CONV = 3
FILT_BANDS = 16
FILT_EMB = 1 + 2 * FILT_BANDS
FILT_HIDDEN = 64
FILT_OUT = HYENA_ORDER * 2 * HYENA_WIDTH
DECAY_SHIFT = 0.05

N_EXPERTS = 64
TOP_K = 8
N_GROUPS = 8
TOPK_GROUPS = 4
D_EXPERT = 256
D_SHARED = 256
ROUTED_SCALE = 2.5
MOE_BLOCK = 128

N_EVEN = (DEPTH + 1) // 2
N_ODD = DEPTH // 2

kernel_name = "hybrid_swa_gmlp_hyena_moe_dit"


def rmsnorm(x, g):
    xf = x.astype(jnp.float32)
    y = xf * lax.rsqrt(jnp.mean(jnp.square(xf), axis=-1, keepdims=True) + EPS)
    return (y * g.astype(jnp.float32)).astype(x.dtype)


def axial_rope_tables(rows):
    row = jnp.repeat(jnp.arange(rows, dtype=jnp.float32), GRID_W)
    col = jnp.tile(jnp.arange(GRID_W, dtype=jnp.float32), rows)
    inv = jnp.power(ROPE_BASE, -jnp.arange(ROPE_FREQS, dtype=jnp.float32) / ROPE_FREQS)
    ang = jnp.stack([row[:, None] * inv, col[:, None] * inv], axis=1)
    return jnp.cos(ang), jnp.sin(ang)


def axial_rope(x, cos, sin):
    B, N, H, Dh = x.shape
    xs = x.astype(jnp.float32).reshape(B, N, H, 2, 2, ROPE_FREQS)
    x1, x2 = xs[..., 0, :], xs[..., 1, :]
    cb, sb = cos[None, :, None], sin[None, :, None]
    out = jnp.stack([x1 * cb - x2 * sb, x2 * cb + x1 * sb], axis=-2)
    return out.reshape(B, N, H, Dh).astype(x.dtype)


def window_attention(q, k, v, k_ctx, v_ctx, sink):
    B, L = q.shape[:2]
    C = k_ctx.shape[1]
    nb = L // BLOCK
    G = N_Q_HEADS // N_KV_HEADS
    scale = HEAD_DIM ** -0.5
    qb = q.reshape(B, nb, BLOCK, N_KV_HEADS, G, HEAD_DIM)

    def band(t):
        tp = jnp.pad(t, ((0, 0), (BLOCK, BLOCK), (0, 0), (0, 0)))
        tp = tp.reshape(B, nb + 2, BLOCK, N_KV_HEADS, HEAD_DIM)
        return jnp.concatenate([tp[:, :-2], tp[:, 1:-1], tp[:, 2:]], axis=2)

    kb, vb = band(k), band(v)
    s_loc = jnp.einsum('bnqhgd,bnkhd->bnhgqk', qb, kb).astype(jnp.float32) * scale
    s_ctx = jnp.einsum('bnqhgd,bchd->bnhgqc', qb, k_ctx).astype(jnp.float32) * scale
    blk = jnp.arange(nb)[:, None, None]
    qpos = blk * BLOCK + jnp.arange(BLOCK)[None, :, None]
    kpos = (blk - 1) * BLOCK + jnp.arange(3 * BLOCK)[None, None, :]
    mask = (jnp.abs(qpos - kpos) <= WINDOW) & (kpos >= 0) & (kpos < L)
    s_loc = jnp.where(mask[None, :, None, None], s_loc, -jnp.inf)
    s_sink = jnp.broadcast_to(sink.astype(jnp.float32).reshape(1, 1, N_KV_HEADS, G, 1, 1),
                              s_ctx.shape[:-1] + (1,))
    p = jax.nn.softmax(jnp.concatenate([s_loc, s_ctx, s_sink], axis=-1), axis=-1)
    p_loc = p[..., :3 * BLOCK].astype(v.dtype)
    p_ctx = p[..., 3 * BLOCK:3 * BLOCK + C].astype(v.dtype)
    o = (jnp.einsum('bnhgqk,bnkhd->bnqhgd', p_loc, vb)
         + jnp.einsum('bnhgqc,bchd->bnqhgd', p_ctx, v_ctx))
    return o.reshape(B, L, ATTN_WIDTH)


def context_attention(q, k, v, sink):
    B, C = q.shape[:2]
    G = N_Q_HEADS // N_KV_HEADS
    qg = q.reshape(B, C, N_KV_HEADS, G, HEAD_DIM)
    s = jnp.einsum('bqhgd,bkhd->bhgqk', qg, k).astype(jnp.float32) * HEAD_DIM ** -0.5
    s_sink = jnp.broadcast_to(sink.astype(jnp.float32).reshape(1, N_KV_HEADS, G, 1, 1),
                              s.shape[:-1] + (1,))
    p = jax.nn.softmax(jnp.concatenate([s, s_sink], axis=-1), axis=-1)[..., :C].astype(v.dtype)
    return jnp.einsum('bhgqk,bkhd->bqhgd', p, v).reshape(B, C, ATTN_WIDTH)


def spatial_gating(u, v, w_s, b_s):
    B, N, _ = u.shape
    nc = N // CHUNK
    ug = jax.nn.gelu(u).reshape(B, nc, CHUNK, N_SG_GROUPS, SG_GROUP_DIM)
    vf = jax.nn.gelu(v).astype(jnp.float32).reshape(B, nc, CHUNK, N_SG_GROUPS, SG_GROUP_DIM)
    vc = vf - jnp.mean(vf, axis=-1, keepdims=True)
    vn = (vc * lax.rsqrt(jnp.mean(jnp.square(vc), axis=-1, keepdims=True) + EPS)).astype(u.dtype)
    s = jnp.einsum('gij,bcjgd->bcigd', w_s, vn) + b_s.T[None, None, :, :, None]
    return (ug * s).reshape(B, N, SG_WIDTH)


def mixer_ab(h, hc, cos, sin, w_in, sink, w_s, b_s, w_out, ctx_out):
    B, L, _ = h.shape
    C = hc.shape[1]
    q, k, v, u, g = jnp.split(h @ w_in, SPLITS_AB, axis=-1)
    q = axial_rope(q.reshape(B, L, N_Q_HEADS, HEAD_DIM), cos, sin)
    k = axial_rope(k.reshape(B, L, N_KV_HEADS, HEAD_DIM), cos, sin)
    v = v.reshape(B, L, N_KV_HEADS, HEAD_DIM)
    if ctx_out:
        qc, kc, vc, uc, gc = jnp.split(hc @ w_in, SPLITS_AB, axis=-1)
        qc = qc.reshape(B, C, N_Q_HEADS, HEAD_DIM)
    else:
        kc, vc = jnp.split(hc @ w_in[:, ATTN_WIDTH:ATTN_WIDTH + 2 * KV_WIDTH], 2, axis=-1)
    kc = kc.reshape(B, C, N_KV_HEADS, HEAD_DIM)
    vc = vc.reshape(B, C, N_KV_HEADS, HEAD_DIM)
    y = jnp.concatenate([window_attention(q, k, v, kc, vc, sink),
                         spatial_gating(u, g, w_s, b_s)], axis=-1) @ w_out
    yc = None
    if ctx_out:
        yc = jnp.concatenate([context_attention(qc, kc, vc, sink),
                              spatial_gating(uc, gc, w_s, b_s)], axis=-1) @ w_out
    return y, yc


def short_conv(z, w, b):
    N = z.shape[1]
    pad = SHORT_CONV // 2
    zp = jnp.pad(z, ((0, 0), (pad, pad), (0, 0)))
    return sum(zp[:, j:j + N] * w[j] for j in range(SHORT_CONV)) + b


def hyena_filters(N, w1, b1, w2, b2, w3, freq, delta):
    f32 = jnp.float32
    t = jnp.arange(N, dtype=f32)
    t01 = t / max(N - 1, 1)
    bands = jnp.linspace(1e-4, FILT_BANDS - 1, FILT_BANDS, dtype=f32)
    ang = (2.0 * math.pi / N) * t[:, None] * bands[None, :]
    feats = jnp.concatenate([t01[:, None], jnp.cos(ang), jnp.sin(ang)], axis=-1)
    fr = freq.astype(f32)
    a = jnp.sin(fr * (feats @ w1.astype(f32) + b1.astype(f32)))
    a = jnp.sin(fr * (a @ w2.astype(f32) + b2.astype(f32)))
    hf = a @ w3.astype(f32)
    window = jnp.exp(-t01[:, None] * jnp.abs(delta.astype(f32))[None, :]) + DECAY_SHIFT
    hf = (hf * window).reshape(N, HYENA_ORDER, 2, HYENA_WIDTH)
    h_fwd, h_bwd = hf[:, :, 0], hf[:, :, 1]
    l1 = jnp.sum(jnp.abs(h_fwd), axis=0) + jnp.sum(jnp.abs(h_bwd[1:]), axis=0)
    k_circ = jnp.concatenate([h_fwd, jnp.zeros((1, HYENA_ORDER, HYENA_WIDTH), f32),
                              h_bwd[:0:-1]], axis=0) / l1
    return jnp.fft.rfft(k_circ, axis=0)


def hyena(h, w_in, conv_w, conv_b, w1, b1, w2, b2, w3, freq, delta, f_bias, w_out):
    N = h.shape[1]
    z = short_conv(h @ w_in, conv_w, conv_b)
    v, x1, x2 = jnp.split(z, 3, axis=-1)
    K = hyena_filters(N, w1, b1, w2, b2, w3, freq, delta)
    y = v
    for o, gate in enumerate((x1, x2)):
        Y = jnp.fft.rfft(y.astype(jnp.float32), n=2 * N, axis=1)
        conv = jnp.fft.irfft(Y * K[None, :, o], n=2 * N, axis=1)[:, :N]
        y = gate * (conv.astype(y.dtype) + f_bias[o] * y)
    return y @ w_out


def moe(h, w_router, e_bias, w_gate, w_up, w_down, ws_gate, ws_up, ws_down):
    T, D = h.shape
    scores = jax.nn.sigmoid((h @ w_router).astype(jnp.float32))
    biased = scores + e_bias.astype(jnp.float32)
    grp = biased.reshape(T, N_GROUPS, N_EXPERTS // N_GROUPS)
    grp_score = jnp.sum(lax.top_k(grp, 2)[0], axis=-1)
    _, gidx = lax.top_k(grp_score, TOPK_GROUPS)
    gmask = jnp.sum(jax.nn.one_hot(gidx, N_GROUPS), axis=1) > 0
    emask = jnp.repeat(gmask, N_EXPERTS // N_GROUPS, axis=1)
    _, eidx = lax.top_k(jnp.where(emask, biased, -jnp.inf), TOP_K)
    wts = jnp.take_along_axis(scores, eidx, axis=1)
    wts = wts / jnp.sum(wts, axis=-1, keepdims=True) * ROUTED_SCALE

    n_pairs = T * TOP_K
    e_flat = eidx.reshape(-1)
    tok_flat = jnp.arange(n_pairs, dtype=jnp.int32) // TOP_K
    order = jnp.argsort(e_flat)
    e_s, tok_s, w_s = e_flat[order], tok_flat[order], wts.reshape(-1)[order]
    counts = jnp.bincount(e_flat, length=N_EXPERTS)
    starts = jnp.cumsum(counts) - counts
    pcounts = (counts + MOE_BLOCK - 1) // MOE_BLOCK * MOE_BLOCK
    pends = jnp.cumsum(pcounts)
    pstarts = pends - pcounts
    dest = pstarts[e_s] + jnp.arange(n_pairs) - starts[e_s]
    n_blocks = -(-n_pairs // MOE_BLOCK) + N_EXPERTS
    n_slots = n_blocks * MOE_BLOCK
    slot_tok = jnp.zeros((n_slots,), jnp.int32).at[dest].set(tok_s)
    slot_w = jnp.zeros((n_slots,), jnp.float32).at[dest].set(w_s)
    blk_e = jnp.minimum(jnp.searchsorted(pends, jnp.arange(n_blocks) * MOE_BLOCK, side='right'),
                        N_EXPERTS - 1)

    def expert_block(args):
        tok, wt, e = args
        xb = h[tok]
        a = jax.nn.silu(xb @ w_gate[e]) * (xb @ w_up[e])
        return (a @ w_down[e]) * wt[:, None].astype(h.dtype)

    out = lax.map(expert_block, (slot_tok.reshape(n_blocks, MOE_BLOCK),
                                 slot_w.reshape(n_blocks, MOE_BLOCK), blk_e))
    routed = jax.ops.segment_sum(out.reshape(n_slots, D), slot_tok, num_segments=T)
    shared = (jax.nn.silu(h @ ws_gate) * (h @ ws_up)) @ ws_down
    return shared + routed


def setup_inputs(seed: int = 0) -> dict:
    key = jax.random.key(seed)
    ks = iter(jax.random.split(key, 40))
    f32 = jnp.float32

    def nrm(shape, scale):
        return jax.random.normal(next(ks), shape, f32) * scale

    D = D_MODEL
    decay = jnp.tile(jnp.linspace(math.log(1e-2) / 1.5, math.log(1e-2) / 0.3, HYENA_WIDTH,
                                  dtype=f32), HYENA_ORDER * 2)
    return {
        "x": nrm((BATCH, SEQ, D), 1.0),
        "c": nrm((BATCH, D), 1.0),
        "ctx": nrm((BATCH, CTX_LEN, D), 1.0),
        "c_ctx": nrm((D,), 1.0),
        "w_mod": nrm((DEPTH, D, 6 * D), 0.5 * D ** -0.5),
        "b_mod": nrm((DEPTH, 6 * D), 0.02),
        "norm_g": 1.0 + nrm((DEPTH, 2, D), 0.05),
        "w_in_ab": nrm((N_EVEN, D, IN_AB), D ** -0.5),
        "sink": nrm((N_EVEN, N_Q_HEADS), 0.5),
        "w_spatial": nrm((N_EVEN, N_SG_GROUPS, CHUNK, CHUNK), CHUNK ** -0.5),
        "b_spatial": 1.0 + nrm((N_EVEN, N_SG_GROUPS, CHUNK), 0.05),
        "w_out_ab": nrm((N_EVEN, MIX_AB, D), MIX_AB ** -0.5),
        "w_in_c": nrm((N_ODD, D, 3 * HYENA_WIDTH), D ** -0.5),
        "conv_w": nrm((N_ODD, SHORT_CONV, 3 * HYENA_WIDTH), SHORT_CONV ** -0.5),
        "conv_b": nrm((N_ODD, 3 * HYENA_WIDTH), 0.02),
        "filt_w1": nrm((N_ODD, FILT_EMB, FILT_HIDDEN), 1.0),
        "filt_b1": nrm((N_ODD, FILT_HIDDEN), 0.1),
        "filt_w2": nrm((N_ODD, FILT_HIDDEN, FILT_HIDDEN), FILT_HIDDEN ** -0.5),
        "filt_b2": nrm((N_ODD, FILT_HIDDEN), 0.1),
        "filt_w3": nrm((N_ODD, FILT_HIDDEN, FILT_OUT), FILT_HIDDEN ** -0.5),
        "filt_freq": 1.0 + nrm((N_ODD, FILT_HIDDEN), 0.1),
        "filt_delta": decay[None, :] * (1.0 + nrm((N_ODD, FILT_OUT), 0.05)),
        "filt_bias": nrm((N_ODD, HYENA_ORDER, HYENA_WIDTH), 0.5),
        "w_out_c": nrm((N_ODD, HYENA_WIDTH, D), HYENA_WIDTH ** -0.5),
        "w_router": nrm((DEPTH, D, N_EXPERTS), D ** -0.5),
        "e_bias": nrm((DEPTH, N_EXPERTS), 0.01),
        "w_gate": nrm((DEPTH, N_EXPERTS, D, D_EXPERT), D ** -0.5),
        "w_up": nrm((DEPTH, N_EXPERTS, D, D_EXPERT), D ** -0.5),
        "w_down": nrm((DEPTH, N_EXPERTS, D_EXPERT, D), D_EXPERT ** -0.5),
        "ws_gate": nrm((DEPTH, D, D_SHARED), D ** -0.5),
        "ws_up": nrm((DEPTH, D, D_SHARED), D ** -0.5),
        "ws_down": nrm((DEPTH, D_SHARED, D), D_SHARED ** -0.5),
        "final_g": 1.0 + nrm((D,), 0.05),
    }


def reference(x, c, ctx, c_ctx, w_mod, b_mod, norm_g, w_in_ab, sink, w_spatial, b_spatial,
              w_out_ab, w_in_c, conv_w, conv_b, filt_w1, filt_b1, filt_w2, filt_b2, filt_w3,
              filt_freq, filt_delta, filt_bias, w_out_c, w_router, e_bias, w_gate, w_up, w_down,
              ws_gate, ws_up, ws_down, final_g):
    B, L, D = x.shape
    C = ctx.shape[1]
    rows = L // GRID_W
    cos, sin = axial_rope_tables(rows)
    x_ctx = ctx
    for i in range(DEPTH):
        j = i // 2
        ctx_read = (i % 2 == 0)
        ctx_next = any(l % 2 == 0 for l in range(i + 1, DEPTH))
        m = jax.nn.silu(c) @ w_mod[i] + b_mod[i]
        sh1, sc1, g1, sh2, sc2, g2 = [t[:, None, :] for t in jnp.split(m, 6, axis=-1)]
        h = rmsnorm(x, norm_g[i, 0]) * (1.0 + sc1) + sh1
        hc = None
        if ctx_read or ctx_next:
            mc = jax.nn.silu(c_ctx) @ w_mod[i] + b_mod[i]
            csh1, csc1, cg1, csh2, csc2, cg2 = jnp.split(mc, 6)
            hc = rmsnorm(x_ctx, norm_g[i, 0]) * (1.0 + csc1) + csh1
        if i % 2 == 0:
            y, yc = mixer_ab(h, hc, cos, sin, w_in_ab[j], sink[j], w_spatial[j], b_spatial[j],
                             w_out_ab[j], ctx_next)
        else:
            hy = (w_in_c[j], conv_w[j], conv_b[j], filt_w1[j], filt_b1[j], filt_w2[j], filt_b2[j],
                  filt_w3[j], filt_freq[j], filt_delta[j], filt_bias[j], w_out_c[j])
            y = hyena(h, *hy)
            yc = hyena(hc, *hy) if ctx_next else None
        x = x + g1 * y
        h2 = rmsnorm(x, norm_g[i, 1]) * (1.0 + sc2) + sh2
        moe_p = (w_router[i], e_bias[i], w_gate[i], w_up[i], w_down[i],
                 ws_gate[i], ws_up[i], ws_down[i])
        if ctx_next:
            x_ctx = x_ctx + cg1 * yc
            hc2 = rmsnorm(x_ctx, norm_g[i, 1]) * (1.0 + csc2) + csh2
            out = moe(jnp.concatenate([h2.reshape(B * L, D), hc2.reshape(B * C, D)], axis=0), *moe_p)
            x = x + g2 * out[:B * L].reshape(B, L, D)
            x_ctx = x_ctx + cg2 * out[B * L:].reshape(B, C, D)
        else:
            x = x + g2 * moe(h2.reshape(B * L, D), *moe_p).reshape(B, L, D)
    return rmsnorm(x, final_g)
```

```python
import functools
import math

import numpy as np
import jax
import jax.numpy as jnp
from jax import lax
from jax.experimental import pallas as pl
from jax.experimental.pallas import tpu as pltpu

F32 = jnp.float32
BF16 = jnp.bfloat16
I32 = jnp.int32
HIGHEST = lax.Precision.HIGHEST
SDS = jax.ShapeDtypeStruct
BS = pl.BlockSpec

EPS = 1e-6
NEG = -1e30

GRID_W = 64
N_Q_HEADS = 8
N_KV_HEADS = 2
HEAD_DIM = 64
ATTN_WIDTH = N_Q_HEADS * HEAD_DIM
KV_WIDTH = N_KV_HEADS * HEAD_DIM
WINDOW = 128
BLOCK = 128
ROPE_BASE = 10000.0
ROPE_FREQS = HEAD_DIM // 4
N_SG_GROUPS = 8
SG_GROUP_DIM = 64
SG_WIDTH = N_SG_GROUPS * SG_GROUP_DIM
HYENA_ORDER = 2
FILT_BANDS = 16
DECAY_SHIFT = 0.05
N_EXPERTS = 64
TOP_K = 8
N_GROUPS = 8
TOPK_GROUPS = 4
ROUTED_SCALE = 2.5

LANES = 128
SUBLANES = 8
VMEM_LIMIT = 56 * 1024 * 1024

TM_PROJ = 512
TQ_MIX = 256
TM_ROUTE = 512
TT_DISP = 256
TT_COMB = 128
BM_FFN = 256


def _cparams(*sem):
    return pltpu.CompilerParams(dimension_semantics=sem, vmem_limit_bytes=VMEM_LIMIT)


def _dot(a, b):
    return jnp.dot(a, b, preferred_element_type=F32)


def _dot_nt(a, b):
    return lax.dot_general(a, b, (((1,), (1,)), ((), ())), preferred_element_type=F32)


def _dot_hp(a, b):
    return jnp.dot(a, b, preferred_element_type=F32, precision=HIGHEST)


def _norm_mod(x, g, sc, sh):
    ms = jnp.mean(x * x, axis=-1, keepdims=True)
    y = x * lax.rsqrt(ms + EPS)
    return (y * g) * (1.0 + sc) + sh


def _gelu_tanh(x):
    c = math.sqrt(2.0 / math.pi)
    return 0.5 * x * (1.0 + jnp.tanh(c * (x + 0.044715 * (x * x * x))))


def _silu(x):
    return x * jax.nn.sigmoid(x)


def _mod_body(c_ref, w_ref, b_ref, o_ref):
    o_ref[0] = _dot_hp(_silu(c_ref[...]), w_ref[0]) + b_ref[0]


def _mod_vectors(cc, w_mod, b_mod):
    depth, d, n = w_mod.shape
    tn = 1536
    return pl.pallas_call(
        _mod_body,
        out_shape=SDS((depth, SUBLANES, n), F32),
        grid=(depth, n // tn),
        in_specs=[BS((SUBLANES, d), lambda l, j: (0, 0)),
                  BS((1, d, tn), lambda l, j: (l, 0, j)),
                  BS((1, 1, tn), lambda l, j: (l, 0, j))],
        out_specs=BS((1, SUBLANES, tn), lambda l, j: (l, 0, j)),
        compiler_params=_cparams("arbitrary", "arbitrary"),
        name="mod_vectors",
    )(cc, w_mod, b_mod.reshape(depth, 1, n))


def _ctx_kv_body(ctx_ref, mod_ref, g_ref, w_ref, kc_ref, vc_ref):
    h = _norm_mod(ctx_ref[0], g_ref[...], mod_ref[1:2, :], mod_ref[0:1, :])
    z = _dot(h.astype(BF16), w_ref[...])
    kc_ref[0] = z[:, :KV_WIDTH].astype(BF16)
    vc_ref[0] = z[:, KV_WIDTH:].astype(BF16)


def _ctx_kv(ctx, mod_ctx, g, w_kv):
    b, c, d = ctx.shape
    return pl.pallas_call(
        _ctx_kv_body,
        out_shape=(SDS((b, c, KV_WIDTH), BF16), SDS((b, c, KV_WIDTH), BF16)),
        grid=(b,),
        in_specs=[BS((1, c, d), lambda i: (i, 0, 0)),
                  BS((SUBLANES, d), lambda i: (0, 0)),
                  BS((1, d), lambda i: (0, 0)),
                  BS((d, 2 * KV_WIDTH), lambda i: (0, 0))],
        out_specs=(BS((1, c, KV_WIDTH), lambda i: (i, 0, 0)),
                   BS((1, c, KV_WIDTH), lambda i: (i, 0, 0))),
        compiler_params=_cparams("arbitrary"),
        name="ctx_kv",
    )(ctx, mod_ctx, g, w_kv)


def _inproj_ab_body(x_ref, mod_ref, g_ref, w_ref, cos_ref, sin_ref, avg_ref,
                    q_ref, k_ref, v_ref, ug_ref, vn_ref):
    h = _norm_mod(x_ref[0], g_ref[...], mod_ref[0, 1:2, :], mod_ref[0, 0:1, :]).astype(BF16)
    cs = cos_ref[...]
    sn = sin_ref[...]
    rot0 = ATTN_WIDTH + 2 * KV_WIDTH + 2 * SG_WIDTH
    scale = HEAD_DIM ** -0.5
    for j in range(ATTN_WIDTH // LANES):
        z = _dot(h, w_ref[:, j * LANES:(j + 1) * LANES])
        zr = _dot(h, w_ref[:, rot0 + j * LANES:rot0 + (j + 1) * LANES])
        q_ref[0, :, j * LANES:(j + 1) * LANES] = ((z * cs + zr * sn) * scale).astype(BF16)
    zk = _dot(h, w_ref[:, ATTN_WIDTH:ATTN_WIDTH + KV_WIDTH])
    zkr = _dot(h, w_ref[:, rot0 + ATTN_WIDTH:rot0 + ATTN_WIDTH + KV_WIDTH])
    k_ref[0] = (zk * cs + zkr * sn).astype(BF16)
    v_ref[0] = _dot(h, w_ref[:, ATTN_WIDTH + KV_WIDTH:ATTN_WIDTH + 2 * KV_WIDTH]).astype(BF16)
    u0 = ATTN_WIDTH + 2 * KV_WIDTH
    ug_ref[0] = _gelu_tanh(_dot(h, w_ref[:, u0:u0 + SG_WIDTH]))
    vf = _gelu_tanh(_dot(h, w_ref[:, u0 + SG_WIDTH:u0 + 2 * SG_WIDTH]))
    avg = avg_ref[...]

    def gmean(t):
        hi = t.astype(BF16)
        lo = (t - hi.astype(F32)).astype(BF16)
        return _dot(hi, avg) + _dot(lo, avg)

    vc = vf - gmean(vf)
    vn_ref[0] = (vc * lax.rsqrt(gmean(vc * vc) + EPS)).astype(BF16)


def _inproj_ab(x, mod_lat, g, w_cat, cos_t, sin_t, avg):
    b, l, d = x.shape
    tm = TM_PROJ
    ncol = w_cat.shape[1]
    return pl.pallas_call(
        _inproj_ab_body,
        out_shape=(SDS((b, l, ATTN_WIDTH), BF16), SDS((b, l, KV_WIDTH), BF16),
                   SDS((b, l, KV_WIDTH), BF16), SDS((b, l, SG_WIDTH), F32),
                   SDS((b, l, SG_WIDTH), BF16)),
        grid=(b, l // tm),
        in_specs=[BS((1, tm, d), lambda bi, i: (bi, i, 0)),
                  BS((1, SUBLANES, d), lambda bi, i: (bi, 0, 0)),
                  BS((1, d), lambda bi, i: (0, 0)),
                  BS((d, ncol), lambda bi, i: (0, 0)),
                  BS((tm, LANES), lambda bi, i: (i, 0)),
                  BS((tm, LANES), lambda bi, i: (i, 0)),
                  BS((SG_WIDTH, SG_WIDTH), lambda bi, i: (0, 0))],
        out_specs=(BS((1, tm, ATTN_WIDTH), lambda bi, i: (bi, i, 0)),
                   BS((1, tm, KV_WIDTH), lambda bi, i: (bi, i, 0)),
                   BS((1, tm, KV_WIDTH), lambda bi, i: (bi, i, 0)),
                   BS((1, tm, SG_WIDTH), lambda bi, i: (bi, i, 0)),
                   BS((1, tm, SG_WIDTH), lambda bi, i: (bi, i, 0))),
        compiler_params=_cparams("arbitrary", "arbitrary"),
        name="inproj_ab",
    )(x, mod_lat, g, w_cat, cos_t, sin_t, avg)


def _mixer_body(sink_ref, q_ref, kp_ref, kcur_ref, kn_ref, vp_ref, vcur_ref, vn_ref,
                kc_ref, vc_ref, ug_ref, vnorm_ref, ws_ref, bs_ref, wout_ref, x_ref, mod_ref,
                o_ref, cat_ref, *, seq_len, sub_blocks):
    i = pl.program_id(1)
    kk = jnp.concatenate([kp_ref[0], kcur_ref[0], kn_ref[0]], axis=0)
    vv = jnp.concatenate([vp_ref[0], vcur_ref[0], vn_ref[0]], axis=0)
    kc = kc_ref[0]
    vc = vc_ref[0]
    span = 3 * BLOCK
    ii = lax.broadcasted_iota(I32, (BLOCK, span), 0)
    jj = lax.broadcasted_iota(I32, (BLOCK, span), 1)
    dd = jj - ii
    in_window = jnp.where(dd >= 0, jnp.where(dd <= 2 * WINDOW, 1, 0), 0)
    group = N_Q_HEADS // N_KV_HEADS
    for r in range(sub_blocks):
        rows = slice(r * BLOCK, (r + 1) * BLOCK)
        kpos = (i * sub_blocks + r - 1) * BLOCK + jj
        in_seq = jnp.where(kpos >= 0, jnp.where(kpos < seq_len, 1, 0), 0)
        bias = jnp.where(in_window * in_seq > 0, 0.0, NEG)
        qb = q_ref[0, rows, :]
        kl = kk[r * BLOCK:r * BLOCK + span]
        vl = vv[r * BLOCK:r * BLOCK + span]
        for hq in range(N_Q_HEADS):
            hk = hq // group
            ks = slice(hk * HEAD_DIM, (hk + 1) * HEAD_DIM)
            qh = qb[:, hq * HEAD_DIM:(hq + 1) * HEAD_DIM]
            s_loc = _dot_nt(qh, kl[:, ks]) + bias
            s_ctx = _dot_nt(qh, kc[:, ks])
            sk = sink_ref[hq]
            m = jnp.maximum(jnp.maximum(jnp.max(s_loc, axis=-1, keepdims=True),
                                        jnp.max(s_ctx, axis=-1, keepdims=True)), sk)
            p_loc = jnp.exp(s_loc - m)
            p_ctx = jnp.exp(s_ctx - m)
            den = (jnp.sum(p_loc, axis=-1, keepdims=True) + jnp.sum(p_ctx, axis=-1, keepdims=True)
                   + jnp.exp(sk - m))
            o = _dot(p_loc.astype(BF16), vl[:, ks]) + _dot(p_ctx.astype(BF16), vc[:, ks])
            cat_ref[rows, hq * HEAD_DIM:(hq + 1) * HEAD_DIM] = (o / den).astype(BF16)
        vnb = vnorm_ref[0, rows, :]
        ugb = ug_ref[0, rows, :]
        for g in range(N_SG_GROUPS):
            gs = slice(g * SG_GROUP_DIM, (g + 1) * SG_GROUP_DIM)
            sg = _dot(ws_ref[g], vnb[:, gs]) + bs_ref[:, gs]
            cat_ref[rows, ATTN_WIDTH + g * SG_GROUP_DIM:ATTN_WIDTH + (g + 1) * SG_GROUP_DIM] = (
                ugb[:, gs] * sg).astype(BF16)
    y = _dot(cat_ref[...], wout_ref[...])
    o_ref[0] = x_ref[0] + mod_ref[0, 2:3, :] * y


def _mixer(sink, q, k, v, kc, vc, ug, vn, w_s, b_full, w_out, x, mod_lat):
    b, l, d = x.shape
    tq = TQ_MIX
    r = tq // BLOCK
    nb = l // BLOCK
    c = kc.shape[1]
    prev_map = lambda bi, i: (bi, jnp.maximum(i * r - 1, 0), 0)
    next_map = lambda bi, i: (bi, jnp.minimum((i + 1) * r, nb - 1), 0)
    cur_map = lambda bi, i: (bi, i, 0)
    body = functools.partial(_mixer_body, seq_len=l, sub_blocks=r)
    return pl.pallas_call(
        body,
        out_shape=SDS((b, l, d), F32),
        grid=(b, l // tq),
        in_specs=[BS(memory_space=pltpu.SMEM),
                  BS((1, tq, ATTN_WIDTH), cur_map),
                  BS((1, BLOCK, KV_WIDTH), prev_map), BS((1, tq, KV_WIDTH), cur_map),
                  BS((1, BLOCK, KV_WIDTH), next_map),
                  BS((1, BLOCK, KV_WIDTH), prev_map), BS((1, tq, KV_WIDTH), cur_map),
                  BS((1, BLOCK, KV_WIDTH), next_map),
                  BS((1, c, KV_WIDTH), lambda bi, i: (bi, 0, 0)),
                  BS((1, c, KV_WIDTH), lambda bi, i: (bi, 0, 0)),
                  BS((1, tq, SG_WIDTH), cur_map), BS((1, tq, SG_WIDTH), cur_map),
                  BS((N_SG_GROUPS, BLOCK, BLOCK), lambda bi, i: (0, 0, 0)),
                  BS((BLOCK, SG_WIDTH), lambda bi, i: (0, 0)),
                  BS((d, d), lambda bi, i: (0, 0)),
                  BS((1, tq, d), cur_map),
                  BS((1, SUBLANES, d), lambda bi, i: (bi, 0, 0))],
        out_specs=BS((1, tq, d), cur_map),
        scratch_shapes=[pltpu.VMEM((tq, d), BF16)],
        compiler_params=_cparams("arbitrary", "arbitrary"),
        name="mixer_ab",
    )(sink, q, k, k, k, v, v, v, kc, vc, ug, vn, w_s, b_full, w_out, x, mod_lat)


def _router_body(x_ref, mod_ref, g_ref, wr_ref, eb_ref, tri_ref,
                 h2s_ref, h2b_ref, eidx_ref, wts_ref, rank_ref, cnt_ref, carry_ref, *, tm):
    i = pl.program_id(0)

    @pl.when(i == 0)
    def _():
        carry_ref[...] = jnp.zeros_like(carry_ref)

    h2 = _norm_mod(x_ref[...], g_ref[...], mod_ref[0, 4:5, :], mod_ref[0, 3:4, :])
    h2b_ref[...] = h2.astype(BF16)
    for j in range(SUBLANES):
        h2s_ref[:, j, :] = h2[:, j * LANES:(j + 1) * LANES]

    logits = lax.dot_general(wr_ref[...], h2, (((1,), (1,)), ((), ())),
                             preferred_element_type=F32, precision=HIGHEST)
    scores = jax.nn.sigmoid(logits)
    per_group = N_EXPERTS // N_GROUPS
    shape3 = (N_GROUPS, per_group, tm)
    s3 = scores.reshape(shape3)
    b3 = (scores + eb_ref[...]).reshape(shape3)
    sub = lax.broadcasted_iota(I32, shape3, 1)
    eid = lax.broadcasted_iota(I32, shape3, 0) * per_group + sub

    m1 = jnp.max(b3, axis=1, keepdims=True)
    i1 = jnp.min(jnp.where(b3 == m1, sub, per_group), axis=1, keepdims=True)
    m2 = jnp.max(jnp.where(sub == i1, -jnp.inf, b3), axis=1, keepdims=True)
    gs = m1 + m2
    keep = []
    for g in range(N_GROUPS):
        beaten = jnp.zeros((1, tm), I32)
        for g2 in range(N_GROUPS):
            if g2 == g:
                continue
            wins = (gs[g2] >= gs[g]) if g2 < g else (gs[g2] > gs[g])
            beaten = beaten + jnp.where(wins, 1, 0)
        keep.append(jnp.where(beaten < TOPK_GROUPS, 1, 0)[None])
    keep3 = jnp.concatenate(keep, axis=0)
    val = jnp.where(keep3 > 0, b3, -jnp.inf)

    def red(fn, a):
        return fn(fn(a, axis=0, keepdims=True), axis=1, keepdims=True)

    idxs, ws = [], []
    member = jnp.zeros(shape3, F32)
    for _ in range(TOP_K):
        m = red(jnp.max, val)
        idx = red(jnp.min, jnp.where(val == m, eid, N_EXPERTS))
        hit = eid == idx
        ws.append(red(jnp.sum, jnp.where(hit, s3, 0.0)))
        val = jnp.where(hit, -jnp.inf, val)
        member = member + jnp.where(hit, 1.0, 0.0)
        idxs.append(idx)
    wsum = ws[0]
    for w in ws[1:]:
        wsum = wsum + w

    member2 = member.reshape(N_EXPERTS, tm)
    before = _dot(member2.astype(BF16), tri_ref[...]) + carry_ref[:, 0:1]
    before3 = before.reshape(shape3)
    for k in range(TOP_K):
        eidx_ref[k:k + 1, :] = idxs[k].reshape(1, tm)
        wts_ref[k:k + 1, :] = (ws[k] / wsum * ROUTED_SCALE).reshape(1, tm)
        rk = red(jnp.sum, jnp.where(eid == idxs[k], before3, 0.0))
        rank_ref[k:k + 1, :] = rk.reshape(1, tm).astype(I32)
    total = carry_ref[...] + jnp.sum(member2, axis=1, keepdims=True)
    carry_ref[...] = total
    cnt_ref[...] = total


def _router(x_flat, mod_lat, g, wr_t, e_bias, tri, tokens_per_batch):
    t, d = x_flat.shape
    tm = TM_ROUTE
    tiles_per_batch = tokens_per_batch // tm
    body = functools.partial(_router_body, tm=tm)
    return pl.pallas_call(
        body,
        out_shape=(SDS((t, SUBLANES, LANES), F32), SDS((t, d), BF16),
                   SDS((TOP_K, t), I32), SDS((TOP_K, t), F32), SDS((TOP_K, t), I32),
                   SDS((N_EXPERTS, LANES), F32)),
        grid=(t // tm,),
        in_specs=[BS((tm, d), lambda i: (i, 0)),
                  BS((1, SUBLANES, d), lambda i: (i // tiles_per_batch, 0, 0)),
                  BS((1, d), lambda i: (0, 0)),
                  BS((N_EXPERTS, d), lambda i: (0, 0)),
                  BS((N_EXPERTS, 1), lambda i: (0, 0)),
                  BS((tm, tm), lambda i: (0, 0))],
        out_specs=(BS((tm, SUBLANES, LANES), lambda i: (i, 0, 0)),
                   BS((tm, d), lambda i: (i, 0)),
                   BS((TOP_K, tm), lambda i: (0, i)),
                   BS((TOP_K, tm), lambda i: (0, i)),
                   BS((TOP_K, tm), lambda i: (0, i)),
                   BS((N_EXPERTS, LANES), lambda i: (0, 0))),
        scratch_shapes=[pltpu.VMEM((N_EXPERTS, LANES), F32)],
        compiler_params=_cparams("arbitrary"),
        name="moe_router",
    )(x_flat, mod_lat, g, wr_t, e_bias, tri)


def _dispatch_body(pend_ref, pcnt_ref, nu_ref, dest_ref, h_ref, xs_ref, zbuf_ref, sem, zsem, *,
                   tt, bm, n_blocks):
    i = pl.program_id(0)

    def zero_copy(row0):
        return pltpu.make_async_copy(zbuf_ref, xs_ref.at[pl.ds(row0, bm)], zsem)

    @pl.when(i == 0)
    def _():
        zbuf_ref[...] = jnp.zeros_like(zbuf_ref)

        def start(e, c):
            @pl.when(pcnt_ref[e] > 0)
            def _():
                zero_copy(pend_ref[e] - bm).start()
            return c

        def wait(e, c):
            @pl.when(pcnt_ref[e] > 0)
            def _():
                zero_copy(pend_ref[e] - bm).wait()
            return c

        def start_tail(j, c):
            zero_copy(j * bm).start()
            return c

        def wait_tail(j, c):
            zero_copy(j * bm).wait()
            return c

        lax.fori_loop(0, N_EXPERTS, start, 0)
        lax.fori_loop(nu_ref[0], n_blocks, start_tail, 0)
        lax.fori_loop(0, N_EXPERTS, wait, 0)
        lax.fori_loop(nu_ref[0], n_blocks, wait_tail, 0)

    def issue(t, c):
        for k in range(TOP_K):
            d = dest_ref[0, 0, k * tt + t]
            pltpu.make_async_copy(h_ref.at[t], xs_ref.at[d], sem).start()
        return c

    lax.fori_loop(0, tt, issue, 0)
    for k in range(TOP_K):
        pltpu.make_async_copy(h_ref, xs_ref.at[pl.ds(0, tt)], sem).wait()


def _dispatch(pend, pcnt, n_used, dest_tiles, h2s, n_slots):
    t = h2s.shape[0]
    tt = TT_DISP
    body = functools.partial(_dispatch_body, tt=tt, bm=BM_FFN, n_blocks=n_slots // BM_FFN)
    grid_spec = pltpu.PrefetchScalarGridSpec(
        num_scalar_prefetch=3,
        grid=(t // tt,),
        in_specs=[BS((1, 1, TOP_K * tt), lambda i, *_: (i, 0, 0), memory_space=pltpu.SMEM),
                  BS((tt, SUBLANES, LANES), lambda i, *_: (i, 0, 0))],
        out_specs=BS(memory_space=pl.ANY),
        scratch_shapes=[pltpu.VMEM((BM_FFN, SUBLANES, LANES), F32),
                        pltpu.SemaphoreType.DMA(()), pltpu.SemaphoreType.DMA(())],
    )
    return pl.pallas_call(
        body,
        out_shape=SDS((n_slots, SUBLANES, LANES), F32),
        grid_spec=grid_spec,
        compiler_params=_cparams("arbitrary"),
        name="moe_dispatch",
    )(pend, pcnt, n_used, dest_tiles, h2s)


def _ffn_body(be_ref, nu_ref, xs_ref, wg_ref, wu_ref, wd_ref, ys_ref, wgb_ref, wub_ref, wdb_ref):
    i = pl.program_id(0)
    fresh = jnp.logical_or(i == 0, be_ref[i] != be_ref[jnp.maximum(i - 1, 0)])

    @pl.when(jnp.logical_and(fresh, i < nu_ref[0]))
    def _():
        wgb_ref[...] = wg_ref[0].astype(BF16)
        wub_ref[...] = wu_ref[0].astype(BF16)
        wdb_ref[...] = wd_ref[0].astype(BF16)

    @pl.when(i < nu_ref[0])
    def _():
        x = jnp.concatenate([xs_ref[:, j, :] for j in range(SUBLANES)], axis=1).astype(BF16)
        a = _silu(_dot(x, wgb_ref[...])) * _dot(x, wub_ref[...])
        y = _dot(a.astype(BF16), wdb_ref[...])
        for j in range(SUBLANES):
            ys_ref[:, j, :] = y[:, j * LANES:(j + 1) * LANES]

    @pl.when(i >= nu_ref[0])
    def _():
        ys_ref[...] = jnp.zeros_like(ys_ref)


def _ffn(blk_e, n_used, xs, w_gate, w_up, w_down):
    n_slots = xs.shape[0]
    bm = BM_FFN
    _, d, de = w_gate.shape
    grid_spec = pltpu.PrefetchScalarGridSpec(
        num_scalar_prefetch=2,
        grid=(n_slots // bm,),
        in_specs=[BS((bm, SUBLANES, LANES), lambda i, be, nu: (jnp.minimum(i, nu[0] - 1), 0, 0)),
                  BS((1, d, de), lambda i, be, nu: (be[i], 0, 0)),
                  BS((1, d, de), lambda i, be, nu: (be[i], 0, 0)),
                  BS((1, de, d), lambda i, be, nu: (be[i], 0, 0))],
        out_specs=BS((bm, SUBLANES, LANES), lambda i, be, nu: (i, 0, 0)),
        scratch_shapes=[pltpu.VMEM((d, de), BF16), pltpu.VMEM((d, de), BF16),
                        pltpu.VMEM((de, d), BF16)],
    )
    return pl.pallas_call(
        _ffn_body,
        out_shape=SDS((n_slots, SUBLANES, LANES), F32),
        grid_spec=grid_spec,
        compiler_params=_cparams("arbitrary"),
        name="moe_experts",
    )(blk_e, n_used, xs, w_gate, w_up, w_down)


def _combine_body(dest_ref, ys_ref, x_ref, h2b_ref, wt_ref, mod_ref, wsg_ref, wsu_ref, wsd_ref,
                  fg_ref, o_ref, gbuf_ref, sem, *, tt, final):
    def issue(t, c):
        for k in range(TOP_K):
            d = dest_ref[0, 0, k * tt + t]
            pltpu.make_async_copy(ys_ref.at[d], gbuf_ref.at[k * tt + t], sem).start()
        return c

    lax.fori_loop(0, tt, issue, 0)
    hb = h2b_ref[...]
    a = _silu(_dot(hb, wsg_ref[...])) * _dot(hb, wsu_ref[...])
    shared = _dot(a.astype(BF16), wsd_ref[...])
    for k in range(TOP_K):
        pltpu.make_async_copy(ys_ref.at[pl.ds(0, tt)], gbuf_ref.at[pl.ds(k * tt, tt)], sem).wait()
    wt = wt_ref[...]
    g2 = mod_ref[0, 5:6, :]
    for j in range(SUBLANES):
        cols = slice(j * LANES, (j + 1) * LANES)
        routed = wt[:, 0:1] * gbuf_ref[0:tt, j, :]
        for k in range(1, TOP_K):
            routed = routed + wt[:, k:k + 1] * gbuf_ref[k * tt:(k + 1) * tt, j, :]
        o_ref[:, cols] = x_ref[:, cols] + g2[:, cols] * (shared[:, cols] + routed)
    if final:
        xo = o_ref[...]
        ms = jnp.mean(xo * xo, axis=-1, keepdims=True)
        o_ref[...] = (xo * lax.rsqrt(ms + EPS)) * fg_ref[...]


def _combine(dest_tiles, ys, x_flat, h2b, wt_tk, mod_lat, wsg, wsu, wsd, final_g,
             tokens_per_batch, final):
    t, d = x_flat.shape
    tt = TT_COMB
    ds = wsg.shape[1]
    tiles_per_batch = tokens_per_batch // tt
    body = functools.partial(_combine_body, tt=tt, final=final)
    return pl.pallas_call(
        body,
        out_shape=SDS((t, d), F32),
        grid=(t // tt,),
        in_specs=[BS((1, 1, TOP_K * tt), lambda i: (i, 0, 0), memory_space=pltpu.SMEM),
                  BS(memory_space=pl.ANY),
                  BS((tt, d), lambda i: (i, 0)),
                  BS((tt, d), lambda i: (i, 0)),
                  BS((tt, TOP_K), lambda i: (i, 0)),
                  BS((1, SUBLANES, d), lambda i: (i // tiles_per_batch, 0, 0)),
                  BS((d, ds), lambda i: (0, 0)),
                  BS((d, ds), lambda i: (0, 0)),
                  BS((ds, d), lambda i: (0, 0)),
                  BS((1, d), lambda i: (0, 0))],
        out_specs=BS((tt, d), lambda i: (i, 0)),
        scratch_shapes=[pltpu.VMEM((TOP_K * tt, SUBLANES, LANES), F32),
                        pltpu.SemaphoreType.DMA(())],
        compiler_params=_cparams("arbitrary"),
        name="moe_combine",
    )(dest_tiles, ys, x_flat, h2b, wt_tk, mod_lat, wsg, wsu, wsd, final_g)


def _tile_major(a, tt):
    k, t = a.shape
    return a.reshape(k, t // tt, tt).transpose(1, 0, 2).reshape(t // tt, 1, k * tt)


def _moe(x_flat, mod_lat, g, w_router, e_bias, w_gate, w_up, w_down, ws_gate, ws_up, ws_down,
         final_g, tri, tokens_per_batch, final):
    t, d = x_flat.shape
    bm = BM_FFN
    h2s, h2b, eidx, wts, rank, cnt = _router(
        x_flat, mod_lat, g, w_router.T, e_bias.reshape(N_EXPERTS, 1), tri, tokens_per_batch)
    counts = cnt[:, 0].astype(I32)
    pcnt = (counts + bm - 1) // bm * bm
    pend = jnp.cumsum(pcnt)
    pstart = pend - pcnt
    n_blocks = t * TOP_K // bm + N_EXPERTS
    n_slots = n_blocks * bm
    dest = pstart[eidx] + rank
    blk_e = jnp.minimum(jnp.searchsorted(pend, jnp.arange(n_blocks, dtype=I32) * bm, side="right"),
                        N_EXPERTS - 1).astype(I32)
    n_used = (pend[-1:] // bm).astype(I32)
    xs = _dispatch(pend.astype(I32), pcnt.astype(I32), n_used, _tile_major(dest, TT_DISP), h2s,
                   n_slots)
    ys = _ffn(blk_e, n_used, xs, w_gate, w_up, w_down)
    return _combine(_tile_major(dest, TT_COMB), ys, x_flat, h2b, wts.T, mod_lat,
                    ws_gate.astype(BF16), ws_up.astype(BF16), ws_down.astype(BF16),
                    final_g.reshape(1, d), tokens_per_batch, final)


def _inproj_c_body(xp_ref, x_ref, xn_ref, mod_ref, g_ref, w_ref, cw_ref, cb_ref,
                   v_ref, g1_ref, g2_ref, *, tm, n_tiles):
    i = pl.program_id(1)
    halo = SUBLANES
    xe = jnp.concatenate([xp_ref[0], x_ref[0], xn_ref[0]], axis=0)
    h = _norm_mod(xe, g_ref[...], mod_ref[0, 1:2, :], mod_ref[0, 0:1, :])
    row = lax.broadcasted_iota(I32, (tm + 2 * halo, 1), 0)
    outside = jnp.logical_or(jnp.logical_and(i == 0, row < halo),
                             jnp.logical_and(i == n_tiles - 1, row >= tm + halo))
    hb = jnp.where(outside, 0.0, h).astype(BF16)
    width = v_ref.shape[-1]
    for part, o_ref in enumerate((v_ref, g1_ref, g2_ref)):
        cols = slice(part * width, (part + 1) * width)
        zp = _dot(hb, w_ref[:, cols])
        up = pltpu.roll(zp, 1, 0)
        dn = pltpu.roll(zp, tm + 2 * halo - 1, 0)
        z = cw_ref[0:1, cols] * up + cw_ref[1:2, cols] * zp + cw_ref[2:3, cols] * dn + cb_ref[:, cols]
        o_ref[0] = z[halo:halo + tm]


def _inproj_c(x, mod_lat, g, w_in, conv_w, conv_b):
    b, l, d = x.shape
    tm = TM_PROJ
    n_tiles = l // tm
    w3 = w_in.shape[1]
    width = w3 // 3
    r8 = tm // SUBLANES
    body = functools.partial(_inproj_c_body, tm=tm, n_tiles=n_tiles)
    out = SDS((b, l, width), F32)
    return pl.pallas_call(
        body,
        out_shape=(out, out, out),
        grid=(b, n_tiles),
        in_specs=[BS((1, SUBLANES, d), lambda bi, i: (bi, jnp.maximum(i * r8 - 1, 0), 0)),
                  BS((1, tm, d), lambda bi, i: (bi, i, 0)),
                  BS((1, SUBLANES, d), lambda bi, i: (bi, jnp.minimum((i + 1) * r8, l // SUBLANES - 1), 0)),
                  BS((1, SUBLANES, d), lambda bi, i: (bi, 0, 0)),
                  BS((1, d), lambda bi, i: (0, 0)),
                  BS((d, w3), lambda bi, i: (0, 0)),
                  BS((3, w3), lambda bi, i: (0, 0)),
                  BS((1, w3), lambda bi, i: (0, 0))],
        out_specs=(BS((1, tm, width), lambda bi, i: (bi, i, 0)),
                   BS((1, tm, width), lambda bi, i: (bi, i, 0)),
                   BS((1, tm, width), lambda bi, i: (bi, i, 0))),
        compiler_params=_cparams("arbitrary", "arbitrary"),
        name="inproj_c",
    )(x, x, x, mod_lat, g, w_in, conv_w, conv_b)


def _filter_body(f_ref, w1_ref, b1_ref, w2_ref, b2_ref, w3_ref, fr_ref, dl_ref, keep0_ref,
                 hf_ref, l1_ref, *, tp):
    i = pl.program_id(0)
    feats = f_ref[...]
    fr = fr_ref[...]
    a = jnp.sin(fr * (_dot_hp(feats, w1_ref[...]) + b1_ref[...]))
    a = jnp.sin(fr * (_dot_hp(a, w2_ref[...]) + b2_ref[...]))
    hf = _dot_hp(a, w3_ref[...])
    t01 = feats[:, 0:1]
    hf = hf * (jnp.exp(-t01 * jnp.abs(dl_ref[...])) + DECAY_SHIFT)
    row = lax.broadcasted_iota(I32, hf.shape, 0) + i * tp
    hf = jnp.where(row == 0, hf * keep0_ref[...], hf)
    hf_ref[...] = hf

    @pl.when(i == 0)
    def _():
        l1_ref[...] = jnp.zeros_like(l1_ref)

    l1_ref[...] = l1_ref[...] + jnp.sum(jnp.abs(hf), axis=0, keepdims=True)


def _filters(feats, w1, b1, w2, b2, w3, freq, delta, width):
    n, fe = feats.shape
    hid = w2.shape[0]
    fo = w3.shape[1]
    tp = 256
    lag0_keep = jnp.tile(jnp.repeat(jnp.array([1.0, 0.0], F32), width), fo // (2 * width)).reshape(1, fo)
    body = functools.partial(_filter_body, tp=tp)
    full = lambda shape: BS(shape, lambda i: (0, 0))
    return pl.pallas_call(
        body,
        out_shape=(SDS((n, fo), F32), SDS((SUBLANES, fo), F32)),
        grid=(n // tp,),
        in_specs=[BS((tp, fe), lambda i: (i, 0)), full((fe, hid)), full((1, hid)),
                  full((hid, hid)), full((1, hid)), full((hid, fo)), full((1, hid)), full((1, fo)),
                  full((1, fo))],
        out_specs=(BS((tp, fo), lambda i: (i, 0)), BS((SUBLANES, fo), lambda i: (0, 0))),
        compiler_params=_cparams("arbitrary"),
        name="hyena_filters",
    )(feats, w1, b1, w2, b2, w3, freq, delta, lag0_keep)


DFT_R = 128


def _dft_tables(n):
    r = DFT_R
    m = 2 * n
    na = n // r
    two_pi = 2.0 * np.pi
    a = np.arange(na)[None, :]
    v = np.arange(r)[:, None]
    ang1 = two_pi * ((a * v) % r) / r
    f1 = np.concatenate([np.cos(ang1), -np.sin(ang1)], axis=0)
    b = np.arange(r)[None, None, :]
    u = np.arange(r)[None, :, None]
    vv = np.arange(r)[:, None, None]
    ang2 = two_pi * ((b * (r * u + vv)) % m) / m
    gr, gi = np.cos(ang2), -np.sin(ang2)
    fwd = np.concatenate([np.concatenate([gr, -gi], axis=2),
                          np.concatenate([gi, gr], axis=2)], axis=1)
    hr, hi = np.transpose(gr, (0, 2, 1)), -np.transpose(gi, (0, 2, 1))
    inv = np.concatenate([np.concatenate([hr, -hi], axis=2),
                          np.concatenate([hi, hr], axis=2)], axis=1)
    ang3 = two_pi * ((np.arange(na)[:, None] * np.arange(r)[None, :]) % r) / r
    f3 = np.concatenate([np.cos(ang3), -np.sin(ang3)], axis=1) / m
    cast = lambda t: jnp.asarray(t.astype(np.float32)).astype(BF16)
    return cast(f1), cast(fwd), cast(inv), cast(f3)


def _dft_s1_body(y_ref, f1_ref, ar_ref, ai_ref):
    f1 = f1_ref[...]
    for j in range(SUBLANES):
        res = _dot(f1, y_ref[:, j, :].astype(BF16))
        ar_ref[:, j, :] = res[:DFT_R]
        ai_ref[:, j, :] = res[DFT_R:]


def _dft_s1(y4, f1):
    nb, na, r, c = y4.shape
    ct = min(c, 1024)
    out = SDS((nb, r, r, c), F32)
    return pl.pallas_call(
        _dft_s1_body,
        out_shape=(out, out),
        grid=(nb, c // ct, r // SUBLANES),
        in_specs=[BS((None, na, SUBLANES, ct), lambda n, cc, j: (n, 0, j, cc)),
                  BS((2 * r, na), lambda n, cc, j: (0, 0))],
        out_specs=(BS((None, r, SUBLANES, ct), lambda n, cc, j: (n, 0, j, cc)),
                   BS((None, r, SUBLANES, ct), lambda n, cc, j: (n, 0, j, cc))),
        compiler_params=_cparams("arbitrary", "arbitrary", "arbitrary"),
        name="dft_stage1",
    )(y4, f1)


def _filter_spec_body(arf_ref, aif_ref, arb_ref, aib_ref, g_ref, l1f_ref, l1b_ref, kr_ref, ki_ref):
    g = g_ref[...]
    yf = _dot(g, jnp.concatenate([arf_ref[...], aif_ref[...]], axis=0).astype(BF16))
    yb = _dot(g, jnp.concatenate([arb_ref[...], aib_ref[...]], axis=0).astype(BF16))
    inv = 1.0 / (l1f_ref[0:1, :] + l1b_ref[0:1, :])
    kr_ref[...] = (yf[:DFT_R] + yb[:DFT_R]) * inv
    ki_ref[...] = (yf[DFT_R:] - yb[DFT_R:]) * inv


def _filter_spectrum(ar, ai, fwd, l1, width):
    r = DFT_R
    a_spec = lambda d: BS((None, None, r, width), lambda v, o: (0, v, 0, 2 * o + d))
    l_spec = lambda d: BS((SUBLANES, width), lambda v, o: (0, 2 * o + d))
    out = SDS((r, r, HYENA_ORDER * width), F32)
    return pl.pallas_call(
        _filter_spec_body,
        out_shape=(out, out),
        grid=(r, HYENA_ORDER),
        in_specs=[a_spec(0), a_spec(0), a_spec(1), a_spec(1),
                  BS((None, 2 * r, 2 * r), lambda v, o: (v, 0, 0)), l_spec(0), l_spec(1)],
        out_specs=(BS((None, r, width), lambda v, o: (v, 0, o)),
                   BS((None, r, width), lambda v, o: (v, 0, o))),
        compiler_params=_cparams("arbitrary", "arbitrary"),
        name="hyena_filter_spectrum",
    )(ar, ai, ar, ai, fwd, l1, l1)


def _conv_mid_body(ar_ref, ai_ref, g_ref, h_ref, kr_ref, ki_ref, qr_ref, qi_ref):
    y = _dot(g_ref[...], jnp.concatenate([ar_ref[...], ai_ref[...]], axis=0).astype(BF16))
    yr, yi = y[:DFT_R], y[DFT_R:]
    kr, ki = kr_ref[...], ki_ref[...]
    p = jnp.concatenate([yr * kr - yi * ki, yr * ki + yi * kr], axis=0).astype(BF16)
    q = _dot(h_ref[...], p)
    qr_ref[...] = q[:DFT_R]
    qi_ref[...] = q[DFT_R:]


def _conv_mid(ar, ai, fwd, inv, kr, ki, order):
    nb, r, _, c = ar.shape
    a_spec = BS((None, None, r, c), lambda n, v: (n, v, 0, 0))
    m_spec = BS((None, 2 * r, 2 * r), lambda n, v: (v, 0, 0))
    k_spec = BS((None, r, c), lambda n, v: (v, 0, order))
    out = SDS((nb, r, r, c), F32)
    return pl.pallas_call(
        _conv_mid_body,
        out_shape=(out, out),
        grid=(nb, r),
        in_specs=[a_spec, a_spec, m_spec, m_spec, k_spec, k_spec],
        out_specs=(a_spec, a_spec),
        compiler_params=_cparams("arbitrary", "arbitrary"),
        name="hyena_spectral_product",
    )(ar, ai, fwd, inv, kr, ki)


def _idft_gate_body(qr_ref, qi_ref, f3_ref, y_ref, gate_ref, fb_ref, o_ref):
    f3 = f3_ref[...]
    fb = fb_ref[...]
    for j in range(SUBLANES):
        q = jnp.concatenate([qr_ref[:, j, :], qi_ref[:, j, :]], axis=0).astype(BF16)
        conv = _dot(f3, q)
        o_ref[:, j, :] = gate_ref[:, j, :] * (conv + fb * y_ref[:, j, :])


def _idft_gate(qr, qi, f3, y4, gate4, fbias):
    nb, na, r, c = y4.shape
    q_spec = BS((None, r, SUBLANES, c), lambda n, j: (n, 0, j, 0))
    y_spec = BS((None, na, SUBLANES, c), lambda n, j: (n, 0, j, 0))
    return pl.pallas_call(
        _idft_gate_body,
        out_shape=SDS((nb, na, r, c), F32),
        grid=(nb, r // SUBLANES),
        in_specs=[q_spec, q_spec, BS((na, 2 * r), lambda n, j: (0, 0)), y_spec, y_spec,
                  BS((1, c), lambda n, j: (0, 0))],
        out_specs=y_spec,
        compiler_params=_cparams("arbitrary", "arbitrary"),
        name="hyena_idft_gate",
    )(qr, qi, f3, y4, gate4, fbias)


def _outproj_body(y_ref, w_ref, x_ref, mod_ref, o_ref):
    o_ref[0] = x_ref[0] + mod_ref[0, 2:3, :] * _dot(y_ref[0].astype(BF16), w_ref[...])


def _outproj(y, w_out, x, mod_lat):
    b, l, d = x.shape
    tm = TM_PROJ
    wdt = y.shape[-1]
    return pl.pallas_call(
        _outproj_body,
        out_shape=SDS((b, l, d), F32),
        grid=(b, l // tm),
        in_specs=[BS((1, tm, wdt), lambda bi, i: (bi, i, 0)),
                  BS((wdt, d), lambda bi, i: (0, 0)),
                  BS((1, tm, d), lambda bi, i: (bi, i, 0)),
                  BS((1, SUBLANES, d), lambda bi, i: (bi, 0, 0))],
        out_specs=BS((1, tm, d), lambda bi, i: (bi, i, 0)),
        compiler_params=_cparams("arbitrary", "arbitrary"),
        name="outproj_c",
    )(y, w_out, x, mod_lat)


def _hyena(x, mod_lat, g, w_in, conv_w, conv_b, w1, b1, w2, b2, w3, freq, delta, f_bias, w_out):
    b, n, d = x.shape
    width = w_out.shape[0]
    r = DFT_R
    na = n // r
    f1, fwd, inv, f3 = _dft_tables(n)
    v, gate1, gate2 = _inproj_c(x, mod_lat, g, w_in.astype(BF16), conv_w, conv_b.reshape(1, -1))

    t = jnp.arange(n, dtype=F32)
    t01 = t / max(n - 1, 1)
    bands = jnp.linspace(1e-4, FILT_BANDS - 1, FILT_BANDS, dtype=F32)
    ang = (2.0 * math.pi / n) * t[:, None] * bands[None, :]
    feats = jnp.concatenate([t01[:, None], jnp.cos(ang), jnp.sin(ang)], axis=-1)
    fe = feats.shape[1]
    feats = jnp.pad(feats, ((0, 0), (0, LANES - fe)))
    w1p = jnp.pad(w1, ((0, LANES - fe), (0, 0)))
    hf, l1 = _filters(feats, w1p, b1.reshape(1, -1), w2, b2.reshape(1, -1), w3,
                      freq.reshape(1, -1), delta.reshape(1, -1), width)
    far, fai = _dft_s1(hf.reshape(1, na, r, hf.shape[1]), f1)
    kr, ki = _filter_spectrum(far, fai, fwd, l1, width)

    y4 = v.reshape(b, na, r, width)
    for o, gate in enumerate((gate1, gate2)):
        ar, ai = _dft_s1(y4, f1)
        qr, qi = _conv_mid(ar, ai, fwd, inv, kr, ki, o)
        y4 = _idft_gate(qr, qi, f3, y4, gate.reshape(b, na, r, width), f_bias[o].reshape(1, width))
    return _outproj(y4.reshape(b, n, width), w_out.astype(BF16), x, mod_lat)


def _rope_tables(seq_len):
    rows = seq_len // GRID_W
    row = jnp.repeat(jnp.arange(rows, dtype=F32), GRID_W)
    col = jnp.tile(jnp.arange(GRID_W, dtype=F32), rows)
    inv = jnp.power(ROPE_BASE, -jnp.arange(ROPE_FREQS, dtype=F32) / ROPE_FREQS)
    ar, ac = row[:, None] * inv, col[:, None] * inv
    cos_h = jnp.concatenate([jnp.cos(ar), jnp.cos(ar), jnp.cos(ac), jnp.cos(ac)], axis=1)
    sin_h = jnp.concatenate([-jnp.sin(ar), jnp.sin(ar), -jnp.sin(ac), jnp.sin(ac)], axis=1)
    reps = LANES // HEAD_DIM
    return jnp.tile(cos_h, (1, reps)), jnp.tile(sin_h, (1, reps))


def _rotate_partner_columns(w):
    ncol = w.shape[1]
    lane = np.arange(ncol)
    partner = np.where((lane % (2 * ROPE_FREQS)) < ROPE_FREQS, lane + ROPE_FREQS, lane - ROPE_FREQS)
    return w[:, partner]


def kernel(x, c, ctx, c_ctx, w_mod, b_mod, norm_g, w_in_ab, sink, w_spatial, b_spatial, w_out_ab,
           w_in_c, conv_w, conv_b, filt_w1, filt_b1, filt_w2, filt_b2, filt_w3, filt_freq,
           filt_delta, filt_bias, w_out_c, w_router, e_bias, w_gate, w_up, w_down, ws_gate,
           ws_up, ws_down, final_g):
    b, l, d = x.shape
    depth = w_mod.shape[0]
    assert depth == 2 and b + 1 <= SUBLANES

    cc = jnp.zeros((SUBLANES, d), F32).at[:b].set(c).at[b].set(c_ctx)
    m_all = _mod_vectors(cc, w_mod, b_mod)

    def mod_rows(layer, row0, nrow):
        m = m_all[layer, row0:row0 + nrow].reshape(nrow, 6, d)
        return jnp.pad(m, ((0, 0), (0, SUBLANES - 6), (0, 0)))

    tri = jnp.triu(jnp.ones((TM_ROUTE, TM_ROUTE), F32), k=1).astype(BF16)

    mod_lat = mod_rows(0, 0, b)
    mod_ctx = mod_rows(0, b, 1)[0]
    w_in = w_in_ab[0]
    qk = ATTN_WIDTH + KV_WIDTH
    w_cat = jnp.concatenate([w_in, _rotate_partner_columns(w_in[:, :qk])], axis=1).astype(BF16)
    cos_t, sin_t = _rope_tables(l)
    group_avg = jnp.kron(jnp.eye(N_SG_GROUPS, dtype=F32),
                         jnp.full((SG_GROUP_DIM, SG_GROUP_DIM), 1.0 / SG_GROUP_DIM, F32)).astype(BF16)
    kc, vc = _ctx_kv(ctx, mod_ctx, norm_g[0, 0].reshape(1, d),
                     w_in[:, ATTN_WIDTH:ATTN_WIDTH + 2 * KV_WIDTH].astype(BF16))
    q, k, v, ug, vn = _inproj_ab(x, mod_lat, norm_g[0, 0].reshape(1, d), w_cat, cos_t, sin_t, group_avg)
    b_full = jnp.repeat(b_spatial[0].T, SG_GROUP_DIM, axis=1)
    x1 = _mixer(sink[0], q, k, v, kc, vc, ug, vn, w_spatial[0].astype(BF16), b_full,
                w_out_ab[0].astype(BF16), x, mod_lat)
    x2 = _moe(x1.reshape(b * l, d), mod_lat, norm_g[0, 1].reshape(1, d), w_router[0], e_bias[0],
              w_gate[0], w_up[0], w_down[0], ws_gate[0], ws_up[0], ws_down[0], final_g, tri, l,
              final=False).reshape(b, l, d)

    mod_lat = mod_rows(1, 0, b)
    x3 = _hyena(x2, mod_lat, norm_g[1, 0].reshape(1, d), w_in_c[0], conv_w[0], conv_b[0],
                filt_w1[0], filt_b1[0], filt_w2[0], filt_b2[0], filt_w3[0], filt_freq[0],
                filt_delta[0], filt_bias[0], w_out_c[0])
    out = _moe(x3.reshape(b * l, d), mod_lat, norm_g[1, 1].reshape(1, d), w_router[1], e_bias[1],
               w_gate[1], w_up[1], w_down[1], ws_gate[1], ws_up[1], ws_down[1], final_g, tri, l,
               final=True)
    return out.reshape(b, l, d)
```

```python
import functools
import math

import numpy as np
import jax
import jax.numpy as jnp
from jax import lax
from jax.experimental import pallas as pl
from jax.experimental.pallas import tpu as pltpu

F32 = jnp.float32
BF16 = jnp.bfloat16
I32 = jnp.int32
HIGHEST = lax.Precision.HIGHEST
SDS = jax.ShapeDtypeStruct
BS = pl.BlockSpec

EPS = 1e-6
NEG = -1e30

GRID_W = 64
N_Q_HEADS = 8
N_KV_HEADS = 2
HEAD_DIM = 64
ATTN_WIDTH = N_Q_HEADS * HEAD_DIM
KV_WIDTH = N_KV_HEADS * HEAD_DIM
WINDOW = 128
BLOCK = 128
ROPE_BASE = 10000.0
ROPE_FREQS = HEAD_DIM // 4
N_SG_GROUPS = 8
SG_GROUP_DIM = 64
SG_WIDTH = N_SG_GROUPS * SG_GROUP_DIM
HYENA_ORDER = 2
FILT_BANDS = 16
DECAY_SHIFT = 0.05
N_EXPERTS = 64
TOP_K = 8
N_GROUPS = 8
TOPK_GROUPS = 4
ROUTED_SCALE = 2.5

LANES = 128
SUBLANES = 8
VMEM_LIMIT = 56 * 1024 * 1024

TM_PROJ = 512
TQ_MIX = 256
TM_ROUTE = 512
TT_DISP = 256
TT_COMB = 128
BM_FFN = 256


def _cparams(*sem):
    return pltpu.CompilerParams(dimension_semantics=sem, vmem_limit_bytes=VMEM_LIMIT)


def _dot(a, b):
    return jnp.dot(a, b, preferred_element_type=F32)


def _dot_nt(a, b):
    return lax.dot_general(a, b, (((1,), (1,)), ((), ())), preferred_element_type=F32)


def _dot_hp(a, b):
    return jnp.dot(a, b, preferred_element_type=F32, precision=HIGHEST)


def _norm_mod(x, g, sc, sh):
    ms = jnp.mean(x * x, axis=-1, keepdims=True)
    y = x * lax.rsqrt(ms + EPS)
    return (y * g) * (1.0 + sc) + sh


def _gelu_tanh(x):
    c = math.sqrt(2.0 / math.pi)
    return 0.5 * x * (1.0 + jnp.tanh(c * (x + 0.044715 * (x * x * x))))


def _silu(x):
    return x * jax.nn.sigmoid(x)


def _mod_body(c_ref, w_ref, b_ref, o_ref):
    o_ref[0] = _dot_hp(_silu(c_ref[...]), w_ref[0]) + b_ref[0]


def _mod_vectors(cc, w_mod, b_mod):
    depth, d, n = w_mod.shape
    tn = 1536
    return pl.pallas_call(
        _mod_body,
        out_shape=SDS((depth, SUBLANES, n), F32),
        grid=(depth, n // tn),
        in_specs=[BS((SUBLANES, d), lambda l, j: (0, 0)),
                  BS((1, d, tn), lambda l, j: (l, 0, j)),
                  BS((1, 1, tn), lambda l, j: (l, 0, j))],
        out_specs=BS((1, SUBLANES, tn), lambda l, j: (l, 0, j)),
        compiler_params=_cparams("arbitrary", "arbitrary"),
        name="mod_vectors",
    )(cc, w_mod, b_mod.reshape(depth, 1, n))


def _ctx_kv_body(ctx_ref, mod_ref, g_ref, w_ref, kc_ref, vc_ref):
    h = _norm_mod(ctx_ref[0], g_ref[...], mod_ref[1:2, :], mod_ref[0:1, :])
    z = _dot(h.astype(BF16), w_ref[...])
    kc_ref[0] = z[:, :KV_WIDTH].astype(BF16)
    vc_ref[0] = z[:, KV_WIDTH:].astype(BF16)


def _ctx_kv(ctx, mod_ctx, g, w_kv):
    b, c, d = ctx.shape
    return pl.pallas_call(
        _ctx_kv_body,
        out_shape=(SDS((b, c, KV_WIDTH), BF16), SDS((b, c, KV_WIDTH), BF16)),
        grid=(b,),
        in_specs=[BS((1, c, d), lambda i: (i, 0, 0)),
                  BS((SUBLANES, d), lambda i: (0, 0)),
                  BS((1, d), lambda i: (0, 0)),
                  BS((d, 2 * KV_WIDTH), lambda i: (0, 0))],
        out_specs=(BS((1, c, KV_WIDTH), lambda i: (i, 0, 0)),
                   BS((1, c, KV_WIDTH), lambda i: (i, 0, 0))),
        compiler_params=_cparams("arbitrary"),
        name="ctx_kv",
    )(ctx, mod_ctx, g, w_kv)


def _inproj_ab_body(x_ref, mod_ref, g_ref, w_ref, cos_ref, sin_ref, avg_ref,
                    q_ref, k_ref, v_ref, ug_ref, vn_ref):
    h = _norm_mod(x_ref[0], g_ref[...], mod_ref[0, 1:2, :], mod_ref[0, 0:1, :]).astype(BF16)
    cs = cos_ref[...]
    sn = sin_ref[...]
    rot0 = ATTN_WIDTH + 2 * KV_WIDTH + 2 * SG_WIDTH
    scale = HEAD_DIM ** -0.5
    for j in range(ATTN_WIDTH // LANES):
        z = _dot(h, w_ref[:, j * LANES:(j + 1) * LANES])
        zr = _dot(h, w_ref[:, rot0 + j * LANES:rot0 + (j + 1) * LANES])
        q_ref[0, :, j * LANES:(j + 1) * LANES] = ((z * cs + zr * sn) * scale).astype(BF16)
    zk = _dot(h, w_ref[:, ATTN_WIDTH:ATTN_WIDTH + KV_WIDTH])
    zkr = _dot(h, w_ref[:, rot0 + ATTN_WIDTH:rot0 + ATTN_WIDTH + KV_WIDTH])
    k_ref[0] = (zk * cs + zkr * sn).astype(BF16)
    v_ref[0] = _dot(h, w_ref[:, ATTN_WIDTH + KV_WIDTH:ATTN_WIDTH + 2 * KV_WIDTH]).astype(BF16)
    u0 = ATTN_WIDTH + 2 * KV_WIDTH
    ug_ref[0] = _gelu_tanh(_dot(h, w_ref[:, u0:u0 + SG_WIDTH]))
    vf = _gelu_tanh(_dot(h, w_ref[:, u0 + SG_WIDTH:u0 + 2 * SG_WIDTH]))
    avg = avg_ref[...]

    def gmean(t):
        hi = t.astype(BF16)
        lo = (t - hi.astype(F32)).astype(BF16)
        return _dot(hi, avg) + _dot(lo, avg)

    vc = vf - gmean(vf)
    vn_ref[0] = (vc * lax.rsqrt(gmean(vc * vc) + EPS)).astype(BF16)


def _inproj_ab(x, mod_lat, g, w_cat, cos_t, sin_t, avg):
    b, l, d = x.shape
    tm = TM_PROJ
    ncol = w_cat.shape[1]
    return pl.pallas_call(
        _inproj_ab_body,
        out_shape=(SDS((b, l, ATTN_WIDTH), BF16), SDS((b, l, KV_WIDTH), BF16),
                   SDS((b, l, KV_WIDTH), BF16), SDS((b, l, SG_WIDTH), F32),
                   SDS((b, l, SG_WIDTH), BF16)),
        grid=(b, l // tm),
        in_specs=[BS((1, tm, d), lambda bi, i: (bi, i, 0)),
                  BS((1, SUBLANES, d), lambda bi, i: (bi, 0, 0)),
                  BS((1, d), lambda bi, i: (0, 0)),
                  BS((d, ncol), lambda bi, i: (0, 0)),
                  BS((tm, LANES), lambda bi, i: (i, 0)),
                  BS((tm, LANES), lambda bi, i: (i, 0)),
                  BS((SG_WIDTH, SG_WIDTH), lambda bi, i: (0, 0))],
        out_specs=(BS((1, tm, ATTN_WIDTH), lambda bi, i: (bi, i, 0)),
                   BS((1, tm, KV_WIDTH), lambda bi, i: (bi, i, 0)),
                   BS((1, tm, KV_WIDTH), lambda bi, i: (bi, i, 0)),
                   BS((1, tm, SG_WIDTH), lambda bi, i: (bi, i, 0)),
                   BS((1, tm, SG_WIDTH), lambda bi, i: (bi, i, 0))),
        compiler_params=_cparams("arbitrary", "arbitrary"),
        name="inproj_ab",
    )(x, mod_lat, g, w_cat, cos_t, sin_t, avg)


def _mixer_body(sink_ref, q_ref, kp_ref, kcur_ref, kn_ref, vp_ref, vcur_ref, vn_ref,
                kc_ref, vc_ref, ug_ref, vnorm_ref, ws_ref, bs_ref, wout_ref, x_ref, mod_ref,
                o_ref, cat_ref, *, seq_len, sub_blocks):
    i = pl.program_id(1)
    kk = jnp.concatenate([kp_ref[0], kcur_ref[0], kn_ref[0]], axis=0)
    vv = jnp.concatenate([vp_ref[0], vcur_ref[0], vn_ref[0]], axis=0)
    kc = kc_ref[0]
    vc = vc_ref[0]
    span = 3 * BLOCK
    ii = lax.broadcasted_iota(I32, (BLOCK, span), 0)
    jj = lax.broadcasted_iota(I32, (BLOCK, span), 1)
    dd = jj - ii
    in_window = jnp.where(dd >= 0, jnp.where(dd <= 2 * WINDOW, 1, 0), 0)
    group = N_Q_HEADS // N_KV_HEADS
    for r in range(sub_blocks):
        rows = slice(r * BLOCK, (r + 1) * BLOCK)
        kpos = (i * sub_blocks + r - 1) * BLOCK + jj
        in_seq = jnp.where(kpos >= 0, jnp.where(kpos < seq_len, 1, 0), 0)
        bias = jnp.where(in_window * in_seq > 0, 0.0, NEG)
        qb = q_ref[0, rows, :]
        kl = kk[r * BLOCK:r * BLOCK + span]
        vl = vv[r * BLOCK:r * BLOCK + span]
        for hq in range(N_Q_HEADS):
            hk = hq // group
            ks = slice(hk * HEAD_DIM, (hk + 1) * HEAD_DIM)
            qh = qb[:, hq * HEAD_DIM:(hq + 1) * HEAD_DIM]
            s_loc = _dot_nt(qh, kl[:, ks]) + bias
            s_ctx = _dot_nt(qh, kc[:, ks])
            sk = sink_ref[hq]
            m = jnp.maximum(jnp.maximum(jnp.max(s_loc, axis=-1, keepdims=True),
                                        jnp.max(s_ctx, axis=-1, keepdims=True)), sk)
            p_loc = jnp.exp(s_loc - m)
            p_ctx = jnp.exp(s_ctx - m)
            den = (jnp.sum(p_loc, axis=-1, keepdims=True) + jnp.sum(p_ctx, axis=-1, keepdims=True)
                   + jnp.exp(sk - m))
            o = _dot(p_loc.astype(BF16), vl[:, ks]) + _dot(p_ctx.astype(BF16), vc[:, ks])
            cat_ref[rows, hq * HEAD_DIM:(hq + 1) * HEAD_DIM] = (o / den).astype(BF16)
        vnb = vnorm_ref[0, rows, :]
        ugb = ug_ref[0, rows, :]
        for g in range(N_SG_GROUPS):
            gs = slice(g * SG_GROUP_DIM, (g + 1) * SG_GROUP_DIM)
            sg = _dot(ws_ref[g], vnb[:, gs]) + bs_ref[:, gs]
            cat_ref[rows, ATTN_WIDTH + g * SG_GROUP_DIM:ATTN_WIDTH + (g + 1) * SG_GROUP_DIM] = (
                ugb[:, gs] * sg).astype(BF16)
    y = _dot(cat_ref[...], wout_ref[...])
    o_ref[0] = x_ref[0] + mod_ref[0, 2:3, :] * y


def _mixer(sink, q, k, v, kc, vc, ug, vn, w_s, b_full, w_out, x, mod_lat):
    b, l, d = x.shape
    tq = TQ_MIX
    r = tq // BLOCK
    nb = l // BLOCK
    c = kc.shape[1]
    prev_map = lambda bi, i: (bi, jnp.maximum(i * r - 1, 0), 0)
    next_map = lambda bi, i: (bi, jnp.minimum((i + 1) * r, nb - 1), 0)
    cur_map = lambda bi, i: (bi, i, 0)
    body = functools.partial(_mixer_body, seq_len=l, sub_blocks=r)
    return pl.pallas_call(
        body,
        out_shape=SDS((b, l, d), F32),
        grid=(b, l // tq),
        in_specs=[BS(memory_space=pltpu.SMEM),
                  BS((1, tq, ATTN_WIDTH), cur_map),
                  BS((1, BLOCK, KV_WIDTH), prev_map), BS((1, tq, KV_WIDTH), cur_map),
                  BS((1, BLOCK, KV_WIDTH), next_map),
                  BS((1, BLOCK, KV_WIDTH), prev_map), BS((1, tq, KV_WIDTH), cur_map),
                  BS((1, BLOCK, KV_WIDTH), next_map),
                  BS((1, c, KV_WIDTH), lambda bi, i: (bi, 0, 0)),
                  BS((1, c, KV_WIDTH), lambda bi, i: (bi, 0, 0)),
                  BS((1, tq, SG_WIDTH), cur_map), BS((1, tq, SG_WIDTH), cur_map),
                  BS((N_SG_GROUPS, BLOCK, BLOCK), lambda bi, i: (0, 0, 0)),
                  BS((BLOCK, SG_WIDTH), lambda bi, i: (0, 0)),
                  BS((d, d), lambda bi, i: (0, 0)),
                  BS((1, tq, d), cur_map),
                  BS((1, SUBLANES, d), lambda bi, i: (bi, 0, 0))],
        out_specs=BS((1, tq, d), cur_map),
        scratch_shapes=[pltpu.VMEM((tq, d), BF16)],
        compiler_params=_cparams("arbitrary", "arbitrary"),
        name="mixer_ab",
    )(sink, q, k, k, k, v, v, v, kc, vc, ug, vn, w_s, b_full, w_out, x, mod_lat)


def _router_body(x_ref, mod_ref, g_ref, wr_ref, eb_ref, tri_ref,
                 h2s_ref, h2b_ref, eidx_ref, wts_ref, rank_ref, cnt_ref, carry_ref, *, tm):
    i = pl.program_id(0)

    @pl.when(i == 0)
    def _():
        carry_ref[...] = jnp.zeros_like(carry_ref)

    h2 = _norm_mod(x_ref[...], g_ref[...], mod_ref[0, 4:5, :], mod_ref[0, 3:4, :])
    h2b_ref[...] = h2.astype(BF16)
    h2s_ref[...] = h2

    logits = lax.dot_general(wr_ref[...], h2, (((1,), (1,)), ((), ())),
                             preferred_element_type=F32, precision=HIGHEST)
    scores = jax.nn.sigmoid(logits)
    per_group = N_EXPERTS // N_GROUPS
    shape3 = (N_GROUPS, per_group, tm)
    s3 = scores.reshape(shape3)
    b3 = (scores + eb_ref[...]).reshape(shape3)
    sub = lax.broadcasted_iota(I32, shape3, 1)
    eid = lax.broadcasted_iota(I32, shape3, 0) * per_group + sub

    m1 = jnp.max(b3, axis=1, keepdims=True)
    i1 = jnp.min(jnp.where(b3 == m1, sub, per_group), axis=1, keepdims=True)
    m2 = jnp.max(jnp.where(sub == i1, -jnp.inf, b3), axis=1, keepdims=True)
    gs = m1 + m2
    keep = []
    for g in range(N_GROUPS):
        beaten = jnp.zeros((1, tm), I32)
        for g2 in range(N_GROUPS):
            if g2 == g:
                continue
            wins = (gs[g2] >= gs[g]) if g2 < g else (gs[g2] > gs[g])
            beaten = beaten + jnp.where(wins, 1, 0)
        keep.append(jnp.where(beaten < TOPK_GROUPS, 1, 0)[None])
    keep3 = jnp.concatenate(keep, axis=0)
    val = jnp.where(keep3 > 0, b3, -jnp.inf)

    def red(fn, a):
        return fn(fn(a, axis=0, keepdims=True), axis=1, keepdims=True)

    idxs, ws = [], []
    member = jnp.zeros(shape3, F32)
    for _ in range(TOP_K):
        m = red(jnp.max, val)
        idx = red(jnp.min, jnp.where(val == m, eid, N_EXPERTS))
        hit = eid == idx
        ws.append(red(jnp.sum, jnp.where(hit, s3, 0.0)))
        val = jnp.where(hit, -jnp.inf, val)
        member = member + jnp.where(hit, 1.0, 0.0)
        idxs.append(idx)
    wsum = ws[0]
    for w in ws[1:]:
        wsum = wsum + w

    member2 = member.reshape(N_EXPERTS, tm)
    before = _dot(member2.astype(BF16), tri_ref[...]) + carry_ref[:, 0:1]
    before3 = before.reshape(shape3)
    for k in range(TOP_K):
        eidx_ref[k:k + 1, :] = idxs[k].reshape(1, tm)
        wts_ref[k:k + 1, :] = (ws[k] / wsum * ROUTED_SCALE).reshape(1, tm)
        rk = red(jnp.sum, jnp.where(eid == idxs[k], before3, 0.0))
        rank_ref[k:k + 1, :] = rk.reshape(1, tm).astype(I32)
    total = carry_ref[...] + jnp.sum(member2, axis=1, keepdims=True)
    carry_ref[...] = total
    cnt_ref[...] = total


def _router(x_flat, mod_lat, g, wr_t, e_bias, tri, tokens_per_batch):
    t, d = x_flat.shape
    tm = TM_ROUTE
    tiles_per_batch = tokens_per_batch // tm
    body = functools.partial(_router_body, tm=tm)
    return pl.pallas_call(
        body,
        out_shape=(SDS((t, d), F32), SDS((t, d), BF16),
                   SDS((TOP_K, t), I32), SDS((TOP_K, t), F32), SDS((TOP_K, t), I32),
                   SDS((N_EXPERTS, LANES), F32)),
        grid=(t // tm,),
        in_specs=[BS((tm, d), lambda i: (i, 0)),
                  BS((1, SUBLANES, d), lambda i: (i // tiles_per_batch, 0, 0)),
                  BS((1, d), lambda i: (0, 0)),
                  BS((N_EXPERTS, d), lambda i: (0, 0)),
                  BS((N_EXPERTS, 1), lambda i: (0, 0)),
                  BS((tm, tm), lambda i: (0, 0))],
        out_specs=(BS((tm, d), lambda i: (i, 0)),
                   BS((tm, d), lambda i: (i, 0)),
                   BS((TOP_K, tm), lambda i: (0, i)),
                   BS((TOP_K, tm), lambda i: (0, i)),
                   BS((TOP_K, tm), lambda i: (0, i)),
                   BS((N_EXPERTS, LANES), lambda i: (0, 0))),
        scratch_shapes=[pltpu.VMEM((N_EXPERTS, LANES), F32)],
        compiler_params=_cparams("arbitrary"),
        name="moe_router",
    )(x_flat, mod_lat, g, wr_t, e_bias, tri)


def _plan_body(pstart_ref, eidx_ref, rank_ref, dest_ref):
    e = eidx_ref[...]
    dest = rank_ref[...]
    for x in range(N_EXPERTS):
        dest = dest + jnp.where(e == x, pstart_ref[x], 0)
    dest_ref[...] = dest


def _plan(pstart, eidx, rank):
    k, t = eidx.shape
    tl = min(t, 2048)
    grid_spec = pltpu.PrefetchScalarGridSpec(
        num_scalar_prefetch=1,
        grid=(t // tl,),
        in_specs=[BS((k, tl), lambda i, *_: (0, i)), BS((k, tl), lambda i, *_: (0, i))],
        out_specs=BS((k, tl), lambda i, *_: (0, i)),
    )
    return pl.pallas_call(
        _plan_body,
        out_shape=SDS((k, t), I32),
        grid_spec=grid_spec,
        compiler_params=_cparams("arbitrary"),
        name="moe_plan",
    )(pstart, eidx, rank)


def _dispatch_body(pend_ref, pcnt_ref, nu_ref, dest_ref, h_ref, xs_ref, zbuf_ref, sem, zsem, *,
                   tt, bm, n_blocks):
    i = pl.program_id(0)

    def zero_copy(row0):
        return pltpu.make_async_copy(
            zbuf_ref, xs_ref.at[pl.ds(pl.multiple_of(row0, SUBLANES), bm), :], zsem)

    @pl.when(i == 0)
    def _():
        zbuf_ref[...] = jnp.zeros_like(zbuf_ref)

        def start(e, c):
            @pl.when(pcnt_ref[e] > 0)
            def _():
                zero_copy(pend_ref[e] - bm).start()
            return c

        def wait(e, c):
            @pl.when(pcnt_ref[e] > 0)
            def _():
                zero_copy(pend_ref[e] - bm).wait()
            return c

        def start_tail(j, c):
            zero_copy(j * bm).start()
            return c

        def wait_tail(j, c):
            zero_copy(j * bm).wait()
            return c

        lax.fori_loop(0, N_EXPERTS, start, 0)
        lax.fori_loop(nu_ref[0], n_blocks, start_tail, 0)
        lax.fori_loop(0, N_EXPERTS, wait, 0)
        lax.fori_loop(nu_ref[0], n_blocks, wait_tail, 0)

    def issue(t, c):
        for k in range(TOP_K):
            d = dest_ref[0, 0, k * tt + t]
            pltpu.make_async_copy(h_ref.at[pl.ds(t, 1), :], xs_ref.at[pl.ds(d, 1), :], sem).start()
        return c

    lax.fori_loop(0, tt, issue, 0)
    for k in range(TOP_K):
        pltpu.make_async_copy(h_ref, xs_ref.at[pl.ds(0, tt), :], sem).wait()


def _dispatch(pend, pcnt, n_used, dest_tiles, h2s, n_slots):
    t, d = h2s.shape
    tt = TT_DISP
    body = functools.partial(_dispatch_body, tt=tt, bm=BM_FFN, n_blocks=n_slots // BM_FFN)
    grid_spec = pltpu.PrefetchScalarGridSpec(
        num_scalar_prefetch=3,
        grid=(t // tt,),
        in_specs=[BS((1, 1, TOP_K * tt), lambda i, *_: (i, 0, 0), memory_space=pltpu.SMEM),
                  BS((tt, d), lambda i, *_: (i, 0))],
        out_specs=BS(memory_space=pl.ANY),
        scratch_shapes=[pltpu.VMEM((BM_FFN, d), F32),
                        pltpu.SemaphoreType.DMA(()), pltpu.SemaphoreType.DMA(())],
    )
    return pl.pallas_call(
        body,
        out_shape=SDS((n_slots, d), F32),
        grid_spec=grid_spec,
        compiler_params=_cparams("arbitrary"),
        name="moe_dispatch",
    )(pend, pcnt, n_used, dest_tiles, h2s)


def _ffn_body(be_ref, nu_ref, xs_ref, wg_ref, wu_ref, wd_ref, ys_ref, wgb_ref, wub_ref, wdb_ref):
    i = pl.program_id(0)
    fresh = jnp.logical_or(i == 0, be_ref[i] != be_ref[jnp.maximum(i - 1, 0)])

    @pl.when(jnp.logical_and(fresh, i < nu_ref[0]))
    def _():
        wgb_ref[...] = wg_ref[...].astype(BF16)
        wub_ref[...] = wu_ref[...].astype(BF16)
        wdb_ref[...] = wd_ref[...].astype(BF16)

    @pl.when(i < nu_ref[0])
    def _():
        x = xs_ref[...].astype(BF16)
        a = _silu(_dot(x, wgb_ref[...])) * _dot(x, wub_ref[...])
        ys_ref[...] = _dot(a.astype(BF16), wdb_ref[...])

    @pl.when(i >= nu_ref[0])
    def _():
        ys_ref[...] = jnp.zeros_like(ys_ref)


def _ffn(blk_e, n_used, xs, w_gate, w_up, w_down, layer):
    n_slots, d = xs.shape
    bm = BM_FFN
    de = w_gate.shape[-1]
    grid_spec = pltpu.PrefetchScalarGridSpec(
        num_scalar_prefetch=2,
        grid=(n_slots // bm,),
        in_specs=[BS((bm, d), lambda i, be, nu: (jnp.minimum(i, nu[0] - 1), 0)),
                  BS((None, None, d, de), lambda i, be, nu: (layer, be[i], 0, 0)),
                  BS((None, None, d, de), lambda i, be, nu: (layer, be[i], 0, 0)),
                  BS((None, None, de, d), lambda i, be, nu: (layer, be[i], 0, 0))],
        out_specs=BS((bm, d), lambda i, be, nu: (i, 0)),
        scratch_shapes=[pltpu.VMEM((d, de), BF16), pltpu.VMEM((d, de), BF16),
                        pltpu.VMEM((de, d), BF16)],
    )
    return pl.pallas_call(
        _ffn_body,
        out_shape=SDS((n_slots, d), F32),
        grid_spec=grid_spec,
        compiler_params=_cparams("arbitrary"),
        name="moe_experts",
    )(blk_e, n_used, xs, w_gate, w_up, w_down)


def _combine_body(dest_ref, ys_ref, x_ref, h2b_ref, wt_ref, mod_ref, wsg_ref, wsu_ref, wsd_ref,
                  fg_ref, o_ref, gbuf_ref, sem, *, tt, final):
    def issue(t, c):
        for k in range(TOP_K):
            d = dest_ref[0, 0, k * tt + t]
            pltpu.make_async_copy(ys_ref.at[pl.ds(d, 1), :],
                                  gbuf_ref.at[pl.ds(k * tt + t, 1), :], sem).start()
        return c

    lax.fori_loop(0, tt, issue, 0)
    hb = h2b_ref[...]
    a = _silu(_dot(hb, wsg_ref[...])) * _dot(hb, wsu_ref[...])
    shared = _dot(a.astype(BF16), wsd_ref[...])
    for k in range(TOP_K):
        pltpu.make_async_copy(ys_ref.at[pl.ds(0, tt), :], gbuf_ref.at[pl.ds(k * tt, tt), :],
                              sem).wait()
    wt = wt_ref[...]
    routed = wt[:, 0:1] * gbuf_ref[0:tt, :]
    for k in range(1, TOP_K):
        routed = routed + wt[:, k:k + 1] * gbuf_ref[k * tt:(k + 1) * tt, :]
    xo = x_ref[...] + mod_ref[0, 5:6, :] * (shared + routed)
    if final:
        ms = jnp.mean(xo * xo, axis=-1, keepdims=True)
        xo = (xo * lax.rsqrt(ms + EPS)) * fg_ref[...]
    o_ref[...] = xo


def _combine(dest_tiles, ys, x_flat, h2b, wt_tk, mod_lat, wsg, wsu, wsd, final_g,
             tokens_per_batch, final):
    t, d = x_flat.shape
    tt = TT_COMB
    ds = wsg.shape[1]
    tiles_per_batch = tokens_per_batch // tt
    body = functools.partial(_combine_body, tt=tt, final=final)
    return pl.pallas_call(
        body,
        out_shape=SDS((t, d), F32),
        grid=(t // tt,),
        in_specs=[BS((1, 1, TOP_K * tt), lambda i: (i, 0, 0), memory_space=pltpu.SMEM),
                  BS(memory_space=pl.ANY),
                  BS((tt, d), lambda i: (i, 0)),
                  BS((tt, d), lambda i: (i, 0)),
                  BS((tt, TOP_K), lambda i: (i, 0)),
                  BS((1, SUBLANES, d), lambda i: (i // tiles_per_batch, 0, 0)),
                  BS((d, ds), lambda i: (0, 0)),
                  BS((d, ds), lambda i: (0, 0)),
                  BS((ds, d), lambda i: (0, 0)),
                  BS((1, d), lambda i: (0, 0))],
        out_specs=BS((tt, d), lambda i: (i, 0)),
        scratch_shapes=[pltpu.VMEM((TOP_K * tt, d), F32), pltpu.SemaphoreType.DMA(())],
        compiler_params=_cparams("arbitrary"),
        name="moe_combine",
    )(dest_tiles, ys, x_flat, h2b, wt_tk, mod_lat, wsg, wsu, wsd, final_g)


def _tile_major(a, tt):
    k, t = a.shape
    return a.reshape(k, t // tt, tt).transpose(1, 0, 2).reshape(t // tt, 1, k * tt)


def _moe(x_flat, mod_lat, g, w_router, e_bias, w_gate, w_up, w_down, ws_gate, ws_up, ws_down,
         final_g, tri, tokens_per_batch, layer, final):
    t, d = x_flat.shape
    bm = BM_FFN
    h2s, h2b, eidx, wts, rank, cnt = _router(
        x_flat, mod_lat, g, w_router.T, e_bias.reshape(N_EXPERTS, 1), tri, tokens_per_batch)
    counts = cnt[:, 0].astype(I32)
    pcnt = (counts + bm - 1) // bm * bm
    pend = jnp.cumsum(pcnt).astype(I32)
    pstart = pend - pcnt
    n_blocks = t * TOP_K // bm + N_EXPERTS
    n_slots = n_blocks * bm
    block_row0 = jnp.arange(n_blocks, dtype=I32) * bm
    blk_e = jnp.minimum(jnp.sum((pend[None, :] <= block_row0[:, None]).astype(I32), axis=1),
                        N_EXPERTS - 1)
    n_used = pend[-1:] // bm
    dest = _plan(pstart, eidx, rank)
    xs = _dispatch(pend, pcnt, n_used, _tile_major(dest, TT_DISP), h2s, n_slots)
    ys = _ffn(blk_e, n_used, xs, w_gate, w_up, w_down, layer)
    return _combine(_tile_major(dest, TT_COMB), ys, x_flat, h2b, wts.T, mod_lat,
                    ws_gate.astype(BF16), ws_up.astype(BF16), ws_down.astype(BF16),
                    final_g.reshape(1, d), tokens_per_batch, final)


def _inproj_c_body(xp_ref, x_ref, xn_ref, mod_ref, g_ref, w_ref, cw_ref, cb_ref,
                   v_ref, g1_ref, g2_ref, *, tm, n_tiles):
    i = pl.program_id(1)
    halo = SUBLANES
    xe = jnp.concatenate([xp_ref[0], x_ref[0], xn_ref[0]], axis=0)
    h = _norm_mod(xe, g_ref[...], mod_ref[0, 1:2, :], mod_ref[0, 0:1, :])
    row = lax.broadcasted_iota(I32, (tm + 2 * halo, 1), 0)
    outside = jnp.logical_or(jnp.logical_and(i == 0, row < halo),
                             jnp.logical_and(i == n_tiles - 1, row >= tm + halo))
    hb = jnp.where(outside, 0.0, h).astype(BF16)
    width = v_ref.shape[-1]
    for part, o_ref in enumerate((v_ref, g1_ref, g2_ref)):
        cols = slice(part * width, (part + 1) * width)
        zp = _dot(hb, w_ref[:, cols])
        up = pltpu.roll(zp, 1, 0)
        dn = pltpu.roll(zp, tm + 2 * halo - 1, 0)
        z = cw_ref[0:1, cols] * up + cw_ref[1:2, cols] * zp + cw_ref[2:3, cols] * dn + cb_ref[:, cols]
        o_ref[0] = z[halo:halo + tm]


def _inproj_c(x, mod_lat, g, w_in, conv_w, conv_b):
    b, l, d = x.shape
    tm = TM_PROJ
    n_tiles = l // tm
    w3 = w_in.shape[1]
    width = w3 // 3
    r8 = tm // SUBLANES
    body = functools.partial(_inproj_c_body, tm=tm, n_tiles=n_tiles)
    out = SDS((b, l, width), F32)
    return pl.pallas_call(
        body,
        out_shape=(out, out, out),
        grid=(b, n_tiles),
        in_specs=[BS((1, SUBLANES, d), lambda bi, i: (bi, jnp.maximum(i * r8 - 1, 0), 0)),
                  BS((1, tm, d), lambda bi, i: (bi, i, 0)),
                  BS((1, SUBLANES, d), lambda bi, i: (bi, jnp.minimum((i + 1) * r8, l // SUBLANES - 1), 0)),
                  BS((1, SUBLANES, d), lambda bi, i: (bi, 0, 0)),
                  BS((1, d), lambda bi, i: (0, 0)),
                  BS((d, w3), lambda bi, i: (0, 0)),
                  BS((3, w3), lambda bi, i: (0, 0)),
                  BS((1, w3), lambda bi, i: (0, 0))],
        out_specs=(BS((1, tm, width), lambda bi, i: (bi, i, 0)),
                   BS((1, tm, width), lambda bi, i: (bi, i, 0)),
                   BS((1, tm, width), lambda bi, i: (bi, i, 0))),
        compiler_params=_cparams("arbitrary", "arbitrary"),
        name="inproj_c",
    )(x, x, x, mod_lat, g, w_in, conv_w, conv_b)


def _filter_body(f_ref, w1_ref, b1_ref, w2_ref, b2_ref, w3_ref, fr_ref, dl_ref, keep0_ref,
                 hf_ref, l1_ref, *, tp):
    i = pl.program_id(0)
    feats = f_ref[...]
    fr = fr_ref[...]
    a = jnp.sin(fr * (_dot_hp(feats, w1_ref[...]) + b1_ref[...]))
    a = jnp.sin(fr * (_dot_hp(a, w2_ref[...]) + b2_ref[...]))
    hf = _dot_hp(a, w3_ref[...])
    t01 = feats[:, 0:1]
    hf = hf * (jnp.exp(-t01 * jnp.abs(dl_ref[...])) + DECAY_SHIFT)
    row = lax.broadcasted_iota(I32, hf.shape, 0) + i * tp
    hf = jnp.where(row == 0, hf * keep0_ref[...], hf)
    hf_ref[...] = hf

    @pl.when(i == 0)
    def _():
        l1_ref[...] = jnp.zeros_like(l1_ref)

    l1_ref[...] = l1_ref[...] + jnp.sum(jnp.abs(hf), axis=0, keepdims=True)


def _filters(feats, w1, b1, w2, b2, w3, freq, delta, width):
    n, fe = feats.shape
    hid = w2.shape[0]
    fo = w3.shape[1]
    tp = 256
    lag0_keep = jnp.tile(jnp.repeat(jnp.array([1.0, 0.0], F32), width), fo // (2 * width)).reshape(1, fo)
    body = functools.partial(_filter_body, tp=tp)
    full = lambda shape: BS(shape, lambda i: (0, 0))
    return pl.pallas_call(
        body,
        out_shape=(SDS((n, fo), F32), SDS((SUBLANES, fo), F32)),
        grid=(n // tp,),
        in_specs=[BS((tp, fe), lambda i: (i, 0)), full((fe, hid)), full((1, hid)),
                  full((hid, hid)), full((1, hid)), full((hid, fo)), full((1, hid)), full((1, fo)),
                  full((1, fo))],
        out_specs=(BS((tp, fo), lambda i: (i, 0)), BS((SUBLANES, fo), lambda i: (0, 0))),
        compiler_params=_cparams("arbitrary"),
        name="hyena_filters",
    )(feats, w1, b1, w2, b2, w3, freq, delta, lag0_keep)


DFT_R = 128


def _dft_tables(n):
    r = DFT_R
    m = 2 * n
    na = n // r
    two_pi = 2.0 * np.pi
    a = np.arange(na)[None, :]
    v = np.arange(r)[:, None]
    ang1 = two_pi * ((a * v) % r) / r
    f1 = np.concatenate([np.cos(ang1), -np.sin(ang1)], axis=0)
    b = np.arange(r)[None, None, :]
    u = np.arange(r)[None, :, None]
    vv = np.arange(r)[:, None, None]
    ang2 = two_pi * ((b * (r * u + vv)) % m) / m
    gr, gi = np.cos(ang2), -np.sin(ang2)
    fwd = np.concatenate([np.concatenate([gr, -gi], axis=2),
                          np.concatenate([gi, gr], axis=2)], axis=1)
    hr, hi = np.transpose(gr, (0, 2, 1)), -np.transpose(gi, (0, 2, 1))
    inv = np.concatenate([np.concatenate([hr, -hi], axis=2),
                          np.concatenate([hi, hr], axis=2)], axis=1)
    ang3 = two_pi * ((np.arange(na)[:, None] * np.arange(r)[None, :]) % r) / r
    f3 = np.concatenate([np.cos(ang3), -np.sin(ang3)], axis=1) / m
    cast = lambda t: jnp.asarray(t.astype(np.float32)).astype(BF16)
    return cast(f1), cast(fwd), cast(inv), cast(f3)


def _dft_s1_body(y_ref, f1_ref, ar_ref, ai_ref):
    f1 = f1_ref[...]
    for j in range(SUBLANES):
        res = _dot(f1, y_ref[:, j, :].astype(BF16))
        ar_ref[:, j, :] = res[:DFT_R]
        ai_ref[:, j, :] = res[DFT_R:]


def _dft_s1(y4, f1):
    nb, na, r, c = y4.shape
    ct = min(c, 1024)
    out = SDS((nb, r, r, c), F32)
    return pl.pallas_call(
        _dft_s1_body,
        out_shape=(out, out),
        grid=(nb, c // ct, r // SUBLANES),
        in_specs=[BS((None, na, SUBLANES, ct), lambda n, cc, j: (n, 0, j, cc)),
                  BS((2 * r, na), lambda n, cc, j: (0, 0))],
        out_specs=(BS((None, r, SUBLANES, ct), lambda n, cc, j: (n, 0, j, cc)),
                   BS((None, r, SUBLANES, ct), lambda n, cc, j: (n, 0, j, cc))),
        compiler_params=_cparams("arbitrary", "arbitrary", "arbitrary"),
        name="dft_stage1",
    )(y4, f1)


def _filter_spec_body(arf_ref, aif_ref, arb_ref, aib_ref, g_ref, l1f_ref, l1b_ref, kr_ref, ki_ref):
    g = g_ref[...]
    yf = _dot(g, jnp.concatenate([arf_ref[...], aif_ref[...]], axis=0).astype(BF16))
    yb = _dot(g, jnp.concatenate([arb_ref[...], aib_ref[...]], axis=0).astype(BF16))
    inv = 1.0 / (l1f_ref[0:1, :] + l1b_ref[0:1, :])
    kr_ref[...] = (yf[:DFT_R] + yb[:DFT_R]) * inv
    ki_ref[...] = (yf[DFT_R:] - yb[DFT_R:]) * inv


def _filter_spectrum(ar, ai, fwd, l1, width):
    r = DFT_R
    a_spec = lambda d: BS((None, None, r, width), lambda v, o: (0, v, 0, 2 * o + d))
    l_spec = lambda d: BS((SUBLANES, width), lambda v, o: (0, 2 * o + d))
    out = SDS((r, r, HYENA_ORDER * width), F32)
    return pl.pallas_call(
        _filter_spec_body,
        out_shape=(out, out),
        grid=(r, HYENA_ORDER),
        in_specs=[a_spec(0), a_spec(0), a_spec(1), a_spec(1),
                  BS((None, 2 * r, 2 * r), lambda v, o: (v, 0, 0)), l_spec(0), l_spec(1)],
        out_specs=(BS((None, r, width), lambda v, o: (v, 0, o)),
                   BS((None, r, width), lambda v, o: (v, 0, o))),
        compiler_params=_cparams("arbitrary", "arbitrary"),
        name="hyena_filter_spectrum",
    )(ar, ai, ar, ai, fwd, l1, l1)


def _conv_mid_body(ar_ref, ai_ref, g_ref, h_ref, kr_ref, ki_ref, qr_ref, qi_ref):
    y = _dot(g_ref[...], jnp.concatenate([ar_ref[...], ai_ref[...]], axis=0).astype(BF16))
    yr, yi = y[:DFT_R], y[DFT_R:]
    kr, ki = kr_ref[...], ki_ref[...]
    p = jnp.concatenate([yr * kr - yi * ki, yr * ki + yi * kr], axis=0).astype(BF16)
    q = _dot(h_ref[...], p)
    qr_ref[...] = q[:DFT_R]
    qi_ref[...] = q[DFT_R:]


def _conv_mid(ar, ai, fwd, inv, kr, ki, order):
    nb, r, _, c = ar.shape
    a_spec = BS((None, None, r, c), lambda n, v: (n, v, 0, 0))
    m_spec = BS((None, 2 * r, 2 * r), lambda n, v: (v, 0, 0))
    k_spec = BS((None, r, c), lambda n, v: (v, 0, order))
    out = SDS((nb, r, r, c), F32)
    return pl.pallas_call(
        _conv_mid_body,
        out_shape=(out, out),
        grid=(nb, r),
        in_specs=[a_spec, a_spec, m_spec, m_spec, k_spec, k_spec],
        out_specs=(a_spec, a_spec),
        compiler_params=_cparams("arbitrary", "arbitrary"),
        name="hyena_spectral_product",
    )(ar, ai, fwd, inv, kr, ki)


def _idft_gate_body(qr_ref, qi_ref, f3_ref, y_ref, gate_ref, fb_ref, o_ref):
    f3 = f3_ref[...]
    fb = fb_ref[...]
    for j in range(SUBLANES):
        q = jnp.concatenate([qr_ref[:, j, :], qi_ref[:, j, :]], axis=0).astype(BF16)
        conv = _dot(f3, q)
        o_ref[:, j, :] = gate_ref[:, j, :] * (conv + fb * y_ref[:, j, :])


def _idft_gate(qr, qi, f3, y4, gate4, fbias):
    nb, na, r, c = y4.shape
    q_spec = BS((None, r, SUBLANES, c), lambda n, j: (n, 0, j, 0))
    y_spec = BS((None, na, SUBLANES, c), lambda n, j: (n, 0, j, 0))
    return pl.pallas_call(
        _idft_gate_body,
        out_shape=SDS((nb, na, r, c), F32),
        grid=(nb, r // SUBLANES),
        in_specs=[q_spec, q_spec, BS((na, 2 * r), lambda n, j: (0, 0)), y_spec, y_spec,
                  BS((1, c), lambda n, j: (0, 0))],
        out_specs=y_spec,
        compiler_params=_cparams("arbitrary", "arbitrary"),
        name="hyena_idft_gate",
    )(qr, qi, f3, y4, gate4, fbias)


def _outproj_body(y_ref, w_ref, x_ref, mod_ref, o_ref):
    o_ref[0] = x_ref[0] + mod_ref[0, 2:3, :] * _dot(y_ref[0].astype(BF16), w_ref[...])


def _outproj(y, w_out, x, mod_lat):
    b, l, d = x.shape
    tm = TM_PROJ
    wdt = y.shape[-1]
    return pl.pallas_call(
        _outproj_body,
        out_shape=SDS((b, l, d), F32),
        grid=(b, l // tm),
        in_specs=[BS((1, tm, wdt), lambda bi, i: (bi, i, 0)),
                  BS((wdt, d), lambda bi, i: (0, 0)),
                  BS((1, tm, d), lambda bi, i: (bi, i, 0)),
                  BS((1, SUBLANES, d), lambda bi, i: (bi, 0, 0))],
        out_specs=BS((1, tm, d), lambda bi, i: (bi, i, 0)),
        compiler_params=_cparams("arbitrary", "arbitrary"),
        name="outproj_c",
    )(y, w_out, x, mod_lat)


def _hyena(x, mod_lat, g, w_in, conv_w, conv_b, w1, b1, w2, b2, w3, freq, delta, f_bias, w_out):
    b, n, d = x.shape
    width = w_out.shape[0]
    r = DFT_R
    na = n // r
    f1, fwd, inv, f3 = _dft_tables(n)
    v, gate1, gate2 = _inproj_c(x, mod_lat, g, w_in.astype(BF16), conv_w, conv_b.reshape(1, -1))

    t = jnp.arange(n, dtype=F32)
    t01 = t / max(n - 1, 1)
    bands = jnp.linspace(1e-4, FILT_BANDS - 1, FILT_BANDS, dtype=F32)
    ang = (2.0 * math.pi / n) * t[:, None] * bands[None, :]
    feats = jnp.concatenate([t01[:, None], jnp.cos(ang), jnp.sin(ang)], axis=-1)
    fe = feats.shape[1]
    feats = jnp.pad(feats, ((0, 0), (0, LANES - fe)))
    w1p = jnp.pad(w1, ((0, LANES - fe), (0, 0)))
    hf, l1 = _filters(feats, w1p, b1.reshape(1, -1), w2, b2.reshape(1, -1), w3,
                      freq.reshape(1, -1), delta.reshape(1, -1), width)
    far, fai = _dft_s1(hf.reshape(1, na, r, hf.shape[1]), f1)
    kr, ki = _filter_spectrum(far, fai, fwd, l1, width)

    y4 = v.reshape(b, na, r, width)
    for o, gate in enumerate((gate1, gate2)):
        ar, ai = _dft_s1(y4, f1)
        qr, qi = _conv_mid(ar, ai, fwd, inv, kr, ki, o)
        y4 = _idft_gate(qr, qi, f3, y4, gate.reshape(b, na, r, width), f_bias[o].reshape(1, width))
    return _outproj(y4.reshape(b, n, width), w_out.astype(BF16), x, mod_lat)


def _rope_tables(seq_len):
    rows = seq_len // GRID_W
    row = jnp.repeat(jnp.arange(rows, dtype=F32), GRID_W)
    col = jnp.tile(jnp.arange(GRID_W, dtype=F32), rows)
    inv = jnp.power(ROPE_BASE, -jnp.arange(ROPE_FREQS, dtype=F32) / ROPE_FREQS)
    ar, ac = row[:, None] * inv, col[:, None] * inv
    cos_h = jnp.concatenate([jnp.cos(ar), jnp.cos(ar), jnp.cos(ac), jnp.cos(ac)], axis=1)
    sin_h = jnp.concatenate([-jnp.sin(ar), jnp.sin(ar), -jnp.sin(ac), jnp.sin(ac)], axis=1)
    reps = LANES // HEAD_DIM
    return jnp.tile(cos_h, (1, reps)), jnp.tile(sin_h, (1, reps))


def _rotate_partner_columns(w):
    ncol = w.shape[1]
    lane = np.arange(ncol)
    partner = np.where((lane % (2 * ROPE_FREQS)) < ROPE_FREQS, lane + ROPE_FREQS, lane - ROPE_FREQS)
    return w[:, partner]


def kernel(x, c, ctx, c_ctx, w_mod, b_mod, norm_g, w_in_ab, sink, w_spatial, b_spatial, w_out_ab,
           w_in_c, conv_w, conv_b, filt_w1, filt_b1, filt_w2, filt_b2, filt_w3, filt_freq,
           filt_delta, filt_bias, w_out_c, w_router, e_bias, w_gate, w_up, w_down, ws_gate,
           ws_up, ws_down, final_g):
    b, l, d = x.shape
    depth = w_mod.shape[0]
    assert depth == 2 and b + 1 <= SUBLANES

    cc = jnp.zeros((SUBLANES, d), F32).at[:b].set(c).at[b].set(c_ctx)
    m_all = _mod_vectors(cc, w_mod, b_mod)

    def mod_rows(layer, row0, nrow):
        m = m_all[layer, row0:row0 + nrow].reshape(nrow, 6, d)
        return jnp.pad(m, ((0, 0), (0, SUBLANES - 6), (0, 0)))

    tri = jnp.triu(jnp.ones((TM_ROUTE, TM_ROUTE), F32), k=1).astype(BF16)

    mod_lat = mod_rows(0, 0, b)
    mod_ctx = mod_rows(0, b, 1)[0]
    w_in = w_in_ab[0]
    qk = ATTN_WIDTH + KV_WIDTH
    w_cat = jnp.concatenate([w_in, _rotate_partner_columns(w_in[:, :qk])], axis=1).astype(BF16)
    cos_t, sin_t = _rope_tables(l)
    group_avg = jnp.kron(jnp.eye(N_SG_GROUPS, dtype=F32),
                         jnp.full((SG_GROUP_DIM, SG_GROUP_DIM), 1.0 / SG_GROUP_DIM, F32)).astype(BF16)
    kc, vc = _ctx_kv(ctx, mod_ctx, norm_g[0, 0].reshape(1, d),
                     w_in[:, ATTN_WIDTH:ATTN_WIDTH + 2 * KV_WIDTH].astype(BF16))
    q, k, v, ug, vn = _inproj_ab(x, mod_lat, norm_g[0, 0].reshape(1, d), w_cat, cos_t, sin_t, group_avg)
    b_full = jnp.repeat(b_spatial[0].T, SG_GROUP_DIM, axis=1)
    x1 = _mixer(sink[0], q, k, v, kc, vc, ug, vn, w_spatial[0].astype(BF16), b_full,
                w_out_ab[0].astype(BF16), x, mod_lat)
    x2 = _moe(x1.reshape(b * l, d), mod_lat, norm_g[0, 1].reshape(1, d), w_router[0], e_bias[0],
              w_gate, w_up, w_down, ws_gate[0], ws_up[0], ws_down[0], final_g, tri, l,
              layer=0, final=False).reshape(b, l, d)

    mod_lat = mod_rows(1, 0, b)
    x3 = _hyena(x2, mod_lat, norm_g[1, 0].reshape(1, d), w_in_c[0], conv_w[0], conv_b[0],
                filt_w1[0], filt_b1[0], filt_w2[0], filt_b2[0], filt_w3[0], filt_freq[0],
                filt_delta[0], filt_bias[0], w_out_c[0])
    out = _moe(x3.reshape(b * l, d), mod_lat, norm_g[1, 1].reshape(1, d), w_router[1], e_bias[1],
               w_gate, w_up, w_down, ws_gate[1], ws_up[1], ws_down[1], final_g, tri, l,
               layer=1, final=True)
    return out.reshape(b, l, d)
```

```python
import functools
import math

import numpy as np
import jax
import jax.numpy as jnp
from jax import lax
from jax.experimental import pallas as pl
from jax.experimental.pallas import tpu as pltpu

F32 = jnp.float32
BF16 = jnp.bfloat16
I32 = jnp.int32
HIGHEST = lax.Precision.HIGHEST
SDS = jax.ShapeDtypeStruct
BS = pl.BlockSpec

EPS = 1e-6
NEG = -1e30

GRID_W = 64
N_Q_HEADS = 8
N_KV_HEADS = 2
HEAD_DIM = 64
ATTN_WIDTH = N_Q_HEADS * HEAD_DIM
KV_WIDTH = N_KV_HEADS * HEAD_DIM
WINDOW = 128
BLOCK = 128
ROPE_BASE = 10000.0
ROPE_FREQS = HEAD_DIM // 4
N_SG_GROUPS = 8
SG_GROUP_DIM = 64
SG_WIDTH = N_SG_GROUPS * SG_GROUP_DIM
HYENA_ORDER = 2
FILT_BANDS = 16
DECAY_SHIFT = 0.05
N_EXPERTS = 64
TOP_K = 8
N_GROUPS = 8
TOPK_GROUPS = 4
ROUTED_SCALE = 2.5

LANES = 128
SUBLANES = 8
VMEM_LIMIT = 56 * 1024 * 1024

TM_PROJ = 512
TQ_MIX = 256
MOE_TILE = 256
BM_FFN = 256
RUN_ALIGN = 16
SLOT_CHUNK = 512
LOCAL_SLOTS = -(-(TOP_K * MOE_TILE + N_EXPERTS * (RUN_ALIGN - 1)) // SLOT_CHUNK) * SLOT_CHUNK
RUN_TABLE_WIDTH = 4 * N_EXPERTS


def _cparams(*sem):
    return pltpu.CompilerParams(dimension_semantics=sem, vmem_limit_bytes=VMEM_LIMIT)


def _dot(a, b):
    return jnp.dot(a, b, preferred_element_type=F32)


def _dot_nt(a, b):
    return lax.dot_general(a, b, (((1,), (1,)), ((), ())), preferred_element_type=F32)


def _dot_hp(a, b):
    return jnp.dot(a, b, preferred_element_type=F32, precision=HIGHEST)


def _norm_mod(x, g, sc, sh):
    ms = jnp.mean(x * x, axis=-1, keepdims=True)
    y = x * lax.rsqrt(ms + EPS)
    return (y * g) * (1.0 + sc) + sh


def _gelu_tanh(x):
    c = math.sqrt(2.0 / math.pi)
    return 0.5 * x * (1.0 + jnp.tanh(c * (x + 0.044715 * (x * x * x))))


def _silu(x):
    return x * jax.nn.sigmoid(x)


def _mod_body(c_ref, w_ref, b_ref, o_ref):
    o_ref[0] = _dot_hp(_silu(c_ref[...]), w_ref[0]) + b_ref[0]


def _mod_vectors(cc, w_mod, b_mod):
    depth, d, n = w_mod.shape
    tn = 1536
    return pl.pallas_call(
        _mod_body,
        out_shape=SDS((depth, SUBLANES, n), F32),
        grid=(depth, n // tn),
        in_specs=[BS((SUBLANES, d), lambda l, j: (0, 0)),
                  BS((1, d, tn), lambda l, j: (l, 0, j)),
                  BS((1, 1, tn), lambda l, j: (l, 0, j))],
        out_specs=BS((1, SUBLANES, tn), lambda l, j: (l, 0, j)),
        compiler_params=_cparams("arbitrary", "arbitrary"),
        name="mod_vectors",
    )(cc, w_mod, b_mod.reshape(depth, 1, n))


def _ctx_kv_body(ctx_ref, mod_ref, g_ref, w_ref, kc_ref, vc_ref):
    h = _norm_mod(ctx_ref[0], g_ref[...], mod_ref[1:2, :], mod_ref[0:1, :])
    z = _dot(h.astype(BF16), w_ref[...])
    kc_ref[0] = z[:, :KV_WIDTH].astype(BF16)
    vc_ref[0] = z[:, KV_WIDTH:].astype(BF16)


def _ctx_kv(ctx, mod_ctx, g, w_kv):
    b, c, d = ctx.shape
    return pl.pallas_call(
        _ctx_kv_body,
        out_shape=(SDS((b, c, KV_WIDTH), BF16), SDS((b, c, KV_WIDTH), BF16)),
        grid=(b,),
        in_specs=[BS((1, c, d), lambda i: (i, 0, 0)),
                  BS((SUBLANES, d), lambda i: (0, 0)),
                  BS((1, d), lambda i: (0, 0)),
                  BS((d, 2 * KV_WIDTH), lambda i: (0, 0))],
        out_specs=(BS((1, c, KV_WIDTH), lambda i: (i, 0, 0)),
                   BS((1, c, KV_WIDTH), lambda i: (i, 0, 0))),
        compiler_params=_cparams("arbitrary"),
        name="ctx_kv",
    )(ctx, mod_ctx, g, w_kv)


def _inproj_ab_body(x_ref, mod_ref, g_ref, w_ref, cos_ref, sin_ref, avg_ref,
                    q_ref, k_ref, v_ref, ug_ref, vn_ref):
    h = _norm_mod(x_ref[0], g_ref[...], mod_ref[0, 1:2, :], mod_ref[0, 0:1, :]).astype(BF16)
    cs = cos_ref[...]
    sn = sin_ref[...]
    rot0 = ATTN_WIDTH + 2 * KV_WIDTH + 2 * SG_WIDTH
    scale = HEAD_DIM ** -0.5
    for j in range(ATTN_WIDTH // LANES):
        z = _dot(h, w_ref[:, j * LANES:(j + 1) * LANES])
        zr = _dot(h, w_ref[:, rot0 + j * LANES:rot0 + (j + 1) * LANES])
        q_ref[0, :, j * LANES:(j + 1) * LANES] = ((z * cs + zr * sn) * scale).astype(BF16)
    zk = _dot(h, w_ref[:, ATTN_WIDTH:ATTN_WIDTH + KV_WIDTH])
    zkr = _dot(h, w_ref[:, rot0 + ATTN_WIDTH:rot0 + ATTN_WIDTH + KV_WIDTH])
    k_ref[0] = (zk * cs + zkr * sn).astype(BF16)
    v_ref[0] = _dot(h, w_ref[:, ATTN_WIDTH + KV_WIDTH:ATTN_WIDTH + 2 * KV_WIDTH]).astype(BF16)
    u0 = ATTN_WIDTH + 2 * KV_WIDTH
    ug_ref[0] = _gelu_tanh(_dot(h, w_ref[:, u0:u0 + SG_WIDTH]))
    vf = _gelu_tanh(_dot(h, w_ref[:, u0 + SG_WIDTH:u0 + 2 * SG_WIDTH]))
    avg = avg_ref[...]

    def gmean(t):
        hi = t.astype(BF16)
        lo = (t - hi.astype(F32)).astype(BF16)
        return _dot(hi, avg) + _dot(lo, avg)

    vc = vf - gmean(vf)
    vn_ref[0] = (vc * lax.rsqrt(gmean(vc * vc) + EPS)).astype(BF16)


def _inproj_ab(x, mod_lat, g, w_cat, cos_t, sin_t, avg):
    b, l, d = x.shape
    tm = TM_PROJ
    ncol = w_cat.shape[1]
    return pl.pallas_call(
        _inproj_ab_body,
        out_shape=(SDS((b, l, ATTN_WIDTH), BF16), SDS((b, l, KV_WIDTH), BF16),
                   SDS((b, l, KV_WIDTH), BF16), SDS((b, l, SG_WIDTH), F32),
                   SDS((b, l, SG_WIDTH), BF16)),
        grid=(b, l // tm),
        in_specs=[BS((1, tm, d), lambda bi, i: (bi, i, 0)),
                  BS((1, SUBLANES, d), lambda bi, i: (bi, 0, 0)),
                  BS((1, d), lambda bi, i: (0, 0)),
                  BS((d, ncol), lambda bi, i: (0, 0)),
                  BS((tm, LANES), lambda bi, i: (i, 0)),
                  BS((tm, LANES), lambda bi, i: (i, 0)),
                  BS((SG_WIDTH, SG_WIDTH), lambda bi, i: (0, 0))],
        out_specs=(BS((1, tm, ATTN_WIDTH), lambda bi, i: (bi, i, 0)),
                   BS((1, tm, KV_WIDTH), lambda bi, i: (bi, i, 0)),
                   BS((1, tm, KV_WIDTH), lambda bi, i: (bi, i, 0)),
                   BS((1, tm, SG_WIDTH), lambda bi, i: (bi, i, 0)),
                   BS((1, tm, SG_WIDTH), lambda bi, i: (bi, i, 0))),
        compiler_params=_cparams("arbitrary", "arbitrary"),
        name="inproj_ab",
    )(x, mod_lat, g, w_cat, cos_t, sin_t, avg)


def _mixer_body(sink_ref, q_ref, kp_ref, kcur_ref, kn_ref, vp_ref, vcur_ref, vn_ref,
                kc_ref, vc_ref, ug_ref, vnorm_ref, ws_ref, bs_ref, wout_ref, x_ref, mod_ref,
                o_ref, cat_ref, *, seq_len, sub_blocks):
    i = pl.program_id(1)
    kk = jnp.concatenate([kp_ref[0], kcur_ref[0], kn_ref[0]], axis=0)
    vv = jnp.concatenate([vp_ref[0], vcur_ref[0], vn_ref[0]], axis=0)
    kc = kc_ref[0]
    vc = vc_ref[0]
    span = 3 * BLOCK
    ii = lax.broadcasted_iota(I32, (BLOCK, span), 0)
    jj = lax.broadcasted_iota(I32, (BLOCK, span), 1)
    dd = jj - ii
    in_window = jnp.where(dd >= 0, jnp.where(dd <= 2 * WINDOW, 1, 0), 0)
    group = N_Q_HEADS // N_KV_HEADS
    for r in range(sub_blocks):
        rows = slice(r * BLOCK, (r + 1) * BLOCK)
        kpos = (i * sub_blocks + r - 1) * BLOCK + jj
        in_seq = jnp.where(kpos >= 0, jnp.where(kpos < seq_len, 1, 0), 0)
        bias = jnp.where(in_window * in_seq > 0, 0.0, NEG)
        qb = q_ref[0, rows, :]
        kl = kk[r * BLOCK:r * BLOCK + span]
        vl = vv[r * BLOCK:r * BLOCK + span]
        for hq in range(N_Q_HEADS):
            hk = hq // group
            ks = slice(hk * HEAD_DIM, (hk + 1) * HEAD_DIM)
            qh = qb[:, hq * HEAD_DIM:(hq + 1) * HEAD_DIM]
            s_loc = _dot_nt(qh, kl[:, ks]) + bias
            s_ctx = _dot_nt(qh, kc[:, ks])
            sk = sink_ref[hq]
            m = jnp.maximum(jnp.maximum(jnp.max(s_loc, axis=-1, keepdims=True),
                                        jnp.max(s_ctx, axis=-1, keepdims=True)), sk)
            p_loc = jnp.exp(s_loc - m)
            p_ctx = jnp.exp(s_ctx - m)
            den = (jnp.sum(p_loc, axis=-1, keepdims=True) + jnp.sum(p_ctx, axis=-1, keepdims=True)
                   + jnp.exp(sk - m))
            o = _dot(p_loc.astype(BF16), vl[:, ks]) + _dot(p_ctx.astype(BF16), vc[:, ks])
            cat_ref[rows, hq * HEAD_DIM:(hq + 1) * HEAD_DIM] = (o / den).astype(BF16)
        vnb = vnorm_ref[0, rows, :]
        ugb = ug_ref[0, rows, :]
        for g in range(N_SG_GROUPS):
            gs = slice(g * SG_GROUP_DIM, (g + 1) * SG_GROUP_DIM)
            sg = _dot(ws_ref[g], vnb[:, gs]) + bs_ref[:, gs]
            cat_ref[rows, ATTN_WIDTH + g * SG_GROUP_DIM:ATTN_WIDTH + (g + 1) * SG_GROUP_DIM] = (
                ugb[:, gs] * sg).astype(BF16)
    y = _dot(cat_ref[...], wout_ref[...])
    o_ref[0] = x_ref[0] + mod_ref[0, 2:3, :] * y


def _mixer(sink, q, k, v, kc, vc, ug, vn, w_s, b_full, w_out, x, mod_lat):
    b, l, d = x.shape
    tq = TQ_MIX
    r = tq // BLOCK
    nb = l // BLOCK
    c = kc.shape[1]
    prev_map = lambda bi, i: (bi, jnp.maximum(i * r - 1, 0), 0)
    next_map = lambda bi, i: (bi, jnp.minimum((i + 1) * r, nb - 1), 0)
    cur_map = lambda bi, i: (bi, i, 0)
    body = functools.partial(_mixer_body, seq_len=l, sub_blocks=r)
    return pl.pallas_call(
        body,
        out_shape=SDS((b, l, d), F32),
        grid=(b, l // tq),
        in_specs=[BS(memory_space=pltpu.SMEM),
                  BS((1, tq, ATTN_WIDTH), cur_map),
                  BS((1, BLOCK, KV_WIDTH), prev_map), BS((1, tq, KV_WIDTH), cur_map),
                  BS((1, BLOCK, KV_WIDTH), next_map),
                  BS((1, BLOCK, KV_WIDTH), prev_map), BS((1, tq, KV_WIDTH), cur_map),
                  BS((1, BLOCK, KV_WIDTH), next_map),
                  BS((1, c, KV_WIDTH), lambda bi, i: (bi, 0, 0)),
                  BS((1, c, KV_WIDTH), lambda bi, i: (bi, 0, 0)),
                  BS((1, tq, SG_WIDTH), cur_map), BS((1, tq, SG_WIDTH), cur_map),
                  BS((N_SG_GROUPS, BLOCK, BLOCK), lambda bi, i: (0, 0, 0)),
                  BS((BLOCK, SG_WIDTH), lambda bi, i: (0, 0)),
                  BS((d, d), lambda bi, i: (0, 0)),
                  BS((1, tq, d), cur_map),
                  BS((1, SUBLANES, d), lambda bi, i: (bi, 0, 0))],
        out_specs=BS((1, tq, d), cur_map),
        scratch_shapes=[pltpu.VMEM((tq, d), BF16)],
        compiler_params=_cparams("arbitrary", "arbitrary"),
        name="mixer_ab",
    )(sink, q, k, k, k, v, v, v, kc, vc, ug, vn, w_s, b_full, w_out, x, mod_lat)


def _router_body(x_ref, mod_ref, g_ref, wr_ref, eb_ref, tri_ref,
                 h2b_ref, lslot_ref, wts_ref, loff_ref, gbase_ref, runlen_ref, cnt_ref, carry_ref,
                 *, tm):
    i = pl.program_id(0)

    @pl.when(i == 0)
    def _():
        carry_ref[...] = jnp.zeros_like(carry_ref)

    h2 = _norm_mod(x_ref[...], g_ref[...], mod_ref[0, 4:5, :], mod_ref[0, 3:4, :])
    h2b_ref[...] = h2.astype(BF16)

    logits = lax.dot_general(wr_ref[...], h2, (((1,), (1,)), ((), ())),
                             preferred_element_type=F32, precision=HIGHEST)
    scores = jax.nn.sigmoid(logits)
    per_group = N_EXPERTS // N_GROUPS
    shape3 = (N_GROUPS, per_group, tm)
    s3 = scores.reshape(shape3)
    b3 = (scores + eb_ref[...]).reshape(shape3)
    sub = lax.broadcasted_iota(I32, shape3, 1)
    eid = lax.broadcasted_iota(I32, shape3, 0) * per_group + sub

    m1 = jnp.max(b3, axis=1, keepdims=True)
    i1 = jnp.min(jnp.where(b3 == m1, sub, per_group), axis=1, keepdims=True)
    m2 = jnp.max(jnp.where(sub == i1, -jnp.inf, b3), axis=1, keepdims=True)
    gs = m1 + m2
    keep = []
    for g in range(N_GROUPS):
        beaten = jnp.zeros((1, tm), I32)
        for g2 in range(N_GROUPS):
            if g2 == g:
                continue
            wins = (gs[g2] >= gs[g]) if g2 < g else (gs[g2] > gs[g])
            beaten = beaten + jnp.where(wins, 1, 0)
        keep.append(jnp.where(beaten < TOPK_GROUPS, 1, 0)[None])
    keep3 = jnp.concatenate(keep, axis=0)
    val = jnp.where(keep3 > 0, b3, -jnp.inf)

    def red(fn, a):
        return fn(fn(a, axis=0, keepdims=True), axis=1, keepdims=True)

    idxs, ws = [], []
    member = jnp.zeros(shape3, F32)
    for _ in range(TOP_K):
        m = red(jnp.max, val)
        idx = red(jnp.min, jnp.where(val == m, eid, N_EXPERTS))
        hit = eid == idx
        ws.append(red(jnp.sum, jnp.where(hit, s3, 0.0)))
        val = jnp.where(hit, -jnp.inf, val)
        member = member + jnp.where(hit, 1.0, 0.0)
        idxs.append(idx)
    wsum = ws[0]
    for w in ws[1:]:
        wsum = wsum + w

    member2 = member.reshape(N_EXPERTS, tm)
    cnt = jnp.sum(member2, axis=1, keepdims=True)
    runlen = jnp.floor((cnt + (RUN_ALIGN - 1)) * (1.0 / RUN_ALIGN)) * RUN_ALIGN
    runlen_b = jnp.broadcast_to(runlen, (N_EXPERTS, LANES))
    e_row = lax.broadcasted_iota(I32, (N_EXPERTS, N_EXPERTS), 0)
    e_col = lax.broadcasted_iota(I32, (N_EXPERTS, N_EXPERTS), 1)
    earlier = jnp.where(e_col < e_row, 1.0, 0.0).astype(BF16)
    loff = _dot(earlier, runlen_b.astype(BF16))
    slot = _dot(member2.astype(BF16), tri_ref[...]) + loff[:, 0:1]
    slot3 = slot.reshape(shape3)
    for k in range(TOP_K):
        wts_ref[k:k + 1, :] = (ws[k] / wsum * ROUTED_SCALE).reshape(1, tm)
        sk = red(jnp.sum, jnp.where(eid == idxs[k], slot3, 0.0))
        lslot_ref[k:k + 1, :] = sk.reshape(1, tm).astype(I32)
    loff_ref[0] = loff
    gbase_ref[0] = carry_ref[...]
    runlen_ref[0] = runlen_b
    total = carry_ref[...] + runlen_b
    carry_ref[...] = total
    cnt_ref[...] = total


def _router(x_flat, mod_lat, g, wr_t, e_bias, tri, tokens_per_batch):
    t, d = x_flat.shape
    tm = MOE_TILE
    n_tiles = t // tm
    tiles_per_batch = tokens_per_batch // tm
    body = functools.partial(_router_body, tm=tm)
    table = SDS((n_tiles, N_EXPERTS, LANES), F32)
    table_spec = BS((1, N_EXPERTS, LANES), lambda i: (i, 0, 0))
    return pl.pallas_call(
        body,
        out_shape=(SDS((t, d), BF16), SDS((TOP_K, t), I32), SDS((TOP_K, t), F32),
                   table, table, table, SDS((N_EXPERTS, LANES), F32)),
        grid=(n_tiles,),
        in_specs=[BS((tm, d), lambda i: (i, 0)),
                  BS((1, SUBLANES, d), lambda i: (i // tiles_per_batch, 0, 0)),
                  BS((1, d), lambda i: (0, 0)),
                  BS((N_EXPERTS, d), lambda i: (0, 0)),
                  BS((N_EXPERTS, 1), lambda i: (0, 0)),
                  BS((tm, tm), lambda i: (0, 0))],
        out_specs=(BS((tm, d), lambda i: (i, 0)),
                   BS((TOP_K, tm), lambda i: (0, i)),
                   BS((TOP_K, tm), lambda i: (0, i)),
                   table_spec, table_spec, table_spec,
                   BS((N_EXPERTS, LANES), lambda i: (0, 0))),
        scratch_shapes=[pltpu.VMEM((N_EXPERTS, LANES), F32)],
        compiler_params=_cparams("arbitrary"),
        name="moe_router",
    )(x_flat, mod_lat, g, wr_t, e_bias, tri)


def _run_table_entry(tab_ref, field, e):
    return tab_ref[0, 0, field * N_EXPERTS + e]


def _for_each_run_chunk(tab_ref, fn):
    def per_expert(e, carry):
        loc0 = _run_table_entry(tab_ref, 0, e)
        slot0 = _run_table_entry(tab_ref, 1, e)

        def per_chunk(c, carry2):
            fn(pl.multiple_of(loc0 + c * RUN_ALIGN, RUN_ALIGN),
               pl.multiple_of(slot0 + c * RUN_ALIGN, RUN_ALIGN))
            return carry2

        return lax.fori_loop(0, _run_table_entry(tab_ref, 2, e), per_chunk, carry)

    lax.fori_loop(0, N_EXPERTS, per_expert, 0)


def _dispatch_body(pend_ref, pcnt_ref, nu_ref, tab_ref, lslot_ref, h_ref, xs_ref, loc_ref, zbuf_ref,
                   sem, zsem, *, tl, bm, n_blocks):
    i = pl.program_id(0)

    def zero_copy(row0):
        return pltpu.make_async_copy(
            zbuf_ref, xs_ref.at[pl.ds(pl.multiple_of(row0, RUN_ALIGN), bm), :], zsem)

    @pl.when(i == 0)
    def _():
        zbuf_ref[...] = jnp.zeros_like(zbuf_ref)

        def start(e, c):
            @pl.when(pcnt_ref[e] > 0)
            def _():
                zero_copy(pend_ref[e] - bm).start()
            return c

        def wait(e, c):
            @pl.when(pcnt_ref[e] > 0)
            def _():
                zero_copy(pend_ref[e] - bm).wait()
            return c

        def start_tail(j, c):
            zero_copy(j * bm).start()
            return c

        def wait_tail(j, c):
            zero_copy(j * bm).wait()
            return c

        lax.fori_loop(0, N_EXPERTS, start, 0)
        lax.fori_loop(nu_ref[0], n_blocks, start_tail, 0)
        lax.fori_loop(0, N_EXPERTS, wait, 0)
        lax.fori_loop(nu_ref[0], n_blocks, wait_tail, 0)

    rows_used = tab_ref[0, 0, 3 * N_EXPERTS]
    ls = lslot_ref[...]
    h = h_ref[...]
    for c in range(LOCAL_SLOTS // SLOT_CHUNK):
        @pl.when(c * SLOT_CHUNK < rows_used)
        def _():
            row = lax.broadcasted_iota(I32, (SLOT_CHUNK, tl), 0) + c * SLOT_CHUNK
            onehot = jnp.zeros((SLOT_CHUNK, tl), F32)
            for k in range(TOP_K):
                onehot = jnp.where(row == ls[k:k + 1, :], 1.0, onehot)
            loc_ref[c * SLOT_CHUNK:(c + 1) * SLOT_CHUNK, :] = _dot(onehot.astype(BF16), h).astype(BF16)

    def run_copy(loc0, slot0):
        return pltpu.make_async_copy(loc_ref.at[pl.ds(loc0, RUN_ALIGN), :],
                                     xs_ref.at[pl.ds(slot0, RUN_ALIGN), :], sem)

    _for_each_run_chunk(tab_ref, lambda loc0, slot0: run_copy(loc0, slot0).start())
    _for_each_run_chunk(tab_ref, lambda loc0, slot0: run_copy(loc0, slot0).wait())


def _dispatch(pend, pcnt, n_used, run_table, lslot, h2b, n_slots):
    t, d = h2b.shape
    tl = MOE_TILE
    body = functools.partial(_dispatch_body, tl=tl, bm=BM_FFN, n_blocks=n_slots // BM_FFN)
    grid_spec = pltpu.PrefetchScalarGridSpec(
        num_scalar_prefetch=3,
        grid=(t // tl,),
        in_specs=[BS((1, 1, RUN_TABLE_WIDTH), lambda i, *_: (i, 0, 0), memory_space=pltpu.SMEM),
                  BS((TOP_K, tl), lambda i, *_: (0, i)),
                  BS((tl, d), lambda i, *_: (i, 0))],
        out_specs=BS(memory_space=pl.ANY),
        scratch_shapes=[pltpu.VMEM((LOCAL_SLOTS, d), BF16), pltpu.VMEM((BM_FFN, d), BF16),
                        pltpu.SemaphoreType.DMA(()), pltpu.SemaphoreType.DMA(())],
    )
    return pl.pallas_call(
        body,
        out_shape=SDS((n_slots, d), BF16),
        grid_spec=grid_spec,
        compiler_params=_cparams("arbitrary"),
        name="moe_dispatch",
    )(pend, pcnt, n_used, run_table, lslot, h2b)


def _ffn_body(be_ref, nu_ref, xs_ref, wg_ref, wu_ref, wd_ref, ys_ref, wgb_ref, wub_ref, wdb_ref):
    i = pl.program_id(0)
    fresh = jnp.logical_or(i == 0, be_ref[i] != be_ref[jnp.maximum(i - 1, 0)])

    @pl.when(jnp.logical_and(fresh, i < nu_ref[0]))
    def _():
        wgb_ref[...] = wg_ref[...].astype(BF16)
        wub_ref[...] = wu_ref[...].astype(BF16)
        wdb_ref[...] = wd_ref[...].astype(BF16)

    @pl.when(i < nu_ref[0])
    def _():
        x = xs_ref[...]
        a = _silu(_dot(x, wgb_ref[...])) * _dot(x, wub_ref[...])
        ys_ref[...] = _dot(a.astype(BF16), wdb_ref[...]).astype(BF16)

    @pl.when(i >= nu_ref[0])
    def _():
        ys_ref[...] = jnp.zeros_like(ys_ref)


def _ffn(blk_e, n_used, xs, w_gate, w_up, w_down, layer):
    n_slots, d = xs.shape
    bm = BM_FFN
    de = w_gate.shape[-1]
    grid_spec = pltpu.PrefetchScalarGridSpec(
        num_scalar_prefetch=2,
        grid=(n_slots // bm,),
        in_specs=[BS((bm, d), lambda i, be, nu: (jnp.minimum(i, nu[0] - 1), 0)),
                  BS((None, None, d, de), lambda i, be, nu: (layer, be[i], 0, 0)),
                  BS((None, None, d, de), lambda i, be, nu: (layer, be[i], 0, 0)),
                  BS((None, None, de, d), lambda i, be, nu: (layer, be[i], 0, 0))],
        out_specs=BS((bm, d), lambda i, be, nu: (i, 0)),
        scratch_shapes=[pltpu.VMEM((d, de), BF16), pltpu.VMEM((d, de), BF16),
                        pltpu.VMEM((de, d), BF16)],
    )
    return pl.pallas_call(
        _ffn_body,
        out_shape=SDS((n_slots, d), BF16),
        grid_spec=grid_spec,
        compiler_params=_cparams("arbitrary"),
        name="moe_experts",
    )(blk_e, n_used, xs, w_gate, w_up, w_down)


def _combine_body(tab_ref, ys_ref, lslot_ref, wt_ref, x_ref, h2b_ref, mod_ref, wsg_ref, wsu_ref,
                  wsd_ref, fg_ref, o_ref, loc_ref, acc_ref, sem, *, tl, final):
    i = pl.program_id(0)

    @pl.when(i == 0)
    def _():
        loc_ref[...] = jnp.zeros_like(loc_ref)

    def run_copy(loc0, slot0):
        return pltpu.make_async_copy(ys_ref.at[pl.ds(slot0, RUN_ALIGN), :],
                                     loc_ref.at[pl.ds(loc0, RUN_ALIGN), :], sem)

    _for_each_run_chunk(tab_ref, lambda loc0, slot0: run_copy(loc0, slot0).start())
    hb = h2b_ref[...]
    a = _silu(_dot(hb, wsg_ref[...])) * _dot(hb, wsu_ref[...])
    acc_ref[...] = _dot(a.astype(BF16), wsd_ref[...])
    _for_each_run_chunk(tab_ref, lambda loc0, slot0: run_copy(loc0, slot0).wait())

    rows_used = tab_ref[0, 0, 3 * N_EXPERTS]
    ls = lslot_ref[...]
    wt = wt_ref[...]
    for c in range(LOCAL_SLOTS // SLOT_CHUNK):
        @pl.when(c * SLOT_CHUNK < rows_used)
        def _():
            col = lax.broadcasted_iota(I32, (tl, SLOT_CHUNK), 1) + c * SLOT_CHUNK
            gate = jnp.zeros((tl, SLOT_CHUNK), F32)
            for k in range(TOP_K):
                gate = jnp.where(col == ls[:, k:k + 1], wt[:, k:k + 1], gate)
            hi = gate.astype(BF16)
            lo = (gate - hi.astype(F32)).astype(BF16)
            y = loc_ref[c * SLOT_CHUNK:(c + 1) * SLOT_CHUNK, :]
            acc_ref[...] = acc_ref[...] + _dot(hi, y) + _dot(lo, y)

    xo = x_ref[...] + mod_ref[0, 5:6, :] * acc_ref[...]
    if final:
        ms = jnp.mean(xo * xo, axis=-1, keepdims=True)
        xo = (xo * lax.rsqrt(ms + EPS)) * fg_ref[...]
    o_ref[...] = xo


def _combine(run_table, ys, lslot_tk, wt_tk, x_flat, h2b, mod_lat, wsg, wsu, wsd, final_g,
             tokens_per_batch, final):
    t, d = x_flat.shape
    tl = MOE_TILE
    ds = wsg.shape[1]
    tiles_per_batch = tokens_per_batch // tl
    body = functools.partial(_combine_body, tl=tl, final=final)
    return pl.pallas_call(
        body,
        out_shape=SDS((t, d), F32),
        grid=(t // tl,),
        in_specs=[BS((1, 1, RUN_TABLE_WIDTH), lambda i: (i, 0, 0), memory_space=pltpu.SMEM),
                  BS(memory_space=pl.ANY),
                  BS((tl, TOP_K), lambda i: (i, 0)),
                  BS((tl, TOP_K), lambda i: (i, 0)),
                  BS((tl, d), lambda i: (i, 0)),
                  BS((tl, d), lambda i: (i, 0)),
                  BS((1, SUBLANES, d), lambda i: (i // tiles_per_batch, 0, 0)),
                  BS((d, ds), lambda i: (0, 0)),
                  BS((d, ds), lambda i: (0, 0)),
                  BS((ds, d), lambda i: (0, 0)),
                  BS((1, d), lambda i: (0, 0))],
        out_specs=BS((tl, d), lambda i: (i, 0)),
        scratch_shapes=[pltpu.VMEM((LOCAL_SLOTS, d), BF16), pltpu.VMEM((tl, d), F32),
                        pltpu.SemaphoreType.DMA(())],
        compiler_params=_cparams("arbitrary"),
        name="moe_combine",
    )(run_table, ys, lslot_tk, wt_tk, x_flat, h2b, mod_lat, wsg, wsu, wsd, final_g)


def _moe(x_flat, mod_lat, g, w_router, e_bias, w_gate, w_up, w_down, ws_gate, ws_up, ws_down,
         final_g, tri, tokens_per_batch, layer, final):
    t, d = x_flat.shape
    bm = BM_FFN
    n_tiles = t // MOE_TILE
    h2b, lslot, wts, loff, gbase, runlen, seg = _router(
        x_flat, mod_lat, g, w_router.T, e_bias.reshape(N_EXPERTS, 1), tri, tokens_per_batch)
    seg_rows = seg[:, 0].astype(I32)
    pcnt = (seg_rows + bm - 1) // bm * bm
    pend = jnp.cumsum(pcnt).astype(I32)
    pstart = pend - pcnt
    max_rows = t * TOP_K + n_tiles * N_EXPERTS * (RUN_ALIGN - 1)
    n_blocks = -(-max_rows // bm) + N_EXPERTS
    n_slots = n_blocks * bm
    block_row0 = jnp.arange(n_blocks, dtype=I32) * bm
    blk_e = jnp.minimum(jnp.sum((pend[None, :] <= block_row0[:, None]).astype(I32), axis=1),
                        N_EXPERTS - 1)
    n_used = pend[-1:] // bm
    loff_i = loff[:, :, 0].astype(I32)
    runlen_i = runlen[:, :, 0].astype(I32)
    slot0 = pstart[None, :] + gbase[:, :, 0].astype(I32)
    rows_used = jnp.broadcast_to((loff_i[:, -1] + runlen_i[:, -1])[:, None], (n_tiles, N_EXPERTS))
    run_table = jnp.concatenate([loff_i, slot0, runlen_i // RUN_ALIGN, rows_used],
                                axis=1).reshape(n_tiles, 1, RUN_TABLE_WIDTH)
    xs = _dispatch(pend, pcnt, n_used, run_table, lslot, h2b, n_slots)
    ys = _ffn(blk_e, n_used, xs, w_gate, w_up, w_down, layer)
    return _combine(run_table, ys, lslot.T, wts.T, x_flat, h2b, mod_lat,
                    ws_gate.astype(BF16), ws_up.astype(BF16), ws_down.astype(BF16),
                    final_g.reshape(1, d), tokens_per_batch, final)


def _inproj_c_body(xp_ref, x_ref, xn_ref, mod_ref, g_ref, w_ref, cw_ref, cb_ref,
                   v_ref, g1_ref, g2_ref, *, tm, n_tiles):
    i = pl.program_id(1)
    halo = SUBLANES
    xe = jnp.concatenate([xp_ref[0], x_ref[0], xn_ref[0]], axis=0)
    h = _norm_mod(xe, g_ref[...], mod_ref[0, 1:2, :], mod_ref[0, 0:1, :])
    row = lax.broadcasted_iota(I32, (tm + 2 * halo, 1), 0)
    outside = jnp.logical_or(jnp.logical_and(i == 0, row < halo),
                             jnp.logical_and(i == n_tiles - 1, row >= tm + halo))
    hb = jnp.where(outside, 0.0, h).astype(BF16)
    width = v_ref.shape[-1]
    for part, o_ref in enumerate((v_ref, g1_ref, g2_ref)):
        cols = slice(part * width, (part + 1) * width)
        zp = _dot(hb, w_ref[:, cols])
        up = pltpu.roll(zp, 1, 0)
        dn = pltpu.roll(zp, tm + 2 * halo - 1, 0)
        z = cw_ref[0:1, cols] * up + cw_ref[1:2, cols] * zp + cw_ref[2:3, cols] * dn + cb_ref[:, cols]
        o_ref[0] = z[halo:halo + tm]


def _inproj_c(x, mod_lat, g, w_in, conv_w, conv_b):
    b, l, d = x.shape
    tm = TM_PROJ
    n_tiles = l // tm
    w3 = w_in.shape[1]
    width = w3 // 3
    r8 = tm // SUBLANES
    body = functools.partial(_inproj_c_body, tm=tm, n_tiles=n_tiles)
    out = SDS((b, l, width), F32)
    return pl.pallas_call(
        body,
        out_shape=(out, out, out),
        grid=(b, n_tiles),
        in_specs=[BS((1, SUBLANES, d), lambda bi, i: (bi, jnp.maximum(i * r8 - 1, 0), 0)),
                  BS((1, tm, d), lambda bi, i: (bi, i, 0)),
                  BS((1, SUBLANES, d), lambda bi, i: (bi, jnp.minimum((i + 1) * r8, l // SUBLANES - 1), 0)),
                  BS((1, SUBLANES, d), lambda bi, i: (bi, 0, 0)),
                  BS((1, d), lambda bi, i: (0, 0)),
                  BS((d, w3), lambda bi, i: (0, 0)),
                  BS((3, w3), lambda bi, i: (0, 0)),
                  BS((1, w3), lambda bi, i: (0, 0))],
        out_specs=(BS((1, tm, width), lambda bi, i: (bi, i, 0)),
                   BS((1, tm, width), lambda bi, i: (bi, i, 0)),
                   BS((1, tm, width), lambda bi, i: (bi, i, 0))),
        compiler_params=_cparams("arbitrary", "arbitrary"),
        name="inproj_c",
    )(x, x, x, mod_lat, g, w_in, conv_w, conv_b)


def _filter_body(f_ref, w1_ref, b1_ref, w2_ref, b2_ref, w3_ref, fr_ref, dl_ref, keep0_ref,
                 hf_ref, l1_ref, *, tp):
    i = pl.program_id(0)
    feats = f_ref[...]
    fr = fr_ref[...]
    a = jnp.sin(fr * (_dot_hp(feats, w1_ref[...]) + b1_ref[...]))
    a = jnp.sin(fr * (_dot_hp(a, w2_ref[...]) + b2_ref[...]))
    hf = _dot_hp(a, w3_ref[...])
    t01 = feats[:, 0:1]
    hf = hf * (jnp.exp(-t01 * jnp.abs(dl_ref[...])) + DECAY_SHIFT)
    row = lax.broadcasted_iota(I32, hf.shape, 0) + i * tp
    hf = jnp.where(row == 0, hf * keep0_ref[...], hf)
    hf_ref[...] = hf

    @pl.when(i == 0)
    def _():
        l1_ref[...] = jnp.zeros_like(l1_ref)

    l1_ref[...] = l1_ref[...] + jnp.sum(jnp.abs(hf), axis=0, keepdims=True)


def _filters(feats, w1, b1, w2, b2, w3, freq, delta, width):
    n, fe = feats.shape
    hid = w2.shape[0]
    fo = w3.shape[1]
    tp = 256
    lag0_keep = jnp.tile(jnp.repeat(jnp.array([1.0, 0.0], F32), width), fo // (2 * width)).reshape(1, fo)
    body = functools.partial(_filter_body, tp=tp)
    full = lambda shape: BS(shape, lambda i: (0, 0))
    return pl.pallas_call(
        body,
        out_shape=(SDS((n, fo), F32), SDS((SUBLANES, fo), F32)),
        grid=(n // tp,),
        in_specs=[BS((tp, fe), lambda i: (i, 0)), full((fe, hid)), full((1, hid)),
                  full((hid, hid)), full((1, hid)), full((hid, fo)), full((1, hid)), full((1, fo)),
                  full((1, fo))],
        out_specs=(BS((tp, fo), lambda i: (i, 0)), BS((SUBLANES, fo), lambda i: (0, 0))),
        compiler_params=_cparams("arbitrary"),
        name="hyena_filters",
    )(feats, w1, b1, w2, b2, w3, freq, delta, lag0_keep)


DFT_R = 128


def _dft_tables(n):
    r = DFT_R
    m = 2 * n
    na = n // r
    two_pi = 2.0 * np.pi
    a = np.arange(na)[None, :]
    v = np.arange(r)[:, None]
    ang1 = two_pi * ((a * v) % r) / r
    f1 = np.concatenate([np.cos(ang1), -np.sin(ang1)], axis=0)
    b = np.arange(r)[None, None, :]
    u = np.arange(r)[None, :, None]
    vv = np.arange(r)[:, None, None]
    ang2 = two_pi * ((b * (r * u + vv)) % m) / m
    gr, gi = np.cos(ang2), -np.sin(ang2)
    fwd = np.concatenate([np.concatenate([gr, -gi], axis=2),
                          np.concatenate([gi, gr], axis=2)], axis=1)
    hr, hi = np.transpose(gr, (0, 2, 1)), -np.transpose(gi, (0, 2, 1))
    inv = np.concatenate([np.concatenate([hr, -hi], axis=2),
                          np.concatenate([hi, hr], axis=2)], axis=1)
    ang3 = two_pi * ((np.arange(na)[:, None] * np.arange(r)[None, :]) % r) / r
    f3 = np.concatenate([np.cos(ang3), -np.sin(ang3)], axis=1) / m
    cast = lambda t: jnp.asarray(t.astype(np.float32)).astype(BF16)
    return cast(f1), cast(fwd), cast(inv), cast(f3)


def _dft_s1_body(y_ref, f1_ref, ar_ref, ai_ref):
    f1 = f1_ref[...]
    for j in range(SUBLANES):
        res = _dot(f1, y_ref[:, j, :].astype(BF16))
        ar_ref[:, j, :] = res[:DFT_R]
        ai_ref[:, j, :] = res[DFT_R:]


def _dft_s1(y4, f1):
    nb, na, r, c = y4.shape
    ct = min(c, 1024)
    out = SDS((nb, r, r, c), F32)
    return pl.pallas_call(
        _dft_s1_body,
        out_shape=(out, out),
        grid=(nb, c // ct, r // SUBLANES),
        in_specs=[BS((None, na, SUBLANES, ct), lambda n, cc, j: (n, 0, j, cc)),
                  BS((2 * r, na), lambda n, cc, j: (0, 0))],
        out_specs=(BS((None, r, SUBLANES, ct), lambda n, cc, j: (n, 0, j, cc)),
                   BS((None, r, SUBLANES, ct), lambda n, cc, j: (n, 0, j, cc))),
        compiler_params=_cparams("arbitrary", "arbitrary", "arbitrary"),
        name="dft_stage1",
    )(y4, f1)


def _filter_spec_body(arf_ref, aif_ref, arb_ref, aib_ref, g_ref, l1f_ref, l1b_ref, kr_ref, ki_ref):
    g = g_ref[...]
    yf = _dot(g, jnp.concatenate([arf_ref[...], aif_ref[...]], axis=0).astype(BF16))
    yb = _dot(g, jnp.concatenate([arb_ref[...], aib_ref[...]], axis=0).astype(BF16))
    inv = 1.0 / (l1f_ref[0:1, :] + l1b_ref[0:1, :])
    kr_ref[...] = (yf[:DFT_R] + yb[:DFT_R]) * inv
    ki_ref[...] = (yf[DFT_R:] - yb[DFT_R:]) * inv


def _filter_spectrum(ar, ai, fwd, l1, width):
    r = DFT_R
    a_spec = lambda d: BS((None, None, r, width), lambda v, o: (0, v, 0, 2 * o + d))
    l_spec = lambda d: BS((SUBLANES, width), lambda v, o: (0, 2 * o + d))
    out = SDS((r, r, HYENA_ORDER * width), F32)
    return pl.pallas_call(
        _filter_spec_body,
        out_shape=(out, out),
        grid=(r, HYENA_ORDER),
        in_specs=[a_spec(0), a_spec(0), a_spec(1), a_spec(1),
                  BS((None, 2 * r, 2 * r), lambda v, o: (v, 0, 0)), l_spec(0), l_spec(1)],
        out_specs=(BS((None, r, width), lambda v, o: (v, 0, o)),
                   BS((None, r, width), lambda v, o: (v, 0, o))),
        compiler_params=_cparams("arbitrary", "arbitrary"),
        name="hyena_filter_spectrum",
    )(ar, ai, ar, ai, fwd, l1, l1)


def _conv_mid_body(ar_ref, ai_ref, g_ref, h_ref, kr_ref, ki_ref, qr_ref, qi_ref):
    y = _dot(g_ref[...], jnp.concatenate([ar_ref[...], ai_ref[...]], axis=0).astype(BF16))
    yr, yi = y[:DFT_R], y[DFT_R:]
    kr, ki = kr_ref[...], ki_ref[...]
    p = jnp.concatenate([yr * kr - yi * ki, yr * ki + yi * kr], axis=0).astype(BF16)
    q = _dot(h_ref[...], p)
    qr_ref[...] = q[:DFT_R]
    qi_ref[...] = q[DFT_R:]


def _conv_mid(ar, ai, fwd, inv, kr, ki, order):
    nb, r, _, c = ar.shape
    a_spec = BS((None, None, r, c), lambda n, v: (n, v, 0, 0))
    m_spec = BS((None, 2 * r, 2 * r), lambda n, v: (v, 0, 0))
    k_spec = BS((None, r, c), lambda n, v: (v, 0, order))
    out = SDS((nb, r, r, c), F32)
    return pl.pallas_call(
        _conv_mid_body,
        out_shape=(out, out),
        grid=(nb, r),
        in_specs=[a_spec, a_spec, m_spec, m_spec, k_spec, k_spec],
        out_specs=(a_spec, a_spec),
        compiler_params=_cparams("arbitrary", "arbitrary"),
        name="hyena_spectral_product",
    )(ar, ai, fwd, inv, kr, ki)


def _idft_gate_body(qr_ref, qi_ref, f3_ref, y_ref, gate_ref, fb_ref, o_ref):
    f3 = f3_ref[...]
    fb = fb_ref[...]
    for j in range(SUBLANES):
        q = jnp.concatenate([qr_ref[:, j, :], qi_ref[:, j, :]], axis=0).astype(BF16)
        conv = _dot(f3, q)
        o_ref[:, j, :] = gate_ref[:, j, :] * (conv + fb * y_ref[:, j, :])


def _idft_gate(qr, qi, f3, y4, gate4, fbias):
    nb, na, r, c = y4.shape
    q_spec = BS((None, r, SUBLANES, c), lambda n, j: (n, 0, j, 0))
    y_spec = BS((None, na, SUBLANES, c), lambda n, j: (n, 0, j, 0))
    return pl.pallas_call(
        _idft_gate_body,
        out_shape=SDS((nb, na, r, c), F32),
        grid=(nb, r // SUBLANES),
        in_specs=[q_spec, q_spec, BS((na, 2 * r), lambda n, j: (0, 0)), y_spec, y_spec,
                  BS((1, c), lambda n, j: (0, 0))],
        out_specs=y_spec,
        compiler_params=_cparams("arbitrary", "arbitrary"),
        name="hyena_idft_gate",
    )(qr, qi, f3, y4, gate4, fbias)


def _outproj_body(y_ref, w_ref, x_ref, mod_ref, o_ref):
    o_ref[0] = x_ref[0] + mod_ref[0, 2:3, :] * _dot(y_ref[0].astype(BF16), w_ref[...])


def _outproj(y, w_out, x, mod_lat):
    b, l, d = x.shape
    tm = TM_PROJ
    wdt = y.shape[-1]
    return pl.pallas_call(
        _outproj_body,
        out_shape=SDS((b, l, d), F32),
        grid=(b, l // tm),
        in_specs=[BS((1, tm, wdt), lambda bi, i: (bi, i, 0)),
                  BS((wdt, d), lambda bi, i: (0, 0)),
                  BS((1, tm, d), lambda bi, i: (bi, i, 0)),
                  BS((1, SUBLANES, d), lambda bi, i: (bi, 0, 0))],
        out_specs=BS((1, tm, d), lambda bi, i: (bi, i, 0)),
        compiler_params=_cparams("arbitrary", "arbitrary"),
        name="outproj_c",
    )(y, w_out, x, mod_lat)


def _hyena(x, mod_lat, g, w_in, conv_w, conv_b, w1, b1, w2, b2, w3, freq, delta, f_bias, w_out):
    b, n, d = x.shape
    width = w_out.shape[0]
    r = DFT_R
    na = n // r
    f1, fwd, inv, f3 = _dft_tables(n)
    v, gate1, gate2 = _inproj_c(x, mod_lat, g, w_in.astype(BF16), conv_w, conv_b.reshape(1, -1))

    t = jnp.arange(n, dtype=F32)
    t01 = t / max(n - 1, 1)
    bands = jnp.linspace(1e-4, FILT_BANDS - 1, FILT_BANDS, dtype=F32)
    ang = (2.0 * math.pi / n) * t[:, None] * bands[None, :]
    feats = jnp.concatenate([t01[:, None], jnp.cos(ang), jnp.sin(ang)], axis=-1)
    fe = feats.shape[1]
    feats = jnp.pad(feats, ((0, 0), (0, LANES - fe)))
    w1p = jnp.pad(w1, ((0, LANES - fe), (0, 0)))
    hf, l1 = _filters(feats, w1p, b1.reshape(1, -1), w2, b2.reshape(1, -1), w3,
                      freq.reshape(1, -1), delta.reshape(1, -1), width)
    far, fai = _dft_s1(hf.reshape(1, na, r, hf.shape[1]), f1)
    kr, ki = _filter_spectrum(far, fai, fwd, l1, width)

    y4 = v.reshape(b, na, r, width)
    for o, gate in enumerate((gate1, gate2)):
        ar, ai = _dft_s1(y4, f1)
        qr, qi = _conv_mid(ar, ai, fwd, inv, kr, ki, o)
        y4 = _idft_gate(qr, qi, f3, y4, gate.reshape(b, na, r, width), f_bias[o].reshape(1, width))
    return _outproj(y4.reshape(b, n, width), w_out.astype(BF16), x, mod_lat)


def _rope_tables(seq_len):
    rows = seq_len // GRID_W
    row = jnp.repeat(jnp.arange(rows, dtype=F32), GRID_W)
    col = jnp.tile(jnp.arange(GRID_W, dtype=F32), rows)
    inv = jnp.power(ROPE_BASE, -jnp.arange(ROPE_FREQS, dtype=F32) / ROPE_FREQS)
    ar, ac = row[:, None] * inv, col[:, None] * inv
    cos_h = jnp.concatenate([jnp.cos(ar), jnp.cos(ar), jnp.cos(ac), jnp.cos(ac)], axis=1)
    sin_h = jnp.concatenate([-jnp.sin(ar), jnp.sin(ar), -jnp.sin(ac), jnp.sin(ac)], axis=1)
    reps = LANES // HEAD_DIM
    return jnp.tile(cos_h, (1, reps)), jnp.tile(sin_h, (1, reps))


def _rotate_partner_columns(w):
    ncol = w.shape[1]
    lane = np.arange(ncol)
    partner = np.where((lane % (2 * ROPE_FREQS)) < ROPE_FREQS, lane + ROPE_FREQS, lane - ROPE_FREQS)
    return w[:, partner]


def kernel(x, c, ctx, c_ctx, w_mod, b_mod, norm_g, w_in_ab, sink, w_spatial, b_spatial, w_out_ab,
           w_in_c, conv_w, conv_b, filt_w1, filt_b1, filt_w2, filt_b2, filt_w3, filt_freq,
           filt_delta, filt_bias, w_out_c, w_router, e_bias, w_gate, w_up, w_down, ws_gate,
           ws_up, ws_down, final_g):
    b, l, d = x.shape
    depth = w_mod.shape[0]
    assert depth == 2 and b + 1 <= SUBLANES

    cc = jnp.zeros((SUBLANES, d), F32).at[:b].set(c).at[b].set(c_ctx)
    m_all = _mod_vectors(cc, w_mod, b_mod)

    def mod_rows(layer, row0, nrow):
        m = m_all[layer, row0:row0 + nrow].reshape(nrow, 6, d)
        return jnp.pad(m, ((0, 0), (0, SUBLANES - 6), (0, 0)))

    tri = jnp.triu(jnp.ones((MOE_TILE, MOE_TILE), F32), k=1).astype(BF16)

    mod_lat = mod_rows(0, 0, b)
    mod_ctx = mod_rows(0, b, 1)[0]
    w_in = w_in_ab[0]
    qk = ATTN_WIDTH + KV_WIDTH
    w_cat = jnp.concatenate([w_in, _rotate_partner_columns(w_in[:, :qk])], axis=1).astype(BF16)
    cos_t, sin_t = _rope_tables(l)
    group_avg = jnp.kron(jnp.eye(N_SG_GROUPS, dtype=F32),
                         jnp.full((SG_GROUP_DIM, SG_GROUP_DIM), 1.0 / SG_GROUP_DIM, F32)).astype(BF16)
    kc, vc = _ctx_kv(ctx, mod_ctx, norm_g[0, 0].reshape(1, d),
                     w_in[:, ATTN_WIDTH:ATTN_WIDTH + 2 * KV_WIDTH].astype(BF16))
    q, k, v, ug, vn = _inproj_ab(x, mod_lat, norm_g[0, 0].reshape(1, d), w_cat, cos_t, sin_t, group_avg)
    b_full = jnp.repeat(b_spatial[0].T, SG_GROUP_DIM, axis=1)
    x1 = _mixer(sink[0], q, k, v, kc, vc, ug, vn, w_spatial[0].astype(BF16), b_full,
                w_out_ab[0].astype(BF16), x, mod_lat)
    x2 = _moe(x1.reshape(b * l, d), mod_lat, norm_g[0, 1].reshape(1, d), w_router[0], e_bias[0],
              w_gate, w_up, w_down, ws_gate[0], ws_up[0], ws_down[0], final_g, tri, l,
              layer=0, final=False).reshape(b, l, d)

    mod_lat = mod_rows(1, 0, b)
    x3 = _hyena(x2, mod_lat, norm_g[1, 0].reshape(1, d), w_in_c[0], conv_w[0], conv_b[0],
                filt_w1[0], filt_b1[0], filt_w2[0], filt_b2[0], filt_w3[0], filt_freq[0],
                filt_delta[0], filt_bias[0], w_out_c[0])
    out = _moe(x3.reshape(b * l, d), mod_lat, norm_g[1, 1].reshape(1, d), w_router[1], e_bias[1],
               w_gate, w_up, w_down, ws_gate[1], ws_up[1], ws_down[1], final_g, tri, l,
               layer=1, final=True)
    return out.reshape(b, l, d)
```

```python
import functools
import math

import numpy as np
import jax
import jax.numpy as jnp
from jax import lax
from jax.experimental import pallas as pl
from jax.experimental.pallas import tpu as pltpu

F32 = jnp.float32
BF16 = jnp.bfloat16
I32 = jnp.int32
HIGHEST = lax.Precision.HIGHEST
SDS = jax.ShapeDtypeStruct
BS = pl.BlockSpec

EPS = 1e-6
NEG = -1e30

GRID_W = 64
N_Q_HEADS = 8
N_KV_HEADS = 2
HEAD_DIM = 64
ATTN_WIDTH = N_Q_HEADS * HEAD_DIM
KV_WIDTH = N_KV_HEADS * HEAD_DIM
WINDOW = 128
BLOCK = 128
ROPE_BASE = 10000.0
ROPE_FREQS = HEAD_DIM // 4
N_SG_GROUPS = 8
SG_GROUP_DIM = 64
SG_WIDTH = N_SG_GROUPS * SG_GROUP_DIM
HYENA_ORDER = 2
FILT_BANDS = 16
DECAY_SHIFT = 0.05
N_EXPERTS = 64
TOP_K = 8
N_GROUPS = 8
TOPK_GROUPS = 4
ROUTED_SCALE = 2.5

LANES = 128
SUBLANES = 8
VMEM_LIMIT = 56 * 1024 * 1024

TM_PROJ = 512
TQ_MIX = 256
MOE_TILE = 256
BM_FFN = 512
RUN_ALIGN = 16
SLOT_CHUNK = 512
ONEHOT_ROWS = 64
GATE_ROWS = 32
LOCAL_SLOTS = -(-(TOP_K * MOE_TILE + N_EXPERTS * (RUN_ALIGN - 1)) // SLOT_CHUNK) * SLOT_CHUNK
RUN_TABLE_WIDTH = 4 * N_EXPERTS


def _cparams(*sem):
    return pltpu.CompilerParams(dimension_semantics=sem, vmem_limit_bytes=VMEM_LIMIT)


def _dot(a, b):
    return jnp.dot(a, b, preferred_element_type=F32)


def _dot_nt(a, b):
    return lax.dot_general(a, b, (((1,), (1,)), ((), ())), preferred_element_type=F32)


def _dot_hp(a, b):
    return jnp.dot(a, b, preferred_element_type=F32, precision=HIGHEST)


def _norm_mod(x, g, sc, sh):
    ms = jnp.mean(x * x, axis=-1, keepdims=True)
    y = x * lax.rsqrt(ms + EPS)
    return (y * g) * (1.0 + sc) + sh


def _gelu_tanh(x):
    c = math.sqrt(2.0 / math.pi)
    return 0.5 * x * (1.0 + jnp.tanh(c * (x + 0.044715 * (x * x * x))))


def _silu(x):
    return x * jax.nn.sigmoid(x)


def _mod_body(c_ref, w_ref, b_ref, o_ref):
    o_ref[0] = _dot_hp(_silu(c_ref[...]), w_ref[0]) + b_ref[0]


def _mod_vectors(cc, w_mod, b_mod):
    depth, d, n = w_mod.shape
    tn = 1536
    return pl.pallas_call(
        _mod_body,
        out_shape=SDS((depth, SUBLANES, n), F32),
        grid=(depth, n // tn),
        in_specs=[BS((SUBLANES, d), lambda l, j: (0, 0)),
                  BS((1, d, tn), lambda l, j: (l, 0, j)),
                  BS((1, 1, tn), lambda l, j: (l, 0, j))],
        out_specs=BS((1, SUBLANES, tn), lambda l, j: (l, 0, j)),
        compiler_params=_cparams("arbitrary", "arbitrary"),
        name="mod_vectors",
    )(cc, w_mod, b_mod.reshape(depth, 1, n))


def _ctx_kv_body(ctx_ref, mod_ref, g_ref, w_ref, kc_ref, vc_ref):
    h = _norm_mod(ctx_ref[0], g_ref[...], mod_ref[1:2, :], mod_ref[0:1, :])
    z = _dot(h.astype(BF16), w_ref[...])
    kc_ref[0] = z[:, :KV_WIDTH].astype(BF16)
    vc_ref[0] = z[:, KV_WIDTH:].astype(BF16)


def _ctx_kv(ctx, mod_ctx, g, w_kv):
    b, c, d = ctx.shape
    return pl.pallas_call(
        _ctx_kv_body,
        out_shape=(SDS((b, c, KV_WIDTH), BF16), SDS((b, c, KV_WIDTH), BF16)),
        grid=(b,),
        in_specs=[BS((1, c, d), lambda i: (i, 0, 0)),
                  BS((SUBLANES, d), lambda i: (0, 0)),
                  BS((1, d), lambda i: (0, 0)),
                  BS((d, 2 * KV_WIDTH), lambda i: (0, 0))],
        out_specs=(BS((1, c, KV_WIDTH), lambda i: (i, 0, 0)),
                   BS((1, c, KV_WIDTH), lambda i: (i, 0, 0))),
        compiler_params=_cparams("arbitrary"),
        name="ctx_kv",
    )(ctx, mod_ctx, g, w_kv)


def _inproj_ab_body(x_ref, mod_ref, g_ref, w_ref, cos_ref, sin_ref, avg_ref,
                    q_ref, k_ref, v_ref, ug_ref, vn_ref):
    h = _norm_mod(x_ref[0], g_ref[...], mod_ref[0, 1:2, :], mod_ref[0, 0:1, :]).astype(BF16)
    cs = cos_ref[...]
    sn = sin_ref[...]
    rot0 = ATTN_WIDTH + 2 * KV_WIDTH + 2 * SG_WIDTH
    scale = HEAD_DIM ** -0.5
    for j in range(ATTN_WIDTH // LANES):
        z = _dot(h, w_ref[:, j * LANES:(j + 1) * LANES])
        zr = _dot(h, w_ref[:, rot0 + j * LANES:rot0 + (j + 1) * LANES])
        q_ref[0, :, j * LANES:(j + 1) * LANES] = ((z * cs + zr * sn) * scale).astype(BF16)
    zk = _dot(h, w_ref[:, ATTN_WIDTH:ATTN_WIDTH + KV_WIDTH])
    zkr = _dot(h, w_ref[:, rot0 + ATTN_WIDTH:rot0 + ATTN_WIDTH + KV_WIDTH])
    k_ref[0] = (zk * cs + zkr * sn).astype(BF16)
    v_ref[0] = _dot(h, w_ref[:, ATTN_WIDTH + KV_WIDTH:ATTN_WIDTH + 2 * KV_WIDTH]).astype(BF16)
    u0 = ATTN_WIDTH + 2 * KV_WIDTH
    ug_ref[0] = _gelu_tanh(_dot(h, w_ref[:, u0:u0 + SG_WIDTH]))
    vf = _gelu_tanh(_dot(h, w_ref[:, u0 + SG_WIDTH:u0 + 2 * SG_WIDTH]))
    avg = avg_ref[...]

    def gmean(t):
        hi = t.astype(BF16)
        lo = (t - hi.astype(F32)).astype(BF16)
        return _dot(hi, avg) + _dot(lo, avg)

    vc = vf - gmean(vf)
    vn_ref[0] = (vc * lax.rsqrt(gmean(vc * vc) + EPS)).astype(BF16)


def _inproj_ab(x, mod_lat, g, w_cat, cos_t, sin_t, avg):
    b, l, d = x.shape
    tm = TM_PROJ
    ncol = w_cat.shape[1]
    return pl.pallas_call(
        _inproj_ab_body,
        out_shape=(SDS((b, l, ATTN_WIDTH), BF16), SDS((b, l, KV_WIDTH), BF16),
                   SDS((b, l, KV_WIDTH), BF16), SDS((b, l, SG_WIDTH), F32),
                   SDS((b, l, SG_WIDTH), BF16)),
        grid=(b, l // tm),
        in_specs=[BS((1, tm, d), lambda bi, i: (bi, i, 0)),
                  BS((1, SUBLANES, d), lambda bi, i: (bi, 0, 0)),
                  BS((1, d), lambda bi, i: (0, 0)),
                  BS((d, ncol), lambda bi, i: (0, 0)),
                  BS((tm, LANES), lambda bi, i: (i, 0)),
                  BS((tm, LANES), lambda bi, i: (i, 0)),
                  BS((SG_WIDTH, SG_WIDTH), lambda bi, i: (0, 0))],
        out_specs=(BS((1, tm, ATTN_WIDTH), lambda bi, i: (bi, i, 0)),
                   BS((1, tm, KV_WIDTH), lambda bi, i: (bi, i, 0)),
                   BS((1, tm, KV_WIDTH), lambda bi, i: (bi, i, 0)),
                   BS((1, tm, SG_WIDTH), lambda bi, i: (bi, i, 0)),
                   BS((1, tm, SG_WIDTH), lambda bi, i: (bi, i, 0))),
        compiler_params=_cparams("arbitrary", "arbitrary"),
        name="inproj_ab",
    )(x, mod_lat, g, w_cat, cos_t, sin_t, avg)


def _mixer_body(sink_ref, q_ref, kp_ref, kcur_ref, kn_ref, vp_ref, vcur_ref, vn_ref,
                kc_ref, vc_ref, ug_ref, vnorm_ref, ws_ref, bs_ref, wout_ref, x_ref, mod_ref,
                o_ref, cat_ref, *, seq_len, sub_blocks):
    i = pl.program_id(1)
    kk = jnp.concatenate([kp_ref[0], kcur_ref[0], kn_ref[0]], axis=0)
    vv = jnp.concatenate([vp_ref[0], vcur_ref[0], vn_ref[0]], axis=0)
    kc = kc_ref[0]
    vc = vc_ref[0]
    span = 3 * BLOCK
    ii = lax.broadcasted_iota(I32, (BLOCK, span), 0)
    jj = lax.broadcasted_iota(I32, (BLOCK, span), 1)
    dd = jj - ii
    in_window = jnp.where(dd >= 0, jnp.where(dd <= 2 * WINDOW, 1, 0), 0)
    group = N_Q_HEADS // N_KV_HEADS
    for r in range(sub_blocks):
        rows = slice(r * BLOCK, (r + 1) * BLOCK)
        kpos = (i * sub_blocks + r - 1) * BLOCK + jj
        in_seq = jnp.where(kpos >= 0, jnp.where(kpos < seq_len, 1, 0), 0)
        bias = jnp.where(in_window * in_seq > 0, 0.0, NEG)
        qb = q_ref[0, rows, :]
        kl = kk[r * BLOCK:r * BLOCK + span]
        vl = vv[r * BLOCK:r * BLOCK + span]
        for hq in range(N_Q_HEADS):
            hk = hq // group
            ks = slice(hk * HEAD_DIM, (hk + 1) * HEAD_DIM)
            qh = qb[:, hq * HEAD_DIM:(hq + 1) * HEAD_DIM]
            s_loc = _dot_nt(qh, kl[:, ks]) + bias
            s_ctx = _dot_nt(qh, kc[:, ks])
            sk = sink_ref[hq]
            m = jnp.maximum(jnp.maximum(jnp.max(s_loc, axis=-1, keepdims=True),
                                        jnp.max(s_ctx, axis=-1, keepdims=True)), sk)
            p_loc = jnp.exp(s_loc - m)
            p_ctx = jnp.exp(s_ctx - m)
            den = (jnp.sum(p_loc, axis=-1, keepdims=True) + jnp.sum(p_ctx, axis=-1, keepdims=True)
                   + jnp.exp(sk - m))
            o = _dot(p_loc.astype(BF16), vl[:, ks]) + _dot(p_ctx.astype(BF16), vc[:, ks])
            cat_ref[rows, hq * HEAD_DIM:(hq + 1) * HEAD_DIM] = (o / den).astype(BF16)
        vnb = vnorm_ref[0, rows, :]
        ugb = ug_ref[0, rows, :]
        for g in range(N_SG_GROUPS):
            gs = slice(g * SG_GROUP_DIM, (g + 1) * SG_GROUP_DIM)
            sg = _dot(ws_ref[g], vnb[:, gs]) + bs_ref[:, gs]
            cat_ref[rows, ATTN_WIDTH + g * SG_GROUP_DIM:ATTN_WIDTH + (g + 1) * SG_GROUP_DIM] = (
                ugb[:, gs] * sg).astype(BF16)
    y = _dot(cat_ref[...], wout_ref[...])
    o_ref[0] = x_ref[0] + mod_ref[0, 2:3, :] * y


def _mixer(sink, q, k, v, kc, vc, ug, vn, w_s, b_full, w_out, x, mod_lat):
    b, l, d = x.shape
    tq = TQ_MIX
    r = tq // BLOCK
    nb = l // BLOCK
    c = kc.shape[1]
    prev_map = lambda bi, i: (bi, jnp.maximum(i * r - 1, 0), 0)
    next_map = lambda bi, i: (bi, jnp.minimum((i + 1) * r, nb - 1), 0)
    cur_map = lambda bi, i: (bi, i, 0)
    body = functools.partial(_mixer_body, seq_len=l, sub_blocks=r)
    return pl.pallas_call(
        body,
        out_shape=SDS((b, l, d), F32),
        grid=(b, l // tq),
        in_specs=[BS(memory_space=pltpu.SMEM),
                  BS((1, tq, ATTN_WIDTH), cur_map),
                  BS((1, BLOCK, KV_WIDTH), prev_map), BS((1, tq, KV_WIDTH), cur_map),
                  BS((1, BLOCK, KV_WIDTH), next_map),
                  BS((1, BLOCK, KV_WIDTH), prev_map), BS((1, tq, KV_WIDTH), cur_map),
                  BS((1, BLOCK, KV_WIDTH), next_map),
                  BS((1, c, KV_WIDTH), lambda bi, i: (bi, 0, 0)),
                  BS((1, c, KV_WIDTH), lambda bi, i: (bi, 0, 0)),
                  BS((1, tq, SG_WIDTH), cur_map), BS((1, tq, SG_WIDTH), cur_map),
                  BS((N_SG_GROUPS, BLOCK, BLOCK), lambda bi, i: (0, 0, 0)),
                  BS((BLOCK, SG_WIDTH), lambda bi, i: (0, 0)),
                  BS((d, d), lambda bi, i: (0, 0)),
                  BS((1, tq, d), cur_map),
                  BS((1, SUBLANES, d), lambda bi, i: (bi, 0, 0))],
        out_specs=BS((1, tq, d), cur_map),
        scratch_shapes=[pltpu.VMEM((tq, d), BF16)],
        compiler_params=_cparams("arbitrary", "arbitrary"),
        name="mixer_ab",
    )(sink, q, k, k, k, v, v, v, kc, vc, ug, vn, w_s, b_full, w_out, x, mod_lat)


def _router_body(x_ref, mod_ref, g_ref, wr_ref, eb_ref, tri_ref,
                 h2b_ref, lslot_ref, wts_ref, loff_ref, gbase_ref, runlen_ref, cnt_ref, carry_ref,
                 *, tm):
    i = pl.program_id(0)

    @pl.when(i == 0)
    def _():
        carry_ref[...] = jnp.zeros_like(carry_ref)

    h2 = _norm_mod(x_ref[...], g_ref[...], mod_ref[0, 4:5, :], mod_ref[0, 3:4, :])
    h2b_ref[...] = h2.astype(BF16)

    logits = lax.dot_general(wr_ref[...], h2, (((1,), (1,)), ((), ())),
                             preferred_element_type=F32, precision=HIGHEST)
    scores = jax.nn.sigmoid(logits)
    per_group = N_EXPERTS // N_GROUPS
    shape3 = (N_GROUPS, per_group, tm)
    s3 = scores.reshape(shape3)
    b3 = (scores + eb_ref[...]).reshape(shape3)
    sub = lax.broadcasted_iota(I32, shape3, 1)
    eid = lax.broadcasted_iota(I32, shape3, 0) * per_group + sub

    m1 = jnp.max(b3, axis=1, keepdims=True)
    i1 = jnp.min(jnp.where(b3 == m1, sub, per_group), axis=1, keepdims=True)
    m2 = jnp.max(jnp.where(sub == i1, -jnp.inf, b3), axis=1, keepdims=True)
    gs = m1 + m2
    keep = []
    for g in range(N_GROUPS):
        beaten = jnp.zeros((1, tm), I32)
        for g2 in range(N_GROUPS):
            if g2 == g:
                continue
            wins = (gs[g2] >= gs[g]) if g2 < g else (gs[g2] > gs[g])
            beaten = beaten + jnp.where(wins, 1, 0)
        keep.append(jnp.where(beaten < TOPK_GROUPS, 1, 0)[None])
    keep3 = jnp.concatenate(keep, axis=0)
    val = jnp.where(keep3 > 0, b3, -jnp.inf)

    def red(fn, a):
        return fn(fn(a, axis=0, keepdims=True), axis=1, keepdims=True)

    idxs, ws = [], []
    member = jnp.zeros(shape3, F32)
    for _ in range(TOP_K):
        m = red(jnp.max, val)
        idx = red(jnp.min, jnp.where(val == m, eid, N_EXPERTS))
        hit = eid == idx
        ws.append(red(jnp.sum, jnp.where(hit, s3, 0.0)))
        val = jnp.where(hit, -jnp.inf, val)
        member = member + jnp.where(hit, 1.0, 0.0)
        idxs.append(idx)
    wsum = ws[0]
    for w in ws[1:]:
        wsum = wsum + w

    member2 = member.reshape(N_EXPERTS, tm)
    cnt = jnp.sum(member2, axis=1, keepdims=True)
    runlen = jnp.floor((cnt + (RUN_ALIGN - 1)) * (1.0 / RUN_ALIGN)) * RUN_ALIGN
    runlen_b = jnp.broadcast_to(runlen, (N_EXPERTS, LANES))
    e_row = lax.broadcasted_iota(I32, (N_EXPERTS, N_EXPERTS), 0)
    e_col = lax.broadcasted_iota(I32, (N_EXPERTS, N_EXPERTS), 1)
    earlier = jnp.where(e_col < e_row, 1.0, 0.0).astype(BF16)
    loff = _dot(earlier, runlen_b.astype(BF16))
    slot = _dot(member2.astype(BF16), tri_ref[...]) + loff[:, 0:1]
    slot3 = slot.reshape(shape3)
    for k in range(TOP_K):
        wts_ref[k:k + 1, :] = (ws[k] / wsum * ROUTED_SCALE).reshape(1, tm)
        sk = red(jnp.sum, jnp.where(eid == idxs[k], slot3, 0.0))
        lslot_ref[k:k + 1, :] = sk.reshape(1, tm).astype(I32)
    loff_ref[0] = loff
    gbase_ref[0] = carry_ref[...]
    runlen_ref[0] = runlen_b
    total = carry_ref[...] + runlen_b
    carry_ref[...] = total
    cnt_ref[...] = total


def _router(x_flat, mod_lat, g, wr_t, e_bias, tri, tokens_per_batch):
    t, d = x_flat.shape
    tm = MOE_TILE
    n_tiles = t // tm
    tiles_per_batch = tokens_per_batch // tm
    body = functools.partial(_router_body, tm=tm)
    table = SDS((n_tiles, N_EXPERTS, LANES), F32)
    table_spec = BS((1, N_EXPERTS, LANES), lambda i: (i, 0, 0))
    return pl.pallas_call(
        body,
        out_shape=(SDS((t, d), BF16), SDS((TOP_K, t), I32), SDS((TOP_K, t), F32),
                   table, table, table, SDS((N_EXPERTS, LANES), F32)),
        grid=(n_tiles,),
        in_specs=[BS((tm, d), lambda i: (i, 0)),
                  BS((1, SUBLANES, d), lambda i: (i // tiles_per_batch, 0, 0)),
                  BS((1, d), lambda i: (0, 0)),
                  BS((N_EXPERTS, d), lambda i: (0, 0)),
                  BS((N_EXPERTS, 1), lambda i: (0, 0)),
                  BS((tm, tm), lambda i: (0, 0))],
        out_specs=(BS((tm, d), lambda i: (i, 0)),
                   BS((TOP_K, tm), lambda i: (0, i)),
                   BS((TOP_K, tm), lambda i: (0, i)),
                   table_spec, table_spec, table_spec,
                   BS((N_EXPERTS, LANES), lambda i: (0, 0))),
        scratch_shapes=[pltpu.VMEM((N_EXPERTS, LANES), F32)],
        compiler_params=_cparams("arbitrary"),
        name="moe_router",
    )(x_flat, mod_lat, g, wr_t, e_bias, tri)


def _run_table_entry(tab_ref, field, e):
    return tab_ref[0, 0, field * N_EXPERTS + e]


def _for_each_run_chunk(tab_ref, fn):
    def per_expert(e, carry):
        loc0 = _run_table_entry(tab_ref, 0, e)
        slot0 = _run_table_entry(tab_ref, 1, e)

        def per_chunk(c, carry2):
            fn(pl.multiple_of(loc0 + c * RUN_ALIGN, RUN_ALIGN),
               pl.multiple_of(slot0 + c * RUN_ALIGN, RUN_ALIGN))
            return carry2

        return lax.fori_loop(0, _run_table_entry(tab_ref, 2, e), per_chunk, carry)

    lax.fori_loop(0, N_EXPERTS, per_expert, 0)


def _wait_run_chunks(copy_of_one_chunk, n_chunks):
    def body(_, carry):
        copy_of_one_chunk.wait()
        return carry

    lax.fori_loop(0, n_chunks, body, 0)


def _dispatch_body(pend_ref, pcnt_ref, nu_ref, tab_ref, tabprev_ref, lslot_ref, h_ref, xs_ref,
                   loc_ref, oh_ref, zbuf_ref, sem, zsem, *, tl, bm, n_blocks):
    i = pl.program_id(0)
    last = pl.num_programs(0) - 1
    buf = i % 2

    def zero_copy(row0):
        return pltpu.make_async_copy(
            zbuf_ref, xs_ref.at[pl.ds(pl.multiple_of(row0, RUN_ALIGN), bm), :], zsem)

    @pl.when(i == 0)
    def _():
        zbuf_ref[...] = jnp.zeros_like(zbuf_ref)

        def start(e, c):
            @pl.when(pcnt_ref[e] > 0)
            def _():
                zero_copy(pend_ref[e] - bm).start()
            return c

        def wait(e, c):
            @pl.when(pcnt_ref[e] > 0)
            def _():
                zero_copy(pend_ref[e] - bm).wait()
            return c

        def start_tail(j, c):
            zero_copy(j * bm).start()
            return c

        def wait_tail(j, c):
            zero_copy(j * bm).wait()
            return c

        lax.fori_loop(0, N_EXPERTS, start, 0)
        lax.fori_loop(nu_ref[0], n_blocks, start_tail, 0)
        lax.fori_loop(0, N_EXPERTS, wait, 0)
        lax.fori_loop(nu_ref[0], n_blocks, wait_tail, 0)

    rows_used = tab_ref[0, 0, 3 * N_EXPERTS]
    ls = lslot_ref[...]
    h = h_ref[...]

    def sort_chunk(c, carry):
        for s in range(SLOT_CHUNK // ONEHOT_ROWS):
            row = (lax.broadcasted_iota(I32, (ONEHOT_ROWS, tl), 0)
                   + (c * SLOT_CHUNK + s * ONEHOT_ROWS))
            onehot = jnp.zeros((ONEHOT_ROWS, tl), F32)
            for k in range(TOP_K):
                onehot = jnp.where(row == ls[k:k + 1, :], 1.0, onehot)
            oh_ref[s * ONEHOT_ROWS:(s + 1) * ONEHOT_ROWS, :] = onehot.astype(BF16)
        rows = pl.ds(pl.multiple_of(c * SLOT_CHUNK, SLOT_CHUNK), SLOT_CHUNK)
        loc_ref[buf, rows, :] = _dot(oh_ref[...], h).astype(BF16)
        return carry

    lax.fori_loop(0, (rows_used + SLOT_CHUNK - 1) // SLOT_CHUNK, sort_chunk, 0)

    def run_copy(b, loc0, slot0):
        return pltpu.make_async_copy(loc_ref.at[b, pl.ds(loc0, RUN_ALIGN), :],
                                     xs_ref.at[pl.ds(slot0, RUN_ALIGN), :], sem.at[b])

    _for_each_run_chunk(tab_ref, lambda loc0, slot0: run_copy(buf, loc0, slot0).start())

    @pl.when(i > 0)
    def _():
        _wait_run_chunks(run_copy(1 - buf, 0, 0), tabprev_ref[0, 0, 3 * N_EXPERTS] // RUN_ALIGN)

    @pl.when(i == last)
    def _():
        _wait_run_chunks(run_copy(buf, 0, 0), rows_used // RUN_ALIGN)


def _dispatch(pend, pcnt, n_used, run_table, lslot, h2b, n_slots):
    t, d = h2b.shape
    tl = MOE_TILE
    body = functools.partial(_dispatch_body, tl=tl, bm=BM_FFN, n_blocks=n_slots // BM_FFN)
    table_spec = lambda index: BS((1, 1, RUN_TABLE_WIDTH), index, memory_space=pltpu.SMEM)
    grid_spec = pltpu.PrefetchScalarGridSpec(
        num_scalar_prefetch=3,
        grid=(t // tl,),
        in_specs=[table_spec(lambda i, *_: (i, 0, 0)),
                  table_spec(lambda i, *_: (jnp.maximum(i - 1, 0), 0, 0)),
                  BS((TOP_K, tl), lambda i, *_: (0, i)),
                  BS((tl, d), lambda i, *_: (i, 0))],
        out_specs=BS(memory_space=pl.ANY),
        scratch_shapes=[pltpu.VMEM((2, LOCAL_SLOTS, d), BF16), pltpu.VMEM((SLOT_CHUNK, tl), BF16),
                        pltpu.VMEM((BM_FFN, d), BF16),
                        pltpu.SemaphoreType.DMA((2,)), pltpu.SemaphoreType.DMA(())],
    )
    return pl.pallas_call(
        body,
        out_shape=SDS((n_slots, d), BF16),
        grid_spec=grid_spec,
        compiler_params=_cparams("arbitrary"),
        name="moe_dispatch",
    )(pend, pcnt, n_used, run_table, run_table, lslot, h2b)


def _ffn_body(be_ref, nu_ref, xs_ref, wg_ref, wu_ref, wd_ref, ys_ref, wgb_ref, wub_ref, wdb_ref):
    i = pl.program_id(0)
    fresh = jnp.logical_or(i == 0, be_ref[i] != be_ref[jnp.maximum(i - 1, 0)])

    @pl.when(jnp.logical_and(fresh, i < nu_ref[0]))
    def _():
        wgb_ref[...] = wg_ref[...].astype(BF16)
        wub_ref[...] = wu_ref[...].astype(BF16)
        wdb_ref[...] = wd_ref[...].astype(BF16)

    @pl.when(i < nu_ref[0])
    def _():
        x = xs_ref[...]
        a = _silu(_dot(x, wgb_ref[...])) * _dot(x, wub_ref[...])
        ys_ref[...] = _dot(a.astype(BF16), wdb_ref[...]).astype(BF16)

    @pl.when(i >= nu_ref[0])
    def _():
        ys_ref[...] = jnp.zeros_like(ys_ref)


def _ffn(blk_e, n_used, xs, w_gate, w_up, w_down, layer):
    n_slots, d = xs.shape
    bm = BM_FFN
    de = w_gate.shape[-1]
    grid_spec = pltpu.PrefetchScalarGridSpec(
        num_scalar_prefetch=2,
        grid=(n_slots // bm,),
        in_specs=[BS((bm, d), lambda i, be, nu: (jnp.minimum(i, nu[0] - 1), 0)),
                  BS((None, None, d, de), lambda i, be, nu: (layer, be[i], 0, 0)),
                  BS((None, None, d, de), lambda i, be, nu: (layer, be[i], 0, 0)),
                  BS((None, None, de, d), lambda i, be, nu: (layer, be[i], 0, 0))],
        out_specs=BS((bm, d), lambda i, be, nu: (i, 0)),
        scratch_shapes=[pltpu.VMEM((d, de), BF16), pltpu.VMEM((d, de), BF16),
                        pltpu.VMEM((de, d), BF16)],
    )
    return pl.pallas_call(
        _ffn_body,
        out_shape=SDS((n_slots, d), BF16),
        grid_spec=grid_spec,
        compiler_params=_cparams("arbitrary"),
        name="moe_experts",
    )(blk_e, n_used, xs, w_gate, w_up, w_down)


def _combine_body(tab_ref, tabnext_ref, ys_ref, lslot_ref, wt_ref, x_ref, h2b_ref, mod_ref, wsg_ref,
                  wsu_ref, wsd_ref, fg_ref, o_ref, loc_ref, acc_ref, lsb_ref, wtb_ref, ghi_ref,
                  glo_ref, sem, *, tl, final):
    i = pl.program_id(0)
    last = pl.num_programs(0) - 1
    buf = i % 2

    def run_copy(b, loc0, slot0):
        return pltpu.make_async_copy(ys_ref.at[pl.ds(slot0, RUN_ALIGN), :],
                                     loc_ref.at[b, pl.ds(loc0, RUN_ALIGN), :], sem.at[b])

    @pl.when(i == 0)
    def _():
        loc_ref[...] = jnp.zeros_like(loc_ref)
        _for_each_run_chunk(tab_ref, lambda loc0, slot0: run_copy(buf, loc0, slot0).start())

    @pl.when(i < last)
    def _():
        _for_each_run_chunk(tabnext_ref, lambda loc0, slot0: run_copy(1 - buf, loc0, slot0).start())

    hb = h2b_ref[...]
    a = _silu(_dot(hb, wsg_ref[...])) * _dot(hb, wsu_ref[...])
    acc_ref[...] = _dot(a.astype(BF16), wsd_ref[...])
    ls = lslot_ref[...]
    wt = wt_ref[...]
    for k in range(TOP_K):
        lsb_ref[k] = jnp.broadcast_to(ls[:, k:k + 1], (tl, LANES))
        wtb_ref[k] = jnp.broadcast_to(wt[:, k:k + 1], (tl, LANES))

    rows_used = tab_ref[0, 0, 3 * N_EXPERTS]
    _wait_run_chunks(run_copy(buf, 0, 0), rows_used // RUN_ALIGN)

    def unsort_chunk(c, carry):
        def gate_rows(s, carry2):
            rows = pl.ds(pl.multiple_of(s * GATE_ROWS, GATE_ROWS), GATE_ROWS)
            for q in range(SLOT_CHUNK // LANES):
                col = (lax.broadcasted_iota(I32, (GATE_ROWS, LANES), 1)
                       + (c * SLOT_CHUNK + q * LANES))
                gate = jnp.zeros((GATE_ROWS, LANES), F32)
                for k in range(TOP_K):
                    gate = jnp.where(col == lsb_ref[k, rows, :], wtb_ref[k, rows, :], gate)
                hi = gate.astype(BF16)
                ghi_ref[rows, q * LANES:(q + 1) * LANES] = hi
                glo_ref[rows, q * LANES:(q + 1) * LANES] = (gate - hi.astype(F32)).astype(BF16)
            return carry2

        lax.fori_loop(0, tl // GATE_ROWS, gate_rows, 0)
        y = loc_ref[buf, pl.ds(pl.multiple_of(c * SLOT_CHUNK, SLOT_CHUNK), SLOT_CHUNK), :]
        acc_ref[...] = acc_ref[...] + _dot(ghi_ref[...], y) + _dot(glo_ref[...], y)
        return carry

    lax.fori_loop(0, (rows_used + SLOT_CHUNK - 1) // SLOT_CHUNK, unsort_chunk, 0)

    xo = x_ref[...] + mod_ref[0, 5:6, :] * acc_ref[...]
    if final:
        ms = jnp.mean(xo * xo, axis=-1, keepdims=True)
        xo = (xo * lax.rsqrt(ms + EPS)) * fg_ref[...]
    o_ref[...] = xo


def _combine(run_table, ys, lslot_tk, wt_tk, x_flat, h2b, mod_lat, wsg, wsu, wsd, final_g,
             tokens_per_batch, final):
    t, d = x_flat.shape
    tl = MOE_TILE
    ds = wsg.shape[1]
    tiles_per_batch = tokens_per_batch // tl
    body = functools.partial(_combine_body, tl=tl, final=final)
    n_tiles = t // tl
    table_spec = lambda index: BS((1, 1, RUN_TABLE_WIDTH), index, memory_space=pltpu.SMEM)
    return pl.pallas_call(
        body,
        out_shape=SDS((t, d), F32),
        grid=(n_tiles,),
        in_specs=[table_spec(lambda i: (i, 0, 0)),
                  table_spec(lambda i: (jnp.minimum(i + 1, n_tiles - 1), 0, 0)),
                  BS(memory_space=pl.ANY),
                  BS((tl, TOP_K), lambda i: (i, 0)),
                  BS((tl, TOP_K), lambda i: (i, 0)),
                  BS((tl, d), lambda i: (i, 0)),
                  BS((tl, d), lambda i: (i, 0)),
                  BS((1, SUBLANES, d), lambda i: (i // tiles_per_batch, 0, 0)),
                  BS((d, ds), lambda i: (0, 0)),
                  BS((d, ds), lambda i: (0, 0)),
                  BS((ds, d), lambda i: (0, 0)),
                  BS((1, d), lambda i: (0, 0))],
        out_specs=BS((tl, d), lambda i: (i, 0)),
        scratch_shapes=[pltpu.VMEM((2, LOCAL_SLOTS, d), BF16), pltpu.VMEM((tl, d), F32),
                        pltpu.VMEM((TOP_K, tl, LANES), I32), pltpu.VMEM((TOP_K, tl, LANES), F32),
                        pltpu.VMEM((tl, SLOT_CHUNK), BF16), pltpu.VMEM((tl, SLOT_CHUNK), BF16),
                        pltpu.SemaphoreType.DMA((2,))],
        compiler_params=_cparams("arbitrary"),
        name="moe_combine",
    )(run_table, run_table, ys, lslot_tk, wt_tk, x_flat, h2b, mod_lat, wsg, wsu, wsd, final_g)


def _moe(x_flat, mod_lat, g, w_router, e_bias, w_gate, w_up, w_down, ws_gate, ws_up, ws_down,
         final_g, tri, tokens_per_batch, layer, final):
    t, d = x_flat.shape
    bm = BM_FFN
    n_tiles = t // MOE_TILE
    h2b, lslot, wts, loff, gbase, runlen, seg = _router(
        x_flat, mod_lat, g, w_router.T, e_bias.reshape(N_EXPERTS, 1), tri, tokens_per_batch)
    seg_rows = seg[:, 0].astype(I32)
    pcnt = (seg_rows + bm - 1) // bm * bm
    pend = jnp.cumsum(pcnt).astype(I32)
    pstart = pend - pcnt
    max_rows = t * TOP_K + n_tiles * N_EXPERTS * (RUN_ALIGN - 1)
    n_blocks = -(-max_rows // bm) + N_EXPERTS
    n_slots = n_blocks * bm
    block_row0 = jnp.arange(n_blocks, dtype=I32) * bm
    blk_e = jnp.minimum(jnp.sum((pend[None, :] <= block_row0[:, None]).astype(I32), axis=1),
                        N_EXPERTS - 1)
    n_used = pend[-1:] // bm
    loff_i = loff[:, :, 0].astype(I32)
    runlen_i = runlen[:, :, 0].astype(I32)
    slot0 = pstart[None, :] + gbase[:, :, 0].astype(I32)
    rows_used = jnp.broadcast_to((loff_i[:, -1] + runlen_i[:, -1])[:, None], (n_tiles, N_EXPERTS))
    run_table = jnp.concatenate([loff_i, slot0, runlen_i // RUN_ALIGN, rows_used],
                                axis=1).reshape(n_tiles, 1, RUN_TABLE_WIDTH)
    xs = _dispatch(pend, pcnt, n_used, run_table, lslot, h2b, n_slots)
    ys = _ffn(blk_e, n_used, xs, w_gate, w_up, w_down, layer)
    return _combine(run_table, ys, lslot.T, wts.T, x_flat, h2b, mod_lat,
                    ws_gate.astype(BF16), ws_up.astype(BF16), ws_down.astype(BF16),
                    final_g.reshape(1, d), tokens_per_batch, final)


def _inproj_c_body(xp_ref, x_ref, xn_ref, mod_ref, g_ref, w_ref, cw_ref, cb_ref,
                   v_ref, g1_ref, g2_ref, *, tm, n_tiles):
    i = pl.program_id(1)
    halo = SUBLANES
    xe = jnp.concatenate([xp_ref[0], x_ref[0], xn_ref[0]], axis=0)
    h = _norm_mod(xe, g_ref[...], mod_ref[0, 1:2, :], mod_ref[0, 0:1, :])
    row = lax.broadcasted_iota(I32, (tm + 2 * halo, 1), 0)
    outside = jnp.logical_or(jnp.logical_and(i == 0, row < halo),
                             jnp.logical_and(i == n_tiles - 1, row >= tm + halo))
    hb = jnp.where(outside, 0.0, h).astype(BF16)
    width = v_ref.shape[-1]
    for part, o_ref in enumerate((v_ref, g1_ref, g2_ref)):
        cols = slice(part * width, (part + 1) * width)
        zp = _dot(hb, w_ref[:, cols])
        up = pltpu.roll(zp, 1, 0)
        dn = pltpu.roll(zp, tm + 2 * halo - 1, 0)
        z = cw_ref[0:1, cols] * up + cw_ref[1:2, cols] * zp + cw_ref[2:3, cols] * dn + cb_ref[:, cols]
        o_ref[0] = z[halo:halo + tm]


def _inproj_c(x, mod_lat, g, w_in, conv_w, conv_b):
    b, l, d = x.shape
    tm = TM_PROJ
    n_tiles = l // tm
    w3 = w_in.shape[1]
    width = w3 // 3
    r8 = tm // SUBLANES
    body = functools.partial(_inproj_c_body, tm=tm, n_tiles=n_tiles)
    out = SDS((b, l, width), F32)
    return pl.pallas_call(
        body,
        out_shape=(out, out, out),
        grid=(b, n_tiles),
        in_specs=[BS((1, SUBLANES, d), lambda bi, i: (bi, jnp.maximum(i * r8 - 1, 0), 0)),
                  BS((1, tm, d), lambda bi, i: (bi, i, 0)),
                  BS((1, SUBLANES, d), lambda bi, i: (bi, jnp.minimum((i + 1) * r8, l // SUBLANES - 1), 0)),
                  BS((1, SUBLANES, d), lambda bi, i: (bi, 0, 0)),
                  BS((1, d), lambda bi, i: (0, 0)),
                  BS((d, w3), lambda bi, i: (0, 0)),
                  BS((3, w3), lambda bi, i: (0, 0)),
                  BS((1, w3), lambda bi, i: (0, 0))],
        out_specs=(BS((1, tm, width), lambda bi, i: (bi, i, 0)),
                   BS((1, tm, width), lambda bi, i: (bi, i, 0)),
                   BS((1, tm, width), lambda bi, i: (bi, i, 0))),
        compiler_params=_cparams("arbitrary", "arbitrary"),
        name="inproj_c",
    )(x, x, x, mod_lat, g, w_in, conv_w, conv_b)


def _filter_body(f_ref, w1_ref, b1_ref, w2_ref, b2_ref, w3_ref, fr_ref, dl_ref, keep0_ref,
                 hf_ref, l1_ref, *, tp):
    i = pl.program_id(0)
    feats = f_ref[...]
    fr = fr_ref[...]
    a = jnp.sin(fr * (_dot_hp(feats, w1_ref[...]) + b1_ref[...]))
    a = jnp.sin(fr * (_dot_hp(a, w2_ref[...]) + b2_ref[...]))
    hf = _dot_hp(a, w3_ref[...])
    t01 = feats[:, 0:1]
    hf = hf * (jnp.exp(-t01 * jnp.abs(dl_ref[...])) + DECAY_SHIFT)
    row = lax.broadcasted_iota(I32, hf.shape, 0) + i * tp
    hf = jnp.where(row == 0, hf * keep0_ref[...], hf)
    hf_ref[...] = hf

    @pl.when(i == 0)
    def _():
        l1_ref[...] = jnp.zeros_like(l1_ref)

    l1_ref[...] = l1_ref[...] + jnp.sum(jnp.abs(hf), axis=0, keepdims=True)


def _filters(feats, w1, b1, w2, b2, w3, freq, delta, width):
    n, fe = feats.shape
    hid = w2.shape[0]
    fo = w3.shape[1]
    tp = 256
    lag0_keep = jnp.tile(jnp.repeat(jnp.array([1.0, 0.0], F32), width), fo // (2 * width)).reshape(1, fo)
    body = functools.partial(_filter_body, tp=tp)
    full = lambda shape: BS(shape, lambda i: (0, 0))
    return pl.pallas_call(
        body,
        out_shape=(SDS((n, fo), F32), SDS((SUBLANES, fo), F32)),
        grid=(n // tp,),
        in_specs=[BS((tp, fe), lambda i: (i, 0)), full((fe, hid)), full((1, hid)),
                  full((hid, hid)), full((1, hid)), full((hid, fo)), full((1, hid)), full((1, fo)),
                  full((1, fo))],
        out_specs=(BS((tp, fo), lambda i: (i, 0)), BS((SUBLANES, fo), lambda i: (0, 0))),
        compiler_params=_cparams("arbitrary"),
        name="hyena_filters",
    )(feats, w1, b1, w2, b2, w3, freq, delta, lag0_keep)


DFT_R = 128


def _dft_tables(n):
    r = DFT_R
    m = 2 * n
    na = n // r
    two_pi = 2.0 * np.pi
    a = np.arange(na)[None, :]
    v = np.arange(r)[:, None]
    ang1 = two_pi * ((a * v) % r) / r
    f1 = np.concatenate([np.cos(ang1), -np.sin(ang1)], axis=0)
    b = np.arange(r)[None, None, :]
    u = np.arange(r)[None, :, None]
    vv = np.arange(r)[:, None, None]
    ang2 = two_pi * ((b * (r * u + vv)) % m) / m
    gr, gi = np.cos(ang2), -np.sin(ang2)
    fwd = np.concatenate([np.concatenate([gr, -gi], axis=2),
                          np.concatenate([gi, gr], axis=2)], axis=1)
    hr, hi = np.transpose(gr, (0, 2, 1)), -np.transpose(gi, (0, 2, 1))
    inv = np.concatenate([np.concatenate([hr, -hi], axis=2),
                          np.concatenate([hi, hr], axis=2)], axis=1)
    ang3 = two_pi * ((np.arange(na)[:, None] * np.arange(r)[None, :]) % r) / r
    f3 = np.concatenate([np.cos(ang3), -np.sin(ang3)], axis=1) / m
    cast = lambda t: jnp.asarray(t.astype(np.float32)).astype(BF16)
    return cast(f1), cast(fwd), cast(inv), cast(f3)


def _dft_s1_body(y_ref, f1_ref, ar_ref, ai_ref):
    f1 = f1_ref[...]
    for j in range(SUBLANES):
        res = _dot(f1, y_ref[:, j, :].astype(BF16))
        ar_ref[:, j, :] = res[:DFT_R]
        ai_ref[:, j, :] = res[DFT_R:]


def _dft_s1(y4, f1):
    nb, na, r, c = y4.shape
    ct = min(c, 1024)
    out = SDS((nb, r, r, c), F32)
    return pl.pallas_call(
        _dft_s1_body,
        out_shape=(out, out),
        grid=(nb, c // ct, r // SUBLANES),
        in_specs=[BS((None, na, SUBLANES, ct), lambda n, cc, j: (n, 0, j, cc)),
                  BS((2 * r, na), lambda n, cc, j: (0, 0))],
        out_specs=(BS((None, r, SUBLANES, ct), lambda n, cc, j: (n, 0, j, cc)),
                   BS((None, r, SUBLANES, ct), lambda n, cc, j: (n, 0, j, cc))),
        compiler_params=_cparams("arbitrary", "arbitrary", "arbitrary"),
        name="dft_stage1",
    )(y4, f1)


def _filter_spec_body(arf_ref, aif_ref, arb_ref, aib_ref, g_ref, l1f_ref, l1b_ref, kr_ref, ki_ref):
    g = g_ref[...]
    yf = _dot(g, jnp.concatenate([arf_ref[...], aif_ref[...]], axis=0).astype(BF16))
    yb = _dot(g, jnp.concatenate([arb_ref[...], aib_ref[...]], axis=0).astype(BF16))
    inv = 1.0 / (l1f_ref[0:1, :] + l1b_ref[0:1, :])
    kr_ref[...] = (yf[:DFT_R] + yb[:DFT_R]) * inv
    ki_ref[...] = (yf[DFT_R:] - yb[DFT_R:]) * inv


def _filter_spectrum(ar, ai, fwd, l1, width):
    r = DFT_R
    a_spec = lambda d: BS((None, None, r, width), lambda v, o: (0, v, 0, 2 * o + d))
    l_spec = lambda d: BS((SUBLANES, width), lambda v, o: (0, 2 * o + d))
    out = SDS((r, r, HYENA_ORDER * width), F32)
    return pl.pallas_call(
        _filter_spec_body,
        out_shape=(out, out),
        grid=(r, HYENA_ORDER),
        in_specs=[a_spec(0), a_spec(0), a_spec(1), a_spec(1),
                  BS((None, 2 * r, 2 * r), lambda v, o: (v, 0, 0)), l_spec(0), l_spec(1)],
        out_specs=(BS((None, r, width), lambda v, o: (v, 0, o)),
                   BS((None, r, width), lambda v, o: (v, 0, o))),
        compiler_params=_cparams("arbitrary", "arbitrary"),
        name="hyena_filter_spectrum",
    )(ar, ai, ar, ai, fwd, l1, l1)


def _conv_mid_body(ar_ref, ai_ref, g_ref, h_ref, kr_ref, ki_ref, qr_ref, qi_ref):
    y = _dot(g_ref[...], jnp.concatenate([ar_ref[...], ai_ref[...]], axis=0).astype(BF16))
    yr, yi = y[:DFT_R], y[DFT_R:]
    kr, ki = kr_ref[...], ki_ref[...]
    p = jnp.concatenate([yr * kr - yi * ki, yr * ki + yi * kr], axis=0).astype(BF16)
    q = _dot(h_ref[...], p)
    qr_ref[...] = q[:DFT_R]
    qi_ref[...] = q[DFT_R:]


def _conv_mid(ar, ai, fwd, inv, kr, ki, order):
    nb, r, _, c = ar.shape
    a_spec = BS((None, None, r, c), lambda n, v: (n, v, 0, 0))
    m_spec = BS((None, 2 * r, 2 * r), lambda n, v: (v, 0, 0))
    k_spec = BS((None, r, c), lambda n, v: (v, 0, order))
    out = SDS((nb, r, r, c), F32)
    return pl.pallas_call(
        _conv_mid_body,
        out_shape=(out, out),
        grid=(nb, r),
        in_specs=[a_spec, a_spec, m_spec, m_spec, k_spec, k_spec],
        out_specs=(a_spec, a_spec),
        compiler_params=_cparams("arbitrary", "arbitrary"),
        name="hyena_spectral_product",
    )(ar, ai, fwd, inv, kr, ki)


def _idft_gate_body(qr_ref, qi_ref, f3_ref, y_ref, gate_ref, fb_ref, o_ref):
    f3 = f3_ref[...]
    fb = fb_ref[...]
    for j in range(SUBLANES):
        q = jnp.concatenate([qr_ref[:, j, :], qi_ref[:, j, :]], axis=0).astype(BF16)
        conv = _dot(f3, q)
        o_ref[:, j, :] = gate_ref[:, j, :] * (conv + fb * y_ref[:, j, :])


def _idft_gate(qr, qi, f3, y4, gate4, fbias):
    nb, na, r, c = y4.shape
    q_spec = BS((None, r, SUBLANES, c), lambda n, j: (n, 0, j, 0))
    y_spec = BS((None, na, SUBLANES, c), lambda n, j: (n, 0, j, 0))
    return pl.pallas_call(
        _idft_gate_body,
        out_shape=SDS((nb, na, r, c), F32),
        grid=(nb, r // SUBLANES),
        in_specs=[q_spec, q_spec, BS((na, 2 * r), lambda n, j: (0, 0)), y_spec, y_spec,
                  BS((1, c), lambda n, j: (0, 0))],
        out_specs=y_spec,
        compiler_params=_cparams("arbitrary", "arbitrary"),
        name="hyena_idft_gate",
    )(qr, qi, f3, y4, gate4, fbias)


def _outproj_body(y_ref, w_ref, x_ref, mod_ref, o_ref):
    o_ref[0] = x_ref[0] + mod_ref[0, 2:3, :] * _dot(y_ref[0].astype(BF16), w_ref[...])


def _outproj(y, w_out, x, mod_lat):
    b, l, d = x.shape
    tm = TM_PROJ
    wdt = y.shape[-1]
    return pl.pallas_call(
        _outproj_body,
        out_shape=SDS((b, l, d), F32),
        grid=(b, l // tm),
        in_specs=[BS((1, tm, wdt), lambda bi, i: (bi, i, 0)),
                  BS((wdt, d), lambda bi, i: (0, 0)),
                  BS((1, tm, d), lambda bi, i: (bi, i, 0)),
                  BS((1, SUBLANES, d), lambda bi, i: (bi, 0, 0))],
        out_specs=BS((1, tm, d), lambda bi, i: (bi, i, 0)),
        compiler_params=_cparams("arbitrary", "arbitrary"),
        name="outproj_c",
    )(y, w_out, x, mod_lat)


def _hyena(x, mod_lat, g, w_in, conv_w, conv_b, w1, b1, w2, b2, w3, freq, delta, f_bias, w_out):
    b, n, d = x.shape
    width = w_out.shape[0]
    r = DFT_R
    na = n // r
    f1, fwd, inv, f3 = _dft_tables(n)
    v, gate1, gate2 = _inproj_c(x, mod_lat, g, w_in.astype(BF16), conv_w, conv_b.reshape(1, -1))

    t = jnp.arange(n, dtype=F32)
    t01 = t / max(n - 1, 1)
    bands = jnp.linspace(1e-4, FILT_BANDS - 1, FILT_BANDS, dtype=F32)
    ang = (2.0 * math.pi / n) * t[:, None] * bands[None, :]
    feats = jnp.concatenate([t01[:, None], jnp.cos(ang), jnp.sin(ang)], axis=-1)
    fe = feats.shape[1]
    feats = jnp.pad(feats, ((0, 0), (0, LANES - fe)))
    w1p = jnp.pad(w1, ((0, LANES - fe), (0, 0)))
    hf, l1 = _filters(feats, w1p, b1.reshape(1, -1), w2, b2.reshape(1, -1), w3,
                      freq.reshape(1, -1), delta.reshape(1, -1), width)
    far, fai = _dft_s1(hf.reshape(1, na, r, hf.shape[1]), f1)
    kr, ki = _filter_spectrum(far, fai, fwd, l1, width)

    y4 = v.reshape(b, na, r, width)
    for o, gate in enumerate((gate1, gate2)):
        ar, ai = _dft_s1(y4, f1)
        qr, qi = _conv_mid(ar, ai, fwd, inv, kr, ki, o)
        y4 = _idft_gate(qr, qi, f3, y4, gate.reshape(b, na, r, width), f_bias[o].reshape(1, width))
    return _outproj(y4.reshape(b, n, width), w_out.astype(BF16), x, mod_lat)


def _rope_tables(seq_len):
    rows = seq_len // GRID_W
    row = jnp.repeat(jnp.arange(rows, dtype=F32), GRID_W)
    col = jnp.tile(jnp.arange(GRID_W, dtype=F32), rows)
    inv = jnp.power(ROPE_BASE, -jnp.arange(ROPE_FREQS, dtype=F32) / ROPE_FREQS)
    ar, ac = row[:, None] * inv, col[:, None] * inv
    cos_h = jnp.concatenate([jnp.cos(ar), jnp.cos(ar), jnp.cos(ac), jnp.cos(ac)], axis=1)
    sin_h = jnp.concatenate([-jnp.sin(ar), jnp.sin(ar), -jnp.sin(ac), jnp.sin(ac)], axis=1)
    reps = LANES // HEAD_DIM
    return jnp.tile(cos_h, (1, reps)), jnp.tile(sin_h, (1, reps))


def _rotate_partner_columns(w):
    ncol = w.shape[1]
    lane = np.arange(ncol)
    partner = np.where((lane % (2 * ROPE_FREQS)) < ROPE_FREQS, lane + ROPE_FREQS, lane - ROPE_FREQS)
    return w[:, partner]


def kernel(x, c, ctx, c_ctx, w_mod, b_mod, norm_g, w_in_ab, sink, w_spatial, b_spatial, w_out_ab,
           w_in_c, conv_w, conv_b, filt_w1, filt_b1, filt_w2, filt_b2, filt_w3, filt_freq,
           filt_delta, filt_bias, w_out_c, w_router, e_bias, w_gate, w_up, w_down, ws_gate,
           ws_up, ws_down, final_g):
    b, l, d = x.shape
    depth = w_mod.shape[0]
    assert depth == 2 and b + 1 <= SUBLANES

    cc = jnp.zeros((SUBLANES, d), F32).at[:b].set(c).at[b].set(c_ctx)
    m_all = _mod_vectors(cc, w_mod, b_mod)

    def mod_rows(layer, row0, nrow):
        m = m_all[layer, row0:row0 + nrow].reshape(nrow, 6, d)
        return jnp.pad(m, ((0, 0), (0, SUBLANES - 6), (0, 0)))

    tri = jnp.triu(jnp.ones((MOE_TILE, MOE_TILE), F32), k=1).astype(BF16)

    mod_lat = mod_rows(0, 0, b)
    mod_ctx = mod_rows(0, b, 1)[0]
    w_in = w_in_ab[0]
    qk = ATTN_WIDTH + KV_WIDTH
    w_cat = jnp.concatenate([w_in, _rotate_partner_columns(w_in[:, :qk])], axis=1).astype(BF16)
    cos_t, sin_t = _rope_tables(l)
    group_avg = jnp.kron(jnp.eye(N_SG_GROUPS, dtype=F32),
                         jnp.full((SG_GROUP_DIM, SG_GROUP_DIM), 1.0 / SG_GROUP_DIM, F32)).astype(BF16)
    kc, vc = _ctx_kv(ctx, mod_ctx, norm_g[0, 0].reshape(1, d),
                     w_in[:, ATTN_WIDTH:ATTN_WIDTH + 2 * KV_WIDTH].astype(BF16))
    q, k, v, ug, vn = _inproj_ab(x, mod_lat, norm_g[0, 0].reshape(1, d), w_cat, cos_t, sin_t, group_avg)
    b_full = jnp.repeat(b_spatial[0].T, SG_GROUP_DIM, axis=1)
    x1 = _mixer(sink[0], q, k, v, kc, vc, ug, vn, w_spatial[0].astype(BF16), b_full,
                w_out_ab[0].astype(BF16), x, mod_lat)
    x2 = _moe(x1.reshape(b * l, d), mod_lat, norm_g[0, 1].reshape(1, d), w_router[0], e_bias[0],
              w_gate, w_up, w_down, ws_gate[0], ws_up[0], ws_down[0], final_g, tri, l,
              layer=0, final=False).reshape(b, l, d)

    mod_lat = mod_rows(1, 0, b)
    x3 = _hyena(x2, mod_lat, norm_g[1, 0].reshape(1, d), w_in_c[0], conv_w[0], conv_b[0],
                filt_w1[0], filt_b1[0], filt_w2[0], filt_b2[0], filt_w3[0], filt_freq[0],
                filt_delta[0], filt_bias[0], w_out_c[0])
    out = _moe(x3.reshape(b * l, d), mod_lat, norm_g[1, 1].reshape(1, d), w_router[1], e_bias[1],
               w_gate, w_up, w_down, ws_gate[1], ws_up[1], ws_down[1], final_g, tri, l,
               layer=1, final=True)
    return out.reshape(b, l, d)
```

```python
import functools
import math

import numpy as np
import jax
import jax.numpy as jnp
from jax import lax
from jax.experimental import pallas as pl
from jax.experimental.pallas import tpu as pltpu

F32 = jnp.float32
BF16 = jnp.bfloat16
I32 = jnp.int32
HIGHEST = lax.Precision.HIGHEST
SDS = jax.ShapeDtypeStruct
BS = pl.BlockSpec

EPS = 1e-6
NEG = -1e30

GRID_W = 64
N_Q_HEADS = 8
N_KV_HEADS = 2
HEAD_DIM = 64
ATTN_WIDTH = N_Q_HEADS * HEAD_DIM
KV_WIDTH = N_KV_HEADS * HEAD_DIM
WINDOW = 128
BLOCK = 128
ROPE_BASE = 10000.0
ROPE_FREQS = HEAD_DIM // 4
N_SG_GROUPS = 8
SG_GROUP_DIM = 64
SG_WIDTH = N_SG_GROUPS * SG_GROUP_DIM
HYENA_ORDER = 2
FILT_BANDS = 16
DECAY_SHIFT = 0.05
N_EXPERTS = 64
TOP_K = 8
N_GROUPS = 8
TOPK_GROUPS = 4
ROUTED_SCALE = 2.5

LANES = 128
SUBLANES = 8
VMEM_LIMIT = 56 * 1024 * 1024

TM_PROJ = 512
TQ_MIX = 256
MOE_TILE = 256
BM_FFN = 512
RUN_ALIGN = 16
SLOT_CHUNK = 512
ONEHOT_ROWS = 64
GATE_ROWS = 32
LOCAL_SLOTS = -(-(TOP_K * MOE_TILE + N_EXPERTS * (RUN_ALIGN - 1)) // SLOT_CHUNK) * SLOT_CHUNK
CHUNK_TABLE = -(-(LOCAL_SLOTS // RUN_ALIGN) // LANES) * LANES
FFN_IN_BUFS = 4


def _cparams(*sem):
    return pltpu.CompilerParams(dimension_semantics=sem, vmem_limit_bytes=VMEM_LIMIT)


def _dot(a, b):
    return jnp.dot(a, b, preferred_element_type=F32)


def _dot_nt(a, b):
    return lax.dot_general(a, b, (((1,), (1,)), ((), ())), preferred_element_type=F32)


def _dot_hp(a, b):
    return jnp.dot(a, b, preferred_element_type=F32, precision=HIGHEST)


def _norm_mod(x, g, sc, sh):
    ms = jnp.mean(x * x, axis=-1, keepdims=True)
    y = x * lax.rsqrt(ms + EPS)
    return (y * g) * (1.0 + sc) + sh


def _gelu_tanh(x):
    c = math.sqrt(2.0 / math.pi)
    return 0.5 * x * (1.0 + jnp.tanh(c * (x + 0.044715 * (x * x * x))))


def _silu(x):
    return x * jax.nn.sigmoid(x)


def _mod_body(c_ref, w_ref, b_ref, o_ref):
    o_ref[0] = _dot_hp(_silu(c_ref[...]), w_ref[0]) + b_ref[0]


def _mod_vectors(cc, w_mod, b_mod):
    depth, d, n = w_mod.shape
    tn = 1536
    return pl.pallas_call(
        _mod_body,
        out_shape=SDS((depth, SUBLANES, n), F32),
        grid=(depth, n // tn),
        in_specs=[BS((SUBLANES, d), lambda l, j: (0, 0)),
                  BS((1, d, tn), lambda l, j: (l, 0, j)),
                  BS((1, 1, tn), lambda l, j: (l, 0, j))],
        out_specs=BS((1, SUBLANES, tn), lambda l, j: (l, 0, j)),
        compiler_params=_cparams("arbitrary", "arbitrary"),
        name="mod_vectors",
    )(cc, w_mod, b_mod.reshape(depth, 1, n))


def _ctx_kv_body(ctx_ref, mod_ref, g_ref, w_ref, kc_ref, vc_ref):
    h = _norm_mod(ctx_ref[0], g_ref[...], mod_ref[1:2, :], mod_ref[0:1, :])
    z = _dot(h.astype(BF16), w_ref[...])
    kc_ref[0] = z[:, :KV_WIDTH].astype(BF16)
    vc_ref[0] = z[:, KV_WIDTH:].astype(BF16)


def _ctx_kv(ctx, mod_ctx, g, w_kv):
    b, c, d = ctx.shape
    return pl.pallas_call(
        _ctx_kv_body,
        out_shape=(SDS((b, c, KV_WIDTH), BF16), SDS((b, c, KV_WIDTH), BF16)),
        grid=(b,),
        in_specs=[BS((1, c, d), lambda i: (i, 0, 0)),
                  BS((SUBLANES, d), lambda i: (0, 0)),
                  BS((1, d), lambda i: (0, 0)),
                  BS((d, 2 * KV_WIDTH), lambda i: (0, 0))],
        out_specs=(BS((1, c, KV_WIDTH), lambda i: (i, 0, 0)),
                   BS((1, c, KV_WIDTH), lambda i: (i, 0, 0))),
        compiler_params=_cparams("arbitrary"),
        name="ctx_kv",
    )(ctx, mod_ctx, g, w_kv)


def _inproj_ab_body(x_ref, mod_ref, g_ref, w_ref, cos_ref, sin_ref, avg_ref,
                    q_ref, k_ref, v_ref, ug_ref, vn_ref):
    h = _norm_mod(x_ref[0], g_ref[...], mod_ref[0, 1:2, :], mod_ref[0, 0:1, :]).astype(BF16)
    cs = cos_ref[...]
    sn = sin_ref[...]
    rot0 = ATTN_WIDTH + 2 * KV_WIDTH + 2 * SG_WIDTH
    scale = HEAD_DIM ** -0.5
    for j in range(ATTN_WIDTH // LANES):
        z = _dot(h, w_ref[:, j * LANES:(j + 1) * LANES])
        zr = _dot(h, w_ref[:, rot0 + j * LANES:rot0 + (j + 1) * LANES])
        q_ref[0, :, j * LANES:(j + 1) * LANES] = ((z * cs + zr * sn) * scale).astype(BF16)
    zk = _dot(h, w_ref[:, ATTN_WIDTH:ATTN_WIDTH + KV_WIDTH])
    zkr = _dot(h, w_ref[:, rot0 + ATTN_WIDTH:rot0 + ATTN_WIDTH + KV_WIDTH])
    k_ref[0] = (zk * cs + zkr * sn).astype(BF16)
    v_ref[0] = _dot(h, w_ref[:, ATTN_WIDTH + KV_WIDTH:ATTN_WIDTH + 2 * KV_WIDTH]).astype(BF16)
    u0 = ATTN_WIDTH + 2 * KV_WIDTH
    ug_ref[0] = _gelu_tanh(_dot(h, w_ref[:, u0:u0 + SG_WIDTH]))
    vf = _gelu_tanh(_dot(h, w_ref[:, u0 + SG_WIDTH:u0 + 2 * SG_WIDTH]))
    avg = avg_ref[...]

    def gmean(t):
        hi = t.astype(BF16)
        lo = (t - hi.astype(F32)).astype(BF16)
        return _dot(hi, avg) + _dot(lo, avg)

    vc = vf - gmean(vf)
    vn_ref[0] = (vc * lax.rsqrt(gmean(vc * vc) + EPS)).astype(BF16)


def _inproj_ab(x, mod_lat, g, w_cat, cos_t, sin_t, avg):
    b, l, d = x.shape
    tm = TM_PROJ
    ncol = w_cat.shape[1]
    return pl.pallas_call(
        _inproj_ab_body,
        out_shape=(SDS((b, l, ATTN_WIDTH), BF16), SDS((b, l, KV_WIDTH), BF16),
                   SDS((b, l, KV_WIDTH), BF16), SDS((b, l, SG_WIDTH), F32),
                   SDS((b, l, SG_WIDTH), BF16)),
        grid=(b, l // tm),
        in_specs=[BS((1, tm, d), lambda bi, i: (bi, i, 0)),
                  BS((1, SUBLANES, d), lambda bi, i: (bi, 0, 0)),
                  BS((1, d), lambda bi, i: (0, 0)),
                  BS((d, ncol), lambda bi, i: (0, 0)),
                  BS((tm, LANES), lambda bi, i: (i, 0)),
                  BS((tm, LANES), lambda bi, i: (i, 0)),
                  BS((SG_WIDTH, SG_WIDTH), lambda bi, i: (0, 0))],
        out_specs=(BS((1, tm, ATTN_WIDTH), lambda bi, i: (bi, i, 0)),
                   BS((1, tm, KV_WIDTH), lambda bi, i: (bi, i, 0)),
                   BS((1, tm, KV_WIDTH), lambda bi, i: (bi, i, 0)),
                   BS((1, tm, SG_WIDTH), lambda bi, i: (bi, i, 0)),
                   BS((1, tm, SG_WIDTH), lambda bi, i: (bi, i, 0))),
        compiler_params=_cparams("arbitrary", "arbitrary"),
        name="inproj_ab",
    )(x, mod_lat, g, w_cat, cos_t, sin_t, avg)


def _mixer_body(sink_ref, q_ref, kp_ref, kcur_ref, kn_ref, vp_ref, vcur_ref, vn_ref,
                kc_ref, vc_ref, ug_ref, vnorm_ref, ws_ref, bs_ref, wout_ref, x_ref, mod_ref,
                o_ref, cat_ref, *, seq_len, sub_blocks):
    i = pl.program_id(1)
    kk = jnp.concatenate([kp_ref[0], kcur_ref[0], kn_ref[0]], axis=0)
    vv = jnp.concatenate([vp_ref[0], vcur_ref[0], vn_ref[0]], axis=0)
    kc = kc_ref[0]
    vc = vc_ref[0]
    span = 3 * BLOCK
    ii = lax.broadcasted_iota(I32, (BLOCK, span), 0)
    jj = lax.broadcasted_iota(I32, (BLOCK, span), 1)
    dd = jj - ii
    in_window = jnp.where(dd >= 0, jnp.where(dd <= 2 * WINDOW, 1, 0), 0)
    group = N_Q_HEADS // N_KV_HEADS
    for r in range(sub_blocks):
        rows = slice(r * BLOCK, (r + 1) * BLOCK)
        kpos = (i * sub_blocks + r - 1) * BLOCK + jj
        in_seq = jnp.where(kpos >= 0, jnp.where(kpos < seq_len, 1, 0), 0)
        bias = jnp.where(in_window * in_seq > 0, 0.0, NEG)
        qb = q_ref[0, rows, :]
        kl = kk[r * BLOCK:r * BLOCK + span]
        vl = vv[r * BLOCK:r * BLOCK + span]
        for hq in range(N_Q_HEADS):
            hk = hq // group
            ks = slice(hk * HEAD_DIM, (hk + 1) * HEAD_DIM)
            qh = qb[:, hq * HEAD_DIM:(hq + 1) * HEAD_DIM]
            s_loc = _dot_nt(qh, kl[:, ks]) + bias
            s_ctx = _dot_nt(qh, kc[:, ks])
            sk = sink_ref[hq]
            m = jnp.maximum(jnp.maximum(jnp.max(s_loc, axis=-1, keepdims=True),
                                        jnp.max(s_ctx, axis=-1, keepdims=True)), sk)
            p_loc = jnp.exp(s_loc - m)
            p_ctx = jnp.exp(s_ctx - m)
            den = (jnp.sum(p_loc, axis=-1, keepdims=True) + jnp.sum(p_ctx, axis=-1, keepdims=True)
                   + jnp.exp(sk - m))
            o = _dot(p_loc.astype(BF16), vl[:, ks]) + _dot(p_ctx.astype(BF16), vc[:, ks])
            cat_ref[rows, hq * HEAD_DIM:(hq + 1) * HEAD_DIM] = (o / den).astype(BF16)
        vnb = vnorm_ref[0, rows, :]
        ugb = ug_ref[0, rows, :]
        for g in range(N_SG_GROUPS):
            gs = slice(g * SG_GROUP_DIM, (g + 1) * SG_GROUP_DIM)
            sg = _dot(ws_ref[g], vnb[:, gs]) + bs_ref[:, gs]
            cat_ref[rows, ATTN_WIDTH + g * SG_GROUP_DIM:ATTN_WIDTH + (g + 1) * SG_GROUP_DIM] = (
                ugb[:, gs] * sg).astype(BF16)
    y = _dot(cat_ref[...], wout_ref[...])
    o_ref[0] = x_ref[0] + mod_ref[0, 2:3, :] * y


def _mixer(sink, q, k, v, kc, vc, ug, vn, w_s, b_full, w_out, x, mod_lat):
    b, l, d = x.shape
    tq = TQ_MIX
    r = tq // BLOCK
    nb = l // BLOCK
    c = kc.shape[1]
    prev_map = lambda bi, i: (bi, jnp.maximum(i * r - 1, 0), 0)
    next_map = lambda bi, i: (bi, jnp.minimum((i + 1) * r, nb - 1), 0)
    cur_map = lambda bi, i: (bi, i, 0)
    body = functools.partial(_mixer_body, seq_len=l, sub_blocks=r)
    return pl.pallas_call(
        body,
        out_shape=SDS((b, l, d), F32),
        grid=(b, l // tq),
        in_specs=[BS(memory_space=pltpu.SMEM),
                  BS((1, tq, ATTN_WIDTH), cur_map),
                  BS((1, BLOCK, KV_WIDTH), prev_map), BS((1, tq, KV_WIDTH), cur_map),
                  BS((1, BLOCK, KV_WIDTH), next_map),
                  BS((1, BLOCK, KV_WIDTH), prev_map), BS((1, tq, KV_WIDTH), cur_map),
                  BS((1, BLOCK, KV_WIDTH), next_map),
                  BS((1, c, KV_WIDTH), lambda bi, i: (bi, 0, 0)),
                  BS((1, c, KV_WIDTH), lambda bi, i: (bi, 0, 0)),
                  BS((1, tq, SG_WIDTH), cur_map), BS((1, tq, SG_WIDTH), cur_map),
                  BS((N_SG_GROUPS, BLOCK, BLOCK), lambda bi, i: (0, 0, 0)),
                  BS((BLOCK, SG_WIDTH), lambda bi, i: (0, 0)),
                  BS((d, d), lambda bi, i: (0, 0)),
                  BS((1, tq, d), cur_map),
                  BS((1, SUBLANES, d), lambda bi, i: (bi, 0, 0))],
        out_specs=BS((1, tq, d), cur_map),
        scratch_shapes=[pltpu.VMEM((tq, d), BF16)],
        compiler_params=_cparams("arbitrary", "arbitrary"),
        name="mixer_ab",
    )(sink, q, k, k, k, v, v, v, kc, vc, ug, vn, w_s, b_full, w_out, x, mod_lat)


def _router_body(x_ref, mod_ref, g_ref, wr_ref, eb_ref, tri_ref,
                 h2b_ref, lslot_ref, wts_ref, chunk_e_ref, chunk_rel_ref, rows_ref, cnt_ref,
                 carry_ref, *, tm):
    i = pl.program_id(0)

    @pl.when(i == 0)
    def _():
        carry_ref[...] = jnp.zeros_like(carry_ref)

    h2 = _norm_mod(x_ref[...], g_ref[...], mod_ref[0, 4:5, :], mod_ref[0, 3:4, :])
    h2b_ref[...] = h2.astype(BF16)

    logits = lax.dot_general(wr_ref[...], h2, (((1,), (1,)), ((), ())),
                             preferred_element_type=F32, precision=HIGHEST)
    scores = jax.nn.sigmoid(logits)
    per_group = N_EXPERTS // N_GROUPS
    shape3 = (N_GROUPS, per_group, tm)
    s3 = scores.reshape(shape3)
    b3 = (scores + eb_ref[...]).reshape(shape3)
    sub = lax.broadcasted_iota(I32, shape3, 1)
    eid = lax.broadcasted_iota(I32, shape3, 0) * per_group + sub

    m1 = jnp.max(b3, axis=1, keepdims=True)
    i1 = jnp.min(jnp.where(b3 == m1, sub, per_group), axis=1, keepdims=True)
    m2 = jnp.max(jnp.where(sub == i1, -jnp.inf, b3), axis=1, keepdims=True)
    gs = m1 + m2
    keep = []
    for g in range(N_GROUPS):
        beaten = jnp.zeros((1, tm), I32)
        for g2 in range(N_GROUPS):
            if g2 == g:
                continue
            wins = (gs[g2] >= gs[g]) if g2 < g else (gs[g2] > gs[g])
            beaten = beaten + jnp.where(wins, 1, 0)
        keep.append(jnp.where(beaten < TOPK_GROUPS, 1, 0)[None])
    keep3 = jnp.concatenate(keep, axis=0)
    val = jnp.where(keep3 > 0, b3, -jnp.inf)

    def red(fn, a):
        return fn(fn(a, axis=0, keepdims=True), axis=1, keepdims=True)

    idxs, ws = [], []
    member = jnp.zeros(shape3, F32)
    for _ in range(TOP_K):
        m = red(jnp.max, val)
        idx = red(jnp.min, jnp.where(val == m, eid, N_EXPERTS))
        hit = eid == idx
        ws.append(red(jnp.sum, jnp.where(hit, s3, 0.0)))
        val = jnp.where(hit, -jnp.inf, val)
        member = member + jnp.where(hit, 1.0, 0.0)
        idxs.append(idx)
    wsum = ws[0]
    for w in ws[1:]:
        wsum = wsum + w

    member2 = member.reshape(N_EXPERTS, tm)
    cnt = jnp.sum(member2, axis=1, keepdims=True)
    runlen = jnp.floor((cnt + (RUN_ALIGN - 1)) * (1.0 / RUN_ALIGN)) * RUN_ALIGN
    runlen_b = jnp.broadcast_to(runlen, (N_EXPERTS, LANES))
    e_row = lax.broadcasted_iota(I32, (N_EXPERTS, N_EXPERTS), 0)
    e_col = lax.broadcasted_iota(I32, (N_EXPERTS, N_EXPERTS), 1)
    earlier = jnp.where(e_col < e_row, 1.0, 0.0).astype(BF16)
    loff = _dot(earlier, runlen_b.astype(BF16))
    slot = _dot(member2.astype(BF16), tri_ref[...]) + loff[:, 0:1]
    slot3 = slot.reshape(shape3)
    for k in range(TOP_K):
        wts_ref[k:k + 1, :] = (ws[k] / wsum * ROUTED_SCALE).reshape(1, tm)
        sk = red(jnp.sum, jnp.where(eid == idxs[k], slot3, 0.0))
        lslot_ref[k:k + 1, :] = sk.reshape(1, tm).astype(I32)
    chunk_hi = (loff[:, 0:1] + runlen) * (1.0 / RUN_ALIGN)
    j = lax.broadcasted_iota(I32, (N_EXPERTS, CHUNK_TABLE), 1).astype(F32)
    e_of_chunk = jnp.sum(jnp.where(chunk_hi <= j, 1.0, 0.0), axis=0, keepdims=True)
    e_iota = lax.broadcasted_iota(I32, (N_EXPERTS, CHUNK_TABLE), 0).astype(F32)
    seg_off = carry_ref[:, 0:1] - loff[:, 0:1]
    rel = jnp.sum(jnp.where(e_iota == e_of_chunk, seg_off, 0.0), axis=0, keepdims=True)
    chunk_e_ref[0] = jnp.minimum(e_of_chunk, N_EXPERTS - 1.0).astype(I32)
    chunk_rel_ref[0] = (rel + j[0:1, :] * RUN_ALIGN).astype(I32)
    rows_ref[0] = (loff[N_EXPERTS - 1:N_EXPERTS, :] + runlen_b[N_EXPERTS - 1:N_EXPERTS, :]).astype(I32)
    total = carry_ref[...] + runlen_b
    carry_ref[...] = total
    cnt_ref[...] = total


def _router(x_flat, mod_lat, g, wr_t, e_bias, tri, tokens_per_batch):
    t, d = x_flat.shape
    tm = MOE_TILE
    n_tiles = t // tm
    tiles_per_batch = tokens_per_batch // tm
    body = functools.partial(_router_body, tm=tm)
    table = SDS((n_tiles, 1, CHUNK_TABLE), I32)
    table_spec = BS((1, 1, CHUNK_TABLE), lambda i: (i, 0, 0))
    return pl.pallas_call(
        body,
        out_shape=(SDS((t, d), BF16), SDS((TOP_K, t), I32), SDS((TOP_K, t), F32),
                   table, table, SDS((n_tiles, 1, LANES), I32), SDS((N_EXPERTS, LANES), F32)),
        grid=(n_tiles,),
        in_specs=[BS((tm, d), lambda i: (i, 0)),
                  BS((1, SUBLANES, d), lambda i: (i // tiles_per_batch, 0, 0)),
                  BS((1, d), lambda i: (0, 0)),
                  BS((N_EXPERTS, d), lambda i: (0, 0)),
                  BS((N_EXPERTS, 1), lambda i: (0, 0)),
                  BS((tm, tm), lambda i: (0, 0))],
        out_specs=(BS((tm, d), lambda i: (i, 0)),
                   BS((TOP_K, tm), lambda i: (0, i)),
                   BS((TOP_K, tm), lambda i: (0, i)),
                   table_spec, table_spec, BS((1, 1, LANES), lambda i: (i, 0, 0)),
                   BS((N_EXPERTS, LANES), lambda i: (0, 0))),
        scratch_shapes=[pltpu.VMEM((N_EXPERTS, LANES), F32)],
        compiler_params=_cparams("arbitrary"),
        name="moe_router",
    )(x_flat, mod_lat, g, wr_t, e_bias, tri)


def _start_run_chunks(pstart_ref, chunk_e_ref, chunk_rel_ref, rows, chunk_copy):
    def body(j, carry):
        slot0 = pstart_ref[chunk_e_ref[0, 0, j]] + chunk_rel_ref[0, 0, j]
        chunk_copy(pl.multiple_of(j * RUN_ALIGN, RUN_ALIGN),
                   pl.multiple_of(slot0, RUN_ALIGN)).start()
        return carry

    lax.fori_loop(0, rows // RUN_ALIGN, body, 0)


def _wait_run_rows(copy_of_rows, rows):
    def wait_n(nrows):
        def body(_, carry):
            copy_of_rows(nrows).wait()
            return carry
        return body

    lax.fori_loop(0, rows // SLOT_CHUNK, wait_n(SLOT_CHUNK), 0)
    lax.fori_loop(0, (rows % SLOT_CHUNK) // RUN_ALIGN, wait_n(RUN_ALIGN), 0)


def _dispatch_body(pend_ref, pcnt_ref, nu_ref, pstart_ref, rows_ref, chunk_e_ref, chunk_rel_ref,
                   lslot_ref, h_ref, xs_ref, loc_ref, oh_ref, zbuf_ref, sem, zsem, *,
                   tl, bm, n_blocks):
    i = pl.program_id(0)
    last = pl.num_programs(0) - 1
    buf = i % 2

    def zero_copy(row0):
        return pltpu.make_async_copy(
            zbuf_ref, xs_ref.at[pl.ds(pl.multiple_of(row0, RUN_ALIGN), bm), :], zsem)

    @pl.when(i == 0)
    def _():
        zbuf_ref[...] = jnp.zeros_like(zbuf_ref)

        def start(e, c):
            @pl.when(pcnt_ref[e] > 0)
            def _():
                zero_copy(pend_ref[e] - bm).start()
            return c

        def wait(e, c):
            @pl.when(pcnt_ref[e] > 0)
            def _():
                zero_copy(pend_ref[e] - bm).wait()
            return c

        def start_tail(j, c):
            zero_copy(j * bm).start()
            return c

        def wait_tail(j, c):
            zero_copy(j * bm).wait()
            return c

        lax.fori_loop(0, N_EXPERTS, start, 0)
        lax.fori_loop(nu_ref[0], n_blocks, start_tail, 0)
        lax.fori_loop(0, N_EXPERTS, wait, 0)
        lax.fori_loop(nu_ref[0], n_blocks, wait_tail, 0)

    rows_used = rows_ref[i]
    ls = lslot_ref[...]
    h = h_ref[...]

    def sort_chunk(c, carry):
        for s in range(SLOT_CHUNK // ONEHOT_ROWS):
            row = (lax.broadcasted_iota(I32, (ONEHOT_ROWS, tl), 0)
                   + (c * SLOT_CHUNK + s * ONEHOT_ROWS))
            onehot = jnp.zeros((ONEHOT_ROWS, tl), F32)
            for k in range(TOP_K):
                onehot = jnp.where(row == ls[k:k + 1, :], 1.0, onehot)
            oh_ref[s * ONEHOT_ROWS:(s + 1) * ONEHOT_ROWS, :] = onehot.astype(BF16)
        rows = pl.ds(pl.multiple_of(c * SLOT_CHUNK, SLOT_CHUNK), SLOT_CHUNK)
        loc_ref[buf, rows, :] = _dot(oh_ref[...], h).astype(BF16)
        return carry

    lax.fori_loop(0, (rows_used + SLOT_CHUNK - 1) // SLOT_CHUNK, sort_chunk, 0)

    def run_copy(b, loc0, slot0, nrows=RUN_ALIGN):
        return pltpu.make_async_copy(loc_ref.at[b, pl.ds(loc0, nrows), :],
                                     xs_ref.at[pl.ds(slot0, nrows), :], sem.at[b])

    _start_run_chunks(pstart_ref, chunk_e_ref, chunk_rel_ref, rows_used,
                      lambda loc0, slot0: run_copy(buf, loc0, slot0))

    @pl.when(i > 0)
    def _():
        _wait_run_rows(lambda n: run_copy(1 - buf, 0, 0, n), rows_ref[jnp.maximum(i - 1, 0)])

    @pl.when(i == last)
    def _():
        _wait_run_rows(lambda n: run_copy(buf, 0, 0, n), rows_used)


def _dispatch(pend, pcnt, n_used, pstart, rows, chunk_e, chunk_rel, lslot, h2b, n_slots):
    t, d = h2b.shape
    tl = MOE_TILE
    body = functools.partial(_dispatch_body, tl=tl, bm=BM_FFN, n_blocks=n_slots // BM_FFN)
    table_spec = lambda index: BS((1, 1, CHUNK_TABLE), index, memory_space=pltpu.SMEM)
    grid_spec = pltpu.PrefetchScalarGridSpec(
        num_scalar_prefetch=5,
        grid=(t // tl,),
        in_specs=[table_spec(lambda i, *_: (i, 0, 0)),
                  table_spec(lambda i, *_: (i, 0, 0)),
                  BS((TOP_K, tl), lambda i, *_: (0, i)),
                  BS((tl, d), lambda i, *_: (i, 0))],
        out_specs=BS(memory_space=pl.ANY),
        scratch_shapes=[pltpu.VMEM((2, LOCAL_SLOTS, d), BF16), pltpu.VMEM((SLOT_CHUNK, tl), BF16),
                        pltpu.VMEM((BM_FFN, d), BF16),
                        pltpu.SemaphoreType.DMA((2,)), pltpu.SemaphoreType.DMA(())],
    )
    return pl.pallas_call(
        body,
        out_shape=SDS((n_slots, d), BF16),
        grid_spec=grid_spec,
        compiler_params=_cparams("arbitrary"),
        name="moe_dispatch",
    )(pend, pcnt, n_used, pstart, rows, chunk_e, chunk_rel, lslot, h2b)


def _ffn_body(be_ref, nu_ref, xs_ref, wg_ref, wu_ref, wd_ref, ys_ref, wgb_ref, wub_ref, wdb_ref,
              xbuf_ref, ybuf_ref, isem, osem, *, bm, n_blocks):
    i = pl.program_id(0)
    nu = nu_ref[0]

    def block_rows(blk):
        return pl.ds(pl.multiple_of(blk * bm, bm), bm)

    def in_copy(blk, slot):
        return pltpu.make_async_copy(xs_ref.at[block_rows(blk), :], xbuf_ref.at[slot], isem.at[slot])

    def out_copy(blk, slot):
        return pltpu.make_async_copy(ybuf_ref.at[slot], ys_ref.at[block_rows(blk), :], osem.at[slot])

    @pl.when(i == 0)
    def _():
        for s in range(FFN_IN_BUFS):
            @pl.when(s < nu)
            def _():
                in_copy(s, s).start()

    fresh = jnp.logical_or(i == 0, be_ref[i] != be_ref[jnp.maximum(i - 1, 0)])

    @pl.when(jnp.logical_and(fresh, i < nu))
    def _():
        wgb_ref[...] = wg_ref[...].astype(BF16)
        wub_ref[...] = wu_ref[...].astype(BF16)
        wdb_ref[...] = wd_ref[...].astype(BF16)

    @pl.when(i < nu)
    def _():
        slot = i % FFN_IN_BUFS
        oslot = i % 2
        in_copy(i, slot).wait()
        x = xbuf_ref[slot]
        a = _silu(_dot(x, wgb_ref[...])) * _dot(x, wub_ref[...])
        y = _dot(a.astype(BF16), wdb_ref[...]).astype(BF16)

        @pl.when(i >= 2)
        def _():
            out_copy(i - 2, oslot).wait()

        ybuf_ref[oslot] = y
        out_copy(i, oslot).start()

        @pl.when(i + FFN_IN_BUFS < nu)
        def _():
            in_copy(i + FFN_IN_BUFS, slot).start()

    @pl.when(i == nu - 1)
    def _():
        @pl.when(i >= 1)
        def _():
            out_copy(i - 1, (i - 1) % 2).wait()

        out_copy(i, i % 2).wait()
        ybuf_ref[0] = jnp.zeros((bm, ybuf_ref.shape[-1]), BF16)

        def start_tail(j, c):
            out_copy(j, 0).start()
            return c

        def wait_tail(j, c):
            out_copy(j, 0).wait()
            return c

        lax.fori_loop(nu, n_blocks, start_tail, 0)
        lax.fori_loop(nu, n_blocks, wait_tail, 0)


def _ffn(blk_e, n_used, xs, w_gate, w_up, w_down, layer):
    n_slots, d = xs.shape
    bm = BM_FFN
    de = w_gate.shape[-1]
    n_blocks = n_slots // bm
    body = functools.partial(_ffn_body, bm=bm, n_blocks=n_blocks)
    grid_spec = pltpu.PrefetchScalarGridSpec(
        num_scalar_prefetch=2,
        grid=(n_blocks,),
        in_specs=[BS(memory_space=pl.ANY),
                  BS((None, None, d, de), lambda i, be, nu: (layer, be[i], 0, 0)),
                  BS((None, None, d, de), lambda i, be, nu: (layer, be[i], 0, 0)),
                  BS((None, None, de, d), lambda i, be, nu: (layer, be[i], 0, 0))],
        out_specs=BS(memory_space=pl.ANY),
        scratch_shapes=[pltpu.VMEM((d, de), BF16), pltpu.VMEM((d, de), BF16),
                        pltpu.VMEM((de, d), BF16),
                        pltpu.VMEM((FFN_IN_BUFS, bm, d), BF16), pltpu.VMEM((2, bm, d), BF16),
                        pltpu.SemaphoreType.DMA((FFN_IN_BUFS,)), pltpu.SemaphoreType.DMA((2,))],
    )
    return pl.pallas_call(
        body,
        out_shape=SDS((n_slots, d), BF16),
        grid_spec=grid_spec,
        compiler_params=_cparams("arbitrary"),
        name="moe_experts",
    )(blk_e, n_used, xs, w_gate, w_up, w_down)


def _combine_body(pstart_ref, rows_ref, chunk_e_ref, chunk_rel_ref, next_e_ref, next_rel_ref, ys_ref,
                  lslot_ref, wt_ref, x_ref, h2b_ref, mod_ref, wsg_ref, wsu_ref, wsd_ref, fg_ref,
                  o_ref, loc_ref, acc_ref, lsb_ref, wtb_ref, ghi_ref, glo_ref, sem, *, tl, final):
    i = pl.program_id(0)
    last = pl.num_programs(0) - 1
    buf = i % 2
    rows_used = rows_ref[i]

    def run_copy(b, loc0, slot0, nrows=RUN_ALIGN):
        return pltpu.make_async_copy(ys_ref.at[pl.ds(slot0, nrows), :],
                                     loc_ref.at[b, pl.ds(loc0, nrows), :], sem.at[b])

    @pl.when(i == 0)
    def _():
        loc_ref[...] = jnp.zeros_like(loc_ref)
        _start_run_chunks(pstart_ref, chunk_e_ref, chunk_rel_ref, rows_used,
                          lambda loc0, slot0: run_copy(buf, loc0, slot0))

    @pl.when(i < last)
    def _():
        _start_run_chunks(pstart_ref, next_e_ref, next_rel_ref, rows_ref[jnp.minimum(i + 1, last)],
                          lambda loc0, slot0: run_copy(1 - buf, loc0, slot0))

    hb = h2b_ref[...]
    a = _silu(_dot(hb, wsg_ref[...])) * _dot(hb, wsu_ref[...])
    acc_ref[...] = _dot(a.astype(BF16), wsd_ref[...])
    ls = lslot_ref[...]
    wt = wt_ref[...]
    for k in range(TOP_K):
        lsb_ref[k] = jnp.broadcast_to(ls[:, k:k + 1], (tl, LANES))
        wtb_ref[k] = jnp.broadcast_to(wt[:, k:k + 1], (tl, LANES))

    _wait_run_rows(lambda n: run_copy(buf, 0, 0, n), rows_used)

    def unsort_chunk(c, carry):
        def gate_rows(s, carry2):
            rows = pl.ds(pl.multiple_of(s * GATE_ROWS, GATE_ROWS), GATE_ROWS)
            for q in range(SLOT_CHUNK // LANES):
                col = (lax.broadcasted_iota(I32, (GATE_ROWS, LANES), 1)
                       + (c * SLOT_CHUNK + q * LANES))
                gate = jnp.zeros((GATE_ROWS, LANES), F32)
                for k in range(TOP_K):
                    gate = jnp.where(col == lsb_ref[k, rows, :], wtb_ref[k, rows, :], gate)
                hi = gate.astype(BF16)
                ghi_ref[rows, q * LANES:(q + 1) * LANES] = hi
                glo_ref[rows, q * LANES:(q + 1) * LANES] = (gate - hi.astype(F32)).astype(BF16)
            return carry2

        lax.fori_loop(0, tl // GATE_ROWS, gate_rows, 0)
        y = loc_ref[buf, pl.ds(pl.multiple_of(c * SLOT_CHUNK, SLOT_CHUNK), SLOT_CHUNK), :]
        acc_ref[...] = acc_ref[...] + _dot(ghi_ref[...], y) + _dot(glo_ref[...], y)
        return carry

    lax.fori_loop(0, (rows_used + SLOT_CHUNK - 1) // SLOT_CHUNK, unsort_chunk, 0)

    xo = x_ref[...] + mod_ref[0, 5:6, :] * acc_ref[...]
    if final:
        ms = jnp.mean(xo * xo, axis=-1, keepdims=True)
        xo = (xo * lax.rsqrt(ms + EPS)) * fg_ref[...]
    o_ref[...] = xo


def _combine(pstart, rows, chunk_e, chunk_rel, ys, lslot_tk, wt_tk, x_flat, h2b, mod_lat, wsg, wsu,
             wsd, final_g, tokens_per_batch, final):
    t, d = x_flat.shape
    tl = MOE_TILE
    ds = wsg.shape[1]
    tiles_per_batch = tokens_per_batch // tl
    body = functools.partial(_combine_body, tl=tl, final=final)
    n_tiles = t // tl
    table_spec = lambda index: BS((1, 1, CHUNK_TABLE), index, memory_space=pltpu.SMEM)
    this_tile = lambda i, *_: (i, 0, 0)
    next_tile = lambda i, *_: (jnp.minimum(i + 1, n_tiles - 1), 0, 0)
    grid_spec = pltpu.PrefetchScalarGridSpec(
        num_scalar_prefetch=2,
        grid=(n_tiles,),
        in_specs=[table_spec(this_tile), table_spec(this_tile),
                  table_spec(next_tile), table_spec(next_tile),
                  BS(memory_space=pl.ANY),
                  BS((tl, TOP_K), lambda i, *_: (i, 0)),
                  BS((tl, TOP_K), lambda i, *_: (i, 0)),
                  BS((tl, d), lambda i, *_: (i, 0)),
                  BS((tl, d), lambda i, *_: (i, 0)),
                  BS((1, SUBLANES, d), lambda i, *_: (i // tiles_per_batch, 0, 0)),
                  BS((d, ds), lambda i, *_: (0, 0)),
                  BS((d, ds), lambda i, *_: (0, 0)),
                  BS((ds, d), lambda i, *_: (0, 0)),
                  BS((1, d), lambda i, *_: (0, 0))],
        out_specs=BS((tl, d), lambda i, *_: (i, 0)),
        scratch_shapes=[pltpu.VMEM((2, LOCAL_SLOTS, d), BF16), pltpu.VMEM((tl, d), F32),
                        pltpu.VMEM((TOP_K, tl, LANES), I32), pltpu.VMEM((TOP_K, tl, LANES), F32),
                        pltpu.VMEM((tl, SLOT_CHUNK), BF16), pltpu.VMEM((tl, SLOT_CHUNK), BF16),
                        pltpu.SemaphoreType.DMA((2,))],
    )
    return pl.pallas_call(
        body,
        out_shape=SDS((t, d), F32),
        grid_spec=grid_spec,
        compiler_params=_cparams("arbitrary"),
        name="moe_combine",
    )(pstart, rows, chunk_e, chunk_rel, chunk_e, chunk_rel, ys, lslot_tk, wt_tk, x_flat, h2b,
      mod_lat, wsg, wsu, wsd, final_g)


def _moe(x_flat, mod_lat, g, w_router, e_bias, w_gate, w_up, w_down, ws_gate, ws_up, ws_down,
         final_g, tri, tokens_per_batch, layer, final):
    t, d = x_flat.shape
    bm = BM_FFN
    n_tiles = t // MOE_TILE
    h2b, lslot, wts, chunk_e, chunk_rel, rows, seg = _router(
        x_flat, mod_lat, g, w_router.T, e_bias.reshape(N_EXPERTS, 1), tri, tokens_per_batch)
    rows = rows[:, 0, 0]
    seg_rows = seg[:, 0].astype(I32)
    pcnt = (seg_rows + bm - 1) // bm * bm
    pend = jnp.cumsum(pcnt).astype(I32)
    pstart = pend - pcnt
    max_rows = t * TOP_K + n_tiles * N_EXPERTS * (RUN_ALIGN - 1)
    n_blocks = -(-max_rows // bm) + N_EXPERTS
    n_slots = n_blocks * bm
    block_row0 = jnp.arange(n_blocks, dtype=I32) * bm
    blk_e = jnp.minimum(jnp.sum((pend[None, :] <= block_row0[:, None]).astype(I32), axis=1),
                        N_EXPERTS - 1)
    n_used = pend[-1:] // bm
    xs = _dispatch(pend, pcnt, n_used, pstart, rows, chunk_e, chunk_rel, lslot, h2b, n_slots)
    ys = _ffn(blk_e, n_used, xs, w_gate, w_up, w_down, layer)
    return _combine(pstart, rows, chunk_e, chunk_rel, ys, lslot.T, wts.T, x_flat, h2b, mod_lat,
                    ws_gate.astype(BF16), ws_up.astype(BF16), ws_down.astype(BF16),
                    final_g.reshape(1, d), tokens_per_batch, final)


def _inproj_c_body(xp_ref, x_ref, xn_ref, mod_ref, g_ref, w_ref, cw_ref, cb_ref,
                   v_ref, g1_ref, g2_ref, *, tm, n_tiles):
    i = pl.program_id(1)
    halo = SUBLANES
    xe = jnp.concatenate([xp_ref[0], x_ref[0], xn_ref[0]], axis=0)
    h = _norm_mod(xe, g_ref[...], mod_ref[0, 1:2, :], mod_ref[0, 0:1, :])
    row = lax.broadcasted_iota(I32, (tm + 2 * halo, 1), 0)
    outside = jnp.logical_or(jnp.logical_and(i == 0, row < halo),
                             jnp.logical_and(i == n_tiles - 1, row >= tm + halo))
    hb = jnp.where(outside, 0.0, h).astype(BF16)
    width = v_ref.shape[-1]
    for part, o_ref in enumerate((v_ref, g1_ref, g2_ref)):
        cols = slice(part * width, (part + 1) * width)
        zp = _dot(hb, w_ref[:, cols])
        up = pltpu.roll(zp, 1, 0)
        dn = pltpu.roll(zp, tm + 2 * halo - 1, 0)
        z = cw_ref[0:1, cols] * up + cw_ref[1:2, cols] * zp + cw_ref[2:3, cols] * dn + cb_ref[:, cols]
        o_ref[0] = z[halo:halo + tm]


def _inproj_c(x, mod_lat, g, w_in, conv_w, conv_b):
    b, l, d = x.shape
    tm = TM_PROJ
    n_tiles = l // tm
    w3 = w_in.shape[1]
    width = w3 // 3
    r8 = tm // SUBLANES
    body = functools.partial(_inproj_c_body, tm=tm, n_tiles=n_tiles)
    out = SDS((b, l, width), F32)
    return pl.pallas_call(
        body,
        out_shape=(out, out, out),
        grid=(b, n_tiles),
        in_specs=[BS((1, SUBLANES, d), lambda bi, i: (bi, jnp.maximum(i * r8 - 1, 0), 0)),
                  BS((1, tm, d), lambda bi, i: (bi, i, 0)),
                  BS((1, SUBLANES, d), lambda bi, i: (bi, jnp.minimum((i + 1) * r8, l // SUBLANES - 1), 0)),
                  BS((1, SUBLANES, d), lambda bi, i: (bi, 0, 0)),
                  BS((1, d), lambda bi, i: (0, 0)),
                  BS((d, w3), lambda bi, i: (0, 0)),
                  BS((3, w3), lambda bi, i: (0, 0)),
                  BS((1, w3), lambda bi, i: (0, 0))],
        out_specs=(BS((1, tm, width), lambda bi, i: (bi, i, 0)),
                   BS((1, tm, width), lambda bi, i: (bi, i, 0)),
                   BS((1, tm, width), lambda bi, i: (bi, i, 0))),
        compiler_params=_cparams("arbitrary", "arbitrary"),
        name="inproj_c",
    )(x, x, x, mod_lat, g, w_in, conv_w, conv_b)


def _filter_body(f_ref, w1_ref, b1_ref, w2_ref, b2_ref, w3_ref, fr_ref, dl_ref, keep0_ref,
                 hf_ref, l1_ref, *, tp):
    i = pl.program_id(0)
    feats = f_ref[...]
    fr = fr_ref[...]
    a = jnp.sin(fr * (_dot_hp(feats, w1_ref[...]) + b1_ref[...]))
    a = jnp.sin(fr * (_dot_hp(a, w2_ref[...]) + b2_ref[...]))
    hf = _dot_hp(a, w3_ref[...])
    t01 = feats[:, 0:1]
    hf = hf * (jnp.exp(-t01 * jnp.abs(dl_ref[...])) + DECAY_SHIFT)
    row = lax.broadcasted_iota(I32, hf.shape, 0) + i * tp
    hf = jnp.where(row == 0, hf * keep0_ref[...], hf)
    hf_ref[...] = hf

    @pl.when(i == 0)
    def _():
        l1_ref[...] = jnp.zeros_like(l1_ref)

    l1_ref[...] = l1_ref[...] + jnp.sum(jnp.abs(hf), axis=0, keepdims=True)


def _filters(feats, w1, b1, w2, b2, w3, freq, delta, width):
    n, fe = feats.shape
    hid = w2.shape[0]
    fo = w3.shape[1]
    tp = 256
    lag0_keep = jnp.tile(jnp.repeat(jnp.array([1.0, 0.0], F32), width), fo // (2 * width)).reshape(1, fo)
    body = functools.partial(_filter_body, tp=tp)
    full = lambda shape: BS(shape, lambda i: (0, 0))
    return pl.pallas_call(
        body,
        out_shape=(SDS((n, fo), F32), SDS((SUBLANES, fo), F32)),
        grid=(n // tp,),
        in_specs=[BS((tp, fe), lambda i: (i, 0)), full((fe, hid)), full((1, hid)),
                  full((hid, hid)), full((1, hid)), full((hid, fo)), full((1, hid)), full((1, fo)),
                  full((1, fo))],
        out_specs=(BS((tp, fo), lambda i: (i, 0)), BS((SUBLANES, fo), lambda i: (0, 0))),
        compiler_params=_cparams("arbitrary"),
        name="hyena_filters",
    )(feats, w1, b1, w2, b2, w3, freq, delta, lag0_keep)


DFT_R = 128


def _dft_tables(n):
    r = DFT_R
    m = 2 * n
    na = n // r
    two_pi = 2.0 * np.pi
    a = np.arange(na)[None, :]
    v = np.arange(r)[:, None]
    ang1 = two_pi * ((a * v) % r) / r
    f1 = np.concatenate([np.cos(ang1), -np.sin(ang1)], axis=0)
    b = np.arange(r)[None, None, :]
    u = np.arange(r)[None, :, None]
    vv = np.arange(r)[:, None, None]
    ang2 = two_pi * ((b * (r * u + vv)) % m) / m
    gr, gi = np.cos(ang2), -np.sin(ang2)
    fwd = np.concatenate([np.concatenate([gr, -gi], axis=2),
                          np.concatenate([gi, gr], axis=2)], axis=1)
    hr, hi = np.transpose(gr, (0, 2, 1)), -np.transpose(gi, (0, 2, 1))
    inv = np.concatenate([np.concatenate([hr, -hi], axis=2),
                          np.concatenate([hi, hr], axis=2)], axis=1)
    ang3 = two_pi * ((np.arange(na)[:, None] * np.arange(r)[None, :]) % r) / r
    f3 = np.concatenate([np.cos(ang3), -np.sin(ang3)], axis=1) / m
    cast = lambda t: jnp.asarray(t.astype(np.float32)).astype(BF16)
    return cast(f1), cast(fwd), cast(inv), cast(f3)


def _dft_s1_body(y_ref, f1_ref, ar_ref, ai_ref):
    f1 = f1_ref[...]
    for j in range(SUBLANES):
        res = _dot(f1, y_ref[:, j, :].astype(BF16))
        ar_ref[:, j, :] = res[:DFT_R]
        ai_ref[:, j, :] = res[DFT_R:]


def _dft_s1(y4, f1):
    nb, na, r, c = y4.shape
    ct = min(c, 1024)
    out = SDS((nb, r, r, c), F32)
    return pl.pallas_call(
        _dft_s1_body,
        out_shape=(out, out),
        grid=(nb, c // ct, r // SUBLANES),
        in_specs=[BS((None, na, SUBLANES, ct), lambda n, cc, j: (n, 0, j, cc)),
                  BS((2 * r, na), lambda n, cc, j: (0, 0))],
        out_specs=(BS((None, r, SUBLANES, ct), lambda n, cc, j: (n, 0, j, cc)),
                   BS((None, r, SUBLANES, ct), lambda n, cc, j: (n, 0, j, cc))),
        compiler_params=_cparams("arbitrary", "arbitrary", "arbitrary"),
        name="dft_stage1",
    )(y4, f1)


def _filter_spec_body(arf_ref, aif_ref, arb_ref, aib_ref, g_ref, l1f_ref, l1b_ref, kr_ref, ki_ref):
    g = g_ref[...]
    yf = _dot(g, jnp.concatenate([arf_ref[...], aif_ref[...]], axis=0).astype(BF16))
    yb = _dot(g, jnp.concatenate([arb_ref[...], aib_ref[...]], axis=0).astype(BF16))
    inv = 1.0 / (l1f_ref[0:1, :] + l1b_ref[0:1, :])
    kr_ref[...] = (yf[:DFT_R] + yb[:DFT_R]) * inv
    ki_ref[...] = (yf[DFT_R:] - yb[DFT_R:]) * inv


def _filter_spectrum(ar, ai, fwd, l1, width):
    r = DFT_R
    a_spec = lambda d: BS((None, None, r, width), lambda v, o: (0, v, 0, 2 * o + d))
    l_spec = lambda d: BS((SUBLANES, width), lambda v, o: (0, 2 * o + d))
    out = SDS((r, r, HYENA_ORDER * width), F32)
    return pl.pallas_call(
        _filter_spec_body,
        out_shape=(out, out),
        grid=(r, HYENA_ORDER),
        in_specs=[a_spec(0), a_spec(0), a_spec(1), a_spec(1),
                  BS((None, 2 * r, 2 * r), lambda v, o: (v, 0, 0)), l_spec(0), l_spec(1)],
        out_specs=(BS((None, r, width), lambda v, o: (v, 0, o)),
                   BS((None, r, width), lambda v, o: (v, 0, o))),
        compiler_params=_cparams("arbitrary", "arbitrary"),
        name="hyena_filter_spectrum",
    )(ar, ai, ar, ai, fwd, l1, l1)


def _conv_mid_body(ar_ref, ai_ref, g_ref, h_ref, kr_ref, ki_ref, qr_ref, qi_ref):
    y = _dot(g_ref[...], jnp.concatenate([ar_ref[...], ai_ref[...]], axis=0).astype(BF16))
    yr, yi = y[:DFT_R], y[DFT_R:]
    kr, ki = kr_ref[...], ki_ref[...]
    p = jnp.concatenate([yr * kr - yi * ki, yr * ki + yi * kr], axis=0).astype(BF16)
    q = _dot(h_ref[...], p)
    qr_ref[...] = q[:DFT_R]
    qi_ref[...] = q[DFT_R:]


def _conv_mid(ar, ai, fwd, inv, kr, ki, order):
    nb, r, _, c = ar.shape
    a_spec = BS((None, None, r, c), lambda n, v: (n, v, 0, 0))
    m_spec = BS((None, 2 * r, 2 * r), lambda n, v: (v, 0, 0))
    k_spec = BS((None, r, c), lambda n, v: (v, 0, order))
    out = SDS((nb, r, r, c), F32)
    return pl.pallas_call(
        _conv_mid_body,
        out_shape=(out, out),
        grid=(nb, r),
        in_specs=[a_spec, a_spec, m_spec, m_spec, k_spec, k_spec],
        out_specs=(a_spec, a_spec),
        compiler_params=_cparams("arbitrary", "arbitrary"),
        name="hyena_spectral_product",
    )(ar, ai, fwd, inv, kr, ki)


def _idft_gate_body(qr_ref, qi_ref, f3_ref, y_ref, gate_ref, fb_ref, o_ref):
    f3 = f3_ref[...]
    fb = fb_ref[...]
    for j in range(SUBLANES):
        q = jnp.concatenate([qr_ref[:, j, :], qi_ref[:, j, :]], axis=0).astype(BF16)
        conv = _dot(f3, q)
        o_ref[:, j, :] = gate_ref[:, j, :] * (conv + fb * y_ref[:, j, :])


def _idft_gate(qr, qi, f3, y4, gate4, fbias):
    nb, na, r, c = y4.shape
    q_spec = BS((None, r, SUBLANES, c), lambda n, j: (n, 0, j, 0))
    y_spec = BS((None, na, SUBLANES, c), lambda n, j: (n, 0, j, 0))
    return pl.pallas_call(
        _idft_gate_body,
        out_shape=SDS((nb, na, r, c), F32),
        grid=(nb, r // SUBLANES),
        in_specs=[q_spec, q_spec, BS((na, 2 * r), lambda n, j: (0, 0)), y_spec, y_spec,
                  BS((1, c), lambda n, j: (0, 0))],
        out_specs=y_spec,
        compiler_params=_cparams("arbitrary", "arbitrary"),
        name="hyena_idft_gate",
    )(qr, qi, f3, y4, gate4, fbias)


def _outproj_body(y_ref, w_ref, x_ref, mod_ref, o_ref):
    o_ref[0] = x_ref[0] + mod_ref[0, 2:3, :] * _dot(y_ref[0].astype(BF16), w_ref[...])


def _outproj(y, w_out, x, mod_lat):
    b, l, d = x.shape
    tm = TM_PROJ
    wdt = y.shape[-1]
    return pl.pallas_call(
        _outproj_body,
        out_shape=SDS((b, l, d), F32),
        grid=(b, l // tm),
        in_specs=[BS((1, tm, wdt), lambda bi, i: (bi, i, 0)),
                  BS((wdt, d), lambda bi, i: (0, 0)),
                  BS((1, tm, d), lambda bi, i: (bi, i, 0)),
                  BS((1, SUBLANES, d), lambda bi, i: (bi, 0, 0))],
        out_specs=BS((1, tm, d), lambda bi, i: (bi, i, 0)),
        compiler_params=_cparams("arbitrary", "arbitrary"),
        name="outproj_c",
    )(y, w_out, x, mod_lat)


def _hyena(x, mod_lat, g, w_in, conv_w, conv_b, w1, b1, w2, b2, w3, freq, delta, f_bias, w_out):
    b, n, d = x.shape
    width = w_out.shape[0]
    r = DFT_R
    na = n // r
    f1, fwd, inv, f3 = _dft_tables(n)
    v, gate1, gate2 = _inproj_c(x, mod_lat, g, w_in.astype(BF16), conv_w, conv_b.reshape(1, -1))

    t = jnp.arange(n, dtype=F32)
    t01 = t / max(n - 1, 1)
    bands = jnp.linspace(1e-4, FILT_BANDS - 1, FILT_BANDS, dtype=F32)
    ang = (2.0 * math.pi / n) * t[:, None] * bands[None, :]
    feats = jnp.concatenate([t01[:, None], jnp.cos(ang), jnp.sin(ang)], axis=-1)
    fe = feats.shape[1]
    feats = jnp.pad(feats, ((0, 0), (0, LANES - fe)))
    w1p = jnp.pad(w1, ((0, LANES - fe), (0, 0)))
    hf, l1 = _filters(feats, w1p, b1.reshape(1, -1), w2, b2.reshape(1, -1), w3,
                      freq.reshape(1, -1), delta.reshape(1, -1), width)
    far, fai = _dft_s1(hf.reshape(1, na, r, hf.shape[1]), f1)
    kr, ki = _filter_spectrum(far, fai, fwd, l1, width)

    y4 = v.reshape(b, na, r, width)
    for o, gate in enumerate((gate1, gate2)):
        ar, ai = _dft_s1(y4, f1)
        qr, qi = _conv_mid(ar, ai, fwd, inv, kr, ki, o)
        y4 = _idft_gate(qr, qi, f3, y4, gate.reshape(b, na, r, width), f_bias[o].reshape(1, width))
    return _outproj(y4.reshape(b, n, width), w_out.astype(BF16), x, mod_lat)


def _rope_tables(seq_len):
    rows = seq_len // GRID_W
    row = jnp.repeat(jnp.arange(rows, dtype=F32), GRID_W)
    col = jnp.tile(jnp.arange(GRID_W, dtype=F32), rows)
    inv = jnp.power(ROPE_BASE, -jnp.arange(ROPE_FREQS, dtype=F32) / ROPE_FREQS)
    ar, ac = row[:, None] * inv, col[:, None] * inv
    cos_h = jnp.concatenate([jnp.cos(ar), jnp.cos(ar), jnp.cos(ac), jnp.cos(ac)], axis=1)
    sin_h = jnp.concatenate([-jnp.sin(ar), jnp.sin(ar), -jnp.sin(ac), jnp.sin(ac)], axis=1)
    reps = LANES // HEAD_DIM
    return jnp.tile(cos_h, (1, reps)), jnp.tile(sin_h, (1, reps))


def _rotate_partner_columns(w):
    ncol = w.shape[1]
    lane = np.arange(ncol)
    partner = np.where((lane % (2 * ROPE_FREQS)) < ROPE_FREQS, lane + ROPE_FREQS, lane - ROPE_FREQS)
    return w[:, partner]


def kernel(x, c, ctx, c_ctx, w_mod, b_mod, norm_g, w_in_ab, sink, w_spatial, b_spatial, w_out_ab,
           w_in_c, conv_w, conv_b, filt_w1, filt_b1, filt_w2, filt_b2, filt_w3, filt_freq,
           filt_delta, filt_bias, w_out_c, w_router, e_bias, w_gate, w_up, w_down, ws_gate,
           ws_up, ws_down, final_g):
    b, l, d = x.shape
    depth = w_mod.shape[0]
    assert depth == 2 and b + 1 <= SUBLANES

    cc = jnp.zeros((SUBLANES, d), F32).at[:b].set(c).at[b].set(c_ctx)
    m_all = _mod_vectors(cc, w_mod, b_mod)

    def mod_rows(layer, row0, nrow):
        m = m_all[layer, row0:row0 + nrow].reshape(nrow, 6, d)
        return jnp.pad(m, ((0, 0), (0, SUBLANES - 6), (0, 0)))

    tri = jnp.triu(jnp.ones((MOE_TILE, MOE_TILE), F32), k=1).astype(BF16)

    mod_lat = mod_rows(0, 0, b)
    mod_ctx = mod_rows(0, b, 1)[0]
    w_in = w_in_ab[0]
    qk = ATTN_WIDTH + KV_WIDTH
    w_cat = jnp.concatenate([w_in, _rotate_partner_columns(w_in[:, :qk])], axis=1).astype(BF16)
    cos_t, sin_t = _rope_tables(l)
    group_avg = jnp.kron(jnp.eye(N_SG_GROUPS, dtype=F32),
                         jnp.full((SG_GROUP_DIM, SG_GROUP_DIM), 1.0 / SG_GROUP_DIM, F32)).astype(BF16)
    kc, vc = _ctx_kv(ctx, mod_ctx, norm_g[0, 0].reshape(1, d),
                     w_in[:, ATTN_WIDTH:ATTN_WIDTH + 2 * KV_WIDTH].astype(BF16))
    q, k, v, ug, vn = _inproj_ab(x, mod_lat, norm_g[0, 0].reshape(1, d), w_cat, cos_t, sin_t, group_avg)
    b_full = jnp.repeat(b_spatial[0].T, SG_GROUP_DIM, axis=1)
    x1 = _mixer(sink[0], q, k, v, kc, vc, ug, vn, w_spatial[0].astype(BF16), b_full,
                w_out_ab[0].astype(BF16), x, mod_lat)
    x2 = _moe(x1.reshape(b * l, d), mod_lat, norm_g[0, 1].reshape(1, d), w_router[0], e_bias[0],
              w_gate, w_up, w_down, ws_gate[0], ws_up[0], ws_down[0], final_g, tri, l,
              layer=0, final=False).reshape(b, l, d)

    mod_lat = mod_rows(1, 0, b)
    x3 = _hyena(x2, mod_lat, norm_g[1, 0].reshape(1, d), w_in_c[0], conv_w[0], conv_b[0],
                filt_w1[0], filt_b1[0], filt_w2[0], filt_b2[0], filt_w3[0], filt_freq[0],
                filt_delta[0], filt_bias[0], w_out_c[0])
    out = _moe(x3.reshape(b * l, d), mod_lat, norm_g[1, 1].reshape(1, d), w_router[1], e_bias[1],
               w_gate, w_up, w_down, ws_gate[1], ws_up[1], ws_down[1], final_g, tri, l,
               layer=1, final=True)
    return out.reshape(b, l, d)
```

```python
import functools
import math

import numpy as np
import jax
import jax.numpy as jnp
from jax import lax
from jax.experimental import pallas as pl
from jax.experimental.pallas import tpu as pltpu

F32 = jnp.float32
BF16 = jnp.bfloat16
I32 = jnp.int32
HIGHEST = lax.Precision.HIGHEST
SDS = jax.ShapeDtypeStruct
BS = pl.BlockSpec

EPS = 1e-6
NEG = -1e30

GRID_W = 64
N_Q_HEADS = 8
N_KV_HEADS = 2
HEAD_DIM = 64
ATTN_WIDTH = N_Q_HEADS * HEAD_DIM
KV_WIDTH = N_KV_HEADS * HEAD_DIM
WINDOW = 128
BLOCK = 128
ROPE_BASE = 10000.0
ROPE_FREQS = HEAD_DIM // 4
N_SG_GROUPS = 8
SG_GROUP_DIM = 64
SG_WIDTH = N_SG_GROUPS * SG_GROUP_DIM
HYENA_ORDER = 2
FILT_BANDS = 16
DECAY_SHIFT = 0.05
N_EXPERTS = 64
TOP_K = 8
N_GROUPS = 8
TOPK_GROUPS = 4
ROUTED_SCALE = 2.5

LANES = 128
SUBLANES = 8
VMEM_LIMIT = 56 * 1024 * 1024

TM_PROJ = 512
TQ_MIX = 256
MOE_TILE = 256
BM_FFN = 512
RUN_ALIGN = 16
SLOT_CHUNK = 512
ONEHOT_ROWS = 64
GATE_ROWS = 32
LOCAL_SLOTS = -(-(TOP_K * MOE_TILE + N_EXPERTS * (RUN_ALIGN - 1)) // SLOT_CHUNK) * SLOT_CHUNK
CHUNK_TABLE = -(-(LOCAL_SLOTS // RUN_ALIGN) // LANES) * LANES
FFN_IN_BUFS = 4


def _cparams(*sem):
    return pltpu.CompilerParams(dimension_semantics=sem, vmem_limit_bytes=VMEM_LIMIT)


def _dot(a, b):
    return jnp.dot(a, b, preferred_element_type=F32)


def _dot_nt(a, b):
    return lax.dot_general(a, b, (((1,), (1,)), ((), ())), preferred_element_type=F32)


def _dot_hp(a, b):
    return jnp.dot(a, b, preferred_element_type=F32, precision=HIGHEST)


def _norm_mod(x, g, sc, sh):
    ms = jnp.mean(x * x, axis=-1, keepdims=True)
    y = x * lax.rsqrt(ms + EPS)
    return (y * g) * (1.0 + sc) + sh


def _gelu_tanh(x):
    c = math.sqrt(2.0 / math.pi)
    return 0.5 * x * (1.0 + jnp.tanh(c * (x + 0.044715 * (x * x * x))))


def _silu(x):
    return x * jax.nn.sigmoid(x)


def _mod_body(c_ref, w_ref, b_ref, o_ref):
    o_ref[0] = _dot_hp(_silu(c_ref[...]), w_ref[0]) + b_ref[0]


def _mod_vectors(cc, w_mod, b_mod):
    depth, d, n = w_mod.shape
    tn = 1536
    return pl.pallas_call(
        _mod_body,
        out_shape=SDS((depth, SUBLANES, n), F32),
        grid=(depth, n // tn),
        in_specs=[BS((SUBLANES, d), lambda l, j: (0, 0)),
                  BS((1, d, tn), lambda l, j: (l, 0, j)),
                  BS((1, 1, tn), lambda l, j: (l, 0, j))],
        out_specs=BS((1, SUBLANES, tn), lambda l, j: (l, 0, j)),
        compiler_params=_cparams("arbitrary", "arbitrary"),
        name="mod_vectors",
    )(cc, w_mod, b_mod.reshape(depth, 1, n))


def _ctx_kv_body(ctx_ref, mod_ref, g_ref, w_ref, kc_ref, vc_ref):
    h = _norm_mod(ctx_ref[0], g_ref[...], mod_ref[1:2, :], mod_ref[0:1, :])
    z = _dot(h.astype(BF16), w_ref[...])
    kc_ref[0] = z[:, :KV_WIDTH].astype(BF16)
    vc_ref[0] = z[:, KV_WIDTH:].astype(BF16)


def _ctx_kv(ctx, mod_ctx, g, w_kv):
    b, c, d = ctx.shape
    return pl.pallas_call(
        _ctx_kv_body,
        out_shape=(SDS((b, c, KV_WIDTH), BF16), SDS((b, c, KV_WIDTH), BF16)),
        grid=(b,),
        in_specs=[BS((1, c, d), lambda i: (i, 0, 0)),
                  BS((SUBLANES, d), lambda i: (0, 0)),
                  BS((1, d), lambda i: (0, 0)),
                  BS((d, 2 * KV_WIDTH), lambda i: (0, 0))],
        out_specs=(BS((1, c, KV_WIDTH), lambda i: (i, 0, 0)),
                   BS((1, c, KV_WIDTH), lambda i: (i, 0, 0))),
        compiler_params=_cparams("arbitrary"),
        name="ctx_kv",
    )(ctx, mod_ctx, g, w_kv)


def _inproj_ab_body(x_ref, mod_ref, g_ref, w_ref, cos_ref, sin_ref, avg_ref,
                    q_ref, k_ref, v_ref, ug_ref, vn_ref):
    h = _norm_mod(x_ref[0], g_ref[...], mod_ref[0, 1:2, :], mod_ref[0, 0:1, :]).astype(BF16)
    cs = cos_ref[...]
    sn = sin_ref[...]
    rot0 = ATTN_WIDTH + 2 * KV_WIDTH + 2 * SG_WIDTH
    scale = HEAD_DIM ** -0.5
    for j in range(ATTN_WIDTH // LANES):
        z = _dot(h, w_ref[:, j * LANES:(j + 1) * LANES])
        zr = _dot(h, w_ref[:, rot0 + j * LANES:rot0 + (j + 1) * LANES])
        q_ref[0, :, j * LANES:(j + 1) * LANES] = ((z * cs + zr * sn) * scale).astype(BF16)
    zk = _dot(h, w_ref[:, ATTN_WIDTH:ATTN_WIDTH + KV_WIDTH])
    zkr = _dot(h, w_ref[:, rot0 + ATTN_WIDTH:rot0 + ATTN_WIDTH + KV_WIDTH])
    k_ref[0] = (zk * cs + zkr * sn).astype(BF16)
    v_ref[0] = _dot(h, w_ref[:, ATTN_WIDTH + KV_WIDTH:ATTN_WIDTH + 2 * KV_WIDTH]).astype(BF16)
    u0 = ATTN_WIDTH + 2 * KV_WIDTH
    ug_ref[0] = _gelu_tanh(_dot(h, w_ref[:, u0:u0 + SG_WIDTH]))
    vf = _gelu_tanh(_dot(h, w_ref[:, u0 + SG_WIDTH:u0 + 2 * SG_WIDTH]))
    avg = avg_ref[...]

    def gmean(t):
        hi = t.astype(BF16)
        lo = (t - hi.astype(F32)).astype(BF16)
        return _dot(hi, avg) + _dot(lo, avg)

    vc = vf - gmean(vf)
    vn_ref[0] = (vc * lax.rsqrt(gmean(vc * vc) + EPS)).astype(BF16)


def _inproj_ab(x, mod_lat, g, w_cat, cos_t, sin_t, avg):
    b, l, d = x.shape
    tm = TM_PROJ
    ncol = w_cat.shape[1]
    return pl.pallas_call(
        _inproj_ab_body,
        out_shape=(SDS((b, l, ATTN_WIDTH), BF16), SDS((b, l, KV_WIDTH), BF16),
                   SDS((b, l, KV_WIDTH), BF16), SDS((b, l, SG_WIDTH), F32),
                   SDS((b, l, SG_WIDTH), BF16)),
        grid=(b, l // tm),
        in_specs=[BS((1, tm, d), lambda bi, i: (bi, i, 0)),
                  BS((1, SUBLANES, d), lambda bi, i: (bi, 0, 0)),
                  BS((1, d), lambda bi, i: (0, 0)),
                  BS((d, ncol), lambda bi, i: (0, 0)),
                  BS((tm, LANES), lambda bi, i: (i, 0)),
                  BS((tm, LANES), lambda bi, i: (i, 0)),
                  BS((SG_WIDTH, SG_WIDTH), lambda bi, i: (0, 0))],
        out_specs=(BS((1, tm, ATTN_WIDTH), lambda bi, i: (bi, i, 0)),
                   BS((1, tm, KV_WIDTH), lambda bi, i: (bi, i, 0)),
                   BS((1, tm, KV_WIDTH), lambda bi, i: (bi, i, 0)),
                   BS((1, tm, SG_WIDTH), lambda bi, i: (bi, i, 0)),
                   BS((1, tm, SG_WIDTH), lambda bi, i: (bi, i, 0))),
        compiler_params=_cparams("arbitrary", "arbitrary"),
        name="inproj_ab",
    )(x, mod_lat, g, w_cat, cos_t, sin_t, avg)


def _mixer_body(sink_ref, q_ref, kp_ref, kcur_ref, kn_ref, vp_ref, vcur_ref, vn_ref,
                kc_ref, vc_ref, ug_ref, vnorm_ref, ws_ref, bs_ref, wout_ref, x_ref, mod_ref,
                o_ref, cat_ref, *, seq_len, sub_blocks):
    i = pl.program_id(1)
    kk = jnp.concatenate([kp_ref[0], kcur_ref[0], kn_ref[0]], axis=0)
    vv = jnp.concatenate([vp_ref[0], vcur_ref[0], vn_ref[0]], axis=0)
    kc = kc_ref[0]
    vc = vc_ref[0]
    span = 3 * BLOCK
    ii = lax.broadcasted_iota(I32, (BLOCK, span), 0)
    jj = lax.broadcasted_iota(I32, (BLOCK, span), 1)
    dd = jj - ii
    in_window = jnp.where(dd >= 0, jnp.where(dd <= 2 * WINDOW, 1, 0), 0)
    group = N_Q_HEADS // N_KV_HEADS
    for r in range(sub_blocks):
        rows = slice(r * BLOCK, (r + 1) * BLOCK)
        kpos = (i * sub_blocks + r - 1) * BLOCK + jj
        in_seq = jnp.where(kpos >= 0, jnp.where(kpos < seq_len, 1, 0), 0)
        bias = jnp.where(in_window * in_seq > 0, 0.0, NEG)
        qb = q_ref[0, rows, :]
        kl = kk[r * BLOCK:r * BLOCK + span]
        vl = vv[r * BLOCK:r * BLOCK + span]
        for hq in range(N_Q_HEADS):
            hk = hq // group
            ks = slice(hk * HEAD_DIM, (hk + 1) * HEAD_DIM)
            qh = qb[:, hq * HEAD_DIM:(hq + 1) * HEAD_DIM]
            s_loc = _dot_nt(qh, kl[:, ks]) + bias
            s_ctx = _dot_nt(qh, kc[:, ks])
            sk = sink_ref[hq]
            m = jnp.maximum(jnp.maximum(jnp.max(s_loc, axis=-1, keepdims=True),
                                        jnp.max(s_ctx, axis=-1, keepdims=True)), sk)
            p_loc = jnp.exp(s_loc - m)
            p_ctx = jnp.exp(s_ctx - m)
            den = (jnp.sum(p_loc, axis=-1, keepdims=True) + jnp.sum(p_ctx, axis=-1, keepdims=True)
                   + jnp.exp(sk - m))
            o = _dot(p_loc.astype(BF16), vl[:, ks]) + _dot(p_ctx.astype(BF16), vc[:, ks])
            cat_ref[rows, hq * HEAD_DIM:(hq + 1) * HEAD_DIM] = (o / den).astype(BF16)
        vnb = vnorm_ref[0, rows, :]
        ugb = ug_ref[0, rows, :]
        for g in range(N_SG_GROUPS):
            gs = slice(g * SG_GROUP_DIM, (g + 1) * SG_GROUP_DIM)
            sg = _dot(ws_ref[g], vnb[:, gs]) + bs_ref[:, gs]
            cat_ref[rows, ATTN_WIDTH + g * SG_GROUP_DIM:ATTN_WIDTH + (g + 1) * SG_GROUP_DIM] = (
                ugb[:, gs] * sg).astype(BF16)
    y = _dot(cat_ref[...], wout_ref[...])
    o_ref[0] = x_ref[0] + mod_ref[0, 2:3, :] * y


def _mixer(sink, q, k, v, kc, vc, ug, vn, w_s, b_full, w_out, x, mod_lat):
    b, l, d = x.shape
    tq = TQ_MIX
    r = tq // BLOCK
    nb = l // BLOCK
    c = kc.shape[1]
    prev_map = lambda bi, i: (bi, jnp.maximum(i * r - 1, 0), 0)
    next_map = lambda bi, i: (bi, jnp.minimum((i + 1) * r, nb - 1), 0)
    cur_map = lambda bi, i: (bi, i, 0)
    body = functools.partial(_mixer_body, seq_len=l, sub_blocks=r)
    return pl.pallas_call(
        body,
        out_shape=SDS((b, l, d), F32),
        grid=(b, l // tq),
        in_specs=[BS(memory_space=pltpu.SMEM),
                  BS((1, tq, ATTN_WIDTH), cur_map),
                  BS((1, BLOCK, KV_WIDTH), prev_map), BS((1, tq, KV_WIDTH), cur_map),
                  BS((1, BLOCK, KV_WIDTH), next_map),
                  BS((1, BLOCK, KV_WIDTH), prev_map), BS((1, tq, KV_WIDTH), cur_map),
                  BS((1, BLOCK, KV_WIDTH), next_map),
                  BS((1, c, KV_WIDTH), lambda bi, i: (bi, 0, 0)),
                  BS((1, c, KV_WIDTH), lambda bi, i: (bi, 0, 0)),
                  BS((1, tq, SG_WIDTH), cur_map), BS((1, tq, SG_WIDTH), cur_map),
                  BS((N_SG_GROUPS, BLOCK, BLOCK), lambda bi, i: (0, 0, 0)),
                  BS((BLOCK, SG_WIDTH), lambda bi, i: (0, 0)),
                  BS((d, d), lambda bi, i: (0, 0)),
                  BS((1, tq, d), cur_map),
                  BS((1, SUBLANES, d), lambda bi, i: (bi, 0, 0))],
        out_specs=BS((1, tq, d), cur_map),
        scratch_shapes=[pltpu.VMEM((tq, d), BF16)],
        compiler_params=_cparams("arbitrary", "arbitrary"),
        name="mixer_ab",
    )(sink, q, k, k, k, v, v, v, kc, vc, ug, vn, w_s, b_full, w_out, x, mod_lat)


def _router_body(x_ref, mod_ref, g_ref, wr_ref, eb_ref, tri_ref,
                 h2b_ref, lslot_ref, wts_ref, chunk_e_ref, chunk_rel_ref, rows_ref, cnt_ref,
                 carry_ref, *, tm):
    i = pl.program_id(0)

    @pl.when(i == 0)
    def _():
        carry_ref[...] = jnp.zeros_like(carry_ref)

    h2 = _norm_mod(x_ref[...], g_ref[...], mod_ref[0, 4:5, :], mod_ref[0, 3:4, :])
    h2b_ref[...] = h2.astype(BF16)

    logits = lax.dot_general(wr_ref[...], h2, (((1,), (1,)), ((), ())),
                             preferred_element_type=F32, precision=HIGHEST)
    scores = jax.nn.sigmoid(logits)
    per_group = N_EXPERTS // N_GROUPS
    shape3 = (N_GROUPS, per_group, tm)
    s3 = scores.reshape(shape3)
    b3 = (scores + eb_ref[...]).reshape(shape3)
    sub = lax.broadcasted_iota(I32, shape3, 1)
    eid = lax.broadcasted_iota(I32, shape3, 0) * per_group + sub

    m1 = jnp.max(b3, axis=1, keepdims=True)
    i1 = jnp.min(jnp.where(b3 == m1, sub, per_group), axis=1, keepdims=True)
    m2 = jnp.max(jnp.where(sub == i1, -jnp.inf, b3), axis=1, keepdims=True)
    gs = m1 + m2
    keep = []
    for g in range(N_GROUPS):
        beaten = jnp.zeros((1, tm), I32)
        for g2 in range(N_GROUPS):
            if g2 == g:
                continue
            wins = (gs[g2] >= gs[g]) if g2 < g else (gs[g2] > gs[g])
            beaten = beaten + jnp.where(wins, 1, 0)
        keep.append(jnp.where(beaten < TOPK_GROUPS, 1, 0)[None])
    keep3 = jnp.concatenate(keep, axis=0)
    val = jnp.where(keep3 > 0, b3, -jnp.inf)

    def red(fn, a):
        return fn(fn(a, axis=0, keepdims=True), axis=1, keepdims=True)

    idxs, ws = [], []
    member = jnp.zeros(shape3, F32)
    for _ in range(TOP_K):
        m = red(jnp.max, val)
        idx = red(jnp.min, jnp.where(val == m, eid, N_EXPERTS))
        hit = eid == idx
        ws.append(red(jnp.sum, jnp.where(hit, s3, 0.0)))
        val = jnp.where(hit, -jnp.inf, val)
        member = member + jnp.where(hit, 1.0, 0.0)
        idxs.append(idx)
    wsum = ws[0]
    for w in ws[1:]:
        wsum = wsum + w

    member2 = member.reshape(N_EXPERTS, tm)
    cnt = jnp.sum(member2, axis=1, keepdims=True)
    runlen = jnp.floor((cnt + (RUN_ALIGN - 1)) * (1.0 / RUN_ALIGN)) * RUN_ALIGN
    runlen_b = jnp.broadcast_to(runlen, (N_EXPERTS, LANES))
    e_row = lax.broadcasted_iota(I32, (N_EXPERTS, N_EXPERTS), 0)
    e_col = lax.broadcasted_iota(I32, (N_EXPERTS, N_EXPERTS), 1)
    earlier = jnp.where(e_col < e_row, 1.0, 0.0).astype(BF16)
    loff = _dot(earlier, runlen_b.astype(BF16))
    slot = _dot(member2.astype(BF16), tri_ref[...]) + loff[:, 0:1]
    slot3 = slot.reshape(shape3)
    for k in range(TOP_K):
        wts_ref[k:k + 1, :] = (ws[k] / wsum * ROUTED_SCALE).reshape(1, tm)
        sk = red(jnp.sum, jnp.where(eid == idxs[k], slot3, 0.0))
        lslot_ref[k:k + 1, :] = sk.reshape(1, tm).astype(I32)
    chunk_hi = (loff[:, 0:1] + runlen) * (1.0 / RUN_ALIGN)
    j = lax.broadcasted_iota(I32, (N_EXPERTS, CHUNK_TABLE), 1).astype(F32)
    e_of_chunk = jnp.sum(jnp.where(chunk_hi <= j, 1.0, 0.0), axis=0, keepdims=True)
    e_iota = lax.broadcasted_iota(I32, (N_EXPERTS, CHUNK_TABLE), 0).astype(F32)
    seg_off = carry_ref[:, 0:1] - loff[:, 0:1]
    rel = jnp.sum(jnp.where(e_iota == e_of_chunk, seg_off, 0.0), axis=0, keepdims=True)
    chunk_e_ref[0] = jnp.minimum(e_of_chunk, N_EXPERTS - 1.0).astype(I32)
    chunk_rel_ref[0] = (rel + j[0:1, :] * RUN_ALIGN).astype(I32)
    rows_ref[0] = (loff[N_EXPERTS - 1:N_EXPERTS, :] + runlen_b[N_EXPERTS - 1:N_EXPERTS, :]).astype(I32)
    total = carry_ref[...] + runlen_b
    carry_ref[...] = total
    cnt_ref[...] = total


def _router(x_flat, mod_lat, g, wr_t, e_bias, tri, tokens_per_batch):
    t, d = x_flat.shape
    tm = MOE_TILE
    n_tiles = t // tm
    tiles_per_batch = tokens_per_batch // tm
    body = functools.partial(_router_body, tm=tm)
    table = SDS((n_tiles, 1, CHUNK_TABLE), I32)
    table_spec = BS((1, 1, CHUNK_TABLE), lambda i: (i, 0, 0))
    return pl.pallas_call(
        body,
        out_shape=(SDS((t, d), BF16), SDS((TOP_K, t), I32), SDS((TOP_K, t), F32),
                   table, table, SDS((n_tiles, 1, LANES), I32), SDS((N_EXPERTS, LANES), F32)),
        grid=(n_tiles,),
        in_specs=[BS((tm, d), lambda i: (i, 0)),
                  BS((1, SUBLANES, d), lambda i: (i // tiles_per_batch, 0, 0)),
                  BS((1, d), lambda i: (0, 0)),
                  BS((N_EXPERTS, d), lambda i: (0, 0)),
                  BS((N_EXPERTS, 1), lambda i: (0, 0)),
                  BS((tm, tm), lambda i: (0, 0))],
        out_specs=(BS((tm, d), lambda i: (i, 0)),
                   BS((TOP_K, tm), lambda i: (0, i)),
                   BS((TOP_K, tm), lambda i: (0, i)),
                   table_spec, table_spec, BS((1, 1, LANES), lambda i: (i, 0, 0)),
                   BS((N_EXPERTS, LANES), lambda i: (0, 0))),
        scratch_shapes=[pltpu.VMEM((N_EXPERTS, LANES), F32)],
        compiler_params=_cparams("arbitrary"),
        name="moe_router",
    )(x_flat, mod_lat, g, wr_t, e_bias, tri)


def _start_run_chunks(pstart_ref, chunk_e_ref, chunk_rel_ref, rows, chunk_copy):
    def body(j, carry):
        slot0 = pstart_ref[chunk_e_ref[0, 0, j]] + chunk_rel_ref[0, 0, j]
        chunk_copy(pl.multiple_of(j * RUN_ALIGN, RUN_ALIGN),
                   pl.multiple_of(slot0, RUN_ALIGN)).start()
        return carry

    lax.fori_loop(0, rows // RUN_ALIGN, body, 0)


def _wait_run_rows(copy_of_rows, rows):
    def wait_n(nrows):
        def body(_, carry):
            copy_of_rows(nrows).wait()
            return carry
        return body

    lax.fori_loop(0, rows // SLOT_CHUNK, wait_n(SLOT_CHUNK), 0)
    lax.fori_loop(0, (rows % SLOT_CHUNK) // RUN_ALIGN, wait_n(RUN_ALIGN), 0)


def _dispatch_body(pend_ref, pcnt_ref, nu_ref, pstart_ref, rows_ref, chunk_e_ref, chunk_rel_ref,
                   lslot_ref, h_ref, xs_ref, loc_ref, oh_ref, zbuf_ref, sem, zsem, *,
                   tl, bm, n_blocks):
    i = pl.program_id(0)
    last = pl.num_programs(0) - 1
    buf = i % 2

    def zero_copy(row0):
        return pltpu.make_async_copy(
            zbuf_ref, xs_ref.at[pl.ds(pl.multiple_of(row0, RUN_ALIGN), bm), :], zsem)

    @pl.when(i == 0)
    def _():
        zbuf_ref[...] = jnp.zeros_like(zbuf_ref)

        def start(e, c):
            @pl.when(pcnt_ref[e] > 0)
            def _():
                zero_copy(pend_ref[e] - bm).start()
            return c

        def wait(e, c):
            @pl.when(pcnt_ref[e] > 0)
            def _():
                zero_copy(pend_ref[e] - bm).wait()
            return c

        def start_tail(j, c):
            zero_copy(j * bm).start()
            return c

        def wait_tail(j, c):
            zero_copy(j * bm).wait()
            return c

        lax.fori_loop(0, N_EXPERTS, start, 0)
        lax.fori_loop(nu_ref[0], n_blocks, start_tail, 0)
        lax.fori_loop(0, N_EXPERTS, wait, 0)
        lax.fori_loop(nu_ref[0], n_blocks, wait_tail, 0)

    rows_used = rows_ref[i]
    ls = lslot_ref[...]
    h = h_ref[...]

    def sort_chunk(c, carry):
        for s in range(SLOT_CHUNK // ONEHOT_ROWS):
            row = (lax.broadcasted_iota(I32, (ONEHOT_ROWS, tl), 0)
                   + (c * SLOT_CHUNK + s * ONEHOT_ROWS))
            onehot = jnp.zeros((ONEHOT_ROWS, tl), F32)
            for k in range(TOP_K):
                onehot = jnp.where(row == ls[k:k + 1, :], 1.0, onehot)
            oh_ref[s * ONEHOT_ROWS:(s + 1) * ONEHOT_ROWS, :] = onehot.astype(BF16)
        rows = pl.ds(pl.multiple_of(c * SLOT_CHUNK, SLOT_CHUNK), SLOT_CHUNK)
        loc_ref[buf, rows, :] = _dot(oh_ref[...], h).astype(BF16)
        return carry

    lax.fori_loop(0, (rows_used + SLOT_CHUNK - 1) // SLOT_CHUNK, sort_chunk, 0)

    def run_copy(b, loc0, slot0, nrows=RUN_ALIGN):
        return pltpu.make_async_copy(loc_ref.at[b, pl.ds(loc0, nrows), :],
                                     xs_ref.at[pl.ds(slot0, nrows), :], sem.at[b])

    _start_run_chunks(pstart_ref, chunk_e_ref, chunk_rel_ref, rows_used,
                      lambda loc0, slot0: run_copy(buf, loc0, slot0))

    @pl.when(i > 0)
    def _():
        _wait_run_rows(lambda n: run_copy(1 - buf, 0, 0, n), rows_ref[jnp.maximum(i - 1, 0)])

    @pl.when(i == last)
    def _():
        _wait_run_rows(lambda n: run_copy(buf, 0, 0, n), rows_used)


def _dispatch(pend, pcnt, n_used, pstart, rows, chunk_e, chunk_rel, lslot, h2b, n_slots):
    t, d = h2b.shape
    tl = MOE_TILE
    body = functools.partial(_dispatch_body, tl=tl, bm=BM_FFN, n_blocks=n_slots // BM_FFN)
    table_spec = lambda index: BS((1, 1, CHUNK_TABLE), index, memory_space=pltpu.SMEM)
    grid_spec = pltpu.PrefetchScalarGridSpec(
        num_scalar_prefetch=5,
        grid=(t // tl,),
        in_specs=[table_spec(lambda i, *_: (i, 0, 0)),
                  table_spec(lambda i, *_: (i, 0, 0)),
                  BS((TOP_K, tl), lambda i, *_: (0, i)),
                  BS((tl, d), lambda i, *_: (i, 0))],
        out_specs=BS(memory_space=pl.ANY),
        scratch_shapes=[pltpu.VMEM((2, LOCAL_SLOTS, d), BF16), pltpu.VMEM((SLOT_CHUNK, tl), BF16),
                        pltpu.VMEM((BM_FFN, d), BF16),
                        pltpu.SemaphoreType.DMA((2,)), pltpu.SemaphoreType.DMA(())],
    )
    return pl.pallas_call(
        body,
        out_shape=SDS((n_slots, d), BF16),
        grid_spec=grid_spec,
        compiler_params=_cparams("arbitrary"),
        name="moe_dispatch",
    )(pend, pcnt, n_used, pstart, rows, chunk_e, chunk_rel, lslot, h2b)


def _ffn_body(be_ref, nu_ref, xs_ref, wg_ref, wu_ref, wd_ref, ys_ref, wgb_ref, wub_ref, wdb_ref,
              xbuf_ref, ybuf_ref, isem, osem, *, bm, n_blocks):
    i = pl.program_id(0)
    nu = nu_ref[0]

    def block_rows(blk):
        return pl.ds(pl.multiple_of(blk * bm, bm), bm)

    def in_copy(blk, slot):
        return pltpu.make_async_copy(xs_ref.at[block_rows(blk), :], xbuf_ref.at[slot], isem.at[slot])

    def out_copy(blk, slot):
        return pltpu.make_async_copy(ybuf_ref.at[slot], ys_ref.at[block_rows(blk), :], osem.at[slot])

    @pl.when(i == 0)
    def _():
        for s in range(FFN_IN_BUFS):
            @pl.when(s < nu)
            def _():
                in_copy(s, s).start()

    fresh = jnp.logical_or(i == 0, be_ref[i] != be_ref[jnp.maximum(i - 1, 0)])

    @pl.when(jnp.logical_and(fresh, i < nu))
    def _():
        wgb_ref[...] = wg_ref[...].astype(BF16)
        wub_ref[...] = wu_ref[...].astype(BF16)
        wdb_ref[...] = wd_ref[...].astype(BF16)

    @pl.when(i < nu)
    def _():
        slot = i % FFN_IN_BUFS
        oslot = i % 2
        in_copy(i, slot).wait()
        x = xbuf_ref[slot]
        a = _silu(_dot(x, wgb_ref[...])) * _dot(x, wub_ref[...])
        y = _dot(a.astype(BF16), wdb_ref[...]).astype(BF16)

        @pl.when(i >= 2)
        def _():
            out_copy(i - 2, oslot).wait()

        ybuf_ref[oslot] = y
        out_copy(i, oslot).start()

        @pl.when(i + FFN_IN_BUFS < nu)
        def _():
            in_copy(i + FFN_IN_BUFS, slot).start()

    @pl.when(i == nu - 1)
    def _():
        @pl.when(i >= 1)
        def _():
            out_copy(i - 1, (i - 1) % 2).wait()

        out_copy(i, i % 2).wait()
        ybuf_ref[0] = jnp.zeros((bm, ybuf_ref.shape[-1]), BF16)

        def start_tail(j, c):
            out_copy(j, 0).start()
            return c

        def wait_tail(j, c):
            out_copy(j, 0).wait()
            return c

        lax.fori_loop(nu, n_blocks, start_tail, 0)
        lax.fori_loop(nu, n_blocks, wait_tail, 0)


def _ffn(blk_e, n_used, xs, w_gate, w_up, w_down, layer):
    n_slots, d = xs.shape
    bm = BM_FFN
    de = w_gate.shape[-1]
    n_blocks = n_slots // bm
    body = functools.partial(_ffn_body, bm=bm, n_blocks=n_blocks)
    grid_spec = pltpu.PrefetchScalarGridSpec(
        num_scalar_prefetch=2,
        grid=(n_blocks,),
        in_specs=[BS(memory_space=pl.ANY),
                  BS((None, None, d, de), lambda i, be, nu: (layer, be[i], 0, 0)),
                  BS((None, None, d, de), lambda i, be, nu: (layer, be[i], 0, 0)),
                  BS((None, None, de, d), lambda i, be, nu: (layer, be[i], 0, 0))],
        out_specs=BS(memory_space=pl.ANY),
        scratch_shapes=[pltpu.VMEM((d, de), BF16), pltpu.VMEM((d, de), BF16),
                        pltpu.VMEM((de, d), BF16),
                        pltpu.VMEM((FFN_IN_BUFS, bm, d), BF16), pltpu.VMEM((2, bm, d), BF16),
                        pltpu.SemaphoreType.DMA((FFN_IN_BUFS,)), pltpu.SemaphoreType.DMA((2,))],
    )
    return pl.pallas_call(
        body,
        out_shape=SDS((n_slots, d), BF16),
        grid_spec=grid_spec,
        compiler_params=_cparams("arbitrary"),
        name="moe_experts",
    )(blk_e, n_used, xs, w_gate, w_up, w_down)


def _combine_body(pstart_ref, rows_ref, chunk_e_ref, chunk_rel_ref, next_e_ref, next_rel_ref, ys_ref,
                  lslot_ref, wt_ref, x_ref, h2b_ref, mod_ref, wsg_ref, wsu_ref, wsd_ref, fg_ref,
                  o_ref, loc_ref, acc_ref, lsb_ref, wtb_ref, ghi_ref, glo_ref, sem, *, tl, final):
    i = pl.program_id(0)
    last = pl.num_programs(0) - 1
    buf = i % 2
    rows_used = rows_ref[i]

    def run_copy(b, loc0, slot0, nrows=RUN_ALIGN):
        return pltpu.make_async_copy(ys_ref.at[pl.ds(slot0, nrows), :],
                                     loc_ref.at[b, pl.ds(loc0, nrows), :], sem.at[b])

    @pl.when(i == 0)
    def _():
        loc_ref[...] = jnp.zeros_like(loc_ref)
        _start_run_chunks(pstart_ref, chunk_e_ref, chunk_rel_ref, rows_used,
                          lambda loc0, slot0: run_copy(buf, loc0, slot0))

    @pl.when(i < last)
    def _():
        _start_run_chunks(pstart_ref, next_e_ref, next_rel_ref, rows_ref[jnp.minimum(i + 1, last)],
                          lambda loc0, slot0: run_copy(1 - buf, loc0, slot0))

    hb = h2b_ref[...]
    a = _silu(_dot(hb, wsg_ref[...])) * _dot(hb, wsu_ref[...])
    acc_ref[...] = _dot(a.astype(BF16), wsd_ref[...])
    ls = lslot_ref[...]
    wt = wt_ref[...]
    for k in range(TOP_K):
        lsb_ref[k] = jnp.broadcast_to(ls[:, k:k + 1], (tl, LANES))
        wtb_ref[k] = jnp.broadcast_to(wt[:, k:k + 1], (tl, LANES))

    _wait_run_rows(lambda n: run_copy(buf, 0, 0, n), rows_used)

    def unsort_chunk(c, carry):
        def gate_rows(s, carry2):
            rows = pl.ds(pl.multiple_of(s * GATE_ROWS, GATE_ROWS), GATE_ROWS)
            for q in range(SLOT_CHUNK // LANES):
                col = (lax.broadcasted_iota(I32, (GATE_ROWS, LANES), 1)
                       + (c * SLOT_CHUNK + q * LANES))
                gate = jnp.zeros((GATE_ROWS, LANES), F32)
                for k in range(TOP_K):
                    gate = jnp.where(col == lsb_ref[k, rows, :], wtb_ref[k, rows, :], gate)
                hi = gate.astype(BF16)
                ghi_ref[rows, q * LANES:(q + 1) * LANES] = hi
                glo_ref[rows, q * LANES:(q + 1) * LANES] = (gate - hi.astype(F32)).astype(BF16)
            return carry2

        lax.fori_loop(0, tl // GATE_ROWS, gate_rows, 0)
        y = loc_ref[buf, pl.ds(pl.multiple_of(c * SLOT_CHUNK, SLOT_CHUNK), SLOT_CHUNK), :]
        acc_ref[...] = acc_ref[...] + _dot(ghi_ref[...], y) + _dot(glo_ref[...], y)
        return carry

    lax.fori_loop(0, (rows_used + SLOT_CHUNK - 1) // SLOT_CHUNK, unsort_chunk, 0)

    xo = x_ref[...] + mod_ref[0, 5:6, :] * acc_ref[...]
    if final:
        ms = jnp.mean(xo * xo, axis=-1, keepdims=True)
        xo = (xo * lax.rsqrt(ms + EPS)) * fg_ref[...]
    o_ref[...] = xo


def _combine(pstart, rows, chunk_e, chunk_rel, ys, lslot_tk, wt_tk, x_flat, h2b, mod_lat, wsg, wsu,
             wsd, final_g, tokens_per_batch, final):
    t, d = x_flat.shape
    tl = MOE_TILE
    ds = wsg.shape[1]
    tiles_per_batch = tokens_per_batch // tl
    body = functools.partial(_combine_body, tl=tl, final=final)
    n_tiles = t // tl
    table_spec = lambda index: BS((1, 1, CHUNK_TABLE), index, memory_space=pltpu.SMEM)
    this_tile = lambda i, *_: (i, 0, 0)
    next_tile = lambda i, *_: (jnp.minimum(i + 1, n_tiles - 1), 0, 0)
    grid_spec = pltpu.PrefetchScalarGridSpec(
        num_scalar_prefetch=2,
        grid=(n_tiles,),
        in_specs=[table_spec(this_tile), table_spec(this_tile),
                  table_spec(next_tile), table_spec(next_tile),
                  BS(memory_space=pl.ANY),
                  BS((tl, TOP_K), lambda i, *_: (i, 0)),
                  BS((tl, TOP_K), lambda i, *_: (i, 0)),
                  BS((tl, d), lambda i, *_: (i, 0)),
                  BS((tl, d), lambda i, *_: (i, 0)),
                  BS((1, SUBLANES, d), lambda i, *_: (i // tiles_per_batch, 0, 0)),
                  BS((d, ds), lambda i, *_: (0, 0)),
                  BS((d, ds), lambda i, *_: (0, 0)),
                  BS((ds, d), lambda i, *_: (0, 0)),
                  BS((1, d), lambda i, *_: (0, 0))],
        out_specs=BS((tl, d), lambda i, *_: (i, 0)),
        scratch_shapes=[pltpu.VMEM((2, LOCAL_SLOTS, d), BF16), pltpu.VMEM((tl, d), F32),
                        pltpu.VMEM((TOP_K, tl, LANES), I32), pltpu.VMEM((TOP_K, tl, LANES), F32),
                        pltpu.VMEM((tl, SLOT_CHUNK), BF16), pltpu.VMEM((tl, SLOT_CHUNK), BF16),
                        pltpu.SemaphoreType.DMA((2,))],
    )
    return pl.pallas_call(
        body,
        out_shape=SDS((t, d), F32),
        grid_spec=grid_spec,
        compiler_params=_cparams("arbitrary"),
        name="moe_combine",
    )(pstart, rows, chunk_e, chunk_rel, chunk_e, chunk_rel, ys, lslot_tk, wt_tk, x_flat, h2b,
      mod_lat, wsg, wsu, wsd, final_g)


def _moe(x_flat, mod_lat, g, w_router, e_bias, w_gate, w_up, w_down, ws_gate, ws_up, ws_down,
         final_g, tri, tokens_per_batch, layer, final):
    t, d = x_flat.shape
    bm = BM_FFN
    n_tiles = t // MOE_TILE
    h2b, lslot, wts, chunk_e, chunk_rel, rows, seg = _router(
        x_flat, mod_lat, g, w_router.T, e_bias.reshape(N_EXPERTS, 1), tri, tokens_per_batch)
    rows = rows[:, 0, 0]
    seg_rows = seg[:, 0].astype(I32)
    pcnt = (seg_rows + bm - 1) // bm * bm
    pend = jnp.cumsum(pcnt).astype(I32)
    pstart = pend - pcnt
    max_rows = t * TOP_K + n_tiles * N_EXPERTS * (RUN_ALIGN - 1)
    n_blocks = -(-max_rows // bm) + N_EXPERTS
    n_slots = n_blocks * bm
    block_row0 = jnp.arange(n_blocks, dtype=I32) * bm
    blk_e = jnp.minimum(jnp.sum((pend[None, :] <= block_row0[:, None]).astype(I32), axis=1),
                        N_EXPERTS - 1)
    n_used = pend[-1:] // bm
    xs = _dispatch(pend, pcnt, n_used, pstart, rows, chunk_e, chunk_rel, lslot, h2b, n_slots)
    ys = _ffn(blk_e, n_used, xs, w_gate, w_up, w_down, layer)
    return _combine(pstart, rows, chunk_e, chunk_rel, ys, lslot.T, wts.T, x_flat, h2b, mod_lat,
                    ws_gate.astype(BF16), ws_up.astype(BF16), ws_down.astype(BF16),
                    final_g.reshape(1, d), tokens_per_batch, final)


def _inproj_c_body(xp_ref, x_ref, xn_ref, mod_ref, g_ref, w_ref, cw_ref, cb_ref,
                   v_ref, g1_ref, g2_ref, *, tm, n_tiles):
    i = pl.program_id(1)
    halo = SUBLANES
    xe = jnp.concatenate([xp_ref[0], x_ref[0], xn_ref[0]], axis=0)
    h = _norm_mod(xe, g_ref[...], mod_ref[0, 1:2, :], mod_ref[0, 0:1, :])
    row = lax.broadcasted_iota(I32, (tm + 2 * halo, 1), 0)
    outside = jnp.logical_or(jnp.logical_and(i == 0, row < halo),
                             jnp.logical_and(i == n_tiles - 1, row >= tm + halo))
    hb = jnp.where(outside, 0.0, h).astype(BF16)
    width = v_ref.shape[-1]
    for part, o_ref in enumerate((v_ref, g1_ref, g2_ref)):
        cols = slice(part * width, (part + 1) * width)
        zp = _dot(hb, w_ref[:, cols])
        up = pltpu.roll(zp, 1, 0)
        dn = pltpu.roll(zp, tm + 2 * halo - 1, 0)
        z = cw_ref[0:1, cols] * up + cw_ref[1:2, cols] * zp + cw_ref[2:3, cols] * dn + cb_ref[:, cols]
        o_ref[0] = z[halo:halo + tm]


def _inproj_c(x, mod_lat, g, w_in, conv_w, conv_b):
    b, l, d = x.shape
    tm = TM_PROJ
    n_tiles = l // tm
    w3 = w_in.shape[1]
    width = w3 // 3
    r8 = tm // SUBLANES
    body = functools.partial(_inproj_c_body, tm=tm, n_tiles=n_tiles)
    out = SDS((b, l, width), F32)
    return pl.pallas_call(
        body,
        out_shape=(out, out, out),
        grid=(b, n_tiles),
        in_specs=[BS((1, SUBLANES, d), lambda bi, i: (bi, jnp.maximum(i * r8 - 1, 0), 0)),
                  BS((1, tm, d), lambda bi, i: (bi, i, 0)),
                  BS((1, SUBLANES, d), lambda bi, i: (bi, jnp.minimum((i + 1) * r8, l // SUBLANES - 1), 0)),
                  BS((1, SUBLANES, d), lambda bi, i: (bi, 0, 0)),
                  BS((1, d), lambda bi, i: (0, 0)),
                  BS((d, w3), lambda bi, i: (0, 0)),
                  BS((3, w3), lambda bi, i: (0, 0)),
                  BS((1, w3), lambda bi, i: (0, 0))],
        out_specs=(BS((1, tm, width), lambda bi, i: (bi, i, 0)),
                   BS((1, tm, width), lambda bi, i: (bi, i, 0)),
                   BS((1, tm, width), lambda bi, i: (bi, i, 0))),
        compiler_params=_cparams("arbitrary", "arbitrary"),
        name="inproj_c",
    )(x, x, x, mod_lat, g, w_in, conv_w, conv_b)


def _filter_body(f_ref, w1_ref, b1_ref, w2_ref, b2_ref, w3_ref, fr_ref, dl_ref, keep0_ref,
                 hf_ref, l1_ref, *, tp):
    i = pl.program_id(0)
    feats = f_ref[...]
    fr = fr_ref[...]
    a = jnp.sin(fr * (_dot_hp(feats, w1_ref[...]) + b1_ref[...]))
    a = jnp.sin(fr * (_dot_hp(a, w2_ref[...]) + b2_ref[...]))
    hf = _dot_hp(a, w3_ref[...])
    t01 = feats[:, 0:1]
    hf = hf * (jnp.exp(-t01 * jnp.abs(dl_ref[...])) + DECAY_SHIFT)
    row = lax.broadcasted_iota(I32, hf.shape, 0) + i * tp
    hf = jnp.where(row == 0, hf * keep0_ref[...], hf)
    hf_ref[...] = hf

    @pl.when(i == 0)
    def _():
        l1_ref[...] = jnp.zeros_like(l1_ref)

    l1_ref[...] = l1_ref[...] + jnp.sum(jnp.abs(hf), axis=0, keepdims=True)


def _filters(feats, w1, b1, w2, b2, w3, freq, delta, width):
    n, fe = feats.shape
    hid = w2.shape[0]
    fo = w3.shape[1]
    tp = 256
    lag0_keep = jnp.tile(jnp.repeat(jnp.array([1.0, 0.0], F32), width), fo // (2 * width)).reshape(1, fo)
    body = functools.partial(_filter_body, tp=tp)
    full = lambda shape: BS(shape, lambda i: (0, 0))
    return pl.pallas_call(
        body,
        out_shape=(SDS((n, fo), F32), SDS((SUBLANES, fo), F32)),
        grid=(n // tp,),
        in_specs=[BS((tp, fe), lambda i: (i, 0)), full((fe, hid)), full((1, hid)),
                  full((hid, hid)), full((1, hid)), full((hid, fo)), full((1, hid)), full((1, fo)),
                  full((1, fo))],
        out_specs=(BS((tp, fo), lambda i: (i, 0)), BS((SUBLANES, fo), lambda i: (0, 0))),
        compiler_params=_cparams("arbitrary"),
        name="hyena_filters",
    )(feats, w1, b1, w2, b2, w3, freq, delta, lag0_keep)


DFT_R = 128
DFT_VP = 72
DFT_UNROLL = 4


def _dft_tables(n):
    r = DFT_R
    vp = DFT_VP
    m = 2 * n
    na = n // r
    two_pi = 2.0 * np.pi
    live = (np.arange(vp) <= r // 2).astype(np.float64)
    a = np.arange(na)[None, :]
    v = np.arange(vp)[:, None]
    ang1 = two_pi * ((a * v) % r) / r
    f1 = np.concatenate([np.cos(ang1), -np.sin(ang1)], axis=0) * np.tile(live, 2)[:, None]
    b = np.arange(r)[None, None, :]
    u = np.arange(r)[None, :, None]
    vv = np.arange(vp)[:, None, None]
    ang2 = two_pi * ((b * (r * u + vv)) % m) / m
    gr, gi = np.cos(ang2), -np.sin(ang2)
    fwd = np.concatenate([np.concatenate([gr, -gi], axis=2),
                          np.concatenate([gi, gr], axis=2)], axis=1)
    hr, hi = np.transpose(gr, (0, 2, 1)), -np.transpose(gi, (0, 2, 1))
    inv = np.concatenate([np.concatenate([hr, -hi], axis=2),
                          np.concatenate([hi, hr], axis=2)], axis=1)
    weight = live * np.where((np.arange(vp) == 0) | (np.arange(vp) == r // 2), 1.0, 2.0)
    ang3 = two_pi * ((np.arange(na)[:, None] * np.arange(vp)[None, :]) % r) / r
    f3 = np.concatenate([np.cos(ang3) * weight, -np.sin(ang3) * weight], axis=1) / m
    cast = lambda t: jnp.asarray(t.astype(np.float32)).astype(BF16)
    return cast(f1), cast(fwd), cast(inv), cast(f3)


def _dft_s1_body(y_ref, f1_ref, ar_ref, ai_ref):
    na = y_ref.shape[0] // DFT_R
    f1 = f1_ref[...]

    def one_position(b, carry):
        res = _dot(f1, y_ref[pl.ds(b, na, stride=DFT_R), :].astype(BF16))
        ar_ref[pl.ds(b, DFT_VP, stride=DFT_R), :] = res[:DFT_VP]
        ai_ref[pl.ds(b, DFT_VP, stride=DFT_R), :] = res[DFT_VP:]
        return carry

    lax.fori_loop(0, DFT_R, one_position, 0, unroll=DFT_UNROLL)


def _dft_s1(y3, f1):
    nb, n, c = y3.shape
    out = SDS((nb, DFT_VP * DFT_R, c), F32)
    plane_spec = BS((None, DFT_VP * DFT_R, LANES), lambda i, cc: (i, 0, cc))
    return pl.pallas_call(
        _dft_s1_body,
        out_shape=(out, out),
        grid=(nb, c // LANES),
        in_specs=[BS((None, n, LANES), lambda i, cc: (i, 0, cc)),
                  BS((2 * DFT_VP, n // DFT_R), lambda i, cc: (0, 0))],
        out_specs=(plane_spec, plane_spec),
        compiler_params=_cparams("arbitrary", "arbitrary"),
        name="dft_stage1",
    )(y3, f1)


def _filter_spec_body(arf_ref, aif_ref, arb_ref, aib_ref, g_ref, l1f_ref, l1b_ref, kr_ref, ki_ref):
    g = g_ref[...]
    yf = _dot(g, jnp.concatenate([arf_ref[...], aif_ref[...]], axis=0).astype(BF16))
    yb = _dot(g, jnp.concatenate([arb_ref[...], aib_ref[...]], axis=0).astype(BF16))
    inv = 1.0 / (l1f_ref[0:1, :] + l1b_ref[0:1, :])
    kr_ref[...] = (yf[:DFT_R] + yb[:DFT_R]) * inv
    ki_ref[...] = (yf[DFT_R:] - yb[DFT_R:]) * inv


def _filter_spectrum(ar, ai, fwd, l1, width):
    r = DFT_R
    a_spec = lambda d: BS((None, None, r, width), lambda v, o: (0, v, 0, 2 * o + d))
    l_spec = lambda d: BS((SUBLANES, width), lambda v, o: (0, 2 * o + d))
    out = SDS((DFT_VP, r, HYENA_ORDER * width), F32)
    return pl.pallas_call(
        _filter_spec_body,
        out_shape=(out, out),
        grid=(DFT_VP, HYENA_ORDER),
        in_specs=[a_spec(0), a_spec(0), a_spec(1), a_spec(1),
                  BS((None, 2 * r, 2 * r), lambda v, o: (v, 0, 0)), l_spec(0), l_spec(1)],
        out_specs=(BS((None, r, width), lambda v, o: (v, 0, o)),
                   BS((None, r, width), lambda v, o: (v, 0, o))),
        compiler_params=_cparams("arbitrary", "arbitrary"),
        name="hyena_filter_spectrum",
    )(ar, ai, ar, ai, fwd, l1, l1)


def _conv_mid_body(ar_ref, ai_ref, g_ref, h_ref, kr_ref, ki_ref, qr_ref, qi_ref):
    y = _dot(g_ref[...], jnp.concatenate([ar_ref[...], ai_ref[...]], axis=0).astype(BF16))
    yr, yi = y[:DFT_R], y[DFT_R:]
    kr, ki = kr_ref[...], ki_ref[...]
    p = jnp.concatenate([yr * kr - yi * ki, yr * ki + yi * kr], axis=0).astype(BF16)
    q = _dot(h_ref[...], p)
    qr_ref[...] = q[:DFT_R]
    qi_ref[...] = q[DFT_R:]


def _conv_mid(ar, ai, fwd, inv, kr, ki, order):
    nb, vp, r, c = ar.shape
    a_spec = BS((None, None, r, c), lambda n, v: (n, v, 0, 0))
    m_spec = BS((None, 2 * r, 2 * r), lambda n, v: (v, 0, 0))
    k_spec = BS((None, r, c), lambda n, v: (v, 0, order))
    out = SDS((nb, vp, r, c), F32)
    return pl.pallas_call(
        _conv_mid_body,
        out_shape=(out, out),
        grid=(nb, vp),
        in_specs=[a_spec, a_spec, m_spec, m_spec, k_spec, k_spec],
        out_specs=(a_spec, a_spec),
        compiler_params=_cparams("arbitrary", "arbitrary"),
        name="hyena_spectral_product",
    )(ar, ai, fwd, inv, kr, ki)


def _idft_gate_body(qr_ref, qi_ref, f3_ref, y_ref, gate_ref, fb_ref, o_ref):
    na = y_ref.shape[0] // DFT_R
    f3 = f3_ref[...]
    fb = fb_ref[...]

    def one_position(b, carry):
        planes = pl.ds(b, DFT_VP, stride=DFT_R)
        rows = pl.ds(b, na, stride=DFT_R)
        q = jnp.concatenate([qr_ref[planes, :], qi_ref[planes, :]], axis=0).astype(BF16)
        conv = _dot(f3, q)
        o_ref[rows, :] = gate_ref[rows, :] * (conv + fb * y_ref[rows, :])
        return carry

    lax.fori_loop(0, DFT_R, one_position, 0, unroll=DFT_UNROLL)


def _idft_gate(qr, qi, f3, y3, gate3, fbias):
    nb, n, c = y3.shape
    q_spec = BS((None, DFT_VP * DFT_R, LANES), lambda i, cc: (i, 0, cc))
    y_spec = BS((None, n, LANES), lambda i, cc: (i, 0, cc))
    return pl.pallas_call(
        _idft_gate_body,
        out_shape=SDS((nb, n, c), F32),
        grid=(nb, c // LANES),
        in_specs=[q_spec, q_spec, BS((n // DFT_R, 2 * DFT_VP), lambda i, cc: (0, 0)), y_spec, y_spec,
                  BS((1, LANES), lambda i, cc: (0, cc))],
        out_specs=y_spec,
        compiler_params=_cparams("arbitrary", "arbitrary"),
        name="hyena_idft_gate",
    )(qr, qi, f3, y3, gate3, fbias)


def _outproj_body(y_ref, w_ref, x_ref, mod_ref, o_ref):
    o_ref[0] = x_ref[0] + mod_ref[0, 2:3, :] * _dot(y_ref[0].astype(BF16), w_ref[...])


def _outproj(y, w_out, x, mod_lat):
    b, l, d = x.shape
    tm = TM_PROJ
    wdt = y.shape[-1]
    return pl.pallas_call(
        _outproj_body,
        out_shape=SDS((b, l, d), F32),
        grid=(b, l // tm),
        in_specs=[BS((1, tm, wdt), lambda bi, i: (bi, i, 0)),
                  BS((wdt, d), lambda bi, i: (0, 0)),
                  BS((1, tm, d), lambda bi, i: (bi, i, 0)),
                  BS((1, SUBLANES, d), lambda bi, i: (bi, 0, 0))],
        out_specs=BS((1, tm, d), lambda bi, i: (bi, i, 0)),
        compiler_params=_cparams("arbitrary", "arbitrary"),
        name="outproj_c",
    )(y, w_out, x, mod_lat)


def _hyena(x, mod_lat, g, w_in, conv_w, conv_b, w1, b1, w2, b2, w3, freq, delta, f_bias, w_out):
    b, n, d = x.shape
    width = w_out.shape[0]
    r = DFT_R
    na = n // r
    f1, fwd, inv, f3 = _dft_tables(n)
    v, gate1, gate2 = _inproj_c(x, mod_lat, g, w_in.astype(BF16), conv_w, conv_b.reshape(1, -1))

    t = jnp.arange(n, dtype=F32)
    t01 = t / max(n - 1, 1)
    bands = jnp.linspace(1e-4, FILT_BANDS - 1, FILT_BANDS, dtype=F32)
    ang = (2.0 * math.pi / n) * t[:, None] * bands[None, :]
    feats = jnp.concatenate([t01[:, None], jnp.cos(ang), jnp.sin(ang)], axis=-1)
    fe = feats.shape[1]
    feats = jnp.pad(feats, ((0, 0), (0, LANES - fe)))
    w1p = jnp.pad(w1, ((0, LANES - fe), (0, 0)))
    hf, l1 = _filters(feats, w1p, b1.reshape(1, -1), w2, b2.reshape(1, -1), w3,
                      freq.reshape(1, -1), delta.reshape(1, -1), width)
    planes4 = lambda p: p.reshape(p.shape[0], DFT_VP, r, p.shape[2])
    planes3 = lambda p: p.reshape(p.shape[0], DFT_VP * r, p.shape[3])
    far, fai = _dft_s1(hf[None], f1)
    kr, ki = _filter_spectrum(planes4(far), planes4(fai), fwd, l1, width)

    y = v
    for o, gate in enumerate((gate1, gate2)):
        ar, ai = _dft_s1(y, f1)
        qr, qi = _conv_mid(planes4(ar), planes4(ai), fwd, inv, kr, ki, o)
        y = _idft_gate(planes3(qr), planes3(qi), f3, y, gate, f_bias[o].reshape(1, width))
    return _outproj(y, w_out.astype(BF16), x, mod_lat)


def _rope_tables(seq_len):
    rows = seq_len // GRID_W
    row = jnp.repeat(jnp.arange(rows, dtype=F32), GRID_W)
    col = jnp.tile(jnp.arange(GRID_W, dtype=F32), rows)
    inv = jnp.power(ROPE_BASE, -jnp.arange(ROPE_FREQS, dtype=F32) / ROPE_FREQS)
    ar, ac = row[:, None] * inv, col[:, None] * inv
    cos_h = jnp.concatenate([jnp.cos(ar), jnp.cos(ar), jnp.cos(ac), jnp.cos(ac)], axis=1)
    sin_h = jnp.concatenate([-jnp.sin(ar), jnp.sin(ar), -jnp.sin(ac), jnp.sin(ac)], axis=1)
    reps = LANES // HEAD_DIM
    return jnp.tile(cos_h, (1, reps)), jnp.tile(sin_h, (1, reps))


def _rotate_partner_columns(w):
    ncol = w.shape[1]
    lane = np.arange(ncol)
    partner = np.where((lane % (2 * ROPE_FREQS)) < ROPE_FREQS, lane + ROPE_FREQS, lane - ROPE_FREQS)
    return w[:, partner]


def kernel(x, c, ctx, c_ctx, w_mod, b_mod, norm_g, w_in_ab, sink, w_spatial, b_spatial, w_out_ab,
           w_in_c, conv_w, conv_b, filt_w1, filt_b1, filt_w2, filt_b2, filt_w3, filt_freq,
           filt_delta, filt_bias, w_out_c, w_router, e_bias, w_gate, w_up, w_down, ws_gate,
           ws_up, ws_down, final_g):
    b, l, d = x.shape
    depth = w_mod.shape[0]
    assert depth == 2 and b + 1 <= SUBLANES

    cc = jnp.zeros((SUBLANES, d), F32).at[:b].set(c).at[b].set(c_ctx)
    m_all = _mod_vectors(cc, w_mod, b_mod)

    def mod_rows(layer, row0, nrow):
        m = m_all[layer, row0:row0 + nrow].reshape(nrow, 6, d)
        return jnp.pad(m, ((0, 0), (0, SUBLANES - 6), (0, 0)))

    tri = jnp.triu(jnp.ones((MOE_TILE, MOE_TILE), F32), k=1).astype(BF16)

    mod_lat = mod_rows(0, 0, b)
    mod_ctx = mod_rows(0, b, 1)[0]
    w_in = w_in_ab[0]
    qk = ATTN_WIDTH + KV_WIDTH
    w_cat = jnp.concatenate([w_in, _rotate_partner_columns(w_in[:, :qk])], axis=1).astype(BF16)
    cos_t, sin_t = _rope_tables(l)
    group_avg = jnp.kron(jnp.eye(N_SG_GROUPS, dtype=F32),
                         jnp.full((SG_GROUP_DIM, SG_GROUP_DIM), 1.0 / SG_GROUP_DIM, F32)).astype(BF16)
    kc, vc = _ctx_kv(ctx, mod_ctx, norm_g[0, 0].reshape(1, d),
                     w_in[:, ATTN_WIDTH:ATTN_WIDTH + 2 * KV_WIDTH].astype(BF16))
    q, k, v, ug, vn = _inproj_ab(x, mod_lat, norm_g[0, 0].reshape(1, d), w_cat, cos_t, sin_t, group_avg)
    b_full = jnp.repeat(b_spatial[0].T, SG_GROUP_DIM, axis=1)
    x1 = _mixer(sink[0], q, k, v, kc, vc, ug, vn, w_spatial[0].astype(BF16), b_full,
                w_out_ab[0].astype(BF16), x, mod_lat)
    x2 = _moe(x1.reshape(b * l, d), mod_lat, norm_g[0, 1].reshape(1, d), w_router[0], e_bias[0],
              w_gate, w_up, w_down, ws_gate[0], ws_up[0], ws_down[0], final_g, tri, l,
              layer=0, final=False).reshape(b, l, d)

    mod_lat = mod_rows(1, 0, b)
    x3 = _hyena(x2, mod_lat, norm_g[1, 0].reshape(1, d), w_in_c[0], conv_w[0], conv_b[0],
                filt_w1[0], filt_b1[0], filt_w2[0], filt_b2[0], filt_w3[0], filt_freq[0],
                filt_delta[0], filt_bias[0], w_out_c[0])
    out = _moe(x3.reshape(b * l, d), mod_lat, norm_g[1, 1].reshape(1, d), w_router[1], e_bias[1],
               w_gate, w_up, w_down, ws_gate[1], ws_up[1], ws_down[1], final_g, tri, l,
               layer=1, final=True)
    return out.reshape(b, l, d)
```

```python
import functools
import math

import numpy as np
import jax
import jax.numpy as jnp
from jax import lax
from jax.experimental import pallas as pl
from jax.experimental.pallas import tpu as pltpu

F32 = jnp.float32
BF16 = jnp.bfloat16
I32 = jnp.int32
HIGHEST = lax.Precision.HIGHEST
SDS = jax.ShapeDtypeStruct
BS = pl.BlockSpec

EPS = 1e-6
NEG = -1e30

GRID_W = 64
N_Q_HEADS = 8
N_KV_HEADS = 2
HEAD_DIM = 64
ATTN_WIDTH = N_Q_HEADS * HEAD_DIM
KV_WIDTH = N_KV_HEADS * HEAD_DIM
WINDOW = 128
BLOCK = 128
ROPE_BASE = 10000.0
ROPE_FREQS = HEAD_DIM // 4
N_SG_GROUPS = 8
SG_GROUP_DIM = 64
SG_WIDTH = N_SG_GROUPS * SG_GROUP_DIM
HYENA_ORDER = 2
FILT_BANDS = 16
DECAY_SHIFT = 0.05
N_EXPERTS = 64
TOP_K = 8
N_GROUPS = 8
TOPK_GROUPS = 4
ROUTED_SCALE = 2.5

LANES = 128
SUBLANES = 8
VMEM_LIMIT = 56 * 1024 * 1024

TM_PROJ = 512
TQ_MIX = 256
MOE_TILE = 256
BM_FFN = 512
RUN_ALIGN = 16
SLOT_CHUNK = 512
ONEHOT_ROWS = 64
GATE_ROWS = 32
LOCAL_SLOTS = -(-(TOP_K * MOE_TILE + N_EXPERTS * (RUN_ALIGN - 1)) // SLOT_CHUNK) * SLOT_CHUNK
CHUNK_TABLE = -(-(LOCAL_SLOTS // RUN_ALIGN) // LANES) * LANES
FFN_IN_BUFS = 4


def _cparams(*sem):
    return pltpu.CompilerParams(dimension_semantics=sem, vmem_limit_bytes=VMEM_LIMIT)


def _dot(a, b):
    return jnp.dot(a, b, preferred_element_type=F32)


def _dot_nt(a, b):
    return lax.dot_general(a, b, (((1,), (1,)), ((), ())), preferred_element_type=F32)


def _dot_hp(a, b):
    return jnp.dot(a, b, preferred_element_type=F32, precision=HIGHEST)


def _norm_mod(x, g, sc, sh):
    ms = jnp.mean(x * x, axis=-1, keepdims=True)
    y = x * lax.rsqrt(ms + EPS)
    return (y * g) * (1.0 + sc) + sh


def _gelu_tanh(x):
    c = math.sqrt(2.0 / math.pi)
    return 0.5 * x * (1.0 + jnp.tanh(c * (x + 0.044715 * (x * x * x))))


def _silu(x):
    return x * jax.nn.sigmoid(x)


def _mod_body(c_ref, w_ref, b_ref, o_ref):
    o_ref[0] = _dot_hp(_silu(c_ref[...]), w_ref[0]) + b_ref[0]


def _mod_vectors(cc, w_mod, b_mod):
    depth, d, n = w_mod.shape
    tn = 1536
    return pl.pallas_call(
        _mod_body,
        out_shape=SDS((depth, SUBLANES, n), F32),
        grid=(depth, n // tn),
        in_specs=[BS((SUBLANES, d), lambda l, j: (0, 0)),
                  BS((1, d, tn), lambda l, j: (l, 0, j)),
                  BS((1, 1, tn), lambda l, j: (l, 0, j))],
        out_specs=BS((1, SUBLANES, tn), lambda l, j: (l, 0, j)),
        compiler_params=_cparams("arbitrary", "arbitrary"),
        name="mod_vectors",
    )(cc, w_mod, b_mod.reshape(depth, 1, n))


def _ctx_kv_body(ctx_ref, mod_ref, g_ref, w_ref, kc_ref, vc_ref):
    h = _norm_mod(ctx_ref[0], g_ref[...], mod_ref[1:2, :], mod_ref[0:1, :])
    z = _dot(h.astype(BF16), w_ref[...])
    kc_ref[0] = z[:, :KV_WIDTH].astype(BF16)
    vc_ref[0] = z[:, KV_WIDTH:].astype(BF16)


def _ctx_kv(ctx, mod_ctx, g, w_kv):
    b, c, d = ctx.shape
    return pl.pallas_call(
        _ctx_kv_body,
        out_shape=(SDS((b, c, KV_WIDTH), BF16), SDS((b, c, KV_WIDTH), BF16)),
        grid=(b,),
        in_specs=[BS((1, c, d), lambda i: (i, 0, 0)),
                  BS((SUBLANES, d), lambda i: (0, 0)),
                  BS((1, d), lambda i: (0, 0)),
                  BS((d, 2 * KV_WIDTH), lambda i: (0, 0))],
        out_specs=(BS((1, c, KV_WIDTH), lambda i: (i, 0, 0)),
                   BS((1, c, KV_WIDTH), lambda i: (i, 0, 0))),
        compiler_params=_cparams("arbitrary"),
        name="ctx_kv",
    )(ctx, mod_ctx, g, w_kv)


def _inproj_ab_body(x_ref, mod_ref, g_ref, w_ref, cos_ref, sin_ref, avg_ref,
                    q_ref, k_ref, v_ref, ug_ref, vn_ref):
    h = _norm_mod(x_ref[0], g_ref[...], mod_ref[0, 1:2, :], mod_ref[0, 0:1, :]).astype(BF16)
    cs = cos_ref[...]
    sn = sin_ref[...]
    rot0 = ATTN_WIDTH + 2 * KV_WIDTH + 2 * SG_WIDTH
    scale = HEAD_DIM ** -0.5
    for j in range(ATTN_WIDTH // LANES):
        z = _dot(h, w_ref[:, j * LANES:(j + 1) * LANES])
        zr = _dot(h, w_ref[:, rot0 + j * LANES:rot0 + (j + 1) * LANES])
        q_ref[0, :, j * LANES:(j + 1) * LANES] = ((z * cs + zr * sn) * scale).astype(BF16)
    zk = _dot(h, w_ref[:, ATTN_WIDTH:ATTN_WIDTH + KV_WIDTH])
    zkr = _dot(h, w_ref[:, rot0 + ATTN_WIDTH:rot0 + ATTN_WIDTH + KV_WIDTH])
    k_ref[0] = (zk * cs + zkr * sn).astype(BF16)
    v_ref[0] = _dot(h, w_ref[:, ATTN_WIDTH + KV_WIDTH:ATTN_WIDTH + 2 * KV_WIDTH]).astype(BF16)
    u0 = ATTN_WIDTH + 2 * KV_WIDTH
    ug_ref[0] = _gelu_tanh(_dot(h, w_ref[:, u0:u0 + SG_WIDTH]))
    vf = _gelu_tanh(_dot(h, w_ref[:, u0 + SG_WIDTH:u0 + 2 * SG_WIDTH]))
    avg = avg_ref[...]

    def gmean(t):
        hi = t.astype(BF16)
        lo = (t - hi.astype(F32)).astype(BF16)
        return _dot(hi, avg) + _dot(lo, avg)

    vc = vf - gmean(vf)
    vn_ref[0] = (vc * lax.rsqrt(gmean(vc * vc) + EPS)).astype(BF16)


def _inproj_ab(x, mod_lat, g, w_cat, cos_t, sin_t, avg):
    b, l, d = x.shape
    tm = TM_PROJ
    ncol = w_cat.shape[1]
    return pl.pallas_call(
        _inproj_ab_body,
        out_shape=(SDS((b, l, ATTN_WIDTH), BF16), SDS((b, l, KV_WIDTH), BF16),
                   SDS((b, l, KV_WIDTH), BF16), SDS((b, l, SG_WIDTH), F32),
                   SDS((b, l, SG_WIDTH), BF16)),
        grid=(b, l // tm),
        in_specs=[BS((1, tm, d), lambda bi, i: (bi, i, 0)),
                  BS((1, SUBLANES, d), lambda bi, i: (bi, 0, 0)),
                  BS((1, d), lambda bi, i: (0, 0)),
                  BS((d, ncol), lambda bi, i: (0, 0)),
                  BS((tm, LANES), lambda bi, i: (i, 0)),
                  BS((tm, LANES), lambda bi, i: (i, 0)),
                  BS((SG_WIDTH, SG_WIDTH), lambda bi, i: (0, 0))],
        out_specs=(BS((1, tm, ATTN_WIDTH), lambda bi, i: (bi, i, 0)),
                   BS((1, tm, KV_WIDTH), lambda bi, i: (bi, i, 0)),
                   BS((1, tm, KV_WIDTH), lambda bi, i: (bi, i, 0)),
                   BS((1, tm, SG_WIDTH), lambda bi, i: (bi, i, 0)),
                   BS((1, tm, SG_WIDTH), lambda bi, i: (bi, i, 0))),
        compiler_params=_cparams("arbitrary", "arbitrary"),
        name="inproj_ab",
    )(x, mod_lat, g, w_cat, cos_t, sin_t, avg)


def _mixer_body(sink_ref, q_ref, kp_ref, kcur_ref, kn_ref, vp_ref, vcur_ref, vn_ref,
                kc_ref, vc_ref, ug_ref, vnorm_ref, ws_ref, bs_ref, wout_ref, x_ref, mod_ref,
                o_ref, cat_ref, *, seq_len, sub_blocks):
    i = pl.program_id(1)
    kk = jnp.concatenate([kp_ref[0], kcur_ref[0], kn_ref[0]], axis=0)
    vv = jnp.concatenate([vp_ref[0], vcur_ref[0], vn_ref[0]], axis=0)
    kc = kc_ref[0]
    vc = vc_ref[0]
    span = 3 * BLOCK
    ii = lax.broadcasted_iota(I32, (BLOCK, span), 0)
    jj = lax.broadcasted_iota(I32, (BLOCK, span), 1)
    dd = jj - ii
    in_window = jnp.where(dd >= 0, jnp.where(dd <= 2 * WINDOW, 1, 0), 0)
    group = N_Q_HEADS // N_KV_HEADS
    for r in range(sub_blocks):
        rows = slice(r * BLOCK, (r + 1) * BLOCK)
        kpos = (i * sub_blocks + r - 1) * BLOCK + jj
        in_seq = jnp.where(kpos >= 0, jnp.where(kpos < seq_len, 1, 0), 0)
        bias = jnp.where(in_window * in_seq > 0, 0.0, NEG)
        qb = q_ref[0, rows, :]
        kl = kk[r * BLOCK:r * BLOCK + span]
        vl = vv[r * BLOCK:r * BLOCK + span]
        for hq in range(N_Q_HEADS):
            hk = hq // group
            ks = slice(hk * HEAD_DIM, (hk + 1) * HEAD_DIM)
            qh = qb[:, hq * HEAD_DIM:(hq + 1) * HEAD_DIM]
            s_loc = _dot_nt(qh, kl[:, ks]) + bias
            s_ctx = _dot_nt(qh, kc[:, ks])
            sk = sink_ref[hq]
            m = jnp.maximum(jnp.maximum(jnp.max(s_loc, axis=-1, keepdims=True),
                                        jnp.max(s_ctx, axis=-1, keepdims=True)), sk)
            p_loc = jnp.exp(s_loc - m)
            p_ctx = jnp.exp(s_ctx - m)
            den = (jnp.sum(p_loc, axis=-1, keepdims=True) + jnp.sum(p_ctx, axis=-1, keepdims=True)
                   + jnp.exp(sk - m))
            o = _dot(p_loc.astype(BF16), vl[:, ks]) + _dot(p_ctx.astype(BF16), vc[:, ks])
            cat_ref[rows, hq * HEAD_DIM:(hq + 1) * HEAD_DIM] = (o / den).astype(BF16)
        vnb = vnorm_ref[0, rows, :]
        ugb = ug_ref[0, rows, :]
        for g in range(N_SG_GROUPS):
            gs = slice(g * SG_GROUP_DIM, (g + 1) * SG_GROUP_DIM)
            sg = _dot(ws_ref[g], vnb[:, gs]) + bs_ref[:, gs]
            cat_ref[rows, ATTN_WIDTH + g * SG_GROUP_DIM:ATTN_WIDTH + (g + 1) * SG_GROUP_DIM] = (
                ugb[:, gs] * sg).astype(BF16)
    y = _dot(cat_ref[...], wout_ref[...])
    o_ref[0] = x_ref[0] + mod_ref[0, 2:3, :] * y


def _mixer(sink, q, k, v, kc, vc, ug, vn, w_s, b_full, w_out, x, mod_lat):
    b, l, d = x.shape
    tq = TQ_MIX
    r = tq // BLOCK
    nb = l // BLOCK
    c = kc.shape[1]
    prev_map = lambda bi, i: (bi, jnp.maximum(i * r - 1, 0), 0)
    next_map = lambda bi, i: (bi, jnp.minimum((i + 1) * r, nb - 1), 0)
    cur_map = lambda bi, i: (bi, i, 0)
    body = functools.partial(_mixer_body, seq_len=l, sub_blocks=r)
    return pl.pallas_call(
        body,
        out_shape=SDS((b, l, d), F32),
        grid=(b, l // tq),
        in_specs=[BS(memory_space=pltpu.SMEM),
                  BS((1, tq, ATTN_WIDTH), cur_map),
                  BS((1, BLOCK, KV_WIDTH), prev_map), BS((1, tq, KV_WIDTH), cur_map),
                  BS((1, BLOCK, KV_WIDTH), next_map),
                  BS((1, BLOCK, KV_WIDTH), prev_map), BS((1, tq, KV_WIDTH), cur_map),
                  BS((1, BLOCK, KV_WIDTH), next_map),
                  BS((1, c, KV_WIDTH), lambda bi, i: (bi, 0, 0)),
                  BS((1, c, KV_WIDTH), lambda bi, i: (bi, 0, 0)),
                  BS((1, tq, SG_WIDTH), cur_map), BS((1, tq, SG_WIDTH), cur_map),
                  BS((N_SG_GROUPS, BLOCK, BLOCK), lambda bi, i: (0, 0, 0)),
                  BS((BLOCK, SG_WIDTH), lambda bi, i: (0, 0)),
                  BS((d, d), lambda bi, i: (0, 0)),
                  BS((1, tq, d), cur_map),
                  BS((1, SUBLANES, d), lambda bi, i: (bi, 0, 0))],
        out_specs=BS((1, tq, d), cur_map),
        scratch_shapes=[pltpu.VMEM((tq, d), BF16)],
        compiler_params=_cparams("arbitrary", "arbitrary"),
        name="mixer_ab",
    )(sink, q, k, k, k, v, v, v, kc, vc, ug, vn, w_s, b_full, w_out, x, mod_lat)


def _router_body(x_ref, mod_ref, g_ref, wr_ref, eb_ref, tri_ref,
                 h2b_ref, lslot_ref, wts_ref, chunk_e_ref, chunk_rel_ref, rows_ref, cnt_ref,
                 carry_ref, *, tm):
    i = pl.program_id(0)

    @pl.when(i == 0)
    def _():
        carry_ref[...] = jnp.zeros_like(carry_ref)

    h2 = _norm_mod(x_ref[...], g_ref[...], mod_ref[0, 4:5, :], mod_ref[0, 3:4, :])
    h2b_ref[...] = h2.astype(BF16)

    logits = lax.dot_general(wr_ref[...], h2, (((1,), (1,)), ((), ())),
                             preferred_element_type=F32, precision=HIGHEST)
    scores = jax.nn.sigmoid(logits)
    per_group = N_EXPERTS // N_GROUPS
    shape3 = (N_GROUPS, per_group, tm)
    s3 = scores.reshape(shape3)
    b3 = (scores + eb_ref[...]).reshape(shape3)
    sub = lax.broadcasted_iota(I32, shape3, 1)
    eid = lax.broadcasted_iota(I32, shape3, 0) * per_group + sub

    m1 = jnp.max(b3, axis=1, keepdims=True)
    i1 = jnp.min(jnp.where(b3 == m1, sub, per_group), axis=1, keepdims=True)
    m2 = jnp.max(jnp.where(sub == i1, -jnp.inf, b3), axis=1, keepdims=True)
    gs = m1 + m2
    keep = []
    for g in range(N_GROUPS):
        beaten = jnp.zeros((1, tm), I32)
        for g2 in range(N_GROUPS):
            if g2 == g:
                continue
            wins = (gs[g2] >= gs[g]) if g2 < g else (gs[g2] > gs[g])
            beaten = beaten + jnp.where(wins, 1, 0)
        keep.append(jnp.where(beaten < TOPK_GROUPS, 1, 0)[None])
    keep3 = jnp.concatenate(keep, axis=0)
    val = jnp.where(keep3 > 0, b3, -jnp.inf)

    def red(fn, a):
        return fn(fn(a, axis=0, keepdims=True), axis=1, keepdims=True)

    idxs, ws = [], []
    member = jnp.zeros(shape3, F32)
    for _ in range(TOP_K):
        m = red(jnp.max, val)
        idx = red(jnp.min, jnp.where(val == m, eid, N_EXPERTS))
        hit = eid == idx
        ws.append(red(jnp.sum, jnp.where(hit, s3, 0.0)))
        val = jnp.where(hit, -jnp.inf, val)
        member = member + jnp.where(hit, 1.0, 0.0)
        idxs.append(idx)
    wsum = ws[0]
    for w in ws[1:]:
        wsum = wsum + w

    member2 = member.reshape(N_EXPERTS, tm)
    cnt = jnp.sum(member2, axis=1, keepdims=True)
    runlen = jnp.floor((cnt + (RUN_ALIGN - 1)) * (1.0 / RUN_ALIGN)) * RUN_ALIGN
    runlen_b = jnp.broadcast_to(runlen, (N_EXPERTS, LANES))
    e_row = lax.broadcasted_iota(I32, (N_EXPERTS, N_EXPERTS), 0)
    e_col = lax.broadcasted_iota(I32, (N_EXPERTS, N_EXPERTS), 1)
    earlier = jnp.where(e_col < e_row, 1.0, 0.0).astype(BF16)
    loff = _dot(earlier, runlen_b.astype(BF16))
    slot = _dot(member2.astype(BF16), tri_ref[...]) + loff[:, 0:1]
    slot3 = slot.reshape(shape3)
    for k in range(TOP_K):
        wts_ref[k:k + 1, :] = (ws[k] / wsum * ROUTED_SCALE).reshape(1, tm)
        sk = red(jnp.sum, jnp.where(eid == idxs[k], slot3, 0.0))
        lslot_ref[k:k + 1, :] = sk.reshape(1, tm).astype(I32)
    chunk_hi = (loff[:, 0:1] + runlen) * (1.0 / RUN_ALIGN)
    j = lax.broadcasted_iota(I32, (N_EXPERTS, CHUNK_TABLE), 1).astype(F32)
    e_of_chunk = jnp.sum(jnp.where(chunk_hi <= j, 1.0, 0.0), axis=0, keepdims=True)
    e_iota = lax.broadcasted_iota(I32, (N_EXPERTS, CHUNK_TABLE), 0).astype(F32)
    seg_off = carry_ref[:, 0:1] - loff[:, 0:1]
    rel = jnp.sum(jnp.where(e_iota == e_of_chunk, seg_off, 0.0), axis=0, keepdims=True)
    chunk_e_ref[0] = jnp.minimum(e_of_chunk, N_EXPERTS - 1.0).astype(I32)
    chunk_rel_ref[0] = (rel + j[0:1, :] * RUN_ALIGN).astype(I32)
    rows_ref[0] = (loff[N_EXPERTS - 1:N_EXPERTS, :] + runlen_b[N_EXPERTS - 1:N_EXPERTS, :]).astype(I32)
    total = carry_ref[...] + runlen_b
    carry_ref[...] = total
    cnt_ref[...] = total


def _router(x_flat, mod_lat, g, wr_t, e_bias, tri, tokens_per_batch):
    t, d = x_flat.shape
    tm = MOE_TILE
    n_tiles = t // tm
    tiles_per_batch = tokens_per_batch // tm
    body = functools.partial(_router_body, tm=tm)
    table = SDS((n_tiles, 1, CHUNK_TABLE), I32)
    table_spec = BS((1, 1, CHUNK_TABLE), lambda i: (i, 0, 0))
    return pl.pallas_call(
        body,
        out_shape=(SDS((t, d), BF16), SDS((TOP_K, t), I32), SDS((TOP_K, t), F32),
                   table, table, SDS((n_tiles, 1, LANES), I32), SDS((N_EXPERTS, LANES), F32)),
        grid=(n_tiles,),
        in_specs=[BS((tm, d), lambda i: (i, 0)),
                  BS((1, SUBLANES, d), lambda i: (i // tiles_per_batch, 0, 0)),
                  BS((1, d), lambda i: (0, 0)),
                  BS((N_EXPERTS, d), lambda i: (0, 0)),
                  BS((N_EXPERTS, 1), lambda i: (0, 0)),
                  BS((tm, tm), lambda i: (0, 0))],
        out_specs=(BS((tm, d), lambda i: (i, 0)),
                   BS((TOP_K, tm), lambda i: (0, i)),
                   BS((TOP_K, tm), lambda i: (0, i)),
                   table_spec, table_spec, BS((1, 1, LANES), lambda i: (i, 0, 0)),
                   BS((N_EXPERTS, LANES), lambda i: (0, 0))),
        scratch_shapes=[pltpu.VMEM((N_EXPERTS, LANES), F32)],
        compiler_params=_cparams("arbitrary"),
        name="moe_router",
    )(x_flat, mod_lat, g, wr_t, e_bias, tri)


def _start_run_chunks(pstart_ref, chunk_e_ref, chunk_rel_ref, rows, chunk_copy):
    def body(j, carry):
        slot0 = pstart_ref[chunk_e_ref[0, 0, j]] + chunk_rel_ref[0, 0, j]
        chunk_copy(pl.multiple_of(j * RUN_ALIGN, RUN_ALIGN),
                   pl.multiple_of(slot0, RUN_ALIGN)).start()
        return carry

    lax.fori_loop(0, rows // RUN_ALIGN, body, 0)


def _wait_run_rows(copy_of_rows, rows):
    def wait_n(nrows):
        def body(_, carry):
            copy_of_rows(nrows).wait()
            return carry
        return body

    lax.fori_loop(0, rows // SLOT_CHUNK, wait_n(SLOT_CHUNK), 0)
    lax.fori_loop(0, (rows % SLOT_CHUNK) // RUN_ALIGN, wait_n(RUN_ALIGN), 0)


def _dispatch_body(pend_ref, pcnt_ref, nu_ref, pstart_ref, rows_ref, chunk_e_ref, chunk_rel_ref,
                   lslot_ref, h_ref, xs_ref, loc_ref, oh_ref, zbuf_ref, sem, zsem, *,
                   tl, bm, n_blocks):
    i = pl.program_id(0)
    last = pl.num_programs(0) - 1
    buf = i % 2

    def zero_copy(row0):
        return pltpu.make_async_copy(
            zbuf_ref, xs_ref.at[pl.ds(pl.multiple_of(row0, RUN_ALIGN), bm), :], zsem)

    @pl.when(i == 0)
    def _():
        zbuf_ref[...] = jnp.zeros_like(zbuf_ref)

        def start(e, c):
            @pl.when(pcnt_ref[e] > 0)
            def _():
                zero_copy(pend_ref[e] - bm).start()
            return c

        def wait(e, c):
            @pl.when(pcnt_ref[e] > 0)
            def _():
                zero_copy(pend_ref[e] - bm).wait()
            return c

        def start_tail(j, c):
            zero_copy(j * bm).start()
            return c

        def wait_tail(j, c):
            zero_copy(j * bm).wait()
            return c

        lax.fori_loop(0, N_EXPERTS, start, 0)
        lax.fori_loop(nu_ref[0], n_blocks, start_tail, 0)
        lax.fori_loop(0, N_EXPERTS, wait, 0)
        lax.fori_loop(nu_ref[0], n_blocks, wait_tail, 0)

    rows_used = rows_ref[i]
    ls = lslot_ref[...]
    h = h_ref[...]

    def sort_chunk(c, carry):
        for s in range(SLOT_CHUNK // ONEHOT_ROWS):
            row = (lax.broadcasted_iota(I32, (ONEHOT_ROWS, tl), 0)
                   + (c * SLOT_CHUNK + s * ONEHOT_ROWS))
            onehot = jnp.zeros((ONEHOT_ROWS, tl), F32)
            for k in range(TOP_K):
                onehot = jnp.where(row == ls[k:k + 1, :], 1.0, onehot)
            oh_ref[s * ONEHOT_ROWS:(s + 1) * ONEHOT_ROWS, :] = onehot.astype(BF16)
        rows = pl.ds(pl.multiple_of(c * SLOT_CHUNK, SLOT_CHUNK), SLOT_CHUNK)
        loc_ref[buf, rows, :] = _dot(oh_ref[...], h).astype(BF16)
        return carry

    lax.fori_loop(0, (rows_used + SLOT_CHUNK - 1) // SLOT_CHUNK, sort_chunk, 0)

    def run_copy(b, loc0, slot0, nrows=RUN_ALIGN):
        return pltpu.make_async_copy(loc_ref.at[b, pl.ds(loc0, nrows), :],
                                     xs_ref.at[pl.ds(slot0, nrows), :], sem.at[b])

    _start_run_chunks(pstart_ref, chunk_e_ref, chunk_rel_ref, rows_used,
                      lambda loc0, slot0: run_copy(buf, loc0, slot0))

    @pl.when(i > 0)
    def _():
        _wait_run_rows(lambda n: run_copy(1 - buf, 0, 0, n), rows_ref[jnp.maximum(i - 1, 0)])

    @pl.when(i == last)
    def _():
        _wait_run_rows(lambda n: run_copy(buf, 0, 0, n), rows_used)


def _dispatch(pend, pcnt, n_used, pstart, rows, chunk_e, chunk_rel, lslot, h2b, n_slots):
    t, d = h2b.shape
    tl = MOE_TILE
    body = functools.partial(_dispatch_body, tl=tl, bm=BM_FFN, n_blocks=n_slots // BM_FFN)
    table_spec = lambda index: BS((1, 1, CHUNK_TABLE), index, memory_space=pltpu.SMEM)
    grid_spec = pltpu.PrefetchScalarGridSpec(
        num_scalar_prefetch=5,
        grid=(t // tl,),
        in_specs=[table_spec(lambda i, *_: (i, 0, 0)),
                  table_spec(lambda i, *_: (i, 0, 0)),
                  BS((TOP_K, tl), lambda i, *_: (0, i)),
                  BS((tl, d), lambda i, *_: (i, 0))],
        out_specs=BS(memory_space=pl.ANY),
        scratch_shapes=[pltpu.VMEM((2, LOCAL_SLOTS, d), BF16), pltpu.VMEM((SLOT_CHUNK, tl), BF16),
                        pltpu.VMEM((BM_FFN, d), BF16),
                        pltpu.SemaphoreType.DMA((2,)), pltpu.SemaphoreType.DMA(())],
    )
    return pl.pallas_call(
        body,
        out_shape=SDS((n_slots, d), BF16),
        grid_spec=grid_spec,
        compiler_params=_cparams("arbitrary"),
        name="moe_dispatch",
    )(pend, pcnt, n_used, pstart, rows, chunk_e, chunk_rel, lslot, h2b)


def _ffn_body(be_ref, nu_ref, xs_ref, wg_ref, wu_ref, wd_ref, ys_ref, wgb_ref, wub_ref, wdb_ref,
              xbuf_ref, ybuf_ref, isem, osem, *, bm, n_blocks):
    i = pl.program_id(0)
    nu = nu_ref[0]

    def block_rows(blk):
        return pl.ds(pl.multiple_of(blk * bm, bm), bm)

    def in_copy(blk, slot):
        return pltpu.make_async_copy(xs_ref.at[block_rows(blk), :], xbuf_ref.at[slot], isem.at[slot])

    def out_copy(blk, slot):
        return pltpu.make_async_copy(ybuf_ref.at[slot], ys_ref.at[block_rows(blk), :], osem.at[slot])

    @pl.when(i == 0)
    def _():
        for s in range(FFN_IN_BUFS):
            @pl.when(s < nu)
            def _():
                in_copy(s, s).start()

    fresh = jnp.logical_or(i == 0, be_ref[i] != be_ref[jnp.maximum(i - 1, 0)])

    @pl.when(jnp.logical_and(fresh, i < nu))
    def _():
        wgb_ref[...] = wg_ref[...].astype(BF16)
        wub_ref[...] = wu_ref[...].astype(BF16)
        wdb_ref[...] = wd_ref[...].astype(BF16)

    @pl.when(i < nu)
    def _():
        slot = i % FFN_IN_BUFS
        oslot = i % 2
        in_copy(i, slot).wait()
        x = xbuf_ref[slot]
        a = _silu(_dot(x, wgb_ref[...])) * _dot(x, wub_ref[...])
        y = _dot(a.astype(BF16), wdb_ref[...]).astype(BF16)

        @pl.when(i >= 2)
        def _():
            out_copy(i - 2, oslot).wait()

        ybuf_ref[oslot] = y
        out_copy(i, oslot).start()

        @pl.when(i + FFN_IN_BUFS < nu)
        def _():
            in_copy(i + FFN_IN_BUFS, slot).start()

    @pl.when(i == nu - 1)
    def _():
        @pl.when(i >= 1)
        def _():
            out_copy(i - 1, (i - 1) % 2).wait()

        out_copy(i, i % 2).wait()
        ybuf_ref[0] = jnp.zeros((bm, ybuf_ref.shape[-1]), BF16)

        def start_tail(j, c):
            out_copy(j, 0).start()
            return c

        def wait_tail(j, c):
            out_copy(j, 0).wait()
            return c

        lax.fori_loop(nu, n_blocks, start_tail, 0)
        lax.fori_loop(nu, n_blocks, wait_tail, 0)


def _ffn(blk_e, n_used, xs, w_gate, w_up, w_down, layer):
    n_slots, d = xs.shape
    bm = BM_FFN
    de = w_gate.shape[-1]
    n_blocks = n_slots // bm
    body = functools.partial(_ffn_body, bm=bm, n_blocks=n_blocks)
    grid_spec = pltpu.PrefetchScalarGridSpec(
        num_scalar_prefetch=2,
        grid=(n_blocks,),
        in_specs=[BS(memory_space=pl.ANY),
                  BS((None, None, d, de), lambda i, be, nu: (layer, be[i], 0, 0)),
                  BS((None, None, d, de), lambda i, be, nu: (layer, be[i], 0, 0)),
                  BS((None, None, de, d), lambda i, be, nu: (layer, be[i], 0, 0))],
        out_specs=BS(memory_space=pl.ANY),
        scratch_shapes=[pltpu.VMEM((d, de), BF16), pltpu.VMEM((d, de), BF16),
                        pltpu.VMEM((de, d), BF16),
                        pltpu.VMEM((FFN_IN_BUFS, bm, d), BF16), pltpu.VMEM((2, bm, d), BF16),
                        pltpu.SemaphoreType.DMA((FFN_IN_BUFS,)), pltpu.SemaphoreType.DMA((2,))],
    )
    return pl.pallas_call(
        body,
        out_shape=SDS((n_slots, d), BF16),
        grid_spec=grid_spec,
        compiler_params=_cparams("arbitrary"),
        name="moe_experts",
    )(blk_e, n_used, xs, w_gate, w_up, w_down)


def _combine_body(pstart_ref, rows_ref, chunk_e_ref, chunk_rel_ref, next_e_ref, next_rel_ref, ys_ref,
                  lslot_ref, wt_ref, x_ref, h2b_ref, mod_ref, wsg_ref, wsu_ref, wsd_ref, fg_ref,
                  o_ref, loc_ref, acc_ref, lsb_ref, wtb_ref, gate_ref, sem, *, tl, final):
    i = pl.program_id(0)
    last = pl.num_programs(0) - 1
    buf = i % 2
    rows_used = rows_ref[i]

    def run_copy(b, loc0, slot0, nrows=RUN_ALIGN):
        return pltpu.make_async_copy(ys_ref.at[pl.ds(slot0, nrows), :],
                                     loc_ref.at[b, pl.ds(loc0, nrows), :], sem.at[b])

    @pl.when(i == 0)
    def _():
        loc_ref[...] = jnp.zeros_like(loc_ref)
        _start_run_chunks(pstart_ref, chunk_e_ref, chunk_rel_ref, rows_used,
                          lambda loc0, slot0: run_copy(buf, loc0, slot0))

    @pl.when(i < last)
    def _():
        _start_run_chunks(pstart_ref, next_e_ref, next_rel_ref, rows_ref[jnp.minimum(i + 1, last)],
                          lambda loc0, slot0: run_copy(1 - buf, loc0, slot0))

    hb = h2b_ref[...]
    a = _silu(_dot(hb, wsg_ref[...])) * _dot(hb, wsu_ref[...])
    acc_ref[...] = _dot(a.astype(BF16), wsd_ref[...])
    ls = lslot_ref[...]
    wt = wt_ref[...]
    for k in range(TOP_K):
        lsb_ref[k] = jnp.broadcast_to(ls[:, k:k + 1], (tl, LANES))
        wtb_ref[k] = jnp.broadcast_to(wt[:, k:k + 1], (tl, LANES))

    _wait_run_rows(lambda n: run_copy(buf, 0, 0, n), rows_used)

    def unsort_chunk(c, carry):
        def gate_rows(s, carry2):
            rows = pl.ds(pl.multiple_of(s * GATE_ROWS, GATE_ROWS), GATE_ROWS)
            for q in range(SLOT_CHUNK // LANES):
                col = (lax.broadcasted_iota(I32, (GATE_ROWS, LANES), 1)
                       + (c * SLOT_CHUNK + q * LANES))
                gate = jnp.zeros((GATE_ROWS, LANES), F32)
                for k in range(TOP_K):
                    gate = jnp.where(col == lsb_ref[k, rows, :], wtb_ref[k, rows, :], gate)
                gate_ref[rows, q * LANES:(q + 1) * LANES] = gate.astype(BF16)
            return carry2

        lax.fori_loop(0, tl // GATE_ROWS, gate_rows, 0)
        y = loc_ref[buf, pl.ds(pl.multiple_of(c * SLOT_CHUNK, SLOT_CHUNK), SLOT_CHUNK), :]
        acc_ref[...] = acc_ref[...] + _dot(gate_ref[...], y)
        return carry

    lax.fori_loop(0, (rows_used + SLOT_CHUNK - 1) // SLOT_CHUNK, unsort_chunk, 0)

    xo = x_ref[...] + mod_ref[0, 5:6, :] * acc_ref[...]
    if final:
        ms = jnp.mean(xo * xo, axis=-1, keepdims=True)
        xo = (xo * lax.rsqrt(ms + EPS)) * fg_ref[...]
    o_ref[...] = xo


def _combine(pstart, rows, chunk_e, chunk_rel, ys, lslot_tk, wt_tk, x_flat, h2b, mod_lat, wsg, wsu,
             wsd, final_g, tokens_per_batch, final):
    t, d = x_flat.shape
    tl = MOE_TILE
    ds = wsg.shape[1]
    tiles_per_batch = tokens_per_batch // tl
    body = functools.partial(_combine_body, tl=tl, final=final)
    n_tiles = t // tl
    table_spec = lambda index: BS((1, 1, CHUNK_TABLE), index, memory_space=pltpu.SMEM)
    this_tile = lambda i, *_: (i, 0, 0)
    next_tile = lambda i, *_: (jnp.minimum(i + 1, n_tiles - 1), 0, 0)
    grid_spec = pltpu.PrefetchScalarGridSpec(
        num_scalar_prefetch=2,
        grid=(n_tiles,),
        in_specs=[table_spec(this_tile), table_spec(this_tile),
                  table_spec(next_tile), table_spec(next_tile),
                  BS(memory_space=pl.ANY),
                  BS((tl, TOP_K), lambda i, *_: (i, 0)),
                  BS((tl, TOP_K), lambda i, *_: (i, 0)),
                  BS((tl, d), lambda i, *_: (i, 0)),
                  BS((tl, d), lambda i, *_: (i, 0)),
                  BS((1, SUBLANES, d), lambda i, *_: (i // tiles_per_batch, 0, 0)),
                  BS((d, ds), lambda i, *_: (0, 0)),
                  BS((d, ds), lambda i, *_: (0, 0)),
                  BS((ds, d), lambda i, *_: (0, 0)),
                  BS((1, d), lambda i, *_: (0, 0))],
        out_specs=BS((tl, d), lambda i, *_: (i, 0)),
        scratch_shapes=[pltpu.VMEM((2, LOCAL_SLOTS, d), BF16), pltpu.VMEM((tl, d), F32),
                        pltpu.VMEM((TOP_K, tl, LANES), I32), pltpu.VMEM((TOP_K, tl, LANES), F32),
                        pltpu.VMEM((tl, SLOT_CHUNK), BF16), pltpu.SemaphoreType.DMA((2,))],
    )
    return pl.pallas_call(
        body,
        out_shape=SDS((t, d), F32),
        grid_spec=grid_spec,
        compiler_params=_cparams("arbitrary"),
        name="moe_combine",
    )(pstart, rows, chunk_e, chunk_rel, chunk_e, chunk_rel, ys, lslot_tk, wt_tk, x_flat, h2b,
      mod_lat, wsg, wsu, wsd, final_g)


def _moe(x_flat, mod_lat, g, w_router, e_bias, w_gate, w_up, w_down, ws_gate, ws_up, ws_down,
         final_g, tri, tokens_per_batch, layer, final):
    t, d = x_flat.shape
    bm = BM_FFN
    n_tiles = t // MOE_TILE
    h2b, lslot, wts, chunk_e, chunk_rel, rows, seg = _router(
        x_flat, mod_lat, g, w_router.T, e_bias.reshape(N_EXPERTS, 1), tri, tokens_per_batch)
    rows = rows[:, 0, 0]
    seg_rows = seg[:, 0].astype(I32)
    pcnt = (seg_rows + bm - 1) // bm * bm
    pend = jnp.cumsum(pcnt).astype(I32)
    pstart = pend - pcnt
    max_rows = t * TOP_K + n_tiles * N_EXPERTS * (RUN_ALIGN - 1)
    n_blocks = -(-max_rows // bm) + N_EXPERTS
    n_slots = n_blocks * bm
    block_row0 = jnp.arange(n_blocks, dtype=I32) * bm
    blk_e = jnp.minimum(jnp.sum((pend[None, :] <= block_row0[:, None]).astype(I32), axis=1),
                        N_EXPERTS - 1)
    n_used = pend[-1:] // bm
    xs = _dispatch(pend, pcnt, n_used, pstart, rows, chunk_e, chunk_rel, lslot, h2b, n_slots)
    ys = _ffn(blk_e, n_used, xs, w_gate, w_up, w_down, layer)
    return _combine(pstart, rows, chunk_e, chunk_rel, ys, lslot.T, wts.T, x_flat, h2b, mod_lat,
                    ws_gate.astype(BF16), ws_up.astype(BF16), ws_down.astype(BF16),
                    final_g.reshape(1, d), tokens_per_batch, final)


def _inproj_c_body(xp_ref, x_ref, xn_ref, mod_ref, g_ref, w_ref, cw_ref, cb_ref,
                   v_ref, g1_ref, g2_ref, *, tm, n_tiles):
    i = pl.program_id(1)
    halo = SUBLANES
    xe = jnp.concatenate([xp_ref[0], x_ref[0], xn_ref[0]], axis=0)
    h = _norm_mod(xe, g_ref[...], mod_ref[0, 1:2, :], mod_ref[0, 0:1, :])
    row = lax.broadcasted_iota(I32, (tm + 2 * halo, 1), 0)
    outside = jnp.logical_or(jnp.logical_and(i == 0, row < halo),
                             jnp.logical_and(i == n_tiles - 1, row >= tm + halo))
    hb = jnp.where(outside, 0.0, h).astype(BF16)
    width = v_ref.shape[-1]
    for part, o_ref in enumerate((v_ref, g1_ref, g2_ref)):
        cols = slice(part * width, (part + 1) * width)
        zp = _dot(hb, w_ref[:, cols])
        up = pltpu.roll(zp, 1, 0)
        dn = pltpu.roll(zp, tm + 2 * halo - 1, 0)
        z = cw_ref[0:1, cols] * up + cw_ref[1:2, cols] * zp + cw_ref[2:3, cols] * dn + cb_ref[:, cols]
        o_ref[0] = z[halo:halo + tm]


def _inproj_c(x, mod_lat, g, w_in, conv_w, conv_b):
    b, l, d = x.shape
    tm = TM_PROJ
    n_tiles = l // tm
    w3 = w_in.shape[1]
    width = w3 // 3
    r8 = tm // SUBLANES
    body = functools.partial(_inproj_c_body, tm=tm, n_tiles=n_tiles)
    out = SDS((b, l, width), F32)
    return pl.pallas_call(
        body,
        out_shape=(out, out, out),
        grid=(b, n_tiles),
        in_specs=[BS((1, SUBLANES, d), lambda bi, i: (bi, jnp.maximum(i * r8 - 1, 0), 0)),
                  BS((1, tm, d), lambda bi, i: (bi, i, 0)),
                  BS((1, SUBLANES, d), lambda bi, i: (bi, jnp.minimum((i + 1) * r8, l // SUBLANES - 1), 0)),
                  BS((1, SUBLANES, d), lambda bi, i: (bi, 0, 0)),
                  BS((1, d), lambda bi, i: (0, 0)),
                  BS((d, w3), lambda bi, i: (0, 0)),
                  BS((3, w3), lambda bi, i: (0, 0)),
                  BS((1, w3), lambda bi, i: (0, 0))],
        out_specs=(BS((1, tm, width), lambda bi, i: (bi, i, 0)),
                   BS((1, tm, width), lambda bi, i: (bi, i, 0)),
                   BS((1, tm, width), lambda bi, i: (bi, i, 0))),
        compiler_params=_cparams("arbitrary", "arbitrary"),
        name="inproj_c",
    )(x, x, x, mod_lat, g, w_in, conv_w, conv_b)


def _filter_body(f_ref, w1_ref, b1_ref, w2_ref, b2_ref, w3_ref, fr_ref, dl_ref, keep0_ref,
                 hf_ref, l1_ref, *, tp):
    i = pl.program_id(0)
    feats = f_ref[...]
    fr = fr_ref[...]
    a = jnp.sin(fr * (_dot_hp(feats, w1_ref[...]) + b1_ref[...]))
    a = jnp.sin(fr * (_dot_hp(a, w2_ref[...]) + b2_ref[...]))
    hf = _dot_hp(a, w3_ref[...])
    t01 = feats[:, 0:1]
    hf = hf * (jnp.exp(-t01 * jnp.abs(dl_ref[...])) + DECAY_SHIFT)
    row = lax.broadcasted_iota(I32, hf.shape, 0) + i * tp
    hf = jnp.where(row == 0, hf * keep0_ref[...], hf)
    hf_ref[...] = hf

    @pl.when(i == 0)
    def _():
        l1_ref[...] = jnp.zeros_like(l1_ref)

    l1_ref[...] = l1_ref[...] + jnp.sum(jnp.abs(hf), axis=0, keepdims=True)


def _filters(feats, w1, b1, w2, b2, w3, freq, delta, width):
    n, fe = feats.shape
    hid = w2.shape[0]
    fo = w3.shape[1]
    tp = 256
    lag0_keep = jnp.tile(jnp.repeat(jnp.array([1.0, 0.0], F32), width), fo // (2 * width)).reshape(1, fo)
    body = functools.partial(_filter_body, tp=tp)
    full = lambda shape: BS(shape, lambda i: (0, 0))
    return pl.pallas_call(
        body,
        out_shape=(SDS((n, fo), F32), SDS((SUBLANES, fo), F32)),
        grid=(n // tp,),
        in_specs=[BS((tp, fe), lambda i: (i, 0)), full((fe, hid)), full((1, hid)),
                  full((hid, hid)), full((1, hid)), full((hid, fo)), full((1, hid)), full((1, fo)),
                  full((1, fo))],
        out_specs=(BS((tp, fo), lambda i: (i, 0)), BS((SUBLANES, fo), lambda i: (0, 0))),
        compiler_params=_cparams("arbitrary"),
        name="hyena_filters",
    )(feats, w1, b1, w2, b2, w3, freq, delta, lag0_keep)


DFT_R = 128
DFT_VP = 72
DFT_BGROUP = 16
DFT_LANE_TILES = 4


def _dft_tables(n):
    r = DFT_R
    vp = DFT_VP
    m = 2 * n
    na = n // r
    two_pi = 2.0 * np.pi
    live = (np.arange(vp) <= r // 2).astype(np.float64)
    a = np.arange(na)[None, :]
    v = np.arange(vp)[:, None]
    ang1 = two_pi * ((a * v) % r) / r
    f1 = np.concatenate([np.cos(ang1), -np.sin(ang1)], axis=0) * np.tile(live, 2)[:, None]
    b = np.arange(r)[None, None, :]
    u = np.arange(r)[None, :, None]
    vv = np.arange(vp)[:, None, None]
    ang2 = two_pi * ((b * (r * u + vv)) % m) / m
    gr, gi = np.cos(ang2), -np.sin(ang2)
    fwd = np.concatenate([np.concatenate([gr, -gi], axis=2),
                          np.concatenate([gi, gr], axis=2)], axis=1)
    hr, hi = np.transpose(gr, (0, 2, 1)), -np.transpose(gi, (0, 2, 1))
    inv = np.concatenate([np.concatenate([hr, -hi], axis=2),
                          np.concatenate([hi, hr], axis=2)], axis=1)
    weight = live * np.where((np.arange(vp) == 0) | (np.arange(vp) == r // 2), 1.0, 2.0)
    ang3 = two_pi * ((np.arange(na)[:, None] * np.arange(vp)[None, :]) % r) / r
    f3 = np.concatenate([np.cos(ang3) * weight, -np.sin(ang3) * weight], axis=1) / m
    cast = lambda t: jnp.asarray(t.astype(np.float32)).astype(BF16)
    return cast(f1), cast(fwd), cast(inv), cast(f3)


def _lane_tile_specs(rows, index_map_of_tile):
    return [BS((None, rows, DFT_BGROUP, LANES), index_map_of_tile(t)) for t in range(DFT_LANE_TILES)]


def _rows_of_position(ref, j):
    x, bg, _ = ref.shape
    return ref.reshape(x * bg, LANES)[pl.ds(j, x, stride=bg), :]


def _dft_s1_body(*refs):
    q = DFT_LANE_TILES
    y_refs, f1_ref, ar_ref, ai_ref = refs[:q], refs[q], refs[q + 1], refs[q + 2]
    f1 = f1_ref[...]
    for t in range(q):
        lanes = slice(t * LANES, (t + 1) * LANES)
        for j in range(DFT_BGROUP):
            res = _dot(f1, _rows_of_position(y_refs[t], j).astype(BF16))
            ar_ref[:, j, lanes] = res[:DFT_VP]
            ai_ref[:, j, lanes] = res[DFT_VP:]


def _dft_s1(y4, f1):
    nb, na, r, c = y4.shape
    q, bg = DFT_LANE_TILES, DFT_BGROUP
    out = SDS((nb, DFT_VP, r, c), F32)
    plane_spec = BS((None, DFT_VP, bg, q * LANES), lambda i, j, cc: (i, 0, j, cc))
    tile_map = lambda t: (lambda i, j, cc: (i, 0, j, cc * q + t))
    return pl.pallas_call(
        _dft_s1_body,
        out_shape=(out, out),
        grid=(nb, r // bg, c // (q * LANES)),
        in_specs=_lane_tile_specs(na, tile_map) + [BS((2 * DFT_VP, na), lambda i, j, cc: (0, 0))],
        out_specs=(plane_spec, plane_spec),
        compiler_params=_cparams("arbitrary", "arbitrary", "arbitrary"),
        name="dft_stage1",
    )(*([y4] * q), f1)


def _filter_spec_body(arf_ref, aif_ref, arb_ref, aib_ref, g_ref, l1f_ref, l1b_ref, kr_ref, ki_ref):
    g = g_ref[...]
    yf = _dot(g, jnp.concatenate([arf_ref[...], aif_ref[...]], axis=0).astype(BF16))
    yb = _dot(g, jnp.concatenate([arb_ref[...], aib_ref[...]], axis=0).astype(BF16))
    inv = 1.0 / (l1f_ref[0:1, :] + l1b_ref[0:1, :])
    kr_ref[...] = (yf[:DFT_R] + yb[:DFT_R]) * inv
    ki_ref[...] = (yf[DFT_R:] - yb[DFT_R:]) * inv


def _filter_spectrum(ar, ai, fwd, l1, width):
    r = DFT_R
    a_spec = lambda d: BS((None, None, r, width), lambda v, o: (0, v, 0, 2 * o + d))
    l_spec = lambda d: BS((SUBLANES, width), lambda v, o: (0, 2 * o + d))
    out = SDS((DFT_VP, r, HYENA_ORDER * width), F32)
    return pl.pallas_call(
        _filter_spec_body,
        out_shape=(out, out),
        grid=(DFT_VP, HYENA_ORDER),
        in_specs=[a_spec(0), a_spec(0), a_spec(1), a_spec(1),
                  BS((None, 2 * r, 2 * r), lambda v, o: (v, 0, 0)), l_spec(0), l_spec(1)],
        out_specs=(BS((None, r, width), lambda v, o: (v, 0, o)),
                   BS((None, r, width), lambda v, o: (v, 0, o))),
        compiler_params=_cparams("arbitrary", "arbitrary"),
        name="hyena_filter_spectrum",
    )(ar, ai, ar, ai, fwd, l1, l1)


def _conv_mid_body(ar_ref, ai_ref, g_ref, h_ref, kr_ref, ki_ref, qr_ref, qi_ref):
    y = _dot(g_ref[...], jnp.concatenate([ar_ref[...], ai_ref[...]], axis=0).astype(BF16))
    yr, yi = y[:DFT_R], y[DFT_R:]
    kr, ki = kr_ref[...], ki_ref[...]
    p = jnp.concatenate([yr * kr - yi * ki, yr * ki + yi * kr], axis=0).astype(BF16)
    q = _dot(h_ref[...], p)
    qr_ref[...] = q[:DFT_R]
    qi_ref[...] = q[DFT_R:]


def _conv_mid(ar, ai, fwd, inv, kr, ki, order):
    nb, vp, r, c = ar.shape
    a_spec = BS((None, None, r, c), lambda n, v: (n, v, 0, 0))
    m_spec = BS((None, 2 * r, 2 * r), lambda n, v: (v, 0, 0))
    k_spec = BS((None, r, c), lambda n, v: (v, 0, order))
    out = SDS((nb, vp, r, c), F32)
    return pl.pallas_call(
        _conv_mid_body,
        out_shape=(out, out),
        grid=(nb, vp),
        in_specs=[a_spec, a_spec, m_spec, m_spec, k_spec, k_spec],
        out_specs=(a_spec, a_spec),
        compiler_params=_cparams("arbitrary", "arbitrary"),
        name="hyena_spectral_product",
    )(ar, ai, fwd, inv, kr, ki)


def _idft_gate_body(*refs):
    q = DFT_LANE_TILES
    qr_refs, qi_refs, y_refs, gate_refs = (refs[k * q:(k + 1) * q] for k in range(4))
    f3_ref, fb_ref, o_ref = refs[4 * q:]
    f3 = f3_ref[...]
    for t in range(q):
        lanes = slice(t * LANES, (t + 1) * LANES)
        fb = fb_ref[:, lanes]
        for j in range(DFT_BGROUP):
            planes = jnp.concatenate([_rows_of_position(qr_refs[t], j),
                                      _rows_of_position(qi_refs[t], j)], axis=0).astype(BF16)
            conv = _dot(f3, planes)
            o_ref[:, j, lanes] = _rows_of_position(gate_refs[t], j) * (
                conv + fb * _rows_of_position(y_refs[t], j))


def _idft_gate(qr, qi, f3, y4, gate4, fbias):
    nb, na, r, c = y4.shape
    q, bg = DFT_LANE_TILES, DFT_BGROUP
    tile_map = lambda t: (lambda i, j, cc: (i, 0, j, cc * q + t))
    wide = lambda rows: BS((None, rows, bg, q * LANES), lambda i, j, cc: (i, 0, j, cc))
    return pl.pallas_call(
        _idft_gate_body,
        out_shape=SDS((nb, na, r, c), F32),
        grid=(nb, r // bg, c // (q * LANES)),
        in_specs=(_lane_tile_specs(DFT_VP, tile_map) + _lane_tile_specs(DFT_VP, tile_map)
                  + _lane_tile_specs(na, tile_map) + _lane_tile_specs(na, tile_map)
                  + [BS((na, 2 * DFT_VP), lambda i, j, cc: (0, 0)),
                     BS((1, q * LANES), lambda i, j, cc: (0, cc))]),
        out_specs=wide(na),
        compiler_params=_cparams("arbitrary", "arbitrary", "arbitrary"),
        name="hyena_idft_gate",
    )(*([qr] * q + [qi] * q + [y4] * q + [gate4] * q), f3, fbias)


def _outproj_body(y_ref, w_ref, x_ref, mod_ref, o_ref):
    o_ref[0] = x_ref[0] + mod_ref[0, 2:3, :] * _dot(y_ref[0].astype(BF16), w_ref[...])


def _outproj(y, w_out, x, mod_lat):
    b, l, d = x.shape
    tm = TM_PROJ
    wdt = y.shape[-1]
    return pl.pallas_call(
        _outproj_body,
        out_shape=SDS((b, l, d), F32),
        grid=(b, l // tm),
        in_specs=[BS((1, tm, wdt), lambda bi, i: (bi, i, 0)),
                  BS((wdt, d), lambda bi, i: (0, 0)),
                  BS((1, tm, d), lambda bi, i: (bi, i, 0)),
                  BS((1, SUBLANES, d), lambda bi, i: (bi, 0, 0))],
        out_specs=BS((1, tm, d), lambda bi, i: (bi, i, 0)),
        compiler_params=_cparams("arbitrary", "arbitrary"),
        name="outproj_c",
    )(y, w_out, x, mod_lat)


def _hyena(x, mod_lat, g, w_in, conv_w, conv_b, w1, b1, w2, b2, w3, freq, delta, f_bias, w_out):
    b, n, d = x.shape
    width = w_out.shape[0]
    r = DFT_R
    na = n // r
    f1, fwd, inv, f3 = _dft_tables(n)
    v, gate1, gate2 = _inproj_c(x, mod_lat, g, w_in.astype(BF16), conv_w, conv_b.reshape(1, -1))

    t = jnp.arange(n, dtype=F32)
    t01 = t / max(n - 1, 1)
    bands = jnp.linspace(1e-4, FILT_BANDS - 1, FILT_BANDS, dtype=F32)
    ang = (2.0 * math.pi / n) * t[:, None] * bands[None, :]
    feats = jnp.concatenate([t01[:, None], jnp.cos(ang), jnp.sin(ang)], axis=-1)
    fe = feats.shape[1]
    feats = jnp.pad(feats, ((0, 0), (0, LANES - fe)))
    w1p = jnp.pad(w1, ((0, LANES - fe), (0, 0)))
    hf, l1 = _filters(feats, w1p, b1.reshape(1, -1), w2, b2.reshape(1, -1), w3,
                      freq.reshape(1, -1), delta.reshape(1, -1), width)
    far, fai = _dft_s1(hf.reshape(1, na, r, hf.shape[1]), f1)
    kr, ki = _filter_spectrum(far, fai, fwd, l1, width)

    y4 = v.reshape(b, na, r, width)
    for o, gate in enumerate((gate1, gate2)):
        ar, ai = _dft_s1(y4, f1)
        qr, qi = _conv_mid(ar, ai, fwd, inv, kr, ki, o)
        y4 = _idft_gate(qr, qi, f3, y4, gate.reshape(b, na, r, width), f_bias[o].reshape(1, width))
    return _outproj(y4.reshape(b, n, width), w_out.astype(BF16), x, mod_lat)


def _rope_tables(seq_len):
    rows = seq_len // GRID_W
    row = jnp.repeat(jnp.arange(rows, dtype=F32), GRID_W)
    col = jnp.tile(jnp.arange(GRID_W, dtype=F32), rows)
    inv = jnp.power(ROPE_BASE, -jnp.arange(ROPE_FREQS, dtype=F32) / ROPE_FREQS)
    ar, ac = row[:, None] * inv, col[:, None] * inv
    cos_h = jnp.concatenate([jnp.cos(ar), jnp.cos(ar), jnp.cos(ac), jnp.cos(ac)], axis=1)
    sin_h = jnp.concatenate([-jnp.sin(ar), jnp.sin(ar), -jnp.sin(ac), jnp.sin(ac)], axis=1)
    reps = LANES // HEAD_DIM
    return jnp.tile(cos_h, (1, reps)), jnp.tile(sin_h, (1, reps))


def _rotate_partner_columns(w):
    ncol = w.shape[1]
    lane = np.arange(ncol)
    partner = np.where((lane % (2 * ROPE_FREQS)) < ROPE_FREQS, lane + ROPE_FREQS, lane - ROPE_FREQS)
    return w[:, partner]


def kernel(x, c, ctx, c_ctx, w_mod, b_mod, norm_g, w_in_ab, sink, w_spatial, b_spatial, w_out_ab,
           w_in_c, conv_w, conv_b, filt_w1, filt_b1, filt_w2, filt_b2, filt_w3, filt_freq,
           filt_delta, filt_bias, w_out_c, w_router, e_bias, w_gate, w_up, w_down, ws_gate,
           ws_up, ws_down, final_g):
    b, l, d = x.shape
    depth = w_mod.shape[0]
    assert depth == 2 and b + 1 <= SUBLANES

    cc = jnp.zeros((SUBLANES, d), F32).at[:b].set(c).at[b].set(c_ctx)
    m_all = _mod_vectors(cc, w_mod, b_mod)

    def mod_rows(layer, row0, nrow):
        m = m_all[layer, row0:row0 + nrow].reshape(nrow, 6, d)
        return jnp.pad(m, ((0, 0), (0, SUBLANES - 6), (0, 0)))

    tri = jnp.triu(jnp.ones((MOE_TILE, MOE_TILE), F32), k=1).astype(BF16)

    mod_lat = mod_rows(0, 0, b)
    mod_ctx = mod_rows(0, b, 1)[0]
    w_in = w_in_ab[0]
    qk = ATTN_WIDTH + KV_WIDTH
    w_cat = jnp.concatenate([w_in, _rotate_partner_columns(w_in[:, :qk])], axis=1).astype(BF16)
    cos_t, sin_t = _rope_tables(l)
    group_avg = jnp.kron(jnp.eye(N_SG_GROUPS, dtype=F32),
                         jnp.full((SG_GROUP_DIM, SG_GROUP_DIM), 1.0 / SG_GROUP_DIM, F32)).astype(BF16)
    kc, vc = _ctx_kv(ctx, mod_ctx, norm_g[0, 0].reshape(1, d),
                     w_in[:, ATTN_WIDTH:ATTN_WIDTH + 2 * KV_WIDTH].astype(BF16))
    q, k, v, ug, vn = _inproj_ab(x, mod_lat, norm_g[0, 0].reshape(1, d), w_cat, cos_t, sin_t, group_avg)
    b_full = jnp.repeat(b_spatial[0].T, SG_GROUP_DIM, axis=1)
    x1 = _mixer(sink[0], q, k, v, kc, vc, ug, vn, w_spatial[0].astype(BF16), b_full,
                w_out_ab[0].astype(BF16), x, mod_lat)
    x2 = _moe(x1.reshape(b * l, d), mod_lat, norm_g[0, 1].reshape(1, d), w_router[0], e_bias[0],
              w_gate, w_up, w_down, ws_gate[0], ws_up[0], ws_down[0], final_g, tri, l,
              layer=0, final=False).reshape(b, l, d)

    mod_lat = mod_rows(1, 0, b)
    x3 = _hyena(x2, mod_lat, norm_g[1, 0].reshape(1, d), w_in_c[0], conv_w[0], conv_b[0],
                filt_w1[0], filt_b1[0], filt_w2[0], filt_b2[0], filt_w3[0], filt_freq[0],
                filt_delta[0], filt_bias[0], w_out_c[0])
    out = _moe(x3.reshape(b * l, d), mod_lat, norm_g[1, 1].reshape(1, d), w_router[1], e_bias[1],
               w_gate, w_up, w_down, ws_gate[1], ws_up[1], ws_down[1], final_g, tri, l,
               layer=1, final=True)
    return out.reshape(b, l, d)
```

```python
import functools
import math

import numpy as np
import jax
import jax.numpy as jnp
from jax import lax
from jax.experimental import pallas as pl
from jax.experimental.pallas import tpu as pltpu

F32 = jnp.float32
BF16 = jnp.bfloat16
I32 = jnp.int32
HIGHEST = lax.Precision.HIGHEST
SDS = jax.ShapeDtypeStruct
BS = pl.BlockSpec

EPS = 1e-6
NEG = -1e30

GRID_W = 64
N_Q_HEADS = 8
N_KV_HEADS = 2
HEAD_DIM = 64
ATTN_WIDTH = N_Q_HEADS * HEAD_DIM
KV_WIDTH = N_KV_HEADS * HEAD_DIM
WINDOW = 128
BLOCK = 128
ROPE_BASE = 10000.0
ROPE_FREQS = HEAD_DIM // 4
N_SG_GROUPS = 8
SG_GROUP_DIM = 64
SG_WIDTH = N_SG_GROUPS * SG_GROUP_DIM
HYENA_ORDER = 2
FILT_BANDS = 16
DECAY_SHIFT = 0.05
N_EXPERTS = 64
TOP_K = 8
N_GROUPS = 8
TOPK_GROUPS = 4
ROUTED_SCALE = 2.5

LANES = 128
SUBLANES = 8
VMEM_LIMIT = 56 * 1024 * 1024

TM_PROJ = 512
TQ_MIX = 256
MOE_TILE = 256
BM_FFN = 512
RUN_ALIGN = 16
SLOT_CHUNK = 512
NO_SLOT = 256 * 256 - 1
LOCAL_SLOTS = -(-(TOP_K * MOE_TILE + N_EXPERTS * (RUN_ALIGN - 1)) // SLOT_CHUNK) * SLOT_CHUNK
CHUNK_TABLE = -(-(LOCAL_SLOTS // RUN_ALIGN) // LANES) * LANES
FFN_IN_BUFS = 4


def _cparams(*sem):
    return pltpu.CompilerParams(dimension_semantics=sem, vmem_limit_bytes=VMEM_LIMIT)


def _dot(a, b):
    return jnp.dot(a, b, preferred_element_type=F32)


def _dot_nt(a, b):
    return lax.dot_general(a, b, (((1,), (1,)), ((), ())), preferred_element_type=F32)


def _dot_hp(a, b):
    return jnp.dot(a, b, preferred_element_type=F32, precision=HIGHEST)


def _norm_mod(x, g, sc, sh):
    ms = jnp.mean(x * x, axis=-1, keepdims=True)
    y = x * lax.rsqrt(ms + EPS)
    return (y * g) * (1.0 + sc) + sh


def _gelu_tanh(x):
    c = math.sqrt(2.0 / math.pi)
    return 0.5 * x * (1.0 + jnp.tanh(c * (x + 0.044715 * (x * x * x))))


def _silu(x):
    return x * jax.nn.sigmoid(x)


def _mod_body(c_ref, w_ref, b_ref, o_ref):
    o_ref[0] = _dot_hp(_silu(c_ref[...]), w_ref[0]) + b_ref[0]


def _mod_vectors(cc, w_mod, b_mod):
    depth, d, n = w_mod.shape
    tn = 1536
    return pl.pallas_call(
        _mod_body,
        out_shape=SDS((depth, SUBLANES, n), F32),
        grid=(depth, n // tn),
        in_specs=[BS((SUBLANES, d), lambda l, j: (0, 0)),
                  BS((1, d, tn), lambda l, j: (l, 0, j)),
                  BS((1, 1, tn), lambda l, j: (l, 0, j))],
        out_specs=BS((1, SUBLANES, tn), lambda l, j: (l, 0, j)),
        compiler_params=_cparams("arbitrary", "arbitrary"),
        name="mod_vectors",
    )(cc, w_mod, b_mod.reshape(depth, 1, n))


def _ctx_kv_body(ctx_ref, mod_ref, g_ref, w_ref, kc_ref, vc_ref):
    h = _norm_mod(ctx_ref[0], g_ref[...], mod_ref[1:2, :], mod_ref[0:1, :])
    z = _dot(h.astype(BF16), w_ref[...])
    kc_ref[0] = z[:, :KV_WIDTH].astype(BF16)
    vc_ref[0] = z[:, KV_WIDTH:].astype(BF16)


def _ctx_kv(ctx, mod_ctx, g, w_kv):
    b, c, d = ctx.shape
    return pl.pallas_call(
        _ctx_kv_body,
        out_shape=(SDS((b, c, KV_WIDTH), BF16), SDS((b, c, KV_WIDTH), BF16)),
        grid=(b,),
        in_specs=[BS((1, c, d), lambda i: (i, 0, 0)),
                  BS((SUBLANES, d), lambda i: (0, 0)),
                  BS((1, d), lambda i: (0, 0)),
                  BS((d, 2 * KV_WIDTH), lambda i: (0, 0))],
        out_specs=(BS((1, c, KV_WIDTH), lambda i: (i, 0, 0)),
                   BS((1, c, KV_WIDTH), lambda i: (i, 0, 0))),
        compiler_params=_cparams("arbitrary"),
        name="ctx_kv",
    )(ctx, mod_ctx, g, w_kv)


def _inproj_ab_body(x_ref, mod_ref, g_ref, w_ref, cos_ref, sin_ref, avg_ref,
                    q_ref, k_ref, v_ref, ug_ref, vn_ref):
    h = _norm_mod(x_ref[0], g_ref[...], mod_ref[0, 1:2, :], mod_ref[0, 0:1, :]).astype(BF16)
    cs = cos_ref[...]
    sn = sin_ref[...]
    rot0 = ATTN_WIDTH + 2 * KV_WIDTH + 2 * SG_WIDTH
    scale = HEAD_DIM ** -0.5
    for j in range(ATTN_WIDTH // LANES):
        z = _dot(h, w_ref[:, j * LANES:(j + 1) * LANES])
        zr = _dot(h, w_ref[:, rot0 + j * LANES:rot0 + (j + 1) * LANES])
        q_ref[0, :, j * LANES:(j + 1) * LANES] = ((z * cs + zr * sn) * scale).astype(BF16)
    zk = _dot(h, w_ref[:, ATTN_WIDTH:ATTN_WIDTH + KV_WIDTH])
    zkr = _dot(h, w_ref[:, rot0 + ATTN_WIDTH:rot0 + ATTN_WIDTH + KV_WIDTH])
    k_ref[0] = (zk * cs + zkr * sn).astype(BF16)
    v_ref[0] = _dot(h, w_ref[:, ATTN_WIDTH + KV_WIDTH:ATTN_WIDTH + 2 * KV_WIDTH]).astype(BF16)
    u0 = ATTN_WIDTH + 2 * KV_WIDTH
    ug_ref[0] = _gelu_tanh(_dot(h, w_ref[:, u0:u0 + SG_WIDTH]))
    vf = _gelu_tanh(_dot(h, w_ref[:, u0 + SG_WIDTH:u0 + 2 * SG_WIDTH]))
    avg = avg_ref[...]

    def gmean(t):
        hi = t.astype(BF16)
        lo = (t - hi.astype(F32)).astype(BF16)
        return _dot(hi, avg) + _dot(lo, avg)

    vc = vf - gmean(vf)
    vn_ref[0] = (vc * lax.rsqrt(gmean(vc * vc) + EPS)).astype(BF16)


def _inproj_ab(x, mod_lat, g, w_cat, cos_t, sin_t, avg):
    b, l, d = x.shape
    tm = TM_PROJ
    ncol = w_cat.shape[1]
    return pl.pallas_call(
        _inproj_ab_body,
        out_shape=(SDS((b, l, ATTN_WIDTH), BF16), SDS((b, l, KV_WIDTH), BF16),
                   SDS((b, l, KV_WIDTH), BF16), SDS((b, l, SG_WIDTH), F32),
                   SDS((b, l, SG_WIDTH), BF16)),
        grid=(b, l // tm),
        in_specs=[BS((1, tm, d), lambda bi, i: (bi, i, 0)),
                  BS((1, SUBLANES, d), lambda bi, i: (bi, 0, 0)),
                  BS((1, d), lambda bi, i: (0, 0)),
                  BS((d, ncol), lambda bi, i: (0, 0)),
                  BS((tm, LANES), lambda bi, i: (i, 0)),
                  BS((tm, LANES), lambda bi, i: (i, 0)),
                  BS((SG_WIDTH, SG_WIDTH), lambda bi, i: (0, 0))],
        out_specs=(BS((1, tm, ATTN_WIDTH), lambda bi, i: (bi, i, 0)),
                   BS((1, tm, KV_WIDTH), lambda bi, i: (bi, i, 0)),
                   BS((1, tm, KV_WIDTH), lambda bi, i: (bi, i, 0)),
                   BS((1, tm, SG_WIDTH), lambda bi, i: (bi, i, 0)),
                   BS((1, tm, SG_WIDTH), lambda bi, i: (bi, i, 0))),
        compiler_params=_cparams("arbitrary", "arbitrary"),
        name="inproj_ab",
    )(x, mod_lat, g, w_cat, cos_t, sin_t, avg)


def _mixer_body(sink_ref, q_ref, kp_ref, kcur_ref, kn_ref, vp_ref, vcur_ref, vn_ref,
                kc_ref, vc_ref, ug_ref, vnorm_ref, ws_ref, bs_ref, wout_ref, x_ref, mod_ref,
                o_ref, cat_ref, *, seq_len, sub_blocks):
    i = pl.program_id(1)
    kk = jnp.concatenate([kp_ref[0], kcur_ref[0], kn_ref[0]], axis=0)
    vv = jnp.concatenate([vp_ref[0], vcur_ref[0], vn_ref[0]], axis=0)
    kc = kc_ref[0]
    vc = vc_ref[0]
    span = 3 * BLOCK
    ii = lax.broadcasted_iota(I32, (BLOCK, span), 0)
    jj = lax.broadcasted_iota(I32, (BLOCK, span), 1)
    dd = jj - ii
    in_window = jnp.where(dd >= 0, jnp.where(dd <= 2 * WINDOW, 1, 0), 0)
    group = N_Q_HEADS // N_KV_HEADS
    for r in range(sub_blocks):
        rows = slice(r * BLOCK, (r + 1) * BLOCK)
        kpos = (i * sub_blocks + r - 1) * BLOCK + jj
        in_seq = jnp.where(kpos >= 0, jnp.where(kpos < seq_len, 1, 0), 0)
        bias = jnp.where(in_window * in_seq > 0, 0.0, NEG)
        qb = q_ref[0, rows, :]
        kl = kk[r * BLOCK:r * BLOCK + span]
        vl = vv[r * BLOCK:r * BLOCK + span]
        for hq in range(N_Q_HEADS):
            hk = hq // group
            ks = slice(hk * HEAD_DIM, (hk + 1) * HEAD_DIM)
            qh = qb[:, hq * HEAD_DIM:(hq + 1) * HEAD_DIM]
            s_loc = _dot_nt(qh, kl[:, ks]) + bias
            s_ctx = _dot_nt(qh, kc[:, ks])
            sk = sink_ref[hq]
            m = jnp.maximum(jnp.maximum(jnp.max(s_loc, axis=-1, keepdims=True),
                                        jnp.max(s_ctx, axis=-1, keepdims=True)), sk)
            p_loc = jnp.exp(s_loc - m)
            p_ctx = jnp.exp(s_ctx - m)
            den = (jnp.sum(p_loc, axis=-1, keepdims=True) + jnp.sum(p_ctx, axis=-1, keepdims=True)
                   + jnp.exp(sk - m))
            o = _dot(p_loc.astype(BF16), vl[:, ks]) + _dot(p_ctx.astype(BF16), vc[:, ks])
            cat_ref[rows, hq * HEAD_DIM:(hq + 1) * HEAD_DIM] = (o / den).astype(BF16)
        vnb = vnorm_ref[0, rows, :]
        ugb = ug_ref[0, rows, :]
        for g in range(N_SG_GROUPS):
            gs = slice(g * SG_GROUP_DIM, (g + 1) * SG_GROUP_DIM)
            sg = _dot(ws_ref[g], vnb[:, gs]) + bs_ref[:, gs]
            cat_ref[rows, ATTN_WIDTH + g * SG_GROUP_DIM:ATTN_WIDTH + (g + 1) * SG_GROUP_DIM] = (
                ugb[:, gs] * sg).astype(BF16)
    y = _dot(cat_ref[...], wout_ref[...])
    o_ref[0] = x_ref[0] + mod_ref[0, 2:3, :] * y


def _mixer(sink, q, k, v, kc, vc, ug, vn, w_s, b_full, w_out, x, mod_lat):
    b, l, d = x.shape
    tq = TQ_MIX
    r = tq // BLOCK
    nb = l // BLOCK
    c = kc.shape[1]
    prev_map = lambda bi, i: (bi, jnp.maximum(i * r - 1, 0), 0)
    next_map = lambda bi, i: (bi, jnp.minimum((i + 1) * r, nb - 1), 0)
    cur_map = lambda bi, i: (bi, i, 0)
    body = functools.partial(_mixer_body, seq_len=l, sub_blocks=r)
    return pl.pallas_call(
        body,
        out_shape=SDS((b, l, d), F32),
        grid=(b, l // tq),
        in_specs=[BS(memory_space=pltpu.SMEM),
                  BS((1, tq, ATTN_WIDTH), cur_map),
                  BS((1, BLOCK, KV_WIDTH), prev_map), BS((1, tq, KV_WIDTH), cur_map),
                  BS((1, BLOCK, KV_WIDTH), next_map),
                  BS((1, BLOCK, KV_WIDTH), prev_map), BS((1, tq, KV_WIDTH), cur_map),
                  BS((1, BLOCK, KV_WIDTH), next_map),
                  BS((1, c, KV_WIDTH), lambda bi, i: (bi, 0, 0)),
                  BS((1, c, KV_WIDTH), lambda bi, i: (bi, 0, 0)),
                  BS((1, tq, SG_WIDTH), cur_map), BS((1, tq, SG_WIDTH), cur_map),
                  BS((N_SG_GROUPS, BLOCK, BLOCK), lambda bi, i: (0, 0, 0)),
                  BS((BLOCK, SG_WIDTH), lambda bi, i: (0, 0)),
                  BS((d, d), lambda bi, i: (0, 0)),
                  BS((1, tq, d), cur_map),
                  BS((1, SUBLANES, d), lambda bi, i: (bi, 0, 0))],
        out_specs=BS((1, tq, d), cur_map),
        scratch_shapes=[pltpu.VMEM((tq, d), BF16)],
        compiler_params=_cparams("arbitrary", "arbitrary"),
        name="mixer_ab",
    )(sink, q, k, k, k, v, v, v, kc, vc, ug, vn, w_s, b_full, w_out, x, mod_lat)


def _router_body(x_ref, mod_ref, g_ref, wr_ref, eb_ref, tri_ref,
                 h2b_ref, slot_hi_ref, slot_lo_ref, gate_hi_ref, gate_lo_ref, lo_col_ref, hi_col_ref,
                 bounds_row_ref, chunk_e_ref, chunk_rel_ref, rows_ref, cnt_ref, carry_ref, *, tm):
    i = pl.program_id(0)

    @pl.when(i == 0)
    def _():
        carry_ref[...] = jnp.zeros_like(carry_ref)

    h2 = _norm_mod(x_ref[...], g_ref[...], mod_ref[0, 4:5, :], mod_ref[0, 3:4, :])
    h2b_ref[...] = h2.astype(BF16)

    logits = lax.dot_general(wr_ref[...], h2, (((1,), (1,)), ((), ())),
                             preferred_element_type=F32, precision=HIGHEST)
    scores = jax.nn.sigmoid(logits)
    per_group = N_EXPERTS // N_GROUPS
    shape3 = (N_GROUPS, per_group, tm)
    s3 = scores.reshape(shape3)
    b3 = (scores + eb_ref[...]).reshape(shape3)
    sub = lax.broadcasted_iota(I32, shape3, 1)
    eid = lax.broadcasted_iota(I32, shape3, 0) * per_group + sub

    m1 = jnp.max(b3, axis=1, keepdims=True)
    i1 = jnp.min(jnp.where(b3 == m1, sub, per_group), axis=1, keepdims=True)
    m2 = jnp.max(jnp.where(sub == i1, -jnp.inf, b3), axis=1, keepdims=True)
    gs = m1 + m2
    keep = []
    for g in range(N_GROUPS):
        beaten = jnp.zeros((1, tm), I32)
        for g2 in range(N_GROUPS):
            if g2 == g:
                continue
            wins = (gs[g2] >= gs[g]) if g2 < g else (gs[g2] > gs[g])
            beaten = beaten + jnp.where(wins, 1, 0)
        keep.append(jnp.where(beaten < TOPK_GROUPS, 1, 0)[None])
    keep3 = jnp.concatenate(keep, axis=0)
    val = jnp.where(keep3 > 0, b3, -jnp.inf)

    def red(fn, a):
        return fn(fn(a, axis=0, keepdims=True), axis=1, keepdims=True)

    idxs, ws = [], []
    member = jnp.zeros(shape3, F32)
    for _ in range(TOP_K):
        m = red(jnp.max, val)
        idx = red(jnp.min, jnp.where(val == m, eid, N_EXPERTS))
        hit = eid == idx
        ws.append(red(jnp.sum, jnp.where(hit, s3, 0.0)))
        val = jnp.where(hit, -jnp.inf, val)
        member = member + jnp.where(hit, 1.0, 0.0)
        idxs.append(idx)
    wsum = ws[0]
    for w in ws[1:]:
        wsum = wsum + w

    member2 = member.reshape(N_EXPERTS, tm)
    cnt = jnp.sum(member2, axis=1, keepdims=True)
    runlen = jnp.floor((cnt + (RUN_ALIGN - 1)) * (1.0 / RUN_ALIGN)) * RUN_ALIGN
    runlen_b = jnp.broadcast_to(runlen, (N_EXPERTS, LANES))
    e_row = lax.broadcasted_iota(I32, (N_EXPERTS, N_EXPERTS), 0)
    e_col = lax.broadcasted_iota(I32, (N_EXPERTS, N_EXPERTS), 1)
    earlier = jnp.where(e_col < e_row, 1.0, 0.0).astype(BF16)
    loff = _dot(earlier, runlen_b.astype(BF16))
    slot = _dot(member2.astype(BF16), tri_ref[...]) + loff[:, 0:1]
    slot = jnp.where(member2 > 0.0, slot, float(NO_SLOT))
    slot_hi = jnp.floor(slot * (1.0 / 256.0))
    slot_hi_ref[0] = slot_hi.astype(BF16)
    slot_lo_ref[0] = (slot - 256.0 * slot_hi).astype(BF16)
    gate3 = jnp.zeros(shape3, F32)
    for k in range(TOP_K):
        gate3 = jnp.where(eid == idxs[k], ws[k] / wsum * ROUTED_SCALE, gate3)
    gate = gate3.reshape(N_EXPERTS, tm)
    gate_hi = gate.astype(BF16)
    gate_hi_ref[0] = gate_hi
    gate_lo_ref[0] = (gate - gate_hi.astype(F32)).astype(BF16)
    run_lo = loff
    run_hi = loff + runlen_b
    lo_col_ref[0] = run_lo
    hi_col_ref[0] = run_hi
    diag = (lax.broadcasted_iota(I32, (N_EXPERTS, LANES), 0)
            == lax.broadcasted_iota(I32, (N_EXPERTS, LANES), 1))
    bounds_row_ref[0] = jnp.concatenate(
        [jnp.sum(jnp.where(diag, run_lo, 0.0), axis=0, keepdims=True),
         jnp.sum(jnp.where(diag, run_hi, 0.0), axis=0, keepdims=True),
         jnp.zeros((SUBLANES - 2, LANES), F32)], axis=0)
    chunk_hi = (loff[:, 0:1] + runlen) * (1.0 / RUN_ALIGN)
    j = lax.broadcasted_iota(I32, (N_EXPERTS, CHUNK_TABLE), 1).astype(F32)
    e_of_chunk = jnp.sum(jnp.where(chunk_hi <= j, 1.0, 0.0), axis=0, keepdims=True)
    e_iota = lax.broadcasted_iota(I32, (N_EXPERTS, CHUNK_TABLE), 0).astype(F32)
    seg_off = carry_ref[:, 0:1] - loff[:, 0:1]
    rel = jnp.sum(jnp.where(e_iota == e_of_chunk, seg_off, 0.0), axis=0, keepdims=True)
    chunk_e_ref[0] = jnp.minimum(e_of_chunk, N_EXPERTS - 1.0).astype(I32)
    chunk_rel_ref[0] = (rel + j[0:1, :] * RUN_ALIGN).astype(I32)
    rows_ref[0] = (loff[N_EXPERTS - 1:N_EXPERTS, :] + runlen_b[N_EXPERTS - 1:N_EXPERTS, :]).astype(I32)
    total = carry_ref[...] + runlen_b
    carry_ref[...] = total
    cnt_ref[...] = total


def _router(x_flat, mod_lat, g, wr_t, e_bias, tri, tokens_per_batch):
    t, d = x_flat.shape
    tm = MOE_TILE
    n_tiles = t // tm
    tiles_per_batch = tokens_per_batch // tm
    body = functools.partial(_router_body, tm=tm)
    table = SDS((n_tiles, 1, CHUNK_TABLE), I32)
    table_spec = BS((1, 1, CHUNK_TABLE), lambda i: (i, 0, 0))
    per_expert_token = SDS((n_tiles, N_EXPERTS, tm), BF16)
    per_expert_token_spec = BS((1, N_EXPERTS, tm), lambda i: (i, 0, 0))
    per_expert = SDS((n_tiles, N_EXPERTS, LANES), F32)
    per_expert_spec = BS((1, N_EXPERTS, LANES), lambda i: (i, 0, 0))
    return pl.pallas_call(
        body,
        out_shape=(SDS((t, d), BF16),
                   per_expert_token, per_expert_token, per_expert_token, per_expert_token,
                   per_expert, per_expert, SDS((n_tiles, SUBLANES, LANES), F32),
                   table, table, SDS((n_tiles, 1, LANES), I32), SDS((N_EXPERTS, LANES), F32)),
        grid=(n_tiles,),
        in_specs=[BS((tm, d), lambda i: (i, 0)),
                  BS((1, SUBLANES, d), lambda i: (i // tiles_per_batch, 0, 0)),
                  BS((1, d), lambda i: (0, 0)),
                  BS((N_EXPERTS, d), lambda i: (0, 0)),
                  BS((N_EXPERTS, 1), lambda i: (0, 0)),
                  BS((tm, tm), lambda i: (0, 0))],
        out_specs=(BS((tm, d), lambda i: (i, 0)),
                   per_expert_token_spec, per_expert_token_spec, per_expert_token_spec,
                   per_expert_token_spec, per_expert_spec, per_expert_spec,
                   BS((1, SUBLANES, LANES), lambda i: (i, 0, 0)),
                   table_spec, table_spec, BS((1, 1, LANES), lambda i: (i, 0, 0)),
                   BS((N_EXPERTS, LANES), lambda i: (0, 0))),
        scratch_shapes=[pltpu.VMEM((N_EXPERTS, LANES), F32)],
        compiler_params=_cparams("arbitrary"),
        name="moe_router",
    )(x_flat, mod_lat, g, wr_t, e_bias, tri)


def _start_run_chunks(pstart_ref, chunk_e_ref, chunk_rel_ref, rows, chunk_copy):
    def body(j, carry):
        slot0 = pstart_ref[chunk_e_ref[0, 0, j]] + chunk_rel_ref[0, 0, j]
        chunk_copy(pl.multiple_of(j * RUN_ALIGN, RUN_ALIGN),
                   pl.multiple_of(slot0, RUN_ALIGN)).start()
        return carry

    lax.fori_loop(0, rows // RUN_ALIGN, body, 0)


def _wait_run_rows(copy_of_rows, rows):
    def wait_n(nrows):
        def body(_, carry):
            copy_of_rows(nrows).wait()
            return carry
        return body

    lax.fori_loop(0, rows // SLOT_CHUNK, wait_n(SLOT_CHUNK), 0)
    lax.fori_loop(0, (rows % SLOT_CHUNK) // RUN_ALIGN, wait_n(RUN_ALIGN), 0)


def _dispatch_body(pend_ref, pcnt_ref, nu_ref, pstart_ref, rows_ref, chunk_e_ref, chunk_rel_ref,
                   slot_hi_ref, slot_lo_ref, bounds_ref, h_ref, xs_ref, loc_ref, zbuf_ref, sem, zsem,
                   *, tl, bm, n_blocks):
    i = pl.program_id(0)
    last = pl.num_programs(0) - 1
    buf = i % 2

    def zero_copy(row0):
        return pltpu.make_async_copy(
            zbuf_ref, xs_ref.at[pl.ds(pl.multiple_of(row0, RUN_ALIGN), bm), :], zsem)

    @pl.when(i == 0)
    def _():
        zbuf_ref[...] = jnp.zeros_like(zbuf_ref)

        def start(e, c):
            @pl.when(pcnt_ref[e] > 0)
            def _():
                zero_copy(pend_ref[e] - bm).start()
            return c

        def wait(e, c):
            @pl.when(pcnt_ref[e] > 0)
            def _():
                zero_copy(pend_ref[e] - bm).wait()
            return c

        def start_tail(j, c):
            zero_copy(j * bm).start()
            return c

        def wait_tail(j, c):
            zero_copy(j * bm).wait()
            return c

        lax.fori_loop(0, N_EXPERTS, start, 0)
        lax.fori_loop(nu_ref[0], n_blocks, start_tail, 0)
        lax.fori_loop(0, N_EXPERTS, wait, 0)
        lax.fori_loop(nu_ref[0], n_blocks, wait_tail, 0)

    rows_used = rows_ref[i]
    h = h_ref[...]
    run_lo = bounds_ref[0, 0:1, :]
    run_hi = bounds_ref[0, 1:2, :]

    def sort_chunk(c, carry):
        row0 = (c * SLOT_CHUNK).astype(F32)
        row_e = lax.broadcasted_iota(I32, (SLOT_CHUNK, LANES), 0).astype(F32) + row0
        in_run = jnp.where(row_e >= run_lo, jnp.where(row_e < run_hi, 1.0, 0.0), 0.0).astype(BF16)
        slot_of_token = (256.0 * _dot(in_run, slot_hi_ref[0]) + _dot(in_run, slot_lo_ref[0]))
        row_t = lax.broadcasted_iota(I32, (SLOT_CHUNK, tl), 0).astype(F32) + row0
        onehot = jnp.where(slot_of_token == row_t, 1.0, 0.0).astype(BF16)
        rows = pl.ds(pl.multiple_of(c * SLOT_CHUNK, SLOT_CHUNK), SLOT_CHUNK)
        loc_ref[buf, rows, :] = _dot(onehot, h).astype(BF16)
        return carry

    lax.fori_loop(0, (rows_used + SLOT_CHUNK - 1) // SLOT_CHUNK, sort_chunk, 0)

    def run_copy(b, loc0, slot0, nrows=RUN_ALIGN):
        return pltpu.make_async_copy(loc_ref.at[b, pl.ds(loc0, nrows), :],
                                     xs_ref.at[pl.ds(slot0, nrows), :], sem.at[b])

    _start_run_chunks(pstart_ref, chunk_e_ref, chunk_rel_ref, rows_used,
                      lambda loc0, slot0: run_copy(buf, loc0, slot0))

    @pl.when(i > 0)
    def _():
        _wait_run_rows(lambda n: run_copy(1 - buf, 0, 0, n), rows_ref[jnp.maximum(i - 1, 0)])

    @pl.when(i == last)
    def _():
        _wait_run_rows(lambda n: run_copy(buf, 0, 0, n), rows_used)


def _dispatch(pend, pcnt, n_used, pstart, rows, chunk_e, chunk_rel, slot_hi, slot_lo, bounds_row,
              h2b, n_slots):
    t, d = h2b.shape
    tl = MOE_TILE
    body = functools.partial(_dispatch_body, tl=tl, bm=BM_FFN, n_blocks=n_slots // BM_FFN)
    table_spec = lambda index: BS((1, 1, CHUNK_TABLE), index, memory_space=pltpu.SMEM)
    grid_spec = pltpu.PrefetchScalarGridSpec(
        num_scalar_prefetch=5,
        grid=(t // tl,),
        in_specs=[table_spec(lambda i, *_: (i, 0, 0)),
                  table_spec(lambda i, *_: (i, 0, 0)),
                  BS((1, LANES, tl), lambda i, *_: (i, 0, 0)),
                  BS((1, LANES, tl), lambda i, *_: (i, 0, 0)),
                  BS((1, SUBLANES, LANES), lambda i, *_: (i, 0, 0)),
                  BS((tl, d), lambda i, *_: (i, 0))],
        out_specs=BS(memory_space=pl.ANY),
        scratch_shapes=[pltpu.VMEM((2, LOCAL_SLOTS, d), BF16), pltpu.VMEM((BM_FFN, d), BF16),
                        pltpu.SemaphoreType.DMA((2,)), pltpu.SemaphoreType.DMA(())],
    )
    return pl.pallas_call(
        body,
        out_shape=SDS((n_slots, d), BF16),
        grid_spec=grid_spec,
        compiler_params=_cparams("arbitrary"),
        name="moe_dispatch",
    )(pend, pcnt, n_used, pstart, rows, chunk_e, chunk_rel, slot_hi, slot_lo, bounds_row, h2b)


def _ffn_body(first_ref, count_ref, nu_ref, xs_ref, wg_ref, wu_ref, wd_ref, ys_ref, wgb_ref, wub_ref,
              wdb_ref, xbuf_ref, ybuf_ref, isem, osem, *, bm, n_blocks):
    e = pl.program_id(0)
    nu = nu_ref[0]

    def block_rows(blk):
        return pl.ds(pl.multiple_of(blk * bm, bm), bm)

    def in_copy(blk, slot):
        return pltpu.make_async_copy(xs_ref.at[block_rows(blk), :], xbuf_ref.at[slot], isem.at[slot])

    def out_copy(blk, slot):
        return pltpu.make_async_copy(ybuf_ref.at[slot], ys_ref.at[block_rows(blk), :], osem.at[slot])

    @pl.when(e == 0)
    def _():
        for s in range(FFN_IN_BUFS):
            @pl.when(s < nu)
            def _():
                in_copy(s, s).start()

    @pl.when(count_ref[e] > 0)
    def _():
        wgb_ref[...] = wg_ref[...].astype(BF16)
        wub_ref[...] = wu_ref[...].astype(BF16)
        wdb_ref[...] = wd_ref[...].astype(BF16)

    def one_block(b, carry):
        i = first_ref[e] + b
        slot = i % FFN_IN_BUFS
        oslot = i % 2
        in_copy(i, slot).wait()
        x = xbuf_ref[slot]
        a = _silu(_dot(x, wgb_ref[...])) * _dot(x, wub_ref[...])
        y = _dot(a.astype(BF16), wdb_ref[...]).astype(BF16)

        @pl.when(i >= 2)
        def _():
            out_copy(i - 2, oslot).wait()

        ybuf_ref[oslot] = y
        out_copy(i, oslot).start()

        @pl.when(i + FFN_IN_BUFS < nu)
        def _():
            in_copy(i + FFN_IN_BUFS, slot).start()

        return carry

    lax.fori_loop(0, count_ref[e], one_block, 0)

    @pl.when(e == pl.num_programs(0) - 1)
    def _():
        @pl.when(nu >= 2)
        def _():
            out_copy(nu - 2, (nu - 2) % 2).wait()

        out_copy(nu - 1, (nu - 1) % 2).wait()
        ybuf_ref[0] = jnp.zeros((bm, ybuf_ref.shape[-1]), BF16)

        def start_tail(j, c):
            out_copy(j, 0).start()
            return c

        def wait_tail(j, c):
            out_copy(j, 0).wait()
            return c

        lax.fori_loop(nu, n_blocks, start_tail, 0)
        lax.fori_loop(nu, n_blocks, wait_tail, 0)


def _ffn(first_block, block_count, n_used, xs, w_gate, w_up, w_down, layer):
    n_slots, d = xs.shape
    bm = BM_FFN
    de = w_gate.shape[-1]
    n_blocks = n_slots // bm
    body = functools.partial(_ffn_body, bm=bm, n_blocks=n_blocks)
    grid_spec = pltpu.PrefetchScalarGridSpec(
        num_scalar_prefetch=3,
        grid=(N_EXPERTS,),
        in_specs=[BS(memory_space=pl.ANY),
                  BS((None, None, d, de), lambda e, *_: (layer, e, 0, 0)),
                  BS((None, None, d, de), lambda e, *_: (layer, e, 0, 0)),
                  BS((None, None, de, d), lambda e, *_: (layer, e, 0, 0))],
        out_specs=BS(memory_space=pl.ANY),
        scratch_shapes=[pltpu.VMEM((d, de), BF16), pltpu.VMEM((d, de), BF16),
                        pltpu.VMEM((de, d), BF16),
                        pltpu.VMEM((FFN_IN_BUFS, bm, d), BF16), pltpu.VMEM((2, bm, d), BF16),
                        pltpu.SemaphoreType.DMA((FFN_IN_BUFS,)), pltpu.SemaphoreType.DMA((2,))],
    )
    return pl.pallas_call(
        body,
        out_shape=SDS((n_slots, d), BF16),
        grid_spec=grid_spec,
        compiler_params=_cparams("arbitrary"),
        name="moe_experts",
    )(first_block, block_count, n_used, xs, w_gate, w_up, w_down)


def _combine_body(pstart_ref, rows_ref, chunk_e_ref, chunk_rel_ref, next_e_ref, next_rel_ref, ys_ref,
                  slot_hi_ref, slot_lo_ref, gate_hi_ref, gate_lo_ref, lo_col_ref, hi_col_ref,
                  x_ref, h2b_ref, mod_ref, wsg_ref, wsu_ref, wsd_ref, fg_ref,
                  o_ref, loc_ref, acc_ref, sem, *, tl, final):
    i = pl.program_id(0)
    last = pl.num_programs(0) - 1
    buf = i % 2
    rows_used = rows_ref[i]

    def run_copy(b, loc0, slot0, nrows=RUN_ALIGN):
        return pltpu.make_async_copy(ys_ref.at[pl.ds(slot0, nrows), :],
                                     loc_ref.at[b, pl.ds(loc0, nrows), :], sem.at[b])

    @pl.when(i == 0)
    def _():
        loc_ref[...] = jnp.zeros_like(loc_ref)
        _start_run_chunks(pstart_ref, chunk_e_ref, chunk_rel_ref, rows_used,
                          lambda loc0, slot0: run_copy(buf, loc0, slot0))

    @pl.when(i < last)
    def _():
        _start_run_chunks(pstart_ref, next_e_ref, next_rel_ref, rows_ref[jnp.minimum(i + 1, last)],
                          lambda loc0, slot0: run_copy(1 - buf, loc0, slot0))

    hb = h2b_ref[...]
    a = _silu(_dot(hb, wsg_ref[...])) * _dot(hb, wsu_ref[...])
    acc_ref[...] = _dot(a.astype(BF16), wsd_ref[...])
    run_lo = lo_col_ref[0][:, 0:1]
    run_hi = hi_col_ref[0][:, 0:1]

    _wait_run_rows(lambda n: run_copy(buf, 0, 0, n), rows_used)

    def unsort_chunk(c, carry):
        col0 = (c * SLOT_CHUNK).astype(F32)
        col_e = lax.broadcasted_iota(I32, (LANES, SLOT_CHUNK), 1).astype(F32) + col0
        in_run = jnp.where(col_e >= run_lo, jnp.where(col_e < run_hi, 1.0, 0.0), 0.0).astype(BF16)
        slot_of_token = (256.0 * _dot(slot_hi_ref[0], in_run) + _dot(slot_lo_ref[0], in_run))
        gate_of_token = _dot(gate_hi_ref[0], in_run) + _dot(gate_lo_ref[0], in_run)
        col_t = lax.broadcasted_iota(I32, (tl, SLOT_CHUNK), 1).astype(F32) + col0
        gate = jnp.where(slot_of_token == col_t, gate_of_token, 0.0).astype(BF16)
        y = loc_ref[buf, pl.ds(pl.multiple_of(c * SLOT_CHUNK, SLOT_CHUNK), SLOT_CHUNK), :]
        acc_ref[...] = acc_ref[...] + _dot(gate, y)
        return carry

    lax.fori_loop(0, (rows_used + SLOT_CHUNK - 1) // SLOT_CHUNK, unsort_chunk, 0)

    xo = x_ref[...] + mod_ref[0, 5:6, :] * acc_ref[...]
    if final:
        ms = jnp.mean(xo * xo, axis=-1, keepdims=True)
        xo = (xo * lax.rsqrt(ms + EPS)) * fg_ref[...]
    o_ref[...] = xo


def _combine(pstart, rows, chunk_e, chunk_rel, ys, slot_hi_t, slot_lo_t, gate_hi_t, gate_lo_t, lo_col,
             hi_col, x_flat, h2b, mod_lat, wsg, wsu, wsd, final_g, tokens_per_batch, final):
    t, d = x_flat.shape
    tl = MOE_TILE
    ds = wsg.shape[1]
    tiles_per_batch = tokens_per_batch // tl
    body = functools.partial(_combine_body, tl=tl, final=final)
    n_tiles = t // tl
    table_spec = lambda index: BS((1, 1, CHUNK_TABLE), index, memory_space=pltpu.SMEM)
    this_tile = lambda i, *_: (i, 0, 0)
    next_tile = lambda i, *_: (jnp.minimum(i + 1, n_tiles - 1), 0, 0)
    grid_spec = pltpu.PrefetchScalarGridSpec(
        num_scalar_prefetch=2,
        grid=(n_tiles,),
        in_specs=[table_spec(this_tile), table_spec(this_tile),
                  table_spec(next_tile), table_spec(next_tile),
                  BS(memory_space=pl.ANY),
                  BS((1, tl, LANES), this_tile), BS((1, tl, LANES), this_tile),
                  BS((1, tl, LANES), this_tile), BS((1, tl, LANES), this_tile),
                  BS((1, LANES, LANES), this_tile), BS((1, LANES, LANES), this_tile),
                  BS((tl, d), lambda i, *_: (i, 0)),
                  BS((tl, d), lambda i, *_: (i, 0)),
                  BS((1, SUBLANES, d), lambda i, *_: (i // tiles_per_batch, 0, 0)),
                  BS((d, ds), lambda i, *_: (0, 0)),
                  BS((d, ds), lambda i, *_: (0, 0)),
                  BS((ds, d), lambda i, *_: (0, 0)),
                  BS((1, d), lambda i, *_: (0, 0))],
        out_specs=BS((tl, d), lambda i, *_: (i, 0)),
        scratch_shapes=[pltpu.VMEM((2, LOCAL_SLOTS, d), BF16), pltpu.VMEM((tl, d), F32),
                        pltpu.SemaphoreType.DMA((2,))],
    )
    return pl.pallas_call(
        body,
        out_shape=SDS((t, d), F32),
        grid_spec=grid_spec,
        compiler_params=_cparams("arbitrary"),
        name="moe_combine",
    )(pstart, rows, chunk_e, chunk_rel, chunk_e, chunk_rel, ys, slot_hi_t, slot_lo_t, gate_hi_t,
      gate_lo_t, lo_col, hi_col, x_flat, h2b, mod_lat, wsg, wsu, wsd, final_g)


def _moe(x_flat, mod_lat, g, w_router, e_bias, w_gate, w_up, w_down, ws_gate, ws_up, ws_down,
         final_g, tri, tokens_per_batch, layer, final):
    t, d = x_flat.shape
    bm = BM_FFN
    n_tiles = t // MOE_TILE
    (h2b, slot_hi, slot_lo, gate_hi, gate_lo, lo_col, hi_col, bounds_row, chunk_e, chunk_rel, rows,
     seg) = _router(x_flat, mod_lat, g, w_router.T, e_bias.reshape(N_EXPERTS, 1), tri,
                    tokens_per_batch)
    pad_e = lambda a: jnp.pad(a, ((0, 0), (0, LANES - N_EXPERTS), (0, 0)))
    pad_e_t = lambda a: jnp.pad(jnp.swapaxes(a, 1, 2), ((0, 0), (0, 0), (0, LANES - N_EXPERTS)))
    rows = rows[:, 0, 0]
    seg_rows = seg[:, 0].astype(I32)
    pcnt = (seg_rows + bm - 1) // bm * bm
    pend = jnp.cumsum(pcnt).astype(I32)
    pstart = pend - pcnt
    max_rows = t * TOP_K + n_tiles * N_EXPERTS * (RUN_ALIGN - 1)
    n_blocks = -(-max_rows // bm) + N_EXPERTS
    n_slots = n_blocks * bm
    n_used = pend[-1:] // bm
    xs = _dispatch(pend, pcnt, n_used, pstart, rows, chunk_e, chunk_rel, pad_e(slot_hi),
                   pad_e(slot_lo), bounds_row, h2b, n_slots)
    ys = _ffn(pstart // bm, pcnt // bm, n_used, xs, w_gate, w_up, w_down, layer)
    return _combine(pstart, rows, chunk_e, chunk_rel, ys, pad_e_t(slot_hi), pad_e_t(slot_lo),
                    pad_e_t(gate_hi), pad_e_t(gate_lo), pad_e(lo_col), pad_e(hi_col), x_flat, h2b,
                    mod_lat,
                    ws_gate.astype(BF16), ws_up.astype(BF16), ws_down.astype(BF16),
                    final_g.reshape(1, d), tokens_per_batch, final)


def _inproj_c_body(xp_ref, x_ref, xn_ref, mod_ref, g_ref, w_ref, cw_ref, cb_ref,
                   v_ref, g1_ref, g2_ref, *, tm, n_tiles):
    i = pl.program_id(1)
    halo = SUBLANES
    xe = jnp.concatenate([xp_ref[0], x_ref[0], xn_ref[0]], axis=0)
    h = _norm_mod(xe, g_ref[...], mod_ref[0, 1:2, :], mod_ref[0, 0:1, :])
    row = lax.broadcasted_iota(I32, (tm + 2 * halo, 1), 0)
    outside = jnp.logical_or(jnp.logical_and(i == 0, row < halo),
                             jnp.logical_and(i == n_tiles - 1, row >= tm + halo))
    hb = jnp.where(outside, 0.0, h).astype(BF16)
    width = v_ref.shape[-1]
    for part, o_ref in enumerate((v_ref, g1_ref, g2_ref)):
        cols = slice(part * width, (part + 1) * width)
        zp = _dot(hb, w_ref[:, cols])
        up = pltpu.roll(zp, 1, 0)
        dn = pltpu.roll(zp, tm + 2 * halo - 1, 0)
        z = cw_ref[0:1, cols] * up + cw_ref[1:2, cols] * zp + cw_ref[2:3, cols] * dn + cb_ref[:, cols]
        o_ref[0] = z[halo:halo + tm]


def _inproj_c(x, mod_lat, g, w_in, conv_w, conv_b):
    b, l, d = x.shape
    tm = TM_PROJ
    n_tiles = l // tm
    w3 = w_in.shape[1]
    width = w3 // 3
    r8 = tm // SUBLANES
    body = functools.partial(_inproj_c_body, tm=tm, n_tiles=n_tiles)
    out = SDS((b, l, width), F32)
    return pl.pallas_call(
        body,
        out_shape=(out, out, out),
        grid=(b, n_tiles),
        in_specs=[BS((1, SUBLANES, d), lambda bi, i: (bi, jnp.maximum(i * r8 - 1, 0), 0)),
                  BS((1, tm, d), lambda bi, i: (bi, i, 0)),
                  BS((1, SUBLANES, d), lambda bi, i: (bi, jnp.minimum((i + 1) * r8, l // SUBLANES - 1), 0)),
                  BS((1, SUBLANES, d), lambda bi, i: (bi, 0, 0)),
                  BS((1, d), lambda bi, i: (0, 0)),
                  BS((d, w3), lambda bi, i: (0, 0)),
                  BS((3, w3), lambda bi, i: (0, 0)),
                  BS((1, w3), lambda bi, i: (0, 0))],
        out_specs=(BS((1, tm, width), lambda bi, i: (bi, i, 0)),
                   BS((1, tm, width), lambda bi, i: (bi, i, 0)),
                   BS((1, tm, width), lambda bi, i: (bi, i, 0))),
        compiler_params=_cparams("arbitrary", "arbitrary"),
        name="inproj_c",
    )(x, x, x, mod_lat, g, w_in, conv_w, conv_b)


def _filter_body(f_ref, w1_ref, b1_ref, w2_ref, b2_ref, w3hi_ref, w3lo_ref, fr_ref, dl_ref, keep0_ref,
                 hf_ref, l1_ref, *, tp):
    i = pl.program_id(0)
    feats = f_ref[...]
    fr = fr_ref[...]
    a = jnp.sin(fr * (_dot_hp(feats, w1_ref[...]) + b1_ref[...]))
    a = jnp.sin(fr * (_dot_hp(a, w2_ref[...]) + b2_ref[...]))
    a_hi = a.astype(BF16)
    a_lo = (a - a_hi.astype(F32)).astype(BF16)
    w3_hi = w3hi_ref[...]
    hf = _dot(a_hi, w3_hi) + _dot(a_hi, w3lo_ref[...]) + _dot(a_lo, w3_hi)
    t01 = feats[:, 0:1]
    hf = hf * (jnp.exp(-t01 * jnp.abs(dl_ref[...])) + DECAY_SHIFT)
    row = lax.broadcasted_iota(I32, hf.shape, 0) + i * tp
    hf = jnp.where(row == 0, hf * keep0_ref[...], hf)
    hf_ref[...] = hf

    @pl.when(i == 0)
    def _():
        l1_ref[...] = jnp.zeros_like(l1_ref)

    l1_ref[...] = l1_ref[...] + jnp.sum(jnp.abs(hf), axis=0, keepdims=True)


def _filters(feats, w1, b1, w2, b2, w3, freq, delta, width):
    n, fe = feats.shape
    hid = w2.shape[0]
    fo = w3.shape[1]
    tp = 256
    lag0_keep = jnp.tile(jnp.repeat(jnp.array([1.0, 0.0], F32), width), fo // (2 * width)).reshape(1, fo)
    body = functools.partial(_filter_body, tp=tp)
    full = lambda shape: BS(shape, lambda i: (0, 0))
    w3_hi = w3.astype(BF16)
    w3_lo = (w3 - w3_hi.astype(F32)).astype(BF16)
    return pl.pallas_call(
        body,
        out_shape=(SDS((n, fo), F32), SDS((SUBLANES, fo), F32)),
        grid=(n // tp,),
        in_specs=[BS((tp, fe), lambda i: (i, 0)), full((fe, hid)), full((1, hid)),
                  full((hid, hid)), full((1, hid)), full((hid, fo)), full((hid, fo)), full((1, hid)),
                  full((1, fo)), full((1, fo))],
        out_specs=(BS((tp, fo), lambda i: (i, 0)), BS((SUBLANES, fo), lambda i: (0, 0))),
        compiler_params=_cparams("arbitrary"),
        name="hyena_filters",
    )(feats, w1, b1, w2, b2, w3_hi, w3_lo, freq, delta, lag0_keep)


DFT_R = 128
DFT_VP = 72
DFT_BGROUP = 16
DFT_LANE_TILES = 4


def _dft_tables(n):
    r = DFT_R
    vp = DFT_VP
    m = 2 * n
    na = n // r
    two_pi = 2.0 * np.pi
    live = (np.arange(vp) <= r // 2).astype(np.float64)
    a = np.arange(na)[None, :]
    v = np.arange(vp)[:, None]
    ang1 = two_pi * ((a * v) % r) / r
    f1 = np.concatenate([np.cos(ang1), -np.sin(ang1)], axis=0) * np.tile(live, 2)[:, None]
    b = np.arange(r)[None, None, :]
    u = np.arange(r)[None, :, None]
    vv = np.arange(vp)[:, None, None]
    ang2 = two_pi * ((b * (r * u + vv)) % m) / m
    gr, gi = np.cos(ang2), -np.sin(ang2)
    fwd = np.concatenate([np.concatenate([gr, -gi], axis=2),
                          np.concatenate([gi, gr], axis=2)], axis=1)
    hr, hi = np.transpose(gr, (0, 2, 1)), -np.transpose(gi, (0, 2, 1))
    inv = np.concatenate([np.concatenate([hr, -hi], axis=2),
                          np.concatenate([hi, hr], axis=2)], axis=1)
    weight = live * np.where((np.arange(vp) == 0) | (np.arange(vp) == r // 2), 1.0, 2.0)
    ang3 = two_pi * ((np.arange(na)[:, None] * np.arange(vp)[None, :]) % r) / r
    f3 = np.concatenate([np.cos(ang3) * weight, -np.sin(ang3) * weight], axis=1) / m
    cast = lambda t: jnp.asarray(t.astype(np.float32)).astype(BF16)
    return cast(f1), cast(fwd), cast(inv), cast(f3)


def _lane_tile_specs(rows, index_map_of_tile):
    return [BS((None, rows, DFT_BGROUP, LANES), index_map_of_tile(t)) for t in range(DFT_LANE_TILES)]


def _rows_of_position(ref, j):
    x, bg, _ = ref.shape
    return ref.reshape(x * bg, LANES)[pl.ds(j, x, stride=bg), :]


def _dft_s1_body(*refs):
    q = DFT_LANE_TILES
    y_refs, f1_ref, ar_ref, ai_ref = refs[:q], refs[q], refs[q + 1], refs[q + 2]
    f1 = f1_ref[...]
    for t in range(q):
        lanes = slice(t * LANES, (t + 1) * LANES)
        for j in range(DFT_BGROUP):
            res = _dot(f1, _rows_of_position(y_refs[t], j).astype(BF16))
            ar_ref[:, j, lanes] = res[:DFT_VP]
            ai_ref[:, j, lanes] = res[DFT_VP:]


def _dft_s1(y4, f1):
    nb, na, r, c = y4.shape
    q, bg = DFT_LANE_TILES, DFT_BGROUP
    out = SDS((nb, DFT_VP, r, c), F32)
    plane_spec = BS((None, DFT_VP, bg, q * LANES), lambda i, j, cc: (i, 0, j, cc))
    tile_map = lambda t: (lambda i, j, cc: (i, 0, j, cc * q + t))
    return pl.pallas_call(
        _dft_s1_body,
        out_shape=(out, out),
        grid=(nb, r // bg, c // (q * LANES)),
        in_specs=_lane_tile_specs(na, tile_map) + [BS((2 * DFT_VP, na), lambda i, j, cc: (0, 0))],
        out_specs=(plane_spec, plane_spec),
        compiler_params=_cparams("arbitrary", "arbitrary", "arbitrary"),
        name="dft_stage1",
    )(*([y4] * q), f1)


def _filter_spec_body(arf_ref, aif_ref, arb_ref, aib_ref, g_ref, l1f_ref, l1b_ref, kr_ref, ki_ref):
    g = g_ref[...]
    yf = _dot(g, jnp.concatenate([arf_ref[...], aif_ref[...]], axis=0).astype(BF16))
    yb = _dot(g, jnp.concatenate([arb_ref[...], aib_ref[...]], axis=0).astype(BF16))
    inv = 1.0 / (l1f_ref[0:1, :] + l1b_ref[0:1, :])
    kr_ref[...] = (yf[:DFT_R] + yb[:DFT_R]) * inv
    ki_ref[...] = (yf[DFT_R:] - yb[DFT_R:]) * inv


def _filter_spectrum(ar, ai, fwd, l1, width):
    r = DFT_R
    a_spec = lambda d: BS((None, None, r, width), lambda v, o: (0, v, 0, 2 * o + d))
    l_spec = lambda d: BS((SUBLANES, width), lambda v, o: (0, 2 * o + d))
    out = SDS((DFT_VP, r, HYENA_ORDER * width), F32)
    return pl.pallas_call(
        _filter_spec_body,
        out_shape=(out, out),
        grid=(DFT_VP, HYENA_ORDER),
        in_specs=[a_spec(0), a_spec(0), a_spec(1), a_spec(1),
                  BS((None, 2 * r, 2 * r), lambda v, o: (v, 0, 0)), l_spec(0), l_spec(1)],
        out_specs=(BS((None, r, width), lambda v, o: (v, 0, o)),
                   BS((None, r, width), lambda v, o: (v, 0, o))),
        compiler_params=_cparams("arbitrary", "arbitrary"),
        name="hyena_filter_spectrum",
    )(ar, ai, ar, ai, fwd, l1, l1)


def _conv_mid_body(ar_ref, ai_ref, g_ref, h_ref, kr_ref, ki_ref, qr_ref, qi_ref):
    y = _dot(g_ref[...], jnp.concatenate([ar_ref[...], ai_ref[...]], axis=0).astype(BF16))
    yr, yi = y[:DFT_R], y[DFT_R:]
    kr, ki = kr_ref[...], ki_ref[...]
    p = jnp.concatenate([yr * kr - yi * ki, yr * ki + yi * kr], axis=0).astype(BF16)
    q = _dot(h_ref[...], p)
    qr_ref[...] = q[:DFT_R]
    qi_ref[...] = q[DFT_R:]


def _conv_mid(ar, ai, fwd, inv, kr, ki, order):
    nb, vp, r, c = ar.shape
    a_spec = BS((None, None, r, c), lambda n, v: (n, v, 0, 0))
    m_spec = BS((None, 2 * r, 2 * r), lambda n, v: (v, 0, 0))
    k_spec = BS((None, r, c), lambda n, v: (v, 0, order))
    out = SDS((nb, vp, r, c), F32)
    return pl.pallas_call(
        _conv_mid_body,
        out_shape=(out, out),
        grid=(nb, vp),
        in_specs=[a_spec, a_spec, m_spec, m_spec, k_spec, k_spec],
        out_specs=(a_spec, a_spec),
        compiler_params=_cparams("arbitrary", "arbitrary"),
        name="hyena_spectral_product",
    )(ar, ai, fwd, inv, kr, ki)


def _idft_gate_body(*refs):
    q = DFT_LANE_TILES
    qr_refs, qi_refs, y_refs, gate_refs = (refs[k * q:(k + 1) * q] for k in range(4))
    f3_ref, fb_ref, o_ref = refs[4 * q:]
    f3 = f3_ref[...]
    for t in range(q):
        lanes = slice(t * LANES, (t + 1) * LANES)
        fb = fb_ref[:, lanes]
        for j in range(DFT_BGROUP):
            planes = jnp.concatenate([_rows_of_position(qr_refs[t], j),
                                      _rows_of_position(qi_refs[t], j)], axis=0).astype(BF16)
            conv = _dot(f3, planes)
            o_ref[:, j, lanes] = _rows_of_position(gate_refs[t], j) * (
                conv + fb * _rows_of_position(y_refs[t], j))


def _idft_gate(qr, qi, f3, y4, gate4, fbias):
    nb, na, r, c = y4.shape
    q, bg = DFT_LANE_TILES, DFT_BGROUP
    tile_map = lambda t: (lambda i, j, cc: (i, 0, j, cc * q + t))
    wide = lambda rows: BS((None, rows, bg, q * LANES), lambda i, j, cc: (i, 0, j, cc))
    return pl.pallas_call(
        _idft_gate_body,
        out_shape=SDS((nb, na, r, c), F32),
        grid=(nb, r // bg, c // (q * LANES)),
        in_specs=(_lane_tile_specs(DFT_VP, tile_map) + _lane_tile_specs(DFT_VP, tile_map)
                  + _lane_tile_specs(na, tile_map) + _lane_tile_specs(na, tile_map)
                  + [BS((na, 2 * DFT_VP), lambda i, j, cc: (0, 0)),
                     BS((1, q * LANES), lambda i, j, cc: (0, cc))]),
        out_specs=wide(na),
        compiler_params=_cparams("arbitrary", "arbitrary", "arbitrary"),
        name="hyena_idft_gate",
    )(*([qr] * q + [qi] * q + [y4] * q + [gate4] * q), f3, fbias)


def _outproj_body(y_ref, w_ref, x_ref, mod_ref, o_ref):
    o_ref[0] = x_ref[0] + mod_ref[0, 2:3, :] * _dot(y_ref[0].astype(BF16), w_ref[...])


def _outproj(y, w_out, x, mod_lat):
    b, l, d = x.shape
    tm = TM_PROJ
    wdt = y.shape[-1]
    return pl.pallas_call(
        _outproj_body,
        out_shape=SDS((b, l, d), F32),
        grid=(b, l // tm),
        in_specs=[BS((1, tm, wdt), lambda bi, i: (bi, i, 0)),
                  BS((wdt, d), lambda bi, i: (0, 0)),
                  BS((1, tm, d), lambda bi, i: (bi, i, 0)),
                  BS((1, SUBLANES, d), lambda bi, i: (bi, 0, 0))],
        out_specs=BS((1, tm, d), lambda bi, i: (bi, i, 0)),
        compiler_params=_cparams("arbitrary", "arbitrary"),
        name="outproj_c",
    )(y, w_out, x, mod_lat)


def _hyena(x, mod_lat, g, w_in, conv_w, conv_b, w1, b1, w2, b2, w3, freq, delta, f_bias, w_out):
    b, n, d = x.shape
    width = w_out.shape[0]
    r = DFT_R
    na = n // r
    f1, fwd, inv, f3 = _dft_tables(n)
    v, gate1, gate2 = _inproj_c(x, mod_lat, g, w_in.astype(BF16), conv_w, conv_b.reshape(1, -1))

    t = jnp.arange(n, dtype=F32)
    t01 = t / max(n - 1, 1)
    bands = jnp.linspace(1e-4, FILT_BANDS - 1, FILT_BANDS, dtype=F32)
    ang = (2.0 * math.pi / n) * t[:, None] * bands[None, :]
    feats = jnp.concatenate([t01[:, None], jnp.cos(ang), jnp.sin(ang)], axis=-1)
    fe = feats.shape[1]
    feats = jnp.pad(feats, ((0, 0), (0, LANES - fe)))
    w1p = jnp.pad(w1, ((0, LANES - fe), (0, 0)))
    hf, l1 = _filters(feats, w1p, b1.reshape(1, -1), w2, b2.reshape(1, -1), w3,
                      freq.reshape(1, -1), delta.reshape(1, -1), width)
    far, fai = _dft_s1(hf.reshape(1, na, r, hf.shape[1]), f1)
    kr, ki = _filter_spectrum(far, fai, fwd, l1, width)

    y4 = v.reshape(b, na, r, width)
    for o, gate in enumerate((gate1, gate2)):
        ar, ai = _dft_s1(y4, f1)
        qr, qi = _conv_mid(ar, ai, fwd, inv, kr, ki, o)
        y4 = _idft_gate(qr, qi, f3, y4, gate.reshape(b, na, r, width), f_bias[o].reshape(1, width))
    return _outproj(y4.reshape(b, n, width), w_out.astype(BF16), x, mod_lat)


def _rope_tables(seq_len):
    rows = seq_len // GRID_W
    row = jnp.repeat(jnp.arange(rows, dtype=F32), GRID_W)
    col = jnp.tile(jnp.arange(GRID_W, dtype=F32), rows)
    inv = jnp.power(ROPE_BASE, -jnp.arange(ROPE_FREQS, dtype=F32) / ROPE_FREQS)
    ar, ac = row[:, None] * inv, col[:, None] * inv
    cos_h = jnp.concatenate([jnp.cos(ar), jnp.cos(ar), jnp.cos(ac), jnp.cos(ac)], axis=1)
    sin_h = jnp.concatenate([-jnp.sin(ar), jnp.sin(ar), -jnp.sin(ac), jnp.sin(ac)], axis=1)
    reps = LANES // HEAD_DIM
    return jnp.tile(cos_h, (1, reps)), jnp.tile(sin_h, (1, reps))


def _rotate_partner_columns(w):
    ncol = w.shape[1]
    lane = np.arange(ncol)
    partner = np.where((lane % (2 * ROPE_FREQS)) < ROPE_FREQS, lane + ROPE_FREQS, lane - ROPE_FREQS)
    return w[:, partner]


def kernel(x, c, ctx, c_ctx, w_mod, b_mod, norm_g, w_in_ab, sink, w_spatial, b_spatial, w_out_ab,
           w_in_c, conv_w, conv_b, filt_w1, filt_b1, filt_w2, filt_b2, filt_w3, filt_freq,
           filt_delta, filt_bias, w_out_c, w_router, e_bias, w_gate, w_up, w_down, ws_gate,
           ws_up, ws_down, final_g):
    b, l, d = x.shape
    depth = w_mod.shape[0]
    assert depth == 2 and b + 1 <= SUBLANES

    cc = jnp.zeros((SUBLANES, d), F32).at[:b].set(c).at[b].set(c_ctx)
    m_all = _mod_vectors(cc, w_mod, b_mod)

    def mod_rows(layer, row0, nrow):
        m = m_all[layer, row0:row0 + nrow].reshape(nrow, 6, d)
        return jnp.pad(m, ((0, 0), (0, SUBLANES - 6), (0, 0)))

    tri = jnp.triu(jnp.ones((MOE_TILE, MOE_TILE), F32), k=1).astype(BF16)

    mod_lat = mod_rows(0, 0, b)
    mod_ctx = mod_rows(0, b, 1)[0]
    w_in = w_in_ab[0]
    qk = ATTN_WIDTH + KV_WIDTH
    w_cat = jnp.concatenate([w_in, _rotate_partner_columns(w_in[:, :qk])], axis=1).astype(BF16)
    cos_t, sin_t = _rope_tables(l)
    group_avg = jnp.kron(jnp.eye(N_SG_GROUPS, dtype=F32),
                         jnp.full((SG_GROUP_DIM, SG_GROUP_DIM), 1.0 / SG_GROUP_DIM, F32)).astype(BF16)
    kc, vc = _ctx_kv(ctx, mod_ctx, norm_g[0, 0].reshape(1, d),
                     w_in[:, ATTN_WIDTH:ATTN_WIDTH + 2 * KV_WIDTH].astype(BF16))
    q, k, v, ug, vn = _inproj_ab(x, mod_lat, norm_g[0, 0].reshape(1, d), w_cat, cos_t, sin_t, group_avg)
    b_full = jnp.repeat(b_spatial[0].T, SG_GROUP_DIM, axis=1)
    x1 = _mixer(sink[0], q, k, v, kc, vc, ug, vn, w_spatial[0].astype(BF16), b_full,
                w_out_ab[0].astype(BF16), x, mod_lat)
    x2 = _moe(x1.reshape(b * l, d), mod_lat, norm_g[0, 1].reshape(1, d), w_router[0], e_bias[0],
              w_gate, w_up, w_down, ws_gate[0], ws_up[0], ws_down[0], final_g, tri, l,
              layer=0, final=False).reshape(b, l, d)

    mod_lat = mod_rows(1, 0, b)
    x3 = _hyena(x2, mod_lat, norm_g[1, 0].reshape(1, d), w_in_c[0], conv_w[0], conv_b[0],
                filt_w1[0], filt_b1[0], filt_w2[0], filt_b2[0], filt_w3[0], filt_freq[0],
                filt_delta[0], filt_bias[0], w_out_c[0])
    out = _moe(x3.reshape(b * l, d), mod_lat, norm_g[1, 1].reshape(1, d), w_router[1], e_bias[1],
               w_gate, w_up, w_down, ws_gate[1], ws_up[1], ws_down[1], final_g, tri, l,
               layer=1, final=True)
    return out.reshape(b, l, d)
```

```python
import functools
import math

import numpy as np
import jax
import jax.numpy as jnp
from jax import lax
from jax.experimental import pallas as pl
from jax.experimental.pallas import tpu as pltpu

F32 = jnp.float32
BF16 = jnp.bfloat16
I32 = jnp.int32
HIGHEST = lax.Precision.HIGHEST
SDS = jax.ShapeDtypeStruct
BS = pl.BlockSpec

EPS = 1e-6
NEG = -1e30

GRID_W = 64
N_Q_HEADS = 8
N_KV_HEADS = 2
HEAD_DIM = 64
ATTN_WIDTH = N_Q_HEADS * HEAD_DIM
KV_WIDTH = N_KV_HEADS * HEAD_DIM
WINDOW = 128
BLOCK = 128
ROPE_BASE = 10000.0
ROPE_FREQS = HEAD_DIM // 4
N_SG_GROUPS = 8
SG_GROUP_DIM = 64
SG_WIDTH = N_SG_GROUPS * SG_GROUP_DIM
HYENA_ORDER = 2
FILT_BANDS = 16
DECAY_SHIFT = 0.05
N_EXPERTS = 64
TOP_K = 8
N_GROUPS = 8
TOPK_GROUPS = 4
ROUTED_SCALE = 2.5

LANES = 128
SUBLANES = 8
VMEM_LIMIT = 56 * 1024 * 1024

TM_PROJ = 512
TQ_MIX = 256
MOE_TILE = 256
BM_FFN = 512
RUN_ALIGN = 16
SLOT_CHUNK = 512
NO_SLOT = 256 * 256 - 1
LOCAL_SLOTS = -(-(TOP_K * MOE_TILE + N_EXPERTS * (RUN_ALIGN - 1)) // SLOT_CHUNK) * SLOT_CHUNK
CHUNK_TABLE = -(-(LOCAL_SLOTS // RUN_ALIGN) // LANES) * LANES
FFN_IN_BUFS = 4


def _cparams(*sem):
    return pltpu.CompilerParams(dimension_semantics=sem, vmem_limit_bytes=VMEM_LIMIT)


def _dot(a, b):
    return jnp.dot(a, b, preferred_element_type=F32)


def _dot_nt(a, b):
    return lax.dot_general(a, b, (((1,), (1,)), ((), ())), preferred_element_type=F32)


def _dot_hp(a, b):
    return jnp.dot(a, b, preferred_element_type=F32, precision=HIGHEST)


def _norm_mod(x, g, sc, sh):
    ms = jnp.mean(x * x, axis=-1, keepdims=True)
    y = x * lax.rsqrt(ms + EPS)
    return (y * g) * (1.0 + sc) + sh


def _gelu_tanh(x):
    c = math.sqrt(2.0 / math.pi)
    return 0.5 * x * (1.0 + jnp.tanh(c * (x + 0.044715 * (x * x * x))))


def _silu(x):
    return x * jax.nn.sigmoid(x)


def _mod_body(c_ref, w_ref, b_ref, o_ref):
    o_ref[0] = _dot_hp(_silu(c_ref[...]), w_ref[0]) + b_ref[0]


def _mod_vectors(cc, w_mod, b_mod):
    depth, d, n = w_mod.shape
    tn = 1536
    return pl.pallas_call(
        _mod_body,
        out_shape=SDS((depth, SUBLANES, n), F32),
        grid=(depth, n // tn),
        in_specs=[BS((SUBLANES, d), lambda l, j: (0, 0)),
                  BS((1, d, tn), lambda l, j: (l, 0, j)),
                  BS((1, 1, tn), lambda l, j: (l, 0, j))],
        out_specs=BS((1, SUBLANES, tn), lambda l, j: (l, 0, j)),
        compiler_params=_cparams("arbitrary", "arbitrary"),
        name="mod_vectors",
    )(cc, w_mod, b_mod.reshape(depth, 1, n))


def _ctx_kv_body(ctx_ref, mod_ref, g_ref, w_ref, kc_ref, vc_ref):
    h = _norm_mod(ctx_ref[0], g_ref[...], mod_ref[1:2, :], mod_ref[0:1, :])
    z = _dot(h.astype(BF16), w_ref[...])
    kc_ref[0] = z[:, :KV_WIDTH].astype(BF16)
    vc_ref[0] = z[:, KV_WIDTH:].astype(BF16)


def _ctx_kv(ctx, mod_ctx, g, w_kv):
    b, c, d = ctx.shape
    return pl.pallas_call(
        _ctx_kv_body,
        out_shape=(SDS((b, c, KV_WIDTH), BF16), SDS((b, c, KV_WIDTH), BF16)),
        grid=(b,),
        in_specs=[BS((1, c, d), lambda i: (i, 0, 0)),
                  BS((SUBLANES, d), lambda i: (0, 0)),
                  BS((1, d), lambda i: (0, 0)),
                  BS((d, 2 * KV_WIDTH), lambda i: (0, 0))],
        out_specs=(BS((1, c, KV_WIDTH), lambda i: (i, 0, 0)),
                   BS((1, c, KV_WIDTH), lambda i: (i, 0, 0))),
        compiler_params=_cparams("arbitrary"),
        name="ctx_kv",
    )(ctx, mod_ctx, g, w_kv)


def _inproj_ab_body(x_ref, mod_ref, g_ref, w_ref, cos_ref, sin_ref, avg_ref,
                    q_ref, k_ref, v_ref, ug_ref, vn_ref):
    h = _norm_mod(x_ref[0], g_ref[...], mod_ref[0, 1:2, :], mod_ref[0, 0:1, :]).astype(BF16)
    cs = cos_ref[...]
    sn = sin_ref[...]
    rot0 = ATTN_WIDTH + 2 * KV_WIDTH + 2 * SG_WIDTH
    scale = HEAD_DIM ** -0.5
    for j in range(ATTN_WIDTH // LANES):
        z = _dot(h, w_ref[:, j * LANES:(j + 1) * LANES])
        zr = _dot(h, w_ref[:, rot0 + j * LANES:rot0 + (j + 1) * LANES])
        q_ref[0, :, j * LANES:(j + 1) * LANES] = ((z * cs + zr * sn) * scale).astype(BF16)
    zk = _dot(h, w_ref[:, ATTN_WIDTH:ATTN_WIDTH + KV_WIDTH])
    zkr = _dot(h, w_ref[:, rot0 + ATTN_WIDTH:rot0 + ATTN_WIDTH + KV_WIDTH])
    k_ref[0] = (zk * cs + zkr * sn).astype(BF16)
    v_ref[0] = _dot(h, w_ref[:, ATTN_WIDTH + KV_WIDTH:ATTN_WIDTH + 2 * KV_WIDTH]).astype(BF16)
    u0 = ATTN_WIDTH + 2 * KV_WIDTH
    ug_ref[0] = _gelu_tanh(_dot(h, w_ref[:, u0:u0 + SG_WIDTH]))
    vf = _gelu_tanh(_dot(h, w_ref[:, u0 + SG_WIDTH:u0 + 2 * SG_WIDTH]))
    avg = avg_ref[...]

    def gmean(t):
        hi = t.astype(BF16)
        lo = (t - hi.astype(F32)).astype(BF16)
        return _dot(hi, avg) + _dot(lo, avg)

    vc = vf - gmean(vf)
    vn_ref[0] = (vc * lax.rsqrt(gmean(vc * vc) + EPS)).astype(BF16)


def _inproj_ab(x, mod_lat, g, w_cat, cos_t, sin_t, avg):
    b, l, d = x.shape
    tm = TM_PROJ
    ncol = w_cat.shape[1]
    return pl.pallas_call(
        _inproj_ab_body,
        out_shape=(SDS((b, l, ATTN_WIDTH), BF16), SDS((b, l, KV_WIDTH), BF16),
                   SDS((b, l, KV_WIDTH), BF16), SDS((b, l, SG_WIDTH), F32),
                   SDS((b, l, SG_WIDTH), BF16)),
        grid=(b, l // tm),
        in_specs=[BS((1, tm, d), lambda bi, i: (bi, i, 0)),
                  BS((1, SUBLANES, d), lambda bi, i: (bi, 0, 0)),
                  BS((1, d), lambda bi, i: (0, 0)),
                  BS((d, ncol), lambda bi, i: (0, 0)),
                  BS((tm, LANES), lambda bi, i: (i, 0)),
                  BS((tm, LANES), lambda bi, i: (i, 0)),
                  BS((SG_WIDTH, SG_WIDTH), lambda bi, i: (0, 0))],
        out_specs=(BS((1, tm, ATTN_WIDTH), lambda bi, i: (bi, i, 0)),
                   BS((1, tm, KV_WIDTH), lambda bi, i: (bi, i, 0)),
                   BS((1, tm, KV_WIDTH), lambda bi, i: (bi, i, 0)),
                   BS((1, tm, SG_WIDTH), lambda bi, i: (bi, i, 0)),
                   BS((1, tm, SG_WIDTH), lambda bi, i: (bi, i, 0))),
        compiler_params=_cparams("arbitrary", "arbitrary"),
        name="inproj_ab",
    )(x, mod_lat, g, w_cat, cos_t, sin_t, avg)


def _mixer_body(sink_ref, q_ref, kp_ref, kcur_ref, kn_ref, vp_ref, vcur_ref, vn_ref,
                kc_ref, vc_ref, ug_ref, vnorm_ref, ws_ref, bs_ref, wout_ref, x_ref, mod_ref,
                o_ref, cat_ref, *, seq_len, sub_blocks):
    i = pl.program_id(1)
    kk = jnp.concatenate([kp_ref[0], kcur_ref[0], kn_ref[0]], axis=0)
    vv = jnp.concatenate([vp_ref[0], vcur_ref[0], vn_ref[0]], axis=0)
    kc = kc_ref[0]
    vc = vc_ref[0]
    span = 3 * BLOCK
    ii = lax.broadcasted_iota(I32, (BLOCK, span), 0)
    jj = lax.broadcasted_iota(I32, (BLOCK, span), 1)
    dd = jj - ii
    in_window = jnp.where(dd >= 0, jnp.where(dd <= 2 * WINDOW, 1, 0), 0)
    group = N_Q_HEADS // N_KV_HEADS
    for r in range(sub_blocks):
        rows = slice(r * BLOCK, (r + 1) * BLOCK)
        kpos = (i * sub_blocks + r - 1) * BLOCK + jj
        in_seq = jnp.where(kpos >= 0, jnp.where(kpos < seq_len, 1, 0), 0)
        bias = jnp.where(in_window * in_seq > 0, 0.0, NEG)
        qb = q_ref[0, rows, :]
        kl = kk[r * BLOCK:r * BLOCK + span]
        vl = vv[r * BLOCK:r * BLOCK + span]
        for hq in range(N_Q_HEADS):
            hk = hq // group
            ks = slice(hk * HEAD_DIM, (hk + 1) * HEAD_DIM)
            qh = qb[:, hq * HEAD_DIM:(hq + 1) * HEAD_DIM]
            s_loc = _dot_nt(qh, kl[:, ks]) + bias
            s_ctx = _dot_nt(qh, kc[:, ks])
            sk = sink_ref[hq]
            m = jnp.maximum(jnp.maximum(jnp.max(s_loc, axis=-1, keepdims=True),
                                        jnp.max(s_ctx, axis=-1, keepdims=True)), sk)
            p_loc = jnp.exp(s_loc - m)
            p_ctx = jnp.exp(s_ctx - m)
            den = (jnp.sum(p_loc, axis=-1, keepdims=True) + jnp.sum(p_ctx, axis=-1, keepdims=True)
                   + jnp.exp(sk - m))
            o = _dot(p_loc.astype(BF16), vl[:, ks]) + _dot(p_ctx.astype(BF16), vc[:, ks])
            cat_ref[rows, hq * HEAD_DIM:(hq + 1) * HEAD_DIM] = (o / den).astype(BF16)
        vnb = vnorm_ref[0, rows, :]
        ugb = ug_ref[0, rows, :]
        for g in range(N_SG_GROUPS):
            gs = slice(g * SG_GROUP_DIM, (g + 1) * SG_GROUP_DIM)
            sg = _dot(ws_ref[g], vnb[:, gs]) + bs_ref[:, gs]
            cat_ref[rows, ATTN_WIDTH + g * SG_GROUP_DIM:ATTN_WIDTH + (g + 1) * SG_GROUP_DIM] = (
                ugb[:, gs] * sg).astype(BF16)
    y = _dot(cat_ref[...], wout_ref[...])
    o_ref[0] = x_ref[0] + mod_ref[0, 2:3, :] * y


def _mixer(sink, q, k, v, kc, vc, ug, vn, w_s, b_full, w_out, x, mod_lat):
    b, l, d = x.shape
    tq = TQ_MIX
    r = tq // BLOCK
    nb = l // BLOCK
    c = kc.shape[1]
    prev_map = lambda bi, i: (bi, jnp.maximum(i * r - 1, 0), 0)
    next_map = lambda bi, i: (bi, jnp.minimum((i + 1) * r, nb - 1), 0)
    cur_map = lambda bi, i: (bi, i, 0)
    body = functools.partial(_mixer_body, seq_len=l, sub_blocks=r)
    return pl.pallas_call(
        body,
        out_shape=SDS((b, l, d), F32),
        grid=(b, l // tq),
        in_specs=[BS(memory_space=pltpu.SMEM),
                  BS((1, tq, ATTN_WIDTH), cur_map),
                  BS((1, BLOCK, KV_WIDTH), prev_map), BS((1, tq, KV_WIDTH), cur_map),
                  BS((1, BLOCK, KV_WIDTH), next_map),
                  BS((1, BLOCK, KV_WIDTH), prev_map), BS((1, tq, KV_WIDTH), cur_map),
                  BS((1, BLOCK, KV_WIDTH), next_map),
                  BS((1, c, KV_WIDTH), lambda bi, i: (bi, 0, 0)),
                  BS((1, c, KV_WIDTH), lambda bi, i: (bi, 0, 0)),
                  BS((1, tq, SG_WIDTH), cur_map), BS((1, tq, SG_WIDTH), cur_map),
                  BS((N_SG_GROUPS, BLOCK, BLOCK), lambda bi, i: (0, 0, 0)),
                  BS((BLOCK, SG_WIDTH), lambda bi, i: (0, 0)),
                  BS((d, d), lambda bi, i: (0, 0)),
                  BS((1, tq, d), cur_map),
                  BS((1, SUBLANES, d), lambda bi, i: (bi, 0, 0))],
        out_specs=BS((1, tq, d), cur_map),
        scratch_shapes=[pltpu.VMEM((tq, d), BF16)],
        compiler_params=_cparams("arbitrary", "arbitrary"),
        name="mixer_ab",
    )(sink, q, k, k, k, v, v, v, kc, vc, ug, vn, w_s, b_full, w_out, x, mod_lat)


def _router_body(x_ref, mod_ref, g_ref, wr_ref, eb_ref, tri_ref,
                 h2b_ref, slot_hi_ref, slot_lo_ref, slot_hi_t_ref, slot_lo_t_ref, gate_t_ref,
                 lo_col_ref, hi_col_ref, bounds_row_ref, chunk_e_ref, chunk_rel_ref, rows_ref,
                 cnt_ref, carry_ref, *, tm):
    i = pl.program_id(0)

    @pl.when(i == 0)
    def _():
        carry_ref[...] = jnp.zeros_like(carry_ref)

    h2 = _norm_mod(x_ref[...], g_ref[...], mod_ref[0, 4:5, :], mod_ref[0, 3:4, :])
    h2b_ref[...] = h2.astype(BF16)

    logits = lax.dot_general(wr_ref[...], h2, (((1,), (1,)), ((), ())),
                             preferred_element_type=F32, precision=HIGHEST)
    scores = jax.nn.sigmoid(logits)
    per_group = N_EXPERTS // N_GROUPS
    shape3 = (N_GROUPS, per_group, tm)
    s3 = scores.reshape(shape3)
    b3 = (scores + eb_ref[...]).reshape(shape3)
    sub = lax.broadcasted_iota(I32, shape3, 1)
    eid = lax.broadcasted_iota(I32, shape3, 0) * per_group + sub

    m1 = jnp.max(b3, axis=1, keepdims=True)
    i1 = jnp.min(jnp.where(b3 == m1, sub, per_group), axis=1, keepdims=True)
    m2 = jnp.max(jnp.where(sub == i1, -jnp.inf, b3), axis=1, keepdims=True)
    gs = m1 + m2
    keep = []
    for g in range(N_GROUPS):
        beaten = jnp.zeros((1, tm), I32)
        for g2 in range(N_GROUPS):
            if g2 == g:
                continue
            wins = (gs[g2] >= gs[g]) if g2 < g else (gs[g2] > gs[g])
            beaten = beaten + jnp.where(wins, 1, 0)
        keep.append(jnp.where(beaten < TOPK_GROUPS, 1, 0)[None])
    keep3 = jnp.concatenate(keep, axis=0)
    val = jnp.where(keep3 > 0, b3, -jnp.inf)

    def red(fn, a):
        return fn(fn(a, axis=0, keepdims=True), axis=1, keepdims=True)

    idxs, ws = [], []
    member = jnp.zeros(shape3, F32)
    for _ in range(TOP_K):
        m = red(jnp.max, val)
        idx = red(jnp.min, jnp.where(val == m, eid, N_EXPERTS))
        hit = eid == idx
        ws.append(red(jnp.sum, jnp.where(hit, s3, 0.0)))
        val = jnp.where(hit, -jnp.inf, val)
        member = member + jnp.where(hit, 1.0, 0.0)
        idxs.append(idx)
    wsum = ws[0]
    for w in ws[1:]:
        wsum = wsum + w

    member2 = member.reshape(N_EXPERTS, tm)
    cnt = jnp.sum(member2, axis=1, keepdims=True)
    runlen = jnp.floor((cnt + (RUN_ALIGN - 1)) * (1.0 / RUN_ALIGN)) * RUN_ALIGN
    runlen_b = jnp.broadcast_to(runlen, (N_EXPERTS, LANES))
    e_row = lax.broadcasted_iota(I32, (N_EXPERTS, N_EXPERTS), 0)
    e_col = lax.broadcasted_iota(I32, (N_EXPERTS, N_EXPERTS), 1)
    earlier = jnp.where(e_col < e_row, 1.0, 0.0).astype(BF16)
    loff = _dot(earlier, runlen_b.astype(BF16))
    slot = _dot(member2.astype(BF16), tri_ref[...]) + loff[:, 0:1]
    slot = jnp.where(member2 > 0.0, slot, float(NO_SLOT))
    slot_hi = jnp.floor(slot * (1.0 / 256.0))
    slot_lo = slot - 256.0 * slot_hi
    gate3 = jnp.zeros(shape3, F32)
    for k in range(TOP_K):
        gate3 = jnp.where(eid == idxs[k], ws[k] / wsum * ROUTED_SCALE, gate3)
    gate = gate3.reshape(N_EXPERTS, tm)
    no_expert = jnp.zeros((LANES - N_EXPERTS, tm), F32)
    pad_e = lambda a: jnp.concatenate([a, no_expert], axis=0)
    slot_hi_ref[0] = pad_e(slot_hi).astype(BF16)
    slot_lo_ref[0] = pad_e(slot_lo).astype(BF16)
    slot_hi_t_ref[0] = pad_e(slot_hi).T.astype(BF16)
    slot_lo_t_ref[0] = pad_e(slot_lo).T.astype(BF16)
    gate_t_ref[0] = pad_e(gate).T.astype(BF16)
    run_lo = loff
    run_hi = loff + runlen_b
    no_run = jnp.zeros((LANES - N_EXPERTS, LANES), F32)
    lo_col_ref[0] = jnp.concatenate([run_lo, no_run], axis=0)
    hi_col_ref[0] = jnp.concatenate([run_hi, no_run], axis=0)
    diag = (lax.broadcasted_iota(I32, (N_EXPERTS, LANES), 0)
            == lax.broadcasted_iota(I32, (N_EXPERTS, LANES), 1))
    bounds_row_ref[0] = jnp.concatenate(
        [jnp.sum(jnp.where(diag, run_lo, 0.0), axis=0, keepdims=True),
         jnp.sum(jnp.where(diag, run_hi, 0.0), axis=0, keepdims=True),
         jnp.zeros((SUBLANES - 2, LANES), F32)], axis=0)
    chunk_hi = (loff[:, 0:1] + runlen) * (1.0 / RUN_ALIGN)
    j = lax.broadcasted_iota(I32, (N_EXPERTS, CHUNK_TABLE), 1).astype(F32)
    e_of_chunk = jnp.sum(jnp.where(chunk_hi <= j, 1.0, 0.0), axis=0, keepdims=True)
    e_iota = lax.broadcasted_iota(I32, (N_EXPERTS, CHUNK_TABLE), 0).astype(F32)
    seg_off = carry_ref[:, 0:1] - loff[:, 0:1]
    rel = jnp.sum(jnp.where(e_iota == e_of_chunk, seg_off, 0.0), axis=0, keepdims=True)
    chunk_e_ref[0] = jnp.minimum(e_of_chunk, N_EXPERTS - 1.0).astype(I32)
    chunk_rel_ref[0] = (rel + j[0:1, :] * RUN_ALIGN).astype(I32)
    rows_ref[0] = (loff[N_EXPERTS - 1:N_EXPERTS, :] + runlen_b[N_EXPERTS - 1:N_EXPERTS, :]).astype(I32)
    total = carry_ref[...] + runlen_b
    carry_ref[...] = total
    cnt_ref[...] = total


def _router(x_flat, mod_lat, g, wr_t, e_bias, tri, tokens_per_batch):
    t, d = x_flat.shape
    tm = MOE_TILE
    n_tiles = t // tm
    tiles_per_batch = tokens_per_batch // tm
    body = functools.partial(_router_body, tm=tm)
    table = SDS((n_tiles, 1, CHUNK_TABLE), I32)
    table_spec = BS((1, 1, CHUNK_TABLE), lambda i: (i, 0, 0))
    expert_major = SDS((n_tiles, LANES, tm), BF16)
    expert_major_spec = BS((1, LANES, tm), lambda i: (i, 0, 0))
    token_major = SDS((n_tiles, tm, LANES), BF16)
    token_major_spec = BS((1, tm, LANES), lambda i: (i, 0, 0))
    per_expert = SDS((n_tiles, LANES, LANES), F32)
    per_expert_spec = BS((1, LANES, LANES), lambda i: (i, 0, 0))
    return pl.pallas_call(
        body,
        out_shape=(SDS((t, d), BF16),
                   expert_major, expert_major, token_major, token_major, token_major,
                   per_expert, per_expert, SDS((n_tiles, SUBLANES, LANES), F32),
                   table, table, SDS((n_tiles, 1, LANES), I32), SDS((N_EXPERTS, LANES), F32)),
        grid=(n_tiles,),
        in_specs=[BS((tm, d), lambda i: (i, 0)),
                  BS((1, SUBLANES, d), lambda i: (i // tiles_per_batch, 0, 0)),
                  BS((1, d), lambda i: (0, 0)),
                  BS((N_EXPERTS, d), lambda i: (0, 0)),
                  BS((N_EXPERTS, 1), lambda i: (0, 0)),
                  BS((tm, tm), lambda i: (0, 0))],
        out_specs=(BS((tm, d), lambda i: (i, 0)),
                   expert_major_spec, expert_major_spec, token_major_spec, token_major_spec,
                   token_major_spec, per_expert_spec, per_expert_spec,
                   BS((1, SUBLANES, LANES), lambda i: (i, 0, 0)),
                   table_spec, table_spec, BS((1, 1, LANES), lambda i: (i, 0, 0)),
                   BS((N_EXPERTS, LANES), lambda i: (0, 0))),
        scratch_shapes=[pltpu.VMEM((N_EXPERTS, LANES), F32)],
        compiler_params=_cparams("arbitrary"),
        name="moe_router",
    )(x_flat, mod_lat, g, wr_t, e_bias, tri)


def _start_run_chunks(pstart_ref, chunk_e_ref, chunk_rel_ref, rows, chunk_copy):
    def body(j, carry):
        slot0 = pstart_ref[chunk_e_ref[0, 0, j]] + chunk_rel_ref[0, 0, j]
        chunk_copy(pl.multiple_of(j * RUN_ALIGN, RUN_ALIGN),
                   pl.multiple_of(slot0, RUN_ALIGN)).start()
        return carry

    lax.fori_loop(0, rows // RUN_ALIGN, body, 0)


def _wait_run_rows(copy_of_rows, rows):
    def wait_n(nrows):
        def body(_, carry):
            copy_of_rows(nrows).wait()
            return carry
        return body

    lax.fori_loop(0, rows // SLOT_CHUNK, wait_n(SLOT_CHUNK), 0)
    lax.fori_loop(0, (rows % SLOT_CHUNK) // RUN_ALIGN, wait_n(RUN_ALIGN), 0)


def _dispatch_body(pend_ref, pcnt_ref, nu_ref, pstart_ref, rows_ref, chunk_e_ref, chunk_rel_ref,
                   slot_hi_ref, slot_lo_ref, bounds_ref, h_ref, xs_ref, loc_ref, zbuf_ref, sem, zsem,
                   *, tl, bm, n_blocks):
    i = pl.program_id(0)
    last = pl.num_programs(0) - 1
    buf = i % 2

    def zero_copy(row0):
        return pltpu.make_async_copy(
            zbuf_ref, xs_ref.at[pl.ds(pl.multiple_of(row0, RUN_ALIGN), bm), :], zsem)

    @pl.when(i == 0)
    def _():
        zbuf_ref[...] = jnp.zeros_like(zbuf_ref)

        def start(e, c):
            @pl.when(pcnt_ref[e] > 0)
            def _():
                zero_copy(pend_ref[e] - bm).start()
            return c

        def wait(e, c):
            @pl.when(pcnt_ref[e] > 0)
            def _():
                zero_copy(pend_ref[e] - bm).wait()
            return c

        def start_tail(j, c):
            zero_copy(j * bm).start()
            return c

        def wait_tail(j, c):
            zero_copy(j * bm).wait()
            return c

        lax.fori_loop(0, N_EXPERTS, start, 0)
        lax.fori_loop(nu_ref[0], n_blocks, start_tail, 0)
        lax.fori_loop(0, N_EXPERTS, wait, 0)
        lax.fori_loop(nu_ref[0], n_blocks, wait_tail, 0)

    rows_used = rows_ref[i]
    h = h_ref[...]
    run_lo = bounds_ref[0, 0:1, :]
    run_hi = bounds_ref[0, 1:2, :]

    def sort_chunk(c, carry):
        row0 = (c * SLOT_CHUNK).astype(F32)
        row_e = lax.broadcasted_iota(I32, (SLOT_CHUNK, LANES), 0).astype(F32) + row0
        in_run = jnp.where(row_e >= run_lo, jnp.where(row_e < run_hi, 1.0, 0.0), 0.0).astype(BF16)
        slot_of_token = (256.0 * _dot(in_run, slot_hi_ref[0]) + _dot(in_run, slot_lo_ref[0]))
        row_t = lax.broadcasted_iota(I32, (SLOT_CHUNK, tl), 0).astype(F32) + row0
        onehot = jnp.where(slot_of_token == row_t, 1.0, 0.0).astype(BF16)
        rows = pl.ds(pl.multiple_of(c * SLOT_CHUNK, SLOT_CHUNK), SLOT_CHUNK)
        loc_ref[buf, rows, :] = _dot(onehot, h).astype(BF16)
        return carry

    lax.fori_loop(0, (rows_used + SLOT_CHUNK - 1) // SLOT_CHUNK, sort_chunk, 0)

    def run_copy(b, loc0, slot0, nrows=RUN_ALIGN):
        return pltpu.make_async_copy(loc_ref.at[b, pl.ds(loc0, nrows), :],
                                     xs_ref.at[pl.ds(slot0, nrows), :], sem.at[b])

    _start_run_chunks(pstart_ref, chunk_e_ref, chunk_rel_ref, rows_used,
                      lambda loc0, slot0: run_copy(buf, loc0, slot0))

    @pl.when(i > 0)
    def _():
        _wait_run_rows(lambda n: run_copy(1 - buf, 0, 0, n), rows_ref[jnp.maximum(i - 1, 0)])

    @pl.when(i == last)
    def _():
        _wait_run_rows(lambda n: run_copy(buf, 0, 0, n), rows_used)


def _dispatch(pend, pcnt, n_used, pstart, rows, chunk_e, chunk_rel, slot_hi, slot_lo, bounds_row,
              h2b, n_slots):
    t, d = h2b.shape
    tl = MOE_TILE
    body = functools.partial(_dispatch_body, tl=tl, bm=BM_FFN, n_blocks=n_slots // BM_FFN)
    table_spec = lambda index: BS((1, 1, CHUNK_TABLE), index, memory_space=pltpu.SMEM)
    grid_spec = pltpu.PrefetchScalarGridSpec(
        num_scalar_prefetch=5,
        grid=(t // tl,),
        in_specs=[table_spec(lambda i, *_: (i, 0, 0)),
                  table_spec(lambda i, *_: (i, 0, 0)),
                  BS((1, LANES, tl), lambda i, *_: (i, 0, 0)),
                  BS((1, LANES, tl), lambda i, *_: (i, 0, 0)),
                  BS((1, SUBLANES, LANES), lambda i, *_: (i, 0, 0)),
                  BS((tl, d), lambda i, *_: (i, 0))],
        out_specs=BS(memory_space=pl.ANY),
        scratch_shapes=[pltpu.VMEM((2, LOCAL_SLOTS, d), BF16), pltpu.VMEM((BM_FFN, d), BF16),
                        pltpu.SemaphoreType.DMA((2,)), pltpu.SemaphoreType.DMA(())],
    )
    return pl.pallas_call(
        body,
        out_shape=SDS((n_slots, d), BF16),
        grid_spec=grid_spec,
        compiler_params=_cparams("arbitrary"),
        name="moe_dispatch",
    )(pend, pcnt, n_used, pstart, rows, chunk_e, chunk_rel, slot_hi, slot_lo, bounds_row, h2b)


def _ffn_body(first_ref, count_ref, nu_ref, xs_ref, wg_ref, wu_ref, wd_ref, ys_ref, wgb_ref, wub_ref,
              wdb_ref, xbuf_ref, ybuf_ref, isem, osem, *, bm, n_blocks):
    e = pl.program_id(0)
    nu = nu_ref[0]

    def block_rows(blk):
        return pl.ds(pl.multiple_of(blk * bm, bm), bm)

    def in_copy(blk, slot):
        return pltpu.make_async_copy(xs_ref.at[block_rows(blk), :], xbuf_ref.at[slot], isem.at[slot])

    def out_copy(blk, slot):
        return pltpu.make_async_copy(ybuf_ref.at[slot], ys_ref.at[block_rows(blk), :], osem.at[slot])

    @pl.when(e == 0)
    def _():
        for s in range(FFN_IN_BUFS):
            @pl.when(s < nu)
            def _():
                in_copy(s, s).start()

    @pl.when(count_ref[e] > 0)
    def _():
        wgb_ref[...] = wg_ref[...].astype(BF16)
        wub_ref[...] = wu_ref[...].astype(BF16)
        wdb_ref[...] = wd_ref[...].astype(BF16)

    def one_block(b, carry):
        i = first_ref[e] + b
        slot = i % FFN_IN_BUFS
        oslot = i % 2
        in_copy(i, slot).wait()
        x = xbuf_ref[slot]
        a = _silu(_dot(x, wgb_ref[...])) * _dot(x, wub_ref[...])
        y = _dot(a.astype(BF16), wdb_ref[...]).astype(BF16)

        @pl.when(i >= 2)
        def _():
            out_copy(i - 2, oslot).wait()

        ybuf_ref[oslot] = y
        out_copy(i, oslot).start()

        @pl.when(i + FFN_IN_BUFS < nu)
        def _():
            in_copy(i + FFN_IN_BUFS, slot).start()

        return carry

    lax.fori_loop(0, count_ref[e], one_block, 0)

    @pl.when(e == pl.num_programs(0) - 1)
    def _():
        @pl.when(nu >= 2)
        def _():
            out_copy(nu - 2, (nu - 2) % 2).wait()

        out_copy(nu - 1, (nu - 1) % 2).wait()
        ybuf_ref[0] = jnp.zeros((bm, ybuf_ref.shape[-1]), BF16)

        def start_tail(j, c):
            out_copy(j, 0).start()
            return c

        def wait_tail(j, c):
            out_copy(j, 0).wait()
            return c

        lax.fori_loop(nu, n_blocks, start_tail, 0)
        lax.fori_loop(nu, n_blocks, wait_tail, 0)


def _ffn(first_block, block_count, n_used, xs, w_gate, w_up, w_down, layer):
    n_slots, d = xs.shape
    bm = BM_FFN
    de = w_gate.shape[-1]
    n_blocks = n_slots // bm
    body = functools.partial(_ffn_body, bm=bm, n_blocks=n_blocks)
    grid_spec = pltpu.PrefetchScalarGridSpec(
        num_scalar_prefetch=3,
        grid=(N_EXPERTS,),
        in_specs=[BS(memory_space=pl.ANY),
                  BS((None, None, d, de), lambda e, *_: (layer, e, 0, 0)),
                  BS((None, None, d, de), lambda e, *_: (layer, e, 0, 0)),
                  BS((None, None, de, d), lambda e, *_: (layer, e, 0, 0))],
        out_specs=BS(memory_space=pl.ANY),
        scratch_shapes=[pltpu.VMEM((d, de), BF16), pltpu.VMEM((d, de), BF16),
                        pltpu.VMEM((de, d), BF16),
                        pltpu.VMEM((FFN_IN_BUFS, bm, d), BF16), pltpu.VMEM((2, bm, d), BF16),
                        pltpu.SemaphoreType.DMA((FFN_IN_BUFS,)), pltpu.SemaphoreType.DMA((2,))],
    )
    return pl.pallas_call(
        body,
        out_shape=SDS((n_slots, d), BF16),
        grid_spec=grid_spec,
        compiler_params=_cparams("arbitrary"),
        name="moe_experts",
    )(first_block, block_count, n_used, xs, w_gate, w_up, w_down)


def _combine_body(pstart_ref, rows_ref, chunk_e_ref, chunk_rel_ref, next_e_ref, next_rel_ref, ys_ref,
                  slot_hi_ref, slot_lo_ref, gate_t_ref, lo_col_ref, hi_col_ref,
                  x_ref, h2b_ref, mod_ref, wsg_ref, wsu_ref, wsd_ref, fg_ref,
                  o_ref, loc_ref, acc_ref, sem, *, tl, final):
    i = pl.program_id(0)
    last = pl.num_programs(0) - 1
    buf = i % 2
    rows_used = rows_ref[i]

    def run_copy(b, loc0, slot0, nrows=RUN_ALIGN):
        return pltpu.make_async_copy(ys_ref.at[pl.ds(slot0, nrows), :],
                                     loc_ref.at[b, pl.ds(loc0, nrows), :], sem.at[b])

    @pl.when(i == 0)
    def _():
        loc_ref[...] = jnp.zeros_like(loc_ref)
        _start_run_chunks(pstart_ref, chunk_e_ref, chunk_rel_ref, rows_used,
                          lambda loc0, slot0: run_copy(buf, loc0, slot0))

    @pl.when(i < last)
    def _():
        _start_run_chunks(pstart_ref, next_e_ref, next_rel_ref, rows_ref[jnp.minimum(i + 1, last)],
                          lambda loc0, slot0: run_copy(1 - buf, loc0, slot0))

    hb = h2b_ref[...]
    a = _silu(_dot(hb, wsg_ref[...])) * _dot(hb, wsu_ref[...])
    acc_ref[...] = _dot(a.astype(BF16), wsd_ref[...])
    run_lo = lo_col_ref[0][:, 0:1]
    run_hi = hi_col_ref[0][:, 0:1]

    _wait_run_rows(lambda n: run_copy(buf, 0, 0, n), rows_used)

    def unsort_chunk(c, carry):
        col0 = (c * SLOT_CHUNK).astype(F32)
        col_e = lax.broadcasted_iota(I32, (LANES, SLOT_CHUNK), 1).astype(F32) + col0
        in_run = jnp.where(col_e >= run_lo, jnp.where(col_e < run_hi, 1.0, 0.0), 0.0).astype(BF16)
        slot_of_token = (256.0 * _dot(slot_hi_ref[0], in_run) + _dot(slot_lo_ref[0], in_run))
        gate_of_token = _dot(gate_t_ref[0], in_run)
        col_t = lax.broadcasted_iota(I32, (tl, SLOT_CHUNK), 1).astype(F32) + col0
        gate = jnp.where(slot_of_token == col_t, gate_of_token, 0.0).astype(BF16)
        y = loc_ref[buf, pl.ds(pl.multiple_of(c * SLOT_CHUNK, SLOT_CHUNK), SLOT_CHUNK), :]
        acc_ref[...] = acc_ref[...] + _dot(gate, y)
        return carry

    lax.fori_loop(0, (rows_used + SLOT_CHUNK - 1) // SLOT_CHUNK, unsort_chunk, 0)

    xo = x_ref[...] + mod_ref[0, 5:6, :] * acc_ref[...]
    if final:
        ms = jnp.mean(xo * xo, axis=-1, keepdims=True)
        xo = (xo * lax.rsqrt(ms + EPS)) * fg_ref[...]
    o_ref[...] = xo


def _combine(pstart, rows, chunk_e, chunk_rel, ys, slot_hi_t, slot_lo_t, gate_t, lo_col, hi_col,
             x_flat, h2b, mod_lat, wsg, wsu, wsd, final_g, tokens_per_batch, final):
    t, d = x_flat.shape
    tl = MOE_TILE
    ds = wsg.shape[1]
    tiles_per_batch = tokens_per_batch // tl
    body = functools.partial(_combine_body, tl=tl, final=final)
    n_tiles = t // tl
    table_spec = lambda index: BS((1, 1, CHUNK_TABLE), index, memory_space=pltpu.SMEM)
    this_tile = lambda i, *_: (i, 0, 0)
    next_tile = lambda i, *_: (jnp.minimum(i + 1, n_tiles - 1), 0, 0)
    grid_spec = pltpu.PrefetchScalarGridSpec(
        num_scalar_prefetch=2,
        grid=(n_tiles,),
        in_specs=[table_spec(this_tile), table_spec(this_tile),
                  table_spec(next_tile), table_spec(next_tile),
                  BS(memory_space=pl.ANY),
                  BS((1, tl, LANES), this_tile), BS((1, tl, LANES), this_tile),
                  BS((1, tl, LANES), this_tile),
                  BS((1, LANES, LANES), this_tile), BS((1, LANES, LANES), this_tile),
                  BS((tl, d), lambda i, *_: (i, 0)),
                  BS((tl, d), lambda i, *_: (i, 0)),
                  BS((1, SUBLANES, d), lambda i, *_: (i // tiles_per_batch, 0, 0)),
                  BS((d, ds), lambda i, *_: (0, 0)),
                  BS((d, ds), lambda i, *_: (0, 0)),
                  BS((ds, d), lambda i, *_: (0, 0)),
                  BS((1, d), lambda i, *_: (0, 0))],
        out_specs=BS((tl, d), lambda i, *_: (i, 0)),
        scratch_shapes=[pltpu.VMEM((2, LOCAL_SLOTS, d), BF16), pltpu.VMEM((tl, d), F32),
                        pltpu.SemaphoreType.DMA((2,))],
    )
    return pl.pallas_call(
        body,
        out_shape=SDS((t, d), F32),
        grid_spec=grid_spec,
        compiler_params=_cparams("arbitrary"),
        name="moe_combine",
    )(pstart, rows, chunk_e, chunk_rel, chunk_e, chunk_rel, ys, slot_hi_t, slot_lo_t, gate_t,
      lo_col, hi_col, x_flat, h2b, mod_lat, wsg, wsu, wsd, final_g)


def _moe(x_flat, mod_lat, g, w_router, e_bias, w_gate, w_up, w_down, ws_gate, ws_up, ws_down,
         final_g, tri, tokens_per_batch, layer, final):
    t, d = x_flat.shape
    bm = BM_FFN
    n_tiles = t // MOE_TILE
    (h2b, slot_hi, slot_lo, slot_hi_t, slot_lo_t, gate_t, lo_col, hi_col, bounds_row, chunk_e,
     chunk_rel, rows, seg) = _router(x_flat, mod_lat, g, w_router.T, e_bias.reshape(N_EXPERTS, 1),
                                     tri, tokens_per_batch)
    rows = rows[:, 0, 0]
    seg_rows = seg[:, 0].astype(I32)
    pcnt = (seg_rows + bm - 1) // bm * bm
    pend = jnp.cumsum(pcnt).astype(I32)
    pstart = pend - pcnt
    max_rows = t * TOP_K + n_tiles * N_EXPERTS * (RUN_ALIGN - 1)
    n_blocks = -(-max_rows // bm) + N_EXPERTS
    n_slots = n_blocks * bm
    n_used = pend[-1:] // bm
    xs = _dispatch(pend, pcnt, n_used, pstart, rows, chunk_e, chunk_rel, slot_hi, slot_lo,
                   bounds_row, h2b, n_slots)
    ys = _ffn(pstart // bm, pcnt // bm, n_used, xs, w_gate, w_up, w_down, layer)
    return _combine(pstart, rows, chunk_e, chunk_rel, ys, slot_hi_t, slot_lo_t, gate_t, lo_col,
                    hi_col, x_flat, h2b, mod_lat,
                    ws_gate.astype(BF16), ws_up.astype(BF16), ws_down.astype(BF16),
                    final_g.reshape(1, d), tokens_per_batch, final)


def _inproj_c_body(xp_ref, x_ref, xn_ref, mod_ref, g_ref, w_ref, cw_ref, cb_ref,
                   v_ref, g1_ref, g2_ref, *, tm, n_tiles):
    i = pl.program_id(1)
    halo = SUBLANES
    xe = jnp.concatenate([xp_ref[0], x_ref[0], xn_ref[0]], axis=0)
    h = _norm_mod(xe, g_ref[...], mod_ref[0, 1:2, :], mod_ref[0, 0:1, :])
    row = lax.broadcasted_iota(I32, (tm + 2 * halo, 1), 0)
    outside = jnp.logical_or(jnp.logical_and(i == 0, row < halo),
                             jnp.logical_and(i == n_tiles - 1, row >= tm + halo))
    hb = jnp.where(outside, 0.0, h).astype(BF16)
    width = v_ref.shape[-1]
    for part, o_ref in enumerate((v_ref, g1_ref, g2_ref)):
        cols = slice(part * width, (part + 1) * width)
        zp = _dot(hb, w_ref[:, cols])
        up = pltpu.roll(zp, 1, 0)
        dn = pltpu.roll(zp, tm + 2 * halo - 1, 0)
        z = cw_ref[0:1, cols] * up + cw_ref[1:2, cols] * zp + cw_ref[2:3, cols] * dn + cb_ref[:, cols]
        o_ref[0] = z[halo:halo + tm]


def _inproj_c(x, mod_lat, g, w_in, conv_w, conv_b):
    b, l, d = x.shape
    tm = TM_PROJ
    n_tiles = l // tm
    w3 = w_in.shape[1]
    width = w3 // 3
    r8 = tm // SUBLANES
    body = functools.partial(_inproj_c_body, tm=tm, n_tiles=n_tiles)
    out = SDS((b, l, width), F32)
    return pl.pallas_call(
        body,
        out_shape=(out, out, out),
        grid=(b, n_tiles),
        in_specs=[BS((1, SUBLANES, d), lambda bi, i: (bi, jnp.maximum(i * r8 - 1, 0), 0)),
                  BS((1, tm, d), lambda bi, i: (bi, i, 0)),
                  BS((1, SUBLANES, d), lambda bi, i: (bi, jnp.minimum((i + 1) * r8, l // SUBLANES - 1), 0)),
                  BS((1, SUBLANES, d), lambda bi, i: (bi, 0, 0)),
                  BS((1, d), lambda bi, i: (0, 0)),
                  BS((d, w3), lambda bi, i: (0, 0)),
                  BS((3, w3), lambda bi, i: (0, 0)),
                  BS((1, w3), lambda bi, i: (0, 0))],
        out_specs=(BS((1, tm, width), lambda bi, i: (bi, i, 0)),
                   BS((1, tm, width), lambda bi, i: (bi, i, 0)),
                   BS((1, tm, width), lambda bi, i: (bi, i, 0))),
        compiler_params=_cparams("arbitrary", "arbitrary"),
        name="inproj_c",
    )(x, x, x, mod_lat, g, w_in, conv_w, conv_b)


def _filter_body(f_ref, w1_ref, b1_ref, w2_ref, b2_ref, w3hi_ref, w3lo_ref, fr_ref, dl_ref, keep0_ref,
                 hf_ref, l1_ref, *, tp):
    i = pl.program_id(0)
    feats = f_ref[...]
    fr = fr_ref[...]
    a = jnp.sin(fr * (_dot_hp(feats, w1_ref[...]) + b1_ref[...]))
    a = jnp.sin(fr * (_dot_hp(a, w2_ref[...]) + b2_ref[...]))
    a_hi = a.astype(BF16)
    a_lo = (a - a_hi.astype(F32)).astype(BF16)
    w3_hi = w3hi_ref[...]
    hf = _dot(a_hi, w3_hi) + _dot(a_hi, w3lo_ref[...]) + _dot(a_lo, w3_hi)
    t01 = feats[:, 0:1]
    hf = hf * (jnp.exp(-t01 * jnp.abs(dl_ref[...])) + DECAY_SHIFT)
    row = lax.broadcasted_iota(I32, hf.shape, 0) + i * tp
    hf = jnp.where(row == 0, hf * keep0_ref[...], hf)
    hf_ref[...] = hf

    @pl.when(i == 0)
    def _():
        l1_ref[...] = jnp.zeros_like(l1_ref)

    l1_ref[...] = l1_ref[...] + jnp.sum(jnp.abs(hf), axis=0, keepdims=True)


def _filters(feats, w1, b1, w2, b2, w3, freq, delta, width):
    n, fe = feats.shape
    hid = w2.shape[0]
    fo = w3.shape[1]
    tp = 256
    lag0_keep = jnp.tile(jnp.repeat(jnp.array([1.0, 0.0], F32), width), fo // (2 * width)).reshape(1, fo)
    body = functools.partial(_filter_body, tp=tp)
    full = lambda shape: BS(shape, lambda i: (0, 0))
    w3_hi = w3.astype(BF16)
    w3_lo = (w3 - w3_hi.astype(F32)).astype(BF16)
    return pl.pallas_call(
        body,
        out_shape=(SDS((n, fo), F32), SDS((SUBLANES, fo), F32)),
        grid=(n // tp,),
        in_specs=[BS((tp, fe), lambda i: (i, 0)), full((fe, hid)), full((1, hid)),
                  full((hid, hid)), full((1, hid)), full((hid, fo)), full((hid, fo)), full((1, hid)),
                  full((1, fo)), full((1, fo))],
        out_specs=(BS((tp, fo), lambda i: (i, 0)), BS((SUBLANES, fo), lambda i: (0, 0))),
        compiler_params=_cparams("arbitrary"),
        name="hyena_filters",
    )(feats, w1, b1, w2, b2, w3_hi, w3_lo, freq, delta, lag0_keep)


DFT_R = 128
DFT_VP = 72
DFT_BGROUP = 16
DFT_LANE_TILES = 4


def _dft_tables(n):
    r = DFT_R
    vp = DFT_VP
    m = 2 * n
    na = n // r
    two_pi = 2.0 * np.pi
    live = (np.arange(vp) <= r // 2).astype(np.float64)
    a = np.arange(na)[None, :]
    v = np.arange(vp)[:, None]
    ang1 = two_pi * ((a * v) % r) / r
    f1 = np.concatenate([np.cos(ang1), -np.sin(ang1)], axis=0) * np.tile(live, 2)[:, None]
    b = np.arange(r)[None, None, :]
    u = np.arange(r)[None, :, None]
    vv = np.arange(vp)[:, None, None]
    ang2 = two_pi * ((b * (r * u + vv)) % m) / m
    gr, gi = np.cos(ang2), -np.sin(ang2)
    fwd = np.concatenate([np.concatenate([gr, -gi], axis=2),
                          np.concatenate([gi, gr], axis=2)], axis=1)
    hr, hi = np.transpose(gr, (0, 2, 1)), -np.transpose(gi, (0, 2, 1))
    inv = np.concatenate([np.concatenate([hr, -hi], axis=2),
                          np.concatenate([hi, hr], axis=2)], axis=1)
    weight = live * np.where((np.arange(vp) == 0) | (np.arange(vp) == r // 2), 1.0, 2.0)
    ang3 = two_pi * ((np.arange(na)[:, None] * np.arange(vp)[None, :]) % r) / r
    f3 = np.concatenate([np.cos(ang3) * weight, -np.sin(ang3) * weight], axis=1) / m
    cast = lambda t: jnp.asarray(t.astype(np.float32)).astype(BF16)
    return cast(f1), cast(fwd), cast(inv), cast(f3)


def _lane_tile_specs(rows, index_map_of_tile):
    return [BS((None, rows, DFT_BGROUP, LANES), index_map_of_tile(t)) for t in range(DFT_LANE_TILES)]


def _rows_of_position(ref, j):
    x, bg, _ = ref.shape
    return ref.reshape(x * bg, LANES)[pl.ds(j, x, stride=bg), :]


def _dft_s1_body(*refs):
    q = DFT_LANE_TILES
    y_refs, f1_ref, ar_ref, ai_ref = refs[:q], refs[q], refs[q + 1], refs[q + 2]
    f1 = f1_ref[...]
    for t in range(q):
        lanes = slice(t * LANES, (t + 1) * LANES)
        for j in range(DFT_BGROUP):
            res = _dot(f1, _rows_of_position(y_refs[t], j).astype(BF16))
            ar_ref[:, j, lanes] = res[:DFT_VP]
            ai_ref[:, j, lanes] = res[DFT_VP:]


def _dft_s1(y4, f1):
    nb, na, r, c = y4.shape
    q, bg = DFT_LANE_TILES, DFT_BGROUP
    out = SDS((nb, DFT_VP, r, c), F32)
    plane_spec = BS((None, DFT_VP, bg, q * LANES), lambda i, j, cc: (i, 0, j, cc))
    tile_map = lambda t: (lambda i, j, cc: (i, 0, j, cc * q + t))
    return pl.pallas_call(
        _dft_s1_body,
        out_shape=(out, out),
        grid=(nb, r // bg, c // (q * LANES)),
        in_specs=_lane_tile_specs(na, tile_map) + [BS((2 * DFT_VP, na), lambda i, j, cc: (0, 0))],
        out_specs=(plane_spec, plane_spec),
        compiler_params=_cparams("arbitrary", "arbitrary", "arbitrary"),
        name="dft_stage1",
    )(*([y4] * q), f1)


def _filter_spec_body(arf_ref, aif_ref, arb_ref, aib_ref, g_ref, l1f_ref, l1b_ref, kr_ref, ki_ref):
    g = g_ref[...]
    yf = _dot(g, jnp.concatenate([arf_ref[...], aif_ref[...]], axis=0).astype(BF16))
    yb = _dot(g, jnp.concatenate([arb_ref[...], aib_ref[...]], axis=0).astype(BF16))
    inv = 1.0 / (l1f_ref[0:1, :] + l1b_ref[0:1, :])
    kr_ref[...] = (yf[:DFT_R] + yb[:DFT_R]) * inv
    ki_ref[...] = (yf[DFT_R:] - yb[DFT_R:]) * inv


def _filter_spectrum(ar, ai, fwd, l1, width):
    r = DFT_R
    a_spec = lambda d: BS((None, None, r, width), lambda v, o: (0, v, 0, 2 * o + d))
    l_spec = lambda d: BS((SUBLANES, width), lambda v, o: (0, 2 * o + d))
    out = SDS((DFT_VP, r, HYENA_ORDER * width), F32)
    return pl.pallas_call(
        _filter_spec_body,
        out_shape=(out, out),
        grid=(DFT_VP, HYENA_ORDER),
        in_specs=[a_spec(0), a_spec(0), a_spec(1), a_spec(1),
                  BS((None, 2 * r, 2 * r), lambda v, o: (v, 0, 0)), l_spec(0), l_spec(1)],
        out_specs=(BS((None, r, width), lambda v, o: (v, 0, o)),
                   BS((None, r, width), lambda v, o: (v, 0, o))),
        compiler_params=_cparams("arbitrary", "arbitrary"),
        name="hyena_filter_spectrum",
    )(ar, ai, ar, ai, fwd, l1, l1)


def _conv_mid_body(ar_ref, ai_ref, g_ref, h_ref, kr_ref, ki_ref, qr_ref, qi_ref):
    y = _dot(g_ref[...], jnp.concatenate([ar_ref[...], ai_ref[...]], axis=0).astype(BF16))
    yr, yi = y[:DFT_R], y[DFT_R:]
    kr, ki = kr_ref[...], ki_ref[...]
    p = jnp.concatenate([yr * kr - yi * ki, yr * ki + yi * kr], axis=0).astype(BF16)
    q = _dot(h_ref[...], p)
    qr_ref[...] = q[:DFT_R]
    qi_ref[...] = q[DFT_R:]


def _conv_mid(ar, ai, fwd, inv, kr, ki, order):
    nb, vp, r, c = ar.shape
    a_spec = BS((None, None, r, c), lambda n, v: (n, v, 0, 0))
    m_spec = BS((None, 2 * r, 2 * r), lambda n, v: (v, 0, 0))
    k_spec = BS((None, r, c), lambda n, v: (v, 0, order))
    out = SDS((nb, vp, r, c), F32)
    return pl.pallas_call(
        _conv_mid_body,
        out_shape=(out, out),
        grid=(nb, vp),
        in_specs=[a_spec, a_spec, m_spec, m_spec, k_spec, k_spec],
        out_specs=(a_spec, a_spec),
        compiler_params=_cparams("arbitrary", "arbitrary"),
        name="hyena_spectral_product",
    )(ar, ai, fwd, inv, kr, ki)


def _idft_gate_body(*refs):
    q = DFT_LANE_TILES
    qr_refs, qi_refs, y_refs, gate_refs = (refs[k * q:(k + 1) * q] for k in range(4))
    f3_ref, fb_ref, o_ref = refs[4 * q:]
    f3 = f3_ref[...]
    for t in range(q):
        lanes = slice(t * LANES, (t + 1) * LANES)
        fb = fb_ref[:, lanes]
        for j in range(DFT_BGROUP):
            planes = jnp.concatenate([_rows_of_position(qr_refs[t], j),
                                      _rows_of_position(qi_refs[t], j)], axis=0).astype(BF16)
            conv = _dot(f3, planes)
            o_ref[:, j, lanes] = _rows_of_position(gate_refs[t], j) * (
                conv + fb * _rows_of_position(y_refs[t], j))


def _idft_gate(qr, qi, f3, y4, gate4, fbias):
    nb, na, r, c = y4.shape
    q, bg = DFT_LANE_TILES, DFT_BGROUP
    tile_map = lambda t: (lambda i, j, cc: (i, 0, j, cc * q + t))
    wide = lambda rows: BS((None, rows, bg, q * LANES), lambda i, j, cc: (i, 0, j, cc))
    return pl.pallas_call(
        _idft_gate_body,
        out_shape=SDS((nb, na, r, c), F32),
        grid=(nb, r // bg, c // (q * LANES)),
        in_specs=(_lane_tile_specs(DFT_VP, tile_map) + _lane_tile_specs(DFT_VP, tile_map)
                  + _lane_tile_specs(na, tile_map) + _lane_tile_specs(na, tile_map)
                  + [BS((na, 2 * DFT_VP), lambda i, j, cc: (0, 0)),
                     BS((1, q * LANES), lambda i, j, cc: (0, cc))]),
        out_specs=wide(na),
        compiler_params=_cparams("arbitrary", "arbitrary", "arbitrary"),
        name="hyena_idft_gate",
    )(*([qr] * q + [qi] * q + [y4] * q + [gate4] * q), f3, fbias)


def _outproj_body(y_ref, w_ref, x_ref, mod_ref, o_ref):
    o_ref[0] = x_ref[0] + mod_ref[0, 2:3, :] * _dot(y_ref[0].astype(BF16), w_ref[...])


def _outproj(y, w_out, x, mod_lat):
    b, l, d = x.shape
    tm = TM_PROJ
    wdt = y.shape[-1]
    return pl.pallas_call(
        _outproj_body,
        out_shape=SDS((b, l, d), F32),
        grid=(b, l // tm),
        in_specs=[BS((1, tm, wdt), lambda bi, i: (bi, i, 0)),
                  BS((wdt, d), lambda bi, i: (0, 0)),
                  BS((1, tm, d), lambda bi, i: (bi, i, 0)),
                  BS((1, SUBLANES, d), lambda bi, i: (bi, 0, 0))],
        out_specs=BS((1, tm, d), lambda bi, i: (bi, i, 0)),
        compiler_params=_cparams("arbitrary", "arbitrary"),
        name="outproj_c",
    )(y, w_out, x, mod_lat)


def _hyena(x, mod_lat, g, w_in, conv_w, conv_b, w1, b1, w2, b2, w3, freq, delta, f_bias, w_out):
    b, n, d = x.shape
    width = w_out.shape[0]
    r = DFT_R
    na = n // r
    f1, fwd, inv, f3 = _dft_tables(n)
    v, gate1, gate2 = _inproj_c(x, mod_lat, g, w_in.astype(BF16), conv_w, conv_b.reshape(1, -1))

    t = jnp.arange(n, dtype=F32)
    t01 = t / max(n - 1, 1)
    bands = jnp.linspace(1e-4, FILT_BANDS - 1, FILT_BANDS, dtype=F32)
    ang = (2.0 * math.pi / n) * t[:, None] * bands[None, :]
    feats = jnp.concatenate([t01[:, None], jnp.cos(ang), jnp.sin(ang)], axis=-1)
    fe = feats.shape[1]
    feats = jnp.pad(feats, ((0, 0), (0, LANES - fe)))
    w1p = jnp.pad(w1, ((0, LANES - fe), (0, 0)))
    hf, l1 = _filters(feats, w1p, b1.reshape(1, -1), w2, b2.reshape(1, -1), w3,
                      freq.reshape(1, -1), delta.reshape(1, -1), width)
    far, fai = _dft_s1(hf.reshape(1, na, r, hf.shape[1]), f1)
    kr, ki = _filter_spectrum(far, fai, fwd, l1, width)

    y4 = v.reshape(b, na, r, width)
    for o, gate in enumerate((gate1, gate2)):
        ar, ai = _dft_s1(y4, f1)
        qr, qi = _conv_mid(ar, ai, fwd, inv, kr, ki, o)
        y4 = _idft_gate(qr, qi, f3, y4, gate.reshape(b, na, r, width), f_bias[o].reshape(1, width))
    return _outproj(y4.reshape(b, n, width), w_out.astype(BF16), x, mod_lat)


def _rope_tables(seq_len):
    rows = seq_len // GRID_W
    row = jnp.repeat(jnp.arange(rows, dtype=F32), GRID_W)
    col = jnp.tile(jnp.arange(GRID_W, dtype=F32), rows)
    inv = jnp.power(ROPE_BASE, -jnp.arange(ROPE_FREQS, dtype=F32) / ROPE_FREQS)
    ar, ac = row[:, None] * inv, col[:, None] * inv
    cos_h = jnp.concatenate([jnp.cos(ar), jnp.cos(ar), jnp.cos(ac), jnp.cos(ac)], axis=1)
    sin_h = jnp.concatenate([-jnp.sin(ar), jnp.sin(ar), -jnp.sin(ac), jnp.sin(ac)], axis=1)
    reps = LANES // HEAD_DIM
    return jnp.tile(cos_h, (1, reps)), jnp.tile(sin_h, (1, reps))


def _rotate_partner_columns(w):
    ncol = w.shape[1]
    lane = np.arange(ncol)
    partner = np.where((lane % (2 * ROPE_FREQS)) < ROPE_FREQS, lane + ROPE_FREQS, lane - ROPE_FREQS)
    return w[:, partner]


def kernel(x, c, ctx, c_ctx, w_mod, b_mod, norm_g, w_in_ab, sink, w_spatial, b_spatial, w_out_ab,
           w_in_c, conv_w, conv_b, filt_w1, filt_b1, filt_w2, filt_b2, filt_w3, filt_freq,
           filt_delta, filt_bias, w_out_c, w_router, e_bias, w_gate, w_up, w_down, ws_gate,
           ws_up, ws_down, final_g):
    b, l, d = x.shape
    depth = w_mod.shape[0]
    assert depth == 2 and b + 1 <= SUBLANES

    cc = jnp.zeros((SUBLANES, d), F32).at[:b].set(c).at[b].set(c_ctx)
    m_all = _mod_vectors(cc, w_mod, b_mod)

    def mod_rows(layer, row0, nrow):
        m = m_all[layer, row0:row0 + nrow].reshape(nrow, 6, d)
        return jnp.pad(m, ((0, 0), (0, SUBLANES - 6), (0, 0)))

    tri = jnp.triu(jnp.ones((MOE_TILE, MOE_TILE), F32), k=1).astype(BF16)

    mod_lat = mod_rows(0, 0, b)
    mod_ctx = mod_rows(0, b, 1)[0]
    w_in = w_in_ab[0]
    qk = ATTN_WIDTH + KV_WIDTH
    w_cat = jnp.concatenate([w_in, _rotate_partner_columns(w_in[:, :qk])], axis=1).astype(BF16)
    cos_t, sin_t = _rope_tables(l)
    group_avg = jnp.kron(jnp.eye(N_SG_GROUPS, dtype=F32),
                         jnp.full((SG_GROUP_DIM, SG_GROUP_DIM), 1.0 / SG_GROUP_DIM, F32)).astype(BF16)
    kc, vc = _ctx_kv(ctx, mod_ctx, norm_g[0, 0].reshape(1, d),
                     w_in[:, ATTN_WIDTH:ATTN_WIDTH + 2 * KV_WIDTH].astype(BF16))
    q, k, v, ug, vn = _inproj_ab(x, mod_lat, norm_g[0, 0].reshape(1, d), w_cat, cos_t, sin_t, group_avg)
    b_full = jnp.repeat(b_spatial[0].T, SG_GROUP_DIM, axis=1)
    x1 = _mixer(sink[0], q, k, v, kc, vc, ug, vn, w_spatial[0].astype(BF16), b_full,
                w_out_ab[0].astype(BF16), x, mod_lat)
    x2 = _moe(x1.reshape(b * l, d), mod_lat, norm_g[0, 1].reshape(1, d), w_router[0], e_bias[0],
              w_gate, w_up, w_down, ws_gate[0], ws_up[0], ws_down[0], final_g, tri, l,
              layer=0, final=False).reshape(b, l, d)

    mod_lat = mod_rows(1, 0, b)
    x3 = _hyena(x2, mod_lat, norm_g[1, 0].reshape(1, d), w_in_c[0], conv_w[0], conv_b[0],
                filt_w1[0], filt_b1[0], filt_w2[0], filt_b2[0], filt_w3[0], filt_freq[0],
                filt_delta[0], filt_bias[0], w_out_c[0])
    out = _moe(x3.reshape(b * l, d), mod_lat, norm_g[1, 1].reshape(1, d), w_router[1], e_bias[1],
               w_gate, w_up, w_down, ws_gate[1], ws_up[1], ws_down[1], final_g, tri, l,
               layer=1, final=True)
    return out.reshape(b, l, d)
```

```python
import functools
import math

import numpy as np
import jax
import jax.numpy as jnp
from jax import lax
from jax.experimental import pallas as pl
from jax.experimental.pallas import tpu as pltpu

F32 = jnp.float32
BF16 = jnp.bfloat16
I32 = jnp.int32
HIGHEST = lax.Precision.HIGHEST
SDS = jax.ShapeDtypeStruct
BS = pl.BlockSpec

EPS = 1e-6
NEG = -1e30

GRID_W = 64
N_Q_HEADS = 8
N_KV_HEADS = 2
HEAD_DIM = 64
ATTN_WIDTH = N_Q_HEADS * HEAD_DIM
KV_WIDTH = N_KV_HEADS * HEAD_DIM
WINDOW = 128
BLOCK = 128
ROPE_BASE = 10000.0
ROPE_FREQS = HEAD_DIM // 4
N_SG_GROUPS = 8
SG_GROUP_DIM = 64
SG_WIDTH = N_SG_GROUPS * SG_GROUP_DIM
HYENA_ORDER = 2
FILT_BANDS = 16
DECAY_SHIFT = 0.05
N_EXPERTS = 64
TOP_K = 8
N_GROUPS = 8
TOPK_GROUPS = 4
ROUTED_SCALE = 2.5

LANES = 128
SUBLANES = 8
VMEM_LIMIT = 56 * 1024 * 1024

TM_PROJ = 512
TQ_MIX = 256
MOE_TILE = 256
BM_FFN = 512
RUN_ALIGN = 16
SLOT_CHUNK = 512
NO_SLOT = 256 * 256 - 1
LOCAL_SLOTS = -(-(TOP_K * MOE_TILE + N_EXPERTS * (RUN_ALIGN - 1)) // SLOT_CHUNK) * SLOT_CHUNK
CHUNK_TABLE = -(-(LOCAL_SLOTS // RUN_ALIGN) // LANES) * LANES
FFN_IN_BUFS = 4


def _cparams(*sem):
    return pltpu.CompilerParams(dimension_semantics=sem, vmem_limit_bytes=VMEM_LIMIT)


def _dot(a, b):
    return jnp.dot(a, b, preferred_element_type=F32)


def _dot_nt(a, b):
    return lax.dot_general(a, b, (((1,), (1,)), ((), ())), preferred_element_type=F32)


def _dot_hp(a, b):
    return jnp.dot(a, b, preferred_element_type=F32, precision=HIGHEST)


def _norm_mod(x, g, sc, sh):
    ms = jnp.mean(x * x, axis=-1, keepdims=True)
    y = x * lax.rsqrt(ms + EPS)
    return (y * g) * (1.0 + sc) + sh


def _gelu_tanh(x):
    c = math.sqrt(2.0 / math.pi)
    return 0.5 * x * (1.0 + jnp.tanh(c * (x + 0.044715 * (x * x * x))))


def _silu(x):
    return x * jax.nn.sigmoid(x)


def _mod_body(c_ref, w_ref, b_ref, o_ref):
    o_ref[0] = _dot_hp(_silu(c_ref[...]), w_ref[0]) + b_ref[0]


def _mod_vectors(cc, w_mod, b_mod):
    depth, d, n = w_mod.shape
    tn = 1536
    return pl.pallas_call(
        _mod_body,
        out_shape=SDS((depth, SUBLANES, n), F32),
        grid=(depth, n // tn),
        in_specs=[BS((SUBLANES, d), lambda l, j: (0, 0)),
                  BS((1, d, tn), lambda l, j: (l, 0, j)),
                  BS((1, 1, tn), lambda l, j: (l, 0, j))],
        out_specs=BS((1, SUBLANES, tn), lambda l, j: (l, 0, j)),
        compiler_params=_cparams("arbitrary", "arbitrary"),
        name="mod_vectors",
    )(cc, w_mod, b_mod.reshape(depth, 1, n))


def _ctx_kv_body(ctx_ref, mod_ref, g_ref, w_ref, kc_ref, vc_ref):
    h = _norm_mod(ctx_ref[0], g_ref[...], mod_ref[1:2, :], mod_ref[0:1, :])
    z = _dot(h.astype(BF16), w_ref[...])
    kc_ref[0] = z[:, :KV_WIDTH].astype(BF16)
    vc_ref[0] = z[:, KV_WIDTH:].astype(BF16)


def _ctx_kv(ctx, mod_ctx, g, w_kv):
    b, c, d = ctx.shape
    return pl.pallas_call(
        _ctx_kv_body,
        out_shape=(SDS((b, c, KV_WIDTH), BF16), SDS((b, c, KV_WIDTH), BF16)),
        grid=(b,),
        in_specs=[BS((1, c, d), lambda i: (i, 0, 0)),
                  BS((SUBLANES, d), lambda i: (0, 0)),
                  BS((1, d), lambda i: (0, 0)),
                  BS((d, 2 * KV_WIDTH), lambda i: (0, 0))],
        out_specs=(BS((1, c, KV_WIDTH), lambda i: (i, 0, 0)),
                   BS((1, c, KV_WIDTH), lambda i: (i, 0, 0))),
        compiler_params=_cparams("arbitrary"),
        name="ctx_kv",
    )(ctx, mod_ctx, g, w_kv)


def _inproj_ab_body(x_ref, mod_ref, g_ref, w_ref, cos_ref, sin_ref, avg_ref,
                    q_ref, k_ref, v_ref, ug_ref, vn_ref):
    h = _norm_mod(x_ref[0], g_ref[...], mod_ref[0, 1:2, :], mod_ref[0, 0:1, :]).astype(BF16)
    cs = cos_ref[...]
    sn = sin_ref[...]
    rot0 = ATTN_WIDTH + 2 * KV_WIDTH + 2 * SG_WIDTH
    scale = HEAD_DIM ** -0.5
    for j in range(ATTN_WIDTH // LANES):
        z = _dot(h, w_ref[:, j * LANES:(j + 1) * LANES])
        zr = _dot(h, w_ref[:, rot0 + j * LANES:rot0 + (j + 1) * LANES])
        q_ref[0, :, j * LANES:(j + 1) * LANES] = ((z * cs + zr * sn) * scale).astype(BF16)
    zk = _dot(h, w_ref[:, ATTN_WIDTH:ATTN_WIDTH + KV_WIDTH])
    zkr = _dot(h, w_ref[:, rot0 + ATTN_WIDTH:rot0 + ATTN_WIDTH + KV_WIDTH])
    k_ref[0] = (zk * cs + zkr * sn).astype(BF16)
    v_ref[0] = _dot(h, w_ref[:, ATTN_WIDTH + KV_WIDTH:ATTN_WIDTH + 2 * KV_WIDTH]).astype(BF16)
    u0 = ATTN_WIDTH + 2 * KV_WIDTH
    ug_ref[0] = _gelu_tanh(_dot(h, w_ref[:, u0:u0 + SG_WIDTH]))
    vf = _gelu_tanh(_dot(h, w_ref[:, u0 + SG_WIDTH:u0 + 2 * SG_WIDTH]))
    avg = avg_ref[...]

    def gmean(t):
        hi = t.astype(BF16)
        lo = (t - hi.astype(F32)).astype(BF16)
        return _dot(hi, avg) + _dot(lo, avg)

    vc = vf - gmean(vf)
    vn_ref[0] = (vc * lax.rsqrt(gmean(vc * vc) + EPS)).astype(BF16)


def _inproj_ab(x, mod_lat, g, w_cat, cos_t, sin_t, avg):
    b, l, d = x.shape
    tm = TM_PROJ
    ncol = w_cat.shape[1]
    return pl.pallas_call(
        _inproj_ab_body,
        out_shape=(SDS((b, l, ATTN_WIDTH), BF16), SDS((b, l, KV_WIDTH), BF16),
                   SDS((b, l, KV_WIDTH), BF16), SDS((b, l, SG_WIDTH), F32),
                   SDS((b, l, SG_WIDTH), BF16)),
        grid=(b, l // tm),
        in_specs=[BS((1, tm, d), lambda bi, i: (bi, i, 0)),
                  BS((1, SUBLANES, d), lambda bi, i: (bi, 0, 0)),
                  BS((1, d), lambda bi, i: (0, 0)),
                  BS((d, ncol), lambda bi, i: (0, 0)),
                  BS((tm, LANES), lambda bi, i: (i, 0)),
                  BS((tm, LANES), lambda bi, i: (i, 0)),
                  BS((SG_WIDTH, SG_WIDTH), lambda bi, i: (0, 0))],
        out_specs=(BS((1, tm, ATTN_WIDTH), lambda bi, i: (bi, i, 0)),
                   BS((1, tm, KV_WIDTH), lambda bi, i: (bi, i, 0)),
                   BS((1, tm, KV_WIDTH), lambda bi, i: (bi, i, 0)),
                   BS((1, tm, SG_WIDTH), lambda bi, i: (bi, i, 0)),
                   BS((1, tm, SG_WIDTH), lambda bi, i: (bi, i, 0))),
        compiler_params=_cparams("arbitrary", "arbitrary"),
        name="inproj_ab",
    )(x, mod_lat, g, w_cat, cos_t, sin_t, avg)


def _mixer_body(sink_ref, q_ref, kp_ref, kcur_ref, kn_ref, vp_ref, vcur_ref, vn_ref,
                kc_ref, vc_ref, ug_ref, vnorm_ref, ws_ref, bs_ref, wout_ref, x_ref, mod_ref,
                o_ref, cat_ref, *, seq_len, sub_blocks):
    i = pl.program_id(1)
    kk = jnp.concatenate([kp_ref[0], kcur_ref[0], kn_ref[0]], axis=0)
    vv = jnp.concatenate([vp_ref[0], vcur_ref[0], vn_ref[0]], axis=0)
    kc = kc_ref[0]
    vc = vc_ref[0]
    span = 3 * BLOCK
    ii = lax.broadcasted_iota(I32, (BLOCK, span), 0)
    jj = lax.broadcasted_iota(I32, (BLOCK, span), 1)
    dd = jj - ii
    in_window = jnp.where(dd >= 0, jnp.where(dd <= 2 * WINDOW, 1, 0), 0)
    group = N_Q_HEADS // N_KV_HEADS
    for r in range(sub_blocks):
        rows = slice(r * BLOCK, (r + 1) * BLOCK)
        kpos = (i * sub_blocks + r - 1) * BLOCK + jj
        in_seq = jnp.where(kpos >= 0, jnp.where(kpos < seq_len, 1, 0), 0)
        bias = jnp.where(in_window * in_seq > 0, 0.0, NEG)
        qb = q_ref[0, rows, :]
        kl = kk[r * BLOCK:r * BLOCK + span]
        vl = vv[r * BLOCK:r * BLOCK + span]
        for hq in range(N_Q_HEADS):
            hk = hq // group
            ks = slice(hk * HEAD_DIM, (hk + 1) * HEAD_DIM)
            qh = qb[:, hq * HEAD_DIM:(hq + 1) * HEAD_DIM]
            s_loc = _dot_nt(qh, kl[:, ks]) + bias
            s_ctx = _dot_nt(qh, kc[:, ks])
            sk = sink_ref[hq]
            m = jnp.maximum(jnp.maximum(jnp.max(s_loc, axis=-1, keepdims=True),
                                        jnp.max(s_ctx, axis=-1, keepdims=True)), sk)
            p_loc = jnp.exp(s_loc - m)
            p_ctx = jnp.exp(s_ctx - m)
            den = (jnp.sum(p_loc, axis=-1, keepdims=True) + jnp.sum(p_ctx, axis=-1, keepdims=True)
                   + jnp.exp(sk - m))
            o = _dot(p_loc.astype(BF16), vl[:, ks]) + _dot(p_ctx.astype(BF16), vc[:, ks])
            cat_ref[rows, hq * HEAD_DIM:(hq + 1) * HEAD_DIM] = (o / den).astype(BF16)
        vnb = vnorm_ref[0, rows, :]
        ugb = ug_ref[0, rows, :]
        for g in range(N_SG_GROUPS):
            gs = slice(g * SG_GROUP_DIM, (g + 1) * SG_GROUP_DIM)
            sg = _dot(ws_ref[g], vnb[:, gs]) + bs_ref[:, gs]
            cat_ref[rows, ATTN_WIDTH + g * SG_GROUP_DIM:ATTN_WIDTH + (g + 1) * SG_GROUP_DIM] = (
                ugb[:, gs] * sg).astype(BF16)
    y = _dot(cat_ref[...], wout_ref[...])
    o_ref[0] = x_ref[0] + mod_ref[0, 2:3, :] * y


def _mixer(sink, q, k, v, kc, vc, ug, vn, w_s, b_full, w_out, x, mod_lat):
    b, l, d = x.shape
    tq = TQ_MIX
    r = tq // BLOCK
    nb = l // BLOCK
    c = kc.shape[1]
    prev_map = lambda bi, i: (bi, jnp.maximum(i * r - 1, 0), 0)
    next_map = lambda bi, i: (bi, jnp.minimum((i + 1) * r, nb - 1), 0)
    cur_map = lambda bi, i: (bi, i, 0)
    body = functools.partial(_mixer_body, seq_len=l, sub_blocks=r)
    return pl.pallas_call(
        body,
        out_shape=SDS((b, l, d), F32),
        grid=(b, l // tq),
        in_specs=[BS(memory_space=pltpu.SMEM),
                  BS((1, tq, ATTN_WIDTH), cur_map),
                  BS((1, BLOCK, KV_WIDTH), prev_map), BS((1, tq, KV_WIDTH), cur_map),
                  BS((1, BLOCK, KV_WIDTH), next_map),
                  BS((1, BLOCK, KV_WIDTH), prev_map), BS((1, tq, KV_WIDTH), cur_map),
                  BS((1, BLOCK, KV_WIDTH), next_map),
                  BS((1, c, KV_WIDTH), lambda bi, i: (bi, 0, 0)),
                  BS((1, c, KV_WIDTH), lambda bi, i: (bi, 0, 0)),
                  BS((1, tq, SG_WIDTH), cur_map), BS((1, tq, SG_WIDTH), cur_map),
                  BS((N_SG_GROUPS, BLOCK, BLOCK), lambda bi, i: (0, 0, 0)),
                  BS((BLOCK, SG_WIDTH), lambda bi, i: (0, 0)),
                  BS((d, d), lambda bi, i: (0, 0)),
                  BS((1, tq, d), cur_map),
                  BS((1, SUBLANES, d), lambda bi, i: (bi, 0, 0))],
        out_specs=BS((1, tq, d), cur_map),
        scratch_shapes=[pltpu.VMEM((tq, d), BF16)],
        compiler_params=_cparams("arbitrary", "arbitrary"),
        name="mixer_ab",
    )(sink, q, k, k, k, v, v, v, kc, vc, ug, vn, w_s, b_full, w_out, x, mod_lat)


def _router_body(x_ref, mod_ref, g_ref, wr_ref, eb_ref, tri_ref,
                 h2b_ref, slot_hi_ref, slot_lo_ref, slot_hi_t_ref, slot_lo_t_ref, gate_t_ref,
                 lo_col_ref, hi_col_ref, bounds_row_ref, chunk_e_ref, chunk_rel_ref, rows_ref,
                 cnt_ref, carry_ref, *, tm):
    i = pl.program_id(0)

    @pl.when(i == 0)
    def _():
        carry_ref[...] = jnp.zeros_like(carry_ref)

    h2 = _norm_mod(x_ref[...], g_ref[...], mod_ref[0, 4:5, :], mod_ref[0, 3:4, :])
    h2b_ref[...] = h2.astype(BF16)

    logits = lax.dot_general(wr_ref[...], h2, (((1,), (1,)), ((), ())),
                             preferred_element_type=F32, precision=HIGHEST)
    scores = jax.nn.sigmoid(logits)
    per_group = N_EXPERTS // N_GROUPS
    shape3 = (N_GROUPS, per_group, tm)
    s3 = scores.reshape(shape3)
    b3 = (scores + eb_ref[...]).reshape(shape3)
    sub = lax.broadcasted_iota(I32, shape3, 1)
    eid = lax.broadcasted_iota(I32, shape3, 0) * per_group + sub

    m1 = jnp.max(b3, axis=1, keepdims=True)
    i1 = jnp.min(jnp.where(b3 == m1, sub, per_group), axis=1, keepdims=True)
    m2 = jnp.max(jnp.where(sub == i1, -jnp.inf, b3), axis=1, keepdims=True)
    gs = m1 + m2
    keep = []
    for g in range(N_GROUPS):
        beaten = jnp.zeros((1, tm), I32)
        for g2 in range(N_GROUPS):
            if g2 == g:
                continue
            wins = (gs[g2] >= gs[g]) if g2 < g else (gs[g2] > gs[g])
            beaten = beaten + jnp.where(wins, 1, 0)
        keep.append(jnp.where(beaten < TOPK_GROUPS, 1, 0)[None])
    keep3 = jnp.concatenate(keep, axis=0)
    val = jnp.where(keep3 > 0, b3, -jnp.inf)

    def red(fn, a):
        return fn(fn(a, axis=0, keepdims=True), axis=1, keepdims=True)

    idxs, ws = [], []
    member = jnp.zeros(shape3, F32)
    for _ in range(TOP_K):
        m = red(jnp.max, val)
        idx = red(jnp.min, jnp.where(val == m, eid, N_EXPERTS))
        hit = eid == idx
        ws.append(red(jnp.sum, jnp.where(hit, s3, 0.0)))
        val = jnp.where(hit, -jnp.inf, val)
        member = member + jnp.where(hit, 1.0, 0.0)
        idxs.append(idx)
    wsum = ws[0]
    for w in ws[1:]:
        wsum = wsum + w

    member2 = member.reshape(N_EXPERTS, tm)
    cnt = jnp.sum(member2, axis=1, keepdims=True)
    runlen = jnp.floor((cnt + (RUN_ALIGN - 1)) * (1.0 / RUN_ALIGN)) * RUN_ALIGN
    runlen_b = jnp.broadcast_to(runlen, (N_EXPERTS, LANES))
    e_row = lax.broadcasted_iota(I32, (N_EXPERTS, N_EXPERTS), 0)
    e_col = lax.broadcasted_iota(I32, (N_EXPERTS, N_EXPERTS), 1)
    earlier = jnp.where(e_col < e_row, 1.0, 0.0).astype(BF16)
    loff = _dot(earlier, runlen_b.astype(BF16))
    slot = _dot(member2.astype(BF16), tri_ref[...]) + loff[:, 0:1]
    slot = jnp.where(member2 > 0.0, slot, float(NO_SLOT))
    slot_hi = jnp.floor(slot * (1.0 / 256.0))
    slot_lo = slot - 256.0 * slot_hi
    gate3 = jnp.zeros(shape3, F32)
    for k in range(TOP_K):
        gate3 = jnp.where(eid == idxs[k], ws[k] / wsum * ROUTED_SCALE, gate3)
    gate = gate3.reshape(N_EXPERTS, tm)
    no_expert = jnp.zeros((LANES - N_EXPERTS, tm), F32)
    pad_e = lambda a: jnp.concatenate([a, no_expert], axis=0)
    slot_hi_ref[0] = pad_e(slot_hi).astype(BF16)
    slot_lo_ref[0] = pad_e(slot_lo).astype(BF16)
    slot_hi_t_ref[0] = pad_e(slot_hi).T.astype(BF16)
    slot_lo_t_ref[0] = pad_e(slot_lo).T.astype(BF16)
    gate_t_ref[0] = pad_e(gate).T.astype(BF16)
    run_lo = loff
    run_hi = loff + runlen_b
    no_run = jnp.zeros((LANES - N_EXPERTS, LANES), F32)
    lo_col_ref[0] = jnp.concatenate([run_lo, no_run], axis=0)
    hi_col_ref[0] = jnp.concatenate([run_hi, no_run], axis=0)
    diag = (lax.broadcasted_iota(I32, (N_EXPERTS, LANES), 0)
            == lax.broadcasted_iota(I32, (N_EXPERTS, LANES), 1))
    bounds_row_ref[0] = jnp.concatenate(
        [jnp.sum(jnp.where(diag, run_lo, 0.0), axis=0, keepdims=True),
         jnp.sum(jnp.where(diag, run_hi, 0.0), axis=0, keepdims=True),
         jnp.zeros((SUBLANES - 2, LANES), F32)], axis=0)
    chunk_hi = (loff[:, 0:1] + runlen) * (1.0 / RUN_ALIGN)
    j = lax.broadcasted_iota(I32, (N_EXPERTS, CHUNK_TABLE), 1).astype(F32)
    e_of_chunk = jnp.sum(jnp.where(chunk_hi <= j, 1.0, 0.0), axis=0, keepdims=True)
    e_iota = lax.broadcasted_iota(I32, (N_EXPERTS, CHUNK_TABLE), 0).astype(F32)
    seg_off = carry_ref[:, 0:1] - loff[:, 0:1]
    rel = jnp.sum(jnp.where(e_iota == e_of_chunk, seg_off, 0.0), axis=0, keepdims=True)
    chunk_e_ref[0] = jnp.minimum(e_of_chunk, N_EXPERTS - 1.0).astype(I32)
    chunk_rel_ref[0] = (rel + j[0:1, :] * RUN_ALIGN).astype(I32)
    rows_ref[0] = (loff[N_EXPERTS - 1:N_EXPERTS, :] + runlen_b[N_EXPERTS - 1:N_EXPERTS, :]).astype(I32)
    total = carry_ref[...] + runlen_b
    carry_ref[...] = total
    cnt_ref[...] = total


def _router(x_flat, mod_lat, g, wr_t, e_bias, tri, tokens_per_batch):
    t, d = x_flat.shape
    tm = MOE_TILE
    n_tiles = t // tm
    tiles_per_batch = tokens_per_batch // tm
    body = functools.partial(_router_body, tm=tm)
    table = SDS((n_tiles, 1, CHUNK_TABLE), I32)
    table_spec = BS((1, 1, CHUNK_TABLE), lambda i: (i, 0, 0))
    expert_major = SDS((n_tiles, LANES, tm), BF16)
    expert_major_spec = BS((1, LANES, tm), lambda i: (i, 0, 0))
    token_major = SDS((n_tiles, tm, LANES), BF16)
    token_major_spec = BS((1, tm, LANES), lambda i: (i, 0, 0))
    per_expert = SDS((n_tiles, LANES, LANES), F32)
    per_expert_spec = BS((1, LANES, LANES), lambda i: (i, 0, 0))
    return pl.pallas_call(
        body,
        out_shape=(SDS((t, d), BF16),
                   expert_major, expert_major, token_major, token_major, token_major,
                   per_expert, per_expert, SDS((n_tiles, SUBLANES, LANES), F32),
                   table, table, SDS((n_tiles, 1, LANES), I32), SDS((N_EXPERTS, LANES), F32)),
        grid=(n_tiles,),
        in_specs=[BS((tm, d), lambda i: (i, 0)),
                  BS((1, SUBLANES, d), lambda i: (i // tiles_per_batch, 0, 0)),
                  BS((1, d), lambda i: (0, 0)),
                  BS((N_EXPERTS, d), lambda i: (0, 0)),
                  BS((N_EXPERTS, 1), lambda i: (0, 0)),
                  BS((tm, tm), lambda i: (0, 0))],
        out_specs=(BS((tm, d), lambda i: (i, 0)),
                   expert_major_spec, expert_major_spec, token_major_spec, token_major_spec,
                   token_major_spec, per_expert_spec, per_expert_spec,
                   BS((1, SUBLANES, LANES), lambda i: (i, 0, 0)),
                   table_spec, table_spec, BS((1, 1, LANES), lambda i: (i, 0, 0)),
                   BS((N_EXPERTS, LANES), lambda i: (0, 0))),
        scratch_shapes=[pltpu.VMEM((N_EXPERTS, LANES), F32)],
        compiler_params=_cparams("arbitrary"),
        name="moe_router",
    )(x_flat, mod_lat, g, wr_t, e_bias, tri)


def _start_run_chunks(pstart_ref, chunk_e_ref, chunk_rel_ref, rows, chunk_copy):
    def body(j, carry):
        slot0 = pstart_ref[chunk_e_ref[0, 0, j]] + chunk_rel_ref[0, 0, j]
        chunk_copy(pl.multiple_of(j * RUN_ALIGN, RUN_ALIGN),
                   pl.multiple_of(slot0, RUN_ALIGN)).start()
        return carry

    lax.fori_loop(0, rows // RUN_ALIGN, body, 0)


def _wait_run_rows(copy_of_rows, rows):
    def wait_n(nrows):
        def body(_, carry):
            copy_of_rows(nrows).wait()
            return carry
        return body

    lax.fori_loop(0, rows // SLOT_CHUNK, wait_n(SLOT_CHUNK), 0)
    lax.fori_loop(0, (rows % SLOT_CHUNK) // RUN_ALIGN, wait_n(RUN_ALIGN), 0)


def _dispatch_body(pend_ref, pcnt_ref, nu_ref, pstart_ref, rows_ref, chunk_e_ref, chunk_rel_ref,
                   slot_hi_ref, slot_lo_ref, bounds_ref, h_ref, xs_ref, loc_ref, zbuf_ref, sem, zsem,
                   *, tl, bm, n_blocks):
    i = pl.program_id(0)
    last = pl.num_programs(0) - 1
    buf = i % 2

    def zero_copy(row0):
        return pltpu.make_async_copy(
            zbuf_ref, xs_ref.at[pl.ds(pl.multiple_of(row0, RUN_ALIGN), bm), :], zsem)

    @pl.when(i == 0)
    def _():
        zbuf_ref[...] = jnp.zeros_like(zbuf_ref)

        def start(e, c):
            @pl.when(pcnt_ref[e] > 0)
            def _():
                zero_copy(pend_ref[e] - bm).start()
            return c

        def wait(e, c):
            @pl.when(pcnt_ref[e] > 0)
            def _():
                zero_copy(pend_ref[e] - bm).wait()
            return c

        def start_tail(j, c):
            zero_copy(j * bm).start()
            return c

        def wait_tail(j, c):
            zero_copy(j * bm).wait()
            return c

        lax.fori_loop(0, N_EXPERTS, start, 0)
        lax.fori_loop(nu_ref[0], n_blocks, start_tail, 0)
        lax.fori_loop(0, N_EXPERTS, wait, 0)
        lax.fori_loop(nu_ref[0], n_blocks, wait_tail, 0)

    rows_used = rows_ref[i]
    h = h_ref[...]
    run_lo = bounds_ref[0, 0:1, :]
    run_hi = bounds_ref[0, 1:2, :]

    def sort_chunk(c, carry):
        row0 = (c * SLOT_CHUNK).astype(F32)
        row_e = lax.broadcasted_iota(I32, (SLOT_CHUNK, LANES), 0).astype(F32) + row0
        in_run = jnp.where(row_e >= run_lo, jnp.where(row_e < run_hi, 1.0, 0.0), 0.0).astype(BF16)
        slot_of_token = (256.0 * _dot(in_run, slot_hi_ref[0]) + _dot(in_run, slot_lo_ref[0]))
        row_t = lax.broadcasted_iota(I32, (SLOT_CHUNK, tl), 0).astype(F32) + row0
        onehot = jnp.where(slot_of_token == row_t, 1.0, 0.0).astype(BF16)
        rows = pl.ds(pl.multiple_of(c * SLOT_CHUNK, SLOT_CHUNK), SLOT_CHUNK)
        loc_ref[buf, rows, :] = _dot(onehot, h).astype(BF16)
        return carry

    lax.fori_loop(0, (rows_used + SLOT_CHUNK - 1) // SLOT_CHUNK, sort_chunk, 0)

    def run_copy(b, loc0, slot0, nrows=RUN_ALIGN):
        return pltpu.make_async_copy(loc_ref.at[b, pl.ds(loc0, nrows), :],
                                     xs_ref.at[pl.ds(slot0, nrows), :], sem.at[b])

    _start_run_chunks(pstart_ref, chunk_e_ref, chunk_rel_ref, rows_used,
                      lambda loc0, slot0: run_copy(buf, loc0, slot0))

    @pl.when(i > 0)
    def _():
        _wait_run_rows(lambda n: run_copy(1 - buf, 0, 0, n), rows_ref[jnp.maximum(i - 1, 0)])

    @pl.when(i == last)
    def _():
        _wait_run_rows(lambda n: run_copy(buf, 0, 0, n), rows_used)


def _dispatch(pend, pcnt, n_used, pstart, rows, chunk_e, chunk_rel, slot_hi, slot_lo, bounds_row,
              h2b, n_slots):
    t, d = h2b.shape
    tl = MOE_TILE
    body = functools.partial(_dispatch_body, tl=tl, bm=BM_FFN, n_blocks=n_slots // BM_FFN)
    table_spec = lambda index: BS((1, 1, CHUNK_TABLE), index, memory_space=pltpu.SMEM)
    grid_spec = pltpu.PrefetchScalarGridSpec(
        num_scalar_prefetch=5,
        grid=(t // tl,),
        in_specs=[table_spec(lambda i, *_: (i, 0, 0)),
                  table_spec(lambda i, *_: (i, 0, 0)),
                  BS((1, LANES, tl), lambda i, *_: (i, 0, 0)),
                  BS((1, LANES, tl), lambda i, *_: (i, 0, 0)),
                  BS((1, SUBLANES, LANES), lambda i, *_: (i, 0, 0)),
                  BS((tl, d), lambda i, *_: (i, 0))],
        out_specs=BS(memory_space=pl.ANY),
        scratch_shapes=[pltpu.VMEM((2, LOCAL_SLOTS, d), BF16), pltpu.VMEM((BM_FFN, d), BF16),
                        pltpu.SemaphoreType.DMA((2,)), pltpu.SemaphoreType.DMA(())],
    )
    return pl.pallas_call(
        body,
        out_shape=SDS((n_slots, d), BF16),
        grid_spec=grid_spec,
        compiler_params=_cparams("arbitrary"),
        name="moe_dispatch",
    )(pend, pcnt, n_used, pstart, rows, chunk_e, chunk_rel, slot_hi, slot_lo, bounds_row, h2b)


def _ffn_body(first_ref, count_ref, nu_ref, xs_ref, wg_ref, wu_ref, wd_ref, ys_ref, wgb_ref, wub_ref,
              wdb_ref, xbuf_ref, ybuf_ref, isem, osem, *, bm, n_blocks):
    e = pl.program_id(0)
    nu = nu_ref[0]

    def block_rows(blk):
        return pl.ds(pl.multiple_of(blk * bm, bm), bm)

    def in_copy(blk, slot):
        return pltpu.make_async_copy(xs_ref.at[block_rows(blk), :], xbuf_ref.at[slot], isem.at[slot])

    def out_copy(blk, slot):
        return pltpu.make_async_copy(ybuf_ref.at[slot], ys_ref.at[block_rows(blk), :], osem.at[slot])

    @pl.when(e == 0)
    def _():
        for s in range(FFN_IN_BUFS):
            @pl.when(s < nu)
            def _():
                in_copy(s, s).start()

    @pl.when(count_ref[e] > 0)
    def _():
        wgb_ref[...] = wg_ref[...].astype(BF16)
        wub_ref[...] = wu_ref[...].astype(BF16)
        wdb_ref[...] = wd_ref[...].astype(BF16)

    def one_block(b, carry):
        i = first_ref[e] + b
        slot = i % FFN_IN_BUFS
        oslot = i % 2
        in_copy(i, slot).wait()
        x = xbuf_ref[slot]
        a = _silu(_dot(x, wgb_ref[...])) * _dot(x, wub_ref[...])
        y = _dot(a.astype(BF16), wdb_ref[...]).astype(BF16)

        @pl.when(i >= 2)
        def _():
            out_copy(i - 2, oslot).wait()

        ybuf_ref[oslot] = y
        out_copy(i, oslot).start()

        @pl.when(i + FFN_IN_BUFS < nu)
        def _():
            in_copy(i + FFN_IN_BUFS, slot).start()

        return carry

    lax.fori_loop(0, count_ref[e], one_block, 0)

    @pl.when(e == pl.num_programs(0) - 1)
    def _():
        @pl.when(nu >= 2)
        def _():
            out_copy(nu - 2, (nu - 2) % 2).wait()

        out_copy(nu - 1, (nu - 1) % 2).wait()
        ybuf_ref[0] = jnp.zeros((bm, ybuf_ref.shape[-1]), BF16)

        def start_tail(j, c):
            out_copy(j, 0).start()
            return c

        def wait_tail(j, c):
            out_copy(j, 0).wait()
            return c

        lax.fori_loop(nu, n_blocks, start_tail, 0)
        lax.fori_loop(nu, n_blocks, wait_tail, 0)


def _ffn(first_block, block_count, n_used, xs, w_gate, w_up, w_down, layer):
    n_slots, d = xs.shape
    bm = BM_FFN
    de = w_gate.shape[-1]
    n_blocks = n_slots // bm
    body = functools.partial(_ffn_body, bm=bm, n_blocks=n_blocks)
    grid_spec = pltpu.PrefetchScalarGridSpec(
        num_scalar_prefetch=3,
        grid=(N_EXPERTS,),
        in_specs=[BS(memory_space=pl.ANY),
                  BS((None, None, d, de), lambda e, *_: (layer, e, 0, 0)),
                  BS((None, None, d, de), lambda e, *_: (layer, e, 0, 0)),
                  BS((None, None, de, d), lambda e, *_: (layer, e, 0, 0))],
        out_specs=BS(memory_space=pl.ANY),
        scratch_shapes=[pltpu.VMEM((d, de), BF16), pltpu.VMEM((d, de), BF16),
                        pltpu.VMEM((de, d), BF16),
                        pltpu.VMEM((FFN_IN_BUFS, bm, d), BF16), pltpu.VMEM((2, bm, d), BF16),
                        pltpu.SemaphoreType.DMA((FFN_IN_BUFS,)), pltpu.SemaphoreType.DMA((2,))],
    )
    return pl.pallas_call(
        body,
        out_shape=SDS((n_slots, d), BF16),
        grid_spec=grid_spec,
        compiler_params=_cparams("arbitrary"),
        name="moe_experts",
    )(first_block, block_count, n_used, xs, w_gate, w_up, w_down)


def _combine_body(pstart_ref, rows_ref, chunk_e_ref, chunk_rel_ref, next_e_ref, next_rel_ref, ys_ref,
                  slot_hi_ref, slot_lo_ref, gate_t_ref, lo_col_ref, hi_col_ref,
                  x_ref, h2b_ref, mod_ref, wsg_ref, wsu_ref, wsd_ref, fg_ref,
                  o_ref, loc_ref, acc_ref, sem, *, tl, final):
    i = pl.program_id(0)
    last = pl.num_programs(0) - 1
    buf = i % 2
    rows_used = rows_ref[i]

    def run_copy(b, loc0, slot0, nrows=RUN_ALIGN):
        return pltpu.make_async_copy(ys_ref.at[pl.ds(slot0, nrows), :],
                                     loc_ref.at[b, pl.ds(loc0, nrows), :], sem.at[b])

    @pl.when(i == 0)
    def _():
        loc_ref[...] = jnp.zeros_like(loc_ref)
        _start_run_chunks(pstart_ref, chunk_e_ref, chunk_rel_ref, rows_used,
                          lambda loc0, slot0: run_copy(buf, loc0, slot0))

    @pl.when(i < last)
    def _():
        _start_run_chunks(pstart_ref, next_e_ref, next_rel_ref, rows_ref[jnp.minimum(i + 1, last)],
                          lambda loc0, slot0: run_copy(1 - buf, loc0, slot0))

    hb = h2b_ref[...]
    a = _silu(_dot(hb, wsg_ref[...])) * _dot(hb, wsu_ref[...])
    acc_ref[...] = _dot(a.astype(BF16), wsd_ref[...])
    run_lo = lo_col_ref[0][:, 0:1]
    run_hi = hi_col_ref[0][:, 0:1]

    _wait_run_rows(lambda n: run_copy(buf, 0, 0, n), rows_used)

    def unsort_chunk(c, carry):
        col0 = (c * SLOT_CHUNK).astype(F32)
        col_e = lax.broadcasted_iota(I32, (LANES, SLOT_CHUNK), 1).astype(F32) + col0
        in_run = jnp.where(col_e >= run_lo, jnp.where(col_e < run_hi, 1.0, 0.0), 0.0).astype(BF16)
        slot_of_token = (256.0 * _dot(slot_hi_ref[0], in_run) + _dot(slot_lo_ref[0], in_run))
        gate_of_token = _dot(gate_t_ref[0], in_run)
        col_t = lax.broadcasted_iota(I32, (tl, SLOT_CHUNK), 1).astype(F32) + col0
        gate = jnp.where(slot_of_token == col_t, gate_of_token, 0.0).astype(BF16)
        y = loc_ref[buf, pl.ds(pl.multiple_of(c * SLOT_CHUNK, SLOT_CHUNK), SLOT_CHUNK), :]
        acc_ref[...] = acc_ref[...] + _dot(gate, y)
        return carry

    lax.fori_loop(0, (rows_used + SLOT_CHUNK - 1) // SLOT_CHUNK, unsort_chunk, 0)

    xo = x_ref[...] + mod_ref[0, 5:6, :] * acc_ref[...]
    if final:
        ms = jnp.mean(xo * xo, axis=-1, keepdims=True)
        xo = (xo * lax.rsqrt(ms + EPS)) * fg_ref[...]
    o_ref[...] = xo


def _combine(pstart, rows, chunk_e, chunk_rel, ys, slot_hi_t, slot_lo_t, gate_t, lo_col, hi_col,
             x_flat, h2b, mod_lat, wsg, wsu, wsd, final_g, tokens_per_batch, final):
    t, d = x_flat.shape
    tl = MOE_TILE
    ds = wsg.shape[1]
    tiles_per_batch = tokens_per_batch // tl
    body = functools.partial(_combine_body, tl=tl, final=final)
    n_tiles = t // tl
    table_spec = lambda index: BS((1, 1, CHUNK_TABLE), index, memory_space=pltpu.SMEM)
    this_tile = lambda i, *_: (i, 0, 0)
    next_tile = lambda i, *_: (jnp.minimum(i + 1, n_tiles - 1), 0, 0)
    grid_spec = pltpu.PrefetchScalarGridSpec(
        num_scalar_prefetch=2,
        grid=(n_tiles,),
        in_specs=[table_spec(this_tile), table_spec(this_tile),
                  table_spec(next_tile), table_spec(next_tile),
                  BS(memory_space=pl.ANY),
                  BS((1, tl, LANES), this_tile), BS((1, tl, LANES), this_tile),
                  BS((1, tl, LANES), this_tile),
                  BS((1, LANES, LANES), this_tile), BS((1, LANES, LANES), this_tile),
                  BS((tl, d), lambda i, *_: (i, 0)),
                  BS((tl, d), lambda i, *_: (i, 0)),
                  BS((1, SUBLANES, d), lambda i, *_: (i // tiles_per_batch, 0, 0)),
                  BS((d, ds), lambda i, *_: (0, 0)),
                  BS((d, ds), lambda i, *_: (0, 0)),
                  BS((ds, d), lambda i, *_: (0, 0)),
                  BS((1, d), lambda i, *_: (0, 0))],
        out_specs=BS((tl, d), lambda i, *_: (i, 0)),
        scratch_shapes=[pltpu.VMEM((2, LOCAL_SLOTS, d), BF16), pltpu.VMEM((tl, d), F32),
                        pltpu.SemaphoreType.DMA((2,))],
    )
    return pl.pallas_call(
        body,
        out_shape=SDS((t, d), F32),
        grid_spec=grid_spec,
        compiler_params=_cparams("arbitrary"),
        name="moe_combine",
    )(pstart, rows, chunk_e, chunk_rel, chunk_e, chunk_rel, ys, slot_hi_t, slot_lo_t, gate_t,
      lo_col, hi_col, x_flat, h2b, mod_lat, wsg, wsu, wsd, final_g)


def _moe(x_flat, mod_lat, g, w_router, e_bias, w_gate, w_up, w_down, ws_gate, ws_up, ws_down,
         final_g, tri, tokens_per_batch, layer, final):
    t, d = x_flat.shape
    bm = BM_FFN
    n_tiles = t // MOE_TILE
    (h2b, slot_hi, slot_lo, slot_hi_t, slot_lo_t, gate_t, lo_col, hi_col, bounds_row, chunk_e,
     chunk_rel, rows, seg) = _router(x_flat, mod_lat, g, w_router.T, e_bias.reshape(N_EXPERTS, 1),
                                     tri, tokens_per_batch)
    rows = rows[:, 0, 0]
    seg_rows = seg[:, 0].astype(I32)
    pcnt = (seg_rows + bm - 1) // bm * bm
    pend = jnp.cumsum(pcnt).astype(I32)
    pstart = pend - pcnt
    max_rows = t * TOP_K + n_tiles * N_EXPERTS * (RUN_ALIGN - 1)
    n_blocks = -(-max_rows // bm) + N_EXPERTS
    n_slots = n_blocks * bm
    n_used = pend[-1:] // bm
    xs = _dispatch(pend, pcnt, n_used, pstart, rows, chunk_e, chunk_rel, slot_hi, slot_lo,
                   bounds_row, h2b, n_slots)
    ys = _ffn(pstart // bm, pcnt // bm, n_used, xs, w_gate, w_up, w_down, layer)
    return _combine(pstart, rows, chunk_e, chunk_rel, ys, slot_hi_t, slot_lo_t, gate_t, lo_col,
                    hi_col, x_flat, h2b, mod_lat,
                    ws_gate.astype(BF16), ws_up.astype(BF16), ws_down.astype(BF16),
                    final_g.reshape(1, d), tokens_per_batch, final)


def _inproj_c_body(xp_ref, x_ref, xn_ref, mod_ref, g_ref, w_ref, cw_ref, cb_ref,
                   v_ref, g1_ref, g2_ref, *, tm, n_tiles):
    i = pl.program_id(1)
    halo = SUBLANES
    xe = jnp.concatenate([xp_ref[0], x_ref[0], xn_ref[0]], axis=0)
    h = _norm_mod(xe, g_ref[...], mod_ref[0, 1:2, :], mod_ref[0, 0:1, :])
    row = lax.broadcasted_iota(I32, (tm + 2 * halo, 1), 0)
    outside = jnp.logical_or(jnp.logical_and(i == 0, row < halo),
                             jnp.logical_and(i == n_tiles - 1, row >= tm + halo))
    hb = jnp.where(outside, 0.0, h).astype(BF16)
    width = v_ref.shape[-1]
    for part, o_ref in enumerate((v_ref, g1_ref, g2_ref)):
        cols = slice(part * width, (part + 1) * width)
        zp = _dot(hb, w_ref[:, cols])
        up = pltpu.roll(zp, 1, 0)
        dn = pltpu.roll(zp, tm + 2 * halo - 1, 0)
        z = cw_ref[0:1, cols] * up + cw_ref[1:2, cols] * zp + cw_ref[2:3, cols] * dn + cb_ref[:, cols]
        o_ref[0] = z[halo:halo + tm]


def _inproj_c(x, mod_lat, g, w_in, conv_w, conv_b):
    b, l, d = x.shape
    tm = TM_PROJ
    n_tiles = l // tm
    w3 = w_in.shape[1]
    width = w3 // 3
    r8 = tm // SUBLANES
    body = functools.partial(_inproj_c_body, tm=tm, n_tiles=n_tiles)
    out = SDS((b, l, width), F32)
    return pl.pallas_call(
        body,
        out_shape=(out, out, out),
        grid=(b, n_tiles),
        in_specs=[BS((1, SUBLANES, d), lambda bi, i: (bi, jnp.maximum(i * r8 - 1, 0), 0)),
                  BS((1, tm, d), lambda bi, i: (bi, i, 0)),
                  BS((1, SUBLANES, d), lambda bi, i: (bi, jnp.minimum((i + 1) * r8, l // SUBLANES - 1), 0)),
                  BS((1, SUBLANES, d), lambda bi, i: (bi, 0, 0)),
                  BS((1, d), lambda bi, i: (0, 0)),
                  BS((d, w3), lambda bi, i: (0, 0)),
                  BS((3, w3), lambda bi, i: (0, 0)),
                  BS((1, w3), lambda bi, i: (0, 0))],
        out_specs=(BS((1, tm, width), lambda bi, i: (bi, i, 0)),
                   BS((1, tm, width), lambda bi, i: (bi, i, 0)),
                   BS((1, tm, width), lambda bi, i: (bi, i, 0))),
        compiler_params=_cparams("arbitrary", "arbitrary"),
        name="inproj_c",
    )(x, x, x, mod_lat, g, w_in, conv_w, conv_b)


def _filter_body(f_ref, w1_ref, b1_ref, w2_ref, b2_ref, w3hi_ref, w3lo_ref, fr_ref, dl_ref, keep0_ref,
                 hf_ref, l1_ref, *, tp):
    i = pl.program_id(0)
    feats = f_ref[...]
    fr = fr_ref[...]
    a = jnp.sin(fr * (_dot_hp(feats, w1_ref[...]) + b1_ref[...]))
    a = jnp.sin(fr * (_dot_hp(a, w2_ref[...]) + b2_ref[...]))
    a_hi = a.astype(BF16)
    a_lo = (a - a_hi.astype(F32)).astype(BF16)
    w3_hi = w3hi_ref[...]
    hf = _dot(a_hi, w3_hi) + _dot(a_hi, w3lo_ref[...]) + _dot(a_lo, w3_hi)
    t01 = feats[:, 0:1]
    hf = hf * (jnp.exp(-t01 * jnp.abs(dl_ref[...])) + DECAY_SHIFT)
    row = lax.broadcasted_iota(I32, hf.shape, 0) + i * tp
    hf = jnp.where(row == 0, hf * keep0_ref[...], hf)
    hf_ref[...] = hf

    @pl.when(i == 0)
    def _():
        l1_ref[...] = jnp.zeros_like(l1_ref)

    l1_ref[...] = l1_ref[...] + jnp.sum(jnp.abs(hf), axis=0, keepdims=True)


def _filters(feats, w1, b1, w2, b2, w3, freq, delta, width):
    n, fe = feats.shape
    hid = w2.shape[0]
    fo = w3.shape[1]
    tp = 256
    lag0_keep = jnp.tile(jnp.repeat(jnp.array([1.0, 0.0], F32), width), fo // (2 * width)).reshape(1, fo)
    body = functools.partial(_filter_body, tp=tp)
    full = lambda shape: BS(shape, lambda i: (0, 0))
    w3_hi = w3.astype(BF16)
    w3_lo = (w3 - w3_hi.astype(F32)).astype(BF16)
    return pl.pallas_call(
        body,
        out_shape=(SDS((n, fo), F32), SDS((SUBLANES, fo), F32)),
        grid=(n // tp,),
        in_specs=[BS((tp, fe), lambda i: (i, 0)), full((fe, hid)), full((1, hid)),
                  full((hid, hid)), full((1, hid)), full((hid, fo)), full((hid, fo)), full((1, hid)),
                  full((1, fo)), full((1, fo))],
        out_specs=(BS((tp, fo), lambda i: (i, 0)), BS((SUBLANES, fo), lambda i: (0, 0))),
        compiler_params=_cparams("arbitrary"),
        name="hyena_filters",
    )(feats, w1, b1, w2, b2, w3_hi, w3_lo, freq, delta, lag0_keep)


DFT_R = 128
DFT_VP = 72
DFT_BGROUP = 16
DFT_LANE_TILES = 4


def _dft_tables(n):
    r = DFT_R
    vp = DFT_VP
    m = 2 * n
    na = n // r
    two_pi = 2.0 * np.pi
    live = (np.arange(vp) <= r // 2).astype(np.float64)
    a = np.arange(na)[None, :]
    v = np.arange(vp)[:, None]
    ang1 = two_pi * ((a * v) % r) / r
    f1 = np.concatenate([np.cos(ang1), -np.sin(ang1)], axis=0) * np.tile(live, 2)[:, None]
    b = np.arange(r)[None, None, :]
    u = np.arange(r)[None, :, None]
    vv = np.arange(vp)[:, None, None]
    ang2 = two_pi * ((b * (r * u + vv)) % m) / m
    gr, gi = np.cos(ang2), -np.sin(ang2)
    fwd = np.concatenate([np.concatenate([gr, -gi], axis=2),
                          np.concatenate([gi, gr], axis=2)], axis=1)
    hr, hi = np.transpose(gr, (0, 2, 1)), -np.transpose(gi, (0, 2, 1))
    inv = np.concatenate([np.concatenate([hr, -hi], axis=2),
                          np.concatenate([hi, hr], axis=2)], axis=1)
    weight = live * np.where((np.arange(vp) == 0) | (np.arange(vp) == r // 2), 1.0, 2.0)
    ang3 = two_pi * ((np.arange(na)[:, None] * np.arange(vp)[None, :]) % r) / r
    f3 = np.concatenate([np.cos(ang3) * weight, -np.sin(ang3) * weight], axis=1) / m
    cast = lambda t: jnp.asarray(t.astype(np.float32)).astype(BF16)
    return cast(f1), cast(fwd), cast(inv), cast(f3)


def _lane_tile_specs(rows, index_map_of_tile):
    return [BS((None, rows, DFT_BGROUP, LANES), index_map_of_tile(t)) for t in range(DFT_LANE_TILES)]


def _rows_of_position(ref, j):
    x, bg, _ = ref.shape
    return ref.reshape(x * bg, LANES)[pl.ds(j, x, stride=bg), :]


def _dft_s1_body(*refs):
    q = DFT_LANE_TILES
    y_refs, f1_ref, ar_ref, ai_ref = refs[:q], refs[q], refs[q + 1], refs[q + 2]
    f1 = f1_ref[...]
    bg = DFT_BGROUP
    for t in range(q):
        ar2 = ar_ref.at[t].reshape(DFT_VP * bg, LANES)
        ai2 = ai_ref.at[t].reshape(DFT_VP * bg, LANES)
        for j in range(bg):
            res = _dot(f1, _rows_of_position(y_refs[t], j).astype(BF16))
            ar2[pl.ds(j, DFT_VP, stride=bg), :] = res[:DFT_VP]
            ai2[pl.ds(j, DFT_VP, stride=bg), :] = res[DFT_VP:]


def _dft_s1(y4, f1):
    nb, na, r, c = y4.shape
    q, bg = DFT_LANE_TILES, DFT_BGROUP
    out = SDS((nb, c // LANES, DFT_VP, r, LANES), F32)
    plane_spec = BS((None, q, DFT_VP, bg, LANES), lambda i, j, cc: (i, cc, 0, j, 0))
    tile_map = lambda t: (lambda i, j, cc: (i, 0, j, cc * q + t))
    return pl.pallas_call(
        _dft_s1_body,
        out_shape=(out, out),
        grid=(nb, r // bg, c // (q * LANES)),
        in_specs=_lane_tile_specs(na, tile_map) + [BS((2 * DFT_VP, na), lambda i, j, cc: (0, 0))],
        out_specs=(plane_spec, plane_spec),
        compiler_params=_cparams("arbitrary", "arbitrary", "arbitrary"),
        name="dft_stage1",
    )(*([y4] * q), f1)


def _plane_rows(re_ref, im_ref):
    wide = lambda ref: jnp.concatenate([ref[t] for t in range(ref.shape[0])], axis=1)
    return jnp.concatenate([wide(re_ref), wide(im_ref)], axis=0).astype(BF16)


def _filter_spec_body(arf_ref, aif_ref, arb_ref, aib_ref, g_ref, l1f_ref, l1b_ref, kr_ref, ki_ref):
    g = g_ref[...]
    yf = _dot(g, _plane_rows(arf_ref, aif_ref))
    yb = _dot(g, _plane_rows(arb_ref, aib_ref))
    inv = 1.0 / (l1f_ref[0:1, :] + l1b_ref[0:1, :])
    kr_ref[...] = (yf[:DFT_R] + yb[:DFT_R]) * inv
    ki_ref[...] = (yf[DFT_R:] - yb[DFT_R:]) * inv


def _filter_spectrum(ar, ai, fwd, l1, width):
    r = DFT_R
    a_spec = lambda d: BS((None, width // LANES, None, r, LANES), lambda v, o: (0, 2 * o + d, v, 0, 0))
    l_spec = lambda d: BS((SUBLANES, width), lambda v, o: (0, 2 * o + d))
    out = SDS((DFT_VP, r, HYENA_ORDER * width), F32)
    return pl.pallas_call(
        _filter_spec_body,
        out_shape=(out, out),
        grid=(DFT_VP, HYENA_ORDER),
        in_specs=[a_spec(0), a_spec(0), a_spec(1), a_spec(1),
                  BS((None, 2 * r, 2 * r), lambda v, o: (v, 0, 0)), l_spec(0), l_spec(1)],
        out_specs=(BS((None, r, width), lambda v, o: (v, 0, o)),
                   BS((None, r, width), lambda v, o: (v, 0, o))),
        compiler_params=_cparams("arbitrary", "arbitrary"),
        name="hyena_filter_spectrum",
    )(ar, ai, ar, ai, fwd, l1, l1)


def _conv_mid_body(ar_ref, ai_ref, g_ref, h_ref, kr_ref, ki_ref, qr_ref, qi_ref):
    y = _dot(g_ref[...], _plane_rows(ar_ref, ai_ref))
    yr, yi = y[:DFT_R], y[DFT_R:]
    kr, ki = kr_ref[...], ki_ref[...]
    p = jnp.concatenate([yr * kr - yi * ki, yr * ki + yi * kr], axis=0).astype(BF16)
    q = _dot(h_ref[...], p)
    for t in range(qr_ref.shape[0]):
        lanes = slice(t * LANES, (t + 1) * LANES)
        qr_ref[t] = q[:DFT_R, lanes]
        qi_ref[t] = q[DFT_R:, lanes]


def _conv_mid(ar, ai, fwd, inv, kr, ki, order):
    nb, tiles, vp, r, _ = ar.shape
    c = tiles * LANES
    a_spec = BS((None, tiles, None, r, LANES), lambda n, v: (n, 0, v, 0, 0))
    m_spec = BS((None, 2 * r, 2 * r), lambda n, v: (v, 0, 0))
    k_spec = BS((None, r, c), lambda n, v: (v, 0, order))
    out = SDS((nb, tiles, vp, r, LANES), F32)
    return pl.pallas_call(
        _conv_mid_body,
        out_shape=(out, out),
        grid=(nb, vp),
        in_specs=[a_spec, a_spec, m_spec, m_spec, k_spec, k_spec],
        out_specs=(a_spec, a_spec),
        compiler_params=_cparams("arbitrary", "arbitrary"),
        name="hyena_spectral_product",
    )(ar, ai, fwd, inv, kr, ki)


def _idft_gate_body(*refs):
    q = DFT_LANE_TILES
    qr_ref, qi_ref = refs[0], refs[1]
    y_refs, gate_refs = refs[2:2 + q], refs[2 + q:2 + 2 * q]
    f3_ref, fb_ref, o_ref = refs[2 + 2 * q:]
    f3 = f3_ref[...]
    for t in range(q):
        lanes = slice(t * LANES, (t + 1) * LANES)
        fb = fb_ref[:, lanes]
        for j in range(DFT_BGROUP):
            planes = jnp.concatenate([_rows_of_position(qr_ref.at[t], j),
                                      _rows_of_position(qi_ref.at[t], j)], axis=0).astype(BF16)
            conv = _dot(f3, planes)
            o_ref[:, j, lanes] = _rows_of_position(gate_refs[t], j) * (
                conv + fb * _rows_of_position(y_refs[t], j))


def _idft_gate(qr, qi, f3, y4, gate4, fbias):
    nb, na, r, c = y4.shape
    q, bg = DFT_LANE_TILES, DFT_BGROUP
    tile_map = lambda t: (lambda i, j, cc: (i, 0, j, cc * q + t))
    wide = lambda rows: BS((None, rows, bg, q * LANES), lambda i, j, cc: (i, 0, j, cc))
    plane_spec = BS((None, q, DFT_VP, bg, LANES), lambda i, j, cc: (i, cc, 0, j, 0))
    return pl.pallas_call(
        _idft_gate_body,
        out_shape=SDS((nb, na, r, c), F32),
        grid=(nb, r // bg, c // (q * LANES)),
        in_specs=([plane_spec, plane_spec]
                  + _lane_tile_specs(na, tile_map) + _lane_tile_specs(na, tile_map)
                  + [BS((na, 2 * DFT_VP), lambda i, j, cc: (0, 0)),
                     BS((1, q * LANES), lambda i, j, cc: (0, cc))]),
        out_specs=wide(na),
        compiler_params=_cparams("arbitrary", "arbitrary", "arbitrary"),
        name="hyena_idft_gate",
    )(qr, qi, *([y4] * q + [gate4] * q), f3, fbias)


def _outproj_body(y_ref, w_ref, x_ref, mod_ref, o_ref):
    o_ref[0] = x_ref[0] + mod_ref[0, 2:3, :] * _dot(y_ref[0].astype(BF16), w_ref[...])


def _outproj(y, w_out, x, mod_lat):
    b, l, d = x.shape
    tm = TM_PROJ
    wdt = y.shape[-1]
    return pl.pallas_call(
        _outproj_body,
        out_shape=SDS((b, l, d), F32),
        grid=(b, l // tm),
        in_specs=[BS((1, tm, wdt), lambda bi, i: (bi, i, 0)),
                  BS((wdt, d), lambda bi, i: (0, 0)),
                  BS((1, tm, d), lambda bi, i: (bi, i, 0)),
                  BS((1, SUBLANES, d), lambda bi, i: (bi, 0, 0))],
        out_specs=BS((1, tm, d), lambda bi, i: (bi, i, 0)),
        compiler_params=_cparams("arbitrary", "arbitrary"),
        name="outproj_c",
    )(y, w_out, x, mod_lat)


def _hyena(x, mod_lat, g, w_in, conv_w, conv_b, w1, b1, w2, b2, w3, freq, delta, f_bias, w_out):
    b, n, d = x.shape
    width = w_out.shape[0]
    r = DFT_R
    na = n // r
    f1, fwd, inv, f3 = _dft_tables(n)
    v, gate1, gate2 = _inproj_c(x, mod_lat, g, w_in.astype(BF16), conv_w, conv_b.reshape(1, -1))

    t = jnp.arange(n, dtype=F32)
    t01 = t / max(n - 1, 1)
    bands = jnp.linspace(1e-4, FILT_BANDS - 1, FILT_BANDS, dtype=F32)
    ang = (2.0 * math.pi / n) * t[:, None] * bands[None, :]
    feats = jnp.concatenate([t01[:, None], jnp.cos(ang), jnp.sin(ang)], axis=-1)
    fe = feats.shape[1]
    feats = jnp.pad(feats, ((0, 0), (0, LANES - fe)))
    w1p = jnp.pad(w1, ((0, LANES - fe), (0, 0)))
    hf, l1 = _filters(feats, w1p, b1.reshape(1, -1), w2, b2.reshape(1, -1), w3,
                      freq.reshape(1, -1), delta.reshape(1, -1), width)
    far, fai = _dft_s1(hf.reshape(1, na, r, hf.shape[1]), f1)
    kr, ki = _filter_spectrum(far, fai, fwd, l1, width)

    y4 = v.reshape(b, na, r, width)
    for o, gate in enumerate((gate1, gate2)):
        ar, ai = _dft_s1(y4, f1)
        qr, qi = _conv_mid(ar, ai, fwd, inv, kr, ki, o)
        y4 = _idft_gate(qr, qi, f3, y4, gate.reshape(b, na, r, width), f_bias[o].reshape(1, width))
    return _outproj(y4.reshape(b, n, width), w_out.astype(BF16), x, mod_lat)


def _rope_tables(seq_len):
    rows = seq_len // GRID_W
    row = jnp.repeat(jnp.arange(rows, dtype=F32), GRID_W)
    col = jnp.tile(jnp.arange(GRID_W, dtype=F32), rows)
    inv = jnp.power(ROPE_BASE, -jnp.arange(ROPE_FREQS, dtype=F32) / ROPE_FREQS)
    ar, ac = row[:, None] * inv, col[:, None] * inv
    cos_h = jnp.concatenate([jnp.cos(ar), jnp.cos(ar), jnp.cos(ac), jnp.cos(ac)], axis=1)
    sin_h = jnp.concatenate([-jnp.sin(ar), jnp.sin(ar), -jnp.sin(ac), jnp.sin(ac)], axis=1)
    reps = LANES // HEAD_DIM
    return jnp.tile(cos_h, (1, reps)), jnp.tile(sin_h, (1, reps))


def _rotate_partner_columns(w):
    ncol = w.shape[1]
    lane = np.arange(ncol)
    partner = np.where((lane % (2 * ROPE_FREQS)) < ROPE_FREQS, lane + ROPE_FREQS, lane - ROPE_FREQS)
    return w[:, partner]


def kernel(x, c, ctx, c_ctx, w_mod, b_mod, norm_g, w_in_ab, sink, w_spatial, b_spatial, w_out_ab,
           w_in_c, conv_w, conv_b, filt_w1, filt_b1, filt_w2, filt_b2, filt_w3, filt_freq,
           filt_delta, filt_bias, w_out_c, w_router, e_bias, w_gate, w_up, w_down, ws_gate,
           ws_up, ws_down, final_g):
    b, l, d = x.shape
    depth = w_mod.shape[0]
    assert depth == 2 and b + 1 <= SUBLANES

    cc = jnp.zeros((SUBLANES, d), F32).at[:b].set(c).at[b].set(c_ctx)
    m_all = _mod_vectors(cc, w_mod, b_mod)

    def mod_rows(layer, row0, nrow):
        m = m_all[layer, row0:row0 + nrow].reshape(nrow, 6, d)
        return jnp.pad(m, ((0, 0), (0, SUBLANES - 6), (0, 0)))

    tri = jnp.triu(jnp.ones((MOE_TILE, MOE_TILE), F32), k=1).astype(BF16)

    mod_lat = mod_rows(0, 0, b)
    mod_ctx = mod_rows(0, b, 1)[0]
    w_in = w_in_ab[0]
    qk = ATTN_WIDTH + KV_WIDTH
    w_cat = jnp.concatenate([w_in, _rotate_partner_columns(w_in[:, :qk])], axis=1).astype(BF16)
    cos_t, sin_t = _rope_tables(l)
    group_avg = jnp.kron(jnp.eye(N_SG_GROUPS, dtype=F32),
                         jnp.full((SG_GROUP_DIM, SG_GROUP_DIM), 1.0 / SG_GROUP_DIM, F32)).astype(BF16)
    kc, vc = _ctx_kv(ctx, mod_ctx, norm_g[0, 0].reshape(1, d),
                     w_in[:, ATTN_WIDTH:ATTN_WIDTH + 2 * KV_WIDTH].astype(BF16))
    q, k, v, ug, vn = _inproj_ab(x, mod_lat, norm_g[0, 0].reshape(1, d), w_cat, cos_t, sin_t, group_avg)
    b_full = jnp.repeat(b_spatial[0].T, SG_GROUP_DIM, axis=1)
    x1 = _mixer(sink[0], q, k, v, kc, vc, ug, vn, w_spatial[0].astype(BF16), b_full,
                w_out_ab[0].astype(BF16), x, mod_lat)
    x2 = _moe(x1.reshape(b * l, d), mod_lat, norm_g[0, 1].reshape(1, d), w_router[0], e_bias[0],
              w_gate, w_up, w_down, ws_gate[0], ws_up[0], ws_down[0], final_g, tri, l,
              layer=0, final=False).reshape(b, l, d)

    mod_lat = mod_rows(1, 0, b)
    x3 = _hyena(x2, mod_lat, norm_g[1, 0].reshape(1, d), w_in_c[0], conv_w[0], conv_b[0],
                filt_w1[0], filt_b1[0], filt_w2[0], filt_b2[0], filt_w3[0], filt_freq[0],
                filt_delta[0], filt_bias[0], w_out_c[0])
    out = _moe(x3.reshape(b * l, d), mod_lat, norm_g[1, 1].reshape(1, d), w_router[1], e_bias[1],
               w_gate, w_up, w_down, ws_gate[1], ws_up[1], ws_down[1], final_g, tri, l,
               layer=1, final=True)
    return out.reshape(b, l, d)
```

```python
import functools
import math

import numpy as np
import jax
import jax.numpy as jnp
from jax import lax
from jax.experimental import pallas as pl
from jax.experimental.pallas import tpu as pltpu

F32 = jnp.float32
BF16 = jnp.bfloat16
I32 = jnp.int32
HIGHEST = lax.Precision.HIGHEST
SDS = jax.ShapeDtypeStruct
BS = pl.BlockSpec

EPS = 1e-6
NEG = -1e30

GRID_W = 64
N_Q_HEADS = 8
N_KV_HEADS = 2
HEAD_DIM = 64
ATTN_WIDTH = N_Q_HEADS * HEAD_DIM
KV_WIDTH = N_KV_HEADS * HEAD_DIM
WINDOW = 128
BLOCK = 128
ROPE_BASE = 10000.0
ROPE_FREQS = HEAD_DIM // 4
N_SG_GROUPS = 8
SG_GROUP_DIM = 64
SG_WIDTH = N_SG_GROUPS * SG_GROUP_DIM
HYENA_ORDER = 2
FILT_BANDS = 16
DECAY_SHIFT = 0.05
N_EXPERTS = 64
TOP_K = 8
N_GROUPS = 8
TOPK_GROUPS = 4
ROUTED_SCALE = 2.5

LANES = 128
SUBLANES = 8
VMEM_LIMIT = 56 * 1024 * 1024

TM_PROJ = 512
TQ_MIX = 256
MOE_TILE = 256
BM_FFN = 512
RUN_ALIGN = 16
SLOT_CHUNK = 512
NO_SLOT = 256 * 256 - 1
LOCAL_SLOTS = -(-(TOP_K * MOE_TILE + N_EXPERTS * (RUN_ALIGN - 1)) // SLOT_CHUNK) * SLOT_CHUNK
COPY_FIELDS = 3
FFN_IN_BUFS = 4


def _cparams(*sem):
    return pltpu.CompilerParams(dimension_semantics=sem, vmem_limit_bytes=VMEM_LIMIT)


def _dot(a, b):
    return jnp.dot(a, b, preferred_element_type=F32)


def _dot_nt(a, b):
    return lax.dot_general(a, b, (((1,), (1,)), ((), ())), preferred_element_type=F32)


def _dot_hp(a, b):
    return jnp.dot(a, b, preferred_element_type=F32, precision=HIGHEST)


def _norm_mod(x, g, sc, sh):
    ms = jnp.mean(x * x, axis=-1, keepdims=True)
    y = x * lax.rsqrt(ms + EPS)
    return (y * g) * (1.0 + sc) + sh


def _gelu_tanh(x):
    c = math.sqrt(2.0 / math.pi)
    return 0.5 * x * (1.0 + jnp.tanh(c * (x + 0.044715 * (x * x * x))))


def _silu(x):
    return x * jax.nn.sigmoid(x)


def _mod_body(c_ref, w_ref, b_ref, o_ref):
    o_ref[0] = _dot_hp(_silu(c_ref[...]), w_ref[0]) + b_ref[0]


def _mod_vectors(cc, w_mod, b_mod):
    depth, d, n = w_mod.shape
    tn = 1536
    return pl.pallas_call(
        _mod_body,
        out_shape=SDS((depth, SUBLANES, n), F32),
        grid=(depth, n // tn),
        in_specs=[BS((SUBLANES, d), lambda l, j: (0, 0)),
                  BS((1, d, tn), lambda l, j: (l, 0, j)),
                  BS((1, 1, tn), lambda l, j: (l, 0, j))],
        out_specs=BS((1, SUBLANES, tn), lambda l, j: (l, 0, j)),
        compiler_params=_cparams("arbitrary", "arbitrary"),
        name="mod_vectors",
    )(cc, w_mod, b_mod.reshape(depth, 1, n))


def _ctx_kv_body(ctx_ref, mod_ref, g_ref, w_ref, kc_ref, vc_ref):
    h = _norm_mod(ctx_ref[0], g_ref[...], mod_ref[1:2, :], mod_ref[0:1, :])
    z = _dot(h.astype(BF16), w_ref[...])
    kc_ref[0] = z[:, :KV_WIDTH].astype(BF16)
    vc_ref[0] = z[:, KV_WIDTH:].astype(BF16)


def _ctx_kv(ctx, mod_ctx, g, w_kv):
    b, c, d = ctx.shape
    return pl.pallas_call(
        _ctx_kv_body,
        out_shape=(SDS((b, c, KV_WIDTH), BF16), SDS((b, c, KV_WIDTH), BF16)),
        grid=(b,),
        in_specs=[BS((1, c, d), lambda i: (i, 0, 0)),
                  BS((SUBLANES, d), lambda i: (0, 0)),
                  BS((1, d), lambda i: (0, 0)),
                  BS((d, 2 * KV_WIDTH), lambda i: (0, 0))],
        out_specs=(BS((1, c, KV_WIDTH), lambda i: (i, 0, 0)),
                   BS((1, c, KV_WIDTH), lambda i: (i, 0, 0))),
        compiler_params=_cparams("arbitrary"),
        name="ctx_kv",
    )(ctx, mod_ctx, g, w_kv)


def _inproj_ab_body(x_ref, mod_ref, g_ref, w_ref, cos_ref, sin_ref, avg_ref,
                    q_ref, k_ref, v_ref, ug_ref, vn_ref):
    h = _norm_mod(x_ref[0], g_ref[...], mod_ref[0, 1:2, :], mod_ref[0, 0:1, :]).astype(BF16)
    cs = cos_ref[...]
    sn = sin_ref[...]
    rot0 = ATTN_WIDTH + 2 * KV_WIDTH + 2 * SG_WIDTH
    scale = HEAD_DIM ** -0.5
    for j in range(ATTN_WIDTH // LANES):
        z = _dot(h, w_ref[:, j * LANES:(j + 1) * LANES])
        zr = _dot(h, w_ref[:, rot0 + j * LANES:rot0 + (j + 1) * LANES])
        q_ref[0, :, j * LANES:(j + 1) * LANES] = ((z * cs + zr * sn) * scale).astype(BF16)
    zk = _dot(h, w_ref[:, ATTN_WIDTH:ATTN_WIDTH + KV_WIDTH])
    zkr = _dot(h, w_ref[:, rot0 + ATTN_WIDTH:rot0 + ATTN_WIDTH + KV_WIDTH])
    k_ref[0] = (zk * cs + zkr * sn).astype(BF16)
    v_ref[0] = _dot(h, w_ref[:, ATTN_WIDTH + KV_WIDTH:ATTN_WIDTH + 2 * KV_WIDTH]).astype(BF16)
    u0 = ATTN_WIDTH + 2 * KV_WIDTH
    ug_ref[0] = _gelu_tanh(_dot(h, w_ref[:, u0:u0 + SG_WIDTH]))
    vf = _gelu_tanh(_dot(h, w_ref[:, u0 + SG_WIDTH:u0 + 2 * SG_WIDTH]))
    avg = avg_ref[...]

    def gmean(t):
        hi = t.astype(BF16)
        lo = (t - hi.astype(F32)).astype(BF16)
        return _dot(hi, avg) + _dot(lo, avg)

    vc = vf - gmean(vf)
    vn_ref[0] = (vc * lax.rsqrt(gmean(vc * vc) + EPS)).astype(BF16)


def _inproj_ab(x, mod_lat, g, w_cat, cos_t, sin_t, avg):
    b, l, d = x.shape
    tm = TM_PROJ
    ncol = w_cat.shape[1]
    return pl.pallas_call(
        _inproj_ab_body,
        out_shape=(SDS((b, l, ATTN_WIDTH), BF16), SDS((b, l, KV_WIDTH), BF16),
                   SDS((b, l, KV_WIDTH), BF16), SDS((b, l, SG_WIDTH), F32),
                   SDS((b, l, SG_WIDTH), BF16)),
        grid=(b, l // tm),
        in_specs=[BS((1, tm, d), lambda bi, i: (bi, i, 0)),
                  BS((1, SUBLANES, d), lambda bi, i: (bi, 0, 0)),
                  BS((1, d), lambda bi, i: (0, 0)),
                  BS((d, ncol), lambda bi, i: (0, 0)),
                  BS((tm, LANES), lambda bi, i: (i, 0)),
                  BS((tm, LANES), lambda bi, i: (i, 0)),
                  BS((SG_WIDTH, SG_WIDTH), lambda bi, i: (0, 0))],
        out_specs=(BS((1, tm, ATTN_WIDTH), lambda bi, i: (bi, i, 0)),
                   BS((1, tm, KV_WIDTH), lambda bi, i: (bi, i, 0)),
                   BS((1, tm, KV_WIDTH), lambda bi, i: (bi, i, 0)),
                   BS((1, tm, SG_WIDTH), lambda bi, i: (bi, i, 0)),
                   BS((1, tm, SG_WIDTH), lambda bi, i: (bi, i, 0))),
        compiler_params=_cparams("arbitrary", "arbitrary"),
        name="inproj_ab",
    )(x, mod_lat, g, w_cat, cos_t, sin_t, avg)


def _mixer_body(sink_ref, q_ref, kp_ref, kcur_ref, kn_ref, vp_ref, vcur_ref, vn_ref,
                kc_ref, vc_ref, ug_ref, vnorm_ref, ws_ref, bs_ref, wout_ref, x_ref, mod_ref,
                o_ref, cat_ref, *, seq_len, sub_blocks):
    i = pl.program_id(1)
    kk = jnp.concatenate([kp_ref[0], kcur_ref[0], kn_ref[0]], axis=0)
    vv = jnp.concatenate([vp_ref[0], vcur_ref[0], vn_ref[0]], axis=0)
    kc = kc_ref[0]
    vc = vc_ref[0]
    span = 3 * BLOCK
    ii = lax.broadcasted_iota(I32, (BLOCK, span), 0)
    jj = lax.broadcasted_iota(I32, (BLOCK, span), 1)
    dd = jj - ii
    in_window = jnp.where(dd >= 0, jnp.where(dd <= 2 * WINDOW, 1, 0), 0)
    group = N_Q_HEADS // N_KV_HEADS
    for r in range(sub_blocks):
        rows = slice(r * BLOCK, (r + 1) * BLOCK)
        kpos = (i * sub_blocks + r - 1) * BLOCK + jj
        in_seq = jnp.where(kpos >= 0, jnp.where(kpos < seq_len, 1, 0), 0)
        bias = jnp.where(in_window * in_seq > 0, 0.0, NEG)
        qb = q_ref[0, rows, :]
        kl = kk[r * BLOCK:r * BLOCK + span]
        vl = vv[r * BLOCK:r * BLOCK + span]
        for hq in range(N_Q_HEADS):
            hk = hq // group
            ks = slice(hk * HEAD_DIM, (hk + 1) * HEAD_DIM)
            qh = qb[:, hq * HEAD_DIM:(hq + 1) * HEAD_DIM]
            s_loc = _dot_nt(qh, kl[:, ks]) + bias
            s_ctx = _dot_nt(qh, kc[:, ks])
            sk = sink_ref[hq]
            m = jnp.maximum(jnp.maximum(jnp.max(s_loc, axis=-1, keepdims=True),
                                        jnp.max(s_ctx, axis=-1, keepdims=True)), sk)
            p_loc = jnp.exp(s_loc - m)
            p_ctx = jnp.exp(s_ctx - m)
            den = (jnp.sum(p_loc, axis=-1, keepdims=True) + jnp.sum(p_ctx, axis=-1, keepdims=True)
                   + jnp.exp(sk - m))
            o = _dot(p_loc.astype(BF16), vl[:, ks]) + _dot(p_ctx.astype(BF16), vc[:, ks])
            cat_ref[rows, hq * HEAD_DIM:(hq + 1) * HEAD_DIM] = (o / den).astype(BF16)
        vnb = vnorm_ref[0, rows, :]
        ugb = ug_ref[0, rows, :]
        for g in range(N_SG_GROUPS):
            gs = slice(g * SG_GROUP_DIM, (g + 1) * SG_GROUP_DIM)
            sg = _dot(ws_ref[g], vnb[:, gs]) + bs_ref[:, gs]
            cat_ref[rows, ATTN_WIDTH + g * SG_GROUP_DIM:ATTN_WIDTH + (g + 1) * SG_GROUP_DIM] = (
                ugb[:, gs] * sg).astype(BF16)
    y = _dot(cat_ref[...], wout_ref[...])
    o_ref[0] = x_ref[0] + mod_ref[0, 2:3, :] * y


def _mixer(sink, q, k, v, kc, vc, ug, vn, w_s, b_full, w_out, x, mod_lat):
    b, l, d = x.shape
    tq = TQ_MIX
    r = tq // BLOCK
    nb = l // BLOCK
    c = kc.shape[1]
    prev_map = lambda bi, i: (bi, jnp.maximum(i * r - 1, 0), 0)
    next_map = lambda bi, i: (bi, jnp.minimum((i + 1) * r, nb - 1), 0)
    cur_map = lambda bi, i: (bi, i, 0)
    body = functools.partial(_mixer_body, seq_len=l, sub_blocks=r)
    return pl.pallas_call(
        body,
        out_shape=SDS((b, l, d), F32),
        grid=(b, l // tq),
        in_specs=[BS(memory_space=pltpu.SMEM),
                  BS((1, tq, ATTN_WIDTH), cur_map),
                  BS((1, BLOCK, KV_WIDTH), prev_map), BS((1, tq, KV_WIDTH), cur_map),
                  BS((1, BLOCK, KV_WIDTH), next_map),
                  BS((1, BLOCK, KV_WIDTH), prev_map), BS((1, tq, KV_WIDTH), cur_map),
                  BS((1, BLOCK, KV_WIDTH), next_map),
                  BS((1, c, KV_WIDTH), lambda bi, i: (bi, 0, 0)),
                  BS((1, c, KV_WIDTH), lambda bi, i: (bi, 0, 0)),
                  BS((1, tq, SG_WIDTH), cur_map), BS((1, tq, SG_WIDTH), cur_map),
                  BS((N_SG_GROUPS, BLOCK, BLOCK), lambda bi, i: (0, 0, 0)),
                  BS((BLOCK, SG_WIDTH), lambda bi, i: (0, 0)),
                  BS((d, d), lambda bi, i: (0, 0)),
                  BS((1, tq, d), cur_map),
                  BS((1, SUBLANES, d), lambda bi, i: (bi, 0, 0))],
        out_specs=BS((1, tq, d), cur_map),
        scratch_shapes=[pltpu.VMEM((tq, d), BF16)],
        compiler_params=_cparams("arbitrary", "arbitrary"),
        name="mixer_ab",
    )(sink, q, k, k, k, v, v, v, kc, vc, ug, vn, w_s, b_full, w_out, x, mod_lat)


def _router_body(x_ref, mod_ref, g_ref, wr_ref, eb_ref, tri_ref,
                 h2b_ref, slot_hi_ref, slot_lo_ref, slot_hi_t_ref, slot_lo_t_ref, gate_t_ref,
                 lo_col_ref, hi_col_ref, bounds_row_ref, copy_ref, rows_ref, cnt_ref, carry_ref,
                 *, tm):
    i = pl.program_id(0)

    @pl.when(i == 0)
    def _():
        carry_ref[...] = jnp.zeros_like(carry_ref)

    h2 = _norm_mod(x_ref[...], g_ref[...], mod_ref[0, 4:5, :], mod_ref[0, 3:4, :])
    h2b_ref[...] = h2.astype(BF16)

    logits = lax.dot_general(wr_ref[...], h2, (((1,), (1,)), ((), ())),
                             preferred_element_type=F32, precision=HIGHEST)
    scores = jax.nn.sigmoid(logits)
    per_group = N_EXPERTS // N_GROUPS
    shape3 = (N_GROUPS, per_group, tm)
    s3 = scores.reshape(shape3)
    b3 = (scores + eb_ref[...]).reshape(shape3)
    sub = lax.broadcasted_iota(I32, shape3, 1)
    eid = lax.broadcasted_iota(I32, shape3, 0) * per_group + sub

    m1 = jnp.max(b3, axis=1, keepdims=True)
    i1 = jnp.min(jnp.where(b3 == m1, sub, per_group), axis=1, keepdims=True)
    m2 = jnp.max(jnp.where(sub == i1, -jnp.inf, b3), axis=1, keepdims=True)
    gs = m1 + m2
    keep = []
    for g in range(N_GROUPS):
        beaten = jnp.zeros((1, tm), I32)
        for g2 in range(N_GROUPS):
            if g2 == g:
                continue
            wins = (gs[g2] >= gs[g]) if g2 < g else (gs[g2] > gs[g])
            beaten = beaten + jnp.where(wins, 1, 0)
        keep.append(jnp.where(beaten < TOPK_GROUPS, 1, 0)[None])
    keep3 = jnp.concatenate(keep, axis=0)
    val = jnp.where(keep3 > 0, b3, -jnp.inf)

    def red(fn, a):
        return fn(fn(a, axis=0, keepdims=True), axis=1, keepdims=True)

    idxs, ws = [], []
    member = jnp.zeros(shape3, F32)
    for _ in range(TOP_K):
        m = red(jnp.max, val)
        idx = red(jnp.min, jnp.where(val == m, eid, N_EXPERTS))
        hit = eid == idx
        ws.append(red(jnp.sum, jnp.where(hit, s3, 0.0)))
        val = jnp.where(hit, -jnp.inf, val)
        member = member + jnp.where(hit, 1.0, 0.0)
        idxs.append(idx)
    wsum = ws[0]
    for w in ws[1:]:
        wsum = wsum + w

    member2 = member.reshape(N_EXPERTS, tm)
    cnt = jnp.sum(member2, axis=1, keepdims=True)
    runlen = jnp.floor((cnt + (RUN_ALIGN - 1)) * (1.0 / RUN_ALIGN)) * RUN_ALIGN
    runlen_b = jnp.broadcast_to(runlen, (N_EXPERTS, LANES))
    e_row = lax.broadcasted_iota(I32, (N_EXPERTS, N_EXPERTS), 0)
    e_col = lax.broadcasted_iota(I32, (N_EXPERTS, N_EXPERTS), 1)
    earlier = jnp.where(e_col < e_row, 1.0, 0.0).astype(BF16)
    loff = _dot(earlier, runlen_b.astype(BF16))
    slot = _dot(member2.astype(BF16), tri_ref[...]) + loff[:, 0:1]
    slot = jnp.where(member2 > 0.0, slot, float(NO_SLOT))
    slot_hi = jnp.floor(slot * (1.0 / 256.0))
    slot_lo = slot - 256.0 * slot_hi
    gate3 = jnp.zeros(shape3, F32)
    for k in range(TOP_K):
        gate3 = jnp.where(eid == idxs[k], ws[k] / wsum * ROUTED_SCALE, gate3)
    gate = gate3.reshape(N_EXPERTS, tm)
    no_expert = jnp.zeros((LANES - N_EXPERTS, tm), F32)
    pad_e = lambda a: jnp.concatenate([a, no_expert], axis=0)
    slot_hi_ref[0] = pad_e(slot_hi).astype(BF16)
    slot_lo_ref[0] = pad_e(slot_lo).astype(BF16)
    slot_hi_t_ref[0] = pad_e(slot_hi).T.astype(BF16)
    slot_lo_t_ref[0] = pad_e(slot_lo).T.astype(BF16)
    gate_t_ref[0] = pad_e(gate).T.astype(BF16)
    run_lo = loff
    run_hi = loff + runlen_b
    no_run = jnp.zeros((LANES - N_EXPERTS, LANES), F32)
    lo_col_ref[0] = jnp.concatenate([run_lo, no_run], axis=0)
    hi_col_ref[0] = jnp.concatenate([run_hi, no_run], axis=0)
    diag = (lax.broadcasted_iota(I32, (N_EXPERTS, LANES), 0)
            == lax.broadcasted_iota(I32, (N_EXPERTS, LANES), 1))
    bounds_row_ref[0] = jnp.concatenate(
        [jnp.sum(jnp.where(diag, run_lo, 0.0), axis=0, keepdims=True),
         jnp.sum(jnp.where(diag, run_hi, 0.0), axis=0, keepdims=True),
         jnp.zeros((SUBLANES - 2, LANES), F32)], axis=0)
    big = 2.0 * RUN_ALIGN
    n_chunks = runlen_b * (1.0 / RUN_ALIGN)
    n_big = jnp.floor(n_chunks * 0.5)
    n_small = n_chunks - 2.0 * n_big
    before_big = _dot(earlier, n_big.astype(BF16))
    before_small = _dot(earlier, n_small.astype(BF16))
    gbase = carry_ref[...]
    j = lax.broadcasted_iota(I32, (N_EXPERTS, LANES), 1).astype(F32)
    e_iota = lax.broadcasted_iota(I32, (N_EXPERTS, LANES), 0).astype(F32)

    def entries(before, count, loc0, rel0, step):
        e_of = jnp.sum(jnp.where(before + count <= j, 1.0, 0.0), axis=0, keepdims=True)
        pick = lambda x: jnp.sum(jnp.where(e_iota == e_of, x, 0.0), axis=0, keepdims=True)
        k = j[0:1, :] - pick(before)
        return jnp.minimum(e_of, N_EXPERTS - 1.0), pick(loc0) + step * k, pick(rel0) + step * k

    e_b, loc_b, rel_b = entries(before_big, n_big, loff, gbase, big)
    e_s, loc_s, rel_s = entries(before_small, n_small, loff + big * n_big, gbase + big * n_big, 0.0)
    copy_ref[0] = jnp.concatenate([jnp.concatenate([e_b, e_s], axis=1),
                                   jnp.concatenate([loc_b, loc_s], axis=1),
                                   jnp.concatenate([rel_b, rel_s], axis=1)], axis=0).astype(I32)
    last = slice(N_EXPERTS - 1, N_EXPERTS)
    lane = lax.broadcasted_iota(I32, (1, LANES), 1)
    rows_used = loff[last, :] + runlen_b[last, :]
    counts = jnp.where(lane == 0, rows_used,
                       jnp.where(lane == 1, before_big[last, :] + n_big[last, :],
                                 before_small[last, :] + n_small[last, :]))
    rows_ref[0] = counts.astype(I32)
    total = carry_ref[...] + runlen_b
    carry_ref[...] = total
    cnt_ref[...] = total


def _router(x_flat, mod_lat, g, wr_t, e_bias, tri, tokens_per_batch):
    t, d = x_flat.shape
    tm = MOE_TILE
    n_tiles = t // tm
    tiles_per_batch = tokens_per_batch // tm
    body = functools.partial(_router_body, tm=tm)
    table = SDS((n_tiles, COPY_FIELDS, 2 * LANES), I32)
    table_spec = BS((1, COPY_FIELDS, 2 * LANES), lambda i: (i, 0, 0))
    expert_major = SDS((n_tiles, LANES, tm), BF16)
    expert_major_spec = BS((1, LANES, tm), lambda i: (i, 0, 0))
    token_major = SDS((n_tiles, tm, LANES), BF16)
    token_major_spec = BS((1, tm, LANES), lambda i: (i, 0, 0))
    per_expert = SDS((n_tiles, LANES, LANES), F32)
    per_expert_spec = BS((1, LANES, LANES), lambda i: (i, 0, 0))
    return pl.pallas_call(
        body,
        out_shape=(SDS((t, d), BF16),
                   expert_major, expert_major, token_major, token_major, token_major,
                   per_expert, per_expert, SDS((n_tiles, SUBLANES, LANES), F32),
                   table, SDS((n_tiles, 1, LANES), I32), SDS((N_EXPERTS, LANES), F32)),
        grid=(n_tiles,),
        in_specs=[BS((tm, d), lambda i: (i, 0)),
                  BS((1, SUBLANES, d), lambda i: (i // tiles_per_batch, 0, 0)),
                  BS((1, d), lambda i: (0, 0)),
                  BS((N_EXPERTS, d), lambda i: (0, 0)),
                  BS((N_EXPERTS, 1), lambda i: (0, 0)),
                  BS((tm, tm), lambda i: (0, 0))],
        out_specs=(BS((tm, d), lambda i: (i, 0)),
                   expert_major_spec, expert_major_spec, token_major_spec, token_major_spec,
                   token_major_spec, per_expert_spec, per_expert_spec,
                   BS((1, SUBLANES, LANES), lambda i: (i, 0, 0)),
                   table_spec, BS((1, 1, LANES), lambda i: (i, 0, 0)),
                   BS((N_EXPERTS, LANES), lambda i: (0, 0))),
        scratch_shapes=[pltpu.VMEM((N_EXPERTS, LANES), F32)],
        compiler_params=_cparams("arbitrary"),
        name="moe_router",
    )(x_flat, mod_lat, g, wr_t, e_bias, tri)


def _start_run_copies(pstart_ref, copy_ref, n_big, n_small, run_copy):
    def entry(base, nrows):
        def body(j, carry):
            slot0 = pstart_ref[copy_ref[0, 0, base + j]] + copy_ref[0, 2, base + j]
            run_copy(pl.multiple_of(copy_ref[0, 1, base + j], RUN_ALIGN),
                     pl.multiple_of(slot0, RUN_ALIGN), nrows).start()
            return carry
        return body

    lax.fori_loop(0, n_big, entry(0, 2 * RUN_ALIGN), 0)
    lax.fori_loop(0, n_small, entry(LANES, RUN_ALIGN), 0)


def _wait_run_rows(copy_of_rows, rows):
    def wait_n(nrows):
        def body(_, carry):
            copy_of_rows(nrows).wait()
            return carry
        return body

    lax.fori_loop(0, rows // SLOT_CHUNK, wait_n(SLOT_CHUNK), 0)
    lax.fori_loop(0, (rows % SLOT_CHUNK) // RUN_ALIGN, wait_n(RUN_ALIGN), 0)


def _dispatch_body(pend_ref, pcnt_ref, nu_ref, pstart_ref, tile_ref, copy_ref,
                   slot_hi_ref, slot_lo_ref, bounds_ref, h_ref, xs_ref, loc_ref, zbuf_ref, sem, zsem,
                   *, tl, bm, n_blocks):
    i = pl.program_id(0)
    last = pl.num_programs(0) - 1
    buf = i % 2

    def zero_copy(row0):
        return pltpu.make_async_copy(
            zbuf_ref, xs_ref.at[pl.ds(pl.multiple_of(row0, RUN_ALIGN), bm), :], zsem)

    @pl.when(i == 0)
    def _():
        zbuf_ref[...] = jnp.zeros_like(zbuf_ref)

        def start(e, c):
            @pl.when(pcnt_ref[e] > 0)
            def _():
                zero_copy(pend_ref[e] - bm).start()
            return c

        def wait(e, c):
            @pl.when(pcnt_ref[e] > 0)
            def _():
                zero_copy(pend_ref[e] - bm).wait()
            return c

        def start_tail(j, c):
            zero_copy(j * bm).start()
            return c

        def wait_tail(j, c):
            zero_copy(j * bm).wait()
            return c

        lax.fori_loop(0, N_EXPERTS, start, 0)
        lax.fori_loop(nu_ref[0], n_blocks, start_tail, 0)
        lax.fori_loop(0, N_EXPERTS, wait, 0)
        lax.fori_loop(nu_ref[0], n_blocks, wait_tail, 0)

    rows_used = tile_ref[i, 0]
    h = h_ref[...]
    run_lo = bounds_ref[0, 0:1, :]
    run_hi = bounds_ref[0, 1:2, :]

    def sort_chunk(c, carry):
        row0 = (c * SLOT_CHUNK).astype(F32)
        row_e = lax.broadcasted_iota(I32, (SLOT_CHUNK, LANES), 0).astype(F32) + row0
        in_run = jnp.where(row_e >= run_lo, jnp.where(row_e < run_hi, 1.0, 0.0), 0.0).astype(BF16)
        slot_of_token = (256.0 * _dot(in_run, slot_hi_ref[0]) + _dot(in_run, slot_lo_ref[0]))
        row_t = lax.broadcasted_iota(I32, (SLOT_CHUNK, tl), 0).astype(F32) + row0
        onehot = jnp.where(slot_of_token == row_t, 1.0, 0.0).astype(BF16)
        rows = pl.ds(pl.multiple_of(c * SLOT_CHUNK, SLOT_CHUNK), SLOT_CHUNK)
        loc_ref[buf, rows, :] = _dot(onehot, h).astype(BF16)
        return carry

    lax.fori_loop(0, (rows_used + SLOT_CHUNK - 1) // SLOT_CHUNK, sort_chunk, 0)

    def run_copy(b, loc0, slot0, nrows=RUN_ALIGN):
        return pltpu.make_async_copy(loc_ref.at[b, pl.ds(loc0, nrows), :],
                                     xs_ref.at[pl.ds(slot0, nrows), :], sem.at[b])

    _start_run_copies(pstart_ref, copy_ref, tile_ref[i, 1], tile_ref[i, 2],
                      lambda loc0, slot0, n: run_copy(buf, loc0, slot0, n))

    @pl.when(i > 0)
    def _():
        _wait_run_rows(lambda n: run_copy(1 - buf, 0, 0, n), tile_ref[jnp.maximum(i - 1, 0), 0])

    @pl.when(i == last)
    def _():
        _wait_run_rows(lambda n: run_copy(buf, 0, 0, n), rows_used)


def _dispatch(pend, pcnt, n_used, pstart, tile_counts, copy_table, slot_hi, slot_lo, bounds_row,
              h2b, n_slots):
    t, d = h2b.shape
    tl = MOE_TILE
    body = functools.partial(_dispatch_body, tl=tl, bm=BM_FFN, n_blocks=n_slots // BM_FFN)
    grid_spec = pltpu.PrefetchScalarGridSpec(
        num_scalar_prefetch=5,
        grid=(t // tl,),
        in_specs=[BS((1, COPY_FIELDS, 2 * LANES), lambda i, *_: (i, 0, 0), memory_space=pltpu.SMEM),
                  BS((1, LANES, tl), lambda i, *_: (i, 0, 0)),
                  BS((1, LANES, tl), lambda i, *_: (i, 0, 0)),
                  BS((1, SUBLANES, LANES), lambda i, *_: (i, 0, 0)),
                  BS((tl, d), lambda i, *_: (i, 0))],
        out_specs=BS(memory_space=pl.ANY),
        scratch_shapes=[pltpu.VMEM((2, LOCAL_SLOTS, d), BF16), pltpu.VMEM((BM_FFN, d), BF16),
                        pltpu.SemaphoreType.DMA((2,)), pltpu.SemaphoreType.DMA(())],
    )
    return pl.pallas_call(
        body,
        out_shape=SDS((n_slots, d), BF16),
        grid_spec=grid_spec,
        compiler_params=_cparams("arbitrary"),
        name="moe_dispatch",
    )(pend, pcnt, n_used, pstart, tile_counts, copy_table, slot_hi, slot_lo, bounds_row, h2b)


def _ffn_body(first_ref, count_ref, nu_ref, xs_ref, wg_ref, wu_ref, wd_ref, ys_ref, wgb_ref, wub_ref,
              wdb_ref, xbuf_ref, ybuf_ref, isem, osem, *, bm, n_blocks):
    e = pl.program_id(0)
    nu = nu_ref[0]

    def block_rows(blk):
        return pl.ds(pl.multiple_of(blk * bm, bm), bm)

    def in_copy(blk, slot):
        return pltpu.make_async_copy(xs_ref.at[block_rows(blk), :], xbuf_ref.at[slot], isem.at[slot])

    def out_copy(blk, slot):
        return pltpu.make_async_copy(ybuf_ref.at[slot], ys_ref.at[block_rows(blk), :], osem.at[slot])

    @pl.when(e == 0)
    def _():
        for s in range(FFN_IN_BUFS):
            @pl.when(s < nu)
            def _():
                in_copy(s, s).start()

    @pl.when(count_ref[e] > 0)
    def _():
        wgb_ref[...] = wg_ref[...].astype(BF16)
        wub_ref[...] = wu_ref[...].astype(BF16)
        wdb_ref[...] = wd_ref[...].astype(BF16)

    def one_block(b, carry):
        i = first_ref[e] + b
        slot = i % FFN_IN_BUFS
        oslot = i % 2
        in_copy(i, slot).wait()
        x = xbuf_ref[slot]
        a = _silu(_dot(x, wgb_ref[...])) * _dot(x, wub_ref[...])
        y = _dot(a.astype(BF16), wdb_ref[...]).astype(BF16)

        @pl.when(i >= 2)
        def _():
            out_copy(i - 2, oslot).wait()

        ybuf_ref[oslot] = y
        out_copy(i, oslot).start()

        @pl.when(i + FFN_IN_BUFS < nu)
        def _():
            in_copy(i + FFN_IN_BUFS, slot).start()

        return carry

    lax.fori_loop(0, count_ref[e], one_block, 0)

    @pl.when(e == pl.num_programs(0) - 1)
    def _():
        @pl.when(nu >= 2)
        def _():
            out_copy(nu - 2, (nu - 2) % 2).wait()

        out_copy(nu - 1, (nu - 1) % 2).wait()
        ybuf_ref[0] = jnp.zeros((bm, ybuf_ref.shape[-1]), BF16)

        def start_tail(j, c):
            out_copy(j, 0).start()
            return c

        def wait_tail(j, c):
            out_copy(j, 0).wait()
            return c

        lax.fori_loop(nu, n_blocks, start_tail, 0)
        lax.fori_loop(nu, n_blocks, wait_tail, 0)


def _ffn(first_block, block_count, n_used, xs, w_gate, w_up, w_down, layer):
    n_slots, d = xs.shape
    bm = BM_FFN
    de = w_gate.shape[-1]
    n_blocks = n_slots // bm
    body = functools.partial(_ffn_body, bm=bm, n_blocks=n_blocks)
    grid_spec = pltpu.PrefetchScalarGridSpec(
        num_scalar_prefetch=3,
        grid=(N_EXPERTS,),
        in_specs=[BS(memory_space=pl.ANY),
                  BS((None, None, d, de), lambda e, *_: (layer, e, 0, 0)),
                  BS((None, None, d, de), lambda e, *_: (layer, e, 0, 0)),
                  BS((None, None, de, d), lambda e, *_: (layer, e, 0, 0))],
        out_specs=BS(memory_space=pl.ANY),
        scratch_shapes=[pltpu.VMEM((d, de), BF16), pltpu.VMEM((d, de), BF16),
                        pltpu.VMEM((de, d), BF16),
                        pltpu.VMEM((FFN_IN_BUFS, bm, d), BF16), pltpu.VMEM((2, bm, d), BF16),
                        pltpu.SemaphoreType.DMA((FFN_IN_BUFS,)), pltpu.SemaphoreType.DMA((2,))],
    )
    return pl.pallas_call(
        body,
        out_shape=SDS((n_slots, d), BF16),
        grid_spec=grid_spec,
        compiler_params=_cparams("arbitrary"),
        name="moe_experts",
    )(first_block, block_count, n_used, xs, w_gate, w_up, w_down)


def _combine_body(pstart_ref, tile_ref, copy_ref, next_copy_ref, ys_ref,
                  slot_hi_ref, slot_lo_ref, gate_t_ref, lo_col_ref, hi_col_ref,
                  x_ref, h2b_ref, mod_ref, wsg_ref, wsu_ref, wsd_ref, fg_ref,
                  o_ref, loc_ref, acc_ref, sem, *, tl, final):
    i = pl.program_id(0)
    last = pl.num_programs(0) - 1
    buf = i % 2
    rows_used = tile_ref[i, 0]
    nxt = jnp.minimum(i + 1, last)

    def run_copy(b, loc0, slot0, nrows=RUN_ALIGN):
        return pltpu.make_async_copy(ys_ref.at[pl.ds(slot0, nrows), :],
                                     loc_ref.at[b, pl.ds(loc0, nrows), :], sem.at[b])

    @pl.when(i == 0)
    def _():
        loc_ref[...] = jnp.zeros_like(loc_ref)
        _start_run_copies(pstart_ref, copy_ref, tile_ref[i, 1], tile_ref[i, 2],
                          lambda loc0, slot0, n: run_copy(buf, loc0, slot0, n))

    @pl.when(i < last)
    def _():
        _start_run_copies(pstart_ref, next_copy_ref, tile_ref[nxt, 1], tile_ref[nxt, 2],
                          lambda loc0, slot0, n: run_copy(1 - buf, loc0, slot0, n))

    hb = h2b_ref[...]
    a = _silu(_dot(hb, wsg_ref[...])) * _dot(hb, wsu_ref[...])
    acc_ref[...] = _dot(a.astype(BF16), wsd_ref[...])
    run_lo = lo_col_ref[0][:, 0:1]
    run_hi = hi_col_ref[0][:, 0:1]

    _wait_run_rows(lambda n: run_copy(buf, 0, 0, n), rows_used)

    def unsort_chunk(c, carry):
        col0 = (c * SLOT_CHUNK).astype(F32)
        col_e = lax.broadcasted_iota(I32, (LANES, SLOT_CHUNK), 1).astype(F32) + col0
        in_run = jnp.where(col_e >= run_lo, jnp.where(col_e < run_hi, 1.0, 0.0), 0.0).astype(BF16)
        slot_of_token = (256.0 * _dot(slot_hi_ref[0], in_run) + _dot(slot_lo_ref[0], in_run))
        gate_of_token = _dot(gate_t_ref[0], in_run)
        col_t = lax.broadcasted_iota(I32, (tl, SLOT_CHUNK), 1).astype(F32) + col0
        gate = jnp.where(slot_of_token == col_t, gate_of_token, 0.0).astype(BF16)
        y = loc_ref[buf, pl.ds(pl.multiple_of(c * SLOT_CHUNK, SLOT_CHUNK), SLOT_CHUNK), :]
        acc_ref[...] = acc_ref[...] + _dot(gate, y)
        return carry

    lax.fori_loop(0, (rows_used + SLOT_CHUNK - 1) // SLOT_CHUNK, unsort_chunk, 0)

    xo = x_ref[...] + mod_ref[0, 5:6, :] * acc_ref[...]
    if final:
        ms = jnp.mean(xo * xo, axis=-1, keepdims=True)
        xo = (xo * lax.rsqrt(ms + EPS)) * fg_ref[...]
    o_ref[...] = xo


def _combine(pstart, tile_counts, copy_table, ys, slot_hi_t, slot_lo_t, gate_t, lo_col, hi_col,
             x_flat, h2b, mod_lat, wsg, wsu, wsd, final_g, tokens_per_batch, final):
    t, d = x_flat.shape
    tl = MOE_TILE
    ds = wsg.shape[1]
    tiles_per_batch = tokens_per_batch // tl
    body = functools.partial(_combine_body, tl=tl, final=final)
    n_tiles = t // tl
    table_spec = lambda index: BS((1, COPY_FIELDS, 2 * LANES), index, memory_space=pltpu.SMEM)
    this_tile = lambda i, *_: (i, 0, 0)
    next_tile = lambda i, *_: (jnp.minimum(i + 1, n_tiles - 1), 0, 0)
    grid_spec = pltpu.PrefetchScalarGridSpec(
        num_scalar_prefetch=2,
        grid=(n_tiles,),
        in_specs=[table_spec(this_tile), table_spec(next_tile),
                  BS(memory_space=pl.ANY),
                  BS((1, tl, LANES), this_tile), BS((1, tl, LANES), this_tile),
                  BS((1, tl, LANES), this_tile),
                  BS((1, LANES, LANES), this_tile), BS((1, LANES, LANES), this_tile),
                  BS((tl, d), lambda i, *_: (i, 0)),
                  BS((tl, d), lambda i, *_: (i, 0)),
                  BS((1, SUBLANES, d), lambda i, *_: (i // tiles_per_batch, 0, 0)),
                  BS((d, ds), lambda i, *_: (0, 0)),
                  BS((d, ds), lambda i, *_: (0, 0)),
                  BS((ds, d), lambda i, *_: (0, 0)),
                  BS((1, d), lambda i, *_: (0, 0))],
        out_specs=BS((tl, d), lambda i, *_: (i, 0)),
        scratch_shapes=[pltpu.VMEM((2, LOCAL_SLOTS, d), BF16), pltpu.VMEM((tl, d), F32),
                        pltpu.SemaphoreType.DMA((2,))],
    )
    return pl.pallas_call(
        body,
        out_shape=SDS((t, d), F32),
        grid_spec=grid_spec,
        compiler_params=_cparams("arbitrary"),
        name="moe_combine",
    )(pstart, tile_counts, copy_table, copy_table, ys, slot_hi_t, slot_lo_t, gate_t,
      lo_col, hi_col, x_flat, h2b, mod_lat, wsg, wsu, wsd, final_g)


def _moe(x_flat, mod_lat, g, w_router, e_bias, w_gate, w_up, w_down, ws_gate, ws_up, ws_down,
         final_g, tri, tokens_per_batch, layer, final):
    t, d = x_flat.shape
    bm = BM_FFN
    n_tiles = t // MOE_TILE
    (h2b, slot_hi, slot_lo, slot_hi_t, slot_lo_t, gate_t, lo_col, hi_col, bounds_row, copy_table,
     tile_counts, seg) = _router(x_flat, mod_lat, g, w_router.T, e_bias.reshape(N_EXPERTS, 1), tri,
                                 tokens_per_batch)
    tile_counts = tile_counts[:, 0, :SUBLANES]
    seg_rows = seg[:, 0].astype(I32)
    pcnt = (seg_rows + bm - 1) // bm * bm
    pend = jnp.cumsum(pcnt).astype(I32)
    pstart = pend - pcnt
    max_rows = t * TOP_K + n_tiles * N_EXPERTS * (RUN_ALIGN - 1)
    n_blocks = -(-max_rows // bm) + N_EXPERTS
    n_slots = n_blocks * bm
    n_used = pend[-1:] // bm
    xs = _dispatch(pend, pcnt, n_used, pstart, tile_counts, copy_table, slot_hi, slot_lo,
                   bounds_row, h2b, n_slots)
    ys = _ffn(pstart // bm, pcnt // bm, n_used, xs, w_gate, w_up, w_down, layer)
    return _combine(pstart, tile_counts, copy_table, ys, slot_hi_t, slot_lo_t, gate_t, lo_col,
                    hi_col, x_flat, h2b, mod_lat,
                    ws_gate.astype(BF16), ws_up.astype(BF16), ws_down.astype(BF16),
                    final_g.reshape(1, d), tokens_per_batch, final)


def _inproj_c_body(xp_ref, x_ref, xn_ref, mod_ref, g_ref, w_ref, cw_ref, cb_ref,
                   v_ref, g1_ref, g2_ref, *, tm, n_tiles):
    i = pl.program_id(1)
    halo = SUBLANES
    xe = jnp.concatenate([xp_ref[0], x_ref[0], xn_ref[0]], axis=0)
    h = _norm_mod(xe, g_ref[...], mod_ref[0, 1:2, :], mod_ref[0, 0:1, :])
    row = lax.broadcasted_iota(I32, (tm + 2 * halo, 1), 0)
    outside = jnp.logical_or(jnp.logical_and(i == 0, row < halo),
                             jnp.logical_and(i == n_tiles - 1, row >= tm + halo))
    hb = jnp.where(outside, 0.0, h).astype(BF16)
    width = v_ref.shape[-1]
    for part, o_ref in enumerate((v_ref, g1_ref, g2_ref)):
        cols = slice(part * width, (part + 1) * width)
        zp = _dot(hb, w_ref[:, cols])
        up = pltpu.roll(zp, 1, 0)
        dn = pltpu.roll(zp, tm + 2 * halo - 1, 0)
        z = cw_ref[0:1, cols] * up + cw_ref[1:2, cols] * zp + cw_ref[2:3, cols] * dn + cb_ref[:, cols]
        o_ref[0] = z[halo:halo + tm]


def _inproj_c(x, mod_lat, g, w_in, conv_w, conv_b):
    b, l, d = x.shape
    tm = TM_PROJ
    n_tiles = l // tm
    w3 = w_in.shape[1]
    width = w3 // 3
    r8 = tm // SUBLANES
    body = functools.partial(_inproj_c_body, tm=tm, n_tiles=n_tiles)
    out = SDS((b, l, width), F32)
    return pl.pallas_call(
        body,
        out_shape=(out, out, out),
        grid=(b, n_tiles),
        in_specs=[BS((1, SUBLANES, d), lambda bi, i: (bi, jnp.maximum(i * r8 - 1, 0), 0)),
                  BS((1, tm, d), lambda bi, i: (bi, i, 0)),
                  BS((1, SUBLANES, d), lambda bi, i: (bi, jnp.minimum((i + 1) * r8, l // SUBLANES - 1), 0)),
                  BS((1, SUBLANES, d), lambda bi, i: (bi, 0, 0)),
                  BS((1, d), lambda bi, i: (0, 0)),
                  BS((d, w3), lambda bi, i: (0, 0)),
                  BS((3, w3), lambda bi, i: (0, 0)),
                  BS((1, w3), lambda bi, i: (0, 0))],
        out_specs=(BS((1, tm, width), lambda bi, i: (bi, i, 0)),
                   BS((1, tm, width), lambda bi, i: (bi, i, 0)),
                   BS((1, tm, width), lambda bi, i: (bi, i, 0))),
        compiler_params=_cparams("arbitrary", "arbitrary"),
        name="inproj_c",
    )(x, x, x, mod_lat, g, w_in, conv_w, conv_b)


def _filter_body(f_ref, w1_ref, b1_ref, w2_ref, b2_ref, w3hi_ref, w3lo_ref, fr_ref, dl_ref, keep0_ref,
                 hf_ref, l1_ref, *, tp):
    i = pl.program_id(0)
    feats = f_ref[...]
    fr = fr_ref[...]
    a = jnp.sin(fr * (_dot_hp(feats, w1_ref[...]) + b1_ref[...]))
    a = jnp.sin(fr * (_dot_hp(a, w2_ref[...]) + b2_ref[...]))
    a_hi = a.astype(BF16)
    a_lo = (a - a_hi.astype(F32)).astype(BF16)
    w3_hi = w3hi_ref[...]
    hf = _dot(a_hi, w3_hi) + _dot(a_hi, w3lo_ref[...]) + _dot(a_lo, w3_hi)
    t01 = feats[:, 0:1]
    hf = hf * (jnp.exp(-t01 * jnp.abs(dl_ref[...])) + DECAY_SHIFT)
    row = lax.broadcasted_iota(I32, hf.shape, 0) + i * tp
    hf = jnp.where(row == 0, hf * keep0_ref[...], hf)
    hf_ref[...] = hf

    @pl.when(i == 0)
    def _():
        l1_ref[...] = jnp.zeros_like(l1_ref)

    l1_ref[...] = l1_ref[...] + jnp.sum(jnp.abs(hf), axis=0, keepdims=True)


def _filters(feats, w1, b1, w2, b2, w3, freq, delta, width):
    n, fe = feats.shape
    hid = w2.shape[0]
    fo = w3.shape[1]
    tp = 256
    lag0_keep = jnp.tile(jnp.repeat(jnp.array([1.0, 0.0], F32), width), fo // (2 * width)).reshape(1, fo)
    body = functools.partial(_filter_body, tp=tp)
    full = lambda shape: BS(shape, lambda i: (0, 0))
    w3_hi = w3.astype(BF16)
    w3_lo = (w3 - w3_hi.astype(F32)).astype(BF16)
    return pl.pallas_call(
        body,
        out_shape=(SDS((n, fo), F32), SDS((SUBLANES, fo), F32)),
        grid=(n // tp,),
        in_specs=[BS((tp, fe), lambda i: (i, 0)), full((fe, hid)), full((1, hid)),
                  full((hid, hid)), full((1, hid)), full((hid, fo)), full((hid, fo)), full((1, hid)),
                  full((1, fo)), full((1, fo))],
        out_specs=(BS((tp, fo), lambda i: (i, 0)), BS((SUBLANES, fo), lambda i: (0, 0))),
        compiler_params=_cparams("arbitrary"),
        name="hyena_filters",
    )(feats, w1, b1, w2, b2, w3_hi, w3_lo, freq, delta, lag0_keep)


DFT_R = 128
DFT_VP = 72
DFT_BGROUP = 16
DFT_LANE_TILES = 4


def _dft_tables(n):
    r = DFT_R
    vp = DFT_VP
    m = 2 * n
    na = n // r
    two_pi = 2.0 * np.pi
    live = (np.arange(vp) <= r // 2).astype(np.float64)
    a = np.arange(na)[None, :]
    v = np.arange(vp)[:, None]
    ang1 = two_pi * ((a * v) % r) / r
    f1 = np.concatenate([np.cos(ang1), -np.sin(ang1)], axis=0) * np.tile(live, 2)[:, None]
    b = np.arange(r)[None, None, :]
    u = np.arange(r)[None, :, None]
    vv = np.arange(vp)[:, None, None]
    ang2 = two_pi * ((b * (r * u + vv)) % m) / m
    gr, gi = np.cos(ang2), -np.sin(ang2)
    fwd = np.concatenate([np.concatenate([gr, -gi], axis=2),
                          np.concatenate([gi, gr], axis=2)], axis=1)
    hr, hi = np.transpose(gr, (0, 2, 1)), -np.transpose(gi, (0, 2, 1))
    inv = np.concatenate([np.concatenate([hr, -hi], axis=2),
                          np.concatenate([hi, hr], axis=2)], axis=1)
    weight = live * np.where((np.arange(vp) == 0) | (np.arange(vp) == r // 2), 1.0, 2.0)
    ang3 = two_pi * ((np.arange(na)[:, None] * np.arange(vp)[None, :]) % r) / r
    f3 = np.concatenate([np.cos(ang3) * weight, -np.sin(ang3) * weight], axis=1) / m
    cast = lambda t: jnp.asarray(t.astype(np.float32)).astype(BF16)
    return cast(f1), cast(fwd), cast(inv), cast(f3)


def _lane_tile_specs(rows, index_map_of_tile):
    return [BS((None, rows, DFT_BGROUP, LANES), index_map_of_tile(t)) for t in range(DFT_LANE_TILES)]


def _rows_of_position(ref, j):
    x, bg, _ = ref.shape
    return ref.reshape(x * bg, LANES)[pl.ds(j, x, stride=bg), :]


def _dft_s1_body(*refs):
    q = DFT_LANE_TILES
    y_refs, f1_ref, ar_ref, ai_ref = refs[:q], refs[q], refs[q + 1], refs[q + 2]
    f1 = f1_ref[...]
    bg = DFT_BGROUP
    for t in range(q):
        ar2 = ar_ref.at[t].reshape(DFT_VP * bg, LANES)
        ai2 = ai_ref.at[t].reshape(DFT_VP * bg, LANES)
        for j in range(bg):
            res = _dot(f1, _rows_of_position(y_refs[t], j).astype(BF16))
            ar2[pl.ds(j, DFT_VP, stride=bg), :] = res[:DFT_VP]
            ai2[pl.ds(j, DFT_VP, stride=bg), :] = res[DFT_VP:]


def _dft_s1(y4, f1):
    nb, na, r, c = y4.shape
    q, bg = DFT_LANE_TILES, DFT_BGROUP
    out = SDS((nb, c // LANES, DFT_VP, r, LANES), F32)
    plane_spec = BS((None, q, DFT_VP, bg, LANES), lambda i, j, cc: (i, cc, 0, j, 0))
    tile_map = lambda t: (lambda i, j, cc: (i, 0, j, cc * q + t))
    return pl.pallas_call(
        _dft_s1_body,
        out_shape=(out, out),
        grid=(nb, r // bg, c // (q * LANES)),
        in_specs=_lane_tile_specs(na, tile_map) + [BS((2 * DFT_VP, na), lambda i, j, cc: (0, 0))],
        out_specs=(plane_spec, plane_spec),
        compiler_params=_cparams("arbitrary", "arbitrary", "arbitrary"),
        name="dft_stage1",
    )(*([y4] * q), f1)


def _plane_rows(re_ref, im_ref):
    wide = lambda ref: jnp.concatenate([ref[t] for t in range(ref.shape[0])], axis=1)
    return jnp.concatenate([wide(re_ref), wide(im_ref)], axis=0).astype(BF16)


def _filter_spec_body(arf_ref, aif_ref, arb_ref, aib_ref, g_ref, l1f_ref, l1b_ref, kr_ref, ki_ref):
    g = g_ref[...]
    yf = _dot(g, _plane_rows(arf_ref, aif_ref))
    yb = _dot(g, _plane_rows(arb_ref, aib_ref))
    inv = 1.0 / (l1f_ref[0:1, :] + l1b_ref[0:1, :])
    kr_ref[...] = ((yf[:DFT_R] + yb[:DFT_R]) * inv).astype(BF16)
    ki_ref[...] = ((yf[DFT_R:] - yb[DFT_R:]) * inv).astype(BF16)


def _filter_spectrum(ar, ai, fwd, l1, width):
    r = DFT_R
    a_spec = lambda d: BS((None, width // LANES, None, r, LANES), lambda v, o: (0, 2 * o + d, v, 0, 0))
    l_spec = lambda d: BS((SUBLANES, width), lambda v, o: (0, 2 * o + d))
    out = SDS((DFT_VP, r, HYENA_ORDER * width), BF16)
    return pl.pallas_call(
        _filter_spec_body,
        out_shape=(out, out),
        grid=(DFT_VP, HYENA_ORDER),
        in_specs=[a_spec(0), a_spec(0), a_spec(1), a_spec(1),
                  BS((None, 2 * r, 2 * r), lambda v, o: (v, 0, 0)), l_spec(0), l_spec(1)],
        out_specs=(BS((None, r, width), lambda v, o: (v, 0, o)),
                   BS((None, r, width), lambda v, o: (v, 0, o))),
        compiler_params=_cparams("arbitrary", "arbitrary"),
        name="hyena_filter_spectrum",
    )(ar, ai, ar, ai, fwd, l1, l1)


def _conv_mid_body(ar_ref, ai_ref, g_ref, h_ref, kr_ref, ki_ref, qr_ref, qi_ref):
    y = _dot(g_ref[...], _plane_rows(ar_ref, ai_ref))
    yr, yi = y[:DFT_R], y[DFT_R:]
    kr, ki = kr_ref[...].astype(F32), ki_ref[...].astype(F32)
    p = jnp.concatenate([yr * kr - yi * ki, yr * ki + yi * kr], axis=0).astype(BF16)
    q = _dot(h_ref[...], p)
    for t in range(qr_ref.shape[0]):
        lanes = slice(t * LANES, (t + 1) * LANES)
        qr_ref[t] = q[:DFT_R, lanes]
        qi_ref[t] = q[DFT_R:, lanes]


def _conv_mid(ar, ai, fwd, inv, kr, ki, order):
    nb, tiles, vp, r, _ = ar.shape
    c = tiles * LANES
    a_spec = BS((None, tiles, None, r, LANES), lambda n, v: (n, 0, v, 0, 0))
    m_spec = BS((None, 2 * r, 2 * r), lambda n, v: (v, 0, 0))
    k_spec = BS((None, r, c), lambda n, v: (v, 0, order))
    out = SDS((nb, tiles, vp, r, LANES), F32)
    return pl.pallas_call(
        _conv_mid_body,
        out_shape=(out, out),
        grid=(nb, vp),
        in_specs=[a_spec, a_spec, m_spec, m_spec, k_spec, k_spec],
        out_specs=(a_spec, a_spec),
        compiler_params=_cparams("arbitrary", "arbitrary"),
        name="hyena_spectral_product",
    )(ar, ai, fwd, inv, kr, ki)


def _idft_gate_body(*refs):
    q = DFT_LANE_TILES
    qr_ref, qi_ref = refs[0], refs[1]
    y_refs, gate_refs = refs[2:2 + q], refs[2 + q:2 + 2 * q]
    f3_ref, fb_ref, o_ref = refs[2 + 2 * q:]
    f3 = f3_ref[...]
    for t in range(q):
        lanes = slice(t * LANES, (t + 1) * LANES)
        fb = fb_ref[:, lanes]
        for j in range(DFT_BGROUP):
            planes = jnp.concatenate([_rows_of_position(qr_ref.at[t], j),
                                      _rows_of_position(qi_ref.at[t], j)], axis=0).astype(BF16)
            conv = _dot(f3, planes)
            o_ref[:, j, lanes] = _rows_of_position(gate_refs[t], j) * (
                conv + fb * _rows_of_position(y_refs[t], j))


def _idft_gate(qr, qi, f3, y4, gate4, fbias):
    nb, na, r, c = y4.shape
    q, bg = DFT_LANE_TILES, DFT_BGROUP
    tile_map = lambda t: (lambda i, j, cc: (i, 0, j, cc * q + t))
    wide = lambda rows: BS((None, rows, bg, q * LANES), lambda i, j, cc: (i, 0, j, cc))
    plane_spec = BS((None, q, DFT_VP, bg, LANES), lambda i, j, cc: (i, cc, 0, j, 0))
    return pl.pallas_call(
        _idft_gate_body,
        out_shape=SDS((nb, na, r, c), F32),
        grid=(nb, r // bg, c // (q * LANES)),
        in_specs=([plane_spec, plane_spec]
                  + _lane_tile_specs(na, tile_map) + _lane_tile_specs(na, tile_map)
                  + [BS((na, 2 * DFT_VP), lambda i, j, cc: (0, 0)),
                     BS((1, q * LANES), lambda i, j, cc: (0, cc))]),
        out_specs=wide(na),
        compiler_params=_cparams("arbitrary", "arbitrary", "arbitrary"),
        name="hyena_idft_gate",
    )(qr, qi, *([y4] * q + [gate4] * q), f3, fbias)


def _outproj_body(y_ref, w_ref, x_ref, mod_ref, o_ref):
    o_ref[0] = x_ref[0] + mod_ref[0, 2:3, :] * _dot(y_ref[0].astype(BF16), w_ref[...])


def _outproj(y, w_out, x, mod_lat):
    b, l, d = x.shape
    tm = TM_PROJ
    wdt = y.shape[-1]
    return pl.pallas_call(
        _outproj_body,
        out_shape=SDS((b, l, d), F32),
        grid=(b, l // tm),
        in_specs=[BS((1, tm, wdt), lambda bi, i: (bi, i, 0)),
                  BS((wdt, d), lambda bi, i: (0, 0)),
                  BS((1, tm, d), lambda bi, i: (bi, i, 0)),
                  BS((1, SUBLANES, d), lambda bi, i: (bi, 0, 0))],
        out_specs=BS((1, tm, d), lambda bi, i: (bi, i, 0)),
        compiler_params=_cparams("arbitrary", "arbitrary"),
        name="outproj_c",
    )(y, w_out, x, mod_lat)


def _hyena(x, mod_lat, g, w_in, conv_w, conv_b, w1, b1, w2, b2, w3, freq, delta, f_bias, w_out):
    b, n, d = x.shape
    width = w_out.shape[0]
    r = DFT_R
    na = n // r
    f1, fwd, inv, f3 = _dft_tables(n)
    v, gate1, gate2 = _inproj_c(x, mod_lat, g, w_in.astype(BF16), conv_w, conv_b.reshape(1, -1))

    t = jnp.arange(n, dtype=F32)
    t01 = t / max(n - 1, 1)
    bands = jnp.linspace(1e-4, FILT_BANDS - 1, FILT_BANDS, dtype=F32)
    ang = (2.0 * math.pi / n) * t[:, None] * bands[None, :]
    feats = jnp.concatenate([t01[:, None], jnp.cos(ang), jnp.sin(ang)], axis=-1)
    fe = feats.shape[1]
    feats = jnp.pad(feats, ((0, 0), (0, LANES - fe)))
    w1p = jnp.pad(w1, ((0, LANES - fe), (0, 0)))
    hf, l1 = _filters(feats, w1p, b1.reshape(1, -1), w2, b2.reshape(1, -1), w3,
                      freq.reshape(1, -1), delta.reshape(1, -1), width)
    far, fai = _dft_s1(hf.reshape(1, na, r, hf.shape[1]), f1)
    kr, ki = _filter_spectrum(far, fai, fwd, l1, width)

    y4 = v.reshape(b, na, r, width)
    for o, gate in enumerate((gate1, gate2)):
        ar, ai = _dft_s1(y4, f1)
        qr, qi = _conv_mid(ar, ai, fwd, inv, kr, ki, o)
        y4 = _idft_gate(qr, qi, f3, y4, gate.reshape(b, na, r, width), f_bias[o].reshape(1, width))
    return _outproj(y4.reshape(b, n, width), w_out.astype(BF16), x, mod_lat)


def _rope_tables(seq_len):
    rows = seq_len // GRID_W
    row = jnp.repeat(jnp.arange(rows, dtype=F32), GRID_W)
    col = jnp.tile(jnp.arange(GRID_W, dtype=F32), rows)
    inv = jnp.power(ROPE_BASE, -jnp.arange(ROPE_FREQS, dtype=F32) / ROPE_FREQS)
    ar, ac = row[:, None] * inv, col[:, None] * inv
    cos_h = jnp.concatenate([jnp.cos(ar), jnp.cos(ar), jnp.cos(ac), jnp.cos(ac)], axis=1)
    sin_h = jnp.concatenate([-jnp.sin(ar), jnp.sin(ar), -jnp.sin(ac), jnp.sin(ac)], axis=1)
    reps = LANES // HEAD_DIM
    return jnp.tile(cos_h, (1, reps)), jnp.tile(sin_h, (1, reps))


def _rotate_partner_columns(w):
    ncol = w.shape[1]
    lane = np.arange(ncol)
    partner = np.where((lane % (2 * ROPE_FREQS)) < ROPE_FREQS, lane + ROPE_FREQS, lane - ROPE_FREQS)
    return w[:, partner]


def kernel(x, c, ctx, c_ctx, w_mod, b_mod, norm_g, w_in_ab, sink, w_spatial, b_spatial, w_out_ab,
           w_in_c, conv_w, conv_b, filt_w1, filt_b1, filt_w2, filt_b2, filt_w3, filt_freq,
           filt_delta, filt_bias, w_out_c, w_router, e_bias, w_gate, w_up, w_down, ws_gate,
           ws_up, ws_down, final_g):
    b, l, d = x.shape
    depth = w_mod.shape[0]
    assert depth == 2 and b + 1 <= SUBLANES

    cc = jnp.zeros((SUBLANES, d), F32).at[:b].set(c).at[b].set(c_ctx)
    m_all = _mod_vectors(cc, w_mod, b_mod)

    def mod_rows(layer, row0, nrow):
        m = m_all[layer, row0:row0 + nrow].reshape(nrow, 6, d)
        return jnp.pad(m, ((0, 0), (0, SUBLANES - 6), (0, 0)))

    tri = jnp.triu(jnp.ones((MOE_TILE, MOE_TILE), F32), k=1).astype(BF16)

    mod_lat = mod_rows(0, 0, b)
    mod_ctx = mod_rows(0, b, 1)[0]
    w_in = w_in_ab[0]
    qk = ATTN_WIDTH + KV_WIDTH
    w_cat = jnp.concatenate([w_in, _rotate_partner_columns(w_in[:, :qk])], axis=1).astype(BF16)
    cos_t, sin_t = _rope_tables(l)
    group_avg = jnp.kron(jnp.eye(N_SG_GROUPS, dtype=F32),
                         jnp.full((SG_GROUP_DIM, SG_GROUP_DIM), 1.0 / SG_GROUP_DIM, F32)).astype(BF16)
    kc, vc = _ctx_kv(ctx, mod_ctx, norm_g[0, 0].reshape(1, d),
                     w_in[:, ATTN_WIDTH:ATTN_WIDTH + 2 * KV_WIDTH].astype(BF16))
    q, k, v, ug, vn = _inproj_ab(x, mod_lat, norm_g[0, 0].reshape(1, d), w_cat, cos_t, sin_t, group_avg)
    b_full = jnp.repeat(b_spatial[0].T, SG_GROUP_DIM, axis=1)
    x1 = _mixer(sink[0], q, k, v, kc, vc, ug, vn, w_spatial[0].astype(BF16), b_full,
                w_out_ab[0].astype(BF16), x, mod_lat)
    x2 = _moe(x1.reshape(b * l, d), mod_lat, norm_g[0, 1].reshape(1, d), w_router[0], e_bias[0],
              w_gate, w_up, w_down, ws_gate[0], ws_up[0], ws_down[0], final_g, tri, l,
              layer=0, final=False).reshape(b, l, d)

    mod_lat = mod_rows(1, 0, b)
    x3 = _hyena(x2, mod_lat, norm_g[1, 0].reshape(1, d), w_in_c[0], conv_w[0], conv_b[0],
                filt_w1[0], filt_b1[0], filt_w2[0], filt_b2[0], filt_w3[0], filt_freq[0],
                filt_delta[0], filt_bias[0], w_out_c[0])
    out = _moe(x3.reshape(b * l, d), mod_lat, norm_g[1, 1].reshape(1, d), w_router[1], e_bias[1],
               w_gate, w_up, w_down, ws_gate[1], ws_up[1], ws_down[1], final_g, tri, l,
               layer=1, final=True)
    return out.reshape(b, l, d)
```

```python
import functools
import math

import numpy as np
import jax
import jax.numpy as jnp
from jax import lax
from jax.experimental import pallas as pl
from jax.experimental.pallas import tpu as pltpu

F32 = jnp.float32
BF16 = jnp.bfloat16
I32 = jnp.int32
HIGHEST = lax.Precision.HIGHEST
SDS = jax.ShapeDtypeStruct
BS = pl.BlockSpec

EPS = 1e-6
NEG = -1e30

GRID_W = 64
N_Q_HEADS = 8
N_KV_HEADS = 2
HEAD_DIM = 64
ATTN_WIDTH = N_Q_HEADS * HEAD_DIM
KV_WIDTH = N_KV_HEADS * HEAD_DIM
WINDOW = 128
BLOCK = 128
ROPE_BASE = 10000.0
ROPE_FREQS = HEAD_DIM // 4
N_SG_GROUPS = 8
SG_GROUP_DIM = 64
SG_WIDTH = N_SG_GROUPS * SG_GROUP_DIM
HYENA_ORDER = 2
FILT_BANDS = 16
DECAY_SHIFT = 0.05
N_EXPERTS = 64
TOP_K = 8
N_GROUPS = 8
TOPK_GROUPS = 4
ROUTED_SCALE = 2.5

LANES = 128
SUBLANES = 8
VMEM_LIMIT = 56 * 1024 * 1024

TM_PROJ = 512
TQ_MIX = 256
MOE_TILE = 256
BM_FFN = 512
RUN_ALIGN = 16
SLOT_CHUNK = 512
NO_SLOT = 256 * 256 - 1
LOCAL_SLOTS = -(-(TOP_K * MOE_TILE + N_EXPERTS * (RUN_ALIGN - 1)) // SLOT_CHUNK) * SLOT_CHUNK
COPY_FIELDS = 3
FFN_IN_BUFS = 4


def _cparams(*sem):
    return pltpu.CompilerParams(dimension_semantics=sem, vmem_limit_bytes=VMEM_LIMIT)


def _dot(a, b):
    return jnp.dot(a, b, preferred_element_type=F32)


def _dot_nt(a, b):
    return lax.dot_general(a, b, (((1,), (1,)), ((), ())), preferred_element_type=F32)


def _dot_hp(a, b):
    return jnp.dot(a, b, preferred_element_type=F32, precision=HIGHEST)


def _norm_mod(x, g, sc, sh):
    ms = jnp.mean(x * x, axis=-1, keepdims=True)
    y = x * lax.rsqrt(ms + EPS)
    return (y * g) * (1.0 + sc) + sh


def _gelu_tanh(x):
    c = math.sqrt(2.0 / math.pi)
    return 0.5 * x * (1.0 + jnp.tanh(c * (x + 0.044715 * (x * x * x))))


def _silu(x):
    return x * jax.nn.sigmoid(x)


def _mod_body(c_ref, w_ref, b_ref, o_ref):
    o_ref[0] = _dot_hp(_silu(c_ref[...]), w_ref[0]) + b_ref[0]


def _mod_vectors(cc, w_mod, b_mod):
    depth, d, n = w_mod.shape
    tn = 1536
    return pl.pallas_call(
        _mod_body,
        out_shape=SDS((depth, SUBLANES, n), F32),
        grid=(depth, n // tn),
        in_specs=[BS((SUBLANES, d), lambda l, j: (0, 0)),
                  BS((1, d, tn), lambda l, j: (l, 0, j)),
                  BS((1, 1, tn), lambda l, j: (l, 0, j))],
        out_specs=BS((1, SUBLANES, tn), lambda l, j: (l, 0, j)),
        compiler_params=_cparams("arbitrary", "arbitrary"),
        name="mod_vectors",
    )(cc, w_mod, b_mod.reshape(depth, 1, n))


def _ctx_kv_body(ctx_ref, mod_ref, g_ref, w_ref, kc_ref, vc_ref):
    h = _norm_mod(ctx_ref[0], g_ref[...], mod_ref[1:2, :], mod_ref[0:1, :])
    z = _dot(h.astype(BF16), w_ref[...])
    kc_ref[0] = z[:, :KV_WIDTH].astype(BF16)
    vc_ref[0] = z[:, KV_WIDTH:].astype(BF16)


def _ctx_kv(ctx, mod_ctx, g, w_kv):
    b, c, d = ctx.shape
    return pl.pallas_call(
        _ctx_kv_body,
        out_shape=(SDS((b, c, KV_WIDTH), BF16), SDS((b, c, KV_WIDTH), BF16)),
        grid=(b,),
        in_specs=[BS((1, c, d), lambda i: (i, 0, 0)),
                  BS((SUBLANES, d), lambda i: (0, 0)),
                  BS((1, d), lambda i: (0, 0)),
                  BS((d, 2 * KV_WIDTH), lambda i: (0, 0))],
        out_specs=(BS((1, c, KV_WIDTH), lambda i: (i, 0, 0)),
                   BS((1, c, KV_WIDTH), lambda i: (i, 0, 0))),
        compiler_params=_cparams("arbitrary"),
        name="ctx_kv",
    )(ctx, mod_ctx, g, w_kv)


def _inproj_ab_body(x_ref, mod_ref, g_ref, w_ref, cos_ref, sin_ref, avg_ref,
                    q_ref, k_ref, v_ref, ug_ref, vn_ref):
    h = _norm_mod(x_ref[0], g_ref[...], mod_ref[0, 1:2, :], mod_ref[0, 0:1, :]).astype(BF16)
    cs = cos_ref[...]
    sn = sin_ref[...]
    rot0 = ATTN_WIDTH + 2 * KV_WIDTH + 2 * SG_WIDTH
    scale = HEAD_DIM ** -0.5
    for j in range(ATTN_WIDTH // LANES):
        z = _dot(h, w_ref[:, j * LANES:(j + 1) * LANES])
        zr = _dot(h, w_ref[:, rot0 + j * LANES:rot0 + (j + 1) * LANES])
        q_ref[0, :, j * LANES:(j + 1) * LANES] = ((z * cs + zr * sn) * scale).astype(BF16)
    zk = _dot(h, w_ref[:, ATTN_WIDTH:ATTN_WIDTH + KV_WIDTH])
    zkr = _dot(h, w_ref[:, rot0 + ATTN_WIDTH:rot0 + ATTN_WIDTH + KV_WIDTH])
    k_ref[0] = (zk * cs + zkr * sn).astype(BF16)
    v_ref[0] = _dot(h, w_ref[:, ATTN_WIDTH + KV_WIDTH:ATTN_WIDTH + 2 * KV_WIDTH]).astype(BF16)
    u0 = ATTN_WIDTH + 2 * KV_WIDTH
    ug_ref[0] = _gelu_tanh(_dot(h, w_ref[:, u0:u0 + SG_WIDTH]))
    vf = _gelu_tanh(_dot(h, w_ref[:, u0 + SG_WIDTH:u0 + 2 * SG_WIDTH]))
    avg = avg_ref[...]

    def gmean(t):
        hi = t.astype(BF16)
        lo = (t - hi.astype(F32)).astype(BF16)
        return _dot(hi, avg) + _dot(lo, avg)

    vc = vf - gmean(vf)
    vn_ref[0] = (vc * lax.rsqrt(gmean(vc * vc) + EPS)).astype(BF16)


def _inproj_ab(x, mod_lat, g, w_cat, cos_t, sin_t, avg):
    b, l, d = x.shape
    tm = TM_PROJ
    ncol = w_cat.shape[1]
    return pl.pallas_call(
        _inproj_ab_body,
        out_shape=(SDS((b, l, ATTN_WIDTH), BF16), SDS((b, l, KV_WIDTH), BF16),
                   SDS((b, l, KV_WIDTH), BF16), SDS((b, l, SG_WIDTH), F32),
                   SDS((b, l, SG_WIDTH), BF16)),
        grid=(b, l // tm),
        in_specs=[BS((1, tm, d), lambda bi, i: (bi, i, 0)),
                  BS((1, SUBLANES, d), lambda bi, i: (bi, 0, 0)),
                  BS((1, d), lambda bi, i: (0, 0)),
                  BS((d, ncol), lambda bi, i: (0, 0)),
                  BS((tm, LANES), lambda bi, i: (i, 0)),
                  BS((tm, LANES), lambda bi, i: (i, 0)),
                  BS((SG_WIDTH, SG_WIDTH), lambda bi, i: (0, 0))],
        out_specs=(BS((1, tm, ATTN_WIDTH), lambda bi, i: (bi, i, 0)),
                   BS((1, tm, KV_WIDTH), lambda bi, i: (bi, i, 0)),
                   BS((1, tm, KV_WIDTH), lambda bi, i: (bi, i, 0)),
                   BS((1, tm, SG_WIDTH), lambda bi, i: (bi, i, 0)),
                   BS((1, tm, SG_WIDTH), lambda bi, i: (bi, i, 0))),
        compiler_params=_cparams("arbitrary", "arbitrary"),
        name="inproj_ab",
    )(x, mod_lat, g, w_cat, cos_t, sin_t, avg)


def _mixer_body(sink_ref, q_ref, kp_ref, kcur_ref, kn_ref, vp_ref, vcur_ref, vn_ref,
                kc_ref, vc_ref, ug_ref, vnorm_ref, ws_ref, bs_ref, wout_ref, x_ref, mod_ref,
                o_ref, cat_ref, *, seq_len, sub_blocks):
    i = pl.program_id(1)
    kk = jnp.concatenate([kp_ref[0], kcur_ref[0], kn_ref[0]], axis=0)
    vv = jnp.concatenate([vp_ref[0], vcur_ref[0], vn_ref[0]], axis=0)
    kc = kc_ref[0]
    vc = vc_ref[0]
    span = 3 * BLOCK
    ii = lax.broadcasted_iota(I32, (BLOCK, span), 0)
    jj = lax.broadcasted_iota(I32, (BLOCK, span), 1)
    dd = jj - ii
    in_window = jnp.where(dd >= 0, jnp.where(dd <= 2 * WINDOW, 1, 0), 0)
    group = N_Q_HEADS // N_KV_HEADS
    for r in range(sub_blocks):
        rows = slice(r * BLOCK, (r + 1) * BLOCK)
        kpos = (i * sub_blocks + r - 1) * BLOCK + jj
        in_seq = jnp.where(kpos >= 0, jnp.where(kpos < seq_len, 1, 0), 0)
        bias = jnp.where(in_window * in_seq > 0, 0.0, NEG)
        qb = q_ref[0, rows, :]
        kl = kk[r * BLOCK:r * BLOCK + span]
        vl = vv[r * BLOCK:r * BLOCK + span]
        for hq in range(N_Q_HEADS):
            hk = hq // group
            ks = slice(hk * HEAD_DIM, (hk + 1) * HEAD_DIM)
            qh = qb[:, hq * HEAD_DIM:(hq + 1) * HEAD_DIM]
            s_loc = _dot_nt(qh, kl[:, ks]) + bias
            s_ctx = _dot_nt(qh, kc[:, ks])
            sk = sink_ref[hq]
            m = jnp.maximum(jnp.maximum(jnp.max(s_loc, axis=-1, keepdims=True),
                                        jnp.max(s_ctx, axis=-1, keepdims=True)), sk)
            p_loc = jnp.exp(s_loc - m)
            p_ctx = jnp.exp(s_ctx - m)
            den = (jnp.sum(p_loc, axis=-1, keepdims=True) + jnp.sum(p_ctx, axis=-1, keepdims=True)
                   + jnp.exp(sk - m))
            o = _dot(p_loc.astype(BF16), vl[:, ks]) + _dot(p_ctx.astype(BF16), vc[:, ks])
            cat_ref[rows, hq * HEAD_DIM:(hq + 1) * HEAD_DIM] = (o / den).astype(BF16)
        vnb = vnorm_ref[0, rows, :]
        ugb = ug_ref[0, rows, :]
        for g in range(N_SG_GROUPS):
            gs = slice(g * SG_GROUP_DIM, (g + 1) * SG_GROUP_DIM)
            sg = _dot(ws_ref[g], vnb[:, gs]) + bs_ref[:, gs]
            cat_ref[rows, ATTN_WIDTH + g * SG_GROUP_DIM:ATTN_WIDTH + (g + 1) * SG_GROUP_DIM] = (
                ugb[:, gs] * sg).astype(BF16)
    y = _dot(cat_ref[...], wout_ref[...])
    o_ref[0] = x_ref[0] + mod_ref[0, 2:3, :] * y


def _mixer(sink, q, k, v, kc, vc, ug, vn, w_s, b_full, w_out, x, mod_lat):
    b, l, d = x.shape
    tq = TQ_MIX
    r = tq // BLOCK
    nb = l // BLOCK
    c = kc.shape[1]
    prev_map = lambda bi, i: (bi, jnp.maximum(i * r - 1, 0), 0)
    next_map = lambda bi, i: (bi, jnp.minimum((i + 1) * r, nb - 1), 0)
    cur_map = lambda bi, i: (bi, i, 0)
    body = functools.partial(_mixer_body, seq_len=l, sub_blocks=r)
    return pl.pallas_call(
        body,
        out_shape=SDS((b, l, d), F32),
        grid=(b, l // tq),
        in_specs=[BS(memory_space=pltpu.SMEM),
                  BS((1, tq, ATTN_WIDTH), cur_map),
                  BS((1, BLOCK, KV_WIDTH), prev_map), BS((1, tq, KV_WIDTH), cur_map),
                  BS((1, BLOCK, KV_WIDTH), next_map),
                  BS((1, BLOCK, KV_WIDTH), prev_map), BS((1, tq, KV_WIDTH), cur_map),
                  BS((1, BLOCK, KV_WIDTH), next_map),
                  BS((1, c, KV_WIDTH), lambda bi, i: (bi, 0, 0)),
                  BS((1, c, KV_WIDTH), lambda bi, i: (bi, 0, 0)),
                  BS((1, tq, SG_WIDTH), cur_map), BS((1, tq, SG_WIDTH), cur_map),
                  BS((N_SG_GROUPS, BLOCK, BLOCK), lambda bi, i: (0, 0, 0)),
                  BS((BLOCK, SG_WIDTH), lambda bi, i: (0, 0)),
                  BS((d, d), lambda bi, i: (0, 0)),
                  BS((1, tq, d), cur_map),
                  BS((1, SUBLANES, d), lambda bi, i: (bi, 0, 0))],
        out_specs=BS((1, tq, d), cur_map),
        scratch_shapes=[pltpu.VMEM((tq, d), BF16)],
        compiler_params=_cparams("arbitrary", "arbitrary"),
        name="mixer_ab",
    )(sink, q, k, k, k, v, v, v, kc, vc, ug, vn, w_s, b_full, w_out, x, mod_lat)


def _router_body(x_ref, mod_ref, g_ref, wr_ref, eb_ref, tri_ref,
                 h2b_ref, slot_hi_ref, slot_lo_ref, slot_hi_t_ref, slot_lo_t_ref, gate_t_ref,
                 lo_col_ref, hi_col_ref, bounds_row_ref, copy_ref, rows_ref, cnt_ref, carry_ref,
                 *, tm):
    i = pl.program_id(0)

    @pl.when(i == 0)
    def _():
        carry_ref[...] = jnp.zeros_like(carry_ref)

    h2 = _norm_mod(x_ref[...], g_ref[...], mod_ref[0, 4:5, :], mod_ref[0, 3:4, :])
    h2b_ref[...] = h2.astype(BF16)

    logits = lax.dot_general(wr_ref[...], h2, (((1,), (1,)), ((), ())),
                             preferred_element_type=F32, precision=HIGHEST)
    scores = jax.nn.sigmoid(logits)
    per_group = N_EXPERTS // N_GROUPS
    shape3 = (N_GROUPS, per_group, tm)
    s3 = scores.reshape(shape3)
    b3 = (scores + eb_ref[...]).reshape(shape3)
    sub = lax.broadcasted_iota(I32, shape3, 1)
    eid = lax.broadcasted_iota(I32, shape3, 0) * per_group + sub

    m1 = jnp.max(b3, axis=1, keepdims=True)
    i1 = jnp.min(jnp.where(b3 == m1, sub, per_group), axis=1, keepdims=True)
    m2 = jnp.max(jnp.where(sub == i1, -jnp.inf, b3), axis=1, keepdims=True)
    gs = m1 + m2
    keep = []
    for g in range(N_GROUPS):
        beaten = jnp.zeros((1, tm), I32)
        for g2 in range(N_GROUPS):
            if g2 == g:
                continue
            wins = (gs[g2] >= gs[g]) if g2 < g else (gs[g2] > gs[g])
            beaten = beaten + jnp.where(wins, 1, 0)
        keep.append(jnp.where(beaten < TOPK_GROUPS, 1, 0)[None])
    keep3 = jnp.concatenate(keep, axis=0)
    val = jnp.where(keep3 > 0, b3, -jnp.inf)

    def red(fn, a):
        return fn(fn(a, axis=0, keepdims=True), axis=1, keepdims=True)

    idxs, ws = [], []
    member = jnp.zeros(shape3, F32)
    for _ in range(TOP_K):
        m = red(jnp.max, val)
        idx = red(jnp.min, jnp.where(val == m, eid, N_EXPERTS))
        hit = eid == idx
        ws.append(red(jnp.sum, jnp.where(hit, s3, 0.0)))
        val = jnp.where(hit, -jnp.inf, val)
        member = member + jnp.where(hit, 1.0, 0.0)
        idxs.append(idx)
    wsum = ws[0]
    for w in ws[1:]:
        wsum = wsum + w

    member2 = member.reshape(N_EXPERTS, tm)
    cnt = jnp.sum(member2, axis=1, keepdims=True)
    runlen = jnp.floor((cnt + (RUN_ALIGN - 1)) * (1.0 / RUN_ALIGN)) * RUN_ALIGN
    runlen_b = jnp.broadcast_to(runlen, (N_EXPERTS, LANES))
    e_row = lax.broadcasted_iota(I32, (N_EXPERTS, N_EXPERTS), 0)
    e_col = lax.broadcasted_iota(I32, (N_EXPERTS, N_EXPERTS), 1)
    earlier = jnp.where(e_col < e_row, 1.0, 0.0).astype(BF16)
    loff = _dot(earlier, runlen_b.astype(BF16))
    slot = _dot(member2.astype(BF16), tri_ref[...]) + loff[:, 0:1]
    slot = jnp.where(member2 > 0.0, slot, float(NO_SLOT))
    slot_hi = jnp.floor(slot * (1.0 / 256.0))
    slot_lo = slot - 256.0 * slot_hi
    gate3 = jnp.zeros(shape3, F32)
    for k in range(TOP_K):
        gate3 = jnp.where(eid == idxs[k], ws[k] / wsum * ROUTED_SCALE, gate3)
    gate = gate3.reshape(N_EXPERTS, tm)
    no_expert = jnp.zeros((LANES - N_EXPERTS, tm), F32)
    pad_e = lambda a: jnp.concatenate([a, no_expert], axis=0)
    slot_hi_ref[0] = pad_e(slot_hi).astype(BF16)
    slot_lo_ref[0] = pad_e(slot_lo).astype(BF16)
    slot_hi_t_ref[0] = pad_e(slot_hi).T.astype(BF16)
    slot_lo_t_ref[0] = pad_e(slot_lo).T.astype(BF16)
    gate_t_ref[0] = pad_e(gate).T.astype(BF16)
    run_lo = loff
    run_hi = loff + runlen_b
    no_run = jnp.zeros((LANES - N_EXPERTS, LANES), F32)
    lo_col_ref[0] = jnp.concatenate([run_lo, no_run], axis=0)
    hi_col_ref[0] = jnp.concatenate([run_hi, no_run], axis=0)
    diag = (lax.broadcasted_iota(I32, (N_EXPERTS, LANES), 0)
            == lax.broadcasted_iota(I32, (N_EXPERTS, LANES), 1))
    bounds_row_ref[0] = jnp.concatenate(
        [jnp.sum(jnp.where(diag, run_lo, 0.0), axis=0, keepdims=True),
         jnp.sum(jnp.where(diag, run_hi, 0.0), axis=0, keepdims=True),
         jnp.zeros((SUBLANES - 2, LANES), F32)], axis=0)
    big = 2.0 * RUN_ALIGN
    n_chunks = runlen_b * (1.0 / RUN_ALIGN)
    n_big = jnp.floor(n_chunks * 0.5)
    n_small = n_chunks - 2.0 * n_big
    before_big = _dot(earlier, n_big.astype(BF16))
    before_small = _dot(earlier, n_small.astype(BF16))
    gbase = carry_ref[...]
    j = lax.broadcasted_iota(I32, (N_EXPERTS, LANES), 1).astype(F32)
    e_iota = lax.broadcasted_iota(I32, (N_EXPERTS, LANES), 0).astype(F32)

    def entries(before, count, loc0, rel0, step):
        e_of = jnp.sum(jnp.where(before + count <= j, 1.0, 0.0), axis=0, keepdims=True)
        pick = lambda x: jnp.sum(jnp.where(e_iota == e_of, x, 0.0), axis=0, keepdims=True)
        k = j[0:1, :] - pick(before)
        return jnp.minimum(e_of, N_EXPERTS - 1.0), pick(loc0) + step * k, pick(rel0) + step * k

    e_b, loc_b, rel_b = entries(before_big, n_big, loff, gbase, big)
    e_s, loc_s, rel_s = entries(before_small, n_small, loff + big * n_big, gbase + big * n_big, 0.0)
    copy_ref[0] = jnp.concatenate([jnp.concatenate([e_b, e_s], axis=1),
                                   jnp.concatenate([loc_b, loc_s], axis=1),
                                   jnp.concatenate([rel_b, rel_s], axis=1)], axis=0).astype(I32)
    last = slice(N_EXPERTS - 1, N_EXPERTS)
    lane = lax.broadcasted_iota(I32, (1, LANES), 1)
    rows_used = loff[last, :] + runlen_b[last, :]
    counts = jnp.where(lane == 0, rows_used,
                       jnp.where(lane == 1, before_big[last, :] + n_big[last, :],
                                 before_small[last, :] + n_small[last, :]))
    rows_ref[0] = counts.astype(I32)
    total = carry_ref[...] + runlen_b
    carry_ref[...] = total
    cnt_ref[...] = total


def _router(x_flat, mod_lat, g, wr_t, e_bias, tri, tokens_per_batch):
    t, d = x_flat.shape
    tm = MOE_TILE
    n_tiles = t // tm
    tiles_per_batch = tokens_per_batch // tm
    body = functools.partial(_router_body, tm=tm)
    table = SDS((n_tiles, COPY_FIELDS, 2 * LANES), I32)
    table_spec = BS((1, COPY_FIELDS, 2 * LANES), lambda i: (i, 0, 0))
    expert_major = SDS((n_tiles, LANES, tm), BF16)
    expert_major_spec = BS((1, LANES, tm), lambda i: (i, 0, 0))
    token_major = SDS((n_tiles, tm, LANES), BF16)
    token_major_spec = BS((1, tm, LANES), lambda i: (i, 0, 0))
    per_expert = SDS((n_tiles, LANES, LANES), F32)
    per_expert_spec = BS((1, LANES, LANES), lambda i: (i, 0, 0))
    return pl.pallas_call(
        body,
        out_shape=(SDS((t, d), BF16),
                   expert_major, expert_major, token_major, token_major, token_major,
                   per_expert, per_expert, SDS((n_tiles, SUBLANES, LANES), F32),
                   table, SDS((n_tiles, 1, LANES), I32), SDS((N_EXPERTS, LANES), F32)),
        grid=(n_tiles,),
        in_specs=[BS((tm, d), lambda i: (i, 0)),
                  BS((1, SUBLANES, d), lambda i: (i // tiles_per_batch, 0, 0)),
                  BS((1, d), lambda i: (0, 0)),
                  BS((N_EXPERTS, d), lambda i: (0, 0)),
                  BS((N_EXPERTS, 1), lambda i: (0, 0)),
                  BS((tm, tm), lambda i: (0, 0))],
        out_specs=(BS((tm, d), lambda i: (i, 0)),
                   expert_major_spec, expert_major_spec, token_major_spec, token_major_spec,
                   token_major_spec, per_expert_spec, per_expert_spec,
                   BS((1, SUBLANES, LANES), lambda i: (i, 0, 0)),
                   table_spec, BS((1, 1, LANES), lambda i: (i, 0, 0)),
                   BS((N_EXPERTS, LANES), lambda i: (0, 0))),
        scratch_shapes=[pltpu.VMEM((N_EXPERTS, LANES), F32)],
        compiler_params=_cparams("arbitrary"),
        name="moe_router",
    )(x_flat, mod_lat, g, wr_t, e_bias, tri)


def _start_run_copies(pstart_ref, copy_ref, n_big, n_small, run_copy):
    def entry(base, nrows):
        def body(j, carry):
            slot0 = pstart_ref[copy_ref[0, 0, base + j]] + copy_ref[0, 2, base + j]
            run_copy(pl.multiple_of(copy_ref[0, 1, base + j], RUN_ALIGN),
                     pl.multiple_of(slot0, RUN_ALIGN), nrows).start()
            return carry
        return body

    lax.fori_loop(0, n_big, entry(0, 2 * RUN_ALIGN), 0)
    lax.fori_loop(0, n_small, entry(LANES, RUN_ALIGN), 0)


def _wait_run_rows(copy_of_rows, rows):
    def wait_n(nrows):
        def body(_, carry):
            copy_of_rows(nrows).wait()
            return carry
        return body

    lax.fori_loop(0, rows // SLOT_CHUNK, wait_n(SLOT_CHUNK), 0)
    lax.fori_loop(0, (rows % SLOT_CHUNK) // RUN_ALIGN, wait_n(RUN_ALIGN), 0)


def _dispatch_body(pend_ref, pcnt_ref, nu_ref, pstart_ref, tile_ref, copy_ref,
                   slot_hi_ref, slot_lo_ref, bounds_ref, h_ref, xs_ref, loc_ref, zbuf_ref, sem, zsem,
                   *, tl, bm, n_blocks):
    i = pl.program_id(0)
    last = pl.num_programs(0) - 1
    buf = i % 2

    def zero_copy(row0):
        return pltpu.make_async_copy(
            zbuf_ref, xs_ref.at[pl.ds(pl.multiple_of(row0, RUN_ALIGN), bm), :], zsem)

    @pl.when(i == 0)
    def _():
        zbuf_ref[...] = jnp.zeros_like(zbuf_ref)

        def start(e, c):
            @pl.when(pcnt_ref[e] > 0)
            def _():
                zero_copy(pend_ref[e] - bm).start()
            return c

        def wait(e, c):
            @pl.when(pcnt_ref[e] > 0)
            def _():
                zero_copy(pend_ref[e] - bm).wait()
            return c

        def start_tail(j, c):
            zero_copy(j * bm).start()
            return c

        def wait_tail(j, c):
            zero_copy(j * bm).wait()
            return c

        lax.fori_loop(0, N_EXPERTS, start, 0)
        lax.fori_loop(nu_ref[0], n_blocks, start_tail, 0)
        lax.fori_loop(0, N_EXPERTS, wait, 0)
        lax.fori_loop(nu_ref[0], n_blocks, wait_tail, 0)

    rows_used = tile_ref[i, 0]
    h = h_ref[...]
    run_lo = bounds_ref[0, 0:1, :]
    run_hi = bounds_ref[0, 1:2, :]

    def sort_chunk(c, carry):
        row0 = (c * SLOT_CHUNK).astype(F32)
        row_e = lax.broadcasted_iota(I32, (SLOT_CHUNK, LANES), 0).astype(F32) + row0
        in_run = jnp.where(row_e >= run_lo, jnp.where(row_e < run_hi, 1.0, 0.0), 0.0).astype(BF16)
        slot_of_token = (256.0 * _dot(in_run, slot_hi_ref[0]) + _dot(in_run, slot_lo_ref[0]))
        row_t = lax.broadcasted_iota(I32, (SLOT_CHUNK, tl), 0).astype(F32) + row0
        onehot = jnp.where(slot_of_token == row_t, 1.0, 0.0).astype(BF16)
        rows = pl.ds(pl.multiple_of(c * SLOT_CHUNK, SLOT_CHUNK), SLOT_CHUNK)
        loc_ref[buf, rows, :] = _dot(onehot, h).astype(BF16)
        return carry

    lax.fori_loop(0, (rows_used + SLOT_CHUNK - 1) // SLOT_CHUNK, sort_chunk, 0)

    def run_copy(b, loc0, slot0, nrows=RUN_ALIGN):
        return pltpu.make_async_copy(loc_ref.at[b, pl.ds(loc0, nrows), :],
                                     xs_ref.at[pl.ds(slot0, nrows), :], sem.at[b])

    _start_run_copies(pstart_ref, copy_ref, tile_ref[i, 1], tile_ref[i, 2],
                      lambda loc0, slot0, n: run_copy(buf, loc0, slot0, n))

    @pl.when(i > 0)
    def _():
        _wait_run_rows(lambda n: run_copy(1 - buf, 0, 0, n), tile_ref[jnp.maximum(i - 1, 0), 0])

    @pl.when(i == last)
    def _():
        _wait_run_rows(lambda n: run_copy(buf, 0, 0, n), rows_used)


def _dispatch(pend, pcnt, n_used, pstart, tile_counts, copy_table, slot_hi, slot_lo, bounds_row,
              h2b, n_slots):
    t, d = h2b.shape
    tl = MOE_TILE
    body = functools.partial(_dispatch_body, tl=tl, bm=BM_FFN, n_blocks=n_slots // BM_FFN)
    grid_spec = pltpu.PrefetchScalarGridSpec(
        num_scalar_prefetch=5,
        grid=(t // tl,),
        in_specs=[BS((1, COPY_FIELDS, 2 * LANES), lambda i, *_: (i, 0, 0), memory_space=pltpu.SMEM),
                  BS((1, LANES, tl), lambda i, *_: (i, 0, 0)),
                  BS((1, LANES, tl), lambda i, *_: (i, 0, 0)),
                  BS((1, SUBLANES, LANES), lambda i, *_: (i, 0, 0)),
                  BS((tl, d), lambda i, *_: (i, 0))],
        out_specs=BS(memory_space=pl.ANY),
        scratch_shapes=[pltpu.VMEM((2, LOCAL_SLOTS, d), BF16), pltpu.VMEM((BM_FFN, d), BF16),
                        pltpu.SemaphoreType.DMA((2,)), pltpu.SemaphoreType.DMA(())],
    )
    return pl.pallas_call(
        body,
        out_shape=SDS((n_slots, d), BF16),
        grid_spec=grid_spec,
        compiler_params=_cparams("arbitrary"),
        name="moe_dispatch",
    )(pend, pcnt, n_used, pstart, tile_counts, copy_table, slot_hi, slot_lo, bounds_row, h2b)


def _ffn_body(first_ref, count_ref, nu_ref, xs_ref, wg_ref, wu_ref, wd_ref, ys_ref, wgb_ref, wub_ref,
              wdb_ref, xbuf_ref, ybuf_ref, isem, osem, *, bm, n_blocks):
    e = pl.program_id(0)
    nu = nu_ref[0]

    def block_rows(blk):
        return pl.ds(pl.multiple_of(blk * bm, bm), bm)

    def in_copy(blk, slot):
        return pltpu.make_async_copy(xs_ref.at[block_rows(blk), :], xbuf_ref.at[slot], isem.at[slot])

    def out_copy(blk, slot):
        return pltpu.make_async_copy(ybuf_ref.at[slot], ys_ref.at[block_rows(blk), :], osem.at[slot])

    @pl.when(e == 0)
    def _():
        for s in range(FFN_IN_BUFS):
            @pl.when(s < nu)
            def _():
                in_copy(s, s).start()

    @pl.when(count_ref[e] > 0)
    def _():
        wgb_ref[...] = wg_ref[...].astype(BF16)
        wub_ref[...] = wu_ref[...].astype(BF16)
        wdb_ref[...] = wd_ref[...].astype(BF16)

    def one_block(b, carry):
        i = first_ref[e] + b
        slot = i % FFN_IN_BUFS
        oslot = i % 2
        in_copy(i, slot).wait()
        x = xbuf_ref[slot]
        a = _silu(_dot(x, wgb_ref[...])) * _dot(x, wub_ref[...])
        y = _dot(a.astype(BF16), wdb_ref[...]).astype(BF16)

        @pl.when(i >= 2)
        def _():
            out_copy(i - 2, oslot).wait()

        ybuf_ref[oslot] = y
        out_copy(i, oslot).start()

        @pl.when(i + FFN_IN_BUFS < nu)
        def _():
            in_copy(i + FFN_IN_BUFS, slot).start()

        return carry

    lax.fori_loop(0, count_ref[e], one_block, 0)

    @pl.when(e == pl.num_programs(0) - 1)
    def _():
        @pl.when(nu >= 2)
        def _():
            out_copy(nu - 2, (nu - 2) % 2).wait()

        out_copy(nu - 1, (nu - 1) % 2).wait()
        ybuf_ref[0] = jnp.zeros((bm, ybuf_ref.shape[-1]), BF16)

        def start_tail(j, c):
            out_copy(j, 0).start()
            return c

        def wait_tail(j, c):
            out_copy(j, 0).wait()
            return c

        lax.fori_loop(nu, n_blocks, start_tail, 0)
        lax.fori_loop(nu, n_blocks, wait_tail, 0)


def _ffn(first_block, block_count, n_used, xs, w_gate, w_up, w_down, layer):
    n_slots, d = xs.shape
    bm = BM_FFN
    de = w_gate.shape[-1]
    n_blocks = n_slots // bm
    body = functools.partial(_ffn_body, bm=bm, n_blocks=n_blocks)
    grid_spec = pltpu.PrefetchScalarGridSpec(
        num_scalar_prefetch=3,
        grid=(N_EXPERTS,),
        in_specs=[BS(memory_space=pl.ANY),
                  BS((None, None, d, de), lambda e, *_: (layer, e, 0, 0)),
                  BS((None, None, d, de), lambda e, *_: (layer, e, 0, 0)),
                  BS((None, None, de, d), lambda e, *_: (layer, e, 0, 0))],
        out_specs=BS(memory_space=pl.ANY),
        scratch_shapes=[pltpu.VMEM((d, de), BF16), pltpu.VMEM((d, de), BF16),
                        pltpu.VMEM((de, d), BF16),
                        pltpu.VMEM((FFN_IN_BUFS, bm, d), BF16), pltpu.VMEM((2, bm, d), BF16),
                        pltpu.SemaphoreType.DMA((FFN_IN_BUFS,)), pltpu.SemaphoreType.DMA((2,))],
    )
    return pl.pallas_call(
        body,
        out_shape=SDS((n_slots, d), BF16),
        grid_spec=grid_spec,
        compiler_params=_cparams("arbitrary"),
        name="moe_experts",
    )(first_block, block_count, n_used, xs, w_gate, w_up, w_down)


def _combine_body(pstart_ref, tile_ref, copy_ref, next_copy_ref, ys_ref,
                  slot_hi_ref, slot_lo_ref, gate_t_ref, lo_col_ref, hi_col_ref,
                  x_ref, h2b_ref, mod_ref, wsg_ref, wsu_ref, wsd_ref, fg_ref,
                  o_ref, loc_ref, acc_ref, sem, *, tl, final):
    i = pl.program_id(0)
    last = pl.num_programs(0) - 1
    buf = i % 2
    rows_used = tile_ref[i, 0]
    nxt = jnp.minimum(i + 1, last)

    def run_copy(b, loc0, slot0, nrows=RUN_ALIGN):
        return pltpu.make_async_copy(ys_ref.at[pl.ds(slot0, nrows), :],
                                     loc_ref.at[b, pl.ds(loc0, nrows), :], sem.at[b])

    @pl.when(i == 0)
    def _():
        loc_ref[...] = jnp.zeros_like(loc_ref)
        _start_run_copies(pstart_ref, copy_ref, tile_ref[i, 1], tile_ref[i, 2],
                          lambda loc0, slot0, n: run_copy(buf, loc0, slot0, n))

    @pl.when(i < last)
    def _():
        _start_run_copies(pstart_ref, next_copy_ref, tile_ref[nxt, 1], tile_ref[nxt, 2],
                          lambda loc0, slot0, n: run_copy(1 - buf, loc0, slot0, n))

    hb = h2b_ref[...]
    a = _silu(_dot(hb, wsg_ref[...])) * _dot(hb, wsu_ref[...])
    acc_ref[...] = _dot(a.astype(BF16), wsd_ref[...])
    run_lo = lo_col_ref[0][:, 0:1]
    run_hi = hi_col_ref[0][:, 0:1]

    _wait_run_rows(lambda n: run_copy(buf, 0, 0, n), rows_used)

    def unsort_chunk(c, carry):
        col0 = (c * SLOT_CHUNK).astype(F32)
        col_e = lax.broadcasted_iota(I32, (LANES, SLOT_CHUNK), 1).astype(F32) + col0
        in_run = jnp.where(col_e >= run_lo, jnp.where(col_e < run_hi, 1.0, 0.0), 0.0).astype(BF16)
        slot_of_token = (256.0 * _dot(slot_hi_ref[0], in_run) + _dot(slot_lo_ref[0], in_run))
        gate_of_token = _dot(gate_t_ref[0], in_run)
        col_t = lax.broadcasted_iota(I32, (tl, SLOT_CHUNK), 1).astype(F32) + col0
        gate = jnp.where(slot_of_token == col_t, gate_of_token, 0.0).astype(BF16)
        y = loc_ref[buf, pl.ds(pl.multiple_of(c * SLOT_CHUNK, SLOT_CHUNK), SLOT_CHUNK), :]
        acc_ref[...] = acc_ref[...] + _dot(gate, y)
        return carry

    lax.fori_loop(0, (rows_used + SLOT_CHUNK - 1) // SLOT_CHUNK, unsort_chunk, 0)

    xo = x_ref[...] + mod_ref[0, 5:6, :] * acc_ref[...]
    if final:
        ms = jnp.mean(xo * xo, axis=-1, keepdims=True)
        xo = (xo * lax.rsqrt(ms + EPS)) * fg_ref[...]
    o_ref[...] = xo


def _combine(pstart, tile_counts, copy_table, ys, slot_hi_t, slot_lo_t, gate_t, lo_col, hi_col,
             x_flat, h2b, mod_lat, wsg, wsu, wsd, final_g, tokens_per_batch, final):
    t, d = x_flat.shape
    tl = MOE_TILE
    ds = wsg.shape[1]
    tiles_per_batch = tokens_per_batch // tl
    body = functools.partial(_combine_body, tl=tl, final=final)
    n_tiles = t // tl
    table_spec = lambda index: BS((1, COPY_FIELDS, 2 * LANES), index, memory_space=pltpu.SMEM)
    this_tile = lambda i, *_: (i, 0, 0)
    next_tile = lambda i, *_: (jnp.minimum(i + 1, n_tiles - 1), 0, 0)
    grid_spec = pltpu.PrefetchScalarGridSpec(
        num_scalar_prefetch=2,
        grid=(n_tiles,),
        in_specs=[table_spec(this_tile), table_spec(next_tile),
                  BS(memory_space=pl.ANY),
                  BS((1, tl, LANES), this_tile), BS((1, tl, LANES), this_tile),
                  BS((1, tl, LANES), this_tile),
                  BS((1, LANES, LANES), this_tile), BS((1, LANES, LANES), this_tile),
                  BS((tl, d), lambda i, *_: (i, 0)),
                  BS((tl, d), lambda i, *_: (i, 0)),
                  BS((1, SUBLANES, d), lambda i, *_: (i // tiles_per_batch, 0, 0)),
                  BS((d, ds), lambda i, *_: (0, 0)),
                  BS((d, ds), lambda i, *_: (0, 0)),
                  BS((ds, d), lambda i, *_: (0, 0)),
                  BS((1, d), lambda i, *_: (0, 0))],
        out_specs=BS((tl, d), lambda i, *_: (i, 0)),
        scratch_shapes=[pltpu.VMEM((2, LOCAL_SLOTS, d), BF16), pltpu.VMEM((tl, d), F32),
                        pltpu.SemaphoreType.DMA((2,))],
    )
    return pl.pallas_call(
        body,
        out_shape=SDS((t, d), F32),
        grid_spec=grid_spec,
        compiler_params=_cparams("arbitrary"),
        name="moe_combine",
    )(pstart, tile_counts, copy_table, copy_table, ys, slot_hi_t, slot_lo_t, gate_t,
      lo_col, hi_col, x_flat, h2b, mod_lat, wsg, wsu, wsd, final_g)


def _moe(x_flat, mod_lat, g, w_router, e_bias, w_gate, w_up, w_down, ws_gate, ws_up, ws_down,
         final_g, tri, tokens_per_batch, layer, final):
    t, d = x_flat.shape
    bm = BM_FFN
    n_tiles = t // MOE_TILE
    (h2b, slot_hi, slot_lo, slot_hi_t, slot_lo_t, gate_t, lo_col, hi_col, bounds_row, copy_table,
     tile_counts, seg) = _router(x_flat, mod_lat, g, w_router.T, e_bias.reshape(N_EXPERTS, 1), tri,
                                 tokens_per_batch)
    tile_counts = tile_counts[:, 0, :SUBLANES]
    seg_rows = seg[:, 0].astype(I32)
    pcnt = (seg_rows + bm - 1) // bm * bm
    pend = jnp.cumsum(pcnt).astype(I32)
    pstart = pend - pcnt
    max_rows = t * TOP_K + n_tiles * N_EXPERTS * (RUN_ALIGN - 1)
    n_blocks = -(-max_rows // bm) + N_EXPERTS
    n_slots = n_blocks * bm
    n_used = pend[-1:] // bm
    xs = _dispatch(pend, pcnt, n_used, pstart, tile_counts, copy_table, slot_hi, slot_lo,
                   bounds_row, h2b, n_slots)
    ys = _ffn(pstart // bm, pcnt // bm, n_used, xs, w_gate, w_up, w_down, layer)
    return _combine(pstart, tile_counts, copy_table, ys, slot_hi_t, slot_lo_t, gate_t, lo_col,
                    hi_col, x_flat, h2b, mod_lat,
                    ws_gate.astype(BF16), ws_up.astype(BF16), ws_down.astype(BF16),
                    final_g.reshape(1, d), tokens_per_batch, final)


def _inproj_c_body(xp_ref, x_ref, xn_ref, mod_ref, g_ref, w_ref, cw_ref, cb_ref,
                   v_ref, g1_ref, g2_ref, *, tm, n_tiles):
    i = pl.program_id(1)
    halo = SUBLANES
    xe = jnp.concatenate([xp_ref[0], x_ref[0], xn_ref[0]], axis=0)
    h = _norm_mod(xe, g_ref[...], mod_ref[0, 1:2, :], mod_ref[0, 0:1, :])
    row = lax.broadcasted_iota(I32, (tm + 2 * halo, 1), 0)
    outside = jnp.logical_or(jnp.logical_and(i == 0, row < halo),
                             jnp.logical_and(i == n_tiles - 1, row >= tm + halo))
    hb = jnp.where(outside, 0.0, h).astype(BF16)
    width = v_ref.shape[-1]
    for part, o_ref in enumerate((v_ref, g1_ref, g2_ref)):
        cols = slice(part * width, (part + 1) * width)
        zp = _dot(hb, w_ref[:, cols])
        up = pltpu.roll(zp, 1, 0)
        dn = pltpu.roll(zp, tm + 2 * halo - 1, 0)
        z = cw_ref[0:1, cols] * up + cw_ref[1:2, cols] * zp + cw_ref[2:3, cols] * dn + cb_ref[:, cols]
        o_ref[0] = z[halo:halo + tm]


def _inproj_c(x, mod_lat, g, w_in, conv_w, conv_b):
    b, l, d = x.shape
    tm = TM_PROJ
    n_tiles = l // tm
    w3 = w_in.shape[1]
    width = w3 // 3
    r8 = tm // SUBLANES
    body = functools.partial(_inproj_c_body, tm=tm, n_tiles=n_tiles)
    out = SDS((b, l, width), F32)
    return pl.pallas_call(
        body,
        out_shape=(out, out, out),
        grid=(b, n_tiles),
        in_specs=[BS((1, SUBLANES, d), lambda bi, i: (bi, jnp.maximum(i * r8 - 1, 0), 0)),
                  BS((1, tm, d), lambda bi, i: (bi, i, 0)),
                  BS((1, SUBLANES, d), lambda bi, i: (bi, jnp.minimum((i + 1) * r8, l // SUBLANES - 1), 0)),
                  BS((1, SUBLANES, d), lambda bi, i: (bi, 0, 0)),
                  BS((1, d), lambda bi, i: (0, 0)),
                  BS((d, w3), lambda bi, i: (0, 0)),
                  BS((3, w3), lambda bi, i: (0, 0)),
                  BS((1, w3), lambda bi, i: (0, 0))],
        out_specs=(BS((1, tm, width), lambda bi, i: (bi, i, 0)),
                   BS((1, tm, width), lambda bi, i: (bi, i, 0)),
                   BS((1, tm, width), lambda bi, i: (bi, i, 0))),
        compiler_params=_cparams("arbitrary", "arbitrary"),
        name="inproj_c",
    )(x, x, x, mod_lat, g, w_in, conv_w, conv_b)


def _filter_body(f_ref, w1_ref, b1_ref, w2_ref, b2_ref, w3hi_ref, w3lo_ref, fr_ref, dl_ref,
                 k_ref, l1_ref, *, tp, n):
    i = pl.program_id(0)
    feats = f_ref[...]
    fr = fr_ref[...]
    a = jnp.sin(fr * (_dot_hp(feats, w1_ref[...]) + b1_ref[...]))
    a = jnp.sin(fr * (_dot_hp(a, w2_ref[...]) + b2_ref[...]))
    a_hi = a.astype(BF16)
    a_lo = (a - a_hi.astype(F32)).astype(BF16)
    w3_hi = w3hi_ref[...]
    hf = _dot(a_hi, w3_hi) + _dot(a_hi, w3lo_ref[...]) + _dot(a_lo, w3_hi)
    t01 = feats[:, 0:1]
    hf = hf * (jnp.exp(-t01 * jnp.abs(dl_ref[...])) + DECAY_SHIFT)
    row = lax.broadcasted_iota(I32, hf.shape, 0) + i * tp
    hf = jnp.where(row == n, 0.0, hf)
    k_ref[...] = hf

    @pl.when(i == 0)
    def _():
        l1_ref[...] = jnp.zeros_like(l1_ref)

    l1_ref[...] = l1_ref[...] + jnp.sum(jnp.abs(hf), axis=0, keepdims=True)


def _filters(feats, w1, b1, w2, b2, w3, freq, delta, width):
    n2, fe = feats.shape
    n = n2 // 2
    hid = w2.shape[0]
    tp = 256
    half_tiles = n // tp
    body = functools.partial(_filter_body, tp=tp, n=n)
    full = lambda shape: BS(shape, lambda i: (0, 0))
    by_direction = lambda a: a.reshape(a.shape[0], HYENA_ORDER, 2, width).transpose(2, 0, 1, 3).reshape(
        2, a.shape[0], HYENA_ORDER * width)
    w3_d = by_direction(w3)
    w3_hi = w3_d.astype(BF16)
    w3_lo = (w3_d - w3_hi.astype(F32)).astype(BF16)
    delta_d = by_direction(delta)
    fo = HYENA_ORDER * width
    direction = lambda i: (i // half_tiles, 0, 0)
    return pl.pallas_call(
        body,
        out_shape=(SDS((n2, fo), F32), SDS((SUBLANES, fo), F32)),
        grid=(n2 // tp,),
        in_specs=[BS((tp, fe), lambda i: (i, 0)), full((fe, hid)), full((1, hid)),
                  full((hid, hid)), full((1, hid)),
                  BS((None, hid, fo), direction), BS((None, hid, fo), direction), full((1, hid)),
                  BS((None, 1, fo), direction)],
        out_specs=(BS((tp, fo), lambda i: (i, 0)), BS((SUBLANES, fo), lambda i: (0, 0))),
        compiler_params=_cparams("arbitrary"),
        name="hyena_filters",
    )(feats, w1, b1, w2, b2, w3_hi, w3_lo, freq, delta_d)


DFT_R = 128
DFT_VP = 72
DFT_BGROUP = 16
DFT_LANE_TILES = 4


def _dft_tables(n):
    r = DFT_R
    vp = DFT_VP
    m = 2 * n
    na = n // r
    two_pi = 2.0 * np.pi
    live = (np.arange(vp) <= r // 2).astype(np.float64)
    a = np.arange(2 * na)[None, :]
    v = np.arange(vp)[:, None]
    ang1 = two_pi * ((a * v) % r) / r
    f1_full = np.concatenate([np.cos(ang1), -np.sin(ang1)], axis=0) * np.tile(live, 2)[:, None]
    f1 = f1_full[:, :na]
    b = np.arange(r)[None, None, :]
    u = np.arange(r)[None, :, None]
    vv = np.arange(vp)[:, None, None]
    ang2 = two_pi * ((b * (r * u + vv)) % m) / m
    gr, gi = np.cos(ang2), -np.sin(ang2)
    fwd = np.concatenate([np.concatenate([gr, -gi], axis=2),
                          np.concatenate([gi, gr], axis=2)], axis=1)
    hr, hi = np.transpose(gr, (0, 2, 1)), -np.transpose(gi, (0, 2, 1))
    inv = np.concatenate([np.concatenate([hr, -hi], axis=2),
                          np.concatenate([hi, hr], axis=2)], axis=1)
    weight = live * np.where((np.arange(vp) == 0) | (np.arange(vp) == r // 2), 1.0, 2.0)
    ang3 = two_pi * ((np.arange(na)[:, None] * np.arange(vp)[None, :]) % r) / r
    f3 = np.concatenate([np.cos(ang3) * weight, -np.sin(ang3) * weight], axis=1) / m
    cast = lambda t: jnp.asarray(t.astype(np.float32)).astype(BF16)
    return cast(f1), cast(f1_full), cast(fwd), cast(inv), cast(f3)


def _lane_tile_specs(rows, index_map_of_tile):
    return [BS((None, rows, DFT_BGROUP, LANES), index_map_of_tile(t)) for t in range(DFT_LANE_TILES)]


def _rows_of_position(ref, j):
    x, bg, _ = ref.shape
    return ref.reshape(x * bg, LANES)[pl.ds(j, x, stride=bg), :]


def _dft_s1_body(*refs):
    q = DFT_LANE_TILES
    y_refs, f1_ref, ar_ref, ai_ref = refs[:q], refs[q], refs[q + 1], refs[q + 2]
    f1 = f1_ref[...]
    bg = DFT_BGROUP
    for t in range(q):
        ar2 = ar_ref.at[t].reshape(DFT_VP * bg, LANES)
        ai2 = ai_ref.at[t].reshape(DFT_VP * bg, LANES)
        for j in range(bg):
            res = _dot(f1, _rows_of_position(y_refs[t], j).astype(BF16))
            ar2[pl.ds(j, DFT_VP, stride=bg), :] = res[:DFT_VP]
            ai2[pl.ds(j, DFT_VP, stride=bg), :] = res[DFT_VP:]


def _dft_s1(y4, f1):
    nb, na, r, c = y4.shape
    q, bg = DFT_LANE_TILES, DFT_BGROUP
    out = SDS((nb, c // LANES, DFT_VP, r, LANES), F32)
    plane_spec = BS((None, q, DFT_VP, bg, LANES), lambda i, j, cc: (i, cc, 0, j, 0))
    tile_map = lambda t: (lambda i, j, cc: (i, 0, j, cc * q + t))
    return pl.pallas_call(
        _dft_s1_body,
        out_shape=(out, out),
        grid=(nb, r // bg, c // (q * LANES)),
        in_specs=_lane_tile_specs(na, tile_map) + [BS((2 * DFT_VP, na), lambda i, j, cc: (0, 0))],
        out_specs=(plane_spec, plane_spec),
        compiler_params=_cparams("arbitrary", "arbitrary", "arbitrary"),
        name="dft_stage1",
    )(*([y4] * q), f1)


def _plane_rows(re_ref, im_ref):
    wide = lambda ref: jnp.concatenate([ref[t] for t in range(ref.shape[0])], axis=1)
    return jnp.concatenate([wide(re_ref), wide(im_ref)], axis=0).astype(BF16)


def _filter_spec_body(ar_ref, ai_ref, g_ref, l1_ref, kr_ref, ki_ref):
    y = _dot(g_ref[...], _plane_rows(ar_ref, ai_ref))
    inv = 1.0 / l1_ref[0:1, :]
    kr_ref[...] = (y[:DFT_R] * inv).astype(BF16)
    ki_ref[...] = (y[DFT_R:] * inv).astype(BF16)


def _filter_spectrum(ar, ai, fwd, l1, width):
    r = DFT_R
    a_spec = BS((None, width // LANES, None, r, LANES), lambda v, o: (0, o, v, 0, 0))
    k_spec = BS((None, r, width), lambda v, o: (v, 0, o))
    out = SDS((DFT_VP, r, HYENA_ORDER * width), BF16)
    return pl.pallas_call(
        _filter_spec_body,
        out_shape=(out, out),
        grid=(DFT_VP, HYENA_ORDER),
        in_specs=[a_spec, a_spec, BS((None, 2 * r, 2 * r), lambda v, o: (v, 0, 0)),
                  BS((SUBLANES, width), lambda v, o: (0, o))],
        out_specs=(k_spec, k_spec),
        compiler_params=_cparams("arbitrary", "arbitrary"),
        name="hyena_filter_spectrum",
    )(ar, ai, fwd, l1)


def _conv_mid_body(ar_ref, ai_ref, g_ref, h_ref, kr_ref, ki_ref, qr_ref, qi_ref):
    y = _dot(g_ref[...], _plane_rows(ar_ref, ai_ref))
    yr, yi = y[:DFT_R], y[DFT_R:]
    kr, ki = kr_ref[...].astype(F32), ki_ref[...].astype(F32)
    p = jnp.concatenate([yr * kr - yi * ki, yr * ki + yi * kr], axis=0).astype(BF16)
    q = _dot(h_ref[...], p)
    for t in range(qr_ref.shape[0]):
        lanes = slice(t * LANES, (t + 1) * LANES)
        qr_ref[t] = q[:DFT_R, lanes]
        qi_ref[t] = q[DFT_R:, lanes]


def _conv_mid(ar, ai, fwd, inv, kr, ki, order):
    nb, tiles, vp, r, _ = ar.shape
    c = tiles * LANES
    a_spec = BS((None, tiles, None, r, LANES), lambda n, v: (n, 0, v, 0, 0))
    m_spec = BS((None, 2 * r, 2 * r), lambda n, v: (v, 0, 0))
    k_spec = BS((None, r, c), lambda n, v: (v, 0, order))
    out = SDS((nb, tiles, vp, r, LANES), F32)
    return pl.pallas_call(
        _conv_mid_body,
        out_shape=(out, out),
        grid=(nb, vp),
        in_specs=[a_spec, a_spec, m_spec, m_spec, k_spec, k_spec],
        out_specs=(a_spec, a_spec),
        compiler_params=_cparams("arbitrary", "arbitrary"),
        name="hyena_spectral_product",
    )(ar, ai, fwd, inv, kr, ki)


def _idft_gate_body(*refs):
    q = DFT_LANE_TILES
    qr_ref, qi_ref = refs[0], refs[1]
    y_refs, gate_refs = refs[2:2 + q], refs[2 + q:2 + 2 * q]
    f3_ref, fb_ref, o_ref = refs[2 + 2 * q:]
    f3 = f3_ref[...]
    for t in range(q):
        lanes = slice(t * LANES, (t + 1) * LANES)
        fb = fb_ref[:, lanes]
        for j in range(DFT_BGROUP):
            planes = jnp.concatenate([_rows_of_position(qr_ref.at[t], j),
                                      _rows_of_position(qi_ref.at[t], j)], axis=0).astype(BF16)
            conv = _dot(f3, planes)
            o_ref[:, j, lanes] = _rows_of_position(gate_refs[t], j) * (
                conv + fb * _rows_of_position(y_refs[t], j))


def _idft_gate(qr, qi, f3, y4, gate4, fbias):
    nb, na, r, c = y4.shape
    q, bg = DFT_LANE_TILES, DFT_BGROUP
    tile_map = lambda t: (lambda i, j, cc: (i, 0, j, cc * q + t))
    wide = lambda rows: BS((None, rows, bg, q * LANES), lambda i, j, cc: (i, 0, j, cc))
    plane_spec = BS((None, q, DFT_VP, bg, LANES), lambda i, j, cc: (i, cc, 0, j, 0))
    return pl.pallas_call(
        _idft_gate_body,
        out_shape=SDS((nb, na, r, c), F32),
        grid=(nb, r // bg, c // (q * LANES)),
        in_specs=([plane_spec, plane_spec]
                  + _lane_tile_specs(na, tile_map) + _lane_tile_specs(na, tile_map)
                  + [BS((na, 2 * DFT_VP), lambda i, j, cc: (0, 0)),
                     BS((1, q * LANES), lambda i, j, cc: (0, cc))]),
        out_specs=wide(na),
        compiler_params=_cparams("arbitrary", "arbitrary", "arbitrary"),
        name="hyena_idft_gate",
    )(qr, qi, *([y4] * q + [gate4] * q), f3, fbias)


def _outproj_body(y_ref, w_ref, x_ref, mod_ref, o_ref):
    o_ref[0] = x_ref[0] + mod_ref[0, 2:3, :] * _dot(y_ref[0].astype(BF16), w_ref[...])


def _outproj(y, w_out, x, mod_lat):
    b, l, d = x.shape
    tm = TM_PROJ
    wdt = y.shape[-1]
    return pl.pallas_call(
        _outproj_body,
        out_shape=SDS((b, l, d), F32),
        grid=(b, l // tm),
        in_specs=[BS((1, tm, wdt), lambda bi, i: (bi, i, 0)),
                  BS((wdt, d), lambda bi, i: (0, 0)),
                  BS((1, tm, d), lambda bi, i: (bi, i, 0)),
                  BS((1, SUBLANES, d), lambda bi, i: (bi, 0, 0))],
        out_specs=BS((1, tm, d), lambda bi, i: (bi, i, 0)),
        compiler_params=_cparams("arbitrary", "arbitrary"),
        name="outproj_c",
    )(y, w_out, x, mod_lat)


def _hyena(x, mod_lat, g, w_in, conv_w, conv_b, w1, b1, w2, b2, w3, freq, delta, f_bias, w_out):
    b, n, d = x.shape
    width = w_out.shape[0]
    r = DFT_R
    na = n // r
    f1, f1_full, fwd, inv, f3 = _dft_tables(n)
    v, gate1, gate2 = _inproj_c(x, mod_lat, g, w_in.astype(BF16), conv_w, conv_b.reshape(1, -1))

    pos = jnp.arange(2 * n, dtype=F32)
    t = jnp.where(pos < n, pos, 2 * n - pos)
    t01 = t / max(n - 1, 1)
    bands = jnp.linspace(1e-4, FILT_BANDS - 1, FILT_BANDS, dtype=F32)
    ang = (2.0 * math.pi / n) * t[:, None] * bands[None, :]
    feats = jnp.concatenate([t01[:, None], jnp.cos(ang), jnp.sin(ang)], axis=-1)
    fe = feats.shape[1]
    feats = jnp.pad(feats, ((0, 0), (0, LANES - fe)))
    w1p = jnp.pad(w1, ((0, LANES - fe), (0, 0)))
    k_circ, l1 = _filters(feats, w1p, b1.reshape(1, -1), w2, b2.reshape(1, -1), w3,
                          freq.reshape(1, -1), delta.reshape(1, -1), width)
    far, fai = _dft_s1(k_circ.reshape(1, 2 * na, r, k_circ.shape[1]), f1_full)
    kr, ki = _filter_spectrum(far, fai, fwd, l1, width)

    y4 = v.reshape(b, na, r, width)
    for o, gate in enumerate((gate1, gate2)):
        ar, ai = _dft_s1(y4, f1)
        qr, qi = _conv_mid(ar, ai, fwd, inv, kr, ki, o)
        y4 = _idft_gate(qr, qi, f3, y4, gate.reshape(b, na, r, width), f_bias[o].reshape(1, width))
    return _outproj(y4.reshape(b, n, width), w_out.astype(BF16), x, mod_lat)


def _rope_tables(seq_len):
    rows = seq_len // GRID_W
    row = jnp.repeat(jnp.arange(rows, dtype=F32), GRID_W)
    col = jnp.tile(jnp.arange(GRID_W, dtype=F32), rows)
    inv = jnp.power(ROPE_BASE, -jnp.arange(ROPE_FREQS, dtype=F32) / ROPE_FREQS)
    ar, ac = row[:, None] * inv, col[:, None] * inv
    cos_h = jnp.concatenate([jnp.cos(ar), jnp.cos(ar), jnp.cos(ac), jnp.cos(ac)], axis=1)
    sin_h = jnp.concatenate([-jnp.sin(ar), jnp.sin(ar), -jnp.sin(ac), jnp.sin(ac)], axis=1)
    reps = LANES // HEAD_DIM
    return jnp.tile(cos_h, (1, reps)), jnp.tile(sin_h, (1, reps))


def _rotate_partner_columns(w):
    ncol = w.shape[1]
    lane = np.arange(ncol)
    partner = np.where((lane % (2 * ROPE_FREQS)) < ROPE_FREQS, lane + ROPE_FREQS, lane - ROPE_FREQS)
    return w[:, partner]


def kernel(x, c, ctx, c_ctx, w_mod, b_mod, norm_g, w_in_ab, sink, w_spatial, b_spatial, w_out_ab,
           w_in_c, conv_w, conv_b, filt_w1, filt_b1, filt_w2, filt_b2, filt_w3, filt_freq,
           filt_delta, filt_bias, w_out_c, w_router, e_bias, w_gate, w_up, w_down, ws_gate,
           ws_up, ws_down, final_g):
    b, l, d = x.shape
    depth = w_mod.shape[0]
    assert depth == 2 and b + 1 <= SUBLANES

    cc = jnp.zeros((SUBLANES, d), F32).at[:b].set(c).at[b].set(c_ctx)
    m_all = _mod_vectors(cc, w_mod, b_mod)

    def mod_rows(layer, row0, nrow):
        m = m_all[layer, row0:row0 + nrow].reshape(nrow, 6, d)
        return jnp.pad(m, ((0, 0), (0, SUBLANES - 6), (0, 0)))

    tri = jnp.triu(jnp.ones((MOE_TILE, MOE_TILE), F32), k=1).astype(BF16)

    mod_lat = mod_rows(0, 0, b)
    mod_ctx = mod_rows(0, b, 1)[0]
    w_in = w_in_ab[0]
    qk = ATTN_WIDTH + KV_WIDTH
    w_cat = jnp.concatenate([w_in, _rotate_partner_columns(w_in[:, :qk])], axis=1).astype(BF16)
    cos_t, sin_t = _rope_tables(l)
    group_avg = jnp.kron(jnp.eye(N_SG_GROUPS, dtype=F32),
                         jnp.full((SG_GROUP_DIM, SG_GROUP_DIM), 1.0 / SG_GROUP_DIM, F32)).astype(BF16)
    kc, vc = _ctx_kv(ctx, mod_ctx, norm_g[0, 0].reshape(1, d),
                     w_in[:, ATTN_WIDTH:ATTN_WIDTH + 2 * KV_WIDTH].astype(BF16))
    q, k, v, ug, vn = _inproj_ab(x, mod_lat, norm_g[0, 0].reshape(1, d), w_cat, cos_t, sin_t, group_avg)
    b_full = jnp.repeat(b_spatial[0].T, SG_GROUP_DIM, axis=1)
    x1 = _mixer(sink[0], q, k, v, kc, vc, ug, vn, w_spatial[0].astype(BF16), b_full,
                w_out_ab[0].astype(BF16), x, mod_lat)
    x2 = _moe(x1.reshape(b * l, d), mod_lat, norm_g[0, 1].reshape(1, d), w_router[0], e_bias[0],
              w_gate, w_up, w_down, ws_gate[0], ws_up[0], ws_down[0], final_g, tri, l,
              layer=0, final=False).reshape(b, l, d)

    mod_lat = mod_rows(1, 0, b)
    x3 = _hyena(x2, mod_lat, norm_g[1, 0].reshape(1, d), w_in_c[0], conv_w[0], conv_b[0],
                filt_w1[0], filt_b1[0], filt_w2[0], filt_b2[0], filt_w3[0], filt_freq[0],
                filt_delta[0], filt_bias[0], w_out_c[0])
    out = _moe(x3.reshape(b * l, d), mod_lat, norm_g[1, 1].reshape(1, d), w_router[1], e_bias[1],
               w_gate, w_up, w_down, ws_gate[1], ws_up[1], ws_down[1], final_g, tri, l,
               layer=1, final=True)
    return out.reshape(b, l, d)
```

```python
import functools
import math

import numpy as np
import jax
import jax.numpy as jnp
from jax import lax
from jax.experimental import pallas as pl
from jax.experimental.pallas import tpu as pltpu

F32 = jnp.float32
BF16 = jnp.bfloat16
I32 = jnp.int32
HIGHEST = lax.Precision.HIGHEST
SDS = jax.ShapeDtypeStruct
BS = pl.BlockSpec

EPS = 1e-6
NEG = -1e30

GRID_W = 64
N_Q_HEADS = 8
N_KV_HEADS = 2
HEAD_DIM = 64
ATTN_WIDTH = N_Q_HEADS * HEAD_DIM
KV_WIDTH = N_KV_HEADS * HEAD_DIM
WINDOW = 128
BLOCK = 128
ROPE_BASE = 10000.0
ROPE_FREQS = HEAD_DIM // 4
N_SG_GROUPS = 8
SG_GROUP_DIM = 64
SG_WIDTH = N_SG_GROUPS * SG_GROUP_DIM
HYENA_ORDER = 2
FILT_BANDS = 16
DECAY_SHIFT = 0.05
N_EXPERTS = 64
TOP_K = 8
N_GROUPS = 8
TOPK_GROUPS = 4
ROUTED_SCALE = 2.5

LANES = 128
SUBLANES = 8
MXU_COLS = 256
VMEM_LIMIT = 56 * 1024 * 1024

TM_PROJ = 512
TQ_MIX = 256
MOE_TILE = 256
BM_FFN = 512
RUN_ALIGN = 16
SLOT_CHUNK = 512
NO_SLOT = 256 * 256 - 1
LOCAL_SLOTS = -(-(TOP_K * MOE_TILE + N_EXPERTS * (RUN_ALIGN - 1)) // SLOT_CHUNK) * SLOT_CHUNK
COPY_FIELDS = 3
FFN_IN_BUFS = 4


def _cparams(*sem):
    return pltpu.CompilerParams(dimension_semantics=sem, vmem_limit_bytes=VMEM_LIMIT)


def _dot(a, b):
    return jnp.dot(a, b, preferred_element_type=F32)


def _dot_nt(a, b):
    return lax.dot_general(a, b, (((1,), (1,)), ((), ())), preferred_element_type=F32)


def _dot_hp(a, b):
    return jnp.dot(a, b, preferred_element_type=F32, precision=HIGHEST)


def _norm_mod(x, g, sc, sh):
    ms = jnp.mean(x * x, axis=-1, keepdims=True)
    y = x * lax.rsqrt(ms + EPS)
    return (y * g) * (1.0 + sc) + sh


def _gelu_tanh(x):
    c = math.sqrt(2.0 / math.pi)
    return 0.5 * x * (1.0 + jnp.tanh(c * (x + 0.044715 * (x * x * x))))


def _silu(x):
    return x * jax.nn.sigmoid(x)


def _mod_body(c_ref, w_ref, b_ref, o_ref):
    o_ref[0] = _dot_hp(_silu(c_ref[...]), w_ref[0]) + b_ref[0]


def _mod_vectors(cc, w_mod, b_mod):
    depth, d, n = w_mod.shape
    tn = 1536
    return pl.pallas_call(
        _mod_body,
        out_shape=SDS((depth, SUBLANES, n), F32),
        grid=(depth, n // tn),
        in_specs=[BS((SUBLANES, d), lambda l, j: (0, 0)),
                  BS((1, d, tn), lambda l, j: (l, 0, j)),
                  BS((1, 1, tn), lambda l, j: (l, 0, j))],
        out_specs=BS((1, SUBLANES, tn), lambda l, j: (l, 0, j)),
        compiler_params=_cparams("arbitrary", "arbitrary"),
        name="mod_vectors",
    )(cc, w_mod, b_mod.reshape(depth, 1, n))


def _ctx_kv_body(ctx_ref, mod_ref, g_ref, w_ref, kc_ref, vc_ref):
    h = _norm_mod(ctx_ref[0], g_ref[...], mod_ref[1:2, :], mod_ref[0:1, :])
    z = _dot(h.astype(BF16), w_ref[...])
    kc_ref[0] = z[:, :KV_WIDTH].astype(BF16)
    vc_ref[0] = z[:, KV_WIDTH:].astype(BF16)


def _ctx_kv(ctx, mod_ctx, g, w_kv):
    b, c, d = ctx.shape
    return pl.pallas_call(
        _ctx_kv_body,
        out_shape=(SDS((b, c, KV_WIDTH), BF16), SDS((b, c, KV_WIDTH), BF16)),
        grid=(b,),
        in_specs=[BS((1, c, d), lambda i: (i, 0, 0)),
                  BS((SUBLANES, d), lambda i: (0, 0)),
                  BS((1, d), lambda i: (0, 0)),
                  BS((d, 2 * KV_WIDTH), lambda i: (0, 0))],
        out_specs=(BS((1, c, KV_WIDTH), lambda i: (i, 0, 0)),
                   BS((1, c, KV_WIDTH), lambda i: (i, 0, 0))),
        compiler_params=_cparams("arbitrary"),
        name="ctx_kv",
    )(ctx, mod_ctx, g, w_kv)


def _inproj_ab_body(x_ref, mod_ref, g_ref, w_ref, cos_ref, sin_ref, avg_ref,
                    q_ref, k_ref, v_ref, ug_ref, vn_ref):
    h = _norm_mod(x_ref[0], g_ref[...], mod_ref[0, 1:2, :], mod_ref[0, 0:1, :]).astype(BF16)
    cs = cos_ref[...]
    sn = sin_ref[...]
    rot0 = ATTN_WIDTH + 2 * KV_WIDTH + 2 * SG_WIDTH
    scale = HEAD_DIM ** -0.5
    for j in range(ATTN_WIDTH // MXU_COLS):
        z = _dot(h, w_ref[:, j * MXU_COLS:(j + 1) * MXU_COLS])
        zr = _dot(h, w_ref[:, rot0 + j * MXU_COLS:rot0 + (j + 1) * MXU_COLS])
        for s in range(MXU_COLS // LANES):
            part = slice(s * LANES, (s + 1) * LANES)
            lanes = slice(j * MXU_COLS + s * LANES, j * MXU_COLS + (s + 1) * LANES)
            q_ref[0, :, lanes] = ((z[:, part] * cs + zr[:, part] * sn) * scale).astype(BF16)
    zkv = _dot(h, w_ref[:, ATTN_WIDTH:ATTN_WIDTH + 2 * KV_WIDTH])
    zkr = _dot(h, w_ref[:, rot0 + ATTN_WIDTH:rot0 + ATTN_WIDTH + KV_WIDTH])
    k_ref[0] = (zkv[:, :KV_WIDTH] * cs + zkr * sn).astype(BF16)
    v_ref[0] = zkv[:, KV_WIDTH:].astype(BF16)
    u0 = ATTN_WIDTH + 2 * KV_WIDTH
    ug_ref[0] = _gelu_tanh(_dot(h, w_ref[:, u0:u0 + SG_WIDTH]))
    vf = _gelu_tanh(_dot(h, w_ref[:, u0 + SG_WIDTH:u0 + 2 * SG_WIDTH]))
    avg = avg_ref[...]

    hi = vf.astype(BF16)
    mean = _dot(hi, avg) + _dot((vf - hi.astype(F32)).astype(BF16), avg)
    vc = vf - mean
    var = _dot((vc * vc).astype(BF16), avg)
    vn_ref[0] = (vc * lax.rsqrt(var + EPS)).astype(BF16)


def _inproj_ab(x, mod_lat, g, w_cat, cos_t, sin_t, avg):
    b, l, d = x.shape
    tm = TM_PROJ
    ncol = w_cat.shape[1]
    return pl.pallas_call(
        _inproj_ab_body,
        out_shape=(SDS((b, l, ATTN_WIDTH), BF16), SDS((b, l, KV_WIDTH), BF16),
                   SDS((b, l, KV_WIDTH), BF16), SDS((b, l, SG_WIDTH), F32),
                   SDS((b, l, SG_WIDTH), BF16)),
        grid=(b, l // tm),
        in_specs=[BS((1, tm, d), lambda bi, i: (bi, i, 0)),
                  BS((1, SUBLANES, d), lambda bi, i: (bi, 0, 0)),
                  BS((1, d), lambda bi, i: (0, 0)),
                  BS((d, ncol), lambda bi, i: (0, 0)),
                  BS((tm, LANES), lambda bi, i: (i, 0)),
                  BS((tm, LANES), lambda bi, i: (i, 0)),
                  BS((SG_WIDTH, SG_WIDTH), lambda bi, i: (0, 0))],
        out_specs=(BS((1, tm, ATTN_WIDTH), lambda bi, i: (bi, i, 0)),
                   BS((1, tm, KV_WIDTH), lambda bi, i: (bi, i, 0)),
                   BS((1, tm, KV_WIDTH), lambda bi, i: (bi, i, 0)),
                   BS((1, tm, SG_WIDTH), lambda bi, i: (bi, i, 0)),
                   BS((1, tm, SG_WIDTH), lambda bi, i: (bi, i, 0))),
        compiler_params=_cparams("arbitrary", "arbitrary"),
        name="inproj_ab",
    )(x, mod_lat, g, w_cat, cos_t, sin_t, avg)


def _mixer_body(sink_ref, q_ref, kp_ref, kcur_ref, kn_ref, vp_ref, vcur_ref, vn_ref,
                kc_ref, vc_ref, ug_ref, vnorm_ref, ws_ref, bs_ref, wout_ref, x_ref, mod_ref,
                o_ref, cat_ref, *, seq_len, sub_blocks):
    i = pl.program_id(1)
    kk = jnp.concatenate([kp_ref[0], kcur_ref[0], kn_ref[0]], axis=0)
    vv = jnp.concatenate([vp_ref[0], vcur_ref[0], vn_ref[0]], axis=0)
    kc = kc_ref[0]
    vc = vc_ref[0]
    span = 3 * BLOCK
    ii = lax.broadcasted_iota(I32, (BLOCK, span), 0)
    jj = lax.broadcasted_iota(I32, (BLOCK, span), 1)
    dd = jj - ii
    in_window = jnp.where(dd >= 0, jnp.where(dd <= 2 * WINDOW, 1, 0), 0)
    group = N_Q_HEADS // N_KV_HEADS
    for r in range(sub_blocks):
        rows = slice(r * BLOCK, (r + 1) * BLOCK)
        kpos = (i * sub_blocks + r - 1) * BLOCK + jj
        in_seq = jnp.where(kpos >= 0, jnp.where(kpos < seq_len, 1, 0), 0)
        bias = jnp.where(in_window * in_seq > 0, 0.0, NEG)
        qb = q_ref[0, rows, :]
        kl = kk[r * BLOCK:r * BLOCK + span]
        vl = vv[r * BLOCK:r * BLOCK + span]
        for hq in range(N_Q_HEADS):
            hk = hq // group
            ks = slice(hk * HEAD_DIM, (hk + 1) * HEAD_DIM)
            qh = qb[:, hq * HEAD_DIM:(hq + 1) * HEAD_DIM]
            s_loc = _dot_nt(qh, kl[:, ks]) + bias
            s_ctx = _dot_nt(qh, kc[:, ks])
            sk = sink_ref[hq]
            m = jnp.maximum(jnp.maximum(jnp.max(s_loc, axis=-1, keepdims=True),
                                        jnp.max(s_ctx, axis=-1, keepdims=True)), sk)
            p_loc = jnp.exp(s_loc - m)
            p_ctx = jnp.exp(s_ctx - m)
            den = (jnp.sum(p_loc, axis=-1, keepdims=True) + jnp.sum(p_ctx, axis=-1, keepdims=True)
                   + jnp.exp(sk - m))
            o = _dot(p_loc.astype(BF16), vl[:, ks]) + _dot(p_ctx.astype(BF16), vc[:, ks])
            cat_ref[rows, hq * HEAD_DIM:(hq + 1) * HEAD_DIM] = (o / den).astype(BF16)
        vnb = vnorm_ref[0, rows, :]
        ugb = ug_ref[0, rows, :]
        for g in range(N_SG_GROUPS):
            gs = slice(g * SG_GROUP_DIM, (g + 1) * SG_GROUP_DIM)
            sg = _dot(ws_ref[g], vnb[:, gs]) + bs_ref[:, gs]
            cat_ref[rows, ATTN_WIDTH + g * SG_GROUP_DIM:ATTN_WIDTH + (g + 1) * SG_GROUP_DIM] = (
                ugb[:, gs] * sg).astype(BF16)
    y = _dot(cat_ref[...], wout_ref[...])
    o_ref[0] = x_ref[0] + mod_ref[0, 2:3, :] * y


def _mixer(sink, q, k, v, kc, vc, ug, vn, w_s, b_full, w_out, x, mod_lat):
    b, l, d = x.shape
    tq = TQ_MIX
    r = tq // BLOCK
    nb = l // BLOCK
    c = kc.shape[1]
    prev_map = lambda bi, i: (bi, jnp.maximum(i * r - 1, 0), 0)
    next_map = lambda bi, i: (bi, jnp.minimum((i + 1) * r, nb - 1), 0)
    cur_map = lambda bi, i: (bi, i, 0)
    body = functools.partial(_mixer_body, seq_len=l, sub_blocks=r)
    return pl.pallas_call(
        body,
        out_shape=SDS((b, l, d), F32),
        grid=(b, l // tq),
        in_specs=[BS(memory_space=pltpu.SMEM),
                  BS((1, tq, ATTN_WIDTH), cur_map),
                  BS((1, BLOCK, KV_WIDTH), prev_map), BS((1, tq, KV_WIDTH), cur_map),
                  BS((1, BLOCK, KV_WIDTH), next_map),
                  BS((1, BLOCK, KV_WIDTH), prev_map), BS((1, tq, KV_WIDTH), cur_map),
                  BS((1, BLOCK, KV_WIDTH), next_map),
                  BS((1, c, KV_WIDTH), lambda bi, i: (bi, 0, 0)),
                  BS((1, c, KV_WIDTH), lambda bi, i: (bi, 0, 0)),
                  BS((1, tq, SG_WIDTH), cur_map), BS((1, tq, SG_WIDTH), cur_map),
                  BS((N_SG_GROUPS, BLOCK, BLOCK), lambda bi, i: (0, 0, 0)),
                  BS((BLOCK, SG_WIDTH), lambda bi, i: (0, 0)),
                  BS((d, d), lambda bi, i: (0, 0)),
                  BS((1, tq, d), cur_map),
                  BS((1, SUBLANES, d), lambda bi, i: (bi, 0, 0))],
        out_specs=BS((1, tq, d), cur_map),
        scratch_shapes=[pltpu.VMEM((tq, d), BF16)],
        compiler_params=_cparams("arbitrary", "arbitrary"),
        name="mixer_ab",
    )(sink, q, k, k, k, v, v, v, kc, vc, ug, vn, w_s, b_full, w_out, x, mod_lat)


def _router_body(x_ref, mod_ref, g_ref, wr_ref, eb_ref, tri_ref,
                 h2b_ref, slot_hi_ref, slot_lo_ref, slot_hi_t_ref, slot_lo_t_ref, gate_t_ref,
                 lo_col_ref, hi_col_ref, bounds_row_ref, copy_ref, rows_ref, cnt_ref, carry_ref,
                 *, tm):
    i = pl.program_id(0)

    @pl.when(i == 0)
    def _():
        carry_ref[...] = jnp.zeros_like(carry_ref)

    h2 = _norm_mod(x_ref[...], g_ref[...], mod_ref[0, 4:5, :], mod_ref[0, 3:4, :])
    h2b_ref[...] = h2.astype(BF16)

    logits = lax.dot_general(wr_ref[...], h2, (((1,), (1,)), ((), ())),
                             preferred_element_type=F32, precision=HIGHEST)
    scores = jax.nn.sigmoid(logits)
    per_group = N_EXPERTS // N_GROUPS
    shape3 = (N_GROUPS, per_group, tm)
    s3 = scores.reshape(shape3)
    b3 = (scores + eb_ref[...]).reshape(shape3)
    sub = lax.broadcasted_iota(I32, shape3, 1)
    eid = lax.broadcasted_iota(I32, shape3, 0) * per_group + sub

    m1 = jnp.max(b3, axis=1, keepdims=True)
    i1 = jnp.min(jnp.where(b3 == m1, sub, per_group), axis=1, keepdims=True)
    m2 = jnp.max(jnp.where(sub == i1, -jnp.inf, b3), axis=1, keepdims=True)
    gs = m1 + m2
    keep = []
    for g in range(N_GROUPS):
        beaten = jnp.zeros((1, tm), I32)
        for g2 in range(N_GROUPS):
            if g2 == g:
                continue
            wins = (gs[g2] >= gs[g]) if g2 < g else (gs[g2] > gs[g])
            beaten = beaten + jnp.where(wins, 1, 0)
        keep.append(jnp.where(beaten < TOPK_GROUPS, 1, 0)[None])
    keep3 = jnp.concatenate(keep, axis=0)
    val = jnp.where(keep3 > 0, b3, -jnp.inf)

    def red(fn, a):
        return fn(fn(a, axis=0, keepdims=True), axis=1, keepdims=True)

    idxs, ws = [], []
    member = jnp.zeros(shape3, F32)
    for _ in range(TOP_K):
        m = red(jnp.max, val)
        idx = red(jnp.min, jnp.where(val == m, eid, N_EXPERTS))
        hit = eid == idx
        ws.append(red(jnp.sum, jnp.where(hit, s3, 0.0)))
        val = jnp.where(hit, -jnp.inf, val)
        member = member + jnp.where(hit, 1.0, 0.0)
        idxs.append(idx)
    wsum = ws[0]
    for w in ws[1:]:
        wsum = wsum + w

    member2 = member.reshape(N_EXPERTS, tm)
    cnt = jnp.sum(member2, axis=1, keepdims=True)
    runlen = jnp.floor((cnt + (RUN_ALIGN - 1)) * (1.0 / RUN_ALIGN)) * RUN_ALIGN
    runlen_b = jnp.broadcast_to(runlen, (N_EXPERTS, LANES))
    e_row = lax.broadcasted_iota(I32, (N_EXPERTS, N_EXPERTS), 0)
    e_col = lax.broadcasted_iota(I32, (N_EXPERTS, N_EXPERTS), 1)
    earlier = jnp.where(e_col < e_row, 1.0, 0.0).astype(BF16)
    loff = _dot(earlier, runlen_b.astype(BF16))
    slot = _dot(member2.astype(BF16), tri_ref[...]) + loff[:, 0:1]
    slot = jnp.where(member2 > 0.0, slot, float(NO_SLOT))
    slot_hi = jnp.floor(slot * (1.0 / 256.0))
    slot_lo = slot - 256.0 * slot_hi
    gate3 = jnp.zeros(shape3, F32)
    for k in range(TOP_K):
        gate3 = jnp.where(eid == idxs[k], ws[k] / wsum * ROUTED_SCALE, gate3)
    gate = gate3.reshape(N_EXPERTS, tm)
    no_expert = jnp.zeros((LANES - N_EXPERTS, tm), F32)
    pad_e = lambda a: jnp.concatenate([a, no_expert], axis=0)
    slot_hi_ref[0] = pad_e(slot_hi).astype(BF16)
    slot_lo_ref[0] = pad_e(slot_lo).astype(BF16)
    slot_hi_t_ref[0] = pad_e(slot_hi).T.astype(BF16)
    slot_lo_t_ref[0] = pad_e(slot_lo).T.astype(BF16)
    gate_t_ref[0] = pad_e(gate).T.astype(BF16)
    run_lo = loff
    run_hi = loff + runlen_b
    no_run = jnp.zeros((LANES - N_EXPERTS, LANES), F32)
    lo_col_ref[0] = jnp.concatenate([run_lo, no_run], axis=0)
    hi_col_ref[0] = jnp.concatenate([run_hi, no_run], axis=0)
    diag = (lax.broadcasted_iota(I32, (N_EXPERTS, LANES), 0)
            == lax.broadcasted_iota(I32, (N_EXPERTS, LANES), 1))
    bounds_row_ref[0] = jnp.concatenate(
        [jnp.sum(jnp.where(diag, run_lo, 0.0), axis=0, keepdims=True),
         jnp.sum(jnp.where(diag, run_hi, 0.0), axis=0, keepdims=True),
         jnp.zeros((SUBLANES - 2, LANES), F32)], axis=0)
    big = 2.0 * RUN_ALIGN
    n_chunks = runlen_b * (1.0 / RUN_ALIGN)
    n_big = jnp.floor(n_chunks * 0.5)
    n_small = n_chunks - 2.0 * n_big
    before_big = _dot(earlier, n_big.astype(BF16))
    before_small = _dot(earlier, n_small.astype(BF16))
    gbase = carry_ref[...]
    j = lax.broadcasted_iota(I32, (N_EXPERTS, LANES), 1).astype(F32)
    e_iota = lax.broadcasted_iota(I32, (N_EXPERTS, LANES), 0).astype(F32)

    def entries(before, count, loc0, rel0, step):
        e_of = jnp.sum(jnp.where(before + count <= j, 1.0, 0.0), axis=0, keepdims=True)
        pick = lambda x: jnp.sum(jnp.where(e_iota == e_of, x, 0.0), axis=0, keepdims=True)
        k = j[0:1, :] - pick(before)
        return jnp.minimum(e_of, N_EXPERTS - 1.0), pick(loc0) + step * k, pick(rel0) + step * k

    e_b, loc_b, rel_b = entries(before_big, n_big, loff, gbase, big)
    e_s, loc_s, rel_s = entries(before_small, n_small, loff + big * n_big, gbase + big * n_big, 0.0)
    copy_ref[0] = jnp.concatenate([jnp.concatenate([e_b, e_s], axis=1),
                                   jnp.concatenate([loc_b, loc_s], axis=1),
                                   jnp.concatenate([rel_b, rel_s], axis=1)], axis=0).astype(I32)
    last = slice(N_EXPERTS - 1, N_EXPERTS)
    lane = lax.broadcasted_iota(I32, (1, LANES), 1)
    rows_used = loff[last, :] + runlen_b[last, :]
    counts = jnp.where(lane == 0, rows_used,
                       jnp.where(lane == 1, before_big[last, :] + n_big[last, :],
                                 before_small[last, :] + n_small[last, :]))
    rows_ref[0] = counts.astype(I32)
    total = carry_ref[...] + runlen_b
    carry_ref[...] = total
    cnt_ref[...] = total


def _router(x_flat, mod_lat, g, wr_t, e_bias, tri, tokens_per_batch):
    t, d = x_flat.shape
    tm = MOE_TILE
    n_tiles = t // tm
    tiles_per_batch = tokens_per_batch // tm
    body = functools.partial(_router_body, tm=tm)
    table = SDS((n_tiles, COPY_FIELDS, 2 * LANES), I32)
    table_spec = BS((1, COPY_FIELDS, 2 * LANES), lambda i: (i, 0, 0))
    expert_major = SDS((n_tiles, LANES, tm), BF16)
    expert_major_spec = BS((1, LANES, tm), lambda i: (i, 0, 0))
    token_major = SDS((n_tiles, tm, LANES), BF16)
    token_major_spec = BS((1, tm, LANES), lambda i: (i, 0, 0))
    per_expert = SDS((n_tiles, LANES, LANES), F32)
    per_expert_spec = BS((1, LANES, LANES), lambda i: (i, 0, 0))
    return pl.pallas_call(
        body,
        out_shape=(SDS((t, d), BF16),
                   expert_major, expert_major, token_major, token_major, token_major,
                   per_expert, per_expert, SDS((n_tiles, SUBLANES, LANES), F32),
                   table, SDS((n_tiles, 1, LANES), I32), SDS((N_EXPERTS, LANES), F32)),
        grid=(n_tiles,),
        in_specs=[BS((tm, d), lambda i: (i, 0)),
                  BS((1, SUBLANES, d), lambda i: (i // tiles_per_batch, 0, 0)),
                  BS((1, d), lambda i: (0, 0)),
                  BS((N_EXPERTS, d), lambda i: (0, 0)),
                  BS((N_EXPERTS, 1), lambda i: (0, 0)),
                  BS((tm, tm), lambda i: (0, 0))],
        out_specs=(BS((tm, d), lambda i: (i, 0)),
                   expert_major_spec, expert_major_spec, token_major_spec, token_major_spec,
                   token_major_spec, per_expert_spec, per_expert_spec,
                   BS((1, SUBLANES, LANES), lambda i: (i, 0, 0)),
                   table_spec, BS((1, 1, LANES), lambda i: (i, 0, 0)),
                   BS((N_EXPERTS, LANES), lambda i: (0, 0))),
        scratch_shapes=[pltpu.VMEM((N_EXPERTS, LANES), F32)],
        compiler_params=_cparams("arbitrary"),
        name="moe_router",
    )(x_flat, mod_lat, g, wr_t, e_bias, tri)


def _start_run_copies(pstart_ref, copy_ref, n_big, n_small, run_copy):
    def entry(base, nrows):
        def body(j, carry):
            slot0 = pstart_ref[copy_ref[0, 0, base + j]] + copy_ref[0, 2, base + j]
            run_copy(pl.multiple_of(copy_ref[0, 1, base + j], RUN_ALIGN),
                     pl.multiple_of(slot0, RUN_ALIGN), nrows).start()
            return carry
        return body

    lax.fori_loop(0, n_big, entry(0, 2 * RUN_ALIGN), 0)
    lax.fori_loop(0, n_small, entry(LANES, RUN_ALIGN), 0)


def _wait_run_rows(copy_of_rows, rows):
    def wait_n(nrows):
        def body(_, carry):
            copy_of_rows(nrows).wait()
            return carry
        return body

    lax.fori_loop(0, rows // SLOT_CHUNK, wait_n(SLOT_CHUNK), 0)
    lax.fori_loop(0, (rows % SLOT_CHUNK) // RUN_ALIGN, wait_n(RUN_ALIGN), 0)


def _dispatch_body(pend_ref, pcnt_ref, nu_ref, pstart_ref, tile_ref, copy_ref,
                   slot_hi_ref, slot_lo_ref, bounds_ref, h_ref, xs_ref, loc_ref, zbuf_ref, sem, zsem,
                   *, tl, bm, n_blocks):
    i = pl.program_id(0)
    last = pl.num_programs(0) - 1
    buf = i % 2

    def zero_copy(row0):
        return pltpu.make_async_copy(
            zbuf_ref, xs_ref.at[pl.ds(pl.multiple_of(row0, RUN_ALIGN), bm), :], zsem)

    @pl.when(i == 0)
    def _():
        zbuf_ref[...] = jnp.zeros_like(zbuf_ref)

        def start(e, c):
            @pl.when(pcnt_ref[e] > 0)
            def _():
                zero_copy(pend_ref[e] - bm).start()
            return c

        def wait(e, c):
            @pl.when(pcnt_ref[e] > 0)
            def _():
                zero_copy(pend_ref[e] - bm).wait()
            return c

        def start_tail(j, c):
            zero_copy(j * bm).start()
            return c

        def wait_tail(j, c):
            zero_copy(j * bm).wait()
            return c

        lax.fori_loop(0, N_EXPERTS, start, 0)
        lax.fori_loop(nu_ref[0], n_blocks, start_tail, 0)
        lax.fori_loop(0, N_EXPERTS, wait, 0)
        lax.fori_loop(nu_ref[0], n_blocks, wait_tail, 0)

    rows_used = tile_ref[i, 0]
    h = h_ref[...]
    run_lo = bounds_ref[0, 0:1, :]
    run_hi = bounds_ref[0, 1:2, :]

    def sort_chunk(c, carry):
        row0 = (c * SLOT_CHUNK).astype(F32)
        row_e = lax.broadcasted_iota(I32, (SLOT_CHUNK, LANES), 0).astype(F32) + row0
        in_run = jnp.where(row_e >= run_lo, jnp.where(row_e < run_hi, 1.0, 0.0), 0.0).astype(BF16)
        slot_of_token = (256.0 * _dot(in_run, slot_hi_ref[0]) + _dot(in_run, slot_lo_ref[0]))
        row_t = lax.broadcasted_iota(I32, (SLOT_CHUNK, tl), 0).astype(F32) + row0
        onehot = jnp.where(slot_of_token == row_t, 1.0, 0.0).astype(BF16)
        rows = pl.ds(pl.multiple_of(c * SLOT_CHUNK, SLOT_CHUNK), SLOT_CHUNK)
        loc_ref[buf, rows, :] = _dot(onehot, h).astype(BF16)
        return carry

    lax.fori_loop(0, (rows_used + SLOT_CHUNK - 1) // SLOT_CHUNK, sort_chunk, 0)

    def run_copy(b, loc0, slot0, nrows=RUN_ALIGN):
        return pltpu.make_async_copy(loc_ref.at[b, pl.ds(loc0, nrows), :],
                                     xs_ref.at[pl.ds(slot0, nrows), :], sem.at[b])

    _start_run_copies(pstart_ref, copy_ref, tile_ref[i, 1], tile_ref[i, 2],
                      lambda loc0, slot0, n: run_copy(buf, loc0, slot0, n))

    @pl.when(i > 0)
    def _():
        _wait_run_rows(lambda n: run_copy(1 - buf, 0, 0, n), tile_ref[jnp.maximum(i - 1, 0), 0])

    @pl.when(i == last)
    def _():
        _wait_run_rows(lambda n: run_copy(buf, 0, 0, n), rows_used)


def _dispatch(pend, pcnt, n_used, pstart, tile_counts, copy_table, slot_hi, slot_lo, bounds_row,
              h2b, n_slots):
    t, d = h2b.shape
    tl = MOE_TILE
    body = functools.partial(_dispatch_body, tl=tl, bm=BM_FFN, n_blocks=n_slots // BM_FFN)
    grid_spec = pltpu.PrefetchScalarGridSpec(
        num_scalar_prefetch=5,
        grid=(t // tl,),
        in_specs=[BS((1, COPY_FIELDS, 2 * LANES), lambda i, *_: (i, 0, 0), memory_space=pltpu.SMEM),
                  BS((1, LANES, tl), lambda i, *_: (i, 0, 0)),
                  BS((1, LANES, tl), lambda i, *_: (i, 0, 0)),
                  BS((1, SUBLANES, LANES), lambda i, *_: (i, 0, 0)),
                  BS((tl, d), lambda i, *_: (i, 0))],
        out_specs=BS(memory_space=pl.ANY),
        scratch_shapes=[pltpu.VMEM((2, LOCAL_SLOTS, d), BF16), pltpu.VMEM((BM_FFN, d), BF16),
                        pltpu.SemaphoreType.DMA((2,)), pltpu.SemaphoreType.DMA(())],
    )
    return pl.pallas_call(
        body,
        out_shape=SDS((n_slots, d), BF16),
        grid_spec=grid_spec,
        compiler_params=_cparams("arbitrary"),
        name="moe_dispatch",
    )(pend, pcnt, n_used, pstart, tile_counts, copy_table, slot_hi, slot_lo, bounds_row, h2b)


def _ffn_body(first_ref, count_ref, nu_ref, xs_ref, wg_ref, wu_ref, wd_ref, ys_ref, wgb_ref, wub_ref,
              wdb_ref, xbuf_ref, ybuf_ref, isem, osem, *, bm, n_blocks):
    e = pl.program_id(0)
    nu = nu_ref[0]

    def block_rows(blk):
        return pl.ds(pl.multiple_of(blk * bm, bm), bm)

    def in_copy(blk, slot):
        return pltpu.make_async_copy(xs_ref.at[block_rows(blk), :], xbuf_ref.at[slot], isem.at[slot])

    def out_copy(blk, slot):
        return pltpu.make_async_copy(ybuf_ref.at[slot], ys_ref.at[block_rows(blk), :], osem.at[slot])

    @pl.when(e == 0)
    def _():
        for s in range(FFN_IN_BUFS):
            @pl.when(s < nu)
            def _():
                in_copy(s, s).start()

    @pl.when(count_ref[e] > 0)
    def _():
        wgb_ref[...] = wg_ref[...].astype(BF16)
        wub_ref[...] = wu_ref[...].astype(BF16)
        wdb_ref[...] = wd_ref[...].astype(BF16)

    def one_block(b, carry):
        i = first_ref[e] + b
        slot = i % FFN_IN_BUFS
        oslot = i % 2
        in_copy(i, slot).wait()
        x = xbuf_ref[slot]
        a = _silu(_dot(x, wgb_ref[...])) * _dot(x, wub_ref[...])
        y = _dot(a.astype(BF16), wdb_ref[...]).astype(BF16)

        @pl.when(i >= 2)
        def _():
            out_copy(i - 2, oslot).wait()

        ybuf_ref[oslot] = y
        out_copy(i, oslot).start()

        @pl.when(i + FFN_IN_BUFS < nu)
        def _():
            in_copy(i + FFN_IN_BUFS, slot).start()

        return carry

    lax.fori_loop(0, count_ref[e], one_block, 0)

    @pl.when(e == pl.num_programs(0) - 1)
    def _():
        @pl.when(nu >= 2)
        def _():
            out_copy(nu - 2, (nu - 2) % 2).wait()

        out_copy(nu - 1, (nu - 1) % 2).wait()
        ybuf_ref[0] = jnp.zeros((bm, ybuf_ref.shape[-1]), BF16)

        def start_tail(j, c):
            out_copy(j, 0).start()
            return c

        def wait_tail(j, c):
            out_copy(j, 0).wait()
            return c

        lax.fori_loop(nu, n_blocks, start_tail, 0)
        lax.fori_loop(nu, n_blocks, wait_tail, 0)


def _ffn(first_block, block_count, n_used, xs, w_gate, w_up, w_down, layer):
    n_slots, d = xs.shape
    bm = BM_FFN
    de = w_gate.shape[-1]
    n_blocks = n_slots // bm
    body = functools.partial(_ffn_body, bm=bm, n_blocks=n_blocks)
    grid_spec = pltpu.PrefetchScalarGridSpec(
        num_scalar_prefetch=3,
        grid=(N_EXPERTS,),
        in_specs=[BS(memory_space=pl.ANY),
                  BS((None, None, d, de), lambda e, *_: (layer, e, 0, 0)),
                  BS((None, None, d, de), lambda e, *_: (layer, e, 0, 0)),
                  BS((None, None, de, d), lambda e, *_: (layer, e, 0, 0))],
        out_specs=BS(memory_space=pl.ANY),
        scratch_shapes=[pltpu.VMEM((d, de), BF16), pltpu.VMEM((d, de), BF16),
                        pltpu.VMEM((de, d), BF16),
                        pltpu.VMEM((FFN_IN_BUFS, bm, d), BF16), pltpu.VMEM((2, bm, d), BF16),
                        pltpu.SemaphoreType.DMA((FFN_IN_BUFS,)), pltpu.SemaphoreType.DMA((2,))],
    )
    return pl.pallas_call(
        body,
        out_shape=SDS((n_slots, d), BF16),
        grid_spec=grid_spec,
        compiler_params=_cparams("arbitrary"),
        name="moe_experts",
    )(first_block, block_count, n_used, xs, w_gate, w_up, w_down)


def _combine_body(pstart_ref, tile_ref, copy_ref, next_copy_ref, ys_ref,
                  slot_hi_ref, slot_lo_ref, gate_t_ref, lo_col_ref, hi_col_ref,
                  x_ref, h2b_ref, mod_ref, wsg_ref, wsu_ref, wsd_ref, fg_ref,
                  o_ref, loc_ref, acc_ref, sem, *, tl, final):
    i = pl.program_id(0)
    last = pl.num_programs(0) - 1
    buf = i % 2
    rows_used = tile_ref[i, 0]
    nxt = jnp.minimum(i + 1, last)

    def run_copy(b, loc0, slot0, nrows=RUN_ALIGN):
        return pltpu.make_async_copy(ys_ref.at[pl.ds(slot0, nrows), :],
                                     loc_ref.at[b, pl.ds(loc0, nrows), :], sem.at[b])

    @pl.when(i == 0)
    def _():
        loc_ref[...] = jnp.zeros_like(loc_ref)
        _start_run_copies(pstart_ref, copy_ref, tile_ref[i, 1], tile_ref[i, 2],
                          lambda loc0, slot0, n: run_copy(buf, loc0, slot0, n))

    @pl.when(i < last)
    def _():
        _start_run_copies(pstart_ref, next_copy_ref, tile_ref[nxt, 1], tile_ref[nxt, 2],
                          lambda loc0, slot0, n: run_copy(1 - buf, loc0, slot0, n))

    hb = h2b_ref[...]
    a = _silu(_dot(hb, wsg_ref[...])) * _dot(hb, wsu_ref[...])
    acc_ref[...] = _dot(a.astype(BF16), wsd_ref[...])
    run_lo = lo_col_ref[0][:, 0:1]
    run_hi = hi_col_ref[0][:, 0:1]

    _wait_run_rows(lambda n: run_copy(buf, 0, 0, n), rows_used)

    def unsort_chunk(c, carry):
        col0 = (c * SLOT_CHUNK).astype(F32)
        col_e = lax.broadcasted_iota(I32, (LANES, SLOT_CHUNK), 1).astype(F32) + col0
        in_run = jnp.where(col_e >= run_lo, jnp.where(col_e < run_hi, 1.0, 0.0), 0.0).astype(BF16)
        slot_of_token = (256.0 * _dot(slot_hi_ref[0], in_run) + _dot(slot_lo_ref[0], in_run))
        gate_of_token = _dot(gate_t_ref[0], in_run)
        col_t = lax.broadcasted_iota(I32, (tl, SLOT_CHUNK), 1).astype(F32) + col0
        gate = jnp.where(slot_of_token == col_t, gate_of_token, 0.0).astype(BF16)
        y = loc_ref[buf, pl.ds(pl.multiple_of(c * SLOT_CHUNK, SLOT_CHUNK), SLOT_CHUNK), :]
        acc_ref[...] = acc_ref[...] + _dot(gate, y)
        return carry

    lax.fori_loop(0, (rows_used + SLOT_CHUNK - 1) // SLOT_CHUNK, unsort_chunk, 0)

    xo = x_ref[...] + mod_ref[0, 5:6, :] * acc_ref[...]
    if final:
        ms = jnp.mean(xo * xo, axis=-1, keepdims=True)
        xo = (xo * lax.rsqrt(ms + EPS)) * fg_ref[...]
    o_ref[...] = xo


def _combine(pstart, tile_counts, copy_table, ys, slot_hi_t, slot_lo_t, gate_t, lo_col, hi_col,
             x_flat, h2b, mod_lat, wsg, wsu, wsd, final_g, tokens_per_batch, final):
    t, d = x_flat.shape
    tl = MOE_TILE
    ds = wsg.shape[1]
    tiles_per_batch = tokens_per_batch // tl
    body = functools.partial(_combine_body, tl=tl, final=final)
    n_tiles = t // tl
    table_spec = lambda index: BS((1, COPY_FIELDS, 2 * LANES), index, memory_space=pltpu.SMEM)
    this_tile = lambda i, *_: (i, 0, 0)
    next_tile = lambda i, *_: (jnp.minimum(i + 1, n_tiles - 1), 0, 0)
    grid_spec = pltpu.PrefetchScalarGridSpec(
        num_scalar_prefetch=2,
        grid=(n_tiles,),
        in_specs=[table_spec(this_tile), table_spec(next_tile),
                  BS(memory_space=pl.ANY),
                  BS((1, tl, LANES), this_tile), BS((1, tl, LANES), this_tile),
                  BS((1, tl, LANES), this_tile),
                  BS((1, LANES, LANES), this_tile), BS((1, LANES, LANES), this_tile),
                  BS((tl, d), lambda i, *_: (i, 0)),
                  BS((tl, d), lambda i, *_: (i, 0)),
                  BS((1, SUBLANES, d), lambda i, *_: (i // tiles_per_batch, 0, 0)),
                  BS((d, ds), lambda i, *_: (0, 0)),
                  BS((d, ds), lambda i, *_: (0, 0)),
                  BS((ds, d), lambda i, *_: (0, 0)),
                  BS((1, d), lambda i, *_: (0, 0))],
        out_specs=BS((tl, d), lambda i, *_: (i, 0)),
        scratch_shapes=[pltpu.VMEM((2, LOCAL_SLOTS, d), BF16), pltpu.VMEM((tl, d), F32),
                        pltpu.SemaphoreType.DMA((2,))],
    )
    return pl.pallas_call(
        body,
        out_shape=SDS((t, d), F32),
        grid_spec=grid_spec,
        compiler_params=_cparams("arbitrary"),
        name="moe_combine",
    )(pstart, tile_counts, copy_table, copy_table, ys, slot_hi_t, slot_lo_t, gate_t,
      lo_col, hi_col, x_flat, h2b, mod_lat, wsg, wsu, wsd, final_g)


def _moe(x_flat, mod_lat, g, w_router, e_bias, w_gate, w_up, w_down, ws_gate, ws_up, ws_down,
         final_g, tri, tokens_per_batch, layer, final):
    t, d = x_flat.shape
    bm = BM_FFN
    n_tiles = t // MOE_TILE
    (h2b, slot_hi, slot_lo, slot_hi_t, slot_lo_t, gate_t, lo_col, hi_col, bounds_row, copy_table,
     tile_counts, seg) = _router(x_flat, mod_lat, g, w_router.T, e_bias.reshape(N_EXPERTS, 1), tri,
                                 tokens_per_batch)
    tile_counts = tile_counts[:, 0, :SUBLANES]
    seg_rows = seg[:, 0].astype(I32)
    pcnt = (seg_rows + bm - 1) // bm * bm
    pend = jnp.cumsum(pcnt).astype(I32)
    pstart = pend - pcnt
    max_rows = t * TOP_K + n_tiles * N_EXPERTS * (RUN_ALIGN - 1)
    n_blocks = -(-max_rows // bm) + N_EXPERTS
    n_slots = n_blocks * bm
    n_used = pend[-1:] // bm
    xs = _dispatch(pend, pcnt, n_used, pstart, tile_counts, copy_table, slot_hi, slot_lo,
                   bounds_row, h2b, n_slots)
    ys = _ffn(pstart // bm, pcnt // bm, n_used, xs, w_gate, w_up, w_down, layer)
    return _combine(pstart, tile_counts, copy_table, ys, slot_hi_t, slot_lo_t, gate_t, lo_col,
                    hi_col, x_flat, h2b, mod_lat,
                    ws_gate.astype(BF16), ws_up.astype(BF16), ws_down.astype(BF16),
                    final_g.reshape(1, d), tokens_per_batch, final)


def _inproj_c_body(xp_ref, x_ref, xn_ref, mod_ref, g_ref, w_ref, cw_ref, cb_ref,
                   v_ref, g1_ref, g2_ref, *, tm, n_tiles):
    i = pl.program_id(1)
    halo = SUBLANES
    xe = jnp.concatenate([xp_ref[0], x_ref[0], xn_ref[0]], axis=0)
    h = _norm_mod(xe, g_ref[...], mod_ref[0, 1:2, :], mod_ref[0, 0:1, :])
    row = lax.broadcasted_iota(I32, (tm + 2 * halo, 1), 0)
    outside = jnp.logical_or(jnp.logical_and(i == 0, row < halo),
                             jnp.logical_and(i == n_tiles - 1, row >= tm + halo))
    hb = jnp.where(outside, 0.0, h).astype(BF16)
    width = v_ref.shape[-1]
    for part, o_ref in enumerate((v_ref, g1_ref, g2_ref)):
        cols = slice(part * width, (part + 1) * width)
        zp = _dot(hb, w_ref[:, cols])
        up = pltpu.roll(zp, 1, 0)
        dn = pltpu.roll(zp, tm + 2 * halo - 1, 0)
        z = cw_ref[0:1, cols] * up + cw_ref[1:2, cols] * zp + cw_ref[2:3, cols] * dn + cb_ref[:, cols]
        o_ref[0] = z[halo:halo + tm]


def _inproj_c(x, mod_lat, g, w_in, conv_w, conv_b):
    b, l, d = x.shape
    tm = TM_PROJ
    n_tiles = l // tm
    w3 = w_in.shape[1]
    width = w3 // 3
    r8 = tm // SUBLANES
    body = functools.partial(_inproj_c_body, tm=tm, n_tiles=n_tiles)
    out = SDS((b, l, width), F32)
    return pl.pallas_call(
        body,
        out_shape=(out, out, out),
        grid=(b, n_tiles),
        in_specs=[BS((1, SUBLANES, d), lambda bi, i: (bi, jnp.maximum(i * r8 - 1, 0), 0)),
                  BS((1, tm, d), lambda bi, i: (bi, i, 0)),
                  BS((1, SUBLANES, d), lambda bi, i: (bi, jnp.minimum((i + 1) * r8, l // SUBLANES - 1), 0)),
                  BS((1, SUBLANES, d), lambda bi, i: (bi, 0, 0)),
                  BS((1, d), lambda bi, i: (0, 0)),
                  BS((d, w3), lambda bi, i: (0, 0)),
                  BS((3, w3), lambda bi, i: (0, 0)),
                  BS((1, w3), lambda bi, i: (0, 0))],
        out_specs=(BS((1, tm, width), lambda bi, i: (bi, i, 0)),
                   BS((1, tm, width), lambda bi, i: (bi, i, 0)),
                   BS((1, tm, width), lambda bi, i: (bi, i, 0))),
        compiler_params=_cparams("arbitrary", "arbitrary"),
        name="inproj_c",
    )(x, x, x, mod_lat, g, w_in, conv_w, conv_b)


def _filter_body(f_ref, w1_ref, b1_ref, w2_ref, b2_ref, w3hi_ref, w3lo_ref, fr_ref, dl_ref,
                 k_ref, l1_ref, *, tp, n):
    i = pl.program_id(0)
    feats = f_ref[...]
    fr = fr_ref[...]
    a = jnp.sin(fr * (_dot_hp(feats, w1_ref[...]) + b1_ref[...]))
    a = jnp.sin(fr * (_dot_hp(a, w2_ref[...]) + b2_ref[...]))
    a_hi = a.astype(BF16)
    a_lo = (a - a_hi.astype(F32)).astype(BF16)
    w3_hi = w3hi_ref[...]
    hf = _dot(a_hi, w3_hi) + _dot(a_hi, w3lo_ref[...]) + _dot(a_lo, w3_hi)
    t01 = feats[:, 0:1]
    hf = hf * (jnp.exp(-t01 * jnp.abs(dl_ref[...])) + DECAY_SHIFT)
    row = lax.broadcasted_iota(I32, hf.shape, 0) + i * tp
    hf = jnp.where(row == n, 0.0, hf)
    k_ref[...] = hf

    @pl.when(i == 0)
    def _():
        l1_ref[...] = jnp.zeros_like(l1_ref)

    l1_ref[...] = l1_ref[...] + jnp.sum(jnp.abs(hf), axis=0, keepdims=True)


def _filters(feats, w1, b1, w2, b2, w3, freq, delta, width):
    n2, fe = feats.shape
    n = n2 // 2
    hid = w2.shape[0]
    tp = 256
    half_tiles = n // tp
    body = functools.partial(_filter_body, tp=tp, n=n)
    full = lambda shape: BS(shape, lambda i: (0, 0))
    by_direction = lambda a: a.reshape(a.shape[0], HYENA_ORDER, 2, width).transpose(2, 0, 1, 3).reshape(
        2, a.shape[0], HYENA_ORDER * width)
    w3_d = by_direction(w3)
    w3_hi = w3_d.astype(BF16)
    w3_lo = (w3_d - w3_hi.astype(F32)).astype(BF16)
    delta_d = by_direction(delta)
    fo = HYENA_ORDER * width
    direction = lambda i: (i // half_tiles, 0, 0)
    return pl.pallas_call(
        body,
        out_shape=(SDS((n2, fo), F32), SDS((SUBLANES, fo), F32)),
        grid=(n2 // tp,),
        in_specs=[BS((tp, fe), lambda i: (i, 0)), full((fe, hid)), full((1, hid)),
                  full((hid, hid)), full((1, hid)),
                  BS((None, hid, fo), direction), BS((None, hid, fo), direction), full((1, hid)),
                  BS((None, 1, fo), direction)],
        out_specs=(BS((tp, fo), lambda i: (i, 0)), BS((SUBLANES, fo), lambda i: (0, 0))),
        compiler_params=_cparams("arbitrary"),
        name="hyena_filters",
    )(feats, w1, b1, w2, b2, w3_hi, w3_lo, freq, delta_d)


DFT_R = 128
DFT_VP = 72
DFT_BGROUP = 16
DFT_LANE_TILES = 4


def _dft_tables(n):
    r = DFT_R
    vp = DFT_VP
    m = 2 * n
    na = n // r
    two_pi = 2.0 * np.pi
    live = (np.arange(vp) <= r // 2).astype(np.float64)
    a = np.arange(2 * na)[None, :]
    v = np.arange(vp)[:, None]
    ang1 = two_pi * ((a * v) % r) / r
    f1_full = np.concatenate([np.cos(ang1), -np.sin(ang1)], axis=0) * np.tile(live, 2)[:, None]
    f1 = f1_full[:, :na]
    b = np.arange(r)[None, None, :]
    u = np.arange(r)[None, :, None]
    vv = np.arange(vp)[:, None, None]
    ang2 = two_pi * ((b * (r * u + vv)) % m) / m
    gr, gi = np.cos(ang2), -np.sin(ang2)
    fwd = np.concatenate([np.concatenate([gr, -gi], axis=2),
                          np.concatenate([gi, gr], axis=2)], axis=1)
    hr, hi = np.transpose(gr, (0, 2, 1)), -np.transpose(gi, (0, 2, 1))
    inv = np.concatenate([np.concatenate([hr, -hi], axis=2),
                          np.concatenate([hi, hr], axis=2)], axis=1)
    weight = live * np.where((np.arange(vp) == 0) | (np.arange(vp) == r // 2), 1.0, 2.0)
    ang3 = two_pi * ((np.arange(na)[:, None] * np.arange(vp)[None, :]) % r) / r
    f3 = np.concatenate([np.cos(ang3) * weight, -np.sin(ang3) * weight], axis=1) / m
    cast = lambda t: jnp.asarray(t.astype(np.float32)).astype(BF16)
    return cast(f1), cast(f1_full), cast(fwd), cast(inv), cast(f3)


def _lane_tile_specs(rows, index_map_of_tile):
    return [BS((None, rows, DFT_BGROUP, LANES), index_map_of_tile(t)) for t in range(DFT_LANE_TILES)]


def _rows_of_position(ref, j):
    x, bg, _ = ref.shape
    return ref.reshape(x * bg, LANES)[pl.ds(j, x, stride=bg), :]


def _dft_s1_body(*refs):
    q = DFT_LANE_TILES
    y_refs, f1_ref, ar_ref, ai_ref = refs[:q], refs[q], refs[q + 1], refs[q + 2]
    f1 = f1_ref[...]
    bg = DFT_BGROUP
    pair = MXU_COLS // LANES
    planes2d = lambda ref, t: ref.at[t].reshape(DFT_VP * bg, LANES)
    for t0 in range(0, q, pair):
        tiles = range(t0, t0 + pair)
        for j in range(bg):
            y = jnp.concatenate([_rows_of_position(y_refs[t], j) for t in tiles], axis=1)
            res = _dot(f1, y.astype(BF16))
            for s, t in enumerate(tiles):
                lanes = slice(s * LANES, (s + 1) * LANES)
                planes2d(ar_ref, t)[pl.ds(j, DFT_VP, stride=bg), :] = res[:DFT_VP, lanes]
                planes2d(ai_ref, t)[pl.ds(j, DFT_VP, stride=bg), :] = res[DFT_VP:, lanes]


def _dft_s1(y4, f1):
    nb, na, r, c = y4.shape
    q, bg = DFT_LANE_TILES, DFT_BGROUP
    out = SDS((nb, c // LANES, DFT_VP, r, LANES), F32)
    plane_spec = BS((None, q, DFT_VP, bg, LANES), lambda i, j, cc: (i, cc, 0, j, 0))
    tile_map = lambda t: (lambda i, j, cc: (i, 0, j, cc * q + t))
    return pl.pallas_call(
        _dft_s1_body,
        out_shape=(out, out),
        grid=(nb, r // bg, c // (q * LANES)),
        in_specs=_lane_tile_specs(na, tile_map) + [BS((2 * DFT_VP, na), lambda i, j, cc: (0, 0))],
        out_specs=(plane_spec, plane_spec),
        compiler_params=_cparams("arbitrary", "arbitrary", "arbitrary"),
        name="dft_stage1",
    )(*([y4] * q), f1)


def _plane_rows(re_ref, im_ref):
    wide = lambda ref: jnp.concatenate([ref[t] for t in range(ref.shape[0])], axis=1)
    return jnp.concatenate([wide(re_ref), wide(im_ref)], axis=0).astype(BF16)


def _filter_spec_body(ar_ref, ai_ref, g_ref, l1_ref, kr_ref, ki_ref):
    y = _dot(g_ref[...], _plane_rows(ar_ref, ai_ref))
    inv = 1.0 / l1_ref[0:1, :]
    kr_ref[...] = (y[:DFT_R] * inv).astype(BF16)
    ki_ref[...] = (y[DFT_R:] * inv).astype(BF16)


def _filter_spectrum(ar, ai, fwd, l1, width):
    r = DFT_R
    a_spec = BS((None, width // LANES, None, r, LANES), lambda v, o: (0, o, v, 0, 0))
    k_spec = BS((None, r, width), lambda v, o: (v, 0, o))
    out = SDS((DFT_VP, r, HYENA_ORDER * width), BF16)
    return pl.pallas_call(
        _filter_spec_body,
        out_shape=(out, out),
        grid=(DFT_VP, HYENA_ORDER),
        in_specs=[a_spec, a_spec, BS((None, 2 * r, 2 * r), lambda v, o: (v, 0, 0)),
                  BS((SUBLANES, width), lambda v, o: (0, o))],
        out_specs=(k_spec, k_spec),
        compiler_params=_cparams("arbitrary", "arbitrary"),
        name="hyena_filter_spectrum",
    )(ar, ai, fwd, l1)


def _conv_mid_body(ar_ref, ai_ref, g_ref, h_ref, kr_ref, ki_ref, qr_ref, qi_ref):
    y = _dot(g_ref[...], _plane_rows(ar_ref, ai_ref))
    yr, yi = y[:DFT_R], y[DFT_R:]
    kr, ki = kr_ref[...].astype(F32), ki_ref[...].astype(F32)
    p = jnp.concatenate([yr * kr - yi * ki, yr * ki + yi * kr], axis=0).astype(BF16)
    q = _dot(h_ref[...], p)
    for t in range(qr_ref.shape[0]):
        lanes = slice(t * LANES, (t + 1) * LANES)
        qr_ref[t] = q[:DFT_R, lanes]
        qi_ref[t] = q[DFT_R:, lanes]


def _conv_mid(ar, ai, fwd, inv, kr, ki, order):
    nb, tiles, vp, r, _ = ar.shape
    c = tiles * LANES
    a_spec = BS((None, tiles, None, r, LANES), lambda n, v: (n, 0, v, 0, 0))
    m_spec = BS((None, 2 * r, 2 * r), lambda n, v: (v, 0, 0))
    k_spec = BS((None, r, c), lambda n, v: (v, 0, order))
    out = SDS((nb, tiles, vp, r, LANES), F32)
    return pl.pallas_call(
        _conv_mid_body,
        out_shape=(out, out),
        grid=(nb, vp),
        in_specs=[a_spec, a_spec, m_spec, m_spec, k_spec, k_spec],
        out_specs=(a_spec, a_spec),
        compiler_params=_cparams("arbitrary", "arbitrary"),
        name="hyena_spectral_product",
    )(ar, ai, fwd, inv, kr, ki)


def _idft_gate_body(*refs):
    q = DFT_LANE_TILES
    qr_ref, qi_ref = refs[0], refs[1]
    y_refs, gate_refs = refs[2:2 + q], refs[2 + q:2 + 2 * q]
    f3_ref, fb_ref, o_ref = refs[2 + 2 * q:]
    f3 = f3_ref[...]
    for t in range(q):
        lanes = slice(t * LANES, (t + 1) * LANES)
        fb = fb_ref[:, lanes]
        for j in range(DFT_BGROUP):
            planes = jnp.concatenate([_rows_of_position(qr_ref.at[t], j),
                                      _rows_of_position(qi_ref.at[t], j)], axis=0).astype(BF16)
            conv = _dot(f3, planes)
            o_ref[:, j, lanes] = _rows_of_position(gate_refs[t], j) * (
                conv + fb * _rows_of_position(y_refs[t], j))


def _idft_gate(qr, qi, f3, y4, gate4, fbias):
    nb, na, r, c = y4.shape
    q, bg = DFT_LANE_TILES, DFT_BGROUP
    tile_map = lambda t: (lambda i, j, cc: (i, 0, j, cc * q + t))
    wide = lambda rows: BS((None, rows, bg, q * LANES), lambda i, j, cc: (i, 0, j, cc))
    plane_spec = BS((None, q, DFT_VP, bg, LANES), lambda i, j, cc: (i, cc, 0, j, 0))
    return pl.pallas_call(
        _idft_gate_body,
        out_shape=SDS((nb, na, r, c), F32),
        grid=(nb, r // bg, c // (q * LANES)),
        in_specs=([plane_spec, plane_spec]
                  + _lane_tile_specs(na, tile_map) + _lane_tile_specs(na, tile_map)
                  + [BS((na, 2 * DFT_VP), lambda i, j, cc: (0, 0)),
                     BS((1, q * LANES), lambda i, j, cc: (0, cc))]),
        out_specs=wide(na),
        compiler_params=_cparams("arbitrary", "arbitrary", "arbitrary"),
        name="hyena_idft_gate",
    )(qr, qi, *([y4] * q + [gate4] * q), f3, fbias)


def _outproj_body(y_ref, w_ref, x_ref, mod_ref, o_ref):
    o_ref[0] = x_ref[0] + mod_ref[0, 2:3, :] * _dot(y_ref[0].astype(BF16), w_ref[...])


def _outproj(y, w_out, x, mod_lat):
    b, l, d = x.shape
    tm = TM_PROJ
    wdt = y.shape[-1]
    return pl.pallas_call(
        _outproj_body,
        out_shape=SDS((b, l, d), F32),
        grid=(b, l // tm),
        in_specs=[BS((1, tm, wdt), lambda bi, i: (bi, i, 0)),
                  BS((wdt, d), lambda bi, i: (0, 0)),
                  BS((1, tm, d), lambda bi, i: (bi, i, 0)),
                  BS((1, SUBLANES, d), lambda bi, i: (bi, 0, 0))],
        out_specs=BS((1, tm, d), lambda bi, i: (bi, i, 0)),
        compiler_params=_cparams("arbitrary", "arbitrary"),
        name="outproj_c",
    )(y, w_out, x, mod_lat)


def _hyena(x, mod_lat, g, w_in, conv_w, conv_b, w1, b1, w2, b2, w3, freq, delta, f_bias, w_out):
    b, n, d = x.shape
    width = w_out.shape[0]
    r = DFT_R
    na = n // r
    f1, f1_full, fwd, inv, f3 = _dft_tables(n)
    v, gate1, gate2 = _inproj_c(x, mod_lat, g, w_in.astype(BF16), conv_w, conv_b.reshape(1, -1))

    pos = jnp.arange(2 * n, dtype=F32)
    t = jnp.where(pos < n, pos, 2 * n - pos)
    t01 = t / max(n - 1, 1)
    bands = jnp.linspace(1e-4, FILT_BANDS - 1, FILT_BANDS, dtype=F32)
    ang = (2.0 * math.pi / n) * t[:, None] * bands[None, :]
    feats = jnp.concatenate([t01[:, None], jnp.cos(ang), jnp.sin(ang)], axis=-1)
    fe = feats.shape[1]
    feats = jnp.pad(feats, ((0, 0), (0, LANES - fe)))
    w1p = jnp.pad(w1, ((0, LANES - fe), (0, 0)))
    k_circ, l1 = _filters(feats, w1p, b1.reshape(1, -1), w2, b2.reshape(1, -1), w3,
                          freq.reshape(1, -1), delta.reshape(1, -1), width)
    far, fai = _dft_s1(k_circ.reshape(1, 2 * na, r, k_circ.shape[1]), f1_full)
    kr, ki = _filter_spectrum(far, fai, fwd, l1, width)

    y4 = v.reshape(b, na, r, width)
    for o, gate in enumerate((gate1, gate2)):
        ar, ai = _dft_s1(y4, f1)
        qr, qi = _conv_mid(ar, ai, fwd, inv, kr, ki, o)
        y4 = _idft_gate(qr, qi, f3, y4, gate.reshape(b, na, r, width), f_bias[o].reshape(1, width))
    return _outproj(y4.reshape(b, n, width), w_out.astype(BF16), x, mod_lat)


def _rope_tables(seq_len):
    rows = seq_len // GRID_W
    row = jnp.repeat(jnp.arange(rows, dtype=F32), GRID_W)
    col = jnp.tile(jnp.arange(GRID_W, dtype=F32), rows)
    inv = jnp.power(ROPE_BASE, -jnp.arange(ROPE_FREQS, dtype=F32) / ROPE_FREQS)
    ar, ac = row[:, None] * inv, col[:, None] * inv
    cos_h = jnp.concatenate([jnp.cos(ar), jnp.cos(ar), jnp.cos(ac), jnp.cos(ac)], axis=1)
    sin_h = jnp.concatenate([-jnp.sin(ar), jnp.sin(ar), -jnp.sin(ac), jnp.sin(ac)], axis=1)
    reps = LANES // HEAD_DIM
    return jnp.tile(cos_h, (1, reps)), jnp.tile(sin_h, (1, reps))


def _rotate_partner_columns(w):
    ncol = w.shape[1]
    lane = np.arange(ncol)
    partner = np.where((lane % (2 * ROPE_FREQS)) < ROPE_FREQS, lane + ROPE_FREQS, lane - ROPE_FREQS)
    return w[:, partner]


def kernel(x, c, ctx, c_ctx, w_mod, b_mod, norm_g, w_in_ab, sink, w_spatial, b_spatial, w_out_ab,
           w_in_c, conv_w, conv_b, filt_w1, filt_b1, filt_w2, filt_b2, filt_w3, filt_freq,
           filt_delta, filt_bias, w_out_c, w_router, e_bias, w_gate, w_up, w_down, ws_gate,
           ws_up, ws_down, final_g):
    b, l, d = x.shape
    depth = w_mod.shape[0]
    assert depth == 2 and b + 1 <= SUBLANES

    cc = jnp.zeros((SUBLANES, d), F32).at[:b].set(c).at[b].set(c_ctx)
    m_all = _mod_vectors(cc, w_mod, b_mod)

    def mod_rows(layer, row0, nrow):
        m = m_all[layer, row0:row0 + nrow].reshape(nrow, 6, d)
        return jnp.pad(m, ((0, 0), (0, SUBLANES - 6), (0, 0)))

    tri = jnp.triu(jnp.ones((MOE_TILE, MOE_TILE), F32), k=1).astype(BF16)

    mod_lat = mod_rows(0, 0, b)
    mod_ctx = mod_rows(0, b, 1)[0]
    w_in = w_in_ab[0]
    qk = ATTN_WIDTH + KV_WIDTH
    w_cat = jnp.concatenate([w_in, _rotate_partner_columns(w_in[:, :qk])], axis=1).astype(BF16)
    cos_t, sin_t = _rope_tables(l)
    group_avg = jnp.kron(jnp.eye(N_SG_GROUPS, dtype=F32),
                         jnp.full((SG_GROUP_DIM, SG_GROUP_DIM), 1.0 / SG_GROUP_DIM, F32)).astype(BF16)
    kc, vc = _ctx_kv(ctx, mod_ctx, norm_g[0, 0].reshape(1, d),
                     w_in[:, ATTN_WIDTH:ATTN_WIDTH + 2 * KV_WIDTH].astype(BF16))
    q, k, v, ug, vn = _inproj_ab(x, mod_lat, norm_g[0, 0].reshape(1, d), w_cat, cos_t, sin_t, group_avg)
    b_full = jnp.repeat(b_spatial[0].T, SG_GROUP_DIM, axis=1)
    x1 = _mixer(sink[0], q, k, v, kc, vc, ug, vn, w_spatial[0].astype(BF16), b_full,
                w_out_ab[0].astype(BF16), x, mod_lat)
    x2 = _moe(x1.reshape(b * l, d), mod_lat, norm_g[0, 1].reshape(1, d), w_router[0], e_bias[0],
              w_gate, w_up, w_down, ws_gate[0], ws_up[0], ws_down[0], final_g, tri, l,
              layer=0, final=False).reshape(b, l, d)

    mod_lat = mod_rows(1, 0, b)
    x3 = _hyena(x2, mod_lat, norm_g[1, 0].reshape(1, d), w_in_c[0], conv_w[0], conv_b[0],
                filt_w1[0], filt_b1[0], filt_w2[0], filt_b2[0], filt_w3[0], filt_freq[0],
                filt_delta[0], filt_bias[0], w_out_c[0])
    out = _moe(x3.reshape(b * l, d), mod_lat, norm_g[1, 1].reshape(1, d), w_router[1], e_bias[1],
               w_gate, w_up, w_down, ws_gate[1], ws_up[1], ws_down[1], final_g, tri, l,
               layer=1, final=True)
    return out.reshape(b, l, d)
```

```python
import functools
import math

import numpy as np
import jax
import jax.numpy as jnp
from jax import lax
from jax.experimental import pallas as pl
from jax.experimental.pallas import tpu as pltpu

F32 = jnp.float32
BF16 = jnp.bfloat16
I32 = jnp.int32
HIGHEST = lax.Precision.HIGHEST
SDS = jax.ShapeDtypeStruct
BS = pl.BlockSpec

EPS = 1e-6
NEG = -1e30

GRID_W = 64
N_Q_HEADS = 8
N_KV_HEADS = 2
HEAD_DIM = 64
ATTN_WIDTH = N_Q_HEADS * HEAD_DIM
KV_WIDTH = N_KV_HEADS * HEAD_DIM
WINDOW = 128
BLOCK = 128
ROPE_BASE = 10000.0
ROPE_FREQS = HEAD_DIM // 4
N_SG_GROUPS = 8
SG_GROUP_DIM = 64
SG_WIDTH = N_SG_GROUPS * SG_GROUP_DIM
HYENA_ORDER = 2
FILT_BANDS = 16
DECAY_SHIFT = 0.05
N_EXPERTS = 64
TOP_K = 8
N_GROUPS = 8
TOPK_GROUPS = 4
ROUTED_SCALE = 2.5

LANES = 128
SUBLANES = 8
MXU_COLS = 256
VMEM_LIMIT = 56 * 1024 * 1024

TM_PROJ = 512
TQ_MIX = 256
MOE_TILE = 256
BM_FFN = 512
RUN_ALIGN = 16
SLOT_CHUNK = 512
NO_SLOT = 256 * 256 - 1
LOCAL_SLOTS = -(-(TOP_K * MOE_TILE + N_EXPERTS * (RUN_ALIGN - 1)) // SLOT_CHUNK) * SLOT_CHUNK
COPY_FIELDS = 3
FFN_IN_BUFS = 4


def _cparams(*sem):
    return pltpu.CompilerParams(dimension_semantics=sem, vmem_limit_bytes=VMEM_LIMIT)


def _dot(a, b):
    return jnp.dot(a, b, preferred_element_type=F32)


def _dot_nt(a, b):
    return lax.dot_general(a, b, (((1,), (1,)), ((), ())), preferred_element_type=F32)


def _dot_hp(a, b):
    return jnp.dot(a, b, preferred_element_type=F32, precision=HIGHEST)


def _norm_mod(x, g, sc, sh):
    ms = jnp.mean(x * x, axis=-1, keepdims=True)
    y = x * lax.rsqrt(ms + EPS)
    return (y * g) * (1.0 + sc) + sh


def _gelu_tanh(x):
    c = math.sqrt(2.0 / math.pi)
    return 0.5 * x * (1.0 + jnp.tanh(c * (x + 0.044715 * (x * x * x))))


def _silu(x):
    return x * jax.nn.sigmoid(x)


def _mod_body(c_ref, w_ref, b_ref, o_ref):
    o_ref[0] = _dot_hp(_silu(c_ref[...]), w_ref[0]) + b_ref[0]


def _mod_vectors(cc, w_mod, b_mod):
    depth, d, n = w_mod.shape
    tn = 1536
    return pl.pallas_call(
        _mod_body,
        out_shape=SDS((depth, SUBLANES, n), F32),
        grid=(depth, n // tn),
        in_specs=[BS((SUBLANES, d), lambda l, j: (0, 0)),
                  BS((1, d, tn), lambda l, j: (l, 0, j)),
                  BS((1, 1, tn), lambda l, j: (l, 0, j))],
        out_specs=BS((1, SUBLANES, tn), lambda l, j: (l, 0, j)),
        compiler_params=_cparams("arbitrary", "arbitrary"),
        name="mod_vectors",
    )(cc, w_mod, b_mod.reshape(depth, 1, n))


def _ctx_kv_body(ctx_ref, mod_ref, g_ref, w_ref, kc_ref, vc_ref):
    h = _norm_mod(ctx_ref[0], g_ref[...], mod_ref[1:2, :], mod_ref[0:1, :])
    z = _dot(h.astype(BF16), w_ref[...])
    kc_ref[0] = z[:, :KV_WIDTH].astype(BF16)
    vc_ref[0] = z[:, KV_WIDTH:].astype(BF16)


def _ctx_kv(ctx, mod_ctx, g, w_kv):
    b, c, d = ctx.shape
    return pl.pallas_call(
        _ctx_kv_body,
        out_shape=(SDS((b, c, KV_WIDTH), BF16), SDS((b, c, KV_WIDTH), BF16)),
        grid=(b,),
        in_specs=[BS((1, c, d), lambda i: (i, 0, 0)),
                  BS((SUBLANES, d), lambda i: (0, 0)),
                  BS((1, d), lambda i: (0, 0)),
                  BS((d, 2 * KV_WIDTH), lambda i: (0, 0))],
        out_specs=(BS((1, c, KV_WIDTH), lambda i: (i, 0, 0)),
                   BS((1, c, KV_WIDTH), lambda i: (i, 0, 0))),
        compiler_params=_cparams("arbitrary"),
        name="ctx_kv",
    )(ctx, mod_ctx, g, w_kv)


def _inproj_ab_body(x_ref, mod_ref, g_ref, w_ref, cos_ref, sin_ref, avg_ref,
                    q_ref, k_ref, v_ref, ug_ref, vn_ref):
    h = _norm_mod(x_ref[0], g_ref[...], mod_ref[0, 1:2, :], mod_ref[0, 0:1, :]).astype(BF16)
    cs = cos_ref[...]
    sn = sin_ref[...]
    rot0 = ATTN_WIDTH + 2 * KV_WIDTH + 2 * SG_WIDTH
    scale = HEAD_DIM ** -0.5
    for j in range(ATTN_WIDTH // MXU_COLS):
        z = _dot(h, w_ref[:, j * MXU_COLS:(j + 1) * MXU_COLS])
        zr = _dot(h, w_ref[:, rot0 + j * MXU_COLS:rot0 + (j + 1) * MXU_COLS])
        for s in range(MXU_COLS // LANES):
            part = slice(s * LANES, (s + 1) * LANES)
            lanes = slice(j * MXU_COLS + s * LANES, j * MXU_COLS + (s + 1) * LANES)
            q_ref[0, :, lanes] = ((z[:, part] * cs + zr[:, part] * sn) * scale).astype(BF16)
    zkv = _dot(h, w_ref[:, ATTN_WIDTH:ATTN_WIDTH + 2 * KV_WIDTH])
    zkr = _dot(h, w_ref[:, rot0 + ATTN_WIDTH:rot0 + ATTN_WIDTH + KV_WIDTH])
    k_ref[0] = (zkv[:, :KV_WIDTH] * cs + zkr * sn).astype(BF16)
    v_ref[0] = zkv[:, KV_WIDTH:].astype(BF16)
    u0 = ATTN_WIDTH + 2 * KV_WIDTH
    ug_ref[0] = _gelu_tanh(_dot(h, w_ref[:, u0:u0 + SG_WIDTH]))
    vf = _gelu_tanh(_dot(h, w_ref[:, u0 + SG_WIDTH:u0 + 2 * SG_WIDTH]))
    avg = avg_ref[...]

    hi = vf.astype(BF16)
    mean = _dot(hi, avg) + _dot((vf - hi.astype(F32)).astype(BF16), avg)
    vc = vf - mean
    var = _dot((vc * vc).astype(BF16), avg)
    vn_ref[0] = (vc * lax.rsqrt(var + EPS)).astype(BF16)


def _inproj_ab(x, mod_lat, g, w_cat, cos_t, sin_t, avg):
    b, l, d = x.shape
    tm = TM_PROJ
    ncol = w_cat.shape[1]
    return pl.pallas_call(
        _inproj_ab_body,
        out_shape=(SDS((b, l, ATTN_WIDTH), BF16), SDS((b, l, KV_WIDTH), BF16),
                   SDS((b, l, KV_WIDTH), BF16), SDS((b, l, SG_WIDTH), F32),
                   SDS((b, l, SG_WIDTH), BF16)),
        grid=(b, l // tm),
        in_specs=[BS((1, tm, d), lambda bi, i: (bi, i, 0)),
                  BS((1, SUBLANES, d), lambda bi, i: (bi, 0, 0)),
                  BS((1, d), lambda bi, i: (0, 0)),
                  BS((d, ncol), lambda bi, i: (0, 0)),
                  BS((tm, LANES), lambda bi, i: (i, 0)),
                  BS((tm, LANES), lambda bi, i: (i, 0)),
                  BS((SG_WIDTH, SG_WIDTH), lambda bi, i: (0, 0))],
        out_specs=(BS((1, tm, ATTN_WIDTH), lambda bi, i: (bi, i, 0)),
                   BS((1, tm, KV_WIDTH), lambda bi, i: (bi, i, 0)),
                   BS((1, tm, KV_WIDTH), lambda bi, i: (bi, i, 0)),
                   BS((1, tm, SG_WIDTH), lambda bi, i: (bi, i, 0)),
                   BS((1, tm, SG_WIDTH), lambda bi, i: (bi, i, 0))),
        compiler_params=_cparams("arbitrary", "arbitrary"),
        name="inproj_ab",
    )(x, mod_lat, g, w_cat, cos_t, sin_t, avg)


def _mixer_body(sink_ref, q_ref, kp_ref, kcur_ref, kn_ref, vp_ref, vcur_ref, vn_ref,
                kc_ref, vc_ref, ug_ref, vnorm_ref, ws_ref, bs_ref, wout_ref, x_ref, mod_ref,
                o_ref, cat_ref, *, seq_len, sub_blocks):
    i = pl.program_id(1)
    kk = jnp.concatenate([kp_ref[0], kcur_ref[0], kn_ref[0]], axis=0)
    vv = jnp.concatenate([vp_ref[0], vcur_ref[0], vn_ref[0]], axis=0)
    kc = kc_ref[0]
    vc = vc_ref[0]
    span = 3 * BLOCK
    ii = lax.broadcasted_iota(I32, (BLOCK, span), 0)
    jj = lax.broadcasted_iota(I32, (BLOCK, span), 1)
    dd = jj - ii
    in_window = jnp.where(dd >= 0, jnp.where(dd <= 2 * WINDOW, 1, 0), 0)
    group = N_Q_HEADS // N_KV_HEADS
    for r in range(sub_blocks):
        rows = slice(r * BLOCK, (r + 1) * BLOCK)
        kpos = (i * sub_blocks + r - 1) * BLOCK + jj
        in_seq = jnp.where(kpos >= 0, jnp.where(kpos < seq_len, 1, 0), 0)
        bias = jnp.where(in_window * in_seq > 0, 0.0, NEG)
        qb = q_ref[0, rows, :]
        kl = kk[r * BLOCK:r * BLOCK + span]
        vl = vv[r * BLOCK:r * BLOCK + span]
        for hq in range(N_Q_HEADS):
            hk = hq // group
            ks = slice(hk * HEAD_DIM, (hk + 1) * HEAD_DIM)
            qh = qb[:, hq * HEAD_DIM:(hq + 1) * HEAD_DIM]
            s_loc = _dot_nt(qh, kl[:, ks]) + bias
            s_ctx = _dot_nt(qh, kc[:, ks])
            sk = sink_ref[hq]
            m = jnp.maximum(jnp.maximum(jnp.max(s_loc, axis=-1, keepdims=True),
                                        jnp.max(s_ctx, axis=-1, keepdims=True)), sk)
            p_loc = jnp.exp(s_loc - m)
            p_ctx = jnp.exp(s_ctx - m)
            den = (jnp.sum(p_loc, axis=-1, keepdims=True) + jnp.sum(p_ctx, axis=-1, keepdims=True)
                   + jnp.exp(sk - m))
            o = _dot(p_loc.astype(BF16), vl[:, ks]) + _dot(p_ctx.astype(BF16), vc[:, ks])
            cat_ref[rows, hq * HEAD_DIM:(hq + 1) * HEAD_DIM] = (o / den).astype(BF16)
        vnb = vnorm_ref[0, rows, :]
        ugb = ug_ref[0, rows, :]
        for g in range(N_SG_GROUPS):
            gs = slice(g * SG_GROUP_DIM, (g + 1) * SG_GROUP_DIM)
            sg = _dot(ws_ref[g], vnb[:, gs]) + bs_ref[:, gs]
            cat_ref[rows, ATTN_WIDTH + g * SG_GROUP_DIM:ATTN_WIDTH + (g + 1) * SG_GROUP_DIM] = (
                ugb[:, gs] * sg).astype(BF16)
    y = _dot(cat_ref[...], wout_ref[...])
    o_ref[0] = x_ref[0] + mod_ref[0, 2:3, :] * y


def _mixer(sink, q, k, v, kc, vc, ug, vn, w_s, b_full, w_out, x, mod_lat):
    b, l, d = x.shape
    tq = TQ_MIX
    r = tq // BLOCK
    nb = l // BLOCK
    c = kc.shape[1]
    prev_map = lambda bi, i: (bi, jnp.maximum(i * r - 1, 0), 0)
    next_map = lambda bi, i: (bi, jnp.minimum((i + 1) * r, nb - 1), 0)
    cur_map = lambda bi, i: (bi, i, 0)
    body = functools.partial(_mixer_body, seq_len=l, sub_blocks=r)
    return pl.pallas_call(
        body,
        out_shape=SDS((b, l, d), F32),
        grid=(b, l // tq),
        in_specs=[BS(memory_space=pltpu.SMEM),
                  BS((1, tq, ATTN_WIDTH), cur_map),
                  BS((1, BLOCK, KV_WIDTH), prev_map), BS((1, tq, KV_WIDTH), cur_map),
                  BS((1, BLOCK, KV_WIDTH), next_map),
                  BS((1, BLOCK, KV_WIDTH), prev_map), BS((1, tq, KV_WIDTH), cur_map),
                  BS((1, BLOCK, KV_WIDTH), next_map),
                  BS((1, c, KV_WIDTH), lambda bi, i: (bi, 0, 0)),
                  BS((1, c, KV_WIDTH), lambda bi, i: (bi, 0, 0)),
                  BS((1, tq, SG_WIDTH), cur_map), BS((1, tq, SG_WIDTH), cur_map),
                  BS((N_SG_GROUPS, BLOCK, BLOCK), lambda bi, i: (0, 0, 0)),
                  BS((BLOCK, SG_WIDTH), lambda bi, i: (0, 0)),
                  BS((d, d), lambda bi, i: (0, 0)),
                  BS((1, tq, d), cur_map),
                  BS((1, SUBLANES, d), lambda bi, i: (bi, 0, 0))],
        out_specs=BS((1, tq, d), cur_map),
        scratch_shapes=[pltpu.VMEM((tq, d), BF16)],
        compiler_params=_cparams("arbitrary", "arbitrary"),
        name="mixer_ab",
    )(sink, q, k, k, k, v, v, v, kc, vc, ug, vn, w_s, b_full, w_out, x, mod_lat)


def _router_body(x_ref, mod_ref, g_ref, wr_ref, eb_ref, tri_ref,
                 h2b_ref, slot_hi_ref, slot_lo_ref, slot_hi_t_ref, slot_lo_t_ref, gate_t_ref,
                 lo_col_ref, hi_col_ref, bounds_row_ref, copy_ref, rows_ref, cnt_ref, carry_ref,
                 *, tm):
    i = pl.program_id(0)

    @pl.when(i == 0)
    def _():
        carry_ref[...] = jnp.zeros_like(carry_ref)

    h2 = _norm_mod(x_ref[...], g_ref[...], mod_ref[0, 4:5, :], mod_ref[0, 3:4, :])
    h2b_ref[...] = h2.astype(BF16)

    logits = lax.dot_general(wr_ref[...], h2, (((1,), (1,)), ((), ())),
                             preferred_element_type=F32, precision=HIGHEST)
    scores = jax.nn.sigmoid(logits)
    per_group = N_EXPERTS // N_GROUPS
    shape3 = (N_GROUPS, per_group, tm)
    s3 = scores.reshape(shape3)
    b3 = (scores + eb_ref[...]).reshape(shape3)
    sub = lax.broadcasted_iota(I32, shape3, 1)
    eid = lax.broadcasted_iota(I32, shape3, 0) * per_group + sub

    m1 = jnp.max(b3, axis=1, keepdims=True)
    i1 = jnp.min(jnp.where(b3 == m1, sub, per_group), axis=1, keepdims=True)
    m2 = jnp.max(jnp.where(sub == i1, -jnp.inf, b3), axis=1, keepdims=True)
    gs = m1 + m2
    keep = []
    for g in range(N_GROUPS):
        beaten = jnp.zeros((1, tm), I32)
        for g2 in range(N_GROUPS):
            if g2 == g:
                continue
            wins = (gs[g2] >= gs[g]) if g2 < g else (gs[g2] > gs[g])
            beaten = beaten + jnp.where(wins, 1, 0)
        keep.append(jnp.where(beaten < TOPK_GROUPS, 1, 0)[None])
    keep3 = jnp.concatenate(keep, axis=0)
    val = jnp.where(keep3 > 0, b3, -jnp.inf)

    def red(fn, a):
        return fn(fn(a, axis=0, keepdims=True), axis=1, keepdims=True)

    member = jnp.zeros(shape3, F32)
    for _ in range(TOP_K):
        m = red(jnp.max, val)
        idx = red(jnp.min, jnp.where(val == m, eid, N_EXPERTS))
        hit = eid == idx
        val = jnp.where(hit, -jnp.inf, val)
        member = member + jnp.where(hit, 1.0, 0.0)
    picked = jnp.where(member > 0.0, s3, 0.0)
    gate3 = picked / red(jnp.sum, picked) * ROUTED_SCALE

    member2 = member.reshape(N_EXPERTS, tm)
    cnt = jnp.sum(member2, axis=1, keepdims=True)
    runlen = jnp.floor((cnt + (RUN_ALIGN - 1)) * (1.0 / RUN_ALIGN)) * RUN_ALIGN
    runlen_b = jnp.broadcast_to(runlen, (N_EXPERTS, LANES))
    e_row = lax.broadcasted_iota(I32, (N_EXPERTS, N_EXPERTS), 0)
    e_col = lax.broadcasted_iota(I32, (N_EXPERTS, N_EXPERTS), 1)
    earlier = jnp.where(e_col < e_row, 1.0, 0.0).astype(BF16)
    loff = _dot(earlier, runlen_b.astype(BF16))
    slot = _dot(member2.astype(BF16), tri_ref[...]) + loff[:, 0:1]
    slot = jnp.where(member2 > 0.0, slot, float(NO_SLOT))
    slot_hi = jnp.floor(slot * (1.0 / 256.0))
    slot_lo = slot - 256.0 * slot_hi
    gate = gate3.reshape(N_EXPERTS, tm)
    no_expert = jnp.zeros((LANES - N_EXPERTS, tm), F32)
    pad_e = lambda a: jnp.concatenate([a, no_expert], axis=0)
    slot_hi_ref[0] = pad_e(slot_hi).astype(BF16)
    slot_lo_ref[0] = pad_e(slot_lo).astype(BF16)
    slot_hi_t_ref[0] = pad_e(slot_hi).T.astype(BF16)
    slot_lo_t_ref[0] = pad_e(slot_lo).T.astype(BF16)
    gate_t_ref[0] = pad_e(gate).T.astype(BF16)
    run_lo = loff
    run_hi = loff + runlen_b
    no_run = jnp.zeros((LANES - N_EXPERTS, LANES), F32)
    lo_col_ref[0] = jnp.concatenate([run_lo, no_run], axis=0)
    hi_col_ref[0] = jnp.concatenate([run_hi, no_run], axis=0)
    diag = (lax.broadcasted_iota(I32, (N_EXPERTS, LANES), 0)
            == lax.broadcasted_iota(I32, (N_EXPERTS, LANES), 1))
    bounds_row_ref[0] = jnp.concatenate(
        [jnp.sum(jnp.where(diag, run_lo, 0.0), axis=0, keepdims=True),
         jnp.sum(jnp.where(diag, run_hi, 0.0), axis=0, keepdims=True),
         jnp.zeros((SUBLANES - 2, LANES), F32)], axis=0)
    big = 2.0 * RUN_ALIGN
    n_chunks = runlen_b * (1.0 / RUN_ALIGN)
    n_big = jnp.floor(n_chunks * 0.5)
    n_small = n_chunks - 2.0 * n_big
    before_big = _dot(earlier, n_big.astype(BF16))
    before_small = _dot(earlier, n_small.astype(BF16))
    gbase = carry_ref[...]
    j = lax.broadcasted_iota(I32, (N_EXPERTS, LANES), 1).astype(F32)
    e_iota = lax.broadcasted_iota(I32, (N_EXPERTS, LANES), 0).astype(F32)

    def entries(before, count, loc0, rel0, step):
        e_of = jnp.sum(jnp.where(before + count <= j, 1.0, 0.0), axis=0, keepdims=True)
        pick = lambda x: jnp.sum(jnp.where(e_iota == e_of, x, 0.0), axis=0, keepdims=True)
        k = j[0:1, :] - pick(before)
        return jnp.minimum(e_of, N_EXPERTS - 1.0), pick(loc0) + step * k, pick(rel0) + step * k

    e_b, loc_b, rel_b = entries(before_big, n_big, loff, gbase, big)
    e_s, loc_s, rel_s = entries(before_small, n_small, loff + big * n_big, gbase + big * n_big, 0.0)
    copy_ref[0] = jnp.concatenate([jnp.concatenate([e_b, e_s], axis=1),
                                   jnp.concatenate([loc_b, loc_s], axis=1),
                                   jnp.concatenate([rel_b, rel_s], axis=1)], axis=0).astype(I32)
    last = slice(N_EXPERTS - 1, N_EXPERTS)
    lane = lax.broadcasted_iota(I32, (1, LANES), 1)
    rows_used = loff[last, :] + runlen_b[last, :]
    counts = jnp.where(lane == 0, rows_used,
                       jnp.where(lane == 1, before_big[last, :] + n_big[last, :],
                                 before_small[last, :] + n_small[last, :]))
    rows_ref[0] = counts.astype(I32)
    total = carry_ref[...] + runlen_b
    carry_ref[...] = total
    cnt_ref[...] = total


def _router(x_flat, mod_lat, g, wr_t, e_bias, tri, tokens_per_batch):
    t, d = x_flat.shape
    tm = MOE_TILE
    n_tiles = t // tm
    tiles_per_batch = tokens_per_batch // tm
    body = functools.partial(_router_body, tm=tm)
    table = SDS((n_tiles, COPY_FIELDS, 2 * LANES), I32)
    table_spec = BS((1, COPY_FIELDS, 2 * LANES), lambda i: (i, 0, 0))
    expert_major = SDS((n_tiles, LANES, tm), BF16)
    expert_major_spec = BS((1, LANES, tm), lambda i: (i, 0, 0))
    token_major = SDS((n_tiles, tm, LANES), BF16)
    token_major_spec = BS((1, tm, LANES), lambda i: (i, 0, 0))
    per_expert = SDS((n_tiles, LANES, LANES), F32)
    per_expert_spec = BS((1, LANES, LANES), lambda i: (i, 0, 0))
    return pl.pallas_call(
        body,
        out_shape=(SDS((t, d), BF16),
                   expert_major, expert_major, token_major, token_major, token_major,
                   per_expert, per_expert, SDS((n_tiles, SUBLANES, LANES), F32),
                   table, SDS((n_tiles, 1, LANES), I32), SDS((N_EXPERTS, LANES), F32)),
        grid=(n_tiles,),
        in_specs=[BS((tm, d), lambda i: (i, 0)),
                  BS((1, SUBLANES, d), lambda i: (i // tiles_per_batch, 0, 0)),
                  BS((1, d), lambda i: (0, 0)),
                  BS((N_EXPERTS, d), lambda i: (0, 0)),
                  BS((N_EXPERTS, 1), lambda i: (0, 0)),
                  BS((tm, tm), lambda i: (0, 0))],
        out_specs=(BS((tm, d), lambda i: (i, 0)),
                   expert_major_spec, expert_major_spec, token_major_spec, token_major_spec,
                   token_major_spec, per_expert_spec, per_expert_spec,
                   BS((1, SUBLANES, LANES), lambda i: (i, 0, 0)),
                   table_spec, BS((1, 1, LANES), lambda i: (i, 0, 0)),
                   BS((N_EXPERTS, LANES), lambda i: (0, 0))),
        scratch_shapes=[pltpu.VMEM((N_EXPERTS, LANES), F32)],
        compiler_params=_cparams("arbitrary"),
        name="moe_router",
    )(x_flat, mod_lat, g, wr_t, e_bias, tri)


def _start_run_copies(pstart_ref, copy_ref, n_big, n_small, run_copy):
    def entry(base, nrows):
        def body(j, carry):
            slot0 = pstart_ref[copy_ref[0, 0, base + j]] + copy_ref[0, 2, base + j]
            run_copy(pl.multiple_of(copy_ref[0, 1, base + j], RUN_ALIGN),
                     pl.multiple_of(slot0, RUN_ALIGN), nrows).start()
            return carry
        return body

    lax.fori_loop(0, n_big, entry(0, 2 * RUN_ALIGN), 0)
    lax.fori_loop(0, n_small, entry(LANES, RUN_ALIGN), 0)


def _wait_run_rows(copy_of_rows, rows):
    def wait_n(nrows):
        def body(_, carry):
            copy_of_rows(nrows).wait()
            return carry
        return body

    lax.fori_loop(0, rows // SLOT_CHUNK, wait_n(SLOT_CHUNK), 0)
    lax.fori_loop(0, (rows % SLOT_CHUNK) // RUN_ALIGN, wait_n(RUN_ALIGN), 0)


def _dispatch_body(pend_ref, pcnt_ref, nu_ref, pstart_ref, tile_ref, copy_ref,
                   slot_hi_ref, slot_lo_ref, bounds_ref, h_ref, xs_ref, loc_ref, zbuf_ref, sem, zsem,
                   tsem, *, tl, bm, n_blocks):
    i = pl.program_id(0)
    last = pl.num_programs(0) - 1
    buf = i % 2

    def zero_copy(row0):
        return pltpu.make_async_copy(
            zbuf_ref, xs_ref.at[pl.ds(pl.multiple_of(row0, RUN_ALIGN), bm), :], zsem)

    def tail_copy(blk):
        return pltpu.make_async_copy(
            zbuf_ref, xs_ref.at[pl.ds(pl.multiple_of(blk * bm, bm), bm), :], tsem)

    @pl.when(i == 0)
    def _():
        zbuf_ref[...] = jnp.zeros_like(zbuf_ref)

        def start(e, c):
            @pl.when(pcnt_ref[e] > 0)
            def _():
                zero_copy(pend_ref[e] - bm).start()
            return c

        def wait(e, c):
            @pl.when(pcnt_ref[e] > 0)
            def _():
                zero_copy(pend_ref[e] - bm).wait()
            return c

        def start_tail(j, c):
            tail_copy(j).start()
            return c

        lax.fori_loop(0, N_EXPERTS, start, 0)
        lax.fori_loop(nu_ref[0], n_blocks, start_tail, 0)
        lax.fori_loop(0, N_EXPERTS, wait, 0)

    rows_used = tile_ref[i, 0]
    h = h_ref[...]
    run_lo = bounds_ref[0, 0:1, :]
    run_hi = bounds_ref[0, 1:2, :]

    def sort_chunk(c, carry):
        row0 = (c * SLOT_CHUNK).astype(F32)
        row_e = lax.broadcasted_iota(I32, (SLOT_CHUNK, LANES), 0).astype(F32) + row0
        in_run = jnp.where(row_e >= run_lo, jnp.where(row_e < run_hi, 1.0, 0.0), 0.0).astype(BF16)
        slot_of_token = (256.0 * _dot(in_run, slot_hi_ref[0]) + _dot(in_run, slot_lo_ref[0]))
        row_t = lax.broadcasted_iota(I32, (SLOT_CHUNK, tl), 0).astype(F32) + row0
        onehot = jnp.where(slot_of_token == row_t, 1.0, 0.0).astype(BF16)
        rows = pl.ds(pl.multiple_of(c * SLOT_CHUNK, SLOT_CHUNK), SLOT_CHUNK)
        loc_ref[buf, rows, :] = _dot(onehot, h).astype(BF16)
        return carry

    lax.fori_loop(0, (rows_used + SLOT_CHUNK - 1) // SLOT_CHUNK, sort_chunk, 0)

    def run_copy(b, loc0, slot0, nrows=RUN_ALIGN):
        return pltpu.make_async_copy(loc_ref.at[b, pl.ds(loc0, nrows), :],
                                     xs_ref.at[pl.ds(slot0, nrows), :], sem.at[b])

    _start_run_copies(pstart_ref, copy_ref, tile_ref[i, 1], tile_ref[i, 2],
                      lambda loc0, slot0, n: run_copy(buf, loc0, slot0, n))

    @pl.when(i > 0)
    def _():
        _wait_run_rows(lambda n: run_copy(1 - buf, 0, 0, n), tile_ref[jnp.maximum(i - 1, 0), 0])

    @pl.when(i == last)
    def _():
        _wait_run_rows(lambda n: run_copy(buf, 0, 0, n), rows_used)

        def wait_tail(j, c):
            tail_copy(j).wait()
            return c

        lax.fori_loop(nu_ref[0], n_blocks, wait_tail, 0)


def _dispatch(pend, pcnt, n_used, pstart, tile_counts, copy_table, slot_hi, slot_lo, bounds_row,
              h2b, n_slots):
    t, d = h2b.shape
    tl = MOE_TILE
    body = functools.partial(_dispatch_body, tl=tl, bm=BM_FFN, n_blocks=n_slots // BM_FFN)
    grid_spec = pltpu.PrefetchScalarGridSpec(
        num_scalar_prefetch=5,
        grid=(t // tl,),
        in_specs=[BS((1, COPY_FIELDS, 2 * LANES), lambda i, *_: (i, 0, 0), memory_space=pltpu.SMEM),
                  BS((1, LANES, tl), lambda i, *_: (i, 0, 0)),
                  BS((1, LANES, tl), lambda i, *_: (i, 0, 0)),
                  BS((1, SUBLANES, LANES), lambda i, *_: (i, 0, 0)),
                  BS((tl, d), lambda i, *_: (i, 0))],
        out_specs=BS(memory_space=pl.ANY),
        scratch_shapes=[pltpu.VMEM((2, LOCAL_SLOTS, d), BF16), pltpu.VMEM((BM_FFN, d), BF16),
                        pltpu.SemaphoreType.DMA((2,)), pltpu.SemaphoreType.DMA(()),
                        pltpu.SemaphoreType.DMA(())],
    )
    return pl.pallas_call(
        body,
        out_shape=SDS((n_slots, d), BF16),
        grid_spec=grid_spec,
        compiler_params=_cparams("arbitrary"),
        name="moe_dispatch",
    )(pend, pcnt, n_used, pstart, tile_counts, copy_table, slot_hi, slot_lo, bounds_row, h2b)


def _ffn_body(first_ref, count_ref, nu_ref, xs_ref, wg_ref, wu_ref, wd_ref, ys_ref, wgb_ref, wub_ref,
              wdb_ref, xbuf_ref, ybuf_ref, zbuf_ref, isem, osem, tsem, *, bm, n_blocks):
    e = pl.program_id(0)
    nu = nu_ref[0]

    def block_rows(blk):
        return pl.ds(pl.multiple_of(blk * bm, bm), bm)

    def in_copy(blk, slot):
        return pltpu.make_async_copy(xs_ref.at[block_rows(blk), :], xbuf_ref.at[slot], isem.at[slot])

    def out_copy(blk, slot):
        return pltpu.make_async_copy(ybuf_ref.at[slot], ys_ref.at[block_rows(blk), :], osem.at[slot])

    def tail_copy(blk):
        return pltpu.make_async_copy(zbuf_ref, ys_ref.at[block_rows(blk), :], tsem)

    @pl.when(e == 0)
    def _():
        for s in range(FFN_IN_BUFS):
            @pl.when(s < nu)
            def _():
                in_copy(s, s).start()

        zbuf_ref[...] = jnp.zeros_like(zbuf_ref)

        def start_tail(j, c):
            tail_copy(j).start()
            return c

        lax.fori_loop(nu, n_blocks, start_tail, 0)

    @pl.when(count_ref[e] > 0)
    def _():
        wgb_ref[...] = wg_ref[...].astype(BF16)
        wub_ref[...] = wu_ref[...].astype(BF16)
        wdb_ref[...] = wd_ref[...].astype(BF16)

    def one_block(b, carry):
        i = first_ref[e] + b
        slot = i % FFN_IN_BUFS
        oslot = i % 2
        in_copy(i, slot).wait()
        x = xbuf_ref[slot]
        a = _silu(_dot(x, wgb_ref[...])) * _dot(x, wub_ref[...])
        y = _dot(a.astype(BF16), wdb_ref[...]).astype(BF16)

        @pl.when(i >= 2)
        def _():
            out_copy(i - 2, oslot).wait()

        ybuf_ref[oslot] = y
        out_copy(i, oslot).start()

        @pl.when(i + FFN_IN_BUFS < nu)
        def _():
            in_copy(i + FFN_IN_BUFS, slot).start()

        return carry

    lax.fori_loop(0, count_ref[e], one_block, 0)

    @pl.when(e == pl.num_programs(0) - 1)
    def _():
        @pl.when(nu >= 2)
        def _():
            out_copy(nu - 2, (nu - 2) % 2).wait()

        out_copy(nu - 1, (nu - 1) % 2).wait()

        def wait_tail(j, c):
            tail_copy(j).wait()
            return c

        lax.fori_loop(nu, n_blocks, wait_tail, 0)


def _ffn(first_block, block_count, n_used, xs, w_gate, w_up, w_down, layer):
    n_slots, d = xs.shape
    bm = BM_FFN
    de = w_gate.shape[-1]
    n_blocks = n_slots // bm
    body = functools.partial(_ffn_body, bm=bm, n_blocks=n_blocks)
    grid_spec = pltpu.PrefetchScalarGridSpec(
        num_scalar_prefetch=3,
        grid=(N_EXPERTS,),
        in_specs=[BS(memory_space=pl.ANY),
                  BS((None, None, d, de), lambda e, *_: (layer, e, 0, 0)),
                  BS((None, None, d, de), lambda e, *_: (layer, e, 0, 0)),
                  BS((None, None, de, d), lambda e, *_: (layer, e, 0, 0))],
        out_specs=BS(memory_space=pl.ANY),
        scratch_shapes=[pltpu.VMEM((d, de), BF16), pltpu.VMEM((d, de), BF16),
                        pltpu.VMEM((de, d), BF16),
                        pltpu.VMEM((FFN_IN_BUFS, bm, d), BF16), pltpu.VMEM((2, bm, d), BF16),
                        pltpu.VMEM((bm, d), BF16),
                        pltpu.SemaphoreType.DMA((FFN_IN_BUFS,)), pltpu.SemaphoreType.DMA((2,)),
                        pltpu.SemaphoreType.DMA(())],
    )
    return pl.pallas_call(
        body,
        out_shape=SDS((n_slots, d), BF16),
        grid_spec=grid_spec,
        compiler_params=_cparams("arbitrary"),
        name="moe_experts",
    )(first_block, block_count, n_used, xs, w_gate, w_up, w_down)


def _combine_body(pstart_ref, tile_ref, copy_ref, next_copy_ref, ys_ref,
                  slot_hi_ref, slot_lo_ref, gate_t_ref, lo_col_ref, hi_col_ref,
                  x_ref, h2b_ref, mod_ref, wsg_ref, wsu_ref, wsd_ref, fg_ref,
                  o_ref, loc_ref, acc_ref, sem, *, tl, final):
    i = pl.program_id(0)
    last = pl.num_programs(0) - 1
    buf = i % 2
    rows_used = tile_ref[i, 0]
    nxt = jnp.minimum(i + 1, last)

    def run_copy(b, loc0, slot0, nrows=RUN_ALIGN):
        return pltpu.make_async_copy(ys_ref.at[pl.ds(slot0, nrows), :],
                                     loc_ref.at[b, pl.ds(loc0, nrows), :], sem.at[b])

    @pl.when(i == 0)
    def _():
        loc_ref[...] = jnp.zeros_like(loc_ref)
        _start_run_copies(pstart_ref, copy_ref, tile_ref[i, 1], tile_ref[i, 2],
                          lambda loc0, slot0, n: run_copy(buf, loc0, slot0, n))

    @pl.when(i < last)
    def _():
        _start_run_copies(pstart_ref, next_copy_ref, tile_ref[nxt, 1], tile_ref[nxt, 2],
                          lambda loc0, slot0, n: run_copy(1 - buf, loc0, slot0, n))

    hb = h2b_ref[...]
    a = _silu(_dot(hb, wsg_ref[...])) * _dot(hb, wsu_ref[...])
    acc_ref[...] = _dot(a.astype(BF16), wsd_ref[...])
    run_lo = lo_col_ref[0][:, 0:1]
    run_hi = hi_col_ref[0][:, 0:1]

    _wait_run_rows(lambda n: run_copy(buf, 0, 0, n), rows_used)

    def unsort_chunk(c, carry):
        col0 = (c * SLOT_CHUNK).astype(F32)
        col_e = lax.broadcasted_iota(I32, (LANES, SLOT_CHUNK), 1).astype(F32) + col0
        in_run = jnp.where(col_e >= run_lo, jnp.where(col_e < run_hi, 1.0, 0.0), 0.0).astype(BF16)
        slot_of_token = (256.0 * _dot(slot_hi_ref[0], in_run) + _dot(slot_lo_ref[0], in_run))
        gate_of_token = _dot(gate_t_ref[0], in_run)
        col_t = lax.broadcasted_iota(I32, (tl, SLOT_CHUNK), 1).astype(F32) + col0
        gate = jnp.where(slot_of_token == col_t, gate_of_token, 0.0).astype(BF16)
        y = loc_ref[buf, pl.ds(pl.multiple_of(c * SLOT_CHUNK, SLOT_CHUNK), SLOT_CHUNK), :]
        acc_ref[...] = acc_ref[...] + _dot(gate, y)
        return carry

    lax.fori_loop(0, (rows_used + SLOT_CHUNK - 1) // SLOT_CHUNK, unsort_chunk, 0)

    xo = x_ref[...] + mod_ref[0, 5:6, :] * acc_ref[...]
    if final:
        ms = jnp.mean(xo * xo, axis=-1, keepdims=True)
        xo = (xo * lax.rsqrt(ms + EPS)) * fg_ref[...]
    o_ref[...] = xo


def _combine(pstart, tile_counts, copy_table, ys, slot_hi_t, slot_lo_t, gate_t, lo_col, hi_col,
             x_flat, h2b, mod_lat, wsg, wsu, wsd, final_g, tokens_per_batch, final):
    t, d = x_flat.shape
    tl = MOE_TILE
    ds = wsg.shape[1]
    tiles_per_batch = tokens_per_batch // tl
    body = functools.partial(_combine_body, tl=tl, final=final)
    n_tiles = t // tl
    table_spec = lambda index: BS((1, COPY_FIELDS, 2 * LANES), index, memory_space=pltpu.SMEM)
    this_tile = lambda i, *_: (i, 0, 0)
    next_tile = lambda i, *_: (jnp.minimum(i + 1, n_tiles - 1), 0, 0)
    grid_spec = pltpu.PrefetchScalarGridSpec(
        num_scalar_prefetch=2,
        grid=(n_tiles,),
        in_specs=[table_spec(this_tile), table_spec(next_tile),
                  BS(memory_space=pl.ANY),
                  BS((1, tl, LANES), this_tile), BS((1, tl, LANES), this_tile),
                  BS((1, tl, LANES), this_tile),
                  BS((1, LANES, LANES), this_tile), BS((1, LANES, LANES), this_tile),
                  BS((tl, d), lambda i, *_: (i, 0)),
                  BS((tl, d), lambda i, *_: (i, 0)),
                  BS((1, SUBLANES, d), lambda i, *_: (i // tiles_per_batch, 0, 0)),
                  BS((d, ds), lambda i, *_: (0, 0)),
                  BS((d, ds), lambda i, *_: (0, 0)),
                  BS((ds, d), lambda i, *_: (0, 0)),
                  BS((1, d), lambda i, *_: (0, 0))],
        out_specs=BS((tl, d), lambda i, *_: (i, 0)),
        scratch_shapes=[pltpu.VMEM((2, LOCAL_SLOTS, d), BF16), pltpu.VMEM((tl, d), F32),
                        pltpu.SemaphoreType.DMA((2,))],
    )
    return pl.pallas_call(
        body,
        out_shape=SDS((t, d), F32),
        grid_spec=grid_spec,
        compiler_params=_cparams("arbitrary"),
        name="moe_combine",
    )(pstart, tile_counts, copy_table, copy_table, ys, slot_hi_t, slot_lo_t, gate_t,
      lo_col, hi_col, x_flat, h2b, mod_lat, wsg, wsu, wsd, final_g)


def _moe(x_flat, mod_lat, g, w_router, e_bias, w_gate, w_up, w_down, ws_gate, ws_up, ws_down,
         final_g, tri, tokens_per_batch, layer, final):
    t, d = x_flat.shape
    bm = BM_FFN
    n_tiles = t // MOE_TILE
    (h2b, slot_hi, slot_lo, slot_hi_t, slot_lo_t, gate_t, lo_col, hi_col, bounds_row, copy_table,
     tile_counts, seg) = _router(x_flat, mod_lat, g, w_router.T, e_bias.reshape(N_EXPERTS, 1), tri,
                                 tokens_per_batch)
    tile_counts = tile_counts[:, 0, :SUBLANES]
    seg_rows = seg[:, 0].astype(I32)
    pcnt = (seg_rows + bm - 1) // bm * bm
    pend = jnp.cumsum(pcnt).astype(I32)
    pstart = pend - pcnt
    max_rows = t * TOP_K + n_tiles * N_EXPERTS * (RUN_ALIGN - 1)
    n_blocks = -(-max_rows // bm) + N_EXPERTS
    n_slots = n_blocks * bm
    n_used = pend[-1:] // bm
    xs = _dispatch(pend, pcnt, n_used, pstart, tile_counts, copy_table, slot_hi, slot_lo,
                   bounds_row, h2b, n_slots)
    ys = _ffn(pstart // bm, pcnt // bm, n_used, xs, w_gate, w_up, w_down, layer)
    return _combine(pstart, tile_counts, copy_table, ys, slot_hi_t, slot_lo_t, gate_t, lo_col,
                    hi_col, x_flat, h2b, mod_lat,
                    ws_gate.astype(BF16), ws_up.astype(BF16), ws_down.astype(BF16),
                    final_g.reshape(1, d), tokens_per_batch, final)


def _inproj_c_body(xp_ref, x_ref, xn_ref, mod_ref, g_ref, w_ref, cw_ref, cb_ref,
                   v_ref, g1_ref, g2_ref, *, tm, n_tiles):
    i = pl.program_id(1)
    halo = SUBLANES
    xe = jnp.concatenate([xp_ref[0], x_ref[0], xn_ref[0]], axis=0)
    h = _norm_mod(xe, g_ref[...], mod_ref[0, 1:2, :], mod_ref[0, 0:1, :])
    row = lax.broadcasted_iota(I32, (tm + 2 * halo, 1), 0)
    outside = jnp.logical_or(jnp.logical_and(i == 0, row < halo),
                             jnp.logical_and(i == n_tiles - 1, row >= tm + halo))
    hb = jnp.where(outside, 0.0, h).astype(BF16)
    width = v_ref.shape[-1]
    for part, o_ref in enumerate((v_ref, g1_ref, g2_ref)):
        cols = slice(part * width, (part + 1) * width)
        zp = _dot(hb, w_ref[:, cols])
        up = pltpu.roll(zp, 1, 0)
        dn = pltpu.roll(zp, tm + 2 * halo - 1, 0)
        z = cw_ref[0:1, cols] * up + cw_ref[1:2, cols] * zp + cw_ref[2:3, cols] * dn + cb_ref[:, cols]
        o_ref[0] = z[halo:halo + tm]


def _inproj_c(x, mod_lat, g, w_in, conv_w, conv_b):
    b, l, d = x.shape
    tm = TM_PROJ
    n_tiles = l // tm
    w3 = w_in.shape[1]
    width = w3 // 3
    r8 = tm // SUBLANES
    body = functools.partial(_inproj_c_body, tm=tm, n_tiles=n_tiles)
    out = SDS((b, l, width), F32)
    return pl.pallas_call(
        body,
        out_shape=(out, out, out),
        grid=(b, n_tiles),
        in_specs=[BS((1, SUBLANES, d), lambda bi, i: (bi, jnp.maximum(i * r8 - 1, 0), 0)),
                  BS((1, tm, d), lambda bi, i: (bi, i, 0)),
                  BS((1, SUBLANES, d), lambda bi, i: (bi, jnp.minimum((i + 1) * r8, l // SUBLANES - 1), 0)),
                  BS((1, SUBLANES, d), lambda bi, i: (bi, 0, 0)),
                  BS((1, d), lambda bi, i: (0, 0)),
                  BS((d, w3), lambda bi, i: (0, 0)),
                  BS((3, w3), lambda bi, i: (0, 0)),
                  BS((1, w3), lambda bi, i: (0, 0))],
        out_specs=(BS((1, tm, width), lambda bi, i: (bi, i, 0)),
                   BS((1, tm, width), lambda bi, i: (bi, i, 0)),
                   BS((1, tm, width), lambda bi, i: (bi, i, 0))),
        compiler_params=_cparams("arbitrary", "arbitrary"),
        name="inproj_c",
    )(x, x, x, mod_lat, g, w_in, conv_w, conv_b)


def _filter_body(f_ref, w1_ref, b1_ref, w2_ref, b2_ref, w3hi_ref, w3lo_ref, fr_ref, dl_ref,
                 k_ref, l1_ref, *, tp, n):
    i = pl.program_id(0)
    feats = f_ref[...]
    fr = fr_ref[...]
    a = jnp.sin(fr * (_dot_hp(feats, w1_ref[...]) + b1_ref[...]))
    a = jnp.sin(fr * (_dot_hp(a, w2_ref[...]) + b2_ref[...]))
    a_hi = a.astype(BF16)
    a_lo = (a - a_hi.astype(F32)).astype(BF16)
    w3_hi = w3hi_ref[...]
    hf = _dot(a_hi, w3_hi) + _dot(a_hi, w3lo_ref[...]) + _dot(a_lo, w3_hi)
    t01 = feats[:, 0:1]
    hf = hf * (jnp.exp(-t01 * jnp.abs(dl_ref[...])) + DECAY_SHIFT)
    row = lax.broadcasted_iota(I32, hf.shape, 0) + i * tp
    hf = jnp.where(row == n, 0.0, hf)
    k_ref[...] = hf

    @pl.when(i == 0)
    def _():
        l1_ref[...] = jnp.zeros_like(l1_ref)

    l1_ref[...] = l1_ref[...] + jnp.sum(jnp.abs(hf), axis=0, keepdims=True)


def _filters(feats, w1, b1, w2, b2, w3, freq, delta, width):
    n2, fe = feats.shape
    n = n2 // 2
    hid = w2.shape[0]
    tp = 256
    half_tiles = n // tp
    body = functools.partial(_filter_body, tp=tp, n=n)
    full = lambda shape: BS(shape, lambda i: (0, 0))
    by_direction = lambda a: a.reshape(a.shape[0], HYENA_ORDER, 2, width).transpose(2, 0, 1, 3).reshape(
        2, a.shape[0], HYENA_ORDER * width)
    w3_d = by_direction(w3)
    w3_hi = w3_d.astype(BF16)
    w3_lo = (w3_d - w3_hi.astype(F32)).astype(BF16)
    delta_d = by_direction(delta)
    fo = HYENA_ORDER * width
    direction = lambda i: (i // half_tiles, 0, 0)
    return pl.pallas_call(
        body,
        out_shape=(SDS((n2, fo), F32), SDS((SUBLANES, fo), F32)),
        grid=(n2 // tp,),
        in_specs=[BS((tp, fe), lambda i: (i, 0)), full((fe, hid)), full((1, hid)),
                  full((hid, hid)), full((1, hid)),
                  BS((None, hid, fo), direction), BS((None, hid, fo), direction), full((1, hid)),
                  BS((None, 1, fo), direction)],
        out_specs=(BS((tp, fo), lambda i: (i, 0)), BS((SUBLANES, fo), lambda i: (0, 0))),
        compiler_params=_cparams("arbitrary"),
        name="hyena_filters",
    )(feats, w1, b1, w2, b2, w3_hi, w3_lo, freq, delta_d)


DFT_R = 128
DFT_VP = 72
DFT_BGROUP = 16
DFT_LANE_TILES = 4


def _dft_tables(n):
    r = DFT_R
    vp = DFT_VP
    m = 2 * n
    na = n // r
    two_pi = 2.0 * np.pi
    live = (np.arange(vp) <= r // 2).astype(np.float64)
    a = np.arange(2 * na)[None, :]
    v = np.arange(vp)[:, None]
    ang1 = two_pi * ((a * v) % r) / r
    f1_full = np.concatenate([np.cos(ang1), -np.sin(ang1)], axis=0) * np.tile(live, 2)[:, None]
    f1 = f1_full[:, :na]
    b = np.arange(r)[None, None, :]
    u = np.arange(r)[None, :, None]
    vv = np.arange(vp)[:, None, None]
    ang2 = two_pi * ((b * (r * u + vv)) % m) / m
    gr, gi = np.cos(ang2), -np.sin(ang2)
    fwd = np.concatenate([np.concatenate([gr, -gi], axis=2),
                          np.concatenate([gi, gr], axis=2)], axis=1)
    hr, hi = np.transpose(gr, (0, 2, 1)), -np.transpose(gi, (0, 2, 1))
    inv = np.concatenate([np.concatenate([hr, -hi], axis=2),
                          np.concatenate([hi, hr], axis=2)], axis=1)
    weight = live * np.where((np.arange(vp) == 0) | (np.arange(vp) == r // 2), 1.0, 2.0)
    ang3 = two_pi * ((np.arange(na)[:, None] * np.arange(vp)[None, :]) % r) / r
    f3 = np.concatenate([np.cos(ang3) * weight, -np.sin(ang3) * weight], axis=1) / m
    cast = lambda t: jnp.asarray(t.astype(np.float32)).astype(BF16)
    return cast(f1), cast(f1_full), cast(fwd), cast(inv), cast(f3)


def _lane_tile_specs(rows, index_map_of_tile):
    return [BS((None, rows, DFT_BGROUP, LANES), index_map_of_tile(t)) for t in range(DFT_LANE_TILES)]


def _rows_of_position(ref, j):
    x, bg, _ = ref.shape
    return ref.reshape(x * bg, LANES)[pl.ds(j, x, stride=bg), :]


def _dft_s1_body(*refs):
    q = DFT_LANE_TILES
    y_refs, f1_ref, ar_ref, ai_ref = refs[:q], refs[q], refs[q + 1], refs[q + 2]
    f1 = f1_ref[...]
    bg = DFT_BGROUP
    pair = MXU_COLS // LANES
    planes2d = lambda ref, t: ref.at[t].reshape(DFT_VP * bg, LANES)
    for t0 in range(0, q, pair):
        tiles = range(t0, t0 + pair)
        for j in range(bg):
            y = jnp.concatenate([_rows_of_position(y_refs[t], j) for t in tiles], axis=1)
            res = _dot(f1, y.astype(BF16))
            for s, t in enumerate(tiles):
                lanes = slice(s * LANES, (s + 1) * LANES)
                planes2d(ar_ref, t)[pl.ds(j, DFT_VP, stride=bg), :] = res[:DFT_VP, lanes]
                planes2d(ai_ref, t)[pl.ds(j, DFT_VP, stride=bg), :] = res[DFT_VP:, lanes]


def _dft_s1(y4, f1):
    nb, na, r, c = y4.shape
    q, bg = DFT_LANE_TILES, DFT_BGROUP
    out = SDS((nb, c // LANES, DFT_VP, r, LANES), F32)
    plane_spec = BS((None, q, DFT_VP, bg, LANES), lambda i, j, cc: (i, cc, 0, j, 0))
    tile_map = lambda t: (lambda i, j, cc: (i, 0, j, cc * q + t))
    return pl.pallas_call(
        _dft_s1_body,
        out_shape=(out, out),
        grid=(nb, r // bg, c // (q * LANES)),
        in_specs=_lane_tile_specs(na, tile_map) + [BS((2 * DFT_VP, na), lambda i, j, cc: (0, 0))],
        out_specs=(plane_spec, plane_spec),
        compiler_params=_cparams("arbitrary", "arbitrary", "arbitrary"),
        name="dft_stage1",
    )(*([y4] * q), f1)


def _plane_rows(re_ref, im_ref):
    wide = lambda ref: jnp.concatenate([ref[t] for t in range(ref.shape[0])], axis=1)
    return jnp.concatenate([wide(re_ref), wide(im_ref)], axis=0).astype(BF16)


def _filter_spec_body(ar_ref, ai_ref, g_ref, l1_ref, kr_ref, ki_ref):
    y = _dot(g_ref[...], _plane_rows(ar_ref, ai_ref))
    inv = 1.0 / l1_ref[0:1, :]
    kr_ref[...] = (y[:DFT_R] * inv).astype(BF16)
    ki_ref[...] = (y[DFT_R:] * inv).astype(BF16)


def _filter_spectrum(ar, ai, fwd, l1, width):
    r = DFT_R
    a_spec = BS((None, width // LANES, None, r, LANES), lambda v, o: (0, o, v, 0, 0))
    k_spec = BS((None, r, width), lambda v, o: (v, 0, o))
    out = SDS((DFT_VP, r, HYENA_ORDER * width), BF16)
    return pl.pallas_call(
        _filter_spec_body,
        out_shape=(out, out),
        grid=(DFT_VP, HYENA_ORDER),
        in_specs=[a_spec, a_spec, BS((None, 2 * r, 2 * r), lambda v, o: (v, 0, 0)),
                  BS((SUBLANES, width), lambda v, o: (0, o))],
        out_specs=(k_spec, k_spec),
        compiler_params=_cparams("arbitrary", "arbitrary"),
        name="hyena_filter_spectrum",
    )(ar, ai, fwd, l1)


def _conv_mid_body(ar_ref, ai_ref, g_ref, h_ref, kr_ref, ki_ref, qr_ref, qi_ref):
    y = _dot(g_ref[...], _plane_rows(ar_ref, ai_ref))
    yr, yi = y[:DFT_R], y[DFT_R:]
    kr, ki = kr_ref[...].astype(F32), ki_ref[...].astype(F32)
    p = jnp.concatenate([yr * kr - yi * ki, yr * ki + yi * kr], axis=0).astype(BF16)
    q = _dot(h_ref[...], p)
    for t in range(qr_ref.shape[0]):
        lanes = slice(t * LANES, (t + 1) * LANES)
        qr_ref[t] = q[:DFT_R, lanes]
        qi_ref[t] = q[DFT_R:, lanes]


def _conv_mid(ar, ai, fwd, inv, kr, ki, order):
    nb, tiles, vp, r, _ = ar.shape
    c = tiles * LANES
    a_spec = BS((None, tiles, None, r, LANES), lambda n, v: (n, 0, v, 0, 0))
    m_spec = BS((None, 2 * r, 2 * r), lambda n, v: (v, 0, 0))
    k_spec = BS((None, r, c), lambda n, v: (v, 0, order))
    out = SDS((nb, tiles, vp, r, LANES), F32)
    return pl.pallas_call(
        _conv_mid_body,
        out_shape=(out, out),
        grid=(nb, vp),
        in_specs=[a_spec, a_spec, m_spec, m_spec, k_spec, k_spec],
        out_specs=(a_spec, a_spec),
        compiler_params=_cparams("arbitrary", "arbitrary"),
        name="hyena_spectral_product",
    )(ar, ai, fwd, inv, kr, ki)


def _idft_gate_body(*refs):
    q = DFT_LANE_TILES
    qr_ref, qi_ref = refs[0], refs[1]
    y_refs, gate_refs = refs[2:2 + q], refs[2 + q:2 + 2 * q]
    f3_ref, fb_ref, o_ref = refs[2 + 2 * q:]
    f3 = f3_ref[...]
    for t in range(q):
        lanes = slice(t * LANES, (t + 1) * LANES)
        fb = fb_ref[:, lanes]
        for j in range(DFT_BGROUP):
            planes = jnp.concatenate([_rows_of_position(qr_ref.at[t], j),
                                      _rows_of_position(qi_ref.at[t], j)], axis=0).astype(BF16)
            conv = _dot(f3, planes)
            o_ref[:, j, lanes] = _rows_of_position(gate_refs[t], j) * (
                conv + fb * _rows_of_position(y_refs[t], j))


def _idft_gate(qr, qi, f3, y4, gate4, fbias):
    nb, na, r, c = y4.shape
    q, bg = DFT_LANE_TILES, DFT_BGROUP
    tile_map = lambda t: (lambda i, j, cc: (i, 0, j, cc * q + t))
    wide = lambda rows: BS((None, rows, bg, q * LANES), lambda i, j, cc: (i, 0, j, cc))
    plane_spec = BS((None, q, DFT_VP, bg, LANES), lambda i, j, cc: (i, cc, 0, j, 0))
    return pl.pallas_call(
        _idft_gate_body,
        out_shape=SDS((nb, na, r, c), F32),
        grid=(nb, r // bg, c // (q * LANES)),
        in_specs=([plane_spec, plane_spec]
                  + _lane_tile_specs(na, tile_map) + _lane_tile_specs(na, tile_map)
                  + [BS((na, 2 * DFT_VP), lambda i, j, cc: (0, 0)),
                     BS((1, q * LANES), lambda i, j, cc: (0, cc))]),
        out_specs=wide(na),
        compiler_params=_cparams("arbitrary", "arbitrary", "arbitrary"),
        name="hyena_idft_gate",
    )(qr, qi, *([y4] * q + [gate4] * q), f3, fbias)


def _outproj_body(y_ref, w_ref, x_ref, mod_ref, o_ref):
    o_ref[0] = x_ref[0] + mod_ref[0, 2:3, :] * _dot(y_ref[0].astype(BF16), w_ref[...])


def _outproj(y, w_out, x, mod_lat):
    b, l, d = x.shape
    tm = TM_PROJ
    wdt = y.shape[-1]
    return pl.pallas_call(
        _outproj_body,
        out_shape=SDS((b, l, d), F32),
        grid=(b, l // tm),
        in_specs=[BS((1, tm, wdt), lambda bi, i: (bi, i, 0)),
                  BS((wdt, d), lambda bi, i: (0, 0)),
                  BS((1, tm, d), lambda bi, i: (bi, i, 0)),
                  BS((1, SUBLANES, d), lambda bi, i: (bi, 0, 0))],
        out_specs=BS((1, tm, d), lambda bi, i: (bi, i, 0)),
        compiler_params=_cparams("arbitrary", "arbitrary"),
        name="outproj_c",
    )(y, w_out, x, mod_lat)


def _hyena(x, mod_lat, g, w_in, conv_w, conv_b, w1, b1, w2, b2, w3, freq, delta, f_bias, w_out):
    b, n, d = x.shape
    width = w_out.shape[0]
    r = DFT_R
    na = n // r
    f1, f1_full, fwd, inv, f3 = _dft_tables(n)
    v, gate1, gate2 = _inproj_c(x, mod_lat, g, w_in.astype(BF16), conv_w, conv_b.reshape(1, -1))

    pos = jnp.arange(2 * n, dtype=F32)
    t = jnp.where(pos < n, pos, 2 * n - pos)
    t01 = t / max(n - 1, 1)
    bands = jnp.linspace(1e-4, FILT_BANDS - 1, FILT_BANDS, dtype=F32)
    ang = (2.0 * math.pi / n) * t[:, None] * bands[None, :]
    feats = jnp.concatenate([t01[:, None], jnp.cos(ang), jnp.sin(ang)], axis=-1)
    fe = feats.shape[1]
    feats = jnp.pad(feats, ((0, 0), (0, LANES - fe)))
    w1p = jnp.pad(w1, ((0, LANES - fe), (0, 0)))
    k_circ, l1 = _filters(feats, w1p, b1.reshape(1, -1), w2, b2.reshape(1, -1), w3,
                          freq.reshape(1, -1), delta.reshape(1, -1), width)
    far, fai = _dft_s1(k_circ.reshape(1, 2 * na, r, k_circ.shape[1]), f1_full)
    kr, ki = _filter_spectrum(far, fai, fwd, l1, width)

    y4 = v.reshape(b, na, r, width)
    for o, gate in enumerate((gate1, gate2)):
        ar, ai = _dft_s1(y4, f1)
        qr, qi = _conv_mid(ar, ai, fwd, inv, kr, ki, o)
        y4 = _idft_gate(qr, qi, f3, y4, gate.reshape(b, na, r, width), f_bias[o].reshape(1, width))
    return _outproj(y4.reshape(b, n, width), w_out.astype(BF16), x, mod_lat)


def _rope_tables(seq_len):
    rows = seq_len // GRID_W
    row = jnp.repeat(jnp.arange(rows, dtype=F32), GRID_W)
    col = jnp.tile(jnp.arange(GRID_W, dtype=F32), rows)
    inv = jnp.power(ROPE_BASE, -jnp.arange(ROPE_FREQS, dtype=F32) / ROPE_FREQS)
    ar, ac = row[:, None] * inv, col[:, None] * inv
    cos_h = jnp.concatenate([jnp.cos(ar), jnp.cos(ar), jnp.cos(ac), jnp.cos(ac)], axis=1)
    sin_h = jnp.concatenate([-jnp.sin(ar), jnp.sin(ar), -jnp.sin(ac), jnp.sin(ac)], axis=1)
    reps = LANES // HEAD_DIM
    return jnp.tile(cos_h, (1, reps)), jnp.tile(sin_h, (1, reps))


def _rotate_partner_columns(w):
    ncol = w.shape[1]
    lane = np.arange(ncol)
    partner = np.where((lane % (2 * ROPE_FREQS)) < ROPE_FREQS, lane + ROPE_FREQS, lane - ROPE_FREQS)
    return w[:, partner]


def kernel(x, c, ctx, c_ctx, w_mod, b_mod, norm_g, w_in_ab, sink, w_spatial, b_spatial, w_out_ab,
           w_in_c, conv_w, conv_b, filt_w1, filt_b1, filt_w2, filt_b2, filt_w3, filt_freq,
           filt_delta, filt_bias, w_out_c, w_router, e_bias, w_gate, w_up, w_down, ws_gate,
           ws_up, ws_down, final_g):
    b, l, d = x.shape
    depth = w_mod.shape[0]
    assert depth == 2 and b + 1 <= SUBLANES

    cc = jnp.zeros((SUBLANES, d), F32).at[:b].set(c).at[b].set(c_ctx)
    m_all = _mod_vectors(cc, w_mod, b_mod)

    def mod_rows(layer, row0, nrow):
        m = m_all[layer, row0:row0 + nrow].reshape(nrow, 6, d)
        return jnp.pad(m, ((0, 0), (0, SUBLANES - 6), (0, 0)))

    tri = jnp.triu(jnp.ones((MOE_TILE, MOE_TILE), F32), k=1).astype(BF16)

    mod_lat = mod_rows(0, 0, b)
    mod_ctx = mod_rows(0, b, 1)[0]
    w_in = w_in_ab[0]
    qk = ATTN_WIDTH + KV_WIDTH
    w_cat = jnp.concatenate([w_in, _rotate_partner_columns(w_in[:, :qk])], axis=1).astype(BF16)
    cos_t, sin_t = _rope_tables(l)
    group_avg = jnp.kron(jnp.eye(N_SG_GROUPS, dtype=F32),
                         jnp.full((SG_GROUP_DIM, SG_GROUP_DIM), 1.0 / SG_GROUP_DIM, F32)).astype(BF16)
    kc, vc = _ctx_kv(ctx, mod_ctx, norm_g[0, 0].reshape(1, d),
                     w_in[:, ATTN_WIDTH:ATTN_WIDTH + 2 * KV_WIDTH].astype(BF16))
    q, k, v, ug, vn = _inproj_ab(x, mod_lat, norm_g[0, 0].reshape(1, d), w_cat, cos_t, sin_t, group_avg)
    b_full = jnp.repeat(b_spatial[0].T, SG_GROUP_DIM, axis=1)
    x1 = _mixer(sink[0], q, k, v, kc, vc, ug, vn, w_spatial[0].astype(BF16), b_full,
                w_out_ab[0].astype(BF16), x, mod_lat)
    x2 = _moe(x1.reshape(b * l, d), mod_lat, norm_g[0, 1].reshape(1, d), w_router[0], e_bias[0],
              w_gate, w_up, w_down, ws_gate[0], ws_up[0], ws_down[0], final_g, tri, l,
              layer=0, final=False).reshape(b, l, d)

    mod_lat = mod_rows(1, 0, b)
    x3 = _hyena(x2, mod_lat, norm_g[1, 0].reshape(1, d), w_in_c[0], conv_w[0], conv_b[0],
                filt_w1[0], filt_b1[0], filt_w2[0], filt_b2[0], filt_w3[0], filt_freq[0],
                filt_delta[0], filt_bias[0], w_out_c[0])
    out = _moe(x3.reshape(b * l, d), mod_lat, norm_g[1, 1].reshape(1, d), w_router[1], e_bias[1],
               w_gate, w_up, w_down, ws_gate[1], ws_up[1], ws_down[1], final_g, tri, l,
               layer=1, final=True)
    return out.reshape(b, l, d)
```

```python
import functools
import math

import numpy as np
import jax
import jax.numpy as jnp
from jax import lax
from jax.experimental import pallas as pl
from jax.experimental.pallas import tpu as pltpu

F32 = jnp.float32
BF16 = jnp.bfloat16
I32 = jnp.int32
HIGHEST = lax.Precision.HIGHEST
SDS = jax.ShapeDtypeStruct
BS = pl.BlockSpec

EPS = 1e-6
NEG = -1e30

GRID_W = 64
N_Q_HEADS = 8
N_KV_HEADS = 2
HEAD_DIM = 64
ATTN_WIDTH = N_Q_HEADS * HEAD_DIM
KV_WIDTH = N_KV_HEADS * HEAD_DIM
WINDOW = 128
BLOCK = 128
ROPE_BASE = 10000.0
ROPE_FREQS = HEAD_DIM // 4
N_SG_GROUPS = 8
SG_GROUP_DIM = 64
SG_WIDTH = N_SG_GROUPS * SG_GROUP_DIM
HYENA_ORDER = 2
FILT_BANDS = 16
DECAY_SHIFT = 0.05
N_EXPERTS = 64
TOP_K = 8
N_GROUPS = 8
TOPK_GROUPS = 4
ROUTED_SCALE = 2.5

LANES = 128
SUBLANES = 8
MXU_COLS = 256
VMEM_LIMIT = 56 * 1024 * 1024

TM_PROJ = 512
TQ_MIX = 256
MOE_TILE = 256
BM_FFN = 512
RUN_ALIGN = 16
SLOT_CHUNK = 512
NO_SLOT = 256 * 256 - 1
LOCAL_SLOTS = -(-(TOP_K * MOE_TILE + N_EXPERTS * (RUN_ALIGN - 1)) // SLOT_CHUNK) * SLOT_CHUNK
COPY_FIELDS = 3
FFN_IN_BUFS = 4


def _cparams(*sem):
    return pltpu.CompilerParams(dimension_semantics=sem, vmem_limit_bytes=VMEM_LIMIT)


def _dot(a, b):
    return jnp.dot(a, b, preferred_element_type=F32)


def _dot_nt(a, b):
    return lax.dot_general(a, b, (((1,), (1,)), ((), ())), preferred_element_type=F32)


def _dot_hp(a, b):
    return jnp.dot(a, b, preferred_element_type=F32, precision=HIGHEST)


def _norm_mod(x, g, sc, sh):
    ms = jnp.mean(x * x, axis=-1, keepdims=True)
    y = x * lax.rsqrt(ms + EPS)
    return (y * g) * (1.0 + sc) + sh


def _gelu_tanh(x):
    c = math.sqrt(2.0 / math.pi)
    return 0.5 * x * (1.0 + jnp.tanh(c * (x + 0.044715 * (x * x * x))))


def _silu(x):
    return x * jax.nn.sigmoid(x)


def _mod_body(c_ref, w_ref, b_ref, o_ref):
    o_ref[0] = _dot_hp(_silu(c_ref[...]), w_ref[0]) + b_ref[0]


def _mod_vectors(cc, w_mod, b_mod):
    depth, d, n = w_mod.shape
    tn = 1536
    return pl.pallas_call(
        _mod_body,
        out_shape=SDS((depth, SUBLANES, n), F32),
        grid=(depth, n // tn),
        in_specs=[BS((SUBLANES, d), lambda l, j: (0, 0)),
                  BS((1, d, tn), lambda l, j: (l, 0, j)),
                  BS((1, 1, tn), lambda l, j: (l, 0, j))],
        out_specs=BS((1, SUBLANES, tn), lambda l, j: (l, 0, j)),
        compiler_params=_cparams("arbitrary", "arbitrary"),
        name="mod_vectors",
    )(cc, w_mod, b_mod.reshape(depth, 1, n))


def _ctx_kv_body(ctx_ref, mod_ref, g_ref, w_ref, kc_ref, vc_ref):
    h = _norm_mod(ctx_ref[0], g_ref[...], mod_ref[1:2, :], mod_ref[0:1, :])
    z = _dot(h.astype(BF16), w_ref[...])
    kc_ref[0] = z[:, :KV_WIDTH].astype(BF16)
    vc_ref[0] = z[:, KV_WIDTH:].astype(BF16)


def _ctx_kv(ctx, mod_ctx, g, w_kv):
    b, c, d = ctx.shape
    return pl.pallas_call(
        _ctx_kv_body,
        out_shape=(SDS((b, c, KV_WIDTH), BF16), SDS((b, c, KV_WIDTH), BF16)),
        grid=(b,),
        in_specs=[BS((1, c, d), lambda i: (i, 0, 0)),
                  BS((SUBLANES, d), lambda i: (0, 0)),
                  BS((1, d), lambda i: (0, 0)),
                  BS((d, 2 * KV_WIDTH), lambda i: (0, 0))],
        out_specs=(BS((1, c, KV_WIDTH), lambda i: (i, 0, 0)),
                   BS((1, c, KV_WIDTH), lambda i: (i, 0, 0))),
        compiler_params=_cparams("arbitrary"),
        name="ctx_kv",
    )(ctx, mod_ctx, g, w_kv)


def _inproj_ab_body(x_ref, mod_ref, g_ref, w_ref, cos_ref, sin_ref, avg_ref,
                    q_ref, k_ref, v_ref, ug_ref, vn_ref):
    h = _norm_mod(x_ref[0], g_ref[...], mod_ref[0, 1:2, :], mod_ref[0, 0:1, :]).astype(BF16)
    cs = cos_ref[...]
    sn = sin_ref[...]
    rot0 = ATTN_WIDTH + 2 * KV_WIDTH + 2 * SG_WIDTH
    scale = HEAD_DIM ** -0.5
    for j in range(ATTN_WIDTH // MXU_COLS):
        z = _dot(h, w_ref[:, j * MXU_COLS:(j + 1) * MXU_COLS])
        zr = _dot(h, w_ref[:, rot0 + j * MXU_COLS:rot0 + (j + 1) * MXU_COLS])
        for s in range(MXU_COLS // LANES):
            part = slice(s * LANES, (s + 1) * LANES)
            lanes = slice(j * MXU_COLS + s * LANES, j * MXU_COLS + (s + 1) * LANES)
            q_ref[0, :, lanes] = ((z[:, part] * cs + zr[:, part] * sn) * scale).astype(BF16)
    zkv = _dot(h, w_ref[:, ATTN_WIDTH:ATTN_WIDTH + 2 * KV_WIDTH])
    zkr = _dot(h, w_ref[:, rot0 + ATTN_WIDTH:rot0 + ATTN_WIDTH + KV_WIDTH])
    k_ref[0] = (zkv[:, :KV_WIDTH] * cs + zkr * sn).astype(BF16)
    v_ref[0] = zkv[:, KV_WIDTH:].astype(BF16)
    u0 = ATTN_WIDTH + 2 * KV_WIDTH
    ug_ref[0] = _gelu_tanh(_dot(h, w_ref[:, u0:u0 + SG_WIDTH]))
    vf = _gelu_tanh(_dot(h, w_ref[:, u0 + SG_WIDTH:u0 + 2 * SG_WIDTH]))
    avg = avg_ref[...]

    hi = vf.astype(BF16)
    mean = _dot(hi, avg) + _dot((vf - hi.astype(F32)).astype(BF16), avg)
    vc = vf - mean
    var = _dot((vc * vc).astype(BF16), avg)
    vn_ref[0] = (vc * lax.rsqrt(var + EPS)).astype(BF16)


def _inproj_ab(x, mod_lat, g, w_cat, cos_t, sin_t, avg):
    b, l, d = x.shape
    tm = TM_PROJ
    ncol = w_cat.shape[1]
    return pl.pallas_call(
        _inproj_ab_body,
        out_shape=(SDS((b, l, ATTN_WIDTH), BF16), SDS((b, l, KV_WIDTH), BF16),
                   SDS((b, l, KV_WIDTH), BF16), SDS((b, l, SG_WIDTH), F32),
                   SDS((b, l, SG_WIDTH), BF16)),
        grid=(b, l // tm),
        in_specs=[BS((1, tm, d), lambda bi, i: (bi, i, 0)),
                  BS((1, SUBLANES, d), lambda bi, i: (bi, 0, 0)),
                  BS((1, d), lambda bi, i: (0, 0)),
                  BS((d, ncol), lambda bi, i: (0, 0)),
                  BS((tm, LANES), lambda bi, i: (i, 0)),
                  BS((tm, LANES), lambda bi, i: (i, 0)),
                  BS((SG_WIDTH, SG_WIDTH), lambda bi, i: (0, 0))],
        out_specs=(BS((1, tm, ATTN_WIDTH), lambda bi, i: (bi, i, 0)),
                   BS((1, tm, KV_WIDTH), lambda bi, i: (bi, i, 0)),
                   BS((1, tm, KV_WIDTH), lambda bi, i: (bi, i, 0)),
                   BS((1, tm, SG_WIDTH), lambda bi, i: (bi, i, 0)),
                   BS((1, tm, SG_WIDTH), lambda bi, i: (bi, i, 0))),
        compiler_params=_cparams("arbitrary", "arbitrary"),
        name="inproj_ab",
    )(x, mod_lat, g, w_cat, cos_t, sin_t, avg)


def _mixer_body(sink_ref, q_ref, kp_ref, kcur_ref, kn_ref, vp_ref, vcur_ref, vn_ref,
                kc_ref, vc_ref, ug_ref, vnorm_ref, ws_ref, bs_ref, wout_ref, x_ref, mod_ref,
                o_ref, cat_ref, *, seq_len, sub_blocks):
    i = pl.program_id(1)
    kk = jnp.concatenate([kp_ref[0], kcur_ref[0], kn_ref[0]], axis=0)
    vv = jnp.concatenate([vp_ref[0], vcur_ref[0], vn_ref[0]], axis=0)
    kc = kc_ref[0]
    vc = vc_ref[0]
    span = 3 * BLOCK
    ii = lax.broadcasted_iota(I32, (BLOCK, span), 0)
    jj = lax.broadcasted_iota(I32, (BLOCK, span), 1)
    dd = jj - ii
    in_window = jnp.where(dd >= 0, jnp.where(dd <= 2 * WINDOW, 1, 0), 0)
    group = N_Q_HEADS // N_KV_HEADS
    for r in range(sub_blocks):
        rows = slice(r * BLOCK, (r + 1) * BLOCK)
        kpos = (i * sub_blocks + r - 1) * BLOCK + jj
        in_seq = jnp.where(kpos >= 0, jnp.where(kpos < seq_len, 1, 0), 0)
        bias = jnp.where(in_window * in_seq > 0, 0.0, NEG)
        qb = q_ref[0, rows, :]
        n_ctx = kc.shape[0]
        bias_all = jnp.concatenate([bias, jnp.zeros((BLOCK, n_ctx), F32)], axis=1)
        ones = jnp.ones((span + n_ctx, HEAD_DIM), BF16)
        keys, values = [], []
        for hk in range(N_KV_HEADS):
            ks = slice(hk * HEAD_DIM, (hk + 1) * HEAD_DIM)
            keys.append(jnp.concatenate([kk[r * BLOCK:r * BLOCK + span, ks], kc[:, ks]], axis=0))
            values.append(jnp.concatenate(
                [jnp.concatenate([vv[r * BLOCK:r * BLOCK + span, ks], vc[:, ks]], axis=0), ones], axis=1))
        for hq in range(N_Q_HEADS):
            hk = hq // group
            qh = qb[:, hq * HEAD_DIM:(hq + 1) * HEAD_DIM]
            s = _dot_nt(qh, keys[hk]) + bias_all
            sk = sink_ref[hq]
            m = jnp.maximum(jnp.max(s, axis=-1, keepdims=True), sk)
            p = jnp.exp(s - m)
            o_den = _dot(p.astype(BF16), values[hk])
            den = o_den[:, HEAD_DIM:HEAD_DIM + 1] + jnp.exp(sk - m)
            cat_ref[rows, hq * HEAD_DIM:(hq + 1) * HEAD_DIM] = (o_den[:, :HEAD_DIM] / den).astype(BF16)
        vnb = vnorm_ref[0, rows, :]
        ugb = ug_ref[0, rows, :]
        for g in range(N_SG_GROUPS):
            gs = slice(g * SG_GROUP_DIM, (g + 1) * SG_GROUP_DIM)
            sg = _dot(ws_ref[g], vnb[:, gs]) + bs_ref[:, gs]
            cat_ref[rows, ATTN_WIDTH + g * SG_GROUP_DIM:ATTN_WIDTH + (g + 1) * SG_GROUP_DIM] = (
                ugb[:, gs] * sg).astype(BF16)
    y = _dot(cat_ref[...], wout_ref[...])
    o_ref[0] = x_ref[0] + mod_ref[0, 2:3, :] * y


def _mixer(sink, q, k, v, kc, vc, ug, vn, w_s, b_full, w_out, x, mod_lat):
    b, l, d = x.shape
    tq = TQ_MIX
    r = tq // BLOCK
    nb = l // BLOCK
    c = kc.shape[1]
    prev_map = lambda bi, i: (bi, jnp.maximum(i * r - 1, 0), 0)
    next_map = lambda bi, i: (bi, jnp.minimum((i + 1) * r, nb - 1), 0)
    cur_map = lambda bi, i: (bi, i, 0)
    body = functools.partial(_mixer_body, seq_len=l, sub_blocks=r)
    return pl.pallas_call(
        body,
        out_shape=SDS((b, l, d), F32),
        grid=(b, l // tq),
        in_specs=[BS(memory_space=pltpu.SMEM),
                  BS((1, tq, ATTN_WIDTH), cur_map),
                  BS((1, BLOCK, KV_WIDTH), prev_map), BS((1, tq, KV_WIDTH), cur_map),
                  BS((1, BLOCK, KV_WIDTH), next_map),
                  BS((1, BLOCK, KV_WIDTH), prev_map), BS((1, tq, KV_WIDTH), cur_map),
                  BS((1, BLOCK, KV_WIDTH), next_map),
                  BS((1, c, KV_WIDTH), lambda bi, i: (bi, 0, 0)),
                  BS((1, c, KV_WIDTH), lambda bi, i: (bi, 0, 0)),
                  BS((1, tq, SG_WIDTH), cur_map), BS((1, tq, SG_WIDTH), cur_map),
                  BS((N_SG_GROUPS, BLOCK, BLOCK), lambda bi, i: (0, 0, 0)),
                  BS((BLOCK, SG_WIDTH), lambda bi, i: (0, 0)),
                  BS((d, d), lambda bi, i: (0, 0)),
                  BS((1, tq, d), cur_map),
                  BS((1, SUBLANES, d), lambda bi, i: (bi, 0, 0))],
        out_specs=BS((1, tq, d), cur_map),
        scratch_shapes=[pltpu.VMEM((tq, d), BF16)],
        compiler_params=_cparams("arbitrary", "arbitrary"),
        name="mixer_ab",
    )(sink, q, k, k, k, v, v, v, kc, vc, ug, vn, w_s, b_full, w_out, x, mod_lat)


def _router_body(x_ref, mod_ref, g_ref, wr_ref, eb_ref, tri_ref,
                 h2b_ref, slot_hi_ref, slot_lo_ref, slot_hi_t_ref, slot_lo_t_ref, gate_t_ref,
                 lo_col_ref, hi_col_ref, bounds_row_ref, copy_ref, rows_ref, cnt_ref, carry_ref,
                 *, tm):
    i = pl.program_id(0)

    @pl.when(i == 0)
    def _():
        carry_ref[...] = jnp.zeros_like(carry_ref)

    h2 = _norm_mod(x_ref[...], g_ref[...], mod_ref[0, 4:5, :], mod_ref[0, 3:4, :])
    h2b_ref[...] = h2.astype(BF16)

    logits = lax.dot_general(wr_ref[...], h2, (((1,), (1,)), ((), ())),
                             preferred_element_type=F32, precision=HIGHEST)
    scores = jax.nn.sigmoid(logits)
    per_group = N_EXPERTS // N_GROUPS
    shape3 = (N_GROUPS, per_group, tm)
    s3 = scores.reshape(shape3)
    b3 = (scores + eb_ref[...]).reshape(shape3)
    sub = lax.broadcasted_iota(I32, shape3, 1)
    eid = lax.broadcasted_iota(I32, shape3, 0) * per_group + sub

    m1 = jnp.max(b3, axis=1, keepdims=True)
    i1 = jnp.min(jnp.where(b3 == m1, sub, per_group), axis=1, keepdims=True)
    m2 = jnp.max(jnp.where(sub == i1, -jnp.inf, b3), axis=1, keepdims=True)
    gs = m1 + m2
    keep = []
    for g in range(N_GROUPS):
        beaten = jnp.zeros((1, tm), I32)
        for g2 in range(N_GROUPS):
            if g2 == g:
                continue
            wins = (gs[g2] >= gs[g]) if g2 < g else (gs[g2] > gs[g])
            beaten = beaten + jnp.where(wins, 1, 0)
        keep.append(jnp.where(beaten < TOPK_GROUPS, 1, 0)[None])
    keep3 = jnp.concatenate(keep, axis=0)
    val = jnp.where(keep3 > 0, b3, -jnp.inf)

    def red(fn, a):
        return fn(fn(a, axis=0, keepdims=True), axis=1, keepdims=True)

    member = jnp.zeros(shape3, F32)
    for _ in range(TOP_K):
        m = red(jnp.max, val)
        idx = red(jnp.min, jnp.where(val == m, eid, N_EXPERTS))
        hit = eid == idx
        val = jnp.where(hit, -jnp.inf, val)
        member = member + jnp.where(hit, 1.0, 0.0)
    picked = jnp.where(member > 0.0, s3, 0.0)
    gate3 = picked / red(jnp.sum, picked) * ROUTED_SCALE

    member2 = member.reshape(N_EXPERTS, tm)
    cnt = jnp.sum(member2, axis=1, keepdims=True)
    runlen = jnp.floor((cnt + (RUN_ALIGN - 1)) * (1.0 / RUN_ALIGN)) * RUN_ALIGN
    runlen_b = jnp.broadcast_to(runlen, (N_EXPERTS, LANES))
    e_row = lax.broadcasted_iota(I32, (N_EXPERTS, N_EXPERTS), 0)
    e_col = lax.broadcasted_iota(I32, (N_EXPERTS, N_EXPERTS), 1)
    earlier = jnp.where(e_col < e_row, 1.0, 0.0).astype(BF16)
    loff = _dot(earlier, runlen_b.astype(BF16))
    slot = _dot(member2.astype(BF16), tri_ref[...]) + loff[:, 0:1]
    slot = jnp.where(member2 > 0.0, slot, float(NO_SLOT))
    slot_hi = jnp.floor(slot * (1.0 / 256.0))
    slot_lo = slot - 256.0 * slot_hi
    gate = gate3.reshape(N_EXPERTS, tm)
    no_expert = jnp.zeros((LANES - N_EXPERTS, tm), F32)
    pad_e = lambda a: jnp.concatenate([a, no_expert], axis=0)
    slot_hi_ref[0] = pad_e(slot_hi).astype(BF16)
    slot_lo_ref[0] = pad_e(slot_lo).astype(BF16)
    slot_hi_t_ref[0] = pad_e(slot_hi).T.astype(BF16)
    slot_lo_t_ref[0] = pad_e(slot_lo).T.astype(BF16)
    gate_t_ref[0] = pad_e(gate).T.astype(BF16)
    run_lo = loff
    run_hi = loff + runlen_b
    no_run = jnp.zeros((LANES - N_EXPERTS, LANES), F32)
    lo_col_ref[0] = jnp.concatenate([run_lo, no_run], axis=0)
    hi_col_ref[0] = jnp.concatenate([run_hi, no_run], axis=0)
    diag = (lax.broadcasted_iota(I32, (N_EXPERTS, LANES), 0)
            == lax.broadcasted_iota(I32, (N_EXPERTS, LANES), 1))
    bounds_row_ref[0] = jnp.concatenate(
        [jnp.sum(jnp.where(diag, run_lo, 0.0), axis=0, keepdims=True),
         jnp.sum(jnp.where(diag, run_hi, 0.0), axis=0, keepdims=True),
         jnp.zeros((SUBLANES - 2, LANES), F32)], axis=0)
    big = 2.0 * RUN_ALIGN
    n_chunks = runlen_b * (1.0 / RUN_ALIGN)
    n_big = jnp.floor(n_chunks * 0.5)
    n_small = n_chunks - 2.0 * n_big
    before_big = _dot(earlier, n_big.astype(BF16))
    before_small = _dot(earlier, n_small.astype(BF16))
    gbase = carry_ref[...]
    j = lax.broadcasted_iota(I32, (N_EXPERTS, LANES), 1).astype(F32)
    e_iota = lax.broadcasted_iota(I32, (N_EXPERTS, LANES), 0).astype(F32)

    def entries(before, count, loc0, rel0, step):
        e_of = jnp.sum(jnp.where(before + count <= j, 1.0, 0.0), axis=0, keepdims=True)
        pick = lambda x: jnp.sum(jnp.where(e_iota == e_of, x, 0.0), axis=0, keepdims=True)
        k = j[0:1, :] - pick(before)
        return jnp.minimum(e_of, N_EXPERTS - 1.0), pick(loc0) + step * k, pick(rel0) + step * k

    e_b, loc_b, rel_b = entries(before_big, n_big, loff, gbase, big)
    e_s, loc_s, rel_s = entries(before_small, n_small, loff + big * n_big, gbase + big * n_big, 0.0)
    copy_ref[0] = jnp.concatenate([jnp.concatenate([e_b, e_s], axis=1),
                                   jnp.concatenate([loc_b, loc_s], axis=1),
                                   jnp.concatenate([rel_b, rel_s], axis=1)], axis=0).astype(I32)
    last = slice(N_EXPERTS - 1, N_EXPERTS)
    lane = lax.broadcasted_iota(I32, (1, LANES), 1)
    rows_used = loff[last, :] + runlen_b[last, :]
    counts = jnp.where(lane == 0, rows_used,
                       jnp.where(lane == 1, before_big[last, :] + n_big[last, :],
                                 before_small[last, :] + n_small[last, :]))
    rows_ref[0] = counts.astype(I32)
    total = carry_ref[...] + runlen_b
    carry_ref[...] = total
    cnt_ref[...] = total


def _router(x_flat, mod_lat, g, wr_t, e_bias, tri, tokens_per_batch):
    t, d = x_flat.shape
    tm = MOE_TILE
    n_tiles = t // tm
    tiles_per_batch = tokens_per_batch // tm
    body = functools.partial(_router_body, tm=tm)
    table = SDS((n_tiles, COPY_FIELDS, 2 * LANES), I32)
    table_spec = BS((1, COPY_FIELDS, 2 * LANES), lambda i: (i, 0, 0))
    expert_major = SDS((n_tiles, LANES, tm), BF16)
    expert_major_spec = BS((1, LANES, tm), lambda i: (i, 0, 0))
    token_major = SDS((n_tiles, tm, LANES), BF16)
    token_major_spec = BS((1, tm, LANES), lambda i: (i, 0, 0))
    per_expert = SDS((n_tiles, LANES, LANES), F32)
    per_expert_spec = BS((1, LANES, LANES), lambda i: (i, 0, 0))
    return pl.pallas_call(
        body,
        out_shape=(SDS((t, d), BF16),
                   expert_major, expert_major, token_major, token_major, token_major,
                   per_expert, per_expert, SDS((n_tiles, SUBLANES, LANES), F32),
                   table, SDS((n_tiles, 1, LANES), I32), SDS((N_EXPERTS, LANES), F32)),
        grid=(n_tiles,),
        in_specs=[BS((tm, d), lambda i: (i, 0)),
                  BS((1, SUBLANES, d), lambda i: (i // tiles_per_batch, 0, 0)),
                  BS((1, d), lambda i: (0, 0)),
                  BS((N_EXPERTS, d), lambda i: (0, 0)),
                  BS((N_EXPERTS, 1), lambda i: (0, 0)),
                  BS((tm, tm), lambda i: (0, 0))],
        out_specs=(BS((tm, d), lambda i: (i, 0)),
                   expert_major_spec, expert_major_spec, token_major_spec, token_major_spec,
                   token_major_spec, per_expert_spec, per_expert_spec,
                   BS((1, SUBLANES, LANES), lambda i: (i, 0, 0)),
                   table_spec, BS((1, 1, LANES), lambda i: (i, 0, 0)),
                   BS((N_EXPERTS, LANES), lambda i: (0, 0))),
        scratch_shapes=[pltpu.VMEM((N_EXPERTS, LANES), F32)],
        compiler_params=_cparams("arbitrary"),
        name="moe_router",
    )(x_flat, mod_lat, g, wr_t, e_bias, tri)


def _start_run_copies(pstart_ref, copy_ref, n_big, n_small, run_copy):
    def entry(base, nrows):
        def body(j, carry):
            slot0 = pstart_ref[copy_ref[0, 0, base + j]] + copy_ref[0, 2, base + j]
            run_copy(pl.multiple_of(copy_ref[0, 1, base + j], RUN_ALIGN),
                     pl.multiple_of(slot0, RUN_ALIGN), nrows).start()
            return carry
        return body

    lax.fori_loop(0, n_big, entry(0, 2 * RUN_ALIGN), 0)
    lax.fori_loop(0, n_small, entry(LANES, RUN_ALIGN), 0)


def _wait_run_rows(copy_of_rows, rows):
    def wait_n(nrows):
        def body(_, carry):
            copy_of_rows(nrows).wait()
            return carry
        return body

    lax.fori_loop(0, rows // SLOT_CHUNK, wait_n(SLOT_CHUNK), 0)
    lax.fori_loop(0, (rows % SLOT_CHUNK) // RUN_ALIGN, wait_n(RUN_ALIGN), 0)


def _dispatch_body(pend_ref, pcnt_ref, nu_ref, pstart_ref, tile_ref, copy_ref,
                   slot_hi_ref, slot_lo_ref, bounds_ref, h_ref, xs_ref, loc_ref, zbuf_ref, sem, zsem,
                   tsem, *, tl, bm, n_blocks):
    i = pl.program_id(0)
    last = pl.num_programs(0) - 1
    buf = i % 2

    def zero_copy(row0):
        return pltpu.make_async_copy(
            zbuf_ref, xs_ref.at[pl.ds(pl.multiple_of(row0, RUN_ALIGN), bm), :], zsem)

    def tail_copy(blk):
        return pltpu.make_async_copy(
            zbuf_ref, xs_ref.at[pl.ds(pl.multiple_of(blk * bm, bm), bm), :], tsem)

    @pl.when(i == 0)
    def _():
        zbuf_ref[...] = jnp.zeros_like(zbuf_ref)

        def start(e, c):
            @pl.when(pcnt_ref[e] > 0)
            def _():
                zero_copy(pend_ref[e] - bm).start()
            return c

        def wait(e, c):
            @pl.when(pcnt_ref[e] > 0)
            def _():
                zero_copy(pend_ref[e] - bm).wait()
            return c

        def start_tail(j, c):
            tail_copy(j).start()
            return c

        lax.fori_loop(0, N_EXPERTS, start, 0)
        lax.fori_loop(nu_ref[0], n_blocks, start_tail, 0)
        lax.fori_loop(0, N_EXPERTS, wait, 0)

    rows_used = tile_ref[i, 0]
    h = h_ref[...]
    run_lo = bounds_ref[0, 0:1, :]
    run_hi = bounds_ref[0, 1:2, :]

    def sort_chunk(c, carry):
        row0 = (c * SLOT_CHUNK).astype(F32)
        row_e = lax.broadcasted_iota(I32, (SLOT_CHUNK, LANES), 0).astype(F32) + row0
        in_run = jnp.where(row_e >= run_lo, jnp.where(row_e < run_hi, 1.0, 0.0), 0.0).astype(BF16)
        slot_of_token = (256.0 * _dot(in_run, slot_hi_ref[0]) + _dot(in_run, slot_lo_ref[0]))
        row_t = lax.broadcasted_iota(I32, (SLOT_CHUNK, tl), 0).astype(F32) + row0
        onehot = jnp.where(slot_of_token == row_t, 1.0, 0.0).astype(BF16)
        rows = pl.ds(pl.multiple_of(c * SLOT_CHUNK, SLOT_CHUNK), SLOT_CHUNK)
        loc_ref[buf, rows, :] = _dot(onehot, h).astype(BF16)
        return carry

    lax.fori_loop(0, (rows_used + SLOT_CHUNK - 1) // SLOT_CHUNK, sort_chunk, 0)

    def run_copy(b, loc0, slot0, nrows=RUN_ALIGN):
        return pltpu.make_async_copy(loc_ref.at[b, pl.ds(loc0, nrows), :],
                                     xs_ref.at[pl.ds(slot0, nrows), :], sem.at[b])

    _start_run_copies(pstart_ref, copy_ref, tile_ref[i, 1], tile_ref[i, 2],
                      lambda loc0, slot0, n: run_copy(buf, loc0, slot0, n))

    @pl.when(i > 0)
    def _():
        _wait_run_rows(lambda n: run_copy(1 - buf, 0, 0, n), tile_ref[jnp.maximum(i - 1, 0), 0])

    @pl.when(i == last)
    def _():
        _wait_run_rows(lambda n: run_copy(buf, 0, 0, n), rows_used)

        def wait_tail(j, c):
            tail_copy(j).wait()
            return c

        lax.fori_loop(nu_ref[0], n_blocks, wait_tail, 0)


def _dispatch(pend, pcnt, n_used, pstart, tile_counts, copy_table, slot_hi, slot_lo, bounds_row,
              h2b, n_slots):
    t, d = h2b.shape
    tl = MOE_TILE
    body = functools.partial(_dispatch_body, tl=tl, bm=BM_FFN, n_blocks=n_slots // BM_FFN)
    grid_spec = pltpu.PrefetchScalarGridSpec(
        num_scalar_prefetch=5,
        grid=(t // tl,),
        in_specs=[BS((1, COPY_FIELDS, 2 * LANES), lambda i, *_: (i, 0, 0), memory_space=pltpu.SMEM),
                  BS((1, LANES, tl), lambda i, *_: (i, 0, 0)),
                  BS((1, LANES, tl), lambda i, *_: (i, 0, 0)),
                  BS((1, SUBLANES, LANES), lambda i, *_: (i, 0, 0)),
                  BS((tl, d), lambda i, *_: (i, 0))],
        out_specs=BS(memory_space=pl.ANY),
        scratch_shapes=[pltpu.VMEM((2, LOCAL_SLOTS, d), BF16), pltpu.VMEM((BM_FFN, d), BF16),
                        pltpu.SemaphoreType.DMA((2,)), pltpu.SemaphoreType.DMA(()),
                        pltpu.SemaphoreType.DMA(())],
    )
    return pl.pallas_call(
        body,
        out_shape=SDS((n_slots, d), BF16),
        grid_spec=grid_spec,
        compiler_params=_cparams("arbitrary"),
        name="moe_dispatch",
    )(pend, pcnt, n_used, pstart, tile_counts, copy_table, slot_hi, slot_lo, bounds_row, h2b)


def _ffn_body(first_ref, count_ref, nu_ref, xs_ref, wg_ref, wu_ref, wd_ref, ys_ref, wgb_ref, wub_ref,
              wdb_ref, xbuf_ref, ybuf_ref, zbuf_ref, isem, osem, tsem, *, bm, n_blocks):
    e = pl.program_id(0)
    nu = nu_ref[0]

    def block_rows(blk):
        return pl.ds(pl.multiple_of(blk * bm, bm), bm)

    def in_copy(blk, slot):
        return pltpu.make_async_copy(xs_ref.at[block_rows(blk), :], xbuf_ref.at[slot], isem.at[slot])

    def out_copy(blk, slot):
        return pltpu.make_async_copy(ybuf_ref.at[slot], ys_ref.at[block_rows(blk), :], osem.at[slot])

    def tail_copy(blk):
        return pltpu.make_async_copy(zbuf_ref, ys_ref.at[block_rows(blk), :], tsem)

    @pl.when(e == 0)
    def _():
        for s in range(FFN_IN_BUFS):
            @pl.when(s < nu)
            def _():
                in_copy(s, s).start()

        zbuf_ref[...] = jnp.zeros_like(zbuf_ref)

        def start_tail(j, c):
            tail_copy(j).start()
            return c

        lax.fori_loop(nu, n_blocks, start_tail, 0)

    @pl.when(count_ref[e] > 0)
    def _():
        wgb_ref[...] = wg_ref[...].astype(BF16)
        wub_ref[...] = wu_ref[...].astype(BF16)
        wdb_ref[...] = wd_ref[...].astype(BF16)

    def one_block(b, carry):
        i = first_ref[e] + b
        slot = i % FFN_IN_BUFS
        oslot = i % 2
        in_copy(i, slot).wait()
        x = xbuf_ref[slot]
        a = _silu(_dot(x, wgb_ref[...])) * _dot(x, wub_ref[...])
        y = _dot(a.astype(BF16), wdb_ref[...]).astype(BF16)

        @pl.when(i >= 2)
        def _():
            out_copy(i - 2, oslot).wait()

        ybuf_ref[oslot] = y
        out_copy(i, oslot).start()

        @pl.when(i + FFN_IN_BUFS < nu)
        def _():
            in_copy(i + FFN_IN_BUFS, slot).start()

        return carry

    lax.fori_loop(0, count_ref[e], one_block, 0)

    @pl.when(e == pl.num_programs(0) - 1)
    def _():
        @pl.when(nu >= 2)
        def _():
            out_copy(nu - 2, (nu - 2) % 2).wait()

        out_copy(nu - 1, (nu - 1) % 2).wait()

        def wait_tail(j, c):
            tail_copy(j).wait()
            return c

        lax.fori_loop(nu, n_blocks, wait_tail, 0)


def _ffn(first_block, block_count, n_used, xs, w_gate, w_up, w_down, layer):
    n_slots, d = xs.shape
    bm = BM_FFN
    de = w_gate.shape[-1]
    n_blocks = n_slots // bm
    body = functools.partial(_ffn_body, bm=bm, n_blocks=n_blocks)
    grid_spec = pltpu.PrefetchScalarGridSpec(
        num_scalar_prefetch=3,
        grid=(N_EXPERTS,),
        in_specs=[BS(memory_space=pl.ANY),
                  BS((None, None, d, de), lambda e, *_: (layer, e, 0, 0)),
                  BS((None, None, d, de), lambda e, *_: (layer, e, 0, 0)),
                  BS((None, None, de, d), lambda e, *_: (layer, e, 0, 0))],
        out_specs=BS(memory_space=pl.ANY),
        scratch_shapes=[pltpu.VMEM((d, de), BF16), pltpu.VMEM((d, de), BF16),
                        pltpu.VMEM((de, d), BF16),
                        pltpu.VMEM((FFN_IN_BUFS, bm, d), BF16), pltpu.VMEM((2, bm, d), BF16),
                        pltpu.VMEM((bm, d), BF16),
                        pltpu.SemaphoreType.DMA((FFN_IN_BUFS,)), pltpu.SemaphoreType.DMA((2,)),
                        pltpu.SemaphoreType.DMA(())],
    )
    return pl.pallas_call(
        body,
        out_shape=SDS((n_slots, d), BF16),
        grid_spec=grid_spec,
        compiler_params=_cparams("arbitrary"),
        name="moe_experts",
    )(first_block, block_count, n_used, xs, w_gate, w_up, w_down)


def _combine_body(pstart_ref, tile_ref, copy_ref, next_copy_ref, ys_ref,
                  slot_hi_ref, slot_lo_ref, gate_t_ref, lo_col_ref, hi_col_ref,
                  x_ref, h2b_ref, mod_ref, wsg_ref, wsu_ref, wsd_ref, fg_ref,
                  o_ref, loc_ref, acc_ref, sem, *, tl, final):
    i = pl.program_id(0)
    last = pl.num_programs(0) - 1
    buf = i % 2
    rows_used = tile_ref[i, 0]
    nxt = jnp.minimum(i + 1, last)

    def run_copy(b, loc0, slot0, nrows=RUN_ALIGN):
        return pltpu.make_async_copy(ys_ref.at[pl.ds(slot0, nrows), :],
                                     loc_ref.at[b, pl.ds(loc0, nrows), :], sem.at[b])

    @pl.when(i == 0)
    def _():
        loc_ref[...] = jnp.zeros_like(loc_ref)
        _start_run_copies(pstart_ref, copy_ref, tile_ref[i, 1], tile_ref[i, 2],
                          lambda loc0, slot0, n: run_copy(buf, loc0, slot0, n))

    @pl.when(i < last)
    def _():
        _start_run_copies(pstart_ref, next_copy_ref, tile_ref[nxt, 1], tile_ref[nxt, 2],
                          lambda loc0, slot0, n: run_copy(1 - buf, loc0, slot0, n))

    hb = h2b_ref[...]
    a = _silu(_dot(hb, wsg_ref[...])) * _dot(hb, wsu_ref[...])
    acc_ref[...] = _dot(a.astype(BF16), wsd_ref[...])
    run_lo = lo_col_ref[0][:, 0:1]
    run_hi = hi_col_ref[0][:, 0:1]

    _wait_run_rows(lambda n: run_copy(buf, 0, 0, n), rows_used)

    def unsort_chunk(c, carry):
        col0 = (c * SLOT_CHUNK).astype(F32)
        col_e = lax.broadcasted_iota(I32, (LANES, SLOT_CHUNK), 1).astype(F32) + col0
        in_run = jnp.where(col_e >= run_lo, jnp.where(col_e < run_hi, 1.0, 0.0), 0.0).astype(BF16)
        slot_of_token = (256.0 * _dot(slot_hi_ref[0], in_run) + _dot(slot_lo_ref[0], in_run))
        gate_of_token = _dot(gate_t_ref[0], in_run)
        col_t = lax.broadcasted_iota(I32, (tl, SLOT_CHUNK), 1).astype(F32) + col0
        gate = jnp.where(slot_of_token == col_t, gate_of_token, 0.0).astype(BF16)
        y = loc_ref[buf, pl.ds(pl.multiple_of(c * SLOT_CHUNK, SLOT_CHUNK), SLOT_CHUNK), :]
        acc_ref[...] = acc_ref[...] + _dot(gate, y)
        return carry

    lax.fori_loop(0, (rows_used + SLOT_CHUNK - 1) // SLOT_CHUNK, unsort_chunk, 0)

    xo = x_ref[...] + mod_ref[0, 5:6, :] * acc_ref[...]
    if final:
        ms = jnp.mean(xo * xo, axis=-1, keepdims=True)
        xo = (xo * lax.rsqrt(ms + EPS)) * fg_ref[...]
    o_ref[...] = xo


def _combine(pstart, tile_counts, copy_table, ys, slot_hi_t, slot_lo_t, gate_t, lo_col, hi_col,
             x_flat, h2b, mod_lat, wsg, wsu, wsd, final_g, tokens_per_batch, final):
    t, d = x_flat.shape
    tl = MOE_TILE
    ds = wsg.shape[1]
    tiles_per_batch = tokens_per_batch // tl
    body = functools.partial(_combine_body, tl=tl, final=final)
    n_tiles = t // tl
    table_spec = lambda index: BS((1, COPY_FIELDS, 2 * LANES), index, memory_space=pltpu.SMEM)
    this_tile = lambda i, *_: (i, 0, 0)
    next_tile = lambda i, *_: (jnp.minimum(i + 1, n_tiles - 1), 0, 0)
    grid_spec = pltpu.PrefetchScalarGridSpec(
        num_scalar_prefetch=2,
        grid=(n_tiles,),
        in_specs=[table_spec(this_tile), table_spec(next_tile),
                  BS(memory_space=pl.ANY),
                  BS((1, tl, LANES), this_tile), BS((1, tl, LANES), this_tile),
                  BS((1, tl, LANES), this_tile),
                  BS((1, LANES, LANES), this_tile), BS((1, LANES, LANES), this_tile),
                  BS((tl, d), lambda i, *_: (i, 0)),
                  BS((tl, d), lambda i, *_: (i, 0)),
                  BS((1, SUBLANES, d), lambda i, *_: (i // tiles_per_batch, 0, 0)),
                  BS((d, ds), lambda i, *_: (0, 0)),
                  BS((d, ds), lambda i, *_: (0, 0)),
                  BS((ds, d), lambda i, *_: (0, 0)),
                  BS((1, d), lambda i, *_: (0, 0))],
        out_specs=BS((tl, d), lambda i, *_: (i, 0)),
        scratch_shapes=[pltpu.VMEM((2, LOCAL_SLOTS, d), BF16), pltpu.VMEM((tl, d), F32),
                        pltpu.SemaphoreType.DMA((2,))],
    )
    return pl.pallas_call(
        body,
        out_shape=SDS((t, d), F32),
        grid_spec=grid_spec,
        compiler_params=_cparams("arbitrary"),
        name="moe_combine",
    )(pstart, tile_counts, copy_table, copy_table, ys, slot_hi_t, slot_lo_t, gate_t,
      lo_col, hi_col, x_flat, h2b, mod_lat, wsg, wsu, wsd, final_g)


def _moe(x_flat, mod_lat, g, w_router, e_bias, w_gate, w_up, w_down, ws_gate, ws_up, ws_down,
         final_g, tri, tokens_per_batch, layer, final):
    t, d = x_flat.shape
    bm = BM_FFN
    n_tiles = t // MOE_TILE
    (h2b, slot_hi, slot_lo, slot_hi_t, slot_lo_t, gate_t, lo_col, hi_col, bounds_row, copy_table,
     tile_counts, seg) = _router(x_flat, mod_lat, g, w_router.T, e_bias.reshape(N_EXPERTS, 1), tri,
                                 tokens_per_batch)
    tile_counts = tile_counts[:, 0, :SUBLANES]
    seg_rows = seg[:, 0].astype(I32)
    pcnt = (seg_rows + bm - 1) // bm * bm
    pend = jnp.cumsum(pcnt).astype(I32)
    pstart = pend - pcnt
    max_rows = t * TOP_K + n_tiles * N_EXPERTS * (RUN_ALIGN - 1)
    n_blocks = -(-max_rows // bm) + N_EXPERTS
    n_slots = n_blocks * bm
    n_used = pend[-1:] // bm
    xs = _dispatch(pend, pcnt, n_used, pstart, tile_counts, copy_table, slot_hi, slot_lo,
                   bounds_row, h2b, n_slots)
    ys = _ffn(pstart // bm, pcnt // bm, n_used, xs, w_gate, w_up, w_down, layer)
    return _combine(pstart, tile_counts, copy_table, ys, slot_hi_t, slot_lo_t, gate_t, lo_col,
                    hi_col, x_flat, h2b, mod_lat,
                    ws_gate.astype(BF16), ws_up.astype(BF16), ws_down.astype(BF16),
                    final_g.reshape(1, d), tokens_per_batch, final)


def _inproj_c_body(xp_ref, x_ref, xn_ref, mod_ref, g_ref, w_ref, cw_ref, cb_ref,
                   v_ref, g1_ref, g2_ref, *, tm, n_tiles):
    i = pl.program_id(1)
    halo = SUBLANES
    xe = jnp.concatenate([xp_ref[0], x_ref[0], xn_ref[0]], axis=0)
    h = _norm_mod(xe, g_ref[...], mod_ref[0, 1:2, :], mod_ref[0, 0:1, :])
    row = lax.broadcasted_iota(I32, (tm + 2 * halo, 1), 0)
    outside = jnp.logical_or(jnp.logical_and(i == 0, row < halo),
                             jnp.logical_and(i == n_tiles - 1, row >= tm + halo))
    hb = jnp.where(outside, 0.0, h).astype(BF16)
    width = v_ref.shape[-1]
    for part, o_ref in enumerate((v_ref, g1_ref, g2_ref)):
        cols = slice(part * width, (part + 1) * width)
        zp = _dot(hb, w_ref[:, cols])
        up = pltpu.roll(zp, 1, 0)
        dn = pltpu.roll(zp, tm + 2 * halo - 1, 0)
        z = cw_ref[0:1, cols] * up + cw_ref[1:2, cols] * zp + cw_ref[2:3, cols] * dn + cb_ref[:, cols]
        o_ref[0] = z[halo:halo + tm]


def _inproj_c(x, mod_lat, g, w_in, conv_w, conv_b):
    b, l, d = x.shape
    tm = TM_PROJ
    n_tiles = l // tm
    w3 = w_in.shape[1]
    width = w3 // 3
    r8 = tm // SUBLANES
    body = functools.partial(_inproj_c_body, tm=tm, n_tiles=n_tiles)
    out = SDS((b, l, width), F32)
    return pl.pallas_call(
        body,
        out_shape=(out, out, out),
        grid=(b, n_tiles),
        in_specs=[BS((1, SUBLANES, d), lambda bi, i: (bi, jnp.maximum(i * r8 - 1, 0), 0)),
                  BS((1, tm, d), lambda bi, i: (bi, i, 0)),
                  BS((1, SUBLANES, d), lambda bi, i: (bi, jnp.minimum((i + 1) * r8, l // SUBLANES - 1), 0)),
                  BS((1, SUBLANES, d), lambda bi, i: (bi, 0, 0)),
                  BS((1, d), lambda bi, i: (0, 0)),
                  BS((d, w3), lambda bi, i: (0, 0)),
                  BS((3, w3), lambda bi, i: (0, 0)),
                  BS((1, w3), lambda bi, i: (0, 0))],
        out_specs=(BS((1, tm, width), lambda bi, i: (bi, i, 0)),
                   BS((1, tm, width), lambda bi, i: (bi, i, 0)),
                   BS((1, tm, width), lambda bi, i: (bi, i, 0))),
        compiler_params=_cparams("arbitrary", "arbitrary"),
        name="inproj_c",
    )(x, x, x, mod_lat, g, w_in, conv_w, conv_b)


def _filter_body(f_ref, w1_ref, b1_ref, w2_ref, b2_ref, w3hi_ref, w3lo_ref, fr_ref, dl_ref,
                 k_ref, l1_ref, *, tp, n):
    i = pl.program_id(0)
    feats = f_ref[...]
    fr = fr_ref[...]
    a = jnp.sin(fr * (_dot_hp(feats, w1_ref[...]) + b1_ref[...]))
    a = jnp.sin(fr * (_dot_hp(a, w2_ref[...]) + b2_ref[...]))
    a_hi = a.astype(BF16)
    a_lo = (a - a_hi.astype(F32)).astype(BF16)
    w3_hi = w3hi_ref[...]
    hf = _dot(a_hi, w3_hi) + _dot(a_hi, w3lo_ref[...]) + _dot(a_lo, w3_hi)
    t01 = feats[:, 0:1]
    hf = hf * (jnp.exp(-t01 * jnp.abs(dl_ref[...])) + DECAY_SHIFT)
    row = lax.broadcasted_iota(I32, hf.shape, 0) + i * tp
    hf = jnp.where(row == n, 0.0, hf)
    k_ref[...] = hf

    @pl.when(i == 0)
    def _():
        l1_ref[...] = jnp.zeros_like(l1_ref)

    l1_ref[...] = l1_ref[...] + jnp.sum(jnp.abs(hf), axis=0, keepdims=True)


def _filters(feats, w1, b1, w2, b2, w3, freq, delta, width):
    n2, fe = feats.shape
    n = n2 // 2
    hid = w2.shape[0]
    tp = 256
    half_tiles = n // tp
    body = functools.partial(_filter_body, tp=tp, n=n)
    full = lambda shape: BS(shape, lambda i: (0, 0))
    by_direction = lambda a: a.reshape(a.shape[0], HYENA_ORDER, 2, width).transpose(2, 0, 1, 3).reshape(
        2, a.shape[0], HYENA_ORDER * width)
    w3_d = by_direction(w3)
    w3_hi = w3_d.astype(BF16)
    w3_lo = (w3_d - w3_hi.astype(F32)).astype(BF16)
    delta_d = by_direction(delta)
    fo = HYENA_ORDER * width
    direction = lambda i: (i // half_tiles, 0, 0)
    return pl.pallas_call(
        body,
        out_shape=(SDS((n2, fo), F32), SDS((SUBLANES, fo), F32)),
        grid=(n2 // tp,),
        in_specs=[BS((tp, fe), lambda i: (i, 0)), full((fe, hid)), full((1, hid)),
                  full((hid, hid)), full((1, hid)),
                  BS((None, hid, fo), direction), BS((None, hid, fo), direction), full((1, hid)),
                  BS((None, 1, fo), direction)],
        out_specs=(BS((tp, fo), lambda i: (i, 0)), BS((SUBLANES, fo), lambda i: (0, 0))),
        compiler_params=_cparams("arbitrary"),
        name="hyena_filters",
    )(feats, w1, b1, w2, b2, w3_hi, w3_lo, freq, delta_d)


DFT_R = 128
DFT_VP = 72
DFT_BGROUP = 16
DFT_LANE_TILES = 4


def _dft_tables(n):
    r = DFT_R
    vp = DFT_VP
    m = 2 * n
    na = n // r
    two_pi = 2.0 * np.pi
    live = (np.arange(vp) <= r // 2).astype(np.float64)
    a = np.arange(2 * na)[None, :]
    v = np.arange(vp)[:, None]
    ang1 = two_pi * ((a * v) % r) / r
    f1_full = np.concatenate([np.cos(ang1), -np.sin(ang1)], axis=0) * np.tile(live, 2)[:, None]
    f1 = f1_full[:, :na]
    b = np.arange(r)[None, None, :]
    u = np.arange(r)[None, :, None]
    vv = np.arange(vp)[:, None, None]
    ang2 = two_pi * ((b * (r * u + vv)) % m) / m
    gr, gi = np.cos(ang2), -np.sin(ang2)
    fwd = np.concatenate([np.concatenate([gr, -gi], axis=2),
                          np.concatenate([gi, gr], axis=2)], axis=1)
    hr, hi = np.transpose(gr, (0, 2, 1)), -np.transpose(gi, (0, 2, 1))
    inv = np.concatenate([np.concatenate([hr, -hi], axis=2),
                          np.concatenate([hi, hr], axis=2)], axis=1)
    weight = live * np.where((np.arange(vp) == 0) | (np.arange(vp) == r // 2), 1.0, 2.0)
    ang3 = two_pi * ((np.arange(na)[:, None] * np.arange(vp)[None, :]) % r) / r
    f3 = np.concatenate([np.cos(ang3) * weight, -np.sin(ang3) * weight], axis=1) / m
    cast = lambda t: jnp.asarray(t.astype(np.float32)).astype(BF16)
    return cast(f1), cast(f1_full), cast(fwd), cast(inv), cast(f3)


def _lane_tile_specs(rows, index_map_of_tile):
    return [BS((None, rows, DFT_BGROUP, LANES), index_map_of_tile(t)) for t in range(DFT_LANE_TILES)]


def _rows_of_position(ref, j):
    x, bg, _ = ref.shape
    return ref.reshape(x * bg, LANES)[pl.ds(j, x, stride=bg), :]


def _dft_s1_body(*refs):
    q = DFT_LANE_TILES
    y_refs, f1_ref, ar_ref, ai_ref = refs[:q], refs[q], refs[q + 1], refs[q + 2]
    f1 = f1_ref[...]
    bg = DFT_BGROUP
    pair = MXU_COLS // LANES
    planes2d = lambda ref, t: ref.at[t].reshape(DFT_VP * bg, LANES)
    for t0 in range(0, q, pair):
        tiles = range(t0, t0 + pair)
        for j in range(bg):
            y = jnp.concatenate([_rows_of_position(y_refs[t], j) for t in tiles], axis=1)
            res = _dot(f1, y.astype(BF16))
            for s, t in enumerate(tiles):
                lanes = slice(s * LANES, (s + 1) * LANES)
                planes2d(ar_ref, t)[pl.ds(j, DFT_VP, stride=bg), :] = res[:DFT_VP, lanes]
                planes2d(ai_ref, t)[pl.ds(j, DFT_VP, stride=bg), :] = res[DFT_VP:, lanes]


def _dft_s1(y4, f1):
    nb, na, r, c = y4.shape
    q, bg = DFT_LANE_TILES, DFT_BGROUP
    out = SDS((nb, c // LANES, DFT_VP, r, LANES), F32)
    plane_spec = BS((None, q, DFT_VP, bg, LANES), lambda i, j, cc: (i, cc, 0, j, 0))
    tile_map = lambda t: (lambda i, j, cc: (i, 0, j, cc * q + t))
    return pl.pallas_call(
        _dft_s1_body,
        out_shape=(out, out),
        grid=(nb, r // bg, c // (q * LANES)),
        in_specs=_lane_tile_specs(na, tile_map) + [BS((2 * DFT_VP, na), lambda i, j, cc: (0, 0))],
        out_specs=(plane_spec, plane_spec),
        compiler_params=_cparams("arbitrary", "arbitrary", "arbitrary"),
        name="dft_stage1",
    )(*([y4] * q), f1)


def _plane_rows(re_ref, im_ref):
    wide = lambda ref: jnp.concatenate([ref[t] for t in range(ref.shape[0])], axis=1)
    return jnp.concatenate([wide(re_ref), wide(im_ref)], axis=0).astype(BF16)


def _filter_spec_body(ar_ref, ai_ref, g_ref, l1_ref, kr_ref, ki_ref):
    y = _dot(g_ref[...], _plane_rows(ar_ref, ai_ref))
    inv = 1.0 / l1_ref[0:1, :]
    kr_ref[...] = (y[:DFT_R] * inv).astype(BF16)
    ki_ref[...] = (y[DFT_R:] * inv).astype(BF16)


def _filter_spectrum(ar, ai, fwd, l1, width):
    r = DFT_R
    a_spec = BS((None, width // LANES, None, r, LANES), lambda v, o: (0, o, v, 0, 0))
    k_spec = BS((None, r, width), lambda v, o: (v, 0, o))
    out = SDS((DFT_VP, r, HYENA_ORDER * width), BF16)
    return pl.pallas_call(
        _filter_spec_body,
        out_shape=(out, out),
        grid=(DFT_VP, HYENA_ORDER),
        in_specs=[a_spec, a_spec, BS((None, 2 * r, 2 * r), lambda v, o: (v, 0, 0)),
                  BS((SUBLANES, width), lambda v, o: (0, o))],
        out_specs=(k_spec, k_spec),
        compiler_params=_cparams("arbitrary", "arbitrary"),
        name="hyena_filter_spectrum",
    )(ar, ai, fwd, l1)


def _conv_mid_body(ar_ref, ai_ref, g_ref, h_ref, kr_ref, ki_ref, qr_ref, qi_ref):
    y = _dot(g_ref[...], _plane_rows(ar_ref, ai_ref))
    yr, yi = y[:DFT_R], y[DFT_R:]
    kr, ki = kr_ref[...].astype(F32), ki_ref[...].astype(F32)
    p = jnp.concatenate([yr * kr - yi * ki, yr * ki + yi * kr], axis=0).astype(BF16)
    q = _dot(h_ref[...], p)
    for t in range(qr_ref.shape[0]):
        lanes = slice(t * LANES, (t + 1) * LANES)
        qr_ref[t] = q[:DFT_R, lanes]
        qi_ref[t] = q[DFT_R:, lanes]


def _conv_mid(ar, ai, fwd, inv, kr, ki, order):
    nb, tiles, vp, r, _ = ar.shape
    c = tiles * LANES
    a_spec = BS((None, tiles, None, r, LANES), lambda n, v: (n, 0, v, 0, 0))
    m_spec = BS((None, 2 * r, 2 * r), lambda n, v: (v, 0, 0))
    k_spec = BS((None, r, c), lambda n, v: (v, 0, order))
    out = SDS((nb, tiles, vp, r, LANES), F32)
    return pl.pallas_call(
        _conv_mid_body,
        out_shape=(out, out),
        grid=(nb, vp),
        in_specs=[a_spec, a_spec, m_spec, m_spec, k_spec, k_spec],
        out_specs=(a_spec, a_spec),
        compiler_params=_cparams("arbitrary", "arbitrary"),
        name="hyena_spectral_product",
    )(ar, ai, fwd, inv, kr, ki)


def _idft_gate_body(*refs):
    q = DFT_LANE_TILES
    qr_ref, qi_ref = refs[0], refs[1]
    y_refs, gate_refs = refs[2:2 + q], refs[2 + q:2 + 2 * q]
    f3_ref, fb_ref, o_ref = refs[2 + 2 * q:]
    f3 = f3_ref[...]
    for t in range(q):
        lanes = slice(t * LANES, (t + 1) * LANES)
        fb = fb_ref[:, lanes]
        for j in range(DFT_BGROUP):
            planes = jnp.concatenate([_rows_of_position(qr_ref.at[t], j),
                                      _rows_of_position(qi_ref.at[t], j)], axis=0).astype(BF16)
            conv = _dot(f3, planes)
            o_ref[:, j, lanes] = _rows_of_position(gate_refs[t], j) * (
                conv + fb * _rows_of_position(y_refs[t], j))


def _idft_gate(qr, qi, f3, y4, gate4, fbias):
    nb, na, r, c = y4.shape
    q, bg = DFT_LANE_TILES, DFT_BGROUP
    tile_map = lambda t: (lambda i, j, cc: (i, 0, j, cc * q + t))
    wide = lambda rows: BS((None, rows, bg, q * LANES), lambda i, j, cc: (i, 0, j, cc))
    plane_spec = BS((None, q, DFT_VP, bg, LANES), lambda i, j, cc: (i, cc, 0, j, 0))
    return pl.pallas_call(
        _idft_gate_body,
        out_shape=SDS((nb, na, r, c), F32),
        grid=(nb, r // bg, c // (q * LANES)),
        in_specs=([plane_spec, plane_spec]
                  + _lane_tile_specs(na, tile_map) + _lane_tile_specs(na, tile_map)
                  + [BS((na, 2 * DFT_VP), lambda i, j, cc: (0, 0)),
                     BS((1, q * LANES), lambda i, j, cc: (0, cc))]),
        out_specs=wide(na),
        compiler_params=_cparams("arbitrary", "arbitrary", "arbitrary"),
        name="hyena_idft_gate",
    )(qr, qi, *([y4] * q + [gate4] * q), f3, fbias)


def _outproj_body(y_ref, w_ref, x_ref, mod_ref, o_ref):
    o_ref[0] = x_ref[0] + mod_ref[0, 2:3, :] * _dot(y_ref[0].astype(BF16), w_ref[...])


def _outproj(y, w_out, x, mod_lat):
    b, l, d = x.shape
    tm = TM_PROJ
    wdt = y.shape[-1]
    return pl.pallas_call(
        _outproj_body,
        out_shape=SDS((b, l, d), F32),
        grid=(b, l // tm),
        in_specs=[BS((1, tm, wdt), lambda bi, i: (bi, i, 0)),
                  BS((wdt, d), lambda bi, i: (0, 0)),
                  BS((1, tm, d), lambda bi, i: (bi, i, 0)),
                  BS((1, SUBLANES, d), lambda bi, i: (bi, 0, 0))],
        out_specs=BS((1, tm, d), lambda bi, i: (bi, i, 0)),
        compiler_params=_cparams("arbitrary", "arbitrary"),
        name="outproj_c",
    )(y, w_out, x, mod_lat)


def _hyena(x, mod_lat, g, w_in, conv_w, conv_b, w1, b1, w2, b2, w3, freq, delta, f_bias, w_out):
    b, n, d = x.shape
    width = w_out.shape[0]
    r = DFT_R
    na = n // r
    f1, f1_full, fwd, inv, f3 = _dft_tables(n)
    v, gate1, gate2 = _inproj_c(x, mod_lat, g, w_in.astype(BF16), conv_w, conv_b.reshape(1, -1))

    pos = jnp.arange(2 * n, dtype=F32)
    t = jnp.where(pos < n, pos, 2 * n - pos)
    t01 = t / max(n - 1, 1)
    bands = jnp.linspace(1e-4, FILT_BANDS - 1, FILT_BANDS, dtype=F32)
    ang = (2.0 * math.pi / n) * t[:, None] * bands[None, :]
    feats = jnp.concatenate([t01[:, None], jnp.cos(ang), jnp.sin(ang)], axis=-1)
    fe = feats.shape[1]
    feats = jnp.pad(feats, ((0, 0), (0, LANES - fe)))
    w1p = jnp.pad(w1, ((0, LANES - fe), (0, 0)))
    k_circ, l1 = _filters(feats, w1p, b1.reshape(1, -1), w2, b2.reshape(1, -1), w3,
                          freq.reshape(1, -1), delta.reshape(1, -1), width)
    far, fai = _dft_s1(k_circ.reshape(1, 2 * na, r, k_circ.shape[1]), f1_full)
    kr, ki = _filter_spectrum(far, fai, fwd, l1, width)

    y4 = v.reshape(b, na, r, width)
    for o, gate in enumerate((gate1, gate2)):
        ar, ai = _dft_s1(y4, f1)
        qr, qi = _conv_mid(ar, ai, fwd, inv, kr, ki, o)
        y4 = _idft_gate(qr, qi, f3, y4, gate.reshape(b, na, r, width), f_bias[o].reshape(1, width))
    return _outproj(y4.reshape(b, n, width), w_out.astype(BF16), x, mod_lat)


def _rope_tables(seq_len):
    rows = seq_len // GRID_W
    row = jnp.repeat(jnp.arange(rows, dtype=F32), GRID_W)
    col = jnp.tile(jnp.arange(GRID_W, dtype=F32), rows)
    inv = jnp.power(ROPE_BASE, -jnp.arange(ROPE_FREQS, dtype=F32) / ROPE_FREQS)
    ar, ac = row[:, None] * inv, col[:, None] * inv
    cos_h = jnp.concatenate([jnp.cos(ar), jnp.cos(ar), jnp.cos(ac), jnp.cos(ac)], axis=1)
    sin_h = jnp.concatenate([-jnp.sin(ar), jnp.sin(ar), -jnp.sin(ac), jnp.sin(ac)], axis=1)
    reps = LANES // HEAD_DIM
    return jnp.tile(cos_h, (1, reps)), jnp.tile(sin_h, (1, reps))


def _rotate_partner_columns(w):
    ncol = w.shape[1]
    lane = np.arange(ncol)
    partner = np.where((lane % (2 * ROPE_FREQS)) < ROPE_FREQS, lane + ROPE_FREQS, lane - ROPE_FREQS)
    return w[:, partner]


def kernel(x, c, ctx, c_ctx, w_mod, b_mod, norm_g, w_in_ab, sink, w_spatial, b_spatial, w_out_ab,
           w_in_c, conv_w, conv_b, filt_w1, filt_b1, filt_w2, filt_b2, filt_w3, filt_freq,
           filt_delta, filt_bias, w_out_c, w_router, e_bias, w_gate, w_up, w_down, ws_gate,
           ws_up, ws_down, final_g):
    b, l, d = x.shape
    depth = w_mod.shape[0]
    assert depth == 2 and b + 1 <= SUBLANES

    cc = jnp.zeros((SUBLANES, d), F32).at[:b].set(c).at[b].set(c_ctx)
    m_all = _mod_vectors(cc, w_mod, b_mod)

    def mod_rows(layer, row0, nrow):
        m = m_all[layer, row0:row0 + nrow].reshape(nrow, 6, d)
        return jnp.pad(m, ((0, 0), (0, SUBLANES - 6), (0, 0)))

    tri = jnp.triu(jnp.ones((MOE_TILE, MOE_TILE), F32), k=1).astype(BF16)

    mod_lat = mod_rows(0, 0, b)
    mod_ctx = mod_rows(0, b, 1)[0]
    w_in = w_in_ab[0]
    qk = ATTN_WIDTH + KV_WIDTH
    w_cat = jnp.concatenate([w_in, _rotate_partner_columns(w_in[:, :qk])], axis=1).astype(BF16)
    cos_t, sin_t = _rope_tables(l)
    group_avg = jnp.kron(jnp.eye(N_SG_GROUPS, dtype=F32),
                         jnp.full((SG_GROUP_DIM, SG_GROUP_DIM), 1.0 / SG_GROUP_DIM, F32)).astype(BF16)
    kc, vc = _ctx_kv(ctx, mod_ctx, norm_g[0, 0].reshape(1, d),
                     w_in[:, ATTN_WIDTH:ATTN_WIDTH + 2 * KV_WIDTH].astype(BF16))
    q, k, v, ug, vn = _inproj_ab(x, mod_lat, norm_g[0, 0].reshape(1, d), w_cat, cos_t, sin_t, group_avg)
    b_full = jnp.repeat(b_spatial[0].T, SG_GROUP_DIM, axis=1)
    x1 = _mixer(sink[0], q, k, v, kc, vc, ug, vn, w_spatial[0].astype(BF16), b_full,
                w_out_ab[0].astype(BF16), x, mod_lat)
    x2 = _moe(x1.reshape(b * l, d), mod_lat, norm_g[0, 1].reshape(1, d), w_router[0], e_bias[0],
              w_gate, w_up, w_down, ws_gate[0], ws_up[0], ws_down[0], final_g, tri, l,
              layer=0, final=False).reshape(b, l, d)

    mod_lat = mod_rows(1, 0, b)
    x3 = _hyena(x2, mod_lat, norm_g[1, 0].reshape(1, d), w_in_c[0], conv_w[0], conv_b[0],
                filt_w1[0], filt_b1[0], filt_w2[0], filt_b2[0], filt_w3[0], filt_freq[0],
                filt_delta[0], filt_bias[0], w_out_c[0])
    out = _moe(x3.reshape(b * l, d), mod_lat, norm_g[1, 1].reshape(1, d), w_router[1], e_bias[1],
               w_gate, w_up, w_down, ws_gate[1], ws_up[1], ws_down[1], final_g, tri, l,
               layer=1, final=True)
    return out.reshape(b, l, d)
```
